```python
import math
import jax, jax.numpy as jnp
from jax import lax
import numpy as np

D_MODEL = 1024
BATCH = 8
SEQ = 2048
DEPTH = 4

GRID_W = 64
CTX_LEN = 256
EPS = 1e-6
SHORT_CONV = 3

HY_WIDTH = 512
HY_ORDER = 2
HY_BANDS = 16
HY_EMB = 1 + 2 * HY_BANDS
HY_HIDDEN = 64
HY_SHORT_DECAY_PCT = 0.3
HY_LONG_DECAY_PCT = 1.5
HY_TARGET = 1e-2

SSM_HEADS = 8
SSM_HEAD_DIM = 64
SSM_WIDTH = SSM_HEADS * SSM_HEAD_DIM
SSM_GROUPS = 2
SSM_HPG = SSM_HEADS // SSM_GROUPS
SSM_STATE = 128
SSM_CHUNK = 128

GDN_HEADS = 4
GDN_DK = 128
GDN_DV = 128
GDN_CHUNK = 64

N_BRANCH = 3
D_FF = -(-8 * D_MODEL // (3 * 256)) * 256

HY_COLS = 3 * HY_WIDTH
SSM_XBC = SSM_WIDTH + 2 * SSM_GROUPS * SSM_STATE
SSM_COLS = SSM_WIDTH + SSM_XBC + 2 * SSM_HEADS
GDN_QKV = GDN_HEADS * (2 * GDN_DK + GDN_DV)
GDN_COLS = GDN_QKV + GDN_HEADS * GDN_DV + 4 * GDN_HEADS
GATE_COLS = N_BRANCH * D_MODEL
OFF_SSM = HY_COLS
OFF_GDN = OFF_SSM + SSM_COLS
OFF_GATE = OFF_GDN + GDN_COLS
IN_COLS = OFF_GATE + GATE_COLS

kernel_name = 'hybrid_hyena_ssd_gdn_prefix_dit'


def rmsnorm(x, w):
    xf = x.astype(jnp.float32)
    y = xf * lax.rsqrt(jnp.mean(xf * xf, axis=-1, keepdims=True) + EPS)
    return (y * w.astype(jnp.float32)).astype(x.dtype)


def l2norm(x):
    return x * lax.rsqrt(jnp.sum(x * x, axis=-1, keepdims=True) + EPS)


def rev(a):
    return a[:, ::-1]


def short_conv(u, w, n_rows):
    bsz, L, ch = u.shape
    row = L // n_rows
    pad = SHORT_CONV // 2
    ur = jnp.pad(u.reshape(bsz, n_rows, row, ch), ((0, 0), (0, 0), (pad, pad), (0, 0)))
    y = sum(ur[:, :, j:j + row] * w[j] for j in range(SHORT_CONV))
    return y.reshape(bsz, L, ch)


def hyena_filters(L, w1, b1, w2, b2, w3, freq):
    t = jnp.linspace(0.0, 1.0, L, dtype=jnp.float32)[:, None]
    w = 2.0 * math.pi * jnp.arange(L, dtype=jnp.float32)[:, None] / L
    f = jnp.linspace(1e-4, HY_BANDS - 1, HY_BANDS, dtype=jnp.float32)[None, :]
    z = jnp.concatenate([t, jnp.cos(f * w), -jnp.sin(f * w)], axis=-1)
    h = jnp.sin(freq[0] * (z @ w1 + b1))
    h = jnp.sin(freq[1] * (h @ w2 + b2))
    h = (h @ w3).astype(jnp.float32).reshape(L, HY_ORDER, 2, HY_WIDTH)
    min_decay = math.log(HY_TARGET) / HY_LONG_DECAY_PCT
    max_decay = math.log(HY_TARGET) / HY_SHORT_DECAY_PCT
    deltas = jnp.linspace(min_decay, max_decay, HY_WIDTH, dtype=jnp.float32)
    window = jnp.exp(-t * jnp.abs(deltas))
    return h * window[:, None, None, :]


def bidir_long_conv(u, h, bias):
    L = u.shape[1]
    k = jnp.concatenate([h[:, 0], jnp.zeros_like(h[:1, 0]), h[:0:-1, 1]], axis=0)
    spec = jnp.fft.rfft(u, n=2 * L, axis=1) * jnp.fft.rfft(k, axis=0)[None]
    y = jnp.fft.irfft(spec, n=2 * L, axis=1)[:, :L]
    return y + u * bias


def hyena_mixer(p, n_rows, conv_w, conv_b, w1, b1, w2, b2, w3, freq, bias):
    L = p.shape[1]
    u = (short_conv(p, conv_w, n_rows) + conv_b).astype(jnp.float32)
    v, x1, x2 = jnp.split(u, 3, axis=-1)
    h = hyena_filters(L, w1, b1, w2, b2, w3, freq)
    z = x1 * bidir_long_conv(v, h[:, 0], bias[0])
    z = x2 * bidir_long_conv(z, h[:, 1], bias[1])
    return z.astype(p.dtype)


def ssd_chunked(x, dt, A, Bm, Cm, h0):
    bsz, L = x.shape[:2]
    nc, Q = L // SSM_CHUNK, SSM_CHUNK

    def chunk(a):
        return a.reshape(bsz, nc, Q, *a.shape[2:])

    a_cum = jnp.cumsum(chunk(dt * A), axis=2)
    xd = chunk(x * dt[..., None])
    Bc, Cc = chunk(Bm), chunk(Cm)
    causal = jnp.tril(jnp.ones((Q, Q), dtype=bool))[:, :, None, None]
    decay_ls = jnp.exp(jnp.where(causal, a_cum[:, :, :, None] - a_cum[:, :, None, :], -jnp.inf))
    cb = jnp.einsum('bclgn,bcsgn->bclsg', Cc, Bc)
    y_diag = jnp.einsum('bclsg,bclsge,bcsgep->bclgep', cb, decay_ls, xd)
    a_last = a_cum[:, :, -1]
    states = jnp.einsum('bcsgn,bcsge,bcsgep->bcgepn', Bc, jnp.exp(a_last[:, :, None] - a_cum), xd)

    def step(h, inp):
        st, dec = inp
        return h * dec[..., None, None] + st, h

    h_last, h_in = lax.scan(step, h0, (jnp.moveaxis(states, 1, 0), jnp.moveaxis(jnp.exp(a_last), 1, 0)))
    h_in = jnp.moveaxis(h_in, 0, 1)
    y_off = jnp.einsum('bclgn,bcgepn,bclge->bclgep', Cc, h_in, jnp.exp(a_cum))
    return (y_diag + y_off).reshape(x.shape), h_last


def ssm_prepare(p, n_rows, conv_w, conv_b, dt_bias):
    bsz, L, _ = p.shape
    z = p[..., :SSM_WIDTH]
    xbc = jax.nn.silu(short_conv(p[..., SSM_WIDTH:SSM_WIDTH + SSM_XBC], conv_w, n_rows) + conv_b).astype(jnp.float32)
    nb = SSM_GROUPS * SSM_STATE
    xs = xbc[..., :SSM_WIDTH].reshape(bsz, L, SSM_GROUPS, SSM_HPG, SSM_HEAD_DIM)
    Bm = xbc[..., SSM_WIDTH:SSM_WIDTH + nb].reshape(bsz, L, SSM_GROUPS, SSM_STATE)
    Cm = xbc[..., SSM_WIDTH + nb:].reshape(bsz, L, SSM_GROUPS, SSM_STATE)
    dt_raw = p[..., SSM_WIDTH + SSM_XBC:].astype(jnp.float32).reshape(bsz, L, 2, SSM_GROUPS, SSM_HPG)
    dt = jax.nn.softplus(dt_raw + dt_bias.astype(jnp.float32).reshape(2, SSM_GROUPS, SSM_HPG))
    return z, xs, Bm, Cm, dt


def ssm_bidir(xs, Bm, Cm, dt, A, h0_f, h0_b):
    y_f, h_f = ssd_chunked(xs, dt[:, :, 0], A[0], Bm, Cm, h0_f)
    y_b, h_b = ssd_chunked(rev(xs), rev(dt[:, :, 1]), A[1], rev(Bm), rev(Cm), h0_b)
    return y_f + rev(y_b), h_f, h_b


def ssm_output(y, xs, z, D_skip, norm_w):
    bsz, L = y.shape[:2]
    y = y + xs * D_skip.astype(jnp.float32).reshape(SSM_GROUPS, SSM_HPG)[..., None]
    gw = SSM_HPG * SSM_HEAD_DIM
    y = y.reshape(bsz, L, SSM_GROUPS, gw) * jax.nn.silu(z.astype(jnp.float32)).reshape(bsz, L, SSM_GROUPS, gw)
    y = rmsnorm(y, norm_w.reshape(SSM_GROUPS, gw))
    return y.reshape(bsz, L, SSM_WIDTH)


def ssm_mixer(p_ctx, p_lat, n_rows, conv_w, conv_b, dt_bias, A_log, D_skip, norm_w):
    A = -jnp.exp(A_log.astype(jnp.float32)).reshape(2, SSM_GROUPS, SSM_HPG)
    zc, xc, Bc, Cc, dtc = ssm_prepare(p_ctx, 1, conv_w, conv_b, dt_bias)
    zl, xl, Bl, Cl, dtl = ssm_prepare(p_lat, n_rows, conv_w, conv_b, dt_bias)
    h0 = jnp.zeros((p_ctx.shape[0], SSM_GROUPS, SSM_HPG, SSM_HEAD_DIM, SSM_STATE), jnp.float32)
    yc, h_f, h_b = ssm_bidir(xc, Bc, Cc, dtc, A, h0, h0)
    yl, _, _ = ssm_bidir(xl, Bl, Cl, dtl, A, h_f, h_b)
    return (ssm_output(yc, xc, zc, D_skip, norm_w).astype(p_ctx.dtype),
            ssm_output(yl, xl, zl, D_skip, norm_w).astype(p_lat.dtype))


def gdn_chunked(q, k, v, g, beta, S0):
    bsz, L, H, _ = q.shape
    nc, C = L // GDN_CHUNK, GDN_CHUNK

    def chunk(a):
        return jnp.moveaxis(a.reshape(bsz, nc, C, H, *a.shape[3:]), 3, 2)

    q, k, v, g, beta = chunk(q), chunk(k), chunk(v), chunk(g), chunk(beta)
    g_cum = jnp.cumsum(g, axis=-1)
    incl = jnp.tril(jnp.ones((C, C), dtype=bool))
    decay = jnp.exp(jnp.where(incl, g_cum[..., :, None] - g_cum[..., None, :], -jnp.inf))
    kb = k * beta[..., None]
    mat = jnp.einsum('bnhid,bnhjd->bnhij', kb, k) * decay
    rhs = jnp.concatenate([v * beta[..., None], kb * jnp.exp(g_cum)[..., None]], axis=-1)
    sol = lax.linalg.triangular_solve(mat, rhs, left_side=True, lower=True, unit_diagonal=True)
    u, w = sol[..., :GDN_DV], sol[..., GDN_DV:]
    qk = jnp.einsum('bnhid,bnhjd->bnhij', q, k) * decay
    qg = q * jnp.exp(g_cum)[..., None]
    g_last = g_cum[..., -1]
    kd = k * jnp.exp(g_last[..., None] - g_cum)[..., None]

    def step(S, inp):
        u_c, w_c, qk_c, qg_c, kd_c, gl_c = inp
        v_new = u_c - jnp.einsum('bhck,bhkv->bhcv', w_c, S)
        o = jnp.einsum('bhck,bhkv->bhcv', qg_c, S) + jnp.einsum('bhij,bhjv->bhiv', qk_c, v_new)
        S = S * jnp.exp(gl_c)[..., None, None] + jnp.einsum('bhck,bhcv->bhkv', kd_c, v_new)
        return S, o

    xs = (jnp.moveaxis(u, 1, 0), jnp.moveaxis(w, 1, 0), jnp.moveaxis(qk, 1, 0),
          jnp.moveaxis(qg, 1, 0), jnp.moveaxis(kd, 1, 0), jnp.moveaxis(g_last, 1, 0))
    S_last, o = lax.scan(step, S0, xs)
    o = jnp.moveaxis(jnp.moveaxis(o, 0, 1), 2, 3).reshape(bsz, L, H, GDN_DV)
    return o, S_last


def gdn_prepare(p, n_rows, conv_w, dt_bias, A_log):
    bsz, L, _ = p.shape
    nq = GDN_HEADS * GDN_DK
    nv = GDN_HEADS * GDN_DV
    qkv = jax.nn.silu(short_conv(p[..., :GDN_QKV], conv_w, n_rows)).astype(jnp.float32)
    q = l2norm(qkv[..., :nq].reshape(bsz, L, GDN_HEADS, GDN_DK)) * GDN_DK ** -0.5
    k = l2norm(qkv[..., nq:2 * nq].reshape(bsz, L, GDN_HEADS, GDN_DK))
    v = qkv[..., 2 * nq:].reshape(bsz, L, GDN_HEADS, GDN_DV)
    gate = p[..., GDN_QKV:GDN_QKV + nv]
    o0 = GDN_QKV + nv
    a = p[..., o0:o0 + 2 * GDN_HEADS].astype(jnp.float32).reshape(bsz, L, 2, GDN_HEADS)
    b = p[..., o0 + 2 * GDN_HEADS:].astype(jnp.float32).reshape(bsz, L, 2, GDN_HEADS)
    g = -jnp.exp(A_log.astype(jnp.float32)) * jax.nn.softplus(a + dt_bias.astype(jnp.float32))
    beta = jax.nn.sigmoid(b)
    return q, k, v, g, beta, gate


def gdn_bidir(q, k, v, g, beta, S_f, S_b):
    o_f, S_f = gdn_chunked(q, k, v, g[:, :, 0], beta[:, :, 0], S_f)
    o_b, S_b = gdn_chunked(rev(q), rev(k), rev(v), rev(g[:, :, 1]), rev(beta[:, :, 1]), S_b)
    return o_f + rev(o_b), S_f, S_b


def gdn_output(o, gate, norm_w):
    bsz, L = o.shape[:2]
    o = rmsnorm(o, norm_w) * jax.nn.silu(gate.astype(jnp.float32)).reshape(bsz, L, GDN_HEADS, GDN_DV)
    return o.reshape(bsz, L, GDN_HEADS * GDN_DV)


def gdn_mixer(p_ctx, p_lat, n_rows, conv_w, dt_bias, A_log, norm_w):
    qc, kc, vc, gc, bc, gtc = gdn_prepare(p_ctx, 1, conv_w, dt_bias, A_log)
    ql, kl, vl, gl, bl, gtl = gdn_prepare(p_lat, n_rows, conv_w, dt_bias, A_log)
    S0 = jnp.zeros((p_ctx.shape[0], GDN_HEADS, GDN_DK, GDN_DV), jnp.float32)
    oc, S_f, S_b = gdn_bidir(qc, kc, vc, gc, bc, S0, S0)
    ol, _, _ = gdn_bidir(ql, kl, vl, gl, bl, S_f, S_b)
    return (gdn_output(oc, gtc, norm_w).astype(p_ctx.dtype),
            gdn_output(ol, gtl, norm_w).astype(p_lat.dtype))


def branch_merge(p_gate, y_hy, y_ssm, y_gdn, w_hy_out, w_ssm_out, w_gdn_out, w_out):
    gates = jax.nn.sigmoid(p_gate.reshape(*p_gate.shape[:-1], N_BRANCH, D_MODEL))
    m = (gates[..., 0, :] * (y_hy @ w_hy_out)
         + gates[..., 1, :] * (y_ssm @ w_ssm_out)
         + gates[..., 2, :] * (y_gdn @ w_gdn_out))
    return m @ w_out


def swiglu(h, w_gate_up, w_down):
    gu = h @ w_gate_up
    return (jax.nn.silu(gu[..., :D_FF]) * gu[..., D_FF:]) @ w_down


def setup_inputs(seed: int = 0) -> dict:
    key = jax.random.key(seed)
    ks = iter(jax.random.split(key, 40))
    D = D_MODEL
    L_ = DEPTH

    def nrm(shape, s):
        return jax.random.normal(next(ks), shape, jnp.float32) * s

    def gain(shape):
        return 1.0 + nrm(shape, 0.02)

    def dt_bias(shape):
        dt = jnp.exp(jax.random.uniform(next(ks), shape, jnp.float32, math.log(1e-3), math.log(1e-1)))
        return dt + jnp.log(-jnp.expm1(-dt))

    def a_log(shape):
        return jnp.log(jax.random.uniform(next(ks), shape, jnp.float32, 1.0, 16.0))

    return {
        'x': nrm((BATCH, SEQ, D), 1.0),
        'c': nrm((BATCH, D), 1.0),
        'ctx': nrm((BATCH, CTX_LEN, D), 1.0),
        'c_ctx': nrm((D,), 1.0),
        'w_ada': nrm((L_, D, 6 * D), 0.5 * D ** -0.5),
        'b_ada': nrm((L_, 6 * D), 0.02),
        'norm1_w': gain((L_, D)),
        'norm2_w': gain((L_, D)),
        'w_in': nrm((L_, D, IN_COLS), D ** -0.5),
        'hy_conv_w': nrm((L_, SHORT_CONV, HY_COLS), SHORT_CONV ** -0.5),
        'hy_conv_b': nrm((L_, HY_COLS), 0.02),
        'hy_w1': nrm((L_, HY_EMB, HY_HIDDEN), HY_EMB ** -0.5),
        'hy_b1': nrm((L_, HY_HIDDEN), HY_EMB ** -0.5),
        'hy_w2': nrm((L_, HY_HIDDEN, HY_HIDDEN), HY_HIDDEN ** -0.5),
        'hy_b2': nrm((L_, HY_HIDDEN), HY_HIDDEN ** -0.5),
        'hy_w3': nrm((L_, HY_HIDDEN, HY_ORDER * 2 * HY_WIDTH), 0.04 * HY_HIDDEN ** -0.5),
        'hy_freq': gain((L_, 2, HY_HIDDEN)),
        'hy_bias': nrm((L_, HY_ORDER, HY_WIDTH), 0.5),
        'ssm_conv_w': nrm((L_, SHORT_CONV, SSM_XBC), SHORT_CONV ** -0.5),
        'ssm_conv_b': nrm((L_, SSM_XBC), 0.02),
        'ssm_dt_bias': dt_bias((L_, 2, SSM_HEADS)),
        'ssm_A_log': a_log((L_, 2, SSM_HEADS)),
        'ssm_D': gain((L_, SSM_HEADS)),
        'ssm_norm_w': gain((L_, SSM_WIDTH)),
        'gdn_conv_w': nrm((L_, SHORT_CONV, GDN_QKV), SHORT_CONV ** -0.5),
        'gdn_dt_bias': dt_bias((L_, 2, GDN_HEADS)),
        'gdn_A_log': a_log((L_, 2, GDN_HEADS)),
        'gdn_norm_w': gain((L_, GDN_DV)),
        'w_hy_out': nrm((L_, HY_WIDTH, D), HY_WIDTH ** -0.5),
        'w_ssm_out': nrm((L_, SSM_WIDTH, D), SSM_WIDTH ** -0.5),
        'w_gdn_out': nrm((L_, GDN_HEADS * GDN_DV, D), (GDN_HEADS * GDN_DV) ** -0.5),
        'w_out': nrm((L_, D, D), D ** -0.5),
        'w_gate_up': nrm((L_, D, 2 * D_FF), D ** -0.5),
        'w_down': nrm((L_, D_FF, D), D_FF ** -0.5),
        'final_norm_w': gain((D,)),
    }


def reference(x, c, ctx, c_ctx, w_ada, b_ada, norm1_w, norm2_w, w_in,
              hy_conv_w, hy_conv_b, hy_w1, hy_b1, hy_w2, hy_b2, hy_w3, hy_freq, hy_bias,
              ssm_conv_w, ssm_conv_b, ssm_dt_bias, ssm_A_log, ssm_D, ssm_norm_w,
              gdn_conv_w, gdn_dt_bias, gdn_A_log, gdn_norm_w,
              w_hy_out, w_ssm_out, w_gdn_out, w_out, w_gate_up, w_down, final_norm_w):
    bsz, n_lat, _ = x.shape
    rows = n_lat // GRID_W
    xl, xc = x, ctx
    s_lat = jax.nn.silu(c)
    s_ctx = jax.nn.silu(c_ctx)
    for l in range(DEPTH):
        last = l == DEPTH - 1
        mod_l = (s_lat @ w_ada[l] + b_ada[l]).reshape(bsz, 6, D_MODEL)[:, :, None, :]
        mod_c = (s_ctx @ w_ada[l] + b_ada[l]).reshape(6, D_MODEL)

        hl = rmsnorm(xl, norm1_w[l]) * (1 + mod_l[:, 1]) + mod_l[:, 0]
        hc = rmsnorm(xc, norm1_w[l]) * (1 + mod_c[1]) + mod_c[0]
        pl = hl @ w_in[l]
        pc = hc @ w_in[l]
        ssm_c, ssm_l = ssm_mixer(pc[..., OFF_SSM:OFF_GDN], pl[..., OFF_SSM:OFF_GDN], rows,
                                 ssm_conv_w[l], ssm_conv_b[l], ssm_dt_bias[l], ssm_A_log[l], ssm_D[l], ssm_norm_w[l])
        gdn_c, gdn_l = gdn_mixer(pc[..., OFF_GDN:OFF_GATE], pl[..., OFF_GDN:OFF_GATE], rows,
                                 gdn_conv_w[l], gdn_dt_bias[l], gdn_A_log[l], gdn_norm_w[l])
        hy_l = hyena_mixer(pl[..., :OFF_SSM], rows, hy_conv_w[l], hy_conv_b[l], hy_w1[l], hy_b1[l],
                           hy_w2[l], hy_b2[l], hy_w3[l], hy_freq[l], hy_bias[l])
        xl = xl + mod_l[:, 2] * branch_merge(pl[..., OFF_GATE:], hy_l, ssm_l, gdn_l,
                                             w_hy_out[l], w_ssm_out[l], w_gdn_out[l], w_out[l])
        xl = xl + mod_l[:, 5] * swiglu(rmsnorm(xl, norm2_w[l]) * (1 + mod_l[:, 4]) + mod_l[:, 3],
                                       w_gate_up[l], w_down[l])

        if not last:
            hy_c = hyena_mixer(pc[..., :OFF_SSM], 1, hy_conv_w[l], hy_conv_b[l], hy_w1[l], hy_b1[l],
                               hy_w2[l], hy_b2[l], hy_w3[l], hy_freq[l], hy_bias[l])
            xc = xc + mod_c[2] * branch_merge(pc[..., OFF_GATE:], hy_c, ssm_c, gdn_c,
                                              w_hy_out[l], w_ssm_out[l], w_gdn_out[l], w_out[l])
            xc = xc + mod_c[5] * swiglu(rmsnorm(xc, norm2_w[l]) * (1 + mod_c[4]) + mod_c[3],
                                        w_gate_up[l], w_down[l])
    return rmsnorm(xl, final_norm_w)
```

```python
import functools
import math

import jax
import jax.numpy as jnp
import numpy as np
from jax import lax
from jax.experimental import pallas as pl
from jax.experimental.pallas import tpu as pltpu

F32 = jnp.float32
BF16 = jnp.bfloat16
HI = lax.Precision.HIGHEST

EPS = 1e-6
D_MODEL = 1024
GRID_W = 64

HY_WIDTH = 512
HY_BANDS = 16
HY_EMB = 1 + 2 * HY_BANDS
HY_HIDDEN = 64
HY_SHORT_DECAY_PCT = 0.3
HY_LONG_DECAY_PCT = 1.5
HY_TARGET = 1e-2

SSM_HEADS = 8
SSM_HEAD_DIM = 64
SSM_WIDTH = 512
SSM_GROUPS = 2
SSM_HPG = 4
SSM_STATE = 128
SSM_CHUNK = 128
SSM_GW = SSM_HPG * SSM_HEAD_DIM

GDN_HEADS = 4
GDN_DK = 128
GDN_DV = 128
GDN_CHUNK = 64

D_FF = 2816

C_HY = 0
C_Z = 1536
C_XBC = 2048
C_QKV = 3072
C_GG = 4608
C_GATE = 5120
C_SM = 8192
N_IN = 8320

CONV_ROWS = 256
FREQ_BLK = 256

VMEM_LIMIT = 56 * 1024 * 1024


def _cp(*sem):
    return pltpu.CompilerParams(dimension_semantics=sem, vmem_limit_bytes=VMEM_LIMIT)


def _sigmoid(x):
    return 1.0 / (1.0 + jnp.exp(-x))


def _silu(x):
    return x * _sigmoid(x)


def _softplus(x):
    return jnp.maximum(x, 0.0) + jnp.log1p(jnp.exp(-jnp.abs(x)))


def _dot(a, b, precision=None):
    return jnp.dot(a, b, precision=precision, preferred_element_type=F32)


def _dot_nt(a, b):
    return lax.dot_general(a, b, (((1,), (1,)), ((), ())), preferred_element_type=F32)


def _dot_tn(a, b):
    return lax.dot_general(a, b, (((0,), (0,)), ((), ())), preferred_element_type=F32)


def _ada_kernel(s_ref, w_ref, b_ref, o_ref):
    s = _silu(s_ref[...])
    o_ref[0] = _dot(s, w_ref[0], HI) + b_ref[0]


def ada_modulation(svec, w_ada, b_ada):
    depth = w_ada.shape[0]
    D = D_MODEL
    return pl.pallas_call(
        _ada_kernel,
        grid=(depth, 6),
        in_specs=[
            pl.BlockSpec((16, D), lambda l, j: (0, 0)),
            pl.BlockSpec((1, D, D), lambda l, j: (l, 0, j)),
            pl.BlockSpec((1, 1, D), lambda l, j: (l, 0, j)),
        ],
        out_specs=pl.BlockSpec((1, 16, D), lambda l, j: (l, 0, j)),
        out_shape=jax.ShapeDtypeStruct((depth, 16, 6 * D), F32),
        compiler_params=_cp("arbitrary", "arbitrary"),
        name="ada",
    )(svec, w_ada, b_ada.reshape(depth, 1, 6 * D))


def _norm_mod(x, nw, scale, shift):
    ms = jnp.mean(x * x, axis=-1, keepdims=True)
    return (x * lax.rsqrt(ms + EPS) * nw) * (1.0 + scale) + shift


def _norm_matmul_kernel(x_ref, nw_ref, mod_ref, w_ref, o_ref, h_ref):
    @pl.when(pl.program_id(1) == 0)
    def _():
        h = _norm_mod(x_ref[...], nw_ref[...], mod_ref[0, 1:2, :], mod_ref[0, 0:1, :])
        h_ref[...] = h.astype(BF16)

    o_ref[...] = _dot(h_ref[...], w_ref[...])


class Rows:
    def __init__(self, B, Lc, Ll):
        self.B, self.Lc, self.Ll = B, Lc, Ll
        self.NC = B * Lc
        self.R = B * Lc + B * Ll
        assert self.NC % Ll == 0 or Ll % self.NC == 0
        tm = 1024
        while self.NC % tm or Ll % tm:
            tm //= 2
        self.tm = tm

    def mod_index(self, tm):
        nctx = self.NC // tm
        per = self.Ll // tm
        return lambda i: jnp.where(i < nctx, 0, 1 + (i - nctx) // per)


def norm_matmul(rw, x, nw, mod, w, tn):
    R, D = x.shape
    N = w.shape[1]
    tm = rw.tm
    mi = rw.mod_index(tm)
    return pl.pallas_call(
        _norm_matmul_kernel,
        grid=(R // tm, N // tn),
        in_specs=[
            pl.BlockSpec((tm, D), lambda i, j: (i, 0)),
            pl.BlockSpec((1, D), lambda i, j: (0, 0)),
            pl.BlockSpec((1, 8, D), lambda i, j: (mi(i), 0, 0)),
            pl.BlockSpec((D, tn), lambda i, j: (0, j)),
        ],
        out_specs=pl.BlockSpec((tm, tn), lambda i, j: (i, j)),
        out_shape=jax.ShapeDtypeStruct((R, N), F32),
        scratch_shapes=[pltpu.VMEM((tm, D), BF16)],
        compiler_params=_cp("arbitrary", "arbitrary"),
        name="norm_matmul",
    )(x, nw.reshape(1, D), mod, w)


def _conv_act_kernel(p_ref, w_ref, b_ref, o_ref, *, nctx_blk, act, l2, l2_scale):
    T = CONV_ROWS
    x = p_ref[...]
    row_len = jnp.where(pl.program_id(0) < nctx_blk, T, GRID_W)
    pos = lax.broadcasted_iota(jnp.int32, (T, 1), 0) & (row_len - 1)
    prev = jnp.where(pos == 0, 0.0, pltpu.roll(x, 1, 0))
    nxt = jnp.where(pos == row_len - 1, 0.0, pltpu.roll(x, T - 1, 0))
    y = prev * w_ref[0:1, :] + x * w_ref[1:2, :] + nxt * w_ref[2:3, :] + b_ref[...]
    if act:
        y = _silu(y)
    if l2:
        parts = []
        for h in range(y.shape[1] // 128):
            yh = y[:, h * 128:(h + 1) * 128]
            ss = jnp.sum(yh * yh, axis=-1, keepdims=True)
            parts.append(yh * lax.rsqrt(ss + EPS) * l2_scale)
        y = jnp.concatenate(parts, axis=1)
    o_ref[...] = y.astype(o_ref.dtype)


def conv_act(rw, p, col0, w, b, act, l2=False, l2_scale=1.0):
    R = p.shape[0]
    ncols = w.shape[1]
    cb = 512
    if b is None:
        b = jnp.zeros((ncols,), F32)
    kern = functools.partial(_conv_act_kernel, nctx_blk=rw.NC // CONV_ROWS, act=act, l2=l2,
                             l2_scale=l2_scale)
    return pl.pallas_call(
        kern,
        grid=(R // CONV_ROWS, ncols // cb),
        in_specs=[
            pl.BlockSpec((CONV_ROWS, cb), lambda i, j: (i, col0 // cb + j)),
            pl.BlockSpec((3, cb), lambda i, j: (0, j)),
            pl.BlockSpec((1, cb), lambda i, j: (0, j)),
        ],
        out_specs=pl.BlockSpec((CONV_ROWS, cb), lambda i, j: (i, j)),
        out_shape=jax.ShapeDtypeStruct((R, ncols), F32),
        compiler_params=_cp("arbitrary", "arbitrary"),
        name="conv_act",
    )(p, w, b.reshape(1, ncols))


def _small_kernel(p_ref, bias_ref, alog_ref, kind_ref, o_ref):
    x = p_ref[...]
    kind = kind_ref[...]
    sp = _softplus(x + bias_ref[...])
    g = -jnp.exp(alog_ref[...]) * sp
    o_ref[...] = jnp.where(kind == 0, sp, jnp.where(kind == 1, g, jnp.where(kind == 2, _sigmoid(x), 0.0)))


def small_prep(rw, p, bias, alog, kind):
    R = p.shape[0]
    tm = rw.tm
    vec = pl.BlockSpec((1, 128), lambda i: (0, 0))
    return pl.pallas_call(
        _small_kernel,
        grid=(R // tm,),
        in_specs=[pl.BlockSpec((tm, 128), lambda i: (i, C_SM // 128)), vec, vec, vec],
        out_specs=pl.BlockSpec((tm, 128), lambda i: (i, 0)),
        out_shape=jax.ShapeDtypeStruct((R, 128), F32),
        compiler_params=_cp("arbitrary"),
        name="small_prep",
    )(p, bias, alog, kind)


def _hy_filter_kernel(z_ref, w1_ref, b1_ref, w2_ref, b2_ref, w3_ref, f0_ref, f1_ref, win_ref, o_ref):
    h = jnp.sin(f0_ref[...] * (_dot(z_ref[...], w1_ref[...], HI) + b1_ref[...]))
    h = jnp.sin(f1_ref[...] * (_dot(h, w2_ref[...], HI) + b2_ref[...]))
    h = _dot(h, w3_ref[...], HI) * win_ref[...]
    tl = h.shape[0]
    row = lax.broadcasted_iota(jnp.int32, (tl, 1), 0) + pl.program_id(0) * tl
    drop = (row == 0) & (pl.program_id(1) % 2 == 1)
    o_ref[...] = jnp.where(drop, 0.0, h).astype(o_ref.dtype)


def hy_filter(L, w1, b1, w2, b2, w3, freq):
    t = jnp.linspace(0.0, 1.0, L, dtype=F32)[:, None]
    w = 2.0 * math.pi * jnp.arange(L, dtype=F32)[:, None] / L
    f = jnp.linspace(1e-4, HY_BANDS - 1, HY_BANDS, dtype=F32)[None, :]
    z = jnp.concatenate([t, jnp.cos(f * w), -jnp.sin(f * w)], axis=-1)
    z = jnp.pad(z, ((0, 0), (0, 128 - HY_EMB)))
    min_decay = math.log(HY_TARGET) / HY_LONG_DECAY_PCT
    max_decay = math.log(HY_TARGET) / HY_SHORT_DECAY_PCT
    deltas = jnp.linspace(min_decay, max_decay, HY_WIDTH, dtype=F32)
    window = jnp.exp(-t * jnp.abs(deltas))
    H = HY_HIDDEN
    w1p = jnp.pad(w1, ((0, 128 - HY_EMB), (0, 128 - H)))
    w2p = jnp.pad(w2, ((0, 128 - H), (0, 128 - H)))
    w3p = jnp.pad(w3, ((0, 128 - H), (0, 0)))
    pad1 = lambda v: jnp.pad(v, (0, 128 - H)).reshape(1, 128)
    tl = 256
    full = lambda shape: pl.BlockSpec(shape, lambda i, j: (0, 0))
    return pl.pallas_call(
        _hy_filter_kernel,
        grid=(L // tl, 4),
        in_specs=[
            pl.BlockSpec((tl, 128), lambda i, j: (i, 0)),
            full((128, 128)), full((1, 128)), full((128, 128)), full((1, 128)),
            pl.BlockSpec((128, HY_WIDTH), lambda i, j: (0, j)),
            full((1, 128)), full((1, 128)),
            pl.BlockSpec((tl, HY_WIDTH), lambda i, j: (i, 0)),
        ],
        out_specs=pl.BlockSpec((tl, HY_WIDTH), lambda i, j: (i, j)),
        out_shape=jax.ShapeDtypeStruct((L, 4 * HY_WIDTH), BF16),
        compiler_params=_cp("arbitrary", "arbitrary"),
        name="hy_filter",
    )(z, w1p, pad1(b1), w2p, pad1(b2), w3p, pad1(freq[0]), pad1(freq[1]), window)


def dft_tables(L):
    N = 2 * L
    f = jnp.arange(L, dtype=jnp.int32)[:, None]
    s = jnp.arange(L, dtype=jnp.int32)[None, :]
    ang = ((f * s) % N).astype(F32) * (2.0 * math.pi / N)
    c, sn = jnp.cos(ang), jnp.sin(ang)
    alt = (1 - 2 * (s % 2)).astype(F32)
    first = f == 0
    fwd = jnp.concatenate([c, jnp.where(first, alt, -sn)], axis=0)
    wgt = jnp.where(first, 1.0, 2.0) / N
    inv = jnp.concatenate([(c * wgt).T, jnp.where(first, alt / N, -sn * wgt).T], axis=1)
    return fwd.astype(BF16), inv.astype(BF16)


def _matmul_kernel(a_ref, b_ref, o_ref):
    o_ref[...] = _dot(a_ref[...], b_ref[...])


def matmul(a, b, tm, tn):
    M, K = a.shape
    N = b.shape[1]
    return pl.pallas_call(
        _matmul_kernel,
        grid=(M // tm, N // tn),
        in_specs=[pl.BlockSpec((tm, K), lambda i, j: (i, 0)), pl.BlockSpec((K, tn), lambda i, j: (0, j))],
        out_specs=pl.BlockSpec((tm, tn), lambda i, j: (i, j)),
        out_shape=jax.ShapeDtypeStruct((M, N), F32),
        compiler_params=_cp("arbitrary", "arbitrary"),
        name="matmul",
    )(a, b)


def _long_conv_kernel(u_ref, g_ref, bias_ref, fr_ref, fi_ref, ic_ref, is_ref,
                      ar0_ref, ar1_ref, ai0_ref, ai1_ref, o_ref, ub_ref, acc_ref):
    f = pl.program_id(1)

    @pl.when(f == 0)
    def _():
        ub_ref[...] = u_ref[...].astype(BF16)
        acc_ref[...] = jnp.zeros_like(acc_ref)

    ub = ub_ref[...]
    ur = _dot(fr_ref[...], ub)
    ui = _dot(fi_ref[...], ub)
    kr = ar0_ref[...] + ar1_ref[...]
    nyq = (lax.broadcasted_iota(jnp.int32, (FREQ_BLK, 1), 0) == 0) & (f == 0)
    ki = jnp.where(nyq, ai0_ref[...] + ai1_ref[...], ai0_ref[...] - ai1_ref[...])
    pr = jnp.where(nyq, ur * kr, ur * kr - ui * ki)
    pi = jnp.where(nyq, ui * ki, ur * ki + ui * kr)
    acc_ref[...] += _dot(ic_ref[...], pr.astype(BF16)) + _dot(is_ref[...], pi.astype(BF16))

    @pl.when(f == pl.num_programs(1) - 1)
    def _():
        u = u_ref[...]
        o_ref[...] = (g_ref[...] * (acc_ref[...] + u * bias_ref[...])).astype(o_ref.dtype)


def long_conv(B, L, u, u_rb0, u_cb, gate, g_rb0, gate_cb, bias, fwd, inv, kspec, order, out_dtype):
    C = HY_WIDTH
    nfb = L // FREQ_BLK
    FB = FREQ_BLK
    return pl.pallas_call(
        _long_conv_kernel,
        grid=(B, nfb),
        in_specs=[
            pl.BlockSpec((L, C), lambda b, f: (u_rb0 + b, u_cb)),
            pl.BlockSpec((L, C), lambda b, f: (g_rb0 + b, gate_cb)),
            pl.BlockSpec((1, C), lambda b, f: (0, 0)),
            pl.BlockSpec((FB, L), lambda b, f: (f, 0)),
            pl.BlockSpec((FB, L), lambda b, f: (nfb + f, 0)),
            pl.BlockSpec((L, FB), lambda b, f: (0, f)),
            pl.BlockSpec((L, FB), lambda b, f: (0, nfb + f)),
            pl.BlockSpec((FB, C), lambda b, f: (f, 2 * order)),
            pl.BlockSpec((FB, C), lambda b, f: (f, 2 * order + 1)),
            pl.BlockSpec((FB, C), lambda b, f: (nfb + f, 2 * order)),
            pl.BlockSpec((FB, C), lambda b, f: (nfb + f, 2 * order + 1)),
        ],
        out_specs=pl.BlockSpec((L, C), lambda b, f: (b, 0)),
        out_shape=jax.ShapeDtypeStruct((B * L, C), out_dtype),
        scratch_shapes=[pltpu.VMEM((L, C), BF16), pltpu.VMEM((L, C), F32)],
        compiler_params=_cp("arbitrary", "arbitrary"),
        name="long_conv",
    )(u, gate, bias.reshape(1, C), fwd, fwd, inv, inv, kspec, kspec, kspec, kspec)


def _scan_block(rw, chunk):
    nbc = rw.Lc // chunk
    nbl = rw.Ll // chunk
    base_l = rw.NC // chunk

    def f(b, d, s):
        jc = jnp.where(d == 0, s, nbc - 1 - s)
        sl = s - nbc
        jl = jnp.where(d == 0, sl, nbl - 1 - sl)
        return jnp.where(s < nbc, b * nbc + jc, base_l + b * nbl + jl)

    return f, nbc + nbl


def _tri_masks(n, d):
    sgn = 1 - 2 * d
    row = lax.broadcasted_iota(jnp.int32, (n, n), 0)
    col = lax.broadcasted_iota(jnp.int32, (n, n), 1)
    dlt = (row - col) * sgn
    return dlt >= 0, dlt > 0, (-dlt) >= 0, row == col


def _ssd_kernel(x_ref, b_ref, c_ref, sm_ref, dtT_ref, e_ref, alx_ref, alc_ref, o_ref, h_ref):
    Q = SSM_CHUNK
    GW = SSM_GW
    d = pl.program_id(1)

    @pl.when(pl.program_id(2) == 0)
    def _():
        h_ref[...] = jnp.zeros_like(h_ref)

    keep, _, keep_t, _ = _tri_masks(Q, d)
    tri = keep.astype(F32)
    tri_t = keep_t.astype(F32)
    dtx = _dot(sm_ref[...], e_ref[0], HI)
    ax = dtx * (-jnp.exp(alx_ref[0]))
    cumx = _dot(tri, ax, HI)
    a_t = dtT_ref[0] * (-jnp.exp(alc_ref[0]))
    cumr = _dot(a_t, tri_t, HI)
    totx = jnp.sum(ax, axis=0, keepdims=True)
    xd = x_ref[...] * dtx
    xdw = xd * jnp.exp(totx - cumx)
    ecum = jnp.exp(cumx)
    lane_head = lax.broadcasted_iota(jnp.int32, (Q, GW), 1) // SSM_HEAD_DIM
    for g in range(SSM_GROUPS):
        gs = slice(g * GW, (g + 1) * GW)
        bg = b_ref[:, g * SSM_STATE:(g + 1) * SSM_STATE].astype(BF16)
        cg = c_ref[:, g * SSM_STATE:(g + 1) * SSM_STATE].astype(BF16)
        cb = _dot_nt(cg, bg)
        xdg = xd[:, gs]
        yd = jnp.zeros((Q, GW), F32)
        for e4 in range(SSM_HPG):
            e = g * SSM_HPG + e4
            diff = cumx[:, e * SSM_HEAD_DIM:e * SSM_HEAD_DIM + 1] - cumr[e:e + 1, :]
            m = (cb * jnp.where(keep, jnp.exp(diff), 0.0)).astype(BF16)
            xm = jnp.where(lane_head == e4, xdg, 0.0).astype(BF16)
            yd = yd + _dot(m, xm)
        h_t = h_ref[g]
        y_off = _dot(cg, h_t.astype(BF16)) * ecum[:, gs]
        o_ref[0, :, gs] = yd + y_off
        st = _dot_tn(bg, xdw[:, gs].astype(BF16))
        h_ref[g] = h_t * jnp.exp(totx[:, gs]) + st


def ssd_scan(rw, xbc, sm, dtT, e_ssm, alx, alc):
    Q = SSM_CHUNK
    blk, nsteps = _scan_block(rw, Q)
    R = xbc.shape[0]
    return pl.pallas_call(
        _ssd_kernel,
        grid=(rw.B, 2, nsteps),
        in_specs=[
            pl.BlockSpec((Q, 512), lambda b, d, s: (blk(b, d, s), 0)),
            pl.BlockSpec((Q, 256), lambda b, d, s: (blk(b, d, s), 2)),
            pl.BlockSpec((Q, 256), lambda b, d, s: (blk(b, d, s), 3)),
            pl.BlockSpec((Q, 128), lambda b, d, s: (blk(b, d, s), 0)),
            pl.BlockSpec((1, 8, Q), lambda b, d, s: (d, 0, blk(b, d, s))),
            pl.BlockSpec((1, 128, 512), lambda b, d, s: (d, 0, 0)),
            pl.BlockSpec((1, 1, 512), lambda b, d, s: (d, 0, 0)),
            pl.BlockSpec((1, 8, 1), lambda b, d, s: (d, 0, 0)),
        ],
        out_specs=pl.BlockSpec((1, Q, 512), lambda b, d, s: (d, blk(b, d, s), 0)),
        out_shape=jax.ShapeDtypeStruct((2, R, 512), F32),
        scratch_shapes=[pltpu.VMEM((SSM_GROUPS, SSM_STATE, SSM_GW), F32)],
        compiler_params=_cp("arbitrary", "arbitrary", "arbitrary"),
        name="ssd_scan",
    )(xbc, xbc, xbc, sm, dtT, e_ssm, alx, alc)


def _ssm_post_kernel(y_ref, x_ref, z_ref, dx_ref, nw_ref, o_ref):
    y = y_ref[0] + y_ref[1] + x_ref[...] * dx_ref[...]
    y = y * _silu(z_ref[...])
    parts = []
    for g in range(SSM_GROUPS):
        yg = y[:, g * SSM_GW:(g + 1) * SSM_GW]
        ms = jnp.mean(yg * yg, axis=-1, keepdims=True)
        parts.append(yg * lax.rsqrt(ms + EPS))
    o_ref[...] = (jnp.concatenate(parts, axis=1) * nw_ref[...]).astype(o_ref.dtype)


def ssm_post(rw, y2, xbc, p, dx, nw):
    R = xbc.shape[0]
    tm = rw.tm
    vec = pl.BlockSpec((1, 512), lambda i: (0, 0))
    return pl.pallas_call(
        _ssm_post_kernel,
        grid=(R // tm,),
        in_specs=[
            pl.BlockSpec((2, tm, 512), lambda i: (0, i, 0)),
            pl.BlockSpec((tm, 512), lambda i: (i, 0)),
            pl.BlockSpec((tm, 512), lambda i: (i, C_Z // 512)),
            vec, vec,
        ],
        out_specs=pl.BlockSpec((tm, 512), lambda i: (i, 0)),
        out_shape=jax.ShapeDtypeStruct((R, 512), BF16),
        compiler_params=_cp("arbitrary"),
        name="ssm_post",
    )(y2, xbc, p, dx, nw)


def _unit_tri_inverse(a, eye):
    t = eye - a
    p = a
    for _ in range(5):
        p = _dot(p, p, HI)
        t = t + _dot(t, p, HI)
    return t


def _gdn_kernel(q_ref, k_ref, v_ref, sm_ref, gT_ref, e_ref, o_ref, s_ref):
    C = GDN_CHUNK
    d = pl.program_id(1)

    @pl.when(pl.program_id(2) == 0)
    def _():
        s_ref[...] = jnp.zeros_like(s_ref)

    keep, strict, keep_t, diag = _tri_masks(C, d)
    tri = keep.astype(F32)
    tri_t = keep_t.astype(F32)
    eye = diag.astype(F32)
    gb = _dot(sm_ref[...], e_ref[0], HI)
    gx = gb[:, :512]
    bx = gb[:, 512:]
    gcx = _dot(tri, gx, HI)
    cumr = _dot(gT_ref[0, 0], tri_t, HI)
    glx = jnp.sum(gx, axis=0, keepdims=True)
    for h in range(GDN_HEADS):
        hs = slice(h * 128, (h + 1) * 128)
        q = q_ref[:, hs]
        k = k_ref[:, hs]
        v = v_ref[:, hs]
        gch = gcx[:, hs]
        bh = bx[:, hs]
        dec = jnp.where(keep, jnp.exp(gch[:, 0:1] - cumr[h:h + 1, :]), 0.0)
        kb = k * bh
        kbf = k.astype(BF16)
        a = jnp.where(strict, _dot_nt(kb.astype(BF16), kbf) * dec, 0.0)
        t = _unit_tri_inverse(a, eye)
        eg = jnp.exp(gch)
        rhs = jnp.concatenate([v * bh, kb * eg], axis=1)
        sol = _dot(t, rhs, HI)
        u = sol[:, :GDN_DV]
        w = sol[:, GDN_DV:]
        qk = _dot_nt(q.astype(BF16), kbf) * dec
        qg = q * eg
        glh = glx[:, hs]
        kd = k * jnp.exp(glh - gch)
        S = s_ref[h]
        Sb = S.astype(BF16)
        v_new = u - _dot(w.astype(BF16), Sb)
        vnb = v_new.astype(BF16)
        o_ref[0, :, hs] = _dot(qg.astype(BF16), Sb) + _dot(qk.astype(BF16), vnb)
        s_ref[h] = S * jnp.exp(glh) + _dot_tn(kd.astype(BF16), vnb)


def gdn_scan(rw, q, k, v, sm, gT, e_gdn):
    C = GDN_CHUNK
    blk, nsteps = _scan_block(rw, C)
    R = q.shape[0]
    row = lambda b, d, s: (blk(b, d, s), 0)
    return pl.pallas_call(
        _gdn_kernel,
        grid=(rw.B, 2, nsteps),
        in_specs=[
            pl.BlockSpec((C, 512), row),
            pl.BlockSpec((C, 512), row),
            pl.BlockSpec((C, 512), row),
            pl.BlockSpec((C, 128), row),
            pl.BlockSpec((1, 1, 8, C), lambda b, d, s: (d, blk(b, d, s), 0, 0)),
            pl.BlockSpec((1, 128, 1024), lambda b, d, s: (d, 0, 0)),
        ],
        out_specs=pl.BlockSpec((1, C, 512), lambda b, d, s: (d, blk(b, d, s), 0)),
        out_shape=jax.ShapeDtypeStruct((2, R, 512), F32),
        scratch_shapes=[pltpu.VMEM((GDN_HEADS, GDN_DK, GDN_DV), F32)],
        compiler_params=_cp("arbitrary", "arbitrary", "arbitrary"),
        name="gdn_scan",
    )(q, k, v, sm, gT, e_gdn)


def _gdn_post_kernel(o_ref_in, g_ref, nw_ref, o_ref):
    o = o_ref_in[0] + o_ref_in[1]
    parts = []
    for h in range(GDN_HEADS):
        oh = o[:, h * 128:(h + 1) * 128]
        ms = jnp.mean(oh * oh, axis=-1, keepdims=True)
        parts.append(oh * lax.rsqrt(ms + EPS))
    o_ref[...] = (jnp.concatenate(parts, axis=1) * nw_ref[...] * _silu(g_ref[...])).astype(o_ref.dtype)


def gdn_post(rw, o2, p, nw):
    R = p.shape[0]
    tm = rw.tm
    return pl.pallas_call(
        _gdn_post_kernel,
        grid=(R // tm,),
        in_specs=[
            pl.BlockSpec((2, tm, 512), lambda i: (0, i, 0)),
            pl.BlockSpec((tm, 512), lambda i: (i, C_GG // 512)),
            pl.BlockSpec((1, 512), lambda i: (0, 0)),
        ],
        out_specs=pl.BlockSpec((tm, 512), lambda i: (i, 0)),
        out_shape=jax.ShapeDtypeStruct((R, 512), BF16),
        compiler_params=_cp("arbitrary"),
        name="gdn_post",
    )(o2, p, nw)


def _merge_kernel(yh_ref, ys_ref, yg_ref, g0_ref, g1_ref, g2_ref, w0_ref, w1_ref, w2_ref, wo_ref,
                  x_ref, mod_ref, o_ref):
    m = (_sigmoid(g0_ref[...]) * _dot(yh_ref[...], w0_ref[...])
         + _sigmoid(g1_ref[...]) * _dot(ys_ref[...], w1_ref[...])
         + _sigmoid(g2_ref[...]) * _dot(yg_ref[...], w2_ref[...]))
    o_ref[...] = x_ref[...] + mod_ref[0, 2:3, :] * _dot(m.astype(BF16), wo_ref[...])


def merge(rw, yh, ys, yg, p, w0, w1, w2, wo, x, mod):
    R, D = x.shape
    tm = min(rw.tm, 512)
    mi = rw.mod_index(tm)
    yspec = pl.BlockSpec((tm, 512), lambda i: (i, 0))
    gspec = lambda k: pl.BlockSpec((tm, D), lambda i: (i, C_GATE // D + k))
    wspec = pl.BlockSpec((512, D), lambda i: (0, 0))
    return pl.pallas_call(
        _merge_kernel,
        grid=(R // tm,),
        in_specs=[yspec, yspec, yspec, gspec(0), gspec(1), gspec(2), wspec, wspec, wspec,
                  pl.BlockSpec((D, D), lambda i: (0, 0)),
                  pl.BlockSpec((tm, D), lambda i: (i, 0)),
                  pl.BlockSpec((1, 8, D), lambda i: (mi(i), 0, 0))],
        out_specs=pl.BlockSpec((tm, D), lambda i: (i, 0)),
        out_shape=jax.ShapeDtypeStruct((R, D), F32),
        compiler_params=_cp("arbitrary"),
        name="merge",
    )(yh, ys, yg, p, p, p, w0, w1, w2, wo, x, mod)


def _swiglu_up_kernel(x_ref, nw_ref, mod_ref, wg_ref, wu_ref, o_ref, h_ref):
    @pl.when(pl.program_id(1) == 0)
    def _():
        h = _norm_mod(x_ref[...], nw_ref[...], mod_ref[0, 4:5, :], mod_ref[0, 3:4, :])
        h_ref[...] = h.astype(BF16)

    h = h_ref[...]
    g = _dot(h, wg_ref[...])
    u = _dot(h, wu_ref[...])
    o_ref[...] = (_silu(g) * u).astype(o_ref.dtype)


def swiglu_up(rw, x, nw, mod, wgu):
    R, D = x.shape
    tm = rw.tm
    tn = D_FF // 2
    nj = D_FF // tn
    mi = rw.mod_index(tm)
    return pl.pallas_call(
        _swiglu_up_kernel,
        grid=(R // tm, nj),
        in_specs=[
            pl.BlockSpec((tm, D), lambda i, j: (i, 0)),
            pl.BlockSpec((1, D), lambda i, j: (0, 0)),
            pl.BlockSpec((1, 8, D), lambda i, j: (mi(i), 0, 0)),
            pl.BlockSpec((D, tn), lambda i, j: (0, j)),
            pl.BlockSpec((D, tn), lambda i, j: (0, nj + j)),
        ],
        out_specs=pl.BlockSpec((tm, tn), lambda i, j: (i, j)),
        out_shape=jax.ShapeDtypeStruct((R, D_FF), BF16),
        scratch_shapes=[pltpu.VMEM((tm, D), BF16)],
        compiler_params=_cp("arbitrary", "arbitrary"),
        name="swiglu_up",
    )(x, nw.reshape(1, D), mod, wgu, wgu)


def _swiglu_down_kernel(a_ref, w_ref, x_ref, mod_ref, o_ref):
    o_ref[...] = x_ref[...] + mod_ref[0, 5:6, :] * _dot(a_ref[...], w_ref[...])


def swiglu_down(rw, a, w, x, mod):
    R, D = x.shape
    tm = min(rw.tm, 512)
    mi = rw.mod_index(tm)
    return pl.pallas_call(
        _swiglu_down_kernel,
        grid=(R // tm,),
        in_specs=[
            pl.BlockSpec((tm, D_FF), lambda i: (i, 0)),
            pl.BlockSpec((D_FF, D), lambda i: (0, 0)),
            pl.BlockSpec((tm, D), lambda i: (i, 0)),
            pl.BlockSpec((1, 8, D), lambda i: (mi(i), 0, 0)),
        ],
        out_specs=pl.BlockSpec((tm, D), lambda i: (i, 0)),
        out_shape=jax.ShapeDtypeStruct((R, D), F32),
        compiler_params=_cp("arbitrary"),
        name="swiglu_down",
    )(a, w, x, mod)


def _final_norm_kernel(x_ref, w_ref, o_ref):
    x = x_ref[...]
    ms = jnp.mean(x * x, axis=-1, keepdims=True)
    o_ref[...] = x * lax.rsqrt(ms + EPS) * w_ref[...]


def final_norm(rw, x, w):
    D = x.shape[1]
    tm = rw.tm
    n0 = rw.NC // tm
    nl = rw.B * rw.Ll
    return pl.pallas_call(
        _final_norm_kernel,
        grid=(nl // tm,),
        in_specs=[pl.BlockSpec((tm, D), lambda i: (n0 + i, 0)), pl.BlockSpec((1, D), lambda i: (0, 0))],
        out_specs=pl.BlockSpec((tm, D), lambda i: (i, 0)),
        out_shape=jax.ShapeDtypeStruct((nl, D), F32),
        compiler_params=_cp("arbitrary"),
        name="final_norm",
    )(x, w.reshape(1, D))


def _regroup_w_in(w_in):
    o_dt = 3072
    o_gdn = 3088
    o_a = o_gdn + 2048
    o_b = o_a + 8
    o_gate = o_gdn + 2064
    pieces = [
        w_in[..., 0:3072],
        w_in[..., o_gdn:o_gdn + 2048],
        w_in[..., o_gate:o_gate + 3072],
        w_in[..., o_dt:o_dt + 16],
        w_in[..., o_a:o_a + 4], w_in[..., o_b:o_b + 4],
        w_in[..., o_a + 4:o_a + 8], w_in[..., o_b + 4:o_b + 8],
        jnp.zeros(w_in.shape[:-1] + (96,), w_in.dtype),
    ]
    return jnp.concatenate(pieces, axis=-1).astype(BF16)


def _small_params(ssm_dt_bias, gdn_dt_bias, gdn_A_log):
    z4 = jnp.zeros((4,), F32)
    bias = jnp.concatenate([ssm_dt_bias.reshape(16), gdn_dt_bias[0], z4, gdn_dt_bias[1], z4, jnp.zeros((96,), F32)])
    alog = jnp.concatenate([jnp.zeros((16,), F32), gdn_A_log[0], z4, gdn_A_log[1], z4, jnp.zeros((96,), F32)])
    return bias.reshape(1, 128), alog.reshape(1, 128)


def _small_kind():
    kind = np.full((128,), 3, np.int32)
    kind[0:16] = 0
    kind[16:20] = 1
    kind[24:28] = 1
    kind[20:24] = 2
    kind[28:32] = 2
    return jnp.asarray(kind.reshape(1, 128))


def _expanders():
    e_ssm = np.zeros((2, 128, 512), np.float32)
    for d in range(2):
        for e in range(SSM_HEADS):
            e_ssm[d, d * 8 + e, e * 64:(e + 1) * 64] = 1.0
    e_gdn = np.zeros((2, 128, 1024), np.float32)
    for d in range(2):
        for h in range(GDN_HEADS):
            e_gdn[d, 16 + 8 * d + h, h * 128:(h + 1) * 128] = 1.0
            e_gdn[d, 16 + 8 * d + 4 + h, 512 + h * 128:512 + (h + 1) * 128] = 1.0
    return jnp.asarray(e_ssm), jnp.asarray(e_gdn)


def kernel(x, c, ctx, c_ctx, w_ada, b_ada, norm1_w, norm2_w, w_in, hy_conv_w, hy_conv_b, hy_w1, hy_b1, hy_w2, hy_b2, hy_w3, hy_freq, hy_bias, ssm_conv_w, ssm_conv_b, ssm_dt_bias, ssm_A_log, ssm_D, ssm_norm_w, gdn_conv_w, gdn_dt_bias, gdn_A_log, gdn_norm_w, w_hy_out, w_ssm_out, w_gdn_out, w_out, w_gate_up, w_down, final_norm_w):
    B, Ll, D = x.shape
    Lc = ctx.shape[1]
    depth = w_ada.shape[0]
    assert Lc == CONV_ROWS and D == D_MODEL and B <= 15
    rw = Rows(B, Lc, Ll)
    R, NC = rw.R, rw.NC

    xa = jnp.concatenate([ctx.reshape(B * Lc, D), x.reshape(B * Ll, D)], axis=0)

    svec = jnp.concatenate([c_ctx[None, :], c, jnp.zeros((15 - B, D), F32)], axis=0)
    mod_all = ada_modulation(svec, w_ada, b_ada)
    mod_all = jnp.pad(mod_all.reshape(depth, 16, 6, D), ((0, 0), (0, 0), (0, 2), (0, 0)))

    w_in_r = _regroup_w_in(w_in)
    kind = _small_kind()
    e_ssm, e_gdn = _expanders()
    fwd_l, inv_l = dft_tables(Ll)
    fwd_c, inv_c = dft_tables(Lc)
    tn_in = N_IN // 5

    for l in range(depth):
        mod = mod_all[l]
        p = norm_matmul(rw, xa, norm1_w[l], mod, w_in_r[l], tn_in)

        sbias, salog = _small_params(ssm_dt_bias[l], gdn_dt_bias[l], gdn_A_log[l])
        sm = small_prep(rw, p, sbias, salog, kind)
        sm32_t = sm[:, :32].T
        dt_t = sm32_t[:16].reshape(2, 8, R)
        g_t = sm32_t[16:32].reshape(2, 8, R // GDN_CHUNK, GDN_CHUNK).transpose(0, 2, 1, 3)

        xbc = conv_act(rw, p, C_XBC, ssm_conv_w[l], ssm_conv_b[l], act=True)
        alx = jnp.repeat(ssm_A_log[l], SSM_HEAD_DIM, axis=-1).reshape(2, 1, 512)
        alc = ssm_A_log[l].reshape(2, 8, 1)
        y2 = ssd_scan(rw, xbc, sm, dt_t, e_ssm, alx, alc)
        dx = jnp.repeat(ssm_D[l], SSM_HEAD_DIM).reshape(1, 512)
        y_ssm = ssm_post(rw, y2, xbc, p, dx, ssm_norm_w[l].reshape(1, 512))

        cw = gdn_conv_w[l]
        q = conv_act(rw, p, C_QKV, cw[:, 0:512], None, act=True, l2=True, l2_scale=GDN_DK ** -0.5)
        k = conv_act(rw, p, C_QKV + 512, cw[:, 512:1024], None, act=True, l2=True)
        v = conv_act(rw, p, C_QKV + 1024, cw[:, 1024:1536], None, act=True)
        o2 = gdn_scan(rw, q, k, v, sm, g_t, e_gdn)
        y_gdn = gdn_post(rw, o2, p, jnp.tile(gdn_norm_w[l], GDN_HEADS).reshape(1, 512))

        hyu = conv_act(rw, p, C_HY, hy_conv_w[l], hy_conv_b[l], act=False)
        parts = []
        for (Bn, L, blk0, fwd, inv) in ((B, Lc, 0, fwd_c, inv_c), (B, Ll, NC // Ll, fwd_l, inv_l)):
            if NC % L:
                raise ValueError("latent length must divide the context row count")
            filt = hy_filter(L, hy_w1[l], hy_b1[l], hy_w2[l], hy_b2[l], hy_w3[l], hy_freq[l])
            kspec = matmul(fwd, filt, min(512, 2 * L), 512)
            z1 = long_conv(Bn, L, hyu, blk0, 0, hyu, blk0, 1, hy_bias[l, 0], fwd, inv, kspec, 0, F32)
            yy = long_conv(Bn, L, z1, 0, 0, hyu, blk0, 2, hy_bias[l, 1], fwd, inv, kspec, 1, BF16)
            parts.append(yy)
        y_hy = jnp.concatenate(parts, axis=0)

        xa = merge(rw, y_hy, y_ssm, y_gdn, p, w_hy_out[l].astype(BF16), w_ssm_out[l].astype(BF16),
                   w_gdn_out[l].astype(BF16), w_out[l].astype(BF16), xa, mod)
        act = swiglu_up(rw, xa, norm2_w[l], mod, w_gate_up[l].astype(BF16))
        xa = swiglu_down(rw, act, w_down[l].astype(BF16), xa, mod)

    out = final_norm(rw, xa, final_norm_w)
    return out.reshape(B, Ll, D)
```

```python
import functools
import math

import jax
import jax.numpy as jnp
import numpy as np
from jax import lax
from jax.experimental import pallas as pl
from jax.experimental.pallas import tpu as pltpu

F32 = jnp.float32
BF16 = jnp.bfloat16
HI = lax.Precision.HIGHEST

EPS = 1e-6
D_MODEL = 1024
GRID_W = 64

HY_WIDTH = 512
HY_BANDS = 16
HY_EMB = 1 + 2 * HY_BANDS
HY_HIDDEN = 64
HY_SHORT_DECAY_PCT = 0.3
HY_LONG_DECAY_PCT = 1.5
HY_TARGET = 1e-2

SSM_HEADS = 8
SSM_HEAD_DIM = 64
SSM_WIDTH = 512
SSM_GROUPS = 2
SSM_HPG = 4
SSM_STATE = 128
SSM_CHUNK = 128
SSM_GW = SSM_HPG * SSM_HEAD_DIM

GDN_HEADS = 4
GDN_DK = 128
GDN_DV = 128
GDN_CHUNK = 64

D_FF = 2816

C_HY = 0
C_Z = 1536
C_XBC = 2048
C_QKV = 3072
C_GG = 4608
C_GATE = 5120
C_SM = 8192
N_IN = 8320

CONV_ROWS = 256
FREQ_BLK = 256

VMEM_LIMIT = 56 * 1024 * 1024


def _cp(*sem):
    return pltpu.CompilerParams(dimension_semantics=sem, vmem_limit_bytes=VMEM_LIMIT)


def _sigmoid(x):
    return 1.0 / (1.0 + jnp.exp(-x))


def _silu(x):
    return x * _sigmoid(x)


def _softplus(x):
    return jnp.maximum(x, 0.0) + jnp.log1p(jnp.exp(-jnp.abs(x)))


def _dot(a, b, precision=None):
    return jnp.dot(a, b, precision=precision, preferred_element_type=F32)


def _dot_nt(a, b):
    return lax.dot_general(a, b, (((1,), (1,)), ((), ())), preferred_element_type=F32)


def _dot_tn(a, b):
    return lax.dot_general(a, b, (((0,), (0,)), ((), ())), preferred_element_type=F32)


def _ada_kernel(s_ref, w_ref, b_ref, o_ref):
    s = _silu(s_ref[...])
    o_ref[0] = _dot(s, w_ref[0], HI) + b_ref[0]


def ada_modulation(svec, w_ada, b_ada):
    depth = w_ada.shape[0]
    D = D_MODEL
    return pl.pallas_call(
        _ada_kernel,
        grid=(depth, 6),
        in_specs=[
            pl.BlockSpec((16, D), lambda l, j: (0, 0)),
            pl.BlockSpec((1, D, D), lambda l, j: (l, 0, j)),
            pl.BlockSpec((1, 1, D), lambda l, j: (l, 0, j)),
        ],
        out_specs=pl.BlockSpec((1, 16, D), lambda l, j: (l, 0, j)),
        out_shape=jax.ShapeDtypeStruct((depth, 16, 6 * D), F32),
        compiler_params=_cp("arbitrary", "arbitrary"),
        name="ada",
    )(svec, w_ada, b_ada.reshape(depth, 1, 6 * D))


def _norm_mod(x, nw, scale, shift):
    ms = jnp.mean(x * x, axis=-1, keepdims=True)
    return (x * lax.rsqrt(ms + EPS) * nw) * (1.0 + scale) + shift


def _norm_matmul_kernel(x_ref, nw_ref, mod_ref, w_ref, o_ref, h_ref):
    @pl.when(pl.program_id(1) == 0)
    def _():
        h = _norm_mod(x_ref[...], nw_ref[...], mod_ref[0, 1:2, :], mod_ref[0, 0:1, :])
        h_ref[...] = h.astype(BF16)

    o_ref[...] = _dot(h_ref[...], w_ref[...])


class Rows:
    def __init__(self, B, Lc, Ll):
        self.B, self.Lc, self.Ll = B, Lc, Ll
        self.NC = B * Lc
        self.R = B * Lc + B * Ll
        assert self.NC % Ll == 0 or Ll % self.NC == 0
        tm = 1024
        while self.NC % tm or Ll % tm:
            tm //= 2
        self.tm = tm

    def mod_index(self, tm):
        nctx = self.NC // tm
        per = self.Ll // tm
        return lambda i: jnp.where(i < nctx, 0, 1 + (i - nctx) // per)


def norm_matmul(rw, x, nw, mod, w, tn):
    R, D = x.shape
    N = w.shape[1]
    tm = rw.tm
    mi = rw.mod_index(tm)
    return pl.pallas_call(
        _norm_matmul_kernel,
        grid=(R // tm, N // tn),
        in_specs=[
            pl.BlockSpec((tm, D), lambda i, j: (i, 0)),
            pl.BlockSpec((1, D), lambda i, j: (0, 0)),
            pl.BlockSpec((1, 8, D), lambda i, j: (mi(i), 0, 0)),
            pl.BlockSpec((D, tn), lambda i, j: (0, j)),
        ],
        out_specs=pl.BlockSpec((tm, tn), lambda i, j: (i, j)),
        out_shape=jax.ShapeDtypeStruct((R, N), F32),
        scratch_shapes=[pltpu.VMEM((tm, D), BF16)],
        compiler_params=_cp("arbitrary", "arbitrary"),
        name="norm_matmul",
    )(x, nw.reshape(1, D), mod, w)


def _conv_act_kernel(p_ref, w_ref, b_ref, o_ref, *, nctx_blk, act, l2, l2_scale):
    T = CONV_ROWS
    x = p_ref[...]
    row_len = jnp.where(pl.program_id(0) < nctx_blk, T, GRID_W)
    pos = lax.broadcasted_iota(jnp.int32, (T, 1), 0) & (row_len - 1)
    prev = jnp.where(pos == 0, 0.0, pltpu.roll(x, 1, 0))
    nxt = jnp.where(pos == row_len - 1, 0.0, pltpu.roll(x, T - 1, 0))
    y = prev * w_ref[0:1, :] + x * w_ref[1:2, :] + nxt * w_ref[2:3, :] + b_ref[...]
    if act:
        y = _silu(y)
    if l2:
        parts = []
        for h in range(y.shape[1] // 128):
            yh = y[:, h * 128:(h + 1) * 128]
            ss = jnp.sum(yh * yh, axis=-1, keepdims=True)
            parts.append(yh * lax.rsqrt(ss + EPS) * l2_scale)
        y = jnp.concatenate(parts, axis=1)
    o_ref[...] = y.astype(o_ref.dtype)


def conv_act(rw, p, col0, w, b, act, l2=False, l2_scale=1.0):
    R = p.shape[0]
    ncols = w.shape[1]
    cb = 512
    if b is None:
        b = jnp.zeros((ncols,), F32)
    kern = functools.partial(_conv_act_kernel, nctx_blk=rw.NC // CONV_ROWS, act=act, l2=l2,
                             l2_scale=l2_scale)
    return pl.pallas_call(
        kern,
        grid=(R // CONV_ROWS, ncols // cb),
        in_specs=[
            pl.BlockSpec((CONV_ROWS, cb), lambda i, j: (i, col0 // cb + j)),
            pl.BlockSpec((3, cb), lambda i, j: (0, j)),
            pl.BlockSpec((1, cb), lambda i, j: (0, j)),
        ],
        out_specs=pl.BlockSpec((CONV_ROWS, cb), lambda i, j: (i, j)),
        out_shape=jax.ShapeDtypeStruct((R, ncols), F32),
        compiler_params=_cp("arbitrary", "arbitrary"),
        name="conv_act",
    )(p, w, b.reshape(1, ncols))


def _small_kernel(p_ref, bias_ref, alog_ref, kind_ref, o_ref):
    x = p_ref[...]
    kind = kind_ref[...]
    sp = _softplus(x + bias_ref[...])
    g = -jnp.exp(alog_ref[...]) * sp
    o_ref[...] = jnp.where(kind == 0, sp, jnp.where(kind == 1, g, jnp.where(kind == 2, _sigmoid(x), 0.0)))


def small_prep(rw, p, bias, alog, kind):
    R = p.shape[0]
    tm = rw.tm
    vec = pl.BlockSpec((1, 128), lambda i: (0, 0))
    return pl.pallas_call(
        _small_kernel,
        grid=(R // tm,),
        in_specs=[pl.BlockSpec((tm, 128), lambda i: (i, C_SM // 128)), vec, vec, vec],
        out_specs=pl.BlockSpec((tm, 128), lambda i: (i, 0)),
        out_shape=jax.ShapeDtypeStruct((R, 128), F32),
        compiler_params=_cp("arbitrary"),
        name="small_prep",
    )(p, bias, alog, kind)


def _hy_filter_kernel(z_ref, w1_ref, b1_ref, w2_ref, b2_ref, w3_ref, f0_ref, f1_ref, win_ref, o_ref):
    h = jnp.sin(f0_ref[...] * (_dot(z_ref[...], w1_ref[...], HI) + b1_ref[...]))
    h = jnp.sin(f1_ref[...] * (_dot(h, w2_ref[...], HI) + b2_ref[...]))
    h = _dot(h, w3_ref[...], HI) * win_ref[...]
    tl = h.shape[0]
    row = lax.broadcasted_iota(jnp.int32, (tl, 1), 0) + pl.program_id(0) * tl
    drop = (row == 0) & (pl.program_id(1) % 2 == 1)
    o_ref[...] = jnp.where(drop, 0.0, h).astype(o_ref.dtype)


def hy_filter(L, w1, b1, w2, b2, w3, freq):
    t = jnp.linspace(0.0, 1.0, L, dtype=F32)[:, None]
    w = 2.0 * math.pi * jnp.arange(L, dtype=F32)[:, None] / L
    f = jnp.linspace(1e-4, HY_BANDS - 1, HY_BANDS, dtype=F32)[None, :]
    z = jnp.concatenate([t, jnp.cos(f * w), -jnp.sin(f * w)], axis=-1)
    z = jnp.pad(z, ((0, 0), (0, 128 - HY_EMB)))
    min_decay = math.log(HY_TARGET) / HY_LONG_DECAY_PCT
    max_decay = math.log(HY_TARGET) / HY_SHORT_DECAY_PCT
    deltas = jnp.linspace(min_decay, max_decay, HY_WIDTH, dtype=F32)
    window = jnp.exp(-t * jnp.abs(deltas))
    H = HY_HIDDEN
    w1p = jnp.pad(w1, ((0, 128 - HY_EMB), (0, 128 - H)))
    w2p = jnp.pad(w2, ((0, 128 - H), (0, 128 - H)))
    w3p = jnp.pad(w3, ((0, 128 - H), (0, 0)))
    pad1 = lambda v: jnp.pad(v, (0, 128 - H)).reshape(1, 128)
    tl = 256
    full = lambda shape: pl.BlockSpec(shape, lambda i, j: (0, 0))
    return pl.pallas_call(
        _hy_filter_kernel,
        grid=(L // tl, 4),
        in_specs=[
            pl.BlockSpec((tl, 128), lambda i, j: (i, 0)),
            full((128, 128)), full((1, 128)), full((128, 128)), full((1, 128)),
            pl.BlockSpec((128, HY_WIDTH), lambda i, j: (0, j)),
            full((1, 128)), full((1, 128)),
            pl.BlockSpec((tl, HY_WIDTH), lambda i, j: (i, 0)),
        ],
        out_specs=pl.BlockSpec((tl, HY_WIDTH), lambda i, j: (i, j)),
        out_shape=jax.ShapeDtypeStruct((L, 4 * HY_WIDTH), BF16),
        compiler_params=_cp("arbitrary", "arbitrary"),
        name="hy_filter",
    )(z, w1p, pad1(b1), w2p, pad1(b2), w3p, pad1(freq[0]), pad1(freq[1]), window)


def dft_tables(L):
    N = 2 * L
    f = jnp.arange(L, dtype=jnp.int32)[:, None]
    s = jnp.arange(L, dtype=jnp.int32)[None, :]
    ang = ((f * s) % N).astype(F32) * (2.0 * math.pi / N)
    c, sn = jnp.cos(ang), jnp.sin(ang)
    alt = (1 - 2 * (s % 2)).astype(F32)
    first = f == 0
    fwd = jnp.concatenate([c, jnp.where(first, alt, -sn)], axis=0)
    wgt = jnp.where(first, 1.0, 2.0) / N
    inv = jnp.concatenate([(c * wgt).T, jnp.where(first, alt / N, -sn * wgt).T], axis=1)
    return fwd.astype(BF16), inv.astype(BF16)


def _matmul_kernel(a_ref, b_ref, o_ref):
    o_ref[...] = _dot(a_ref[...], b_ref[...])


def matmul(a, b, tm, tn):
    M, K = a.shape
    N = b.shape[1]
    return pl.pallas_call(
        _matmul_kernel,
        grid=(M // tm, N // tn),
        in_specs=[pl.BlockSpec((tm, K), lambda i, j: (i, 0)), pl.BlockSpec((K, tn), lambda i, j: (0, j))],
        out_specs=pl.BlockSpec((tm, tn), lambda i, j: (i, j)),
        out_shape=jax.ShapeDtypeStruct((M, N), F32),
        compiler_params=_cp("arbitrary", "arbitrary"),
        name="matmul",
    )(a, b)


def _long_conv_kernel(u_ref, g_ref, bias_ref, fr_ref, fi_ref, ic_ref, is_ref,
                      ar0_ref, ar1_ref, ai0_ref, ai1_ref, o_ref, ub_ref, acc_ref):
    f = pl.program_id(1)

    @pl.when(f == 0)
    def _():
        ub_ref[...] = u_ref[...].astype(BF16)
        acc_ref[...] = jnp.zeros_like(acc_ref)

    ub = ub_ref[...]
    ur = _dot(fr_ref[...], ub)
    ui = _dot(fi_ref[...], ub)
    kr = ar0_ref[...] + ar1_ref[...]
    nyq = (lax.broadcasted_iota(jnp.int32, (FREQ_BLK, 1), 0) == 0) & (f == 0)
    ki = jnp.where(nyq, ai0_ref[...] + ai1_ref[...], ai0_ref[...] - ai1_ref[...])
    pr = jnp.where(nyq, ur * kr, ur * kr - ui * ki)
    pi = jnp.where(nyq, ui * ki, ur * ki + ui * kr)
    acc_ref[...] += _dot(ic_ref[...], pr.astype(BF16)) + _dot(is_ref[...], pi.astype(BF16))

    @pl.when(f == pl.num_programs(1) - 1)
    def _():
        u = u_ref[...]
        o_ref[...] = (g_ref[...] * (acc_ref[...] + u * bias_ref[...])).astype(o_ref.dtype)


def long_conv(B, L, u, u_rb0, u_cb, gate, g_rb0, gate_cb, bias, fwd, inv, kspec, order, out_dtype):
    C = HY_WIDTH
    nfb = L // FREQ_BLK
    FB = FREQ_BLK
    return pl.pallas_call(
        _long_conv_kernel,
        grid=(B, nfb),
        in_specs=[
            pl.BlockSpec((L, C), lambda b, f: (u_rb0 + b, u_cb)),
            pl.BlockSpec((L, C), lambda b, f: (g_rb0 + b, gate_cb)),
            pl.BlockSpec((1, C), lambda b, f: (0, 0)),
            pl.BlockSpec((FB, L), lambda b, f: (f, 0)),
            pl.BlockSpec((FB, L), lambda b, f: (nfb + f, 0)),
            pl.BlockSpec((L, FB), lambda b, f: (0, f)),
            pl.BlockSpec((L, FB), lambda b, f: (0, nfb + f)),
            pl.BlockSpec((FB, C), lambda b, f: (f, 2 * order)),
            pl.BlockSpec((FB, C), lambda b, f: (f, 2 * order + 1)),
            pl.BlockSpec((FB, C), lambda b, f: (nfb + f, 2 * order)),
            pl.BlockSpec((FB, C), lambda b, f: (nfb + f, 2 * order + 1)),
        ],
        out_specs=pl.BlockSpec((L, C), lambda b, f: (b, 0)),
        out_shape=jax.ShapeDtypeStruct((B * L, C), out_dtype),
        scratch_shapes=[pltpu.VMEM((L, C), BF16), pltpu.VMEM((L, C), F32)],
        compiler_params=_cp("arbitrary", "arbitrary"),
        name="long_conv",
    )(u, gate, bias.reshape(1, C), fwd, fwd, inv, inv, kspec, kspec, kspec, kspec)


def _scan_blocks(rw, rows):
    nbc, nbl, base = rw.Lc // rows, rw.Ll // rows, rw.NC // rows

    def make(d):
        def f(b, s):
            jc = s if d == 0 else nbc - 1 - s
            jl = (s - nbc) if d == 0 else nbl - 1 - (s - nbc)
            return jnp.where(s < nbc, b * nbc + jc, base + b * nbl + jl)
        return f

    return [make(0), make(1)], nbc + nbl


def _expand_lanes(x, base, n, width):
    rows = x.shape[0]
    per = 128 // width
    lane = lax.broadcasted_iota(jnp.int32, (rows, 128), 1)
    tiles = []
    for t in range(n // per):
        c0 = base + t * per
        tile = jnp.broadcast_to(x[:, c0:c0 + 1], (rows, 128))
        for i in range(1, per):
            tile = jnp.where(lane >= i * width, jnp.broadcast_to(x[:, c0 + i:c0 + i + 1], (rows, 128)), tile)
        tiles.append(tile)
    return jnp.concatenate(tiles, axis=1)


def _ssd_kernel(xf, bf, cf, smf, dtf, xb, bb, cb_, smb, dtb, alx_ref, alc_ref, of_ref, ob_ref, h_ref):
    Q = SSM_CHUNK
    GW = SSM_GW

    @pl.when(pl.program_id(1) == 0)
    def _():
        h_ref[...] = jnp.zeros_like(h_ref)

    row = lax.broadcasted_iota(jnp.int32, (Q, Q), 0)
    col = lax.broadcasted_iota(jnp.int32, (Q, Q), 1)
    lane_head = lax.broadcasted_iota(jnp.int32, (Q, GW), 1) // SSM_HEAD_DIM
    dirs = ((xf, bf, cf, smf, dtf, of_ref), (xb, bb, cb_, smb, dtb, ob_ref))
    jobs = []
    for d in range(2):
        x_ref, b_ref, c_ref, sm_ref, dt_ref, o_ref = dirs[d]
        keep = (col <= row) if d == 0 else (col >= row)
        tri = keep.astype(BF16)
        tri_t = ((row <= col) if d == 0 else (row >= col)).astype(BF16)
        sm = sm_ref[...]
        a_x = -jnp.exp(alx_ref[d])
        dtx = _expand_lanes(sm, 8 * d, SSM_HEADS, SSM_HEAD_DIM)
        cumx = _expand_lanes(_dot_01_lhs(tri, sm), 8 * d, SSM_HEADS, SSM_HEAD_DIM) * a_x
        cumr = _dot_01_rhs(dt_ref[0], tri_t) * (-jnp.exp(alc_ref[d]))
        last = Q - 1 if d == 0 else 0
        totx = cumx[last:last + 1, :]
        xd = x_ref[...] * dtx
        xdw = xd * jnp.exp(totx - cumx)
        ecum = jnp.exp(cumx)
        for g in range(SSM_GROUPS):
            gs = slice(g * GW, (g + 1) * GW)
            jobs.append(dict(d=d, g=g, gs=gs, keep=keep, cumx=cumx, cumr=cumr, o_ref=o_ref,
                             bg=b_ref[:, g * SSM_STATE:(g + 1) * SSM_STATE].astype(BF16),
                             cg=c_ref[:, g * SSM_STATE:(g + 1) * SSM_STATE].astype(BF16),
                             xdg=xd[:, gs], xdw=xdw[:, gs].astype(BF16), ecum=ecum[:, gs],
                             etot=jnp.exp(totx[:, gs])))
    for j in jobs:
        j["cb"] = _dot_nt(j["cg"], j["bg"])
        j["h"] = h_ref[j["d"], j["g"]]
    for j in jobs:
        ms, xs = [], []
        for e4 in range(SSM_HPG):
            e = j["g"] * SSM_HPG + e4
            diff = j["cumx"][:, e * SSM_HEAD_DIM:e * SSM_HEAD_DIM + 1] - j["cumr"][e:e + 1, :]
            ms.append((j["cb"] * jnp.where(j["keep"], jnp.exp(diff), 0.0)).astype(BF16))
            xs.append(jnp.where(lane_head == e4, j["xdg"], 0.0).astype(BF16))
        yd = _dot(jnp.concatenate(ms, axis=1), jnp.concatenate(xs, axis=0))
        y_off = _dot(j["cg"], j["h"].astype(BF16)) * j["ecum"]
        j["o_ref"][:, j["gs"]] = yd + y_off
    for j in jobs:
        h_ref[j["d"], j["g"]] = j["h"] * j["etot"] + _dot_tn(j["bg"], j["xdw"])


def ssd_scan(rw, xbc, sm, dtT, alx, alc):
    Q = SSM_CHUNK
    blks, nsteps = _scan_blocks(rw, Q)
    R = xbc.shape[0]
    in_specs = []
    for d in range(2):
        f = blks[d]
        in_specs += [
            pl.BlockSpec((Q, 512), lambda b, s, f=f: (f(b, s), 0)),
            pl.BlockSpec((Q, 256), lambda b, s, f=f: (f(b, s), 2)),
            pl.BlockSpec((Q, 256), lambda b, s, f=f: (f(b, s), 3)),
            pl.BlockSpec((Q, 128), lambda b, s, f=f: (f(b, s), 0)),
            pl.BlockSpec((1, 8, Q), lambda b, s, f=f, d=d: (d, 0, f(b, s))),
        ]
    in_specs += [pl.BlockSpec((2, 1, 512), lambda b, s: (0, 0, 0)), pl.BlockSpec((2, 8, 1), lambda b, s: (0, 0, 0))]
    ops = (xbc, xbc, xbc, sm, dtT)
    return pl.pallas_call(
        _ssd_kernel,
        grid=(rw.B, nsteps),
        in_specs=in_specs,
        out_specs=[pl.BlockSpec((Q, 512), lambda b, s, f=blks[d]: (f(b, s), 0)) for d in range(2)],
        out_shape=[jax.ShapeDtypeStruct((R, 512), F32)] * 2,
        scratch_shapes=[pltpu.VMEM((2, SSM_GROUPS, SSM_STATE, SSM_GW), F32)],
        compiler_params=_cp("arbitrary", "arbitrary"),
        name="ssd_scan",
    )(*ops, *ops, alx, alc)


def _ssm_post_kernel(yf_ref, yb_ref, x_ref, z_ref, dx_ref, nw_ref, o_ref):
    y = yf_ref[...] + yb_ref[...] + x_ref[...] * dx_ref[...]
    y = y * _silu(z_ref[...])
    parts = []
    for g in range(SSM_GROUPS):
        yg = y[:, g * SSM_GW:(g + 1) * SSM_GW]
        ms = jnp.mean(yg * yg, axis=-1, keepdims=True)
        parts.append(yg * lax.rsqrt(ms + EPS))
    o_ref[...] = (jnp.concatenate(parts, axis=1) * nw_ref[...]).astype(o_ref.dtype)


def ssm_post(rw, y_f, y_b, xbc, p, dx, nw):
    R = xbc.shape[0]
    tm = rw.tm
    vec = pl.BlockSpec((1, 512), lambda i: (0, 0))
    return pl.pallas_call(
        _ssm_post_kernel,
        grid=(R // tm,),
        in_specs=[
            pl.BlockSpec((tm, 512), lambda i: (i, 0)),
            pl.BlockSpec((tm, 512), lambda i: (i, 0)),
            pl.BlockSpec((tm, 512), lambda i: (i, 0)),
            pl.BlockSpec((tm, 512), lambda i: (i, C_Z // 512)),
            vec, vec,
        ],
        out_specs=pl.BlockSpec((tm, 512), lambda i: (i, 0)),
        out_shape=jax.ShapeDtypeStruct((R, 512), BF16),
        compiler_params=_cp("arbitrary"),
        name="ssm_post",
    )(y_f, y_b, xbc, p, dx, nw)


def _split3(x):
    x1 = x.astype(BF16)
    r = x - x1.astype(F32)
    x2 = r.astype(BF16)
    x3 = (r - x2.astype(F32)).astype(BF16)
    return x1, x2, x3


def _dot_01_lhs(m01, x):
    x1, x2, x3 = _split3(x)
    return _dot(m01, x1) + _dot(m01, x2) + _dot(m01, x3)


def _dot_01_rhs(x, m01):
    x1, x2, x3 = _split3(x)
    return _dot(x1, m01) + _dot(x2, m01) + _dot(x3, m01)


GDN_ROWS = 256


def _gdn_prep_kernel(q_ref, k_ref, v_ref, sm_ref, gT_ref, u_ref, w_ref, qg_ref, kd_ref, qk_ref, egl_ref):
    C = GDN_CHUNK
    row = lax.broadcasted_iota(jnp.int32, (C, C), 0)
    col = lax.broadcasted_iota(jnp.int32, (C, C), 1)
    jobs = []
    levels = []
    for d in range(2):
        keep = (col <= row) if d == 0 else (col >= row)
        late, early = (row, col) if d == 0 else (col, row)
        levels.append([(((row ^ col) >> (j + 1)) == 0) & ((late & (1 << j)) != 0) & ((early & (1 << j)) == 0)
                       for j in range(6)])
        tri = keep.astype(BF16)
        tri_t = ((row <= col) if d == 0 else (row >= col)).astype(BF16)
        last = C - 1 if d == 0 else 0
        for c in range(GDN_ROWS // C):
            rows = slice(c * C, (c + 1) * C)
            smc = sm_ref[rows, :]
            cums = _dot_01_lhs(tri, smc)
            cumr = _dot_01_rhs(gT_ref[c, 8 * d:8 * d + 8, :], tri_t)
            tot = cums[last:last + 1, :]
            for h in range(GDN_HEADS):
                lg = 16 + 8 * d + h
                jobs.append(dict(d=d, c=c, h=h, rows=rows, hs=slice(h * 128, (h + 1) * 128), keep=keep,
                                 gc=cums[:, lg:lg + 1], beta=smc[:, lg + 4:lg + 5],
                                 gl=tot[:, lg:lg + 1], gr=cumr[h:h + 1, :]))
    for j in jobs:
        q = q_ref[j["rows"], j["hs"]]
        k = k_ref[j["rows"], j["hs"]]
        j["dec"] = jnp.where(j["keep"], jnp.exp(j["gc"] - j["gr"]), 0.0)
        kb = k * j["beta"]
        both = _dot_nt(jnp.concatenate([kb, q], axis=0).astype(BF16), k.astype(BF16))
        j["a"] = both[:C] * j["dec"]
        j["n"] = -jnp.where(levels[j["d"]][0], j["a"], 0.0)
        qk_ref[j["d"], j["c"], j["h"]] = (both[C:] * j["dec"]).astype(BF16)
    for lev in range(1, 6):
        for j in jobs:
            l = jnp.where(levels[j["d"]][lev], j["a"], 0.0)
            j["y"] = l + _dot(l.astype(BF16), j["n"].astype(BF16))
        for j in jobs:
            j["n"] = j["n"] - j["y"] - _dot(j["n"].astype(BF16), j["y"].astype(BF16))
    for j in jobs:
        d, rows, hs, gc, gl, beta = j["d"], j["rows"], j["hs"], j["gc"], j["gl"], j["beta"]
        q = q_ref[rows, hs]
        k = k_ref[rows, hs]
        eg = jnp.exp(gc)
        rhs = jnp.concatenate([v_ref[rows, hs] * beta, k * beta * eg], axis=1)
        sol = rhs + _dot(j["n"].astype(BF16), rhs.astype(BF16))
        u_ref[d, rows, hs] = sol[:, :GDN_DV]
        w_ref[d, rows, hs] = sol[:, GDN_DV:].astype(BF16)
        qg_ref[d, rows, hs] = (q * eg).astype(BF16)
        kd_ref[d, rows, hs] = (k * jnp.exp(gl - gc)).astype(BF16)
        egl_ref[d, j["c"], :, hs] = jnp.broadcast_to(jnp.exp(gl), (8, 128))


def gdn_prep(q, k, v, sm, gT):
    R = q.shape[0]
    T, C = GDN_ROWS, GDN_CHUNK
    nc = T // C
    row = pl.BlockSpec((T, 512), lambda i: (i, 0))
    dirrow = pl.BlockSpec((2, T, 512), lambda i: (0, i, 0))
    return pl.pallas_call(
        _gdn_prep_kernel,
        grid=(R // T,),
        in_specs=[row, row, row,
                  pl.BlockSpec((T, 128), lambda i: (i, 0)),
                  pl.BlockSpec((nc, 16, C), lambda i: (i, 0, 0))],
        out_specs=[dirrow, dirrow, dirrow, dirrow,
                   pl.BlockSpec((2, nc, GDN_HEADS, C, C), lambda i: (0, i, 0, 0, 0)),
                   pl.BlockSpec((2, nc, 8, 512), lambda i: (0, i, 0, 0))],
        out_shape=[jax.ShapeDtypeStruct((2, R, 512), F32),
                   jax.ShapeDtypeStruct((2, R, 512), BF16),
                   jax.ShapeDtypeStruct((2, R, 512), BF16),
                   jax.ShapeDtypeStruct((2, R, 512), BF16),
                   jax.ShapeDtypeStruct((2, R // C, GDN_HEADS, C, C), BF16),
                   jax.ShapeDtypeStruct((2, R // C, 8, 512), F32)],
        compiler_params=_cp("arbitrary"),
        name="gdn_prep",
    )(q, k, v, sm, gT)


def _gdn_scan_kernel(uf, wf, qgf, kdf, qkf, eglf, ub, wb, qgb, kdb, qkb, eglb, of_ref, ob_ref, s_ref):
    C = GDN_CHUNK
    nch = GDN_ROWS // C

    @pl.when(pl.program_id(1) == 0)
    def _():
        s_ref[...] = jnp.zeros_like(s_ref)

    dirs = ((uf, wf, qgf, kdf, qkf, eglf, of_ref), (ub, wb, qgb, kdb, qkb, eglb, ob_ref))
    chains = [(d, h) for d in range(2) for h in range(GDN_HEADS)]
    S = {ch: s_ref[ch[0], ch[1]] for ch in chains}
    for i in range(nch):
        Sb, vnb, rows_of, c_of = {}, {}, {}, {}
        for d, h in chains:
            c_of[d] = i if d == 0 else nch - 1 - i
            rows_of[d] = slice(c_of[d] * C, (c_of[d] + 1) * C)
        for d, h in chains:
            hs = slice(h * 128, (h + 1) * 128)
            Sb[d, h] = S[d, h].astype(BF16)
            v_new = dirs[d][0][0, rows_of[d], hs] - _dot(dirs[d][1][0, rows_of[d], hs], Sb[d, h])
            vnb[d, h] = v_new.astype(BF16)
        for d, h in chains:
            hs = slice(h * 128, (h + 1) * 128)
            u_ref, w_ref, qg_ref, kd_ref, qk_ref, egl_ref, o_ref = dirs[d]
            S[d, h] = S[d, h] * egl_ref[0, c_of[d], 0:1, hs] + _dot_tn(kd_ref[0, rows_of[d], hs], vnb[d, h])
        for d, h in chains:
            hs = slice(h * 128, (h + 1) * 128)
            u_ref, w_ref, qg_ref, kd_ref, qk_ref, egl_ref, o_ref = dirs[d]
            o_ref[rows_of[d], hs] = (_dot(qg_ref[0, rows_of[d], hs], Sb[d, h])
                                     + _dot(qk_ref[0, c_of[d], h], vnb[d, h]))
    for ch in chains:
        s_ref[ch[0], ch[1]] = S[ch]


def gdn_scan(rw, u, w, qg, kd, qk, egl):
    T, C = GDN_ROWS, GDN_CHUNK
    nc = T // C
    R = u.shape[1]
    nbc, nbl, base = rw.Lc // T, rw.Ll // T, rw.NC // T

    def blk(d):
        def f(b, s):
            jc = s if d == 0 else nbc - 1 - s
            jl = (s - nbc) if d == 0 else nbl - 1 - (s - nbc)
            return jnp.where(s < nbc, b * nbc + jc, base + b * nbl + jl)
        return f

    in_specs = []
    for d in range(2):
        f = blk(d)
        rowspec = pl.BlockSpec((1, T, 512), lambda b, s, f=f, d=d: (d, f(b, s), 0))
        in_specs += [rowspec, rowspec, rowspec, rowspec,
                     pl.BlockSpec((1, nc, GDN_HEADS, C, C), lambda b, s, f=f, d=d: (d, f(b, s), 0, 0, 0)),
                     pl.BlockSpec((1, nc, 8, 512), lambda b, s, f=f, d=d: (d, f(b, s), 0, 0))]
    out_specs = [pl.BlockSpec((T, 512), lambda b, s, f=blk(d): (f(b, s), 0)) for d in range(2)]
    ops = (u, w, qg, kd, qk, egl)
    return pl.pallas_call(
        _gdn_scan_kernel,
        grid=(rw.B, nbc + nbl),
        in_specs=in_specs,
        out_specs=out_specs,
        out_shape=[jax.ShapeDtypeStruct((R, 512), F32)] * 2,
        scratch_shapes=[pltpu.VMEM((2, GDN_HEADS, GDN_DK, GDN_DV), F32)],
        compiler_params=_cp("arbitrary", "arbitrary"),
        name="gdn_scan",
    )(*ops, *ops)


def _gdn_post_kernel(of_ref, ob_ref, g_ref, nw_ref, o_ref):
    o = of_ref[...] + ob_ref[...]
    parts = []
    for h in range(GDN_HEADS):
        oh = o[:, h * 128:(h + 1) * 128]
        ms = jnp.mean(oh * oh, axis=-1, keepdims=True)
        parts.append(oh * lax.rsqrt(ms + EPS))
    o_ref[...] = (jnp.concatenate(parts, axis=1) * nw_ref[...] * _silu(g_ref[...])).astype(o_ref.dtype)


def gdn_post(rw, o_f, o_b, p, nw):
    R = p.shape[0]
    tm = rw.tm
    return pl.pallas_call(
        _gdn_post_kernel,
        grid=(R // tm,),
        in_specs=[
            pl.BlockSpec((tm, 512), lambda i: (i, 0)),
            pl.BlockSpec((tm, 512), lambda i: (i, 0)),
            pl.BlockSpec((tm, 512), lambda i: (i, C_GG // 512)),
            pl.BlockSpec((1, 512), lambda i: (0, 0)),
        ],
        out_specs=pl.BlockSpec((tm, 512), lambda i: (i, 0)),
        out_shape=jax.ShapeDtypeStruct((R, 512), BF16),
        compiler_params=_cp("arbitrary"),
        name="gdn_post",
    )(o_f, o_b, p, nw)


def _merge_kernel(yh_ref, ys_ref, yg_ref, g0_ref, g1_ref, g2_ref, w0_ref, w1_ref, w2_ref, wo_ref,
                  x_ref, mod_ref, o_ref):
    m = (_sigmoid(g0_ref[...]) * _dot(yh_ref[...], w0_ref[...])
         + _sigmoid(g1_ref[...]) * _dot(ys_ref[...], w1_ref[...])
         + _sigmoid(g2_ref[...]) * _dot(yg_ref[...], w2_ref[...]))
    o_ref[...] = x_ref[...] + mod_ref[0, 2:3, :] * _dot(m.astype(BF16), wo_ref[...])


def merge(rw, yh, ys, yg, p, w0, w1, w2, wo, x, mod):
    R, D = x.shape
    tm = min(rw.tm, 512)
    mi = rw.mod_index(tm)
    yspec = pl.BlockSpec((tm, 512), lambda i: (i, 0))
    gspec = lambda k: pl.BlockSpec((tm, D), lambda i: (i, C_GATE // D + k))
    wspec = pl.BlockSpec((512, D), lambda i: (0, 0))
    return pl.pallas_call(
        _merge_kernel,
        grid=(R // tm,),
        in_specs=[yspec, yspec, yspec, gspec(0), gspec(1), gspec(2), wspec, wspec, wspec,
                  pl.BlockSpec((D, D), lambda i: (0, 0)),
                  pl.BlockSpec((tm, D), lambda i: (i, 0)),
                  pl.BlockSpec((1, 8, D), lambda i: (mi(i), 0, 0))],
        out_specs=pl.BlockSpec((tm, D), lambda i: (i, 0)),
        out_shape=jax.ShapeDtypeStruct((R, D), F32),
        compiler_params=_cp("arbitrary"),
        name="merge",
    )(yh, ys, yg, p, p, p, w0, w1, w2, wo, x, mod)


def _swiglu_up_kernel(x_ref, nw_ref, mod_ref, wg_ref, wu_ref, o_ref, h_ref):
    @pl.when(pl.program_id(1) == 0)
    def _():
        h = _norm_mod(x_ref[...], nw_ref[...], mod_ref[0, 4:5, :], mod_ref[0, 3:4, :])
        h_ref[...] = h.astype(BF16)

    h = h_ref[...]
    g = _dot(h, wg_ref[...])
    u = _dot(h, wu_ref[...])
    o_ref[...] = (_silu(g) * u).astype(o_ref.dtype)


def swiglu_up(rw, x, nw, mod, wgu):
    R, D = x.shape
    tm = rw.tm
    tn = D_FF // 2
    nj = D_FF // tn
    mi = rw.mod_index(tm)
    return pl.pallas_call(
        _swiglu_up_kernel,
        grid=(R // tm, nj),
        in_specs=[
            pl.BlockSpec((tm, D), lambda i, j: (i, 0)),
            pl.BlockSpec((1, D), lambda i, j: (0, 0)),
            pl.BlockSpec((1, 8, D), lambda i, j: (mi(i), 0, 0)),
            pl.BlockSpec((D, tn), lambda i, j: (0, j)),
            pl.BlockSpec((D, tn), lambda i, j: (0, nj + j)),
        ],
        out_specs=pl.BlockSpec((tm, tn), lambda i, j: (i, j)),
        out_shape=jax.ShapeDtypeStruct((R, D_FF), BF16),
        scratch_shapes=[pltpu.VMEM((tm, D), BF16)],
        compiler_params=_cp("arbitrary", "arbitrary"),
        name="swiglu_up",
    )(x, nw.reshape(1, D), mod, wgu, wgu)


def _swiglu_down_kernel(a_ref, w_ref, x_ref, mod_ref, o_ref):
    o_ref[...] = x_ref[...] + mod_ref[0, 5:6, :] * _dot(a_ref[...], w_ref[...])


def swiglu_down(rw, a, w, x, mod):
    R, D = x.shape
    tm = min(rw.tm, 512)
    mi = rw.mod_index(tm)
    return pl.pallas_call(
        _swiglu_down_kernel,
        grid=(R // tm,),
        in_specs=[
            pl.BlockSpec((tm, D_FF), lambda i: (i, 0)),
            pl.BlockSpec((D_FF, D), lambda i: (0, 0)),
            pl.BlockSpec((tm, D), lambda i: (i, 0)),
            pl.BlockSpec((1, 8, D), lambda i: (mi(i), 0, 0)),
        ],
        out_specs=pl.BlockSpec((tm, D), lambda i: (i, 0)),
        out_shape=jax.ShapeDtypeStruct((R, D), F32),
        compiler_params=_cp("arbitrary"),
        name="swiglu_down",
    )(a, w, x, mod)


def _final_norm_kernel(x_ref, w_ref, o_ref):
    x = x_ref[...]
    ms = jnp.mean(x * x, axis=-1, keepdims=True)
    o_ref[...] = x * lax.rsqrt(ms + EPS) * w_ref[...]


def final_norm(rw, x, w):
    D = x.shape[1]
    tm = rw.tm
    n0 = rw.NC // tm
    nl = rw.B * rw.Ll
    return pl.pallas_call(
        _final_norm_kernel,
        grid=(nl // tm,),
        in_specs=[pl.BlockSpec((tm, D), lambda i: (n0 + i, 0)), pl.BlockSpec((1, D), lambda i: (0, 0))],
        out_specs=pl.BlockSpec((tm, D), lambda i: (i, 0)),
        out_shape=jax.ShapeDtypeStruct((nl, D), F32),
        compiler_params=_cp("arbitrary"),
        name="final_norm",
    )(x, w.reshape(1, D))


def _regroup_w_in(w_in):
    o_dt = 3072
    o_gdn = 3088
    o_a = o_gdn + 2048
    o_b = o_a + 8
    o_gate = o_gdn + 2064
    pieces = [
        w_in[..., 0:3072],
        w_in[..., o_gdn:o_gdn + 2048],
        w_in[..., o_gate:o_gate + 3072],
        w_in[..., o_dt:o_dt + 16],
        w_in[..., o_a:o_a + 4], w_in[..., o_b:o_b + 4],
        w_in[..., o_a + 4:o_a + 8], w_in[..., o_b + 4:o_b + 8],
        jnp.zeros(w_in.shape[:-1] + (96,), w_in.dtype),
    ]
    return jnp.concatenate(pieces, axis=-1).astype(BF16)


def _small_params(ssm_dt_bias, gdn_dt_bias, gdn_A_log):
    z4 = jnp.zeros((4,), F32)
    bias = jnp.concatenate([ssm_dt_bias.reshape(16), gdn_dt_bias[0], z4, gdn_dt_bias[1], z4, jnp.zeros((96,), F32)])
    alog = jnp.concatenate([jnp.zeros((16,), F32), gdn_A_log[0], z4, gdn_A_log[1], z4, jnp.zeros((96,), F32)])
    return bias.reshape(1, 128), alog.reshape(1, 128)


def _small_kind():
    kind = np.full((128,), 3, np.int32)
    kind[0:16] = 0
    kind[16:20] = 1
    kind[24:28] = 1
    kind[20:24] = 2
    kind[28:32] = 2
    return jnp.asarray(kind.reshape(1, 128))


def kernel(x, c, ctx, c_ctx, w_ada, b_ada, norm1_w, norm2_w, w_in, hy_conv_w, hy_conv_b, hy_w1, hy_b1, hy_w2, hy_b2, hy_w3, hy_freq, hy_bias, ssm_conv_w, ssm_conv_b, ssm_dt_bias, ssm_A_log, ssm_D, ssm_norm_w, gdn_conv_w, gdn_dt_bias, gdn_A_log, gdn_norm_w, w_hy_out, w_ssm_out, w_gdn_out, w_out, w_gate_up, w_down, final_norm_w):
    B, Ll, D = x.shape
    Lc = ctx.shape[1]
    depth = w_ada.shape[0]
    assert Lc == CONV_ROWS and D == D_MODEL and B <= 15
    rw = Rows(B, Lc, Ll)
    R, NC = rw.R, rw.NC

    xa = jnp.concatenate([ctx.reshape(B * Lc, D), x.reshape(B * Ll, D)], axis=0)

    svec = jnp.concatenate([c_ctx[None, :], c, jnp.zeros((15 - B, D), F32)], axis=0)
    mod_all = ada_modulation(svec, w_ada, b_ada)
    mod_all = jnp.pad(mod_all.reshape(depth, 16, 6, D), ((0, 0), (0, 0), (0, 2), (0, 0)))

    w_in_r = _regroup_w_in(w_in)
    kind = _small_kind()
    fwd_l, inv_l = dft_tables(Ll)
    fwd_c, inv_c = dft_tables(Lc)
    tn_in = N_IN // 5

    for l in range(depth):
        mod = mod_all[l]
        p = norm_matmul(rw, xa, norm1_w[l], mod, w_in_r[l], tn_in)

        sbias, salog = _small_params(ssm_dt_bias[l], gdn_dt_bias[l], gdn_A_log[l])
        sm = small_prep(rw, p, sbias, salog, kind)
        sm32_t = sm[:, :32].T
        dt_t = sm32_t[:16].reshape(2, 8, R)
        g_t = sm32_t[16:32].reshape(16, R // GDN_CHUNK, GDN_CHUNK).transpose(1, 0, 2)

        xbc = conv_act(rw, p, C_XBC, ssm_conv_w[l], ssm_conv_b[l], act=True)
        alx = jnp.repeat(ssm_A_log[l], SSM_HEAD_DIM, axis=-1).reshape(2, 1, 512)
        alc = ssm_A_log[l].reshape(2, 8, 1)
        y_f, y_b = ssd_scan(rw, xbc, sm, dt_t, alx, alc)
        dx = jnp.repeat(ssm_D[l], SSM_HEAD_DIM).reshape(1, 512)
        y_ssm = ssm_post(rw, y_f, y_b, xbc, p, dx, ssm_norm_w[l].reshape(1, 512))

        cw = gdn_conv_w[l]
        q = conv_act(rw, p, C_QKV, cw[:, 0:512], None, act=True, l2=True, l2_scale=GDN_DK ** -0.5)
        k = conv_act(rw, p, C_QKV + 512, cw[:, 512:1024], None, act=True, l2=True)
        v = conv_act(rw, p, C_QKV + 1024, cw[:, 1024:1536], None, act=True)
        o_f, o_b = gdn_scan(rw, *gdn_prep(q, k, v, sm, g_t))
        y_gdn = gdn_post(rw, o_f, o_b, p, jnp.tile(gdn_norm_w[l], GDN_HEADS).reshape(1, 512))

        hyu = conv_act(rw, p, C_HY, hy_conv_w[l], hy_conv_b[l], act=False)
        parts = []
        for (Bn, L, blk0, fwd, inv) in ((B, Lc, 0, fwd_c, inv_c), (B, Ll, NC // Ll, fwd_l, inv_l)):
            if NC % L:
                raise ValueError("latent length must divide the context row count")
            filt = hy_filter(L, hy_w1[l], hy_b1[l], hy_w2[l], hy_b2[l], hy_w3[l], hy_freq[l])
            kspec = matmul(fwd, filt, min(512, 2 * L), 512)
            z1 = long_conv(Bn, L, hyu, blk0, 0, hyu, blk0, 1, hy_bias[l, 0], fwd, inv, kspec, 0, F32)
            yy = long_conv(Bn, L, z1, 0, 0, hyu, blk0, 2, hy_bias[l, 1], fwd, inv, kspec, 1, BF16)
            parts.append(yy)
        y_hy = jnp.concatenate(parts, axis=0)

        xa = merge(rw, y_hy, y_ssm, y_gdn, p, w_hy_out[l].astype(BF16), w_ssm_out[l].astype(BF16),
                   w_gdn_out[l].astype(BF16), w_out[l].astype(BF16), xa, mod)
        act = swiglu_up(rw, xa, norm2_w[l], mod, w_gate_up[l].astype(BF16))
        xa = swiglu_down(rw, act, w_down[l].astype(BF16), xa, mod)

    out = final_norm(rw, xa, final_norm_w)
    return out.reshape(B, Ll, D)
```

```python
import functools
import math

import jax
import jax.numpy as jnp
import numpy as np
from jax import lax
from jax.experimental import pallas as pl
from jax.experimental.pallas import tpu as pltpu

F32 = jnp.float32
BF16 = jnp.bfloat16
HI = lax.Precision.HIGHEST

EPS = 1e-6
D_MODEL = 1024
GRID_W = 64

HY_WIDTH = 512
HY_BANDS = 16
HY_EMB = 1 + 2 * HY_BANDS
HY_HIDDEN = 64
HY_SHORT_DECAY_PCT = 0.3
HY_LONG_DECAY_PCT = 1.5
HY_TARGET = 1e-2

SSM_HEADS = 8
SSM_HEAD_DIM = 64
SSM_WIDTH = 512
SSM_GROUPS = 2
SSM_HPG = 4
SSM_STATE = 128
SSM_CHUNK = 128
SSM_GW = SSM_HPG * SSM_HEAD_DIM

GDN_HEADS = 4
GDN_DK = 128
GDN_DV = 128
GDN_CHUNK = 64

D_FF = 2816

C_HY = 0
C_Z = 1536
C_XBC = 2048
C_QKV = 3072
C_GG = 4608
C_GATE = 5120
C_SM = 8192

CONV_ROWS = 256
FREQ_BLK = 256

VMEM_LIMIT = 56 * 1024 * 1024


def _cp(*sem):
    return pltpu.CompilerParams(dimension_semantics=sem, vmem_limit_bytes=VMEM_LIMIT)


def _sigmoid(x):
    return 1.0 / (1.0 + jnp.exp(-x))


def _silu(x):
    return x * _sigmoid(x)


def _softplus(x):
    return jnp.maximum(x, 0.0) + jnp.log1p(jnp.exp(-jnp.abs(x)))


def _dot(a, b, precision=None):
    return jnp.dot(a, b, precision=precision, preferred_element_type=F32)


def _dot_nt(a, b):
    return lax.dot_general(a, b, (((1,), (1,)), ((), ())), preferred_element_type=F32)


def _dot_tn(a, b):
    return lax.dot_general(a, b, (((0,), (0,)), ((), ())), preferred_element_type=F32)


def _ada_kernel(s_ref, w_ref, b_ref, o_ref):
    s = _silu(s_ref[...])
    o_ref[0] = _dot(s, w_ref[0], HI) + b_ref[0]


def ada_modulation(svec, w_ada, b_ada):
    depth = w_ada.shape[0]
    D = D_MODEL
    return pl.pallas_call(
        _ada_kernel,
        grid=(depth, 6),
        in_specs=[
            pl.BlockSpec((16, D), lambda l, j: (0, 0)),
            pl.BlockSpec((1, D, D), lambda l, j: (l, 0, j)),
            pl.BlockSpec((1, 1, D), lambda l, j: (l, 0, j)),
        ],
        out_specs=pl.BlockSpec((1, 16, D), lambda l, j: (l, 0, j)),
        out_shape=jax.ShapeDtypeStruct((depth, 16, 6 * D), F32),
        compiler_params=_cp("arbitrary", "arbitrary"),
        name="ada",
    )(svec, w_ada, b_ada.reshape(depth, 1, 6 * D))


def _norm_mod(x, nw, scale, shift):
    ms = jnp.mean(x * x, axis=-1, keepdims=True)
    return (x * lax.rsqrt(ms + EPS) * nw) * (1.0 + scale) + shift


IN_TN = 512
IN_MODE_CONV = (0, 1, 2)
IN_MODE_CONV_SILU = (4, 5, 8)
IN_MODE_CONV_SILU_L2 = (6, 7)
IN_MODE_SMALL = (16,)
IN_MODE_RAW = (3, 9, 10, 11, 12, 13, 14, 15)
N_IN_PAD = 17 * IN_TN
PAR_W0, PAR_W1, PAR_W2, PAR_BIAS, PAR_L2SCALE, PAR_SBIAS, PAR_SALOG, PAR_SKIND = range(8)


def _in_proj_kernel(x_ref, nw_ref, mod_ref, w_ref, par_ref, o_ref, h_ref, *, nctx_blk):
    j = pl.program_id(1)

    @pl.when(j == 0)
    def _():
        h = _norm_mod(x_ref[...], nw_ref[...], mod_ref[0, 1:2, :], mod_ref[0, 0:1, :])
        h_ref[...] = h.astype(BF16)

    acc = _dot(h_ref[...], w_ref[...])
    T = acc.shape[0]
    row_len = jnp.where(pl.program_id(0) < nctx_blk, CONV_ROWS, GRID_W)
    pos = lax.broadcasted_iota(jnp.int32, (T, 1), 0) & (row_len - 1)
    is_first = pos == 0
    is_last = pos == row_len - 1

    def conv():
        prev = jnp.where(is_first, 0.0, pltpu.roll(acc, 1, 0))
        nxt = jnp.where(is_last, 0.0, pltpu.roll(acc, T - 1, 0))
        return (prev * par_ref[PAR_W0:PAR_W0 + 1, :] + acc * par_ref[PAR_W1:PAR_W1 + 1, :]
                + nxt * par_ref[PAR_W2:PAR_W2 + 1, :] + par_ref[PAR_BIAS:PAR_BIAS + 1, :])

    def is_in(js):
        m = j == js[0]
        for jj in js[1:]:
            m = m | (j == jj)
        return m

    @pl.when(is_in(IN_MODE_RAW))
    def _():
        o_ref[...] = acc

    @pl.when(is_in(IN_MODE_CONV))
    def _():
        o_ref[...] = conv()

    @pl.when(is_in(IN_MODE_CONV_SILU))
    def _():
        o_ref[...] = _silu(conv())

    @pl.when(is_in(IN_MODE_CONV_SILU_L2))
    def _():
        y = _silu(conv())
        parts = []
        for h in range(IN_TN // 128):
            yh = y[:, h * 128:(h + 1) * 128]
            parts.append(yh * lax.rsqrt(jnp.sum(yh * yh, axis=-1, keepdims=True) + EPS))
        o_ref[...] = jnp.concatenate(parts, axis=1) * par_ref[PAR_L2SCALE:PAR_L2SCALE + 1, :]

    @pl.when(is_in(IN_MODE_SMALL))
    def _():
        kind = par_ref[PAR_SKIND:PAR_SKIND + 1, :]
        sp = _softplus(acc + par_ref[PAR_SBIAS:PAR_SBIAS + 1, :])
        g = -jnp.exp(par_ref[PAR_SALOG:PAR_SALOG + 1, :]) * sp
        o_ref[...] = jnp.where(kind == 0.0, sp, jnp.where(kind == 1.0, g,
                                                          jnp.where(kind == 2.0, _sigmoid(acc), 0.0)))


class Rows:
    def __init__(self, B, Lc, Ll):
        self.B, self.Lc, self.Ll = B, Lc, Ll
        self.NC = B * Lc
        self.R = B * Lc + B * Ll
        assert self.NC % Ll == 0 or Ll % self.NC == 0
        tm = 1024
        while self.NC % tm or Ll % tm:
            tm //= 2
        self.tm = tm

    def mod_index(self, tm):
        nctx = self.NC // tm
        per = self.Ll // tm
        return lambda i: jnp.where(i < nctx, 0, 1 + (i - nctx) // per)


def in_proj(rw, x, nw, mod, w, par):
    R, D = x.shape
    N = w.shape[1]
    tm, tn = rw.tm, IN_TN
    mi = rw.mod_index(tm)
    return pl.pallas_call(
        functools.partial(_in_proj_kernel, nctx_blk=rw.NC // tm),
        grid=(R // tm, N // tn),
        in_specs=[
            pl.BlockSpec((tm, D), lambda i, j: (i, 0)),
            pl.BlockSpec((1, D), lambda i, j: (0, 0)),
            pl.BlockSpec((1, 8, D), lambda i, j: (mi(i), 0, 0)),
            pl.BlockSpec((D, tn), lambda i, j: (0, j)),
            pl.BlockSpec((8, tn), lambda i, j: (0, j)),
        ],
        out_specs=pl.BlockSpec((tm, tn), lambda i, j: (i, j)),
        out_shape=jax.ShapeDtypeStruct((R, N), F32),
        scratch_shapes=[pltpu.VMEM((tm, D), BF16)],
        compiler_params=_cp("arbitrary", "arbitrary"),
        name="in_proj",
    )(x, nw.reshape(1, D), mod, w, par)


def _in_proj_params(hy_conv_w, hy_conv_b, ssm_conv_w, ssm_conv_b, gdn_conv_w, ssm_dt_bias, gdn_dt_bias, gdn_A_log):
    def row(pieces):
        v = jnp.zeros((N_IN_PAD,), F32)
        for off, a in pieces:
            v = lax.dynamic_update_slice(v, a.astype(F32), (off,))
        return v
    z4 = jnp.zeros((4,), F32)
    conv = [row([(C_HY, hy_conv_w[t]), (C_XBC, ssm_conv_w[t]), (C_QKV, gdn_conv_w[t])]) for t in range(3)]
    bias = row([(C_HY, hy_conv_b), (C_XBC, ssm_conv_b)])
    l2s = row([(C_QKV, jnp.full((512,), GDN_DK ** -0.5, F32)), (C_QKV + 512, jnp.ones((512,), F32))])
    sbias = row([(C_SM, jnp.concatenate([ssm_dt_bias.reshape(16), gdn_dt_bias[0], z4, gdn_dt_bias[1], z4]))])
    salog = row([(C_SM + 16, jnp.concatenate([gdn_A_log[0], z4, gdn_A_log[1], z4]))])
    kind = np.full((N_IN_PAD,), 3.0, np.float32)
    kind[C_SM:C_SM + 16] = 0.0
    kind[C_SM + 16:C_SM + 20] = 1.0
    kind[C_SM + 24:C_SM + 28] = 1.0
    kind[C_SM + 20:C_SM + 24] = 2.0
    kind[C_SM + 28:C_SM + 32] = 2.0
    return jnp.stack(conv + [bias, l2s, sbias, salog, jnp.asarray(kind)], axis=0)


def _hy_filter_kernel(z_ref, w1_ref, b1_ref, w2_ref, b2_ref, w3_ref, f0_ref, f1_ref, win_ref, o_ref, h_ref):
    @pl.when(pl.program_id(1) == 0)
    def _():
        h1 = jnp.sin(f0_ref[...] * (_dot(z_ref[...], w1_ref[...], HI) + b1_ref[...]))
        h_ref[...] = jnp.sin(f1_ref[...] * (_dot(h1, w2_ref[...], HI) + b2_ref[...]))

    h = _dot(h_ref[...], w3_ref[...], HI) * win_ref[...]
    tl = h.shape[0]
    row = lax.broadcasted_iota(jnp.int32, (tl, 1), 0) + pl.program_id(0) * tl
    drop = (row == 0) & (pl.program_id(1) % 2 == 1)
    o_ref[...] = jnp.where(drop, 0.0, h).astype(o_ref.dtype)


def hy_filter(L, w1, b1, w2, b2, w3, freq):
    t = jnp.linspace(0.0, 1.0, L, dtype=F32)[:, None]
    w = 2.0 * math.pi * jnp.arange(L, dtype=F32)[:, None] / L
    f = jnp.linspace(1e-4, HY_BANDS - 1, HY_BANDS, dtype=F32)[None, :]
    z = jnp.concatenate([t, jnp.cos(f * w), -jnp.sin(f * w)], axis=-1)
    z = jnp.pad(z, ((0, 0), (0, 128 - HY_EMB)))
    min_decay = math.log(HY_TARGET) / HY_LONG_DECAY_PCT
    max_decay = math.log(HY_TARGET) / HY_SHORT_DECAY_PCT
    deltas = jnp.linspace(min_decay, max_decay, HY_WIDTH, dtype=F32)
    window = jnp.exp(-t * jnp.abs(deltas))
    H = HY_HIDDEN
    w1p = jnp.pad(w1, ((0, 128 - HY_EMB), (0, 128 - H)))
    w2p = jnp.pad(w2, ((0, 128 - H), (0, 128 - H)))
    w3p = jnp.pad(w3, ((0, 128 - H), (0, 0)))
    pad1 = lambda v: jnp.pad(v, (0, 128 - H)).reshape(1, 128)
    tl = 256
    full = lambda shape: pl.BlockSpec(shape, lambda i, j: (0, 0))
    return pl.pallas_call(
        _hy_filter_kernel,
        grid=(L // tl, 4),
        in_specs=[
            pl.BlockSpec((tl, 128), lambda i, j: (i, 0)),
            full((128, 128)), full((1, 128)), full((128, 128)), full((1, 128)),
            pl.BlockSpec((128, HY_WIDTH), lambda i, j: (0, j)),
            full((1, 128)), full((1, 128)),
            pl.BlockSpec((tl, HY_WIDTH), lambda i, j: (i, 0)),
        ],
        out_specs=pl.BlockSpec((tl, HY_WIDTH), lambda i, j: (i, j)),
        out_shape=jax.ShapeDtypeStruct((L, 4 * HY_WIDTH), BF16),
        scratch_shapes=[pltpu.VMEM((tl, 128), F32)],
        compiler_params=_cp("arbitrary", "arbitrary"),
        name="hy_filter",
    )(z, w1p, pad1(b1), w2p, pad1(b2), w3p, pad1(freq[0]), pad1(freq[1]), window)


def dft_tables(L):
    N = 2 * L
    f = jnp.arange(L, dtype=jnp.int32)[:, None]
    s = jnp.arange(L, dtype=jnp.int32)[None, :]
    ang = ((f * s) % N).astype(F32) * (2.0 * math.pi / N)
    c, sn = jnp.cos(ang), jnp.sin(ang)
    alt = (1 - 2 * (s % 2)).astype(F32)
    first = f == 0
    fwd = jnp.concatenate([c, jnp.where(first, alt, -sn)], axis=0)
    wgt = jnp.where(first, 1.0, 2.0) / N
    inv = jnp.concatenate([(c * wgt).T, jnp.where(first, alt / N, -sn * wgt).T], axis=1)
    return fwd.astype(BF16), inv.astype(BF16)


def _matmul_kernel(a_ref, b_ref, o_ref):
    o_ref[...] = _dot(a_ref[...], b_ref[...])


def matmul(a, b, tm, tn):
    M, K = a.shape
    N = b.shape[1]
    return pl.pallas_call(
        _matmul_kernel,
        grid=(M // tm, N // tn),
        in_specs=[pl.BlockSpec((tm, K), lambda i, j: (i, 0)), pl.BlockSpec((K, tn), lambda i, j: (0, j))],
        out_specs=pl.BlockSpec((tm, tn), lambda i, j: (i, j)),
        out_shape=jax.ShapeDtypeStruct((M, N), F32),
        compiler_params=_cp("arbitrary", "arbitrary"),
        name="matmul",
    )(a, b)


def _long_conv_kernel(u_ref, g_ref, bias_ref, fr_ref, fi_ref, ic_ref, is_ref,
                      ar0_ref, ar1_ref, ai0_ref, ai1_ref, o_ref, ub_ref, acc_ref):
    f = pl.program_id(1)

    @pl.when(f == 0)
    def _():
        ub_ref[...] = u_ref[...].astype(BF16)
        acc_ref[...] = jnp.zeros_like(acc_ref)

    ub = ub_ref[...]
    ur = _dot(fr_ref[...], ub)
    ui = _dot(fi_ref[...], ub)
    kr = ar0_ref[...] + ar1_ref[...]
    nyq = (lax.broadcasted_iota(jnp.int32, (FREQ_BLK, 1), 0) == 0) & (f == 0)
    ki = jnp.where(nyq, ai0_ref[...] + ai1_ref[...], ai0_ref[...] - ai1_ref[...])
    pr = jnp.where(nyq, ur * kr, ur * kr - ui * ki)
    pi = jnp.where(nyq, ui * ki, ur * ki + ui * kr)
    acc_ref[...] += _dot(ic_ref[...], pr.astype(BF16)) + _dot(is_ref[...], pi.astype(BF16))

    @pl.when(f == pl.num_programs(1) - 1)
    def _():
        u = u_ref[...]
        o_ref[...] = (g_ref[...] * (acc_ref[...] + u * bias_ref[...])).astype(o_ref.dtype)


def long_conv(B, L, u, u_rb0, u_cb, gate, g_rb0, gate_cb, bias, fwd, inv, kspec, order, out_dtype):
    C = HY_WIDTH
    nfb = L // FREQ_BLK
    FB = FREQ_BLK
    return pl.pallas_call(
        _long_conv_kernel,
        grid=(B, nfb),
        in_specs=[
            pl.BlockSpec((L, C), lambda b, f: (u_rb0 + b, u_cb)),
            pl.BlockSpec((L, C), lambda b, f: (g_rb0 + b, gate_cb)),
            pl.BlockSpec((1, C), lambda b, f: (0, 0)),
            pl.BlockSpec((FB, L), lambda b, f: (f, 0)),
            pl.BlockSpec((FB, L), lambda b, f: (nfb + f, 0)),
            pl.BlockSpec((L, FB), lambda b, f: (0, f)),
            pl.BlockSpec((L, FB), lambda b, f: (0, nfb + f)),
            pl.BlockSpec((FB, C), lambda b, f: (f, 2 * order)),
            pl.BlockSpec((FB, C), lambda b, f: (f, 2 * order + 1)),
            pl.BlockSpec((FB, C), lambda b, f: (nfb + f, 2 * order)),
            pl.BlockSpec((FB, C), lambda b, f: (nfb + f, 2 * order + 1)),
        ],
        out_specs=pl.BlockSpec((L, C), lambda b, f: (b, 0)),
        out_shape=jax.ShapeDtypeStruct((B * L, C), out_dtype),
        scratch_shapes=[pltpu.VMEM((L, C), BF16), pltpu.VMEM((L, C), F32)],
        compiler_params=_cp("arbitrary", "arbitrary"),
        name="long_conv",
    )(u, gate, bias.reshape(1, C), fwd, fwd, inv, inv, kspec, kspec, kspec, kspec)


def _scan_blocks(rw, rows):
    nbc, nbl, base = rw.Lc // rows, rw.Ll // rows, rw.NC // rows

    def make(d):
        def f(b, s):
            jc = s if d == 0 else nbc - 1 - s
            jl = (s - nbc) if d == 0 else nbl - 1 - (s - nbc)
            return jnp.where(s < nbc, b * nbc + jc, base + b * nbl + jl)
        return f

    return [make(0), make(1)], nbc + nbl


def _expand_lanes(x, base, n, width):
    rows = x.shape[0]
    per = 128 // width
    lane = lax.broadcasted_iota(jnp.int32, (rows, 128), 1)
    tiles = []
    for t in range(n // per):
        c0 = base + t * per
        tile = jnp.broadcast_to(x[:, c0:c0 + 1], (rows, 128))
        for i in range(1, per):
            tile = jnp.where(lane >= i * width, jnp.broadcast_to(x[:, c0 + i:c0 + i + 1], (rows, 128)), tile)
        tiles.append(tile)
    return jnp.concatenate(tiles, axis=1)


def _ssd_kernel(xf, bf, cf, smf, dtf, xb, bb, cb_, smb, dtb, alx_ref, alc_ref, of_ref, ob_ref, h_ref):
    Q = SSM_CHUNK
    GW = SSM_GW

    @pl.when(pl.program_id(1) == 0)
    def _():
        h_ref[...] = jnp.zeros_like(h_ref)

    row = lax.broadcasted_iota(jnp.int32, (Q, Q), 0)
    col = lax.broadcasted_iota(jnp.int32, (Q, Q), 1)
    lane_head = lax.broadcasted_iota(jnp.int32, (Q, GW), 1) // SSM_HEAD_DIM
    dirs = ((xf, bf, cf, smf, dtf, of_ref), (xb, bb, cb_, smb, dtb, ob_ref))
    jobs = []
    for d in range(2):
        x_ref, b_ref, c_ref, sm_ref, dt_ref, o_ref = dirs[d]
        keep = (col <= row) if d == 0 else (col >= row)
        tri = keep.astype(BF16)
        tri_t = ((row <= col) if d == 0 else (row >= col)).astype(BF16)
        sm = sm_ref[...]
        a_x = -jnp.exp(alx_ref[d])
        dtx = _expand_lanes(sm, 8 * d, SSM_HEADS, SSM_HEAD_DIM)
        cumx = _expand_lanes(_dot_01_lhs(tri, sm), 8 * d, SSM_HEADS, SSM_HEAD_DIM) * a_x
        cumr = _dot_01_rhs(dt_ref[0], tri_t) * (-jnp.exp(alc_ref[d]))
        last = Q - 1 if d == 0 else 0
        totx = cumx[last:last + 1, :]
        xd = x_ref[...] * dtx
        xdw = xd * jnp.exp(totx - cumx)
        ecum = jnp.exp(cumx)
        for g in range(SSM_GROUPS):
            gs = slice(g * GW, (g + 1) * GW)
            jobs.append(dict(d=d, g=g, gs=gs, keep=keep, cumx=cumx, cumr=cumr, o_ref=o_ref,
                             bg=b_ref[:, g * SSM_STATE:(g + 1) * SSM_STATE].astype(BF16),
                             cg=c_ref[:, g * SSM_STATE:(g + 1) * SSM_STATE].astype(BF16),
                             xdg=xd[:, gs], xdw=xdw[:, gs].astype(BF16), ecum=ecum[:, gs],
                             etot=jnp.exp(totx[:, gs])))
    for j in jobs:
        j["cb"] = _dot_nt(j["cg"], j["bg"])
        j["h"] = h_ref[j["d"], j["g"]]
    for j in jobs:
        ms, xs = [], []
        for e4 in range(SSM_HPG):
            e = j["g"] * SSM_HPG + e4
            diff = j["cumx"][:, e * SSM_HEAD_DIM:e * SSM_HEAD_DIM + 1] - j["cumr"][e:e + 1, :]
            ms.append((j["cb"] * jnp.where(j["keep"], jnp.exp(diff), 0.0)).astype(BF16))
            xs.append(jnp.where(lane_head == e4, j["xdg"], 0.0).astype(BF16))
        yd = _dot(jnp.concatenate(ms, axis=1), jnp.concatenate(xs, axis=0))
        y_off = _dot(j["cg"], j["h"].astype(BF16)) * j["ecum"]
        j["o_ref"][:, j["gs"]] = yd + y_off
    for j in jobs:
        h_ref[j["d"], j["g"]] = j["h"] * j["etot"] + _dot_tn(j["bg"], j["xdw"])


def ssd_scan(rw, p, dtT, alx, alc):
    Q = SSM_CHUNK
    blks, nsteps = _scan_blocks(rw, Q)
    R = p.shape[0]
    in_specs = []
    for d in range(2):
        f = blks[d]
        in_specs += [
            pl.BlockSpec((Q, 512), lambda b, s, f=f: (f(b, s), C_XBC // 512)),
            pl.BlockSpec((Q, 256), lambda b, s, f=f: (f(b, s), C_XBC // 256 + 2)),
            pl.BlockSpec((Q, 256), lambda b, s, f=f: (f(b, s), C_XBC // 256 + 3)),
            pl.BlockSpec((Q, 128), lambda b, s, f=f: (f(b, s), C_SM // 128)),
            pl.BlockSpec((1, 8, Q), lambda b, s, f=f, d=d: (d, 0, f(b, s))),
        ]
    in_specs += [pl.BlockSpec((2, 1, 512), lambda b, s: (0, 0, 0)), pl.BlockSpec((2, 8, 1), lambda b, s: (0, 0, 0))]
    ops = (p, p, p, p, dtT)
    return pl.pallas_call(
        _ssd_kernel,
        grid=(rw.B, nsteps),
        in_specs=in_specs,
        out_specs=[pl.BlockSpec((Q, 512), lambda b, s, f=blks[d]: (f(b, s), 0)) for d in range(2)],
        out_shape=[jax.ShapeDtypeStruct((R, 512), F32)] * 2,
        scratch_shapes=[pltpu.VMEM((2, SSM_GROUPS, SSM_STATE, SSM_GW), F32)],
        compiler_params=_cp("arbitrary", "arbitrary"),
        name="ssd_scan",
    )(*ops, *ops, alx, alc)


def _ssm_post_kernel(yf_ref, yb_ref, x_ref, z_ref, dx_ref, nw_ref, o_ref):
    y = yf_ref[...] + yb_ref[...] + x_ref[...] * dx_ref[...]
    y = y * _silu(z_ref[...])
    parts = []
    for g in range(SSM_GROUPS):
        yg = y[:, g * SSM_GW:(g + 1) * SSM_GW]
        ms = jnp.mean(yg * yg, axis=-1, keepdims=True)
        parts.append(yg * lax.rsqrt(ms + EPS))
    o_ref[...] = (jnp.concatenate(parts, axis=1) * nw_ref[...]).astype(o_ref.dtype)


def ssm_post(rw, y_f, y_b, p, dx, nw):
    R = p.shape[0]
    tm = rw.tm
    vec = pl.BlockSpec((1, 512), lambda i: (0, 0))
    return pl.pallas_call(
        _ssm_post_kernel,
        grid=(R // tm,),
        in_specs=[
            pl.BlockSpec((tm, 512), lambda i: (i, 0)),
            pl.BlockSpec((tm, 512), lambda i: (i, 0)),
            pl.BlockSpec((tm, 512), lambda i: (i, C_XBC // 512)),
            pl.BlockSpec((tm, 512), lambda i: (i, C_Z // 512)),
            vec, vec,
        ],
        out_specs=pl.BlockSpec((tm, 512), lambda i: (i, 0)),
        out_shape=jax.ShapeDtypeStruct((R, 512), BF16),
        compiler_params=_cp("arbitrary"),
        name="ssm_post",
    )(y_f, y_b, p, p, dx, nw)


def _split3(x):
    x1 = x.astype(BF16)
    r = x - x1.astype(F32)
    x2 = r.astype(BF16)
    x3 = (r - x2.astype(F32)).astype(BF16)
    return x1, x2, x3


def _dot_01_lhs(m01, x):
    x1, x2, x3 = _split3(x)
    return _dot(m01, x1) + _dot(m01, x2) + _dot(m01, x3)


def _dot_01_rhs(x, m01):
    x1, x2, x3 = _split3(x)
    return _dot(x1, m01) + _dot(x2, m01) + _dot(x3, m01)


GDN_ROWS = 256


def _gdn_prep_kernel(q_ref, k_ref, v_ref, sm_ref, gT_ref, u_ref, w_ref, qg_ref, kd_ref, qk_ref, egl_ref):
    C = GDN_CHUNK
    row = lax.broadcasted_iota(jnp.int32, (C, C), 0)
    col = lax.broadcasted_iota(jnp.int32, (C, C), 1)
    jobs = []
    levels = []
    for d in range(2):
        keep = (col <= row) if d == 0 else (col >= row)
        late, early = (row, col) if d == 0 else (col, row)
        levels.append([(((row ^ col) >> (j + 1)) == 0) & ((late & (1 << j)) != 0) & ((early & (1 << j)) == 0)
                       for j in range(6)])
        tri = keep.astype(BF16)
        tri_t = ((row <= col) if d == 0 else (row >= col)).astype(BF16)
        last = C - 1 if d == 0 else 0
        for c in range(GDN_ROWS // C):
            rows = slice(c * C, (c + 1) * C)
            smc = sm_ref[rows, :]
            cums = _dot_01_lhs(tri, smc)
            cumr = _dot_01_rhs(gT_ref[c, 8 * d:8 * d + 8, :], tri_t)
            tot = cums[last:last + 1, :]
            for h in range(GDN_HEADS):
                lg = 16 + 8 * d + h
                jobs.append(dict(d=d, c=c, h=h, rows=rows, hs=slice(h * 128, (h + 1) * 128), keep=keep,
                                 gc=cums[:, lg:lg + 1], beta=smc[:, lg + 4:lg + 5],
                                 gl=tot[:, lg:lg + 1], gr=cumr[h:h + 1, :]))
    for j in jobs:
        q = q_ref[j["rows"], j["hs"]]
        k = k_ref[j["rows"], j["hs"]]
        j["dec"] = jnp.where(j["keep"], jnp.exp(j["gc"] - j["gr"]), 0.0)
        kb = k * j["beta"]
        both = _dot_nt(jnp.concatenate([kb, q], axis=0).astype(BF16), k.astype(BF16))
        j["a"] = both[:C] * j["dec"]
        j["n"] = -jnp.where(levels[j["d"]][0], j["a"], 0.0)
        qk_ref[j["d"], j["c"], j["h"]] = (both[C:] * j["dec"]).astype(BF16)
    for lev in range(1, 6):
        for j in jobs:
            l = jnp.where(levels[j["d"]][lev], j["a"], 0.0)
            j["y"] = l + _dot(l.astype(BF16), j["n"].astype(BF16))
        for j in jobs:
            j["n"] = j["n"] - j["y"] - _dot(j["n"].astype(BF16), j["y"].astype(BF16))
    for j in jobs:
        d, rows, hs, gc, gl, beta = j["d"], j["rows"], j["hs"], j["gc"], j["gl"], j["beta"]
        q = q_ref[rows, hs]
        k = k_ref[rows, hs]
        eg = jnp.exp(gc)
        rhs = jnp.concatenate([v_ref[rows, hs] * beta, k * beta * eg], axis=1)
        sol = rhs + _dot(j["n"].astype(BF16), rhs.astype(BF16))
        u_ref[d, rows, hs] = sol[:, :GDN_DV]
        w_ref[d, rows, hs] = sol[:, GDN_DV:].astype(BF16)
        qg_ref[d, rows, hs] = (q * eg).astype(BF16)
        kd_ref[d, rows, hs] = (k * jnp.exp(gl - gc)).astype(BF16)
        egl_ref[d, j["c"], :, hs] = jnp.broadcast_to(jnp.exp(gl), (8, 128))


def gdn_prep(p, gT):
    R = p.shape[0]
    T, C = GDN_ROWS, GDN_CHUNK
    nc = T // C
    col = lambda k: pl.BlockSpec((T, 512), lambda i: (i, C_QKV // 512 + k))
    dirrow = pl.BlockSpec((2, T, 512), lambda i: (0, i, 0))
    return pl.pallas_call(
        _gdn_prep_kernel,
        grid=(R // T,),
        in_specs=[col(0), col(1), col(2),
                  pl.BlockSpec((T, 128), lambda i: (i, C_SM // 128)),
                  pl.BlockSpec((nc, 16, C), lambda i: (i, 0, 0))],
        out_specs=[dirrow, dirrow, dirrow, dirrow,
                   pl.BlockSpec((2, nc, GDN_HEADS, C, C), lambda i: (0, i, 0, 0, 0)),
                   pl.BlockSpec((2, nc, 8, 512), lambda i: (0, i, 0, 0))],
        out_shape=[jax.ShapeDtypeStruct((2, R, 512), F32),
                   jax.ShapeDtypeStruct((2, R, 512), BF16),
                   jax.ShapeDtypeStruct((2, R, 512), BF16),
                   jax.ShapeDtypeStruct((2, R, 512), BF16),
                   jax.ShapeDtypeStruct((2, R // C, GDN_HEADS, C, C), BF16),
                   jax.ShapeDtypeStruct((2, R // C, 8, 512), F32)],
        compiler_params=_cp("arbitrary"),
        name="gdn_prep",
    )(p, p, p, p, gT)


def _gdn_scan_kernel(uf, wf, qgf, kdf, qkf, eglf, ub, wb, qgb, kdb, qkb, eglb, of_ref, ob_ref, s_ref):
    C = GDN_CHUNK
    nch = GDN_ROWS // C

    @pl.when(pl.program_id(1) == 0)
    def _():
        s_ref[...] = jnp.zeros_like(s_ref)

    dirs = ((uf, wf, qgf, kdf, qkf, eglf, of_ref), (ub, wb, qgb, kdb, qkb, eglb, ob_ref))
    chains = [(d, h) for d in range(2) for h in range(GDN_HEADS)]
    S = {ch: s_ref[ch[0], ch[1]] for ch in chains}
    for i in range(nch):
        Sb, vnb, rows_of, c_of = {}, {}, {}, {}
        for d, h in chains:
            c_of[d] = i if d == 0 else nch - 1 - i
            rows_of[d] = slice(c_of[d] * C, (c_of[d] + 1) * C)
        for d, h in chains:
            hs = slice(h * 128, (h + 1) * 128)
            Sb[d, h] = S[d, h].astype(BF16)
            v_new = dirs[d][0][0, rows_of[d], hs] - _dot(dirs[d][1][0, rows_of[d], hs], Sb[d, h])
            vnb[d, h] = v_new.astype(BF16)
        for d, h in chains:
            hs = slice(h * 128, (h + 1) * 128)
            u_ref, w_ref, qg_ref, kd_ref, qk_ref, egl_ref, o_ref = dirs[d]
            S[d, h] = S[d, h] * egl_ref[0, c_of[d], 0:1, hs] + _dot_tn(kd_ref[0, rows_of[d], hs], vnb[d, h])
        for d, h in chains:
            hs = slice(h * 128, (h + 1) * 128)
            u_ref, w_ref, qg_ref, kd_ref, qk_ref, egl_ref, o_ref = dirs[d]
            o_ref[rows_of[d], hs] = (_dot(qg_ref[0, rows_of[d], hs], Sb[d, h])
                                     + _dot(qk_ref[0, c_of[d], h], vnb[d, h]))
    for ch in chains:
        s_ref[ch[0], ch[1]] = S[ch]


def gdn_scan(rw, u, w, qg, kd, qk, egl):
    T, C = GDN_ROWS, GDN_CHUNK
    nc = T // C
    R = u.shape[1]
    nbc, nbl, base = rw.Lc // T, rw.Ll // T, rw.NC // T

    def blk(d):
        def f(b, s):
            jc = s if d == 0 else nbc - 1 - s
            jl = (s - nbc) if d == 0 else nbl - 1 - (s - nbc)
            return jnp.where(s < nbc, b * nbc + jc, base + b * nbl + jl)
        return f

    in_specs = []
    for d in range(2):
        f = blk(d)
        rowspec = pl.BlockSpec((1, T, 512), lambda b, s, f=f, d=d: (d, f(b, s), 0))
        in_specs += [rowspec, rowspec, rowspec, rowspec,
                     pl.BlockSpec((1, nc, GDN_HEADS, C, C), lambda b, s, f=f, d=d: (d, f(b, s), 0, 0, 0)),
                     pl.BlockSpec((1, nc, 8, 512), lambda b, s, f=f, d=d: (d, f(b, s), 0, 0))]
    out_specs = [pl.BlockSpec((T, 512), lambda b, s, f=blk(d): (f(b, s), 0)) for d in range(2)]
    ops = (u, w, qg, kd, qk, egl)
    return pl.pallas_call(
        _gdn_scan_kernel,
        grid=(rw.B, nbc + nbl),
        in_specs=in_specs,
        out_specs=out_specs,
        out_shape=[jax.ShapeDtypeStruct((R, 512), F32)] * 2,
        scratch_shapes=[pltpu.VMEM((2, GDN_HEADS, GDN_DK, GDN_DV), F32)],
        compiler_params=_cp("arbitrary", "arbitrary"),
        name="gdn_scan",
    )(*ops, *ops)


def _gdn_post_kernel(of_ref, ob_ref, g_ref, nw_ref, o_ref):
    o = of_ref[...] + ob_ref[...]
    parts = []
    for h in range(GDN_HEADS):
        oh = o[:, h * 128:(h + 1) * 128]
        ms = jnp.mean(oh * oh, axis=-1, keepdims=True)
        parts.append(oh * lax.rsqrt(ms + EPS))
    o_ref[...] = (jnp.concatenate(parts, axis=1) * nw_ref[...] * _silu(g_ref[...])).astype(o_ref.dtype)


def gdn_post(rw, o_f, o_b, p, nw):
    R = p.shape[0]
    tm = rw.tm
    return pl.pallas_call(
        _gdn_post_kernel,
        grid=(R // tm,),
        in_specs=[
            pl.BlockSpec((tm, 512), lambda i: (i, 0)),
            pl.BlockSpec((tm, 512), lambda i: (i, 0)),
            pl.BlockSpec((tm, 512), lambda i: (i, C_GG // 512)),
            pl.BlockSpec((1, 512), lambda i: (0, 0)),
        ],
        out_specs=pl.BlockSpec((tm, 512), lambda i: (i, 0)),
        out_shape=jax.ShapeDtypeStruct((R, 512), BF16),
        compiler_params=_cp("arbitrary"),
        name="gdn_post",
    )(o_f, o_b, p, nw)


def _merge_kernel(yh_ref, ys_ref, yg_ref, g0_ref, g1_ref, g2_ref, w0_ref, w1_ref, w2_ref, wo_ref,
                  x_ref, mod_ref, o_ref):
    m = (_sigmoid(g0_ref[...]) * _dot(yh_ref[...], w0_ref[...])
         + _sigmoid(g1_ref[...]) * _dot(ys_ref[...], w1_ref[...])
         + _sigmoid(g2_ref[...]) * _dot(yg_ref[...], w2_ref[...]))
    o_ref[...] = x_ref[...] + mod_ref[0, 2:3, :] * _dot(m.astype(BF16), wo_ref[...])


def merge(rw, yh, ys, yg, p, w0, w1, w2, wo, x, mod):
    R, D = x.shape
    tm = min(rw.tm, 512)
    mi = rw.mod_index(tm)
    yspec = pl.BlockSpec((tm, 512), lambda i: (i, 0))
    gspec = lambda k: pl.BlockSpec((tm, D), lambda i: (i, C_GATE // D + k))
    wspec = pl.BlockSpec((512, D), lambda i: (0, 0))
    return pl.pallas_call(
        _merge_kernel,
        grid=(R // tm,),
        in_specs=[yspec, yspec, yspec, gspec(0), gspec(1), gspec(2), wspec, wspec, wspec,
                  pl.BlockSpec((D, D), lambda i: (0, 0)),
                  pl.BlockSpec((tm, D), lambda i: (i, 0)),
                  pl.BlockSpec((1, 8, D), lambda i: (mi(i), 0, 0))],
        out_specs=pl.BlockSpec((tm, D), lambda i: (i, 0)),
        out_shape=jax.ShapeDtypeStruct((R, D), F32),
        compiler_params=_cp("arbitrary"),
        name="merge",
    )(yh, ys, yg, p, p, p, w0, w1, w2, wo, x, mod)


def _swiglu_up_kernel(x_ref, nw_ref, mod_ref, wg_ref, wu_ref, o_ref, h_ref):
    @pl.when(pl.program_id(1) == 0)
    def _():
        h = _norm_mod(x_ref[...], nw_ref[...], mod_ref[0, 4:5, :], mod_ref[0, 3:4, :])
        h_ref[...] = h.astype(BF16)

    h = h_ref[...]
    g = _dot(h, wg_ref[...])
    u = _dot(h, wu_ref[...])
    o_ref[...] = (_silu(g) * u).astype(o_ref.dtype)


def swiglu_up(rw, x, nw, mod, wgu):
    R, D = x.shape
    tm = rw.tm
    tn = D_FF // 2
    nj = D_FF // tn
    mi = rw.mod_index(tm)
    return pl.pallas_call(
        _swiglu_up_kernel,
        grid=(R // tm, nj),
        in_specs=[
            pl.BlockSpec((tm, D), lambda i, j: (i, 0)),
            pl.BlockSpec((1, D), lambda i, j: (0, 0)),
            pl.BlockSpec((1, 8, D), lambda i, j: (mi(i), 0, 0)),
            pl.BlockSpec((D, tn), lambda i, j: (0, j)),
            pl.BlockSpec((D, tn), lambda i, j: (0, nj + j)),
        ],
        out_specs=pl.BlockSpec((tm, tn), lambda i, j: (i, j)),
        out_shape=jax.ShapeDtypeStruct((R, D_FF), BF16),
        scratch_shapes=[pltpu.VMEM((tm, D), BF16)],
        compiler_params=_cp("arbitrary", "arbitrary"),
        name="swiglu_up",
    )(x, nw.reshape(1, D), mod, wgu, wgu)


def _swiglu_down_kernel(a_ref, w_ref, x_ref, mod_ref, o_ref):
    o_ref[...] = x_ref[...] + mod_ref[0, 5:6, :] * _dot(a_ref[...], w_ref[...])


def swiglu_down(rw, a, w, x, mod):
    R, D = x.shape
    tm = min(rw.tm, 512)
    mi = rw.mod_index(tm)
    return pl.pallas_call(
        _swiglu_down_kernel,
        grid=(R // tm,),
        in_specs=[
            pl.BlockSpec((tm, D_FF), lambda i: (i, 0)),
            pl.BlockSpec((D_FF, D), lambda i: (0, 0)),
            pl.BlockSpec((tm, D), lambda i: (i, 0)),
            pl.BlockSpec((1, 8, D), lambda i: (mi(i), 0, 0)),
        ],
        out_specs=pl.BlockSpec((tm, D), lambda i: (i, 0)),
        out_shape=jax.ShapeDtypeStruct((R, D), F32),
        compiler_params=_cp("arbitrary"),
        name="swiglu_down",
    )(a, w, x, mod)


def _final_norm_kernel(x_ref, w_ref, o_ref):
    x = x_ref[...]
    ms = jnp.mean(x * x, axis=-1, keepdims=True)
    o_ref[...] = x * lax.rsqrt(ms + EPS) * w_ref[...]


def final_norm(rw, x, w):
    D = x.shape[1]
    tm = rw.tm
    n0 = rw.NC // tm
    nl = rw.B * rw.Ll
    return pl.pallas_call(
        _final_norm_kernel,
        grid=(nl // tm,),
        in_specs=[pl.BlockSpec((tm, D), lambda i: (n0 + i, 0)), pl.BlockSpec((1, D), lambda i: (0, 0))],
        out_specs=pl.BlockSpec((tm, D), lambda i: (i, 0)),
        out_shape=jax.ShapeDtypeStruct((nl, D), F32),
        compiler_params=_cp("arbitrary"),
        name="final_norm",
    )(x, w.reshape(1, D))


def _regroup_w_in(w_in):
    o_dt = 3072
    o_gdn = 3088
    o_a = o_gdn + 2048
    o_b = o_a + 8
    o_gate = o_gdn + 2064
    pieces = [
        w_in[..., 0:3072],
        w_in[..., o_gdn:o_gdn + 2048],
        w_in[..., o_gate:o_gate + 3072],
        w_in[..., o_dt:o_dt + 16],
        w_in[..., o_a:o_a + 4], w_in[..., o_b:o_b + 4],
        w_in[..., o_a + 4:o_a + 8], w_in[..., o_b + 4:o_b + 8],
        jnp.zeros(w_in.shape[:-1] + (N_IN_PAD - C_SM - 32,), w_in.dtype),
    ]
    return jnp.concatenate(pieces, axis=-1).astype(BF16)


def kernel(x, c, ctx, c_ctx, w_ada, b_ada, norm1_w, norm2_w, w_in, hy_conv_w, hy_conv_b, hy_w1, hy_b1, hy_w2, hy_b2, hy_w3, hy_freq, hy_bias, ssm_conv_w, ssm_conv_b, ssm_dt_bias, ssm_A_log, ssm_D, ssm_norm_w, gdn_conv_w, gdn_dt_bias, gdn_A_log, gdn_norm_w, w_hy_out, w_ssm_out, w_gdn_out, w_out, w_gate_up, w_down, final_norm_w):
    B, Ll, D = x.shape
    Lc = ctx.shape[1]
    depth = w_ada.shape[0]
    assert Lc == CONV_ROWS and D == D_MODEL and B <= 15
    rw = Rows(B, Lc, Ll)
    R, NC = rw.R, rw.NC

    xa = jnp.concatenate([ctx.reshape(B * Lc, D), x.reshape(B * Ll, D)], axis=0)

    svec = jnp.concatenate([c_ctx[None, :], c, jnp.zeros((15 - B, D), F32)], axis=0)
    mod_all = ada_modulation(svec, w_ada, b_ada)
    mod_all = jnp.pad(mod_all.reshape(depth, 16, 6, D), ((0, 0), (0, 0), (0, 2), (0, 0)))

    w_in_r = _regroup_w_in(w_in)
    fwd_l, inv_l = dft_tables(Ll)
    fwd_c, inv_c = dft_tables(Lc)

    for l in range(depth):
        mod = mod_all[l]
        par = _in_proj_params(hy_conv_w[l], hy_conv_b[l], ssm_conv_w[l], ssm_conv_b[l], gdn_conv_w[l],
                              ssm_dt_bias[l], gdn_dt_bias[l], gdn_A_log[l])
        p = in_proj(rw, xa, norm1_w[l], mod, w_in_r[l], par)

        sm32_t = p[:, C_SM:C_SM + 32].T
        dt_t = sm32_t[:16].reshape(2, 8, R)
        g_t = sm32_t[16:32].reshape(16, R // GDN_CHUNK, GDN_CHUNK).transpose(1, 0, 2)

        alx = jnp.repeat(ssm_A_log[l], SSM_HEAD_DIM, axis=-1).reshape(2, 1, 512)
        alc = ssm_A_log[l].reshape(2, 8, 1)
        y_f, y_b = ssd_scan(rw, p, dt_t, alx, alc)
        dx = jnp.repeat(ssm_D[l], SSM_HEAD_DIM).reshape(1, 512)
        y_ssm = ssm_post(rw, y_f, y_b, p, dx, ssm_norm_w[l].reshape(1, 512))

        o_f, o_b = gdn_scan(rw, *gdn_prep(p, g_t))
        y_gdn = gdn_post(rw, o_f, o_b, p, jnp.tile(gdn_norm_w[l], GDN_HEADS).reshape(1, 512))

        hyu = p
        parts = []
        for (Bn, L, blk0, fwd, inv) in ((B, Lc, 0, fwd_c, inv_c), (B, Ll, NC // Ll, fwd_l, inv_l)):
            if NC % L:
                raise ValueError("latent length must divide the context row count")
            filt = hy_filter(L, hy_w1[l], hy_b1[l], hy_w2[l], hy_b2[l], hy_w3[l], hy_freq[l])
            kspec = matmul(fwd, filt, min(512, 2 * L), 512)
            z1 = long_conv(Bn, L, hyu, blk0, 0, hyu, blk0, 1, hy_bias[l, 0], fwd, inv, kspec, 0, F32)
            yy = long_conv(Bn, L, z1, 0, 0, hyu, blk0, 2, hy_bias[l, 1], fwd, inv, kspec, 1, BF16)
            parts.append(yy)
        y_hy = jnp.concatenate(parts, axis=0)

        xa = merge(rw, y_hy, y_ssm, y_gdn, p, w_hy_out[l].astype(BF16), w_ssm_out[l].astype(BF16),
                   w_gdn_out[l].astype(BF16), w_out[l].astype(BF16), xa, mod)
        act = swiglu_up(rw, xa, norm2_w[l], mod, w_gate_up[l].astype(BF16))
        xa = swiglu_down(rw, act, w_down[l].astype(BF16), xa, mod)

    out = final_norm(rw, xa, final_norm_w)
    return out.reshape(B, Ll, D)
```

```python
import functools
import math

import jax
import jax.numpy as jnp
import numpy as np
from jax import lax
from jax.experimental import pallas as pl
from jax.experimental.pallas import tpu as pltpu

F32 = jnp.float32
BF16 = jnp.bfloat16
HI = lax.Precision.HIGHEST

EPS = 1e-6
D_MODEL = 1024
GRID_W = 64

HY_WIDTH = 512
HY_BANDS = 16
HY_EMB = 1 + 2 * HY_BANDS
HY_HIDDEN = 64
HY_SHORT_DECAY_PCT = 0.3
HY_LONG_DECAY_PCT = 1.5
HY_TARGET = 1e-2

SSM_HEADS = 8
SSM_HEAD_DIM = 64
SSM_WIDTH = 512
SSM_GROUPS = 2
SSM_HPG = 4
SSM_STATE = 128
SSM_CHUNK = 128
SSM_GW = SSM_HPG * SSM_HEAD_DIM

GDN_HEADS = 4
GDN_DK = 128
GDN_DV = 128
GDN_CHUNK = 64

D_FF = 2816

C_HY = 0
C_Z = 1536
C_XBC = 2048
C_QKV = 3072
C_GG = 4608
C_GATE = 5120
C_SM = 8192

CONV_ROWS = 256
FREQ_BLK = 256

VMEM_LIMIT = 56 * 1024 * 1024


def _cp(*sem):
    return pltpu.CompilerParams(dimension_semantics=sem, vmem_limit_bytes=VMEM_LIMIT)


def _sigmoid(x):
    return 1.0 / (1.0 + jnp.exp(-x))


def _silu(x):
    return x * _sigmoid(x)


def _softplus(x):
    return jnp.maximum(x, 0.0) + jnp.log1p(jnp.exp(-jnp.abs(x)))


def _dot(a, b, precision=None):
    return jnp.dot(a, b, precision=precision, preferred_element_type=F32)


def _dot_nt(a, b):
    return lax.dot_general(a, b, (((1,), (1,)), ((), ())), preferred_element_type=F32)


def _dot_tn(a, b):
    return lax.dot_general(a, b, (((0,), (0,)), ((), ())), preferred_element_type=F32)


def _ada_kernel(s_ref, w_ref, b_ref, o_ref):
    s = _silu(s_ref[...])
    o_ref[0] = _dot(s, w_ref[0], HI) + b_ref[0]


def ada_modulation(svec, w_ada, b_ada):
    depth = w_ada.shape[0]
    D = D_MODEL
    return pl.pallas_call(
        _ada_kernel,
        grid=(depth, 6),
        in_specs=[
            pl.BlockSpec((16, D), lambda l, j: (0, 0)),
            pl.BlockSpec((1, D, D), lambda l, j: (l, 0, j)),
            pl.BlockSpec((1, 1, D), lambda l, j: (l, 0, j)),
        ],
        out_specs=pl.BlockSpec((1, 16, D), lambda l, j: (l, 0, j)),
        out_shape=jax.ShapeDtypeStruct((depth, 16, 6 * D), F32),
        compiler_params=_cp("arbitrary", "arbitrary"),
        name="ada",
    )(svec, w_ada, b_ada.reshape(depth, 1, 6 * D))


def _norm_mod(x, nw, scale, shift):
    ms = jnp.mean(x * x, axis=-1, keepdims=True)
    return (x * lax.rsqrt(ms + EPS) * nw) * (1.0 + scale) + shift


IN_TN = 512
IN_MODE_CONV = (0, 1, 2)
IN_MODE_CONV_SILU = (4, 5, 8)
IN_MODE_CONV_SILU_L2 = (6, 7)
IN_MODE_SMALL = (16,)
IN_MODE_RAW = (3, 9, 10, 11, 12, 13, 14, 15)
N_IN_PAD = 17 * IN_TN
PAR_W0, PAR_W1, PAR_W2, PAR_BIAS, PAR_L2SCALE, PAR_SBIAS, PAR_SALOG, PAR_SKIND = range(8)


def _in_proj_kernel(x_ref, nw_ref, mod_ref, w_ref, par_ref, o_ref, h_ref, raw0_ref, raw1_ref, *, nctx_blk):
    j = pl.program_id(1)
    nj = N_IN_PAD // IN_TN
    raws = (raw0_ref, raw1_ref)

    @pl.when(j == 0)
    def _():
        h = _norm_mod(x_ref[...], nw_ref[...], mod_ref[0, 1:2, :], mod_ref[0, 0:1, :])
        h_ref[...] = h.astype(BF16)

    T = h_ref.shape[0]
    G = GRID_W
    per_ctx = CONV_ROWS // G
    is_latent = pl.program_id(0) >= nctx_blk
    sub = lax.broadcasted_iota(jnp.int32, (8, 128), 0)

    def raw_piece(src, g, c):
        return src[g * G:(g + 1) * G, c * 128:(c + 1) * 128]

    def conv(src, g, c):
        cs = slice(c * 128, (c + 1) * 128)
        x = raw_piece(src, g, c)
        zero = jnp.zeros((1, 128), F32)
        before = zero if g % per_ctx == 0 else jnp.where(is_latent, 0.0, src[g * G - 1:g * G, cs])
        after = zero if g % per_ctx == per_ctx - 1 else jnp.where(is_latent, 0.0, src[(g + 1) * G:(g + 1) * G + 1, cs])
        rp = pltpu.roll(x, 1, 0)
        rn = pltpu.roll(x, G - 1, 0)
        prev = jnp.concatenate([jnp.where(sub == 0, before, rp[0:8]), rp[8:]], axis=0)
        nxt = jnp.concatenate([rn[:G - 8], jnp.where(sub == 7, after, rn[G - 8:])], axis=0)
        return (prev * par_ref[PAR_W0:PAR_W0 + 1, cs] + x * par_ref[PAR_W1:PAR_W1 + 1, cs]
                + nxt * par_ref[PAR_W2:PAR_W2 + 1, cs] + par_ref[PAR_BIAS:PAR_BIAS + 1, cs])

    def conv_silu(src, g, c):
        return _silu(conv(src, g, c))

    def conv_silu_l2(src, g, c):
        y = _silu(conv(src, g, c))
        y = y * lax.rsqrt(jnp.sum(y * y, axis=-1, keepdims=True) + EPS)
        return y * par_ref[PAR_L2SCALE:PAR_L2SCALE + 1, c * 128:(c + 1) * 128]

    def small(src, g, c):
        if c > 0:
            return jnp.zeros((G, 128), F32)
        cs = slice(c * 128, (c + 1) * 128)
        acc = raw_piece(src, g, c)
        kind = par_ref[PAR_SKIND:PAR_SKIND + 1, cs]
        sp = _softplus(acc + par_ref[PAR_SBIAS:PAR_SBIAS + 1, cs])
        dec = -jnp.exp(par_ref[PAR_SALOG:PAR_SALOG + 1, cs]) * sp
        return jnp.where(kind == 0.0, sp, jnp.where(kind == 1.0, dec, jnp.where(kind == 2.0, _sigmoid(acc), 0.0)))

    def is_in(js):
        m = j == js[0]
        for jj in js[1:]:
            m = m | (j == jj)
        return m

    rows_mm = 256

    def project(dst, r):
        rs = slice(r * rows_mm, (r + 1) * rows_mm)
        dst[rs, :] = _dot(h_ref[rs, :], w_ref[...])

    @pl.when(j == 0)
    def _():
        for r in range(T // rows_mm):
            project(raw0_ref, r)

    epilogues = ((IN_MODE_RAW, raw_piece), (IN_MODE_CONV, conv), (IN_MODE_CONV_SILU, conv_silu),
                 (IN_MODE_CONV_SILU_L2, conv_silu_l2), (IN_MODE_SMALL, small))
    for blocks, fn in epilogues:
        for parity in range(2):
            steps = [jj + 1 for jj in blocks if (jj + 1) % 2 == parity]
            if not steps:
                continue

            @pl.when(is_in(steps))
            def _(fn=fn, parity=parity, steps=steps):
                src, dst = raws[1 - parity], raws[parity]
                for r in range(T // rows_mm):
                    if steps != [nj]:
                        project(dst, r)
                    for g in range(r * rows_mm // G, (r + 1) * rows_mm // G):
                        for c in range(IN_TN // 128):
                            o_ref[g * G:(g + 1) * G, c * 128:(c + 1) * 128] = fn(src, g, c)


class Rows:
    def __init__(self, B, Lc, Ll):
        self.B, self.Lc, self.Ll = B, Lc, Ll
        self.NC = B * Lc
        self.R = B * Lc + B * Ll
        assert self.NC % Ll == 0 or Ll % self.NC == 0
        tm = 1024
        while self.NC % tm or Ll % tm:
            tm //= 2
        self.tm = tm

    def mod_index(self, tm):
        nctx = self.NC // tm
        per = self.Ll // tm
        return lambda i: jnp.where(i < nctx, 0, 1 + (i - nctx) // per)


def in_proj(rw, x, nw, mod, w, par):
    R, D = x.shape
    N = w.shape[1]
    tm, tn = rw.tm, IN_TN
    nj = N // tn
    assert N == N_IN_PAD
    mi = rw.mod_index(tm)
    done = lambda j: jnp.maximum(j - 1, 0)
    return pl.pallas_call(
        functools.partial(_in_proj_kernel, nctx_blk=rw.NC // tm),
        grid=(R // tm, nj + 1),
        in_specs=[
            pl.BlockSpec((tm, D), lambda i, j: (i, 0)),
            pl.BlockSpec((1, D), lambda i, j: (0, 0)),
            pl.BlockSpec((1, 8, D), lambda i, j: (mi(i), 0, 0)),
            pl.BlockSpec((D, tn), lambda i, j: (0, jnp.minimum(j, nj - 1))),
            pl.BlockSpec((8, tn), lambda i, j: (0, done(j))),
        ],
        out_specs=pl.BlockSpec((tm, tn), lambda i, j: (i, done(j))),
        out_shape=jax.ShapeDtypeStruct((R, N), F32),
        scratch_shapes=[pltpu.VMEM((tm, D), BF16), pltpu.VMEM((tm, tn), F32), pltpu.VMEM((tm, tn), F32)],
        compiler_params=_cp("arbitrary", "arbitrary"),
        name="in_proj",
    )(x, nw.reshape(1, D), mod, w, par)


def _in_proj_params(hy_conv_w, hy_conv_b, ssm_conv_w, ssm_conv_b, gdn_conv_w, ssm_dt_bias, gdn_dt_bias, gdn_A_log):
    def row(pieces):
        v = jnp.zeros((N_IN_PAD,), F32)
        for off, a in pieces:
            v = lax.dynamic_update_slice(v, a.astype(F32), (off,))
        return v
    z4 = jnp.zeros((4,), F32)
    conv = [row([(C_HY, hy_conv_w[t]), (C_XBC, ssm_conv_w[t]), (C_QKV, gdn_conv_w[t])]) for t in range(3)]
    bias = row([(C_HY, hy_conv_b), (C_XBC, ssm_conv_b)])
    l2s = row([(C_QKV, jnp.full((512,), GDN_DK ** -0.5, F32)), (C_QKV + 512, jnp.ones((512,), F32))])
    sbias = row([(C_SM, jnp.concatenate([ssm_dt_bias.reshape(16), gdn_dt_bias[0], z4, gdn_dt_bias[1], z4]))])
    salog = row([(C_SM + 16, jnp.concatenate([gdn_A_log[0], z4, gdn_A_log[1], z4]))])
    kind = np.full((N_IN_PAD,), 3.0, np.float32)
    kind[C_SM:C_SM + 16] = 0.0
    kind[C_SM + 16:C_SM + 20] = 1.0
    kind[C_SM + 24:C_SM + 28] = 1.0
    kind[C_SM + 20:C_SM + 24] = 2.0
    kind[C_SM + 28:C_SM + 32] = 2.0
    return jnp.stack(conv + [bias, l2s, sbias, salog, jnp.asarray(kind)], axis=0)


def _hy_filter_kernel(z_ref, w1_ref, b1_ref, w2_ref, b2_ref, w3_ref, f0_ref, f1_ref, win_ref, o_ref, h_ref):
    @pl.when(pl.program_id(1) == 0)
    def _():
        h1 = jnp.sin(f0_ref[...] * (_dot(z_ref[...], w1_ref[...], HI) + b1_ref[...]))
        h_ref[...] = jnp.sin(f1_ref[...] * (_dot(h1, w2_ref[...], HI) + b2_ref[...]))

    h = _dot(h_ref[...], w3_ref[...], HI) * win_ref[...]
    tl = h.shape[0]
    row = lax.broadcasted_iota(jnp.int32, (tl, 1), 0) + pl.program_id(0) * tl
    drop = (row == 0) & (pl.program_id(1) % 2 == 1)
    o_ref[...] = jnp.where(drop, 0.0, h).astype(o_ref.dtype)


def hy_features(L):
    t = jnp.linspace(0.0, 1.0, L, dtype=F32)[:, None]
    w = 2.0 * math.pi * jnp.arange(L, dtype=F32)[:, None] / L
    f = jnp.linspace(1e-4, HY_BANDS - 1, HY_BANDS, dtype=F32)[None, :]
    z = jnp.concatenate([t, jnp.cos(f * w), -jnp.sin(f * w)], axis=-1)
    z = jnp.pad(z, ((0, 0), (0, 128 - HY_EMB)))
    min_decay = math.log(HY_TARGET) / HY_LONG_DECAY_PCT
    max_decay = math.log(HY_TARGET) / HY_SHORT_DECAY_PCT
    deltas = jnp.linspace(min_decay, max_decay, HY_WIDTH, dtype=F32)
    window = jnp.exp(-t * jnp.abs(deltas))
    return z, window


def hy_filter(feat, w1, b1, w2, b2, w3, freq):
    z, window = feat
    L = z.shape[0]
    H = HY_HIDDEN
    w1p = jnp.pad(w1, ((0, 128 - HY_EMB), (0, 128 - H)))
    w2p = jnp.pad(w2, ((0, 128 - H), (0, 128 - H)))
    w3p = jnp.pad(w3, ((0, 128 - H), (0, 0)))
    pad1 = lambda v: jnp.pad(v, (0, 128 - H)).reshape(1, 128)
    tl = 256
    full = lambda shape: pl.BlockSpec(shape, lambda i, j: (0, 0))
    return pl.pallas_call(
        _hy_filter_kernel,
        grid=(L // tl, 4),
        in_specs=[
            pl.BlockSpec((tl, 128), lambda i, j: (i, 0)),
            full((128, 128)), full((1, 128)), full((128, 128)), full((1, 128)),
            pl.BlockSpec((128, HY_WIDTH), lambda i, j: (0, j)),
            full((1, 128)), full((1, 128)),
            pl.BlockSpec((tl, HY_WIDTH), lambda i, j: (i, 0)),
        ],
        out_specs=pl.BlockSpec((tl, HY_WIDTH), lambda i, j: (i, j)),
        out_shape=jax.ShapeDtypeStruct((L, 4 * HY_WIDTH), BF16),
        scratch_shapes=[pltpu.VMEM((tl, 128), F32)],
        compiler_params=_cp("arbitrary", "arbitrary"),
        name="hy_filter",
    )(z, w1p, pad1(b1), w2p, pad1(b2), w3p, pad1(freq[0]), pad1(freq[1]), window)


def dft_tables(L):
    N = 2 * L
    f = jnp.arange(L, dtype=jnp.int32)[:, None]
    s = jnp.arange(L, dtype=jnp.int32)[None, :]
    ang = ((f * s) % N).astype(F32) * (2.0 * math.pi / N)
    c, sn = jnp.cos(ang), jnp.sin(ang)
    alt = (1 - 2 * (s % 2)).astype(F32)
    first = f == 0
    fwd = jnp.concatenate([c, jnp.where(first, alt, -sn)], axis=0)
    wgt = jnp.where(first, 1.0, 2.0) / N
    inv = jnp.concatenate([(c * wgt).T, jnp.where(first, alt / N, -sn * wgt).T], axis=1)
    return fwd.astype(BF16), inv.astype(BF16)


def _matmul_kernel(a_ref, b_ref, o_ref):
    o_ref[...] = _dot(a_ref[...], b_ref[...])


def matmul(a, b, tm, tn):
    M, K = a.shape
    N = b.shape[1]
    return pl.pallas_call(
        _matmul_kernel,
        grid=(M // tm, N // tn),
        in_specs=[pl.BlockSpec((tm, K), lambda i, j: (i, 0)), pl.BlockSpec((K, tn), lambda i, j: (0, j))],
        out_specs=pl.BlockSpec((tm, tn), lambda i, j: (i, j)),
        out_shape=jax.ShapeDtypeStruct((M, N), F32),
        compiler_params=_cp("arbitrary", "arbitrary"),
        name="matmul",
    )(a, b)


def _long_conv_kernel(u_ref, g_ref, bias_ref, fr_ref, fi_ref, ic_ref, is_ref,
                      ar0_ref, ar1_ref, ai0_ref, ai1_ref, o_ref, ub_ref, acc_ref):
    f = pl.program_id(1)

    @pl.when(f == 0)
    def _():
        ub_ref[...] = u_ref[...].astype(BF16)
        acc_ref[...] = jnp.zeros_like(acc_ref)

    ub = ub_ref[...]
    ur = _dot(fr_ref[...], ub)
    ui = _dot(fi_ref[...], ub)
    kr = ar0_ref[...] + ar1_ref[...]
    nyq = (lax.broadcasted_iota(jnp.int32, (FREQ_BLK, 1), 0) == 0) & (f == 0)
    ki = jnp.where(nyq, ai0_ref[...] + ai1_ref[...], ai0_ref[...] - ai1_ref[...])
    pr = jnp.where(nyq, ur * kr, ur * kr - ui * ki)
    pi = jnp.where(nyq, ui * ki, ur * ki + ui * kr)
    acc_ref[...] += _dot(ic_ref[...], pr.astype(BF16)) + _dot(is_ref[...], pi.astype(BF16))

    @pl.when(f == pl.num_programs(1) - 1)
    def _():
        u = u_ref[...]
        o_ref[...] = (g_ref[...] * (acc_ref[...] + u * bias_ref[...])).astype(o_ref.dtype)


def long_conv(B, L, u, u_rb0, u_cb, gate, g_rb0, gate_cb, bias, fwd, inv, kspec, order, out_dtype):
    C = HY_WIDTH
    nfb = L // FREQ_BLK
    FB = FREQ_BLK
    return pl.pallas_call(
        _long_conv_kernel,
        grid=(B, nfb),
        in_specs=[
            pl.BlockSpec((L, C), lambda b, f: (u_rb0 + b, u_cb)),
            pl.BlockSpec((L, C), lambda b, f: (g_rb0 + b, gate_cb)),
            pl.BlockSpec((1, C), lambda b, f: (0, 0)),
            pl.BlockSpec((FB, L), lambda b, f: (f, 0)),
            pl.BlockSpec((FB, L), lambda b, f: (nfb + f, 0)),
            pl.BlockSpec((L, FB), lambda b, f: (0, f)),
            pl.BlockSpec((L, FB), lambda b, f: (0, nfb + f)),
            pl.BlockSpec((FB, C), lambda b, f: (f, 2 * order)),
            pl.BlockSpec((FB, C), lambda b, f: (f, 2 * order + 1)),
            pl.BlockSpec((FB, C), lambda b, f: (nfb + f, 2 * order)),
            pl.BlockSpec((FB, C), lambda b, f: (nfb + f, 2 * order + 1)),
        ],
        out_specs=pl.BlockSpec((L, C), lambda b, f: (b, 0)),
        out_shape=jax.ShapeDtypeStruct((B * L, C), out_dtype),
        scratch_shapes=[pltpu.VMEM((L, C), BF16), pltpu.VMEM((L, C), F32)],
        compiler_params=_cp("arbitrary", "arbitrary"),
        name="long_conv",
    )(u, gate, bias.reshape(1, C), fwd, fwd, inv, inv, kspec, kspec, kspec, kspec)


def _scan_blocks(rw, rows):
    nbc, nbl, base = rw.Lc // rows, rw.Ll // rows, rw.NC // rows

    def make(d):
        def f(b, s):
            jc = s if d == 0 else nbc - 1 - s
            jl = (s - nbc) if d == 0 else nbl - 1 - (s - nbc)
            return jnp.where(s < nbc, b * nbc + jc, base + b * nbl + jl)
        return f

    return [make(0), make(1)], nbc + nbl


def _expand_lanes(x, base, n, width):
    rows = x.shape[0]
    per = 128 // width
    lane = lax.broadcasted_iota(jnp.int32, (rows, 128), 1)
    tiles = []
    for t in range(n // per):
        c0 = base + t * per
        tile = jnp.broadcast_to(x[:, c0:c0 + 1], (rows, 128))
        for i in range(1, per):
            tile = jnp.where(lane >= i * width, jnp.broadcast_to(x[:, c0 + i:c0 + i + 1], (rows, 128)), tile)
        tiles.append(tile)
    return jnp.concatenate(tiles, axis=1)


def _ssd_kernel(xf, bf, cf, smf, dtf, xb, bb, cb_, smb, dtb, alx_ref, alc_ref, of_ref, ob_ref, h_ref):
    Q = SSM_CHUNK
    GW = SSM_GW

    @pl.when(pl.program_id(1) == 0)
    def _():
        h_ref[...] = jnp.zeros_like(h_ref)

    row = lax.broadcasted_iota(jnp.int32, (Q, Q), 0)
    col = lax.broadcasted_iota(jnp.int32, (Q, Q), 1)
    lane_head = lax.broadcasted_iota(jnp.int32, (Q, GW), 1) // SSM_HEAD_DIM
    dirs = ((xf, bf, cf, smf, dtf, of_ref), (xb, bb, cb_, smb, dtb, ob_ref))
    jobs = []
    for d in range(2):
        x_ref, b_ref, c_ref, sm_ref, dt_ref, o_ref = dirs[d]
        keep = (col <= row) if d == 0 else (col >= row)
        tri = keep.astype(BF16)
        tri_t = ((row <= col) if d == 0 else (row >= col)).astype(BF16)
        sm = sm_ref[...]
        a_x = -jnp.exp(alx_ref[d])
        dtx = _expand_lanes(sm, 8 * d, SSM_HEADS, SSM_HEAD_DIM)
        cumx = _expand_lanes(_dot_01_lhs(tri, sm), 8 * d, SSM_HEADS, SSM_HEAD_DIM) * a_x
        cumr = _dot_01_rhs(dt_ref[0], tri_t) * (-jnp.exp(alc_ref[d]))
        last = Q - 1 if d == 0 else 0
        totx = cumx[last:last + 1, :]
        xd = x_ref[...] * dtx
        xdw = xd * jnp.exp(totx - cumx)
        ecum = jnp.exp(cumx)
        for g in range(SSM_GROUPS):
            gs = slice(g * GW, (g + 1) * GW)
            jobs.append(dict(d=d, g=g, gs=gs, keep=keep, cumx=cumx, cumr=cumr, o_ref=o_ref,
                             bg=b_ref[:, g * SSM_STATE:(g + 1) * SSM_STATE].astype(BF16),
                             cg=c_ref[:, g * SSM_STATE:(g + 1) * SSM_STATE].astype(BF16),
                             xdg=xd[:, gs], xdw=xdw[:, gs].astype(BF16), ecum=ecum[:, gs],
                             etot=jnp.exp(totx[:, gs])))
    for j in jobs:
        j["cb"] = _dot_nt(j["cg"], j["bg"])
        j["h"] = h_ref[j["d"], j["g"]]
    for j in jobs:
        ms, xs = [], []
        for e4 in range(SSM_HPG):
            e = j["g"] * SSM_HPG + e4
            diff = j["cumx"][:, e * SSM_HEAD_DIM:e * SSM_HEAD_DIM + 1] - j["cumr"][e:e + 1, :]
            ms.append((j["cb"] * jnp.where(j["keep"], jnp.exp(diff), 0.0)).astype(BF16))
            xs.append(jnp.where(lane_head == e4, j["xdg"], 0.0).astype(BF16))
        yd = _dot(jnp.concatenate(ms, axis=1), jnp.concatenate(xs, axis=0))
        y_off = _dot(j["cg"], j["h"].astype(BF16)) * j["ecum"]
        j["o_ref"][:, j["gs"]] = yd + y_off
    for j in jobs:
        h_ref[j["d"], j["g"]] = j["h"] * j["etot"] + _dot_tn(j["bg"], j["xdw"])


def ssd_scan(rw, p, dtT, alx, alc):
    Q = SSM_CHUNK
    blks, nsteps = _scan_blocks(rw, Q)
    R = p.shape[0]
    in_specs = []
    for d in range(2):
        f = blks[d]
        in_specs += [
            pl.BlockSpec((Q, 512), lambda b, s, f=f: (f(b, s), C_XBC // 512)),
            pl.BlockSpec((Q, 256), lambda b, s, f=f: (f(b, s), C_XBC // 256 + 2)),
            pl.BlockSpec((Q, 256), lambda b, s, f=f: (f(b, s), C_XBC // 256 + 3)),
            pl.BlockSpec((Q, 128), lambda b, s, f=f: (f(b, s), C_SM // 128)),
            pl.BlockSpec((1, 8, Q), lambda b, s, f=f, d=d: (d, 0, f(b, s))),
        ]
    in_specs += [pl.BlockSpec((2, 1, 512), lambda b, s: (0, 0, 0)), pl.BlockSpec((2, 8, 1), lambda b, s: (0, 0, 0))]
    ops = (p, p, p, p, dtT)
    return pl.pallas_call(
        _ssd_kernel,
        grid=(rw.B, nsteps),
        in_specs=in_specs,
        out_specs=[pl.BlockSpec((Q, 512), lambda b, s, f=blks[d]: (f(b, s), 0)) for d in range(2)],
        out_shape=[jax.ShapeDtypeStruct((R, 512), F32)] * 2,
        scratch_shapes=[pltpu.VMEM((2, SSM_GROUPS, SSM_STATE, SSM_GW), F32)],
        compiler_params=_cp("arbitrary", "arbitrary"),
        name="ssd_scan",
    )(*ops, *ops, alx, alc)


def _ssm_post_kernel(yf_ref, yb_ref, x_ref, z_ref, dx_ref, nw_ref, o_ref):
    y = yf_ref[...] + yb_ref[...] + x_ref[...] * dx_ref[...]
    y = y * _silu(z_ref[...])
    parts = []
    for g in range(SSM_GROUPS):
        yg = y[:, g * SSM_GW:(g + 1) * SSM_GW]
        ms = jnp.mean(yg * yg, axis=-1, keepdims=True)
        parts.append(yg * lax.rsqrt(ms + EPS))
    o_ref[...] = (jnp.concatenate(parts, axis=1) * nw_ref[...]).astype(o_ref.dtype)


def ssm_post(rw, y_f, y_b, p, dx, nw):
    R = p.shape[0]
    tm = rw.tm
    vec = pl.BlockSpec((1, 512), lambda i: (0, 0))
    return pl.pallas_call(
        _ssm_post_kernel,
        grid=(R // tm,),
        in_specs=[
            pl.BlockSpec((tm, 512), lambda i: (i, 0)),
            pl.BlockSpec((tm, 512), lambda i: (i, 0)),
            pl.BlockSpec((tm, 512), lambda i: (i, C_XBC // 512)),
            pl.BlockSpec((tm, 512), lambda i: (i, C_Z // 512)),
            vec, vec,
        ],
        out_specs=pl.BlockSpec((tm, 512), lambda i: (i, 0)),
        out_shape=jax.ShapeDtypeStruct((R, 512), BF16),
        compiler_params=_cp("arbitrary"),
        name="ssm_post",
    )(y_f, y_b, p, p, dx, nw)


def _split3(x):
    x1 = x.astype(BF16)
    r = x - x1.astype(F32)
    x2 = r.astype(BF16)
    x3 = (r - x2.astype(F32)).astype(BF16)
    return x1, x2, x3


def _dot_01_lhs(m01, x):
    x1, x2, x3 = _split3(x)
    return _dot(m01, x1) + _dot(m01, x2) + _dot(m01, x3)


def _dot_01_rhs(x, m01):
    x1, x2, x3 = _split3(x)
    return _dot(x1, m01) + _dot(x2, m01) + _dot(x3, m01)


GDN_ROWS = 256


def _gdn_prep_kernel(q_ref, k_ref, v_ref, sm_ref, gT_ref, u_ref, w_ref, qg_ref, kd_ref, qk_ref, egl_ref):
    C = GDN_CHUNK
    row = lax.broadcasted_iota(jnp.int32, (C, C), 0)
    col = lax.broadcasted_iota(jnp.int32, (C, C), 1)
    jobs = []
    levels = []
    for d in range(2):
        keep = (col <= row) if d == 0 else (col >= row)
        late, early = (row, col) if d == 0 else (col, row)
        levels.append([(((row ^ col) >> (j + 1)) == 0) & ((late & (1 << j)) != 0) & ((early & (1 << j)) == 0)
                       for j in range(6)])
        tri = keep.astype(BF16)
        tri_t = ((row <= col) if d == 0 else (row >= col)).astype(BF16)
        last = C - 1 if d == 0 else 0
        for c in range(GDN_ROWS // C):
            rows = slice(c * C, (c + 1) * C)
            smc = sm_ref[rows, :]
            cums = _dot_01_lhs(tri, smc)
            cumr = _dot_01_rhs(gT_ref[c, 8 * d:8 * d + 8, :], tri_t)
            tot = cums[last:last + 1, :]
            for h in range(GDN_HEADS):
                lg = 16 + 8 * d + h
                jobs.append(dict(d=d, c=c, h=h, rows=rows, hs=slice(h * 128, (h + 1) * 128), keep=keep,
                                 gc=cums[:, lg:lg + 1], beta=smc[:, lg + 4:lg + 5],
                                 gl=tot[:, lg:lg + 1], gr=cumr[h:h + 1, :]))
    for j in jobs:
        q = q_ref[j["rows"], j["hs"]]
        k = k_ref[j["rows"], j["hs"]]
        j["dec"] = jnp.where(j["keep"], jnp.exp(j["gc"] - j["gr"]), 0.0)
        kb = k * j["beta"]
        both = _dot_nt(jnp.concatenate([kb, q], axis=0).astype(BF16), k.astype(BF16))
        j["a"] = both[:C] * j["dec"]
        j["n"] = -jnp.where(levels[j["d"]][0], j["a"], 0.0)
        qk_ref[j["d"], j["c"], j["h"]] = (both[C:] * j["dec"]).astype(BF16)
    for lev in range(1, 6):
        for j in jobs:
            l = jnp.where(levels[j["d"]][lev], j["a"], 0.0)
            j["y"] = l + _dot(l.astype(BF16), j["n"].astype(BF16))
        for j in jobs:
            j["n"] = j["n"] - j["y"] - _dot(j["n"].astype(BF16), j["y"].astype(BF16))
    for j in jobs:
        d, rows, hs, gc, gl, beta = j["d"], j["rows"], j["hs"], j["gc"], j["gl"], j["beta"]
        q = q_ref[rows, hs]
        k = k_ref[rows, hs]
        eg = jnp.exp(gc)
        rhs = jnp.concatenate([v_ref[rows, hs] * beta, k * beta * eg], axis=1)
        sol = rhs + _dot(j["n"].astype(BF16), rhs.astype(BF16))
        u_ref[d, rows, hs] = sol[:, :GDN_DV]
        w_ref[d, rows, hs] = sol[:, GDN_DV:].astype(BF16)
        qg_ref[d, rows, hs] = (q * eg).astype(BF16)
        kd_ref[d, rows, hs] = (k * jnp.exp(gl - gc)).astype(BF16)
        egl_ref[d, j["c"], :, hs] = jnp.broadcast_to(jnp.exp(gl), (8, 128))


def gdn_prep(p, gT):
    R = p.shape[0]
    T, C = GDN_ROWS, GDN_CHUNK
    nc = T // C
    col = lambda k: pl.BlockSpec((T, 512), lambda i: (i, C_QKV // 512 + k))
    dirrow = pl.BlockSpec((2, T, 512), lambda i: (0, i, 0))
    return pl.pallas_call(
        _gdn_prep_kernel,
        grid=(R // T,),
        in_specs=[col(0), col(1), col(2),
                  pl.BlockSpec((T, 128), lambda i: (i, C_SM // 128)),
                  pl.BlockSpec((nc, 16, C), lambda i: (i, 0, 0))],
        out_specs=[dirrow, dirrow, dirrow, dirrow,
                   pl.BlockSpec((2, nc, GDN_HEADS, C, C), lambda i: (0, i, 0, 0, 0)),
                   pl.BlockSpec((2, nc, 8, 512), lambda i: (0, i, 0, 0))],
        out_shape=[jax.ShapeDtypeStruct((2, R, 512), F32),
                   jax.ShapeDtypeStruct((2, R, 512), BF16),
                   jax.ShapeDtypeStruct((2, R, 512), BF16),
                   jax.ShapeDtypeStruct((2, R, 512), BF16),
                   jax.ShapeDtypeStruct((2, R // C, GDN_HEADS, C, C), BF16),
                   jax.ShapeDtypeStruct((2, R // C, 8, 512), F32)],
        compiler_params=_cp("arbitrary"),
        name="gdn_prep",
    )(p, p, p, p, gT)


def _gdn_scan_kernel(uf, wf, qgf, kdf, qkf, eglf, ub, wb, qgb, kdb, qkb, eglb, of_ref, ob_ref, s_ref):
    C = GDN_CHUNK
    nch = GDN_ROWS // C

    @pl.when(pl.program_id(1) == 0)
    def _():
        s_ref[...] = jnp.zeros_like(s_ref)

    dirs = ((uf, wf, qgf, kdf, qkf, eglf, of_ref), (ub, wb, qgb, kdb, qkb, eglb, ob_ref))
    chains = [(d, h) for d in range(2) for h in range(GDN_HEADS)]
    S = {ch: s_ref[ch[0], ch[1]] for ch in chains}
    for i in range(nch):
        Sb, vnb, rows_of, c_of = {}, {}, {}, {}
        for d, h in chains:
            c_of[d] = i if d == 0 else nch - 1 - i
            rows_of[d] = slice(c_of[d] * C, (c_of[d] + 1) * C)
        for d, h in chains:
            hs = slice(h * 128, (h + 1) * 128)
            Sb[d, h] = S[d, h].astype(BF16)
            v_new = dirs[d][0][0, rows_of[d], hs] - _dot(dirs[d][1][0, rows_of[d], hs], Sb[d, h])
            vnb[d, h] = v_new.astype(BF16)
        for d, h in chains:
            hs = slice(h * 128, (h + 1) * 128)
            u_ref, w_ref, qg_ref, kd_ref, qk_ref, egl_ref, o_ref = dirs[d]
            S[d, h] = S[d, h] * egl_ref[0, c_of[d], 0:1, hs] + _dot_tn(kd_ref[0, rows_of[d], hs], vnb[d, h])
        for d, h in chains:
            hs = slice(h * 128, (h + 1) * 128)
            u_ref, w_ref, qg_ref, kd_ref, qk_ref, egl_ref, o_ref = dirs[d]
            o_ref[rows_of[d], hs] = (_dot(qg_ref[0, rows_of[d], hs], Sb[d, h])
                                     + _dot(qk_ref[0, c_of[d], h], vnb[d, h]))
    for ch in chains:
        s_ref[ch[0], ch[1]] = S[ch]


def gdn_scan(rw, u, w, qg, kd, qk, egl):
    T, C = GDN_ROWS, GDN_CHUNK
    nc = T // C
    R = u.shape[1]
    nbc, nbl, base = rw.Lc // T, rw.Ll // T, rw.NC // T

    def blk(d):
        def f(b, s):
            jc = s if d == 0 else nbc - 1 - s
            jl = (s - nbc) if d == 0 else nbl - 1 - (s - nbc)
            return jnp.where(s < nbc, b * nbc + jc, base + b * nbl + jl)
        return f

    in_specs = []
    for d in range(2):
        f = blk(d)
        rowspec = pl.BlockSpec((1, T, 512), lambda b, s, f=f, d=d: (d, f(b, s), 0))
        in_specs += [rowspec, rowspec, rowspec, rowspec,
                     pl.BlockSpec((1, nc, GDN_HEADS, C, C), lambda b, s, f=f, d=d: (d, f(b, s), 0, 0, 0)),
                     pl.BlockSpec((1, nc, 8, 512), lambda b, s, f=f, d=d: (d, f(b, s), 0, 0))]
    out_specs = [pl.BlockSpec((T, 512), lambda b, s, f=blk(d): (f(b, s), 0)) for d in range(2)]
    ops = (u, w, qg, kd, qk, egl)
    return pl.pallas_call(
        _gdn_scan_kernel,
        grid=(rw.B, nbc + nbl),
        in_specs=in_specs,
        out_specs=out_specs,
        out_shape=[jax.ShapeDtypeStruct((R, 512), F32)] * 2,
        scratch_shapes=[pltpu.VMEM((2, GDN_HEADS, GDN_DK, GDN_DV), F32)],
        compiler_params=_cp("arbitrary", "arbitrary"),
        name="gdn_scan",
    )(*ops, *ops)


def _gdn_post_kernel(of_ref, ob_ref, g_ref, nw_ref, o_ref):
    o = of_ref[...] + ob_ref[...]
    parts = []
    for h in range(GDN_HEADS):
        oh = o[:, h * 128:(h + 1) * 128]
        ms = jnp.mean(oh * oh, axis=-1, keepdims=True)
        parts.append(oh * lax.rsqrt(ms + EPS))
    o_ref[...] = (jnp.concatenate(parts, axis=1) * nw_ref[...] * _silu(g_ref[...])).astype(o_ref.dtype)


def gdn_post(rw, o_f, o_b, p, nw):
    R = p.shape[0]
    tm = rw.tm
    return pl.pallas_call(
        _gdn_post_kernel,
        grid=(R // tm,),
        in_specs=[
            pl.BlockSpec((tm, 512), lambda i: (i, 0)),
            pl.BlockSpec((tm, 512), lambda i: (i, 0)),
            pl.BlockSpec((tm, 512), lambda i: (i, C_GG // 512)),
            pl.BlockSpec((1, 512), lambda i: (0, 0)),
        ],
        out_specs=pl.BlockSpec((tm, 512), lambda i: (i, 0)),
        out_shape=jax.ShapeDtypeStruct((R, 512), BF16),
        compiler_params=_cp("arbitrary"),
        name="gdn_post",
    )(o_f, o_b, p, nw)


def _merge_kernel(yh_ref, ys_ref, yg_ref, g0_ref, g1_ref, g2_ref, w0_ref, w1_ref, w2_ref, wo_ref,
                  x_ref, mod_ref, o_ref):
    m = (_sigmoid(g0_ref[...]) * _dot(yh_ref[...], w0_ref[...])
         + _sigmoid(g1_ref[...]) * _dot(ys_ref[...], w1_ref[...])
         + _sigmoid(g2_ref[...]) * _dot(yg_ref[...], w2_ref[...]))
    o_ref[...] = x_ref[...] + mod_ref[0, 2:3, :] * _dot(m.astype(BF16), wo_ref[...])


def merge(rw, yh, ys, yg, p, w0, w1, w2, wo, x, mod):
    R, D = x.shape
    tm = min(rw.tm, 512)
    mi = rw.mod_index(tm)
    yspec = pl.BlockSpec((tm, 512), lambda i: (i, 0))
    gspec = lambda k: pl.BlockSpec((tm, D), lambda i: (i, C_GATE // D + k))
    wspec = pl.BlockSpec((512, D), lambda i: (0, 0))
    return pl.pallas_call(
        _merge_kernel,
        grid=(R // tm,),
        in_specs=[yspec, yspec, yspec, gspec(0), gspec(1), gspec(2), wspec, wspec, wspec,
                  pl.BlockSpec((D, D), lambda i: (0, 0)),
                  pl.BlockSpec((tm, D), lambda i: (i, 0)),
                  pl.BlockSpec((1, 8, D), lambda i: (mi(i), 0, 0))],
        out_specs=pl.BlockSpec((tm, D), lambda i: (i, 0)),
        out_shape=jax.ShapeDtypeStruct((R, D), F32),
        compiler_params=_cp("arbitrary"),
        name="merge",
    )(yh, ys, yg, p, p, p, w0, w1, w2, wo, x, mod)


def _swiglu_up_kernel(x_ref, nw_ref, mod_ref, wg_ref, wu_ref, o_ref, h_ref):
    @pl.when(pl.program_id(1) == 0)
    def _():
        h = _norm_mod(x_ref[...], nw_ref[...], mod_ref[0, 4:5, :], mod_ref[0, 3:4, :])
        h_ref[...] = h.astype(BF16)

    h = h_ref[...]
    g = _dot(h, wg_ref[...])
    u = _dot(h, wu_ref[...])
    o_ref[...] = (_silu(g) * u).astype(o_ref.dtype)


def swiglu_up(rw, x, nw, mod, wgu):
    R, D = x.shape
    tm = rw.tm
    tn = D_FF // 2
    nj = D_FF // tn
    mi = rw.mod_index(tm)
    return pl.pallas_call(
        _swiglu_up_kernel,
        grid=(R // tm, nj),
        in_specs=[
            pl.BlockSpec((tm, D), lambda i, j: (i, 0)),
            pl.BlockSpec((1, D), lambda i, j: (0, 0)),
            pl.BlockSpec((1, 8, D), lambda i, j: (mi(i), 0, 0)),
            pl.BlockSpec((D, tn), lambda i, j: (0, j)),
            pl.BlockSpec((D, tn), lambda i, j: (0, nj + j)),
        ],
        out_specs=pl.BlockSpec((tm, tn), lambda i, j: (i, j)),
        out_shape=jax.ShapeDtypeStruct((R, D_FF), BF16),
        scratch_shapes=[pltpu.VMEM((tm, D), BF16)],
        compiler_params=_cp("arbitrary", "arbitrary"),
        name="swiglu_up",
    )(x, nw.reshape(1, D), mod, wgu, wgu)


def _swiglu_down_kernel(a_ref, w_ref, x_ref, mod_ref, o_ref):
    o_ref[...] = x_ref[...] + mod_ref[0, 5:6, :] * _dot(a_ref[...], w_ref[...])


def swiglu_down(rw, a, w, x, mod):
    R, D = x.shape
    tm = min(rw.tm, 512)
    mi = rw.mod_index(tm)
    return pl.pallas_call(
        _swiglu_down_kernel,
        grid=(R // tm,),
        in_specs=[
            pl.BlockSpec((tm, D_FF), lambda i: (i, 0)),
            pl.BlockSpec((D_FF, D), lambda i: (0, 0)),
            pl.BlockSpec((tm, D), lambda i: (i, 0)),
            pl.BlockSpec((1, 8, D), lambda i: (mi(i), 0, 0)),
        ],
        out_specs=pl.BlockSpec((tm, D), lambda i: (i, 0)),
        out_shape=jax.ShapeDtypeStruct((R, D), F32),
        compiler_params=_cp("arbitrary"),
        name="swiglu_down",
    )(a, w, x, mod)


def _final_norm_kernel(x_ref, w_ref, o_ref):
    x = x_ref[...]
    ms = jnp.mean(x * x, axis=-1, keepdims=True)
    o_ref[...] = x * lax.rsqrt(ms + EPS) * w_ref[...]


def final_norm(rw, x, w):
    D = x.shape[1]
    tm = rw.tm
    n0 = rw.NC // tm
    nl = rw.B * rw.Ll
    return pl.pallas_call(
        _final_norm_kernel,
        grid=(nl // tm,),
        in_specs=[pl.BlockSpec((tm, D), lambda i: (n0 + i, 0)), pl.BlockSpec((1, D), lambda i: (0, 0))],
        out_specs=pl.BlockSpec((tm, D), lambda i: (i, 0)),
        out_shape=jax.ShapeDtypeStruct((nl, D), F32),
        compiler_params=_cp("arbitrary"),
        name="final_norm",
    )(x, w.reshape(1, D))


def _regroup_w_in(w_in):
    o_dt = 3072
    o_gdn = 3088
    o_a = o_gdn + 2048
    o_b = o_a + 8
    o_gate = o_gdn + 2064
    pieces = [
        w_in[..., 0:3072],
        w_in[..., o_gdn:o_gdn + 2048],
        w_in[..., o_gate:o_gate + 3072],
        w_in[..., o_dt:o_dt + 16],
        w_in[..., o_a:o_a + 4], w_in[..., o_b:o_b + 4],
        w_in[..., o_a + 4:o_a + 8], w_in[..., o_b + 4:o_b + 8],
        jnp.zeros(w_in.shape[:-1] + (N_IN_PAD - C_SM - 32,), w_in.dtype),
    ]
    return jnp.concatenate(pieces, axis=-1).astype(BF16)


def kernel(x, c, ctx, c_ctx, w_ada, b_ada, norm1_w, norm2_w, w_in, hy_conv_w, hy_conv_b, hy_w1, hy_b1, hy_w2, hy_b2, hy_w3, hy_freq, hy_bias, ssm_conv_w, ssm_conv_b, ssm_dt_bias, ssm_A_log, ssm_D, ssm_norm_w, gdn_conv_w, gdn_dt_bias, gdn_A_log, gdn_norm_w, w_hy_out, w_ssm_out, w_gdn_out, w_out, w_gate_up, w_down, final_norm_w):
    B, Ll, D = x.shape
    Lc = ctx.shape[1]
    depth = w_ada.shape[0]
    assert Lc == CONV_ROWS and D == D_MODEL and B <= 15
    rw = Rows(B, Lc, Ll)
    R, NC = rw.R, rw.NC

    xa = jnp.concatenate([ctx.reshape(B * Lc, D), x.reshape(B * Ll, D)], axis=0)

    svec = jnp.concatenate([c_ctx[None, :], c, jnp.zeros((15 - B, D), F32)], axis=0)
    mod_all = ada_modulation(svec, w_ada, b_ada)
    mod_all = jnp.pad(mod_all.reshape(depth, 16, 6, D), ((0, 0), (0, 0), (0, 2), (0, 0)))

    w_in_r = _regroup_w_in(w_in)
    fwd_l, inv_l = dft_tables(Ll)
    fwd_c, inv_c = dft_tables(Lc)
    feat_l, feat_c = hy_features(Ll), hy_features(Lc)

    for l in range(depth):
        mod = mod_all[l]
        par = _in_proj_params(hy_conv_w[l], hy_conv_b[l], ssm_conv_w[l], ssm_conv_b[l], gdn_conv_w[l],
                              ssm_dt_bias[l], gdn_dt_bias[l], gdn_A_log[l])
        p = in_proj(rw, xa, norm1_w[l], mod, w_in_r[l], par)

        sm32_t = p[:, C_SM:C_SM + 32].T
        dt_t = sm32_t[:16].reshape(2, 8, R)
        g_t = sm32_t[16:32].reshape(16, R // GDN_CHUNK, GDN_CHUNK).transpose(1, 0, 2)

        alx = jnp.repeat(ssm_A_log[l], SSM_HEAD_DIM, axis=-1).reshape(2, 1, 512)
        alc = ssm_A_log[l].reshape(2, 8, 1)
        y_f, y_b = ssd_scan(rw, p, dt_t, alx, alc)
        dx = jnp.repeat(ssm_D[l], SSM_HEAD_DIM).reshape(1, 512)
        y_ssm = ssm_post(rw, y_f, y_b, p, dx, ssm_norm_w[l].reshape(1, 512))

        o_f, o_b = gdn_scan(rw, *gdn_prep(p, g_t))
        y_gdn = gdn_post(rw, o_f, o_b, p, jnp.tile(gdn_norm_w[l], GDN_HEADS).reshape(1, 512))

        hyu = p
        parts = []
        for (Bn, L, blk0, fwd, inv, feat) in ((B, Lc, 0, fwd_c, inv_c, feat_c),
                                              (B, Ll, NC // Ll, fwd_l, inv_l, feat_l)):
            if NC % L:
                raise ValueError("latent length must divide the context row count")
            filt = hy_filter(feat, hy_w1[l], hy_b1[l], hy_w2[l], hy_b2[l], hy_w3[l], hy_freq[l])
            kspec = matmul(fwd, filt, min(512, 2 * L), 512)
            z1 = long_conv(Bn, L, hyu, blk0, 0, hyu, blk0, 1, hy_bias[l, 0], fwd, inv, kspec, 0, F32)
            yy = long_conv(Bn, L, z1, 0, 0, hyu, blk0, 2, hy_bias[l, 1], fwd, inv, kspec, 1, BF16)
            parts.append(yy)
        y_hy = jnp.concatenate(parts, axis=0)

        xa = merge(rw, y_hy, y_ssm, y_gdn, p, w_hy_out[l].astype(BF16), w_ssm_out[l].astype(BF16),
                   w_gdn_out[l].astype(BF16), w_out[l].astype(BF16), xa, mod)
        act = swiglu_up(rw, xa, norm2_w[l], mod, w_gate_up[l].astype(BF16))
        xa = swiglu_down(rw, act, w_down[l].astype(BF16), xa, mod)

    out = final_norm(rw, xa, final_norm_w)
    return out.reshape(B, Ll, D)
```

```python
import functools
import math

import jax
import jax.numpy as jnp
import numpy as np
from jax import lax
from jax.experimental import pallas as pl
from jax.experimental.pallas import tpu as pltpu

F32 = jnp.float32
BF16 = jnp.bfloat16
HI = lax.Precision.HIGHEST

EPS = 1e-6
D_MODEL = 1024
GRID_W = 64

HY_WIDTH = 512
HY_BANDS = 16
HY_EMB = 1 + 2 * HY_BANDS
HY_HIDDEN = 64
HY_SHORT_DECAY_PCT = 0.3
HY_LONG_DECAY_PCT = 1.5
HY_TARGET = 1e-2

SSM_HEADS = 8
SSM_HEAD_DIM = 64
SSM_WIDTH = 512
SSM_GROUPS = 2
SSM_HPG = 4
SSM_STATE = 128
SSM_CHUNK = 128
SSM_GW = SSM_HPG * SSM_HEAD_DIM

GDN_HEADS = 4
GDN_DK = 128
GDN_DV = 128
GDN_CHUNK = 64

D_FF = 2816

C_HY = 0
C_Z = 1536
C_XBC = 2048
C_QKV = 3072
C_GG = 4608
C_GATE = 5120
C_SM = 8192

CONV_ROWS = 256
FREQ_BLK = 256

VMEM_LIMIT = 56 * 1024 * 1024


def _cp(*sem):
    return pltpu.CompilerParams(dimension_semantics=sem, vmem_limit_bytes=VMEM_LIMIT)


def _sigmoid(x):
    return 1.0 / (1.0 + jnp.exp(-x))


def _silu(x):
    return x * _sigmoid(x)


def _softplus(x):
    return jnp.maximum(x, 0.0) + jnp.log1p(jnp.exp(-jnp.abs(x)))


def _dot(a, b, precision=None):
    return jnp.dot(a, b, precision=precision, preferred_element_type=F32)


def _dot_nt(a, b):
    return lax.dot_general(a, b, (((1,), (1,)), ((), ())), preferred_element_type=F32)


def _dot_tn(a, b):
    return lax.dot_general(a, b, (((0,), (0,)), ((), ())), preferred_element_type=F32)


def _ada_kernel(s_ref, w_ref, b_ref, o_ref):
    s = _silu(s_ref[...])
    o_ref[0] = _dot(s, w_ref[0], HI) + b_ref[0]


def ada_modulation(svec, w_ada, b_ada):
    depth = w_ada.shape[0]
    D = D_MODEL
    return pl.pallas_call(
        _ada_kernel,
        grid=(depth, 6),
        in_specs=[
            pl.BlockSpec((16, D), lambda l, j: (0, 0)),
            pl.BlockSpec((1, D, D), lambda l, j: (l, 0, j)),
            pl.BlockSpec((1, 1, D), lambda l, j: (l, 0, j)),
        ],
        out_specs=pl.BlockSpec((1, 16, D), lambda l, j: (l, 0, j)),
        out_shape=jax.ShapeDtypeStruct((depth, 16, 6 * D), F32),
        compiler_params=_cp("arbitrary", "arbitrary"),
        name="ada",
    )(svec, w_ada, b_ada.reshape(depth, 1, 6 * D))


def _norm_mod(x, nw, scale, shift):
    ms = jnp.mean(x * x, axis=-1, keepdims=True)
    return (x * lax.rsqrt(ms + EPS) * nw) * (1.0 + scale) + shift


N_IN_PAD = C_SM + 128
IN_TN = N_IN_PAD // 5
MODE_RAW, MODE_CONV, MODE_CONV_SILU, MODE_CONV_SILU_L2, MODE_SMALL = range(5)


def _tile_mode(tile):
    col = tile * 128
    if col < C_Z:
        return MODE_CONV
    if col < C_XBC:
        return MODE_RAW
    if col < C_QKV:
        return MODE_CONV_SILU
    if col < C_QKV + 1024:
        return MODE_CONV_SILU_L2
    if col < C_GG:
        return MODE_CONV_SILU
    if col < C_SM:
        return MODE_RAW
    return MODE_SMALL
PAR_W0, PAR_W1, PAR_W2, PAR_BIAS, PAR_L2SCALE, PAR_SBIAS, PAR_SALOG, PAR_SKIND = range(8)


def _in_proj_kernel(x_ref, nw_ref, mod_ref, w_ref, par_ref, o_ref, h_ref, raw0_ref, raw1_ref, *, nctx_blk):
    j = pl.program_id(1)
    nj = N_IN_PAD // IN_TN
    raws = (raw0_ref, raw1_ref)

    @pl.when(j == 0)
    def _():
        h = _norm_mod(x_ref[...], nw_ref[...], mod_ref[0, 1:2, :], mod_ref[0, 0:1, :])
        h_ref[...] = h.astype(BF16)

    T = h_ref.shape[0]
    G = GRID_W
    per_ctx = CONV_ROWS // G
    is_latent = pl.program_id(0) >= nctx_blk
    sub = lax.broadcasted_iota(jnp.int32, (8, 128), 0)

    def raw_piece(src, g, c):
        return src[g * G:(g + 1) * G, c * 128:(c + 1) * 128]

    def conv(src, g, c):
        cs = slice(c * 128, (c + 1) * 128)
        x = raw_piece(src, g, c)
        zero = jnp.zeros((1, 128), F32)
        before = zero if g % per_ctx == 0 else jnp.where(is_latent, 0.0, src[g * G - 1:g * G, cs])
        after = zero if g % per_ctx == per_ctx - 1 else jnp.where(is_latent, 0.0, src[(g + 1) * G:(g + 1) * G + 1, cs])
        rp = pltpu.roll(x, 1, 0)
        rn = pltpu.roll(x, G - 1, 0)
        prev = jnp.concatenate([jnp.where(sub == 0, before, rp[0:8]), rp[8:]], axis=0)
        nxt = jnp.concatenate([rn[:G - 8], jnp.where(sub == 7, after, rn[G - 8:])], axis=0)
        return (prev * par_ref[PAR_W0:PAR_W0 + 1, cs] + x * par_ref[PAR_W1:PAR_W1 + 1, cs]
                + nxt * par_ref[PAR_W2:PAR_W2 + 1, cs] + par_ref[PAR_BIAS:PAR_BIAS + 1, cs])

    def conv_silu(src, g, c):
        return _silu(conv(src, g, c))

    def conv_silu_l2(src, g, c):
        y = _silu(conv(src, g, c))
        y = y * lax.rsqrt(jnp.sum(y * y, axis=-1, keepdims=True) + EPS)
        return y * par_ref[PAR_L2SCALE:PAR_L2SCALE + 1, c * 128:(c + 1) * 128]

    def small(src, g, c):
        cs = slice(c * 128, (c + 1) * 128)
        acc = raw_piece(src, g, c)
        kind = par_ref[PAR_SKIND:PAR_SKIND + 1, cs]
        sp = _softplus(acc + par_ref[PAR_SBIAS:PAR_SBIAS + 1, cs])
        dec = -jnp.exp(par_ref[PAR_SALOG:PAR_SALOG + 1, cs]) * sp
        return jnp.where(kind == 0.0, sp, jnp.where(kind == 1.0, dec, jnp.where(kind == 2.0, _sigmoid(acc), 0.0)))

    rows_mm = 256
    tiles = IN_TN // 128
    piece_fn = {MODE_RAW: raw_piece, MODE_CONV: conv, MODE_CONV_SILU: conv_silu,
                MODE_CONV_SILU_L2: conv_silu_l2, MODE_SMALL: small}

    def project(dst, r):
        rs = slice(r * rows_mm, (r + 1) * rows_mm)
        dst[rs, :] = _dot(h_ref[rs, :], w_ref[...])

    for step in range(nj + 1):
        @pl.when(j == step)
        def _(step=step):
            blk = step - 1
            src, dst = raws[blk % 2], raws[step % 2]
            for r in range(T // rows_mm):
                if step < nj:
                    project(dst, r)
                if blk < 0:
                    continue
                for g in range(r * rows_mm // G, (r + 1) * rows_mm // G):
                    for c in range(tiles):
                        fn = piece_fn[_tile_mode(blk * tiles + c)]
                        o_ref[g * G:(g + 1) * G, c * 128:(c + 1) * 128] = fn(src, g, c)


class Rows:
    def __init__(self, B, Lc, Ll):
        self.B, self.Lc, self.Ll = B, Lc, Ll
        self.NC = B * Lc
        self.R = B * Lc + B * Ll
        assert self.NC % Ll == 0 or Ll % self.NC == 0
        tm = 1024
        while self.NC % tm or Ll % tm:
            tm //= 2
        self.tm = tm

    def mod_index(self, tm):
        nctx = self.NC // tm
        per = self.Ll // tm
        return lambda i: jnp.where(i < nctx, 0, 1 + (i - nctx) // per)


def in_proj(rw, x, nw, mod, w, par):
    R, D = x.shape
    N = w.shape[1]
    tm, tn = rw.tm, IN_TN
    nj = N // tn
    assert N == N_IN_PAD
    mi = rw.mod_index(tm)
    done = lambda j: jnp.maximum(j - 1, 0)
    return pl.pallas_call(
        functools.partial(_in_proj_kernel, nctx_blk=rw.NC // tm),
        grid=(R // tm, nj + 1),
        in_specs=[
            pl.BlockSpec((tm, D), lambda i, j: (i, 0)),
            pl.BlockSpec((1, D), lambda i, j: (0, 0)),
            pl.BlockSpec((1, 8, D), lambda i, j: (mi(i), 0, 0)),
            pl.BlockSpec((D, tn), lambda i, j: (0, jnp.minimum(j, nj - 1))),
            pl.BlockSpec((8, tn), lambda i, j: (0, done(j))),
        ],
        out_specs=pl.BlockSpec((tm, tn), lambda i, j: (i, done(j))),
        out_shape=jax.ShapeDtypeStruct((R, N), F32),
        scratch_shapes=[pltpu.VMEM((tm, D), BF16), pltpu.VMEM((tm, tn), F32), pltpu.VMEM((tm, tn), F32)],
        compiler_params=_cp("arbitrary", "arbitrary"),
        name="in_proj",
    )(x, nw.reshape(1, D), mod, w, par)


def _in_proj_params(hy_conv_w, hy_conv_b, ssm_conv_w, ssm_conv_b, gdn_conv_w, ssm_dt_bias, gdn_dt_bias, gdn_A_log):
    def row(pieces):
        v = jnp.zeros((N_IN_PAD,), F32)
        for off, a in pieces:
            v = lax.dynamic_update_slice(v, a.astype(F32), (off,))
        return v
    z4 = jnp.zeros((4,), F32)
    conv = [row([(C_HY, hy_conv_w[t]), (C_XBC, ssm_conv_w[t]), (C_QKV, gdn_conv_w[t])]) for t in range(3)]
    bias = row([(C_HY, hy_conv_b), (C_XBC, ssm_conv_b)])
    l2s = row([(C_QKV, jnp.full((512,), GDN_DK ** -0.5, F32)), (C_QKV + 512, jnp.ones((512,), F32))])
    sbias = row([(C_SM, jnp.concatenate([ssm_dt_bias.reshape(16), gdn_dt_bias[0], z4, gdn_dt_bias[1], z4]))])
    salog = row([(C_SM + 16, jnp.concatenate([gdn_A_log[0], z4, gdn_A_log[1], z4]))])
    kind = np.full((N_IN_PAD,), 3.0, np.float32)
    kind[C_SM:C_SM + 16] = 0.0
    kind[C_SM + 16:C_SM + 20] = 1.0
    kind[C_SM + 24:C_SM + 28] = 1.0
    kind[C_SM + 20:C_SM + 24] = 2.0
    kind[C_SM + 28:C_SM + 32] = 2.0
    return jnp.stack(conv + [bias, l2s, sbias, salog, jnp.asarray(kind)], axis=0)


def _hy_filter_kernel(z_ref, w1_ref, b1_ref, w2_ref, b2_ref, w3_ref, f0_ref, f1_ref, win_ref, o_ref, h_ref):
    @pl.when(pl.program_id(1) == 0)
    def _():
        h1 = jnp.sin(f0_ref[...] * (_dot(z_ref[...], w1_ref[...], HI) + b1_ref[...]))
        h_ref[...] = jnp.sin(f1_ref[...] * (_dot(h1, w2_ref[...], HI) + b2_ref[...]))

    h = _dot(h_ref[...], w3_ref[...], HI) * win_ref[...]
    tl = h.shape[0]
    row = lax.broadcasted_iota(jnp.int32, (tl, 1), 0) + pl.program_id(0) * tl
    drop = (row == 0) & (pl.program_id(1) % 2 == 1)
    o_ref[...] = jnp.where(drop, 0.0, h).astype(o_ref.dtype)


def hy_features(L):
    t = jnp.linspace(0.0, 1.0, L, dtype=F32)[:, None]
    w = 2.0 * math.pi * jnp.arange(L, dtype=F32)[:, None] / L
    f = jnp.linspace(1e-4, HY_BANDS - 1, HY_BANDS, dtype=F32)[None, :]
    z = jnp.concatenate([t, jnp.cos(f * w), -jnp.sin(f * w)], axis=-1)
    z = jnp.pad(z, ((0, 0), (0, 128 - HY_EMB)))
    min_decay = math.log(HY_TARGET) / HY_LONG_DECAY_PCT
    max_decay = math.log(HY_TARGET) / HY_SHORT_DECAY_PCT
    deltas = jnp.linspace(min_decay, max_decay, HY_WIDTH, dtype=F32)
    window = jnp.exp(-t * jnp.abs(deltas))
    return z, window


def hy_filter(feat, w1, b1, w2, b2, w3, freq):
    z, window = feat
    L = z.shape[0]
    H = HY_HIDDEN
    w1p = jnp.pad(w1, ((0, 128 - HY_EMB), (0, 128 - H)))
    w2p = jnp.pad(w2, ((0, 128 - H), (0, 128 - H)))
    w3p = jnp.pad(w3, ((0, 128 - H), (0, 0)))
    pad1 = lambda v: jnp.pad(v, (0, 128 - H)).reshape(1, 128)
    tl = 256
    full = lambda shape: pl.BlockSpec(shape, lambda i, j: (0, 0))
    return pl.pallas_call(
        _hy_filter_kernel,
        grid=(L // tl, 4),
        in_specs=[
            pl.BlockSpec((tl, 128), lambda i, j: (i, 0)),
            full((128, 128)), full((1, 128)), full((128, 128)), full((1, 128)),
            pl.BlockSpec((128, HY_WIDTH), lambda i, j: (0, j)),
            full((1, 128)), full((1, 128)),
            pl.BlockSpec((tl, HY_WIDTH), lambda i, j: (i, 0)),
        ],
        out_specs=pl.BlockSpec((tl, HY_WIDTH), lambda i, j: (i, j)),
        out_shape=jax.ShapeDtypeStruct((L, 4 * HY_WIDTH), BF16),
        scratch_shapes=[pltpu.VMEM((tl, 128), F32)],
        compiler_params=_cp("arbitrary", "arbitrary"),
        name="hy_filter",
    )(z, w1p, pad1(b1), w2p, pad1(b2), w3p, pad1(freq[0]), pad1(freq[1]), window)


def dft_tables(L):
    N = 2 * L
    f = jnp.arange(L, dtype=jnp.int32)[:, None]
    s = jnp.arange(L, dtype=jnp.int32)[None, :]
    ang = ((f * s) % N).astype(F32) * (2.0 * math.pi / N)
    c, sn = jnp.cos(ang), jnp.sin(ang)
    alt = (1 - 2 * (s % 2)).astype(F32)
    first = f == 0
    fwd = jnp.concatenate([c, jnp.where(first, alt, -sn)], axis=0)
    wgt = jnp.where(first, 1.0, 2.0) / N
    inv = jnp.concatenate([(c * wgt).T, jnp.where(first, alt / N, -sn * wgt).T], axis=1)
    return fwd.astype(BF16), inv.astype(BF16)


def _matmul_kernel(a_ref, b_ref, o_ref):
    o_ref[...] = _dot(a_ref[...], b_ref[...])


def matmul(a, b, tm, tn):
    M, K = a.shape
    N = b.shape[1]
    return pl.pallas_call(
        _matmul_kernel,
        grid=(M // tm, N // tn),
        in_specs=[pl.BlockSpec((tm, K), lambda i, j: (i, 0)), pl.BlockSpec((K, tn), lambda i, j: (0, j))],
        out_specs=pl.BlockSpec((tm, tn), lambda i, j: (i, j)),
        out_shape=jax.ShapeDtypeStruct((M, N), F32),
        compiler_params=_cp("arbitrary", "arbitrary"),
        name="matmul",
    )(a, b)


def _long_conv_kernel(u_ref, g_ref, bias_ref, fr_ref, fi_ref, ic_ref, is_ref,
                      ar0_ref, ar1_ref, ai0_ref, ai1_ref, o_ref, ub_ref, acc_ref):
    f = pl.program_id(1)

    @pl.when(f == 0)
    def _():
        ub_ref[...] = u_ref[...].astype(BF16)
        acc_ref[...] = jnp.zeros_like(acc_ref)

    ub = ub_ref[...]
    ur = _dot(fr_ref[...], ub)
    ui = _dot(fi_ref[...], ub)
    kr = ar0_ref[...] + ar1_ref[...]
    nyq = (lax.broadcasted_iota(jnp.int32, (FREQ_BLK, 1), 0) == 0) & (f == 0)
    ki = jnp.where(nyq, ai0_ref[...] + ai1_ref[...], ai0_ref[...] - ai1_ref[...])
    pr = jnp.where(nyq, ur * kr, ur * kr - ui * ki)
    pi = jnp.where(nyq, ui * ki, ur * ki + ui * kr)
    acc_ref[...] += _dot(ic_ref[...], pr.astype(BF16)) + _dot(is_ref[...], pi.astype(BF16))

    @pl.when(f == pl.num_programs(1) - 1)
    def _():
        u = u_ref[...]
        o_ref[...] = (g_ref[...] * (acc_ref[...] + u * bias_ref[...])).astype(o_ref.dtype)


def long_conv(B, L, u, u_rb0, u_cb, gate, g_rb0, gate_cb, bias, fwd, inv, kspec, order, out_dtype):
    C = HY_WIDTH
    nfb = L // FREQ_BLK
    FB = FREQ_BLK
    return pl.pallas_call(
        _long_conv_kernel,
        grid=(B, nfb),
        in_specs=[
            pl.BlockSpec((L, C), lambda b, f: (u_rb0 + b, u_cb)),
            pl.BlockSpec((L, C), lambda b, f: (g_rb0 + b, gate_cb)),
            pl.BlockSpec((1, C), lambda b, f: (0, 0)),
            pl.BlockSpec((FB, L), lambda b, f: (f, 0)),
            pl.BlockSpec((FB, L), lambda b, f: (nfb + f, 0)),
            pl.BlockSpec((L, FB), lambda b, f: (0, f)),
            pl.BlockSpec((L, FB), lambda b, f: (0, nfb + f)),
            pl.BlockSpec((FB, C), lambda b, f: (f, 2 * order)),
            pl.BlockSpec((FB, C), lambda b, f: (f, 2 * order + 1)),
            pl.BlockSpec((FB, C), lambda b, f: (nfb + f, 2 * order)),
            pl.BlockSpec((FB, C), lambda b, f: (nfb + f, 2 * order + 1)),
        ],
        out_specs=pl.BlockSpec((L, C), lambda b, f: (b, 0)),
        out_shape=jax.ShapeDtypeStruct((B * L, C), out_dtype),
        scratch_shapes=[pltpu.VMEM((L, C), BF16), pltpu.VMEM((L, C), F32)],
        compiler_params=_cp("arbitrary", "arbitrary"),
        name="long_conv",
    )(u, gate, bias.reshape(1, C), fwd, fwd, inv, inv, kspec, kspec, kspec, kspec)


def _scan_blocks(rw, rows):
    nbc, nbl, base = rw.Lc // rows, rw.Ll // rows, rw.NC // rows

    def make(d):
        def f(b, s):
            jc = s if d == 0 else nbc - 1 - s
            jl = (s - nbc) if d == 0 else nbl - 1 - (s - nbc)
            return jnp.where(s < nbc, b * nbc + jc, base + b * nbl + jl)
        return f

    return [make(0), make(1)], nbc + nbl


def _expand_lanes(x, base, n, width):
    rows = x.shape[0]
    per = 128 // width
    lane = lax.broadcasted_iota(jnp.int32, (rows, 128), 1)
    tiles = []
    for t in range(n // per):
        c0 = base + t * per
        tile = jnp.broadcast_to(x[:, c0:c0 + 1], (rows, 128))
        for i in range(1, per):
            tile = jnp.where(lane >= i * width, jnp.broadcast_to(x[:, c0 + i:c0 + i + 1], (rows, 128)), tile)
        tiles.append(tile)
    return jnp.concatenate(tiles, axis=1)


def _ssd_kernel(xf, bf, cf, smf, dtf, xb, bb, cb_, smb, dtb, alx_ref, alc_ref, of_ref, ob_ref, h_ref):
    Q = SSM_CHUNK
    GW = SSM_GW

    @pl.when(pl.program_id(1) == 0)
    def _():
        h_ref[...] = jnp.zeros_like(h_ref)

    row = lax.broadcasted_iota(jnp.int32, (Q, Q), 0)
    col = lax.broadcasted_iota(jnp.int32, (Q, Q), 1)
    lane_head = lax.broadcasted_iota(jnp.int32, (Q, GW), 1) // SSM_HEAD_DIM
    dirs = ((xf, bf, cf, smf, dtf, of_ref), (xb, bb, cb_, smb, dtb, ob_ref))
    jobs = []
    for d in range(2):
        x_ref, b_ref, c_ref, sm_ref, dt_ref, o_ref = dirs[d]
        keep = (col <= row) if d == 0 else (col >= row)
        tri = keep.astype(BF16)
        tri_t = ((row <= col) if d == 0 else (row >= col)).astype(BF16)
        sm = sm_ref[...]
        a_x = -jnp.exp(alx_ref[d])
        dtx = _expand_lanes(sm, 8 * d, SSM_HEADS, SSM_HEAD_DIM)
        cumx = _expand_lanes(_dot_01_lhs(tri, sm), 8 * d, SSM_HEADS, SSM_HEAD_DIM) * a_x
        cumr = _dot_01_rhs(dt_ref[0], tri_t) * (-jnp.exp(alc_ref[d]))
        last = Q - 1 if d == 0 else 0
        totx = cumx[last:last + 1, :]
        xd = x_ref[...] * dtx
        xdw = xd * jnp.exp(totx - cumx)
        ecum = jnp.exp(cumx)
        for g in range(SSM_GROUPS):
            gs = slice(g * GW, (g + 1) * GW)
            jobs.append(dict(d=d, g=g, gs=gs, keep=keep, cumx=cumx, cumr=cumr, o_ref=o_ref,
                             bg=b_ref[:, g * SSM_STATE:(g + 1) * SSM_STATE].astype(BF16),
                             cg=c_ref[:, g * SSM_STATE:(g + 1) * SSM_STATE].astype(BF16),
                             xdg=xd[:, gs], xdw=xdw[:, gs].astype(BF16), ecum=ecum[:, gs],
                             etot=jnp.exp(totx[:, gs])))
    for j in jobs:
        j["cb"] = _dot_nt(j["cg"], j["bg"])
        j["h"] = h_ref[j["d"], j["g"]]
    for j in jobs:
        ms, xs = [], []
        for e4 in range(SSM_HPG):
            e = j["g"] * SSM_HPG + e4
            diff = j["cumx"][:, e * SSM_HEAD_DIM:e * SSM_HEAD_DIM + 1] - j["cumr"][e:e + 1, :]
            ms.append((j["cb"] * jnp.where(j["keep"], jnp.exp(diff), 0.0)).astype(BF16))
            xs.append(jnp.where(lane_head == e4, j["xdg"], 0.0).astype(BF16))
        yd = _dot(jnp.concatenate(ms, axis=1), jnp.concatenate(xs, axis=0))
        y_off = _dot(j["cg"], j["h"].astype(BF16)) * j["ecum"]
        j["o_ref"][:, j["gs"]] = yd + y_off
    for j in jobs:
        h_ref[j["d"], j["g"]] = j["h"] * j["etot"] + _dot_tn(j["bg"], j["xdw"])


def ssd_scan(rw, p, dtT, alx, alc):
    Q = SSM_CHUNK
    blks, nsteps = _scan_blocks(rw, Q)
    R = p.shape[0]
    in_specs = []
    for d in range(2):
        f = blks[d]
        in_specs += [
            pl.BlockSpec((Q, 512), lambda b, s, f=f: (f(b, s), C_XBC // 512)),
            pl.BlockSpec((Q, 256), lambda b, s, f=f: (f(b, s), C_XBC // 256 + 2)),
            pl.BlockSpec((Q, 256), lambda b, s, f=f: (f(b, s), C_XBC // 256 + 3)),
            pl.BlockSpec((Q, 128), lambda b, s, f=f: (f(b, s), C_SM // 128)),
            pl.BlockSpec((1, 8, Q), lambda b, s, f=f, d=d: (d, 0, f(b, s))),
        ]
    in_specs += [pl.BlockSpec((2, 1, 512), lambda b, s: (0, 0, 0)), pl.BlockSpec((2, 8, 1), lambda b, s: (0, 0, 0))]
    ops = (p, p, p, p, dtT)
    return pl.pallas_call(
        _ssd_kernel,
        grid=(rw.B, nsteps),
        in_specs=in_specs,
        out_specs=[pl.BlockSpec((Q, 512), lambda b, s, f=blks[d]: (f(b, s), 0)) for d in range(2)],
        out_shape=[jax.ShapeDtypeStruct((R, 512), F32)] * 2,
        scratch_shapes=[pltpu.VMEM((2, SSM_GROUPS, SSM_STATE, SSM_GW), F32)],
        compiler_params=_cp("arbitrary", "arbitrary"),
        name="ssd_scan",
    )(*ops, *ops, alx, alc)


def _ssm_post_kernel(yf_ref, yb_ref, x_ref, z_ref, dx_ref, nw_ref, o_ref):
    y = yf_ref[...] + yb_ref[...] + x_ref[...] * dx_ref[...]
    y = y * _silu(z_ref[...])
    parts = []
    for g in range(SSM_GROUPS):
        yg = y[:, g * SSM_GW:(g + 1) * SSM_GW]
        ms = jnp.mean(yg * yg, axis=-1, keepdims=True)
        parts.append(yg * lax.rsqrt(ms + EPS))
    o_ref[...] = (jnp.concatenate(parts, axis=1) * nw_ref[...]).astype(o_ref.dtype)


def ssm_post(rw, y_f, y_b, p, dx, nw):
    R = p.shape[0]
    tm = rw.tm
    vec = pl.BlockSpec((1, 512), lambda i: (0, 0))
    return pl.pallas_call(
        _ssm_post_kernel,
        grid=(R // tm,),
        in_specs=[
            pl.BlockSpec((tm, 512), lambda i: (i, 0)),
            pl.BlockSpec((tm, 512), lambda i: (i, 0)),
            pl.BlockSpec((tm, 512), lambda i: (i, C_XBC // 512)),
            pl.BlockSpec((tm, 512), lambda i: (i, C_Z // 512)),
            vec, vec,
        ],
        out_specs=pl.BlockSpec((tm, 512), lambda i: (i, 0)),
        out_shape=jax.ShapeDtypeStruct((R, 512), BF16),
        compiler_params=_cp("arbitrary"),
        name="ssm_post",
    )(y_f, y_b, p, p, dx, nw)


def _split3(x):
    x1 = x.astype(BF16)
    r = x - x1.astype(F32)
    x2 = r.astype(BF16)
    x3 = (r - x2.astype(F32)).astype(BF16)
    return x1, x2, x3


def _dot_01_lhs(m01, x):
    x1, x2, x3 = _split3(x)
    return _dot(m01, x1) + _dot(m01, x2) + _dot(m01, x3)


def _dot_01_rhs(x, m01):
    x1, x2, x3 = _split3(x)
    return _dot(x1, m01) + _dot(x2, m01) + _dot(x3, m01)


GDN_ROWS = 256


def _gdn_prep_kernel(q_ref, k_ref, v_ref, sm_ref, gT_ref, u_ref, w_ref, qg_ref, kd_ref, qk_ref, egl_ref):
    C = GDN_CHUNK
    row = lax.broadcasted_iota(jnp.int32, (C, C), 0)
    col = lax.broadcasted_iota(jnp.int32, (C, C), 1)
    jobs = []
    levels = []
    for d in range(2):
        keep = (col <= row) if d == 0 else (col >= row)
        late, early = (row, col) if d == 0 else (col, row)
        levels.append([(((row ^ col) >> (j + 1)) == 0) & ((late & (1 << j)) != 0) & ((early & (1 << j)) == 0)
                       for j in range(6)])
        tri = keep.astype(BF16)
        tri_t = ((row <= col) if d == 0 else (row >= col)).astype(BF16)
        last = C - 1 if d == 0 else 0
        for c in range(GDN_ROWS // C):
            rows = slice(c * C, (c + 1) * C)
            smc = sm_ref[rows, :]
            cums = _dot_01_lhs(tri, smc)
            cumr = _dot_01_rhs(gT_ref[c, 8 * d:8 * d + 8, :], tri_t)
            tot = cums[last:last + 1, :]
            for h in range(GDN_HEADS):
                lg = 16 + 8 * d + h
                jobs.append(dict(d=d, c=c, h=h, rows=rows, hs=slice(h * 128, (h + 1) * 128), keep=keep,
                                 gc=cums[:, lg:lg + 1], beta=smc[:, lg + 4:lg + 5],
                                 gl=tot[:, lg:lg + 1], gr=cumr[h:h + 1, :]))
    for j in jobs:
        q = q_ref[j["rows"], j["hs"]]
        k = k_ref[j["rows"], j["hs"]]
        j["dec"] = jnp.where(j["keep"], jnp.exp(j["gc"] - j["gr"]), 0.0)
        kb = k * j["beta"]
        both = _dot_nt(jnp.concatenate([kb, q], axis=0).astype(BF16), k.astype(BF16))
        j["a"] = both[:C] * j["dec"]
        j["n"] = -jnp.where(levels[j["d"]][0], j["a"], 0.0)
        qk_ref[j["d"], j["c"], j["h"]] = (both[C:] * j["dec"]).astype(BF16)
    for lev in range(1, 6):
        for j in jobs:
            l = jnp.where(levels[j["d"]][lev], j["a"], 0.0)
            j["y"] = l + _dot(l.astype(BF16), j["n"].astype(BF16))
        for j in jobs:
            j["n"] = j["n"] - j["y"] - _dot(j["n"].astype(BF16), j["y"].astype(BF16))
    for j in jobs:
        d, rows, hs, gc, gl, beta = j["d"], j["rows"], j["hs"], j["gc"], j["gl"], j["beta"]
        q = q_ref[rows, hs]
        k = k_ref[rows, hs]
        eg = jnp.exp(gc)
        rhs = jnp.concatenate([v_ref[rows, hs] * beta, k * beta * eg], axis=1)
        sol = rhs + _dot(j["n"].astype(BF16), rhs.astype(BF16))
        u_ref[d, rows, hs] = sol[:, :GDN_DV]
        w_ref[d, rows, hs] = sol[:, GDN_DV:].astype(BF16)
        qg_ref[d, rows, hs] = (q * eg).astype(BF16)
        kd_ref[d, rows, hs] = (k * jnp.exp(gl - gc)).astype(BF16)
        egl_ref[d, j["c"], :, hs] = jnp.broadcast_to(jnp.exp(gl), (8, 128))


def gdn_prep(p, gT):
    R = p.shape[0]
    T, C = GDN_ROWS, GDN_CHUNK
    nc = T // C
    col = lambda k: pl.BlockSpec((T, 512), lambda i: (i, C_QKV // 512 + k))
    dirrow = pl.BlockSpec((2, T, 512), lambda i: (0, i, 0))
    return pl.pallas_call(
        _gdn_prep_kernel,
        grid=(R // T,),
        in_specs=[col(0), col(1), col(2),
                  pl.BlockSpec((T, 128), lambda i: (i, C_SM // 128)),
                  pl.BlockSpec((nc, 16, C), lambda i: (i, 0, 0))],
        out_specs=[dirrow, dirrow, dirrow, dirrow,
                   pl.BlockSpec((2, nc, GDN_HEADS, C, C), lambda i: (0, i, 0, 0, 0)),
                   pl.BlockSpec((2, nc, 8, 512), lambda i: (0, i, 0, 0))],
        out_shape=[jax.ShapeDtypeStruct((2, R, 512), F32),
                   jax.ShapeDtypeStruct((2, R, 512), BF16),
                   jax.ShapeDtypeStruct((2, R, 512), BF16),
                   jax.ShapeDtypeStruct((2, R, 512), BF16),
                   jax.ShapeDtypeStruct((2, R // C, GDN_HEADS, C, C), BF16),
                   jax.ShapeDtypeStruct((2, R // C, 8, 512), F32)],
        compiler_params=_cp("arbitrary"),
        name="gdn_prep",
    )(p, p, p, p, gT)


def _gdn_scan_kernel(uf, wf, qgf, kdf, qkf, eglf, ub, wb, qgb, kdb, qkb, eglb, of_ref, ob_ref, s_ref):
    C = GDN_CHUNK
    nch = GDN_ROWS // C

    @pl.when(pl.program_id(1) == 0)
    def _():
        s_ref[...] = jnp.zeros_like(s_ref)

    dirs = ((uf, wf, qgf, kdf, qkf, eglf, of_ref), (ub, wb, qgb, kdb, qkb, eglb, ob_ref))
    chains = [(d, h) for d in range(2) for h in range(GDN_HEADS)]
    S = {ch: s_ref[ch[0], ch[1]] for ch in chains}
    for i in range(nch):
        Sb, vnb, rows_of, c_of = {}, {}, {}, {}
        for d, h in chains:
            c_of[d] = i if d == 0 else nch - 1 - i
            rows_of[d] = slice(c_of[d] * C, (c_of[d] + 1) * C)
        for d, h in chains:
            hs = slice(h * 128, (h + 1) * 128)
            Sb[d, h] = S[d, h].astype(BF16)
            v_new = dirs[d][0][0, rows_of[d], hs] - _dot(dirs[d][1][0, rows_of[d], hs], Sb[d, h])
            vnb[d, h] = v_new.astype(BF16)
        for d, h in chains:
            hs = slice(h * 128, (h + 1) * 128)
            u_ref, w_ref, qg_ref, kd_ref, qk_ref, egl_ref, o_ref = dirs[d]
            S[d, h] = S[d, h] * egl_ref[0, c_of[d], 0:1, hs] + _dot_tn(kd_ref[0, rows_of[d], hs], vnb[d, h])
        for d, h in chains:
            hs = slice(h * 128, (h + 1) * 128)
            u_ref, w_ref, qg_ref, kd_ref, qk_ref, egl_ref, o_ref = dirs[d]
            o_ref[rows_of[d], hs] = (_dot(qg_ref[0, rows_of[d], hs], Sb[d, h])
                                     + _dot(qk_ref[0, c_of[d], h], vnb[d, h]))
    for ch in chains:
        s_ref[ch[0], ch[1]] = S[ch]


def gdn_scan(rw, u, w, qg, kd, qk, egl):
    T, C = GDN_ROWS, GDN_CHUNK
    nc = T // C
    R = u.shape[1]
    nbc, nbl, base = rw.Lc // T, rw.Ll // T, rw.NC // T

    def blk(d):
        def f(b, s):
            jc = s if d == 0 else nbc - 1 - s
            jl = (s - nbc) if d == 0 else nbl - 1 - (s - nbc)
            return jnp.where(s < nbc, b * nbc + jc, base + b * nbl + jl)
        return f

    in_specs = []
    for d in range(2):
        f = blk(d)
        rowspec = pl.BlockSpec((1, T, 512), lambda b, s, f=f, d=d: (d, f(b, s), 0))
        in_specs += [rowspec, rowspec, rowspec, rowspec,
                     pl.BlockSpec((1, nc, GDN_HEADS, C, C), lambda b, s, f=f, d=d: (d, f(b, s), 0, 0, 0)),
                     pl.BlockSpec((1, nc, 8, 512), lambda b, s, f=f, d=d: (d, f(b, s), 0, 0))]
    out_specs = [pl.BlockSpec((T, 512), lambda b, s, f=blk(d): (f(b, s), 0)) for d in range(2)]
    ops = (u, w, qg, kd, qk, egl)
    return pl.pallas_call(
        _gdn_scan_kernel,
        grid=(rw.B, nbc + nbl),
        in_specs=in_specs,
        out_specs=out_specs,
        out_shape=[jax.ShapeDtypeStruct((R, 512), F32)] * 2,
        scratch_shapes=[pltpu.VMEM((2, GDN_HEADS, GDN_DK, GDN_DV), F32)],
        compiler_params=_cp("arbitrary", "arbitrary"),
        name="gdn_scan",
    )(*ops, *ops)


def _gdn_post_kernel(of_ref, ob_ref, g_ref, nw_ref, o_ref):
    o = of_ref[...] + ob_ref[...]
    parts = []
    for h in range(GDN_HEADS):
        oh = o[:, h * 128:(h + 1) * 128]
        ms = jnp.mean(oh * oh, axis=-1, keepdims=True)
        parts.append(oh * lax.rsqrt(ms + EPS))
    o_ref[...] = (jnp.concatenate(parts, axis=1) * nw_ref[...] * _silu(g_ref[...])).astype(o_ref.dtype)


def gdn_post(rw, o_f, o_b, p, nw):
    R = p.shape[0]
    tm = rw.tm
    return pl.pallas_call(
        _gdn_post_kernel,
        grid=(R // tm,),
        in_specs=[
            pl.BlockSpec((tm, 512), lambda i: (i, 0)),
            pl.BlockSpec((tm, 512), lambda i: (i, 0)),
            pl.BlockSpec((tm, 512), lambda i: (i, C_GG // 512)),
            pl.BlockSpec((1, 512), lambda i: (0, 0)),
        ],
        out_specs=pl.BlockSpec((tm, 512), lambda i: (i, 0)),
        out_shape=jax.ShapeDtypeStruct((R, 512), BF16),
        compiler_params=_cp("arbitrary"),
        name="gdn_post",
    )(o_f, o_b, p, nw)


def _merge_kernel(yh_ref, ys_ref, yg_ref, g0_ref, g1_ref, g2_ref, w0_ref, w1_ref, w2_ref, wo_ref,
                  x_ref, mod_ref, o_ref):
    m = (_sigmoid(g0_ref[...]) * _dot(yh_ref[...], w0_ref[...])
         + _sigmoid(g1_ref[...]) * _dot(ys_ref[...], w1_ref[...])
         + _sigmoid(g2_ref[...]) * _dot(yg_ref[...], w2_ref[...]))
    o_ref[...] = x_ref[...] + mod_ref[0, 2:3, :] * _dot(m.astype(BF16), wo_ref[...])


def merge(rw, yh, ys, yg, p, w0, w1, w2, wo, x, mod):
    R, D = x.shape
    tm = min(rw.tm, 512)
    mi = rw.mod_index(tm)
    yspec = pl.BlockSpec((tm, 512), lambda i: (i, 0))
    gspec = lambda k: pl.BlockSpec((tm, D), lambda i: (i, C_GATE // D + k))
    wspec = pl.BlockSpec((512, D), lambda i: (0, 0))
    return pl.pallas_call(
        _merge_kernel,
        grid=(R // tm,),
        in_specs=[yspec, yspec, yspec, gspec(0), gspec(1), gspec(2), wspec, wspec, wspec,
                  pl.BlockSpec((D, D), lambda i: (0, 0)),
                  pl.BlockSpec((tm, D), lambda i: (i, 0)),
                  pl.BlockSpec((1, 8, D), lambda i: (mi(i), 0, 0))],
        out_specs=pl.BlockSpec((tm, D), lambda i: (i, 0)),
        out_shape=jax.ShapeDtypeStruct((R, D), F32),
        compiler_params=_cp("arbitrary"),
        name="merge",
    )(yh, ys, yg, p, p, p, w0, w1, w2, wo, x, mod)


def _swiglu_up_kernel(x_ref, nw_ref, mod_ref, wg_ref, wu_ref, o_ref, h_ref):
    @pl.when(pl.program_id(1) == 0)
    def _():
        h = _norm_mod(x_ref[...], nw_ref[...], mod_ref[0, 4:5, :], mod_ref[0, 3:4, :])
        h_ref[...] = h.astype(BF16)

    h = h_ref[...]
    g = _dot(h, wg_ref[...])
    u = _dot(h, wu_ref[...])
    o_ref[...] = (_silu(g) * u).astype(o_ref.dtype)


def swiglu_up(rw, x, nw, mod, wgu):
    R, D = x.shape
    tm = rw.tm
    tn = D_FF // 2
    nj = D_FF // tn
    mi = rw.mod_index(tm)
    return pl.pallas_call(
        _swiglu_up_kernel,
        grid=(R // tm, nj),
        in_specs=[
            pl.BlockSpec((tm, D), lambda i, j: (i, 0)),
            pl.BlockSpec((1, D), lambda i, j: (0, 0)),
            pl.BlockSpec((1, 8, D), lambda i, j: (mi(i), 0, 0)),
            pl.BlockSpec((D, tn), lambda i, j: (0, j)),
            pl.BlockSpec((D, tn), lambda i, j: (0, nj + j)),
        ],
        out_specs=pl.BlockSpec((tm, tn), lambda i, j: (i, j)),
        out_shape=jax.ShapeDtypeStruct((R, D_FF), BF16),
        scratch_shapes=[pltpu.VMEM((tm, D), BF16)],
        compiler_params=_cp("arbitrary", "arbitrary"),
        name="swiglu_up",
    )(x, nw.reshape(1, D), mod, wgu, wgu)


def _swiglu_down_kernel(a_ref, w_ref, x_ref, mod_ref, o_ref):
    o_ref[...] = x_ref[...] + mod_ref[0, 5:6, :] * _dot(a_ref[...], w_ref[...])


def swiglu_down(rw, a, w, x, mod):
    R, D = x.shape
    tm = min(rw.tm, 512)
    mi = rw.mod_index(tm)
    return pl.pallas_call(
        _swiglu_down_kernel,
        grid=(R // tm,),
        in_specs=[
            pl.BlockSpec((tm, D_FF), lambda i: (i, 0)),
            pl.BlockSpec((D_FF, D), lambda i: (0, 0)),
            pl.BlockSpec((tm, D), lambda i: (i, 0)),
            pl.BlockSpec((1, 8, D), lambda i: (mi(i), 0, 0)),
        ],
        out_specs=pl.BlockSpec((tm, D), lambda i: (i, 0)),
        out_shape=jax.ShapeDtypeStruct((R, D), F32),
        compiler_params=_cp("arbitrary"),
        name="swiglu_down",
    )(a, w, x, mod)


def _final_norm_kernel(x_ref, w_ref, o_ref):
    x = x_ref[...]
    ms = jnp.mean(x * x, axis=-1, keepdims=True)
    o_ref[...] = x * lax.rsqrt(ms + EPS) * w_ref[...]


def final_norm(rw, x, w):
    D = x.shape[1]
    tm = rw.tm
    n0 = rw.NC // tm
    nl = rw.B * rw.Ll
    return pl.pallas_call(
        _final_norm_kernel,
        grid=(nl // tm,),
        in_specs=[pl.BlockSpec((tm, D), lambda i: (n0 + i, 0)), pl.BlockSpec((1, D), lambda i: (0, 0))],
        out_specs=pl.BlockSpec((tm, D), lambda i: (i, 0)),
        out_shape=jax.ShapeDtypeStruct((nl, D), F32),
        compiler_params=_cp("arbitrary"),
        name="final_norm",
    )(x, w.reshape(1, D))


def _regroup_w_in(w_in):
    o_dt = 3072
    o_gdn = 3088
    o_a = o_gdn + 2048
    o_b = o_a + 8
    o_gate = o_gdn + 2064
    pieces = [
        w_in[..., 0:3072],
        w_in[..., o_gdn:o_gdn + 2048],
        w_in[..., o_gate:o_gate + 3072],
        w_in[..., o_dt:o_dt + 16],
        w_in[..., o_a:o_a + 4], w_in[..., o_b:o_b + 4],
        w_in[..., o_a + 4:o_a + 8], w_in[..., o_b + 4:o_b + 8],
        jnp.zeros(w_in.shape[:-1] + (N_IN_PAD - C_SM - 32,), w_in.dtype),
    ]
    return jnp.concatenate(pieces, axis=-1).astype(BF16)


def kernel(x, c, ctx, c_ctx, w_ada, b_ada, norm1_w, norm2_w, w_in, hy_conv_w, hy_conv_b, hy_w1, hy_b1, hy_w2, hy_b2, hy_w3, hy_freq, hy_bias, ssm_conv_w, ssm_conv_b, ssm_dt_bias, ssm_A_log, ssm_D, ssm_norm_w, gdn_conv_w, gdn_dt_bias, gdn_A_log, gdn_norm_w, w_hy_out, w_ssm_out, w_gdn_out, w_out, w_gate_up, w_down, final_norm_w):
    B, Ll, D = x.shape
    Lc = ctx.shape[1]
    depth = w_ada.shape[0]
    assert Lc == CONV_ROWS and D == D_MODEL and B <= 15
    rw = Rows(B, Lc, Ll)
    R, NC = rw.R, rw.NC

    xa = jnp.concatenate([ctx.reshape(B * Lc, D), x.reshape(B * Ll, D)], axis=0)

    svec = jnp.concatenate([c_ctx[None, :], c, jnp.zeros((15 - B, D), F32)], axis=0)
    mod_all = ada_modulation(svec, w_ada, b_ada)
    mod_all = jnp.pad(mod_all.reshape(depth, 16, 6, D), ((0, 0), (0, 0), (0, 2), (0, 0)))

    w_in_r = _regroup_w_in(w_in)
    fwd_l, inv_l = dft_tables(Ll)
    fwd_c, inv_c = dft_tables(Lc)
    feat_l, feat_c = hy_features(Ll), hy_features(Lc)

    for l in range(depth):
        mod = mod_all[l]
        par = _in_proj_params(hy_conv_w[l], hy_conv_b[l], ssm_conv_w[l], ssm_conv_b[l], gdn_conv_w[l],
                              ssm_dt_bias[l], gdn_dt_bias[l], gdn_A_log[l])
        p = in_proj(rw, xa, norm1_w[l], mod, w_in_r[l], par)

        sm32_t = p[:, C_SM:C_SM + 32].T
        dt_t = sm32_t[:16].reshape(2, 8, R)
        g_t = sm32_t[16:32].reshape(16, R // GDN_CHUNK, GDN_CHUNK).transpose(1, 0, 2)

        alx = jnp.repeat(ssm_A_log[l], SSM_HEAD_DIM, axis=-1).reshape(2, 1, 512)
        alc = ssm_A_log[l].reshape(2, 8, 1)
        y_f, y_b = ssd_scan(rw, p, dt_t, alx, alc)
        dx = jnp.repeat(ssm_D[l], SSM_HEAD_DIM).reshape(1, 512)
        y_ssm = ssm_post(rw, y_f, y_b, p, dx, ssm_norm_w[l].reshape(1, 512))

        o_f, o_b = gdn_scan(rw, *gdn_prep(p, g_t))
        y_gdn = gdn_post(rw, o_f, o_b, p, jnp.tile(gdn_norm_w[l], GDN_HEADS).reshape(1, 512))

        hyu = p
        parts = []
        for (Bn, L, blk0, fwd, inv, feat) in ((B, Lc, 0, fwd_c, inv_c, feat_c),
                                              (B, Ll, NC // Ll, fwd_l, inv_l, feat_l)):
            if NC % L:
                raise ValueError("latent length must divide the context row count")
            filt = hy_filter(feat, hy_w1[l], hy_b1[l], hy_w2[l], hy_b2[l], hy_w3[l], hy_freq[l])
            kspec = matmul(fwd, filt, min(512, 2 * L), 512)
            z1 = long_conv(Bn, L, hyu, blk0, 0, hyu, blk0, 1, hy_bias[l, 0], fwd, inv, kspec, 0, F32)
            yy = long_conv(Bn, L, z1, 0, 0, hyu, blk0, 2, hy_bias[l, 1], fwd, inv, kspec, 1, BF16)
            parts.append(yy)
        y_hy = jnp.concatenate(parts, axis=0)

        xa = merge(rw, y_hy, y_ssm, y_gdn, p, w_hy_out[l].astype(BF16), w_ssm_out[l].astype(BF16),
                   w_gdn_out[l].astype(BF16), w_out[l].astype(BF16), xa, mod)
        act = swiglu_up(rw, xa, norm2_w[l], mod, w_gate_up[l].astype(BF16))
        xa = swiglu_down(rw, act, w_down[l].astype(BF16), xa, mod)

    out = final_norm(rw, xa, final_norm_w)
    return out.reshape(B, Ll, D)
```

```python
import functools
import math

import jax
import jax.numpy as jnp
import numpy as np
from jax import lax
from jax.experimental import pallas as pl
from jax.experimental.pallas import tpu as pltpu

F32 = jnp.float32
BF16 = jnp.bfloat16
HI = lax.Precision.HIGHEST

EPS = 1e-6
D_MODEL = 1024
GRID_W = 64

HY_WIDTH = 512
HY_BANDS = 16
HY_EMB = 1 + 2 * HY_BANDS
HY_HIDDEN = 64
HY_SHORT_DECAY_PCT = 0.3
HY_LONG_DECAY_PCT = 1.5
HY_TARGET = 1e-2

SSM_HEADS = 8
SSM_HEAD_DIM = 64
SSM_WIDTH = 512
SSM_GROUPS = 2
SSM_HPG = 4
SSM_STATE = 128
SSM_CHUNK = 128
SSM_GW = SSM_HPG * SSM_HEAD_DIM

GDN_HEADS = 4
GDN_DK = 128
GDN_DV = 128
GDN_CHUNK = 64

D_FF = 2816

C_HY = 0
C_Z = 1536
C_XBC = 2048
C_QKV = 3072
C_GG = 4608
C_GATE = 5120
C_SM = 8192

CONV_ROWS = 256
FREQ_BLK = 256

VMEM_LIMIT = 56 * 1024 * 1024


def _cp(*sem):
    return pltpu.CompilerParams(dimension_semantics=sem, vmem_limit_bytes=VMEM_LIMIT)


def _sigmoid(x):
    return 1.0 / (1.0 + jnp.exp(-x))


def _silu(x):
    return x * _sigmoid(x)


def _softplus(x):
    return jnp.maximum(x, 0.0) + jnp.log1p(jnp.exp(-jnp.abs(x)))


def _dot(a, b, precision=None):
    return jnp.dot(a, b, precision=precision, preferred_element_type=F32)


def _dot_nt(a, b):
    return lax.dot_general(a, b, (((1,), (1,)), ((), ())), preferred_element_type=F32)


def _dot_tn(a, b):
    return lax.dot_general(a, b, (((0,), (0,)), ((), ())), preferred_element_type=F32)


def _ada_kernel(s_ref, w_ref, b_ref, o_ref):
    s = _silu(s_ref[...])
    o_ref[0] = _dot(s, w_ref[0], HI) + b_ref[0]


def ada_modulation(svec, w_ada, b_ada):
    depth = w_ada.shape[0]
    D = D_MODEL
    return pl.pallas_call(
        _ada_kernel,
        grid=(depth, 6),
        in_specs=[
            pl.BlockSpec((16, D), lambda l, j: (0, 0)),
            pl.BlockSpec((1, D, D), lambda l, j: (l, 0, j)),
            pl.BlockSpec((1, 1, D), lambda l, j: (l, 0, j)),
        ],
        out_specs=pl.BlockSpec((1, 16, D), lambda l, j: (l, 0, j)),
        out_shape=jax.ShapeDtypeStruct((depth, 16, 6 * D), F32),
        compiler_params=_cp("arbitrary", "arbitrary"),
        name="ada",
    )(svec, w_ada, b_ada.reshape(depth, 1, 6 * D))


def _norm_mod(x, nw, scale, shift):
    ms = jnp.mean(x * x, axis=-1, keepdims=True)
    return (x * lax.rsqrt(ms + EPS) * nw) * (1.0 + scale) + shift


N_IN_PAD = C_SM + 128
IN_TN = N_IN_PAD // 5
MODE_RAW, MODE_CONV, MODE_CONV_SILU, MODE_CONV_SILU_L2, MODE_SMALL = range(5)


def _tile_mode(tile):
    col = tile * 128
    if col < C_Z:
        return MODE_CONV
    if col < C_XBC:
        return MODE_RAW
    if col < C_QKV:
        return MODE_CONV_SILU
    if col < C_QKV + 1024:
        return MODE_CONV_SILU_L2
    if col < C_GG:
        return MODE_CONV_SILU
    if col < C_SM:
        return MODE_RAW
    return MODE_SMALL
PAR_W0, PAR_W1, PAR_W2, PAR_BIAS, PAR_L2SCALE, PAR_SBIAS, PAR_SALOG, PAR_SKIND = range(8)


def _in_proj_kernel(x_ref, nw_ref, mod_ref, w_ref, par_ref, o_ref, sm_ref, h_ref, raw0_ref, raw1_ref, *, nctx_blk):
    j = pl.program_id(1)
    nj = N_IN_PAD // IN_TN
    raws = (raw0_ref, raw1_ref)

    @pl.when(j == 0)
    def _():
        h = _norm_mod(x_ref[...], nw_ref[...], mod_ref[0, 1:2, :], mod_ref[0, 0:1, :])
        h_ref[...] = h.astype(BF16)

    T = h_ref.shape[0]
    G = GRID_W
    per_ctx = CONV_ROWS // G
    is_latent = pl.program_id(0) >= nctx_blk
    sub = lax.broadcasted_iota(jnp.int32, (8, 128), 0)

    def raw_piece(src, g, c):
        return src[g * G:(g + 1) * G, c * 128:(c + 1) * 128]

    def conv(src, g, c):
        cs = slice(c * 128, (c + 1) * 128)
        x = raw_piece(src, g, c)
        zero = jnp.zeros((1, 128), F32)
        before = zero if g % per_ctx == 0 else jnp.where(is_latent, 0.0, src[g * G - 1:g * G, cs])
        after = zero if g % per_ctx == per_ctx - 1 else jnp.where(is_latent, 0.0, src[(g + 1) * G:(g + 1) * G + 1, cs])
        rp = pltpu.roll(x, 1, 0)
        rn = pltpu.roll(x, G - 1, 0)
        prev = jnp.concatenate([jnp.where(sub == 0, before, rp[0:8]), rp[8:]], axis=0)
        nxt = jnp.concatenate([rn[:G - 8], jnp.where(sub == 7, after, rn[G - 8:])], axis=0)
        return (prev * par_ref[PAR_W0:PAR_W0 + 1, cs] + x * par_ref[PAR_W1:PAR_W1 + 1, cs]
                + nxt * par_ref[PAR_W2:PAR_W2 + 1, cs] + par_ref[PAR_BIAS:PAR_BIAS + 1, cs])

    def conv_silu(src, g, c):
        return _silu(conv(src, g, c))

    def conv_silu_l2(src, g, c):
        y = _silu(conv(src, g, c))
        y = y * lax.rsqrt(jnp.sum(y * y, axis=-1, keepdims=True) + EPS)
        return y * par_ref[PAR_L2SCALE:PAR_L2SCALE + 1, c * 128:(c + 1) * 128]

    def small(src, g, c):
        cs = slice(c * 128, (c + 1) * 128)
        acc = raw_piece(src, g, c)
        kind = par_ref[PAR_SKIND:PAR_SKIND + 1, cs]
        sp = _softplus(acc + par_ref[PAR_SBIAS:PAR_SBIAS + 1, cs])
        dec = -jnp.exp(par_ref[PAR_SALOG:PAR_SALOG + 1, cs]) * sp
        return jnp.where(kind == 0.0, sp, jnp.where(kind == 1.0, dec, jnp.where(kind == 2.0, _sigmoid(acc), 0.0)))

    rows_mm = 256
    tiles = IN_TN // 128
    piece_fn = {MODE_RAW: raw_piece, MODE_CONV: conv, MODE_CONV_SILU: conv_silu,
                MODE_CONV_SILU_L2: conv_silu_l2, MODE_SMALL: small}

    def project(dst, r):
        rs = slice(r * rows_mm, (r + 1) * rows_mm)
        dst[rs, :] = _dot(h_ref[rs, :], w_ref[...])

    for step in range(nj + 1):
        @pl.when(j == step)
        def _(step=step):
            blk = step - 1
            src, dst = raws[blk % 2], raws[step % 2]
            for r in range(T // rows_mm):
                if step < nj:
                    project(dst, r)
                if blk < 0:
                    continue
                for g in range(r * rows_mm // G, (r + 1) * rows_mm // G):
                    for c in range(tiles):
                        mode = _tile_mode(blk * tiles + c)
                        y = piece_fn[mode](src, g, c)
                        if mode == MODE_SMALL:
                            sm_ref[g * G:(g + 1) * G, :] = y
                            y = jnp.zeros_like(y)
                        o_ref[g * G:(g + 1) * G, c * 128:(c + 1) * 128] = y.astype(o_ref.dtype)


class Rows:
    def __init__(self, B, Lc, Ll):
        self.B, self.Lc, self.Ll = B, Lc, Ll
        self.NC = B * Lc
        self.R = B * Lc + B * Ll
        assert self.NC % Ll == 0 or Ll % self.NC == 0
        tm = 1024
        while self.NC % tm or Ll % tm:
            tm //= 2
        self.tm = tm

    def mod_index(self, tm):
        nctx = self.NC // tm
        per = self.Ll // tm
        return lambda i: jnp.where(i < nctx, 0, 1 + (i - nctx) // per)


def in_proj(rw, x, nw, mod, w, par):
    R, D = x.shape
    N = w.shape[1]
    tm, tn = rw.tm, IN_TN
    nj = N // tn
    assert N == N_IN_PAD
    mi = rw.mod_index(tm)
    done = lambda j: jnp.maximum(j - 1, 0)
    return pl.pallas_call(
        functools.partial(_in_proj_kernel, nctx_blk=rw.NC // tm),
        grid=(R // tm, nj + 1),
        in_specs=[
            pl.BlockSpec((tm, D), lambda i, j: (i, 0)),
            pl.BlockSpec((1, D), lambda i, j: (0, 0)),
            pl.BlockSpec((1, 8, D), lambda i, j: (mi(i), 0, 0)),
            pl.BlockSpec((D, tn), lambda i, j: (0, jnp.minimum(j, nj - 1))),
            pl.BlockSpec((8, tn), lambda i, j: (0, done(j))),
        ],
        out_specs=[pl.BlockSpec((tm, tn), lambda i, j: (i, done(j))),
                   pl.BlockSpec((tm, 128), lambda i, j: (i, 0))],
        out_shape=[jax.ShapeDtypeStruct((R, N), BF16), jax.ShapeDtypeStruct((R, 128), F32)],
        scratch_shapes=[pltpu.VMEM((tm, D), BF16), pltpu.VMEM((tm, tn), F32), pltpu.VMEM((tm, tn), F32)],
        compiler_params=_cp("arbitrary", "arbitrary"),
        name="in_proj",
    )(x, nw.reshape(1, D), mod, w, par)


def _in_proj_params(hy_conv_w, hy_conv_b, ssm_conv_w, ssm_conv_b, gdn_conv_w, ssm_dt_bias, gdn_dt_bias, gdn_A_log):
    def row(pieces):
        v = jnp.zeros((N_IN_PAD,), F32)
        for off, a in pieces:
            v = lax.dynamic_update_slice(v, a.astype(F32), (off,))
        return v
    z4 = jnp.zeros((4,), F32)
    conv = [row([(C_HY, hy_conv_w[t]), (C_XBC, ssm_conv_w[t]), (C_QKV, gdn_conv_w[t])]) for t in range(3)]
    bias = row([(C_HY, hy_conv_b), (C_XBC, ssm_conv_b)])
    l2s = row([(C_QKV, jnp.full((512,), GDN_DK ** -0.5, F32)), (C_QKV + 512, jnp.ones((512,), F32))])
    sbias = row([(C_SM, jnp.concatenate([ssm_dt_bias.reshape(16), gdn_dt_bias[0], z4, gdn_dt_bias[1], z4]))])
    salog = row([(C_SM + 16, jnp.concatenate([gdn_A_log[0], z4, gdn_A_log[1], z4]))])
    kind = np.full((N_IN_PAD,), 3.0, np.float32)
    kind[C_SM:C_SM + 16] = 0.0
    kind[C_SM + 16:C_SM + 20] = 1.0
    kind[C_SM + 24:C_SM + 28] = 1.0
    kind[C_SM + 20:C_SM + 24] = 2.0
    kind[C_SM + 28:C_SM + 32] = 2.0
    return jnp.stack(conv + [bias, l2s, sbias, salog, jnp.asarray(kind)], axis=0)


def _hy_filter_kernel(z_ref, w1_ref, b1_ref, w2_ref, b2_ref, w3_ref, f0_ref, f1_ref, win_ref, o_ref, h_ref):
    @pl.when(pl.program_id(1) == 0)
    def _():
        h1 = jnp.sin(f0_ref[...] * (_dot(z_ref[...], w1_ref[...], HI) + b1_ref[...]))
        h_ref[...] = jnp.sin(f1_ref[...] * (_dot(h1, w2_ref[...], HI) + b2_ref[...]))

    h = _dot(h_ref[...], w3_ref[...], HI) * win_ref[...]
    tl = h.shape[0]
    row = lax.broadcasted_iota(jnp.int32, (tl, 1), 0) + pl.program_id(0) * tl
    drop = (row == 0) & (pl.program_id(1) % 2 == 1)
    o_ref[...] = jnp.where(drop, 0.0, h).astype(o_ref.dtype)


def hy_features(L):
    t = jnp.linspace(0.0, 1.0, L, dtype=F32)[:, None]
    w = 2.0 * math.pi * jnp.arange(L, dtype=F32)[:, None] / L
    f = jnp.linspace(1e-4, HY_BANDS - 1, HY_BANDS, dtype=F32)[None, :]
    z = jnp.concatenate([t, jnp.cos(f * w), -jnp.sin(f * w)], axis=-1)
    z = jnp.pad(z, ((0, 0), (0, 128 - HY_EMB)))
    min_decay = math.log(HY_TARGET) / HY_LONG_DECAY_PCT
    max_decay = math.log(HY_TARGET) / HY_SHORT_DECAY_PCT
    deltas = jnp.linspace(min_decay, max_decay, HY_WIDTH, dtype=F32)
    window = jnp.exp(-t * jnp.abs(deltas))
    return z, window


def hy_filter(feat, w1, b1, w2, b2, w3, freq):
    z, window = feat
    L = z.shape[0]
    H = HY_HIDDEN
    w1p = jnp.pad(w1, ((0, 128 - HY_EMB), (0, 128 - H)))
    w2p = jnp.pad(w2, ((0, 128 - H), (0, 128 - H)))
    w3p = jnp.pad(w3, ((0, 128 - H), (0, 0)))
    pad1 = lambda v: jnp.pad(v, (0, 128 - H)).reshape(1, 128)
    tl = 256
    full = lambda shape: pl.BlockSpec(shape, lambda i, j: (0, 0))
    return pl.pallas_call(
        _hy_filter_kernel,
        grid=(L // tl, 4),
        in_specs=[
            pl.BlockSpec((tl, 128), lambda i, j: (i, 0)),
            full((128, 128)), full((1, 128)), full((128, 128)), full((1, 128)),
            pl.BlockSpec((128, HY_WIDTH), lambda i, j: (0, j)),
            full((1, 128)), full((1, 128)),
            pl.BlockSpec((tl, HY_WIDTH), lambda i, j: (i, 0)),
        ],
        out_specs=pl.BlockSpec((tl, HY_WIDTH), lambda i, j: (i, j)),
        out_shape=jax.ShapeDtypeStruct((L, 4 * HY_WIDTH), BF16),
        scratch_shapes=[pltpu.VMEM((tl, 128), F32)],
        compiler_params=_cp("arbitrary", "arbitrary"),
        name="hy_filter",
    )(z, w1p, pad1(b1), w2p, pad1(b2), w3p, pad1(freq[0]), pad1(freq[1]), window)


def dft_tables(L):
    N = 2 * L
    f = jnp.arange(L, dtype=jnp.int32)[:, None]
    s = jnp.arange(L, dtype=jnp.int32)[None, :]
    ang = ((f * s) % N).astype(F32) * (2.0 * math.pi / N)
    c, sn = jnp.cos(ang), jnp.sin(ang)
    alt = (1 - 2 * (s % 2)).astype(F32)
    first = f == 0
    fwd = jnp.concatenate([c, jnp.where(first, alt, -sn)], axis=0)
    wgt = jnp.where(first, 1.0, 2.0) / N
    inv = jnp.concatenate([(c * wgt).T, jnp.where(first, alt / N, -sn * wgt).T], axis=1)
    return fwd.astype(BF16), inv.astype(BF16)


def _matmul_kernel(a_ref, b_ref, o_ref):
    o_ref[...] = _dot(a_ref[...], b_ref[...])


def matmul(a, b, tm, tn):
    M, K = a.shape
    N = b.shape[1]
    return pl.pallas_call(
        _matmul_kernel,
        grid=(M // tm, N // tn),
        in_specs=[pl.BlockSpec((tm, K), lambda i, j: (i, 0)), pl.BlockSpec((K, tn), lambda i, j: (0, j))],
        out_specs=pl.BlockSpec((tm, tn), lambda i, j: (i, j)),
        out_shape=jax.ShapeDtypeStruct((M, N), F32),
        compiler_params=_cp("arbitrary", "arbitrary"),
        name="matmul",
    )(a, b)


def _long_conv_kernel(u_ref, g_ref, bias_ref, fr_ref, fi_ref, ic_ref, is_ref,
                      ar0_ref, ar1_ref, ai0_ref, ai1_ref, o_ref, ub_ref, acc_ref):
    f = pl.program_id(1)

    @pl.when(f == 0)
    def _():
        ub_ref[...] = u_ref[...].astype(BF16)
        acc_ref[...] = jnp.zeros_like(acc_ref)

    ub = ub_ref[...]
    ur = _dot(fr_ref[...], ub)
    ui = _dot(fi_ref[...], ub)
    kr = ar0_ref[...] + ar1_ref[...]
    nyq = (lax.broadcasted_iota(jnp.int32, (FREQ_BLK, 1), 0) == 0) & (f == 0)
    ki = jnp.where(nyq, ai0_ref[...] + ai1_ref[...], ai0_ref[...] - ai1_ref[...])
    pr = jnp.where(nyq, ur * kr, ur * kr - ui * ki)
    pi = jnp.where(nyq, ui * ki, ur * ki + ui * kr)
    acc_ref[...] += _dot(ic_ref[...], pr.astype(BF16)) + _dot(is_ref[...], pi.astype(BF16))

    @pl.when(f == pl.num_programs(1) - 1)
    def _():
        u = u_ref[...].astype(F32)
        o_ref[...] = (g_ref[...].astype(F32) * (acc_ref[...] + u * bias_ref[...])).astype(o_ref.dtype)


def long_conv(B, L, u, u_rb0, u_cb, gate, g_rb0, gate_cb, bias, fwd, inv, kspec, order, out_dtype):
    C = HY_WIDTH
    nfb = L // FREQ_BLK
    FB = FREQ_BLK
    return pl.pallas_call(
        _long_conv_kernel,
        grid=(B, nfb),
        in_specs=[
            pl.BlockSpec((L, C), lambda b, f: (u_rb0 + b, u_cb)),
            pl.BlockSpec((L, C), lambda b, f: (g_rb0 + b, gate_cb)),
            pl.BlockSpec((1, C), lambda b, f: (0, 0)),
            pl.BlockSpec((FB, L), lambda b, f: (f, 0)),
            pl.BlockSpec((FB, L), lambda b, f: (nfb + f, 0)),
            pl.BlockSpec((L, FB), lambda b, f: (0, f)),
            pl.BlockSpec((L, FB), lambda b, f: (0, nfb + f)),
            pl.BlockSpec((FB, C), lambda b, f: (f, 2 * order)),
            pl.BlockSpec((FB, C), lambda b, f: (f, 2 * order + 1)),
            pl.BlockSpec((FB, C), lambda b, f: (nfb + f, 2 * order)),
            pl.BlockSpec((FB, C), lambda b, f: (nfb + f, 2 * order + 1)),
        ],
        out_specs=pl.BlockSpec((L, C), lambda b, f: (b, 0)),
        out_shape=jax.ShapeDtypeStruct((B * L, C), out_dtype),
        scratch_shapes=[pltpu.VMEM((L, C), BF16), pltpu.VMEM((L, C), F32)],
        compiler_params=_cp("arbitrary", "arbitrary"),
        name="long_conv",
    )(u, gate, bias.reshape(1, C), fwd, fwd, inv, inv, kspec, kspec, kspec, kspec)


def _scan_blocks(rw, rows):
    nbc, nbl, base = rw.Lc // rows, rw.Ll // rows, rw.NC // rows

    def make(d):
        def f(b, s):
            jc = s if d == 0 else nbc - 1 - s
            jl = (s - nbc) if d == 0 else nbl - 1 - (s - nbc)
            return jnp.where(s < nbc, b * nbc + jc, base + b * nbl + jl)
        return f

    return [make(0), make(1)], nbc + nbl


def _expand_lanes(x, base, n, width):
    rows = x.shape[0]
    per = 128 // width
    lane = lax.broadcasted_iota(jnp.int32, (rows, 128), 1)
    tiles = []
    for t in range(n // per):
        c0 = base + t * per
        tile = jnp.broadcast_to(x[:, c0:c0 + 1], (rows, 128))
        for i in range(1, per):
            tile = jnp.where(lane >= i * width, jnp.broadcast_to(x[:, c0 + i:c0 + i + 1], (rows, 128)), tile)
        tiles.append(tile)
    return jnp.concatenate(tiles, axis=1)


def _ssd_kernel(xf, bf, cf, smf, dtf, xb, bb, cb_, smb, dtb, alx_ref, alc_ref, of_ref, ob_ref, h_ref):
    Q = SSM_CHUNK
    GW = SSM_GW

    @pl.when(pl.program_id(1) == 0)
    def _():
        h_ref[...] = jnp.zeros_like(h_ref)

    row = lax.broadcasted_iota(jnp.int32, (Q, Q), 0)
    col = lax.broadcasted_iota(jnp.int32, (Q, Q), 1)
    lane_head = lax.broadcasted_iota(jnp.int32, (Q, GW), 1) // SSM_HEAD_DIM
    dirs = ((xf, bf, cf, smf, dtf, of_ref), (xb, bb, cb_, smb, dtb, ob_ref))
    jobs = []
    for d in range(2):
        x_ref, b_ref, c_ref, sm_ref, dt_ref, o_ref = dirs[d]
        keep = (col <= row) if d == 0 else (col >= row)
        tri = keep.astype(BF16)
        tri_t = ((row <= col) if d == 0 else (row >= col)).astype(BF16)
        sm = sm_ref[...]
        a_x = -jnp.exp(alx_ref[d])
        dtx = _expand_lanes(sm, 8 * d, SSM_HEADS, SSM_HEAD_DIM)
        cumx = _expand_lanes(_dot_01_lhs(tri, sm), 8 * d, SSM_HEADS, SSM_HEAD_DIM) * a_x
        cumr = _dot_01_rhs(dt_ref[0], tri_t) * (-jnp.exp(alc_ref[d]))
        last = Q - 1 if d == 0 else 0
        totx = cumx[last:last + 1, :]
        xd = x_ref[...].astype(F32) * dtx
        xdw = xd * jnp.exp(totx - cumx)
        ecum = jnp.exp(cumx)
        for g in range(SSM_GROUPS):
            gs = slice(g * GW, (g + 1) * GW)
            jobs.append(dict(d=d, g=g, gs=gs, keep=keep, cumx=cumx, cumr=cumr, o_ref=o_ref,
                             bg=b_ref[:, g * SSM_STATE:(g + 1) * SSM_STATE].astype(BF16),
                             cg=c_ref[:, g * SSM_STATE:(g + 1) * SSM_STATE].astype(BF16),
                             xdg=xd[:, gs], xdw=xdw[:, gs].astype(BF16), ecum=ecum[:, gs],
                             etot=jnp.exp(totx[:, gs])))
    for j in jobs:
        j["cb"] = _dot_nt(j["cg"], j["bg"])
        j["h"] = h_ref[j["d"], j["g"]]
    for j in jobs:
        ms, xs = [], []
        for e4 in range(SSM_HPG):
            e = j["g"] * SSM_HPG + e4
            diff = j["cumx"][:, e * SSM_HEAD_DIM:e * SSM_HEAD_DIM + 1] - j["cumr"][e:e + 1, :]
            ms.append((j["cb"] * jnp.where(j["keep"], jnp.exp(diff), 0.0)).astype(BF16))
            xs.append(jnp.where(lane_head == e4, j["xdg"], 0.0).astype(BF16))
        yd = _dot(jnp.concatenate(ms, axis=1), jnp.concatenate(xs, axis=0))
        y_off = _dot(j["cg"], j["h"].astype(BF16)) * j["ecum"]
        j["o_ref"][:, j["gs"]] = yd + y_off
    for j in jobs:
        h_ref[j["d"], j["g"]] = j["h"] * j["etot"] + _dot_tn(j["bg"], j["xdw"])


def ssd_scan(rw, p, sm, dtT, alx, alc):
    Q = SSM_CHUNK
    blks, nsteps = _scan_blocks(rw, Q)
    R = p.shape[0]
    in_specs = []
    for d in range(2):
        f = blks[d]
        in_specs += [
            pl.BlockSpec((Q, 512), lambda b, s, f=f: (f(b, s), C_XBC // 512)),
            pl.BlockSpec((Q, 256), lambda b, s, f=f: (f(b, s), C_XBC // 256 + 2)),
            pl.BlockSpec((Q, 256), lambda b, s, f=f: (f(b, s), C_XBC // 256 + 3)),
            pl.BlockSpec((Q, 128), lambda b, s, f=f: (f(b, s), 0)),
            pl.BlockSpec((1, 8, Q), lambda b, s, f=f, d=d: (d, 0, f(b, s))),
        ]
    in_specs += [pl.BlockSpec((2, 1, 512), lambda b, s: (0, 0, 0)), pl.BlockSpec((2, 8, 1), lambda b, s: (0, 0, 0))]
    ops = (p, p, p, sm, dtT)
    return pl.pallas_call(
        _ssd_kernel,
        grid=(rw.B, nsteps),
        in_specs=in_specs,
        out_specs=[pl.BlockSpec((Q, 512), lambda b, s, f=blks[d]: (f(b, s), 0)) for d in range(2)],
        out_shape=[jax.ShapeDtypeStruct((R, 512), F32)] * 2,
        scratch_shapes=[pltpu.VMEM((2, SSM_GROUPS, SSM_STATE, SSM_GW), F32)],
        compiler_params=_cp("arbitrary", "arbitrary"),
        name="ssd_scan",
    )(*ops, *ops, alx, alc)


def _ssm_post_kernel(yf_ref, yb_ref, x_ref, z_ref, dx_ref, nw_ref, o_ref):
    y = yf_ref[...] + yb_ref[...] + x_ref[...].astype(F32) * dx_ref[...]
    y = y * _silu(z_ref[...].astype(F32))
    parts = []
    for g in range(SSM_GROUPS):
        yg = y[:, g * SSM_GW:(g + 1) * SSM_GW]
        ms = jnp.mean(yg * yg, axis=-1, keepdims=True)
        parts.append(yg * lax.rsqrt(ms + EPS))
    o_ref[...] = (jnp.concatenate(parts, axis=1) * nw_ref[...]).astype(o_ref.dtype)


def ssm_post(rw, y_f, y_b, p, dx, nw):
    R = p.shape[0]
    tm = rw.tm
    vec = pl.BlockSpec((1, 512), lambda i: (0, 0))
    return pl.pallas_call(
        _ssm_post_kernel,
        grid=(R // tm,),
        in_specs=[
            pl.BlockSpec((tm, 512), lambda i: (i, 0)),
            pl.BlockSpec((tm, 512), lambda i: (i, 0)),
            pl.BlockSpec((tm, 512), lambda i: (i, C_XBC // 512)),
            pl.BlockSpec((tm, 512), lambda i: (i, C_Z // 512)),
            vec, vec,
        ],
        out_specs=pl.BlockSpec((tm, 512), lambda i: (i, 0)),
        out_shape=jax.ShapeDtypeStruct((R, 512), BF16),
        compiler_params=_cp("arbitrary"),
        name="ssm_post",
    )(y_f, y_b, p, p, dx, nw)


def _split3(x):
    x1 = x.astype(BF16)
    r = x - x1.astype(F32)
    x2 = r.astype(BF16)
    x3 = (r - x2.astype(F32)).astype(BF16)
    return x1, x2, x3


def _dot_01_lhs(m01, x):
    x1, x2, x3 = _split3(x)
    return _dot(m01, x1) + _dot(m01, x2) + _dot(m01, x3)


def _dot_01_rhs(x, m01):
    x1, x2, x3 = _split3(x)
    return _dot(x1, m01) + _dot(x2, m01) + _dot(x3, m01)


GDN_ROWS = 256


def _gdn_prep_kernel(q_ref, k_ref, v_ref, sm_ref, gT_ref, u_ref, w_ref, qg_ref, kd_ref, qk_ref, egl_ref):
    C = GDN_CHUNK
    row = lax.broadcasted_iota(jnp.int32, (C, C), 0)
    col = lax.broadcasted_iota(jnp.int32, (C, C), 1)
    jobs = []
    levels = []
    for d in range(2):
        keep = (col <= row) if d == 0 else (col >= row)
        late, early = (row, col) if d == 0 else (col, row)
        levels.append([(((row ^ col) >> (j + 1)) == 0) & ((late & (1 << j)) != 0) & ((early & (1 << j)) == 0)
                       for j in range(6)])
        tri = keep.astype(BF16)
        tri_t = ((row <= col) if d == 0 else (row >= col)).astype(BF16)
        last = C - 1 if d == 0 else 0
        for c in range(GDN_ROWS // C):
            rows = slice(c * C, (c + 1) * C)
            smc = sm_ref[rows, :]
            cums = _dot_01_lhs(tri, smc)
            cumr = _dot_01_rhs(gT_ref[c, 8 * d:8 * d + 8, :], tri_t)
            tot = cums[last:last + 1, :]
            for h in range(GDN_HEADS):
                lg = 16 + 8 * d + h
                jobs.append(dict(d=d, c=c, h=h, rows=rows, hs=slice(h * 128, (h + 1) * 128), keep=keep,
                                 gc=cums[:, lg:lg + 1], beta=smc[:, lg + 4:lg + 5],
                                 gl=tot[:, lg:lg + 1], gr=cumr[h:h + 1, :]))
    for j in jobs:
        q = q_ref[j["rows"], j["hs"]].astype(F32)
        k = k_ref[j["rows"], j["hs"]].astype(F32)
        j["dec"] = jnp.where(j["keep"], jnp.exp(j["gc"] - j["gr"]), 0.0)
        kb = k * j["beta"]
        both = _dot_nt(jnp.concatenate([kb, q], axis=0).astype(BF16), k.astype(BF16))
        j["a"] = both[:C] * j["dec"]
        j["n"] = -jnp.where(levels[j["d"]][0], j["a"], 0.0)
        qk_ref[j["d"], j["c"], j["h"]] = (both[C:] * j["dec"]).astype(BF16)
    for lev in range(1, 6):
        for j in jobs:
            l = jnp.where(levels[j["d"]][lev], j["a"], 0.0)
            j["y"] = l + _dot(l.astype(BF16), j["n"].astype(BF16))
        for j in jobs:
            j["n"] = j["n"] - j["y"] - _dot(j["n"].astype(BF16), j["y"].astype(BF16))
    for j in jobs:
        d, rows, hs, gc, gl, beta = j["d"], j["rows"], j["hs"], j["gc"], j["gl"], j["beta"]
        q = q_ref[rows, hs].astype(F32)
        k = k_ref[rows, hs].astype(F32)
        eg = jnp.exp(gc)
        rhs = jnp.concatenate([v_ref[rows, hs].astype(F32) * beta, k * beta * eg], axis=1)
        sol = rhs + _dot(j["n"].astype(BF16), rhs.astype(BF16))
        u_ref[d, rows, hs] = sol[:, :GDN_DV]
        w_ref[d, rows, hs] = sol[:, GDN_DV:].astype(BF16)
        qg_ref[d, rows, hs] = (q * eg).astype(BF16)
        kd_ref[d, rows, hs] = (k * jnp.exp(gl - gc)).astype(BF16)
        egl_ref[d, j["c"], :, hs] = jnp.broadcast_to(jnp.exp(gl), (8, 128))


def gdn_prep(p, sm, gT):
    R = p.shape[0]
    T, C = GDN_ROWS, GDN_CHUNK
    nc = T // C
    col = lambda k: pl.BlockSpec((T, 512), lambda i: (i, C_QKV // 512 + k))
    dirrow = pl.BlockSpec((2, T, 512), lambda i: (0, i, 0))
    return pl.pallas_call(
        _gdn_prep_kernel,
        grid=(R // T,),
        in_specs=[col(0), col(1), col(2),
                  pl.BlockSpec((T, 128), lambda i: (i, 0)),
                  pl.BlockSpec((nc, 16, C), lambda i: (i, 0, 0))],
        out_specs=[dirrow, dirrow, dirrow, dirrow,
                   pl.BlockSpec((2, nc, GDN_HEADS, C, C), lambda i: (0, i, 0, 0, 0)),
                   pl.BlockSpec((2, nc, 8, 512), lambda i: (0, i, 0, 0))],
        out_shape=[jax.ShapeDtypeStruct((2, R, 512), F32),
                   jax.ShapeDtypeStruct((2, R, 512), BF16),
                   jax.ShapeDtypeStruct((2, R, 512), BF16),
                   jax.ShapeDtypeStruct((2, R, 512), BF16),
                   jax.ShapeDtypeStruct((2, R // C, GDN_HEADS, C, C), BF16),
                   jax.ShapeDtypeStruct((2, R // C, 8, 512), F32)],
        compiler_params=_cp("arbitrary"),
        name="gdn_prep",
    )(p, p, p, sm, gT)


def _gdn_scan_kernel(uf, wf, qgf, kdf, qkf, eglf, ub, wb, qgb, kdb, qkb, eglb, of_ref, ob_ref, s_ref):
    C = GDN_CHUNK
    nch = GDN_ROWS // C

    @pl.when(pl.program_id(1) == 0)
    def _():
        s_ref[...] = jnp.zeros_like(s_ref)

    dirs = ((uf, wf, qgf, kdf, qkf, eglf, of_ref), (ub, wb, qgb, kdb, qkb, eglb, ob_ref))
    chains = [(d, h) for d in range(2) for h in range(GDN_HEADS)]
    S = {ch: s_ref[ch[0], ch[1]] for ch in chains}
    for i in range(nch):
        Sb, vnb, rows_of, c_of = {}, {}, {}, {}
        for d, h in chains:
            c_of[d] = i if d == 0 else nch - 1 - i
            rows_of[d] = slice(c_of[d] * C, (c_of[d] + 1) * C)
        for d, h in chains:
            hs = slice(h * 128, (h + 1) * 128)
            Sb[d, h] = S[d, h].astype(BF16)
            v_new = dirs[d][0][0, rows_of[d], hs] - _dot(dirs[d][1][0, rows_of[d], hs], Sb[d, h])
            vnb[d, h] = v_new.astype(BF16)
        for d, h in chains:
            hs = slice(h * 128, (h + 1) * 128)
            u_ref, w_ref, qg_ref, kd_ref, qk_ref, egl_ref, o_ref = dirs[d]
            S[d, h] = S[d, h] * egl_ref[0, c_of[d], 0:1, hs] + _dot_tn(kd_ref[0, rows_of[d], hs], vnb[d, h])
        for d, h in chains:
            hs = slice(h * 128, (h + 1) * 128)
            u_ref, w_ref, qg_ref, kd_ref, qk_ref, egl_ref, o_ref = dirs[d]
            o_ref[rows_of[d], hs] = (_dot(qg_ref[0, rows_of[d], hs], Sb[d, h])
                                     + _dot(qk_ref[0, c_of[d], h], vnb[d, h]))
    for ch in chains:
        s_ref[ch[0], ch[1]] = S[ch]


def gdn_scan(rw, u, w, qg, kd, qk, egl):
    T, C = GDN_ROWS, GDN_CHUNK
    nc = T // C
    R = u.shape[1]
    nbc, nbl, base = rw.Lc // T, rw.Ll // T, rw.NC // T

    def blk(d):
        def f(b, s):
            jc = s if d == 0 else nbc - 1 - s
            jl = (s - nbc) if d == 0 else nbl - 1 - (s - nbc)
            return jnp.where(s < nbc, b * nbc + jc, base + b * nbl + jl)
        return f

    in_specs = []
    for d in range(2):
        f = blk(d)
        rowspec = pl.BlockSpec((1, T, 512), lambda b, s, f=f, d=d: (d, f(b, s), 0))
        in_specs += [rowspec, rowspec, rowspec, rowspec,
                     pl.BlockSpec((1, nc, GDN_HEADS, C, C), lambda b, s, f=f, d=d: (d, f(b, s), 0, 0, 0)),
                     pl.BlockSpec((1, nc, 8, 512), lambda b, s, f=f, d=d: (d, f(b, s), 0, 0))]
    out_specs = [pl.BlockSpec((T, 512), lambda b, s, f=blk(d): (f(b, s), 0)) for d in range(2)]
    ops = (u, w, qg, kd, qk, egl)
    return pl.pallas_call(
        _gdn_scan_kernel,
        grid=(rw.B, nbc + nbl),
        in_specs=in_specs,
        out_specs=out_specs,
        out_shape=[jax.ShapeDtypeStruct((R, 512), F32)] * 2,
        scratch_shapes=[pltpu.VMEM((2, GDN_HEADS, GDN_DK, GDN_DV), F32)],
        compiler_params=_cp("arbitrary", "arbitrary"),
        name="gdn_scan",
    )(*ops, *ops)


def _gdn_post_kernel(of_ref, ob_ref, g_ref, nw_ref, o_ref):
    o = of_ref[...] + ob_ref[...]
    parts = []
    for h in range(GDN_HEADS):
        oh = o[:, h * 128:(h + 1) * 128]
        ms = jnp.mean(oh * oh, axis=-1, keepdims=True)
        parts.append(oh * lax.rsqrt(ms + EPS))
    o_ref[...] = (jnp.concatenate(parts, axis=1) * nw_ref[...] * _silu(g_ref[...].astype(F32))).astype(o_ref.dtype)


def gdn_post(rw, o_f, o_b, p, nw):
    R = p.shape[0]
    tm = rw.tm
    return pl.pallas_call(
        _gdn_post_kernel,
        grid=(R // tm,),
        in_specs=[
            pl.BlockSpec((tm, 512), lambda i: (i, 0)),
            pl.BlockSpec((tm, 512), lambda i: (i, 0)),
            pl.BlockSpec((tm, 512), lambda i: (i, C_GG // 512)),
            pl.BlockSpec((1, 512), lambda i: (0, 0)),
        ],
        out_specs=pl.BlockSpec((tm, 512), lambda i: (i, 0)),
        out_shape=jax.ShapeDtypeStruct((R, 512), BF16),
        compiler_params=_cp("arbitrary"),
        name="gdn_post",
    )(o_f, o_b, p, nw)


def _merge_kernel(yh_ref, ys_ref, yg_ref, g0_ref, g1_ref, g2_ref, w0_ref, w1_ref, w2_ref, wo_ref,
                  x_ref, mod_ref, o_ref):
    m = (_sigmoid(g0_ref[...].astype(F32)) * _dot(yh_ref[...], w0_ref[...])
         + _sigmoid(g1_ref[...].astype(F32)) * _dot(ys_ref[...], w1_ref[...])
         + _sigmoid(g2_ref[...].astype(F32)) * _dot(yg_ref[...], w2_ref[...]))
    o_ref[...] = x_ref[...] + mod_ref[0, 2:3, :] * _dot(m.astype(BF16), wo_ref[...])


def merge(rw, yh, ys, yg, p, w0, w1, w2, wo, x, mod):
    R, D = x.shape
    tm = min(rw.tm, 512)
    mi = rw.mod_index(tm)
    yspec = pl.BlockSpec((tm, 512), lambda i: (i, 0))
    gspec = lambda k: pl.BlockSpec((tm, D), lambda i: (i, C_GATE // D + k))
    wspec = pl.BlockSpec((512, D), lambda i: (0, 0))
    return pl.pallas_call(
        _merge_kernel,
        grid=(R // tm,),
        in_specs=[yspec, yspec, yspec, gspec(0), gspec(1), gspec(2), wspec, wspec, wspec,
                  pl.BlockSpec((D, D), lambda i: (0, 0)),
                  pl.BlockSpec((tm, D), lambda i: (i, 0)),
                  pl.BlockSpec((1, 8, D), lambda i: (mi(i), 0, 0))],
        out_specs=pl.BlockSpec((tm, D), lambda i: (i, 0)),
        out_shape=jax.ShapeDtypeStruct((R, D), F32),
        compiler_params=_cp("arbitrary"),
        name="merge",
    )(yh, ys, yg, p, p, p, w0, w1, w2, wo, x, mod)


def _swiglu_up_kernel(x_ref, nw_ref, mod_ref, wg_ref, wu_ref, o_ref, h_ref):
    @pl.when(pl.program_id(1) == 0)
    def _():
        h = _norm_mod(x_ref[...], nw_ref[...], mod_ref[0, 4:5, :], mod_ref[0, 3:4, :])
        h_ref[...] = h.astype(BF16)

    h = h_ref[...]
    g = _dot(h, wg_ref[...])
    u = _dot(h, wu_ref[...])
    o_ref[...] = (_silu(g) * u).astype(o_ref.dtype)


def swiglu_up(rw, x, nw, mod, wgu):
    R, D = x.shape
    tm = rw.tm
    tn = D_FF // 2
    nj = D_FF // tn
    mi = rw.mod_index(tm)
    return pl.pallas_call(
        _swiglu_up_kernel,
        grid=(R // tm, nj),
        in_specs=[
            pl.BlockSpec((tm, D), lambda i, j: (i, 0)),
            pl.BlockSpec((1, D), lambda i, j: (0, 0)),
            pl.BlockSpec((1, 8, D), lambda i, j: (mi(i), 0, 0)),
            pl.BlockSpec((D, tn), lambda i, j: (0, j)),
            pl.BlockSpec((D, tn), lambda i, j: (0, nj + j)),
        ],
        out_specs=pl.BlockSpec((tm, tn), lambda i, j: (i, j)),
        out_shape=jax.ShapeDtypeStruct((R, D_FF), BF16),
        scratch_shapes=[pltpu.VMEM((tm, D), BF16)],
        compiler_params=_cp("arbitrary", "arbitrary"),
        name="swiglu_up",
    )(x, nw.reshape(1, D), mod, wgu, wgu)


def _swiglu_down_kernel(a_ref, w_ref, x_ref, mod_ref, o_ref):
    o_ref[...] = x_ref[...] + mod_ref[0, 5:6, :] * _dot(a_ref[...], w_ref[...])


def swiglu_down(rw, a, w, x, mod):
    R, D = x.shape
    tm = min(rw.tm, 512)
    mi = rw.mod_index(tm)
    return pl.pallas_call(
        _swiglu_down_kernel,
        grid=(R // tm,),
        in_specs=[
            pl.BlockSpec((tm, D_FF), lambda i: (i, 0)),
            pl.BlockSpec((D_FF, D), lambda i: (0, 0)),
            pl.BlockSpec((tm, D), lambda i: (i, 0)),
            pl.BlockSpec((1, 8, D), lambda i: (mi(i), 0, 0)),
        ],
        out_specs=pl.BlockSpec((tm, D), lambda i: (i, 0)),
        out_shape=jax.ShapeDtypeStruct((R, D), F32),
        compiler_params=_cp("arbitrary"),
        name="swiglu_down",
    )(a, w, x, mod)


def _final_norm_kernel(x_ref, w_ref, o_ref):
    x = x_ref[...]
    ms = jnp.mean(x * x, axis=-1, keepdims=True)
    o_ref[...] = x * lax.rsqrt(ms + EPS) * w_ref[...]


def final_norm(rw, x, w):
    D = x.shape[1]
    tm = rw.tm
    n0 = rw.NC // tm
    nl = rw.B * rw.Ll
    return pl.pallas_call(
        _final_norm_kernel,
        grid=(nl // tm,),
        in_specs=[pl.BlockSpec((tm, D), lambda i: (n0 + i, 0)), pl.BlockSpec((1, D), lambda i: (0, 0))],
        out_specs=pl.BlockSpec((tm, D), lambda i: (i, 0)),
        out_shape=jax.ShapeDtypeStruct((nl, D), F32),
        compiler_params=_cp("arbitrary"),
        name="final_norm",
    )(x, w.reshape(1, D))


def _regroup_w_in(w_in):
    o_dt = 3072
    o_gdn = 3088
    o_a = o_gdn + 2048
    o_b = o_a + 8
    o_gate = o_gdn + 2064
    pieces = [
        w_in[..., 0:3072],
        w_in[..., o_gdn:o_gdn + 2048],
        w_in[..., o_gate:o_gate + 3072],
        w_in[..., o_dt:o_dt + 16],
        w_in[..., o_a:o_a + 4], w_in[..., o_b:o_b + 4],
        w_in[..., o_a + 4:o_a + 8], w_in[..., o_b + 4:o_b + 8],
        jnp.zeros(w_in.shape[:-1] + (N_IN_PAD - C_SM - 32,), w_in.dtype),
    ]
    return jnp.concatenate(pieces, axis=-1).astype(BF16)


def kernel(x, c, ctx, c_ctx, w_ada, b_ada, norm1_w, norm2_w, w_in, hy_conv_w, hy_conv_b, hy_w1, hy_b1, hy_w2, hy_b2, hy_w3, hy_freq, hy_bias, ssm_conv_w, ssm_conv_b, ssm_dt_bias, ssm_A_log, ssm_D, ssm_norm_w, gdn_conv_w, gdn_dt_bias, gdn_A_log, gdn_norm_w, w_hy_out, w_ssm_out, w_gdn_out, w_out, w_gate_up, w_down, final_norm_w):
    B, Ll, D = x.shape
    Lc = ctx.shape[1]
    depth = w_ada.shape[0]
    assert Lc == CONV_ROWS and D == D_MODEL and B <= 15
    rw = Rows(B, Lc, Ll)
    R, NC = rw.R, rw.NC

    xa = jnp.concatenate([ctx.reshape(B * Lc, D), x.reshape(B * Ll, D)], axis=0)

    svec = jnp.concatenate([c_ctx[None, :], c, jnp.zeros((15 - B, D), F32)], axis=0)
    mod_all = ada_modulation(svec, w_ada, b_ada)
    mod_all = jnp.pad(mod_all.reshape(depth, 16, 6, D), ((0, 0), (0, 0), (0, 2), (0, 0)))

    w_in_r = _regroup_w_in(w_in)
    fwd_l, inv_l = dft_tables(Ll)
    fwd_c, inv_c = dft_tables(Lc)
    feat_l, feat_c = hy_features(Ll), hy_features(Lc)

    for l in range(depth):
        mod = mod_all[l]
        par = _in_proj_params(hy_conv_w[l], hy_conv_b[l], ssm_conv_w[l], ssm_conv_b[l], gdn_conv_w[l],
                              ssm_dt_bias[l], gdn_dt_bias[l], gdn_A_log[l])
        p, sm = in_proj(rw, xa, norm1_w[l], mod, w_in_r[l], par)

        sm32_t = sm[:, :32].T
        dt_t = sm32_t[:16].reshape(2, 8, R)
        g_t = sm32_t[16:32].reshape(16, R // GDN_CHUNK, GDN_CHUNK).transpose(1, 0, 2)

        alx = jnp.repeat(ssm_A_log[l], SSM_HEAD_DIM, axis=-1).reshape(2, 1, 512)
        alc = ssm_A_log[l].reshape(2, 8, 1)
        y_f, y_b = ssd_scan(rw, p, sm, dt_t, alx, alc)
        dx = jnp.repeat(ssm_D[l], SSM_HEAD_DIM).reshape(1, 512)
        y_ssm = ssm_post(rw, y_f, y_b, p, dx, ssm_norm_w[l].reshape(1, 512))

        o_f, o_b = gdn_scan(rw, *gdn_prep(p, sm, g_t))
        y_gdn = gdn_post(rw, o_f, o_b, p, jnp.tile(gdn_norm_w[l], GDN_HEADS).reshape(1, 512))

        hyu = p
        parts = []
        for (Bn, L, blk0, fwd, inv, feat) in ((B, Lc, 0, fwd_c, inv_c, feat_c),
                                              (B, Ll, NC // Ll, fwd_l, inv_l, feat_l)):
            if NC % L:
                raise ValueError("latent length must divide the context row count")
            filt = hy_filter(feat, hy_w1[l], hy_b1[l], hy_w2[l], hy_b2[l], hy_w3[l], hy_freq[l])
            kspec = matmul(fwd, filt, min(512, 2 * L), 512)
            z1 = long_conv(Bn, L, hyu, blk0, 0, hyu, blk0, 1, hy_bias[l, 0], fwd, inv, kspec, 0, F32)
            yy = long_conv(Bn, L, z1, 0, 0, hyu, blk0, 2, hy_bias[l, 1], fwd, inv, kspec, 1, BF16)
            parts.append(yy)
        y_hy = jnp.concatenate(parts, axis=0)

        xa = merge(rw, y_hy, y_ssm, y_gdn, p, w_hy_out[l].astype(BF16), w_ssm_out[l].astype(BF16),
                   w_gdn_out[l].astype(BF16), w_out[l].astype(BF16), xa, mod)
        act = swiglu_up(rw, xa, norm2_w[l], mod, w_gate_up[l].astype(BF16))
        xa = swiglu_down(rw, act, w_down[l].astype(BF16), xa, mod)

    out = final_norm(rw, xa, final_norm_w)
    return out.reshape(B, Ll, D)
```

```python
import functools
import math

import jax
import jax.numpy as jnp
import numpy as np
from jax import lax
from jax.experimental import pallas as pl
from jax.experimental.pallas import tpu as pltpu

F32 = jnp.float32
BF16 = jnp.bfloat16
HI = lax.Precision.HIGHEST

EPS = 1e-6
D_MODEL = 1024
GRID_W = 64

HY_WIDTH = 512
HY_BANDS = 16
HY_EMB = 1 + 2 * HY_BANDS
HY_HIDDEN = 64
HY_SHORT_DECAY_PCT = 0.3
HY_LONG_DECAY_PCT = 1.5
HY_TARGET = 1e-2

SSM_HEADS = 8
SSM_HEAD_DIM = 64
SSM_WIDTH = 512
SSM_GROUPS = 2
SSM_HPG = 4
SSM_STATE = 128
SSM_CHUNK = 128
SSM_GW = SSM_HPG * SSM_HEAD_DIM

GDN_HEADS = 4
GDN_DK = 128
GDN_DV = 128
GDN_CHUNK = 64

D_FF = 2816

C_HY = 0
C_Z = 1536
C_XBC = 2048
C_QKV = 3072
C_GG = 4608
C_GATE = 5120
C_SM = 8192

CONV_ROWS = 256
FREQ_BLK = 256

VMEM_LIMIT = 56 * 1024 * 1024


def _cp(*sem):
    return pltpu.CompilerParams(dimension_semantics=sem, vmem_limit_bytes=VMEM_LIMIT)


def _sigmoid(x):
    return 1.0 / (1.0 + jnp.exp(-x))


def _silu(x):
    return x * _sigmoid(x)


def _softplus(x):
    return jnp.maximum(x, 0.0) + jnp.log1p(jnp.exp(-jnp.abs(x)))


def _dot(a, b, precision=None):
    return jnp.dot(a, b, precision=precision, preferred_element_type=F32)


def _dot_nt(a, b):
    return lax.dot_general(a, b, (((1,), (1,)), ((), ())), preferred_element_type=F32)


def _dot_tn(a, b):
    return lax.dot_general(a, b, (((0,), (0,)), ((), ())), preferred_element_type=F32)


def _ada_kernel(s_ref, w_ref, b_ref, o_ref):
    s = _silu(s_ref[...])
    o_ref[0] = _dot(s, w_ref[0], HI) + b_ref[0]


def ada_modulation(svec, w_ada, b_ada):
    depth = w_ada.shape[0]
    D = D_MODEL
    return pl.pallas_call(
        _ada_kernel,
        grid=(depth, 6),
        in_specs=[
            pl.BlockSpec((16, D), lambda l, j: (0, 0)),
            pl.BlockSpec((1, D, D), lambda l, j: (l, 0, j)),
            pl.BlockSpec((1, 1, D), lambda l, j: (l, 0, j)),
        ],
        out_specs=pl.BlockSpec((1, 16, D), lambda l, j: (l, 0, j)),
        out_shape=jax.ShapeDtypeStruct((depth, 16, 6 * D), F32),
        compiler_params=_cp("arbitrary", "arbitrary"),
        name="ada",
    )(svec, w_ada, b_ada.reshape(depth, 1, 6 * D))


def _norm_mod(x, nw, scale, shift):
    ms = jnp.mean(x * x, axis=-1, keepdims=True)
    return (x * lax.rsqrt(ms + EPS) * nw) * (1.0 + scale) + shift


N_IN_PAD = C_SM + 128
IN_TN = N_IN_PAD // 5
MODE_RAW, MODE_CONV, MODE_CONV_SILU, MODE_CONV_SILU_L2, MODE_SMALL = range(5)


def _tile_mode(tile):
    col = tile * 128
    if col < C_Z:
        return MODE_CONV
    if col < C_XBC:
        return MODE_RAW
    if col < C_QKV:
        return MODE_CONV_SILU
    if col < C_QKV + 1024:
        return MODE_CONV_SILU_L2
    if col < C_GG:
        return MODE_CONV_SILU
    if col < C_SM:
        return MODE_RAW
    return MODE_SMALL
PAR_W0, PAR_W1, PAR_W2, PAR_BIAS, PAR_L2SCALE, PAR_SBIAS, PAR_SALOG, PAR_SKIND = range(8)


def _in_proj_kernel(x_ref, nw_ref, mod_ref, w_ref, par_ref, o_ref, sm_ref, h_ref, raw0_ref, raw1_ref, *, nctx_blk):
    j = pl.program_id(1)
    nj = N_IN_PAD // IN_TN
    raws = (raw0_ref, raw1_ref)

    @pl.when(j == 0)
    def _():
        h = _norm_mod(x_ref[...], nw_ref[...], mod_ref[0, 1:2, :], mod_ref[0, 0:1, :])
        h_ref[...] = h.astype(BF16)

    T = h_ref.shape[0]
    G = GRID_W
    per_ctx = CONV_ROWS // G
    is_latent = pl.program_id(0) >= nctx_blk
    sub = lax.broadcasted_iota(jnp.int32, (8, 128), 0)

    def raw_piece(src, g, c):
        return src[g * G:(g + 1) * G, c * 128:(c + 1) * 128]

    def conv(src, g, c):
        cs = slice(c * 128, (c + 1) * 128)
        x = raw_piece(src, g, c)
        zero = jnp.zeros((1, 128), F32)
        before = zero if g % per_ctx == 0 else jnp.where(is_latent, 0.0, src[g * G - 1:g * G, cs])
        after = zero if g % per_ctx == per_ctx - 1 else jnp.where(is_latent, 0.0, src[(g + 1) * G:(g + 1) * G + 1, cs])
        rp = pltpu.roll(x, 1, 0)
        rn = pltpu.roll(x, G - 1, 0)
        prev = jnp.concatenate([jnp.where(sub == 0, before, rp[0:8]), rp[8:]], axis=0)
        nxt = jnp.concatenate([rn[:G - 8], jnp.where(sub == 7, after, rn[G - 8:])], axis=0)
        return (prev * par_ref[PAR_W0:PAR_W0 + 1, cs] + x * par_ref[PAR_W1:PAR_W1 + 1, cs]
                + nxt * par_ref[PAR_W2:PAR_W2 + 1, cs] + par_ref[PAR_BIAS:PAR_BIAS + 1, cs])

    def conv_silu(src, g, c):
        return _silu(conv(src, g, c))

    def conv_silu_l2(src, g, c):
        y = _silu(conv(src, g, c))
        y = y * lax.rsqrt(jnp.sum(y * y, axis=-1, keepdims=True) + EPS)
        return y * par_ref[PAR_L2SCALE:PAR_L2SCALE + 1, c * 128:(c + 1) * 128]

    def small(src, g, c):
        cs = slice(c * 128, (c + 1) * 128)
        acc = raw_piece(src, g, c)
        kind = par_ref[PAR_SKIND:PAR_SKIND + 1, cs]
        sp = _softplus(acc + par_ref[PAR_SBIAS:PAR_SBIAS + 1, cs])
        dec = -jnp.exp(par_ref[PAR_SALOG:PAR_SALOG + 1, cs]) * sp
        return jnp.where(kind == 0.0, sp, jnp.where(kind == 1.0, dec, jnp.where(kind == 2.0, _sigmoid(acc), 0.0)))

    rows_mm = 256
    tiles = IN_TN // 128
    piece_fn = {MODE_RAW: raw_piece, MODE_CONV: conv, MODE_CONV_SILU: conv_silu,
                MODE_CONV_SILU_L2: conv_silu_l2, MODE_SMALL: small}

    def project(dst, r):
        rs = slice(r * rows_mm, (r + 1) * rows_mm)
        dst[rs, :] = _dot(h_ref[rs, :], w_ref[...])

    def finish(src, blk, g, c):
        mode = _tile_mode(blk * tiles + c)
        y = piece_fn[mode](src, g, c)
        if mode == MODE_SMALL:
            sm_ref[g * G:(g + 1) * G, :] = y
            y = jnp.zeros_like(y)
        o_ref[g * G:(g + 1) * G, c * 128:(c + 1) * 128] = y.astype(o_ref.dtype)

    for step in range(nj + 1):
        @pl.when(j == step)
        def _(step=step):
            blk = step - 1
            src, dst = raws[blk % 2], raws[step % 2]
            for r in range(T // rows_mm):
                if step < nj:
                    project(dst, r)
                if blk < 0:
                    continue
                for g in range(r * rows_mm // G, (r + 1) * rows_mm // G):
                    for c in range(tiles):
                        finish(src, blk, g, c)


class Rows:
    def __init__(self, B, Lc, Ll):
        self.B, self.Lc, self.Ll = B, Lc, Ll
        self.NC = B * Lc
        self.R = B * Lc + B * Ll
        assert self.NC % Ll == 0 or Ll % self.NC == 0
        tm = 1024
        while self.NC % tm or Ll % tm:
            tm //= 2
        self.tm = tm

    def mod_index(self, tm):
        nctx = self.NC // tm
        per = self.Ll // tm
        return lambda i: jnp.where(i < nctx, 0, 1 + (i - nctx) // per)


def in_proj(rw, x, nw, mod, w, par):
    R, D = x.shape
    N = w.shape[1]
    tm, tn = rw.tm, IN_TN
    nj = N // tn
    assert N == N_IN_PAD
    mi = rw.mod_index(tm)
    done = lambda j: jnp.maximum(j - 1, 0)
    return pl.pallas_call(
        functools.partial(_in_proj_kernel, nctx_blk=rw.NC // tm),
        grid=(R // tm, nj + 1),
        in_specs=[
            pl.BlockSpec((tm, D), lambda i, j: (i, 0)),
            pl.BlockSpec((1, D), lambda i, j: (0, 0)),
            pl.BlockSpec((1, 8, D), lambda i, j: (mi(i), 0, 0)),
            pl.BlockSpec((D, tn), lambda i, j: (0, jnp.minimum(j, nj - 1))),
            pl.BlockSpec((8, tn), lambda i, j: (0, done(j))),
        ],
        out_specs=[pl.BlockSpec((tm, tn), lambda i, j: (i, done(j))),
                   pl.BlockSpec((tm, 128), lambda i, j: (i, 0))],
        out_shape=[jax.ShapeDtypeStruct((R, N), BF16), jax.ShapeDtypeStruct((R, 128), F32)],
        scratch_shapes=[pltpu.VMEM((tm, D), BF16), pltpu.VMEM((tm, tn), F32), pltpu.VMEM((tm, tn), F32)],
        compiler_params=_cp("arbitrary", "arbitrary"),
        name="in_proj",
    )(x, nw.reshape(1, D), mod, w, par)


def _in_proj_params(hy_conv_w, hy_conv_b, ssm_conv_w, ssm_conv_b, gdn_conv_w, ssm_dt_bias, gdn_dt_bias, gdn_A_log):
    def row(pieces):
        v = jnp.zeros((N_IN_PAD,), F32)
        for off, a in pieces:
            v = lax.dynamic_update_slice(v, a.astype(F32), (off,))
        return v
    z4 = jnp.zeros((4,), F32)
    conv = [row([(C_HY, hy_conv_w[t]), (C_XBC, ssm_conv_w[t]), (C_QKV, gdn_conv_w[t])]) for t in range(3)]
    bias = row([(C_HY, hy_conv_b), (C_XBC, ssm_conv_b)])
    l2s = row([(C_QKV, jnp.full((512,), GDN_DK ** -0.5, F32)), (C_QKV + 512, jnp.ones((512,), F32))])
    sbias = row([(C_SM, jnp.concatenate([ssm_dt_bias.reshape(16), gdn_dt_bias[0], z4, gdn_dt_bias[1], z4]))])
    salog = row([(C_SM + 16, jnp.concatenate([gdn_A_log[0], z4, gdn_A_log[1], z4]))])
    kind = np.full((N_IN_PAD,), 3.0, np.float32)
    kind[C_SM:C_SM + 16] = 0.0
    kind[C_SM + 16:C_SM + 20] = 1.0
    kind[C_SM + 24:C_SM + 28] = 1.0
    kind[C_SM + 20:C_SM + 24] = 2.0
    kind[C_SM + 28:C_SM + 32] = 2.0
    return jnp.stack(conv + [bias, l2s, sbias, salog, jnp.asarray(kind)], axis=0)


def _hy_filter_kernel(z_ref, w1_ref, b1_ref, w2_ref, b2_ref, w3_ref, f0_ref, f1_ref, win_ref, o_ref, h_ref):
    @pl.when(pl.program_id(1) == 0)
    def _():
        h1 = jnp.sin(f0_ref[...] * (_dot(z_ref[...], w1_ref[...], HI) + b1_ref[...]))
        h_ref[...] = jnp.sin(f1_ref[...] * (_dot(h1, w2_ref[...], HI) + b2_ref[...]))

    h = _dot(h_ref[...], w3_ref[...], HI) * win_ref[...]
    tl = h.shape[0]
    row = lax.broadcasted_iota(jnp.int32, (tl, 1), 0) + pl.program_id(0) * tl
    drop = (row == 0) & (pl.program_id(1) % 2 == 1)
    o_ref[...] = jnp.where(drop, 0.0, h).astype(o_ref.dtype)


def hy_features(L):
    t = jnp.linspace(0.0, 1.0, L, dtype=F32)[:, None]
    w = 2.0 * math.pi * jnp.arange(L, dtype=F32)[:, None] / L
    f = jnp.linspace(1e-4, HY_BANDS - 1, HY_BANDS, dtype=F32)[None, :]
    z = jnp.concatenate([t, jnp.cos(f * w), -jnp.sin(f * w)], axis=-1)
    z = jnp.pad(z, ((0, 0), (0, 128 - HY_EMB)))
    min_decay = math.log(HY_TARGET) / HY_LONG_DECAY_PCT
    max_decay = math.log(HY_TARGET) / HY_SHORT_DECAY_PCT
    deltas = jnp.linspace(min_decay, max_decay, HY_WIDTH, dtype=F32)
    window = jnp.exp(-t * jnp.abs(deltas))
    return z, window


def hy_filter(feat, w1, b1, w2, b2, w3, freq):
    z, window = feat
    L = z.shape[0]
    H = HY_HIDDEN
    w1p = jnp.pad(w1, ((0, 128 - HY_EMB), (0, 128 - H)))
    w2p = jnp.pad(w2, ((0, 128 - H), (0, 128 - H)))
    w3p = jnp.pad(w3, ((0, 128 - H), (0, 0)))
    pad1 = lambda v: jnp.pad(v, (0, 128 - H)).reshape(1, 128)
    tl = 256
    full = lambda shape: pl.BlockSpec(shape, lambda i, j: (0, 0))
    return pl.pallas_call(
        _hy_filter_kernel,
        grid=(L // tl, 4),
        in_specs=[
            pl.BlockSpec((tl, 128), lambda i, j: (i, 0)),
            full((128, 128)), full((1, 128)), full((128, 128)), full((1, 128)),
            pl.BlockSpec((128, HY_WIDTH), lambda i, j: (0, j)),
            full((1, 128)), full((1, 128)),
            pl.BlockSpec((tl, HY_WIDTH), lambda i, j: (i, 0)),
        ],
        out_specs=pl.BlockSpec((tl, HY_WIDTH), lambda i, j: (i, j)),
        out_shape=jax.ShapeDtypeStruct((L, 4 * HY_WIDTH), BF16),
        scratch_shapes=[pltpu.VMEM((tl, 128), F32)],
        compiler_params=_cp("arbitrary", "arbitrary"),
        name="hy_filter",
    )(z, w1p, pad1(b1), w2p, pad1(b2), w3p, pad1(freq[0]), pad1(freq[1]), window)


def dft_tables(L):
    N = 2 * L
    f = jnp.arange(L, dtype=jnp.int32)[:, None]
    s = jnp.arange(L, dtype=jnp.int32)[None, :]
    ang = ((f * s) % N).astype(F32) * (2.0 * math.pi / N)
    c, sn = jnp.cos(ang), jnp.sin(ang)
    alt = (1 - 2 * (s % 2)).astype(F32)
    first = f == 0
    fwd = jnp.concatenate([c, jnp.where(first, alt, -sn)], axis=0)
    wgt = jnp.where(first, 1.0, 2.0) / N
    inv = jnp.concatenate([(c * wgt).T, jnp.where(first, alt / N, -sn * wgt).T], axis=1)
    return fwd.astype(BF16), inv.astype(BF16)


def _matmul_kernel(a_ref, b_ref, o_ref):
    o_ref[...] = _dot(a_ref[...], b_ref[...])


def matmul(a, b, tm, tn):
    M, K = a.shape
    N = b.shape[1]
    return pl.pallas_call(
        _matmul_kernel,
        grid=(M // tm, N // tn),
        in_specs=[pl.BlockSpec((tm, K), lambda i, j: (i, 0)), pl.BlockSpec((K, tn), lambda i, j: (0, j))],
        out_specs=pl.BlockSpec((tm, tn), lambda i, j: (i, j)),
        out_shape=jax.ShapeDtypeStruct((M, N), F32),
        compiler_params=_cp("arbitrary", "arbitrary"),
        name="matmul",
    )(a, b)


def _long_conv_kernel(u_ref, g_ref, bias_ref, fr_ref, fi_ref, ic_ref, is_ref,
                      ar0_ref, ar1_ref, ai0_ref, ai1_ref, o_ref, ub_ref, acc_ref):
    f = pl.program_id(1)

    @pl.when(f == 0)
    def _():
        ub_ref[...] = u_ref[...].astype(BF16)
        acc_ref[...] = jnp.zeros_like(acc_ref)

    ub = ub_ref[...]
    ur = _dot(fr_ref[...], ub)
    ui = _dot(fi_ref[...], ub)
    kr = ar0_ref[...] + ar1_ref[...]
    nyq = (lax.broadcasted_iota(jnp.int32, (FREQ_BLK, 1), 0) == 0) & (f == 0)
    ki = jnp.where(nyq, ai0_ref[...] + ai1_ref[...], ai0_ref[...] - ai1_ref[...])
    pr = jnp.where(nyq, ur * kr, ur * kr - ui * ki)
    pi = jnp.where(nyq, ui * ki, ur * ki + ui * kr)
    acc_ref[...] += _dot(ic_ref[...], pr.astype(BF16)) + _dot(is_ref[...], pi.astype(BF16))

    @pl.when(f == pl.num_programs(1) - 1)
    def _():
        u = u_ref[...].astype(F32)
        o_ref[...] = (g_ref[...].astype(F32) * (acc_ref[...] + u * bias_ref[...])).astype(o_ref.dtype)


def long_conv(B, L, u, u_rb0, u_cb, gate, g_rb0, gate_cb, bias, fwd, inv, kspec, order, out_dtype):
    C = HY_WIDTH
    nfb = L // FREQ_BLK
    FB = FREQ_BLK
    return pl.pallas_call(
        _long_conv_kernel,
        grid=(B, nfb),
        in_specs=[
            pl.BlockSpec((L, C), lambda b, f: (u_rb0 + b, u_cb)),
            pl.BlockSpec((L, C), lambda b, f: (g_rb0 + b, gate_cb)),
            pl.BlockSpec((1, C), lambda b, f: (0, 0)),
            pl.BlockSpec((FB, L), lambda b, f: (f, 0)),
            pl.BlockSpec((FB, L), lambda b, f: (nfb + f, 0)),
            pl.BlockSpec((L, FB), lambda b, f: (0, f)),
            pl.BlockSpec((L, FB), lambda b, f: (0, nfb + f)),
            pl.BlockSpec((FB, C), lambda b, f: (f, 2 * order)),
            pl.BlockSpec((FB, C), lambda b, f: (f, 2 * order + 1)),
            pl.BlockSpec((FB, C), lambda b, f: (nfb + f, 2 * order)),
            pl.BlockSpec((FB, C), lambda b, f: (nfb + f, 2 * order + 1)),
        ],
        out_specs=pl.BlockSpec((L, C), lambda b, f: (b, 0)),
        out_shape=jax.ShapeDtypeStruct((B * L, C), out_dtype),
        scratch_shapes=[pltpu.VMEM((L, C), BF16), pltpu.VMEM((L, C), F32)],
        compiler_params=_cp("arbitrary", "arbitrary"),
        name="long_conv",
    )(u, gate, bias.reshape(1, C), fwd, fwd, inv, inv, kspec, kspec, kspec, kspec)


def _scan_blocks(rw, rows):
    nbc, nbl, base = rw.Lc // rows, rw.Ll // rows, rw.NC // rows

    def make(d):
        def f(b, s):
            jc = s if d == 0 else nbc - 1 - s
            jl = (s - nbc) if d == 0 else nbl - 1 - (s - nbc)
            return jnp.where(s < nbc, b * nbc + jc, base + b * nbl + jl)
        return f

    return [make(0), make(1)], nbc + nbl


def _expand_lanes(x, base, n, width):
    rows = x.shape[0]
    per = 128 // width
    lane = lax.broadcasted_iota(jnp.int32, (rows, 128), 1)
    tiles = []
    for t in range(n // per):
        c0 = base + t * per
        tile = jnp.broadcast_to(x[:, c0:c0 + 1], (rows, 128))
        for i in range(1, per):
            tile = jnp.where(lane >= i * width, jnp.broadcast_to(x[:, c0 + i:c0 + i + 1], (rows, 128)), tile)
        tiles.append(tile)
    return jnp.concatenate(tiles, axis=1)


def _ssd_kernel(xf, bf, cf, smf, dtf, xb, bb, cb_, smb, dtb, alx_ref, alc_ref, of_ref, ob_ref, h_ref):
    Q = SSM_CHUNK
    GW = SSM_GW

    @pl.when(pl.program_id(1) == 0)
    def _():
        h_ref[...] = jnp.zeros_like(h_ref)

    row = lax.broadcasted_iota(jnp.int32, (Q, Q), 0)
    col = lax.broadcasted_iota(jnp.int32, (Q, Q), 1)
    lane_head = lax.broadcasted_iota(jnp.int32, (Q, GW), 1) // SSM_HEAD_DIM
    dirs = ((xf, bf, cf, smf, dtf, of_ref), (xb, bb, cb_, smb, dtb, ob_ref))
    jobs = []
    for d in range(2):
        x_ref, b_ref, c_ref, sm_ref, dt_ref, o_ref = dirs[d]
        keep = (col <= row) if d == 0 else (col >= row)
        tri = keep.astype(BF16)
        tri_t = ((row <= col) if d == 0 else (row >= col)).astype(BF16)
        sm = sm_ref[...]
        a_x = -jnp.exp(alx_ref[d])
        dtx = _expand_lanes(sm, 8 * d, SSM_HEADS, SSM_HEAD_DIM)
        cumx = _expand_lanes(_dot_01_lhs(tri, sm), 8 * d, SSM_HEADS, SSM_HEAD_DIM) * a_x
        cumr = _dot_01_rhs(dt_ref[0], tri_t) * (-jnp.exp(alc_ref[d]))
        last = Q - 1 if d == 0 else 0
        totx = cumx[last:last + 1, :]
        xd = x_ref[...].astype(F32) * dtx
        xdw = xd * jnp.exp(totx - cumx)
        ecum = jnp.exp(cumx)
        for g in range(SSM_GROUPS):
            gs = slice(g * GW, (g + 1) * GW)
            jobs.append(dict(d=d, g=g, gs=gs, keep=keep, cumx=cumx, cumr=cumr, o_ref=o_ref,
                             bg=b_ref[:, g * SSM_STATE:(g + 1) * SSM_STATE].astype(BF16),
                             cg=c_ref[:, g * SSM_STATE:(g + 1) * SSM_STATE].astype(BF16),
                             xdg=xd[:, gs], xdw=xdw[:, gs].astype(BF16), ecum=ecum[:, gs],
                             etot=jnp.exp(totx[:, gs])))
    for j in jobs:
        j["cb"] = _dot_nt(j["cg"], j["bg"])
        j["h"] = h_ref[j["d"], j["g"]]
    for j in jobs:
        ms, xs = [], []
        for e4 in range(SSM_HPG):
            e = j["g"] * SSM_HPG + e4
            diff = j["cumx"][:, e * SSM_HEAD_DIM:e * SSM_HEAD_DIM + 1] - j["cumr"][e:e + 1, :]
            ms.append((j["cb"] * jnp.where(j["keep"], jnp.exp(diff), 0.0)).astype(BF16))
            xs.append(jnp.where(lane_head == e4, j["xdg"], 0.0).astype(BF16))
        yd = _dot(jnp.concatenate(ms, axis=1), jnp.concatenate(xs, axis=0))
        y_off = _dot(j["cg"], j["h"].astype(BF16)) * j["ecum"]
        j["o_ref"][:, j["gs"]] = yd + y_off
    for j in jobs:
        h_ref[j["d"], j["g"]] = j["h"] * j["etot"] + _dot_tn(j["bg"], j["xdw"])


def ssd_scan(rw, p, sm, dtT, alx, alc):
    Q = SSM_CHUNK
    blks, nsteps = _scan_blocks(rw, Q)
    R = p.shape[0]
    in_specs = []
    for d in range(2):
        f = blks[d]
        in_specs += [
            pl.BlockSpec((Q, 512), lambda b, s, f=f: (f(b, s), C_XBC // 512)),
            pl.BlockSpec((Q, 256), lambda b, s, f=f: (f(b, s), C_XBC // 256 + 2)),
            pl.BlockSpec((Q, 256), lambda b, s, f=f: (f(b, s), C_XBC // 256 + 3)),
            pl.BlockSpec((Q, 128), lambda b, s, f=f: (f(b, s), 0)),
            pl.BlockSpec((1, 8, Q), lambda b, s, f=f, d=d: (d, 0, f(b, s))),
        ]
    in_specs += [pl.BlockSpec((2, 1, 512), lambda b, s: (0, 0, 0)), pl.BlockSpec((2, 8, 1), lambda b, s: (0, 0, 0))]
    ops = (p, p, p, sm, dtT)
    return pl.pallas_call(
        _ssd_kernel,
        grid=(rw.B, nsteps),
        in_specs=in_specs,
        out_specs=[pl.BlockSpec((Q, 512), lambda b, s, f=blks[d]: (f(b, s), 0)) for d in range(2)],
        out_shape=[jax.ShapeDtypeStruct((R, 512), F32)] * 2,
        scratch_shapes=[pltpu.VMEM((2, SSM_GROUPS, SSM_STATE, SSM_GW), F32)],
        compiler_params=_cp("arbitrary", "arbitrary"),
        name="ssd_scan",
    )(*ops, *ops, alx, alc)


def _split3(x):
    x1 = x.astype(BF16)
    r = x - x1.astype(F32)
    x2 = r.astype(BF16)
    x3 = (r - x2.astype(F32)).astype(BF16)
    return x1, x2, x3


def _dot_01_lhs(m01, x):
    x1, x2, x3 = _split3(x)
    return _dot(m01, x1) + _dot(m01, x2) + _dot(m01, x3)


def _dot_01_rhs(x, m01):
    x1, x2, x3 = _split3(x)
    return _dot(x1, m01) + _dot(x2, m01) + _dot(x3, m01)


GDN_ROWS = 256


def _gdn_prep_kernel(q_ref, k_ref, v_ref, sm_ref, gT_ref, u_ref, w_ref, qg_ref, kd_ref, qk_ref, egl_ref):
    C = GDN_CHUNK
    row = lax.broadcasted_iota(jnp.int32, (C, C), 0)
    col = lax.broadcasted_iota(jnp.int32, (C, C), 1)
    jobs = []
    levels = []
    for d in range(2):
        keep = (col <= row) if d == 0 else (col >= row)
        late, early = (row, col) if d == 0 else (col, row)
        levels.append([(((row ^ col) >> (j + 1)) == 0) & ((late & (1 << j)) != 0) & ((early & (1 << j)) == 0)
                       for j in range(6)])
        tri = keep.astype(BF16)
        tri_t = ((row <= col) if d == 0 else (row >= col)).astype(BF16)
        last = C - 1 if d == 0 else 0
        for c in range(GDN_ROWS // C):
            rows = slice(c * C, (c + 1) * C)
            smc = sm_ref[rows, :]
            cums = _dot_01_lhs(tri, smc)
            cumr = _dot_01_rhs(gT_ref[c, 8 * d:8 * d + 8, :], tri_t)
            tot = cums[last:last + 1, :]
            for h in range(GDN_HEADS):
                lg = 16 + 8 * d + h
                jobs.append(dict(d=d, c=c, h=h, rows=rows, hs=slice(h * 128, (h + 1) * 128), keep=keep,
                                 gc=cums[:, lg:lg + 1], beta=smc[:, lg + 4:lg + 5],
                                 gl=tot[:, lg:lg + 1], gr=cumr[h:h + 1, :]))
    for j in jobs:
        q = q_ref[j["rows"], j["hs"]].astype(F32)
        k = k_ref[j["rows"], j["hs"]].astype(F32)
        j["dec"] = jnp.where(j["keep"], jnp.exp(j["gc"] - j["gr"]), 0.0)
        kb = k * j["beta"]
        both = _dot_nt(jnp.concatenate([kb, q], axis=0).astype(BF16), k.astype(BF16))
        j["a"] = both[:C] * j["dec"]
        j["n"] = -jnp.where(levels[j["d"]][0], j["a"], 0.0)
        qk_ref[j["d"], j["c"], j["h"]] = (both[C:] * j["dec"]).astype(BF16)
    for lev in range(1, 6):
        for j in jobs:
            l = jnp.where(levels[j["d"]][lev], j["a"], 0.0)
            j["y"] = l + _dot(l.astype(BF16), j["n"].astype(BF16))
        for j in jobs:
            j["n"] = j["n"] - j["y"] - _dot(j["n"].astype(BF16), j["y"].astype(BF16))
    for j in jobs:
        d, rows, hs, gc, gl, beta = j["d"], j["rows"], j["hs"], j["gc"], j["gl"], j["beta"]
        q = q_ref[rows, hs].astype(F32)
        k = k_ref[rows, hs].astype(F32)
        eg = jnp.exp(gc)
        rhs = jnp.concatenate([v_ref[rows, hs].astype(F32) * beta, k * beta * eg], axis=1)
        sol = rhs + _dot(j["n"].astype(BF16), rhs.astype(BF16))
        u_ref[d, rows, hs] = sol[:, :GDN_DV].astype(BF16)
        w_ref[d, rows, hs] = sol[:, GDN_DV:].astype(BF16)
        qg_ref[d, rows, hs] = (q * eg).astype(BF16)
        kd_ref[d, rows, hs] = (k * jnp.exp(gl - gc)).astype(BF16)
        egl_ref[d, j["c"], :, hs] = jnp.broadcast_to(jnp.exp(gl), (8, 128))


def gdn_prep(p, sm, gT):
    R = p.shape[0]
    T, C = GDN_ROWS, GDN_CHUNK
    nc = T // C
    col = lambda k: pl.BlockSpec((T, 512), lambda i: (i, C_QKV // 512 + k))
    dirrow = pl.BlockSpec((2, T, 512), lambda i: (0, i, 0))
    return pl.pallas_call(
        _gdn_prep_kernel,
        grid=(R // T,),
        in_specs=[col(0), col(1), col(2),
                  pl.BlockSpec((T, 128), lambda i: (i, 0)),
                  pl.BlockSpec((nc, 16, C), lambda i: (i, 0, 0))],
        out_specs=[dirrow, dirrow, dirrow, dirrow,
                   pl.BlockSpec((2, nc, GDN_HEADS, C, C), lambda i: (0, i, 0, 0, 0)),
                   pl.BlockSpec((2, nc, 8, 512), lambda i: (0, i, 0, 0))],
        out_shape=[jax.ShapeDtypeStruct((2, R, 512), BF16),
                   jax.ShapeDtypeStruct((2, R, 512), BF16),
                   jax.ShapeDtypeStruct((2, R, 512), BF16),
                   jax.ShapeDtypeStruct((2, R, 512), BF16),
                   jax.ShapeDtypeStruct((2, R // C, GDN_HEADS, C, C), BF16),
                   jax.ShapeDtypeStruct((2, R // C, 8, 512), F32)],
        compiler_params=_cp("arbitrary"),
        name="gdn_prep",
    )(p, p, p, sm, gT)


def _gdn_scan_kernel(uf, wf, qgf, kdf, qkf, eglf, ub, wb, qgb, kdb, qkb, eglb, of_ref, ob_ref, s_ref):
    C = GDN_CHUNK
    nch = GDN_ROWS // C

    @pl.when(pl.program_id(1) == 0)
    def _():
        s_ref[...] = jnp.zeros_like(s_ref)

    dirs = ((uf, wf, qgf, kdf, qkf, eglf, of_ref), (ub, wb, qgb, kdb, qkb, eglb, ob_ref))
    chains = [(d, h) for d in range(2) for h in range(GDN_HEADS)]
    S = {ch: s_ref[ch[0], ch[1]] for ch in chains}
    for i in range(nch):
        Sb, vnb, rows_of, c_of = {}, {}, {}, {}
        for d, h in chains:
            c_of[d] = i if d == 0 else nch - 1 - i
            rows_of[d] = slice(c_of[d] * C, (c_of[d] + 1) * C)
        for d, h in chains:
            hs = slice(h * 128, (h + 1) * 128)
            Sb[d, h] = S[d, h].astype(BF16)
            v_new = dirs[d][0][0, rows_of[d], hs].astype(F32) - _dot(dirs[d][1][0, rows_of[d], hs], Sb[d, h])
            vnb[d, h] = v_new.astype(BF16)
        for d, h in chains:
            hs = slice(h * 128, (h + 1) * 128)
            u_ref, w_ref, qg_ref, kd_ref, qk_ref, egl_ref, o_ref = dirs[d]
            S[d, h] = S[d, h] * egl_ref[0, c_of[d], 0:1, hs] + _dot_tn(kd_ref[0, rows_of[d], hs], vnb[d, h])
        for d, h in chains:
            hs = slice(h * 128, (h + 1) * 128)
            u_ref, w_ref, qg_ref, kd_ref, qk_ref, egl_ref, o_ref = dirs[d]
            o_ref[rows_of[d], hs] = (_dot(qg_ref[0, rows_of[d], hs], Sb[d, h])
                                     + _dot(qk_ref[0, c_of[d], h], vnb[d, h]))
    for ch in chains:
        s_ref[ch[0], ch[1]] = S[ch]


def gdn_scan(rw, u, w, qg, kd, qk, egl):
    T, C = GDN_ROWS, GDN_CHUNK
    nc = T // C
    R = u.shape[1]
    nbc, nbl, base = rw.Lc // T, rw.Ll // T, rw.NC // T

    def blk(d):
        def f(b, s):
            jc = s if d == 0 else nbc - 1 - s
            jl = (s - nbc) if d == 0 else nbl - 1 - (s - nbc)
            return jnp.where(s < nbc, b * nbc + jc, base + b * nbl + jl)
        return f

    in_specs = []
    for d in range(2):
        f = blk(d)
        rowspec = pl.BlockSpec((1, T, 512), lambda b, s, f=f, d=d: (d, f(b, s), 0))
        in_specs += [rowspec, rowspec, rowspec, rowspec,
                     pl.BlockSpec((1, nc, GDN_HEADS, C, C), lambda b, s, f=f, d=d: (d, f(b, s), 0, 0, 0)),
                     pl.BlockSpec((1, nc, 8, 512), lambda b, s, f=f, d=d: (d, f(b, s), 0, 0))]
    out_specs = [pl.BlockSpec((T, 512), lambda b, s, f=blk(d): (f(b, s), 0)) for d in range(2)]
    ops = (u, w, qg, kd, qk, egl)
    return pl.pallas_call(
        _gdn_scan_kernel,
        grid=(rw.B, nbc + nbl),
        in_specs=in_specs,
        out_specs=out_specs,
        out_shape=[jax.ShapeDtypeStruct((R, 512), F32)] * 2,
        scratch_shapes=[pltpu.VMEM((2, GDN_HEADS, GDN_DK, GDN_DV), F32)],
        compiler_params=_cp("arbitrary", "arbitrary"),
        name="gdn_scan",
    )(*ops, *ops)


def _merge_kernel(yh_ref, sf_ref, sb_ref, sx_ref, sz_ref, dx_ref, snw_ref, gf_ref, gb_ref, gg_ref, gnw_ref,
                  g0_ref, g1_ref, g2_ref, w0_ref, w1_ref, w2_ref, wo_ref, x_ref, mod_ref, o_ref, ys_ref, yg_ref):
    tm = x_ref.shape[0]
    rp = 64
    for r in range(tm // rp):
        rs = slice(r * rp, (r + 1) * rp)
        y = sf_ref[rs, :] + sb_ref[rs, :] + sx_ref[rs, :].astype(F32) * dx_ref[...]
        y = y * _silu(sz_ref[rs, :].astype(F32))
        parts = []
        for g in range(SSM_GROUPS):
            yg = y[:, g * SSM_GW:(g + 1) * SSM_GW]
            parts.append(yg * lax.rsqrt(jnp.mean(yg * yg, axis=-1, keepdims=True) + EPS))
        ys_ref[rs, :] = (jnp.concatenate(parts, axis=1) * snw_ref[...]).astype(BF16)
        o = gf_ref[rs, :] + gb_ref[rs, :]
        parts = []
        for h in range(GDN_HEADS):
            oh = o[:, h * 128:(h + 1) * 128]
            parts.append(oh * lax.rsqrt(jnp.mean(oh * oh, axis=-1, keepdims=True) + EPS))
        yg_ref[rs, :] = (jnp.concatenate(parts, axis=1) * gnw_ref[...]
                         * _silu(gg_ref[rs, :].astype(F32))).astype(BF16)
    m = (_sigmoid(g0_ref[...].astype(F32)) * _dot(yh_ref[...], w0_ref[...])
         + _sigmoid(g1_ref[...].astype(F32)) * _dot(ys_ref[...], w1_ref[...])
         + _sigmoid(g2_ref[...].astype(F32)) * _dot(yg_ref[...], w2_ref[...]))
    o_ref[...] = x_ref[...] + mod_ref[0, 2:3, :] * _dot(m.astype(BF16), wo_ref[...])


def merge(rw, yh, y_f, y_b, dx, ssm_nw, o_f, o_b, gdn_nw, p, w0, w1, w2, wo, x, mod):
    R, D = x.shape
    tm = min(rw.tm, 512)
    mi = rw.mod_index(tm)
    yspec = pl.BlockSpec((tm, 512), lambda i: (i, 0))
    pspec = lambda col: pl.BlockSpec((tm, 512), lambda i: (i, col // 512))
    vec = pl.BlockSpec((1, 512), lambda i: (0, 0))
    gspec = lambda k: pl.BlockSpec((tm, D), lambda i: (i, C_GATE // D + k))
    wspec = pl.BlockSpec((512, D), lambda i: (0, 0))
    return pl.pallas_call(
        _merge_kernel,
        grid=(R // tm,),
        in_specs=[yspec,
                  yspec, yspec, pspec(C_XBC), pspec(C_Z), vec, vec,
                  yspec, yspec, pspec(C_GG), vec,
                  gspec(0), gspec(1), gspec(2), wspec, wspec, wspec,
                  pl.BlockSpec((D, D), lambda i: (0, 0)),
                  pl.BlockSpec((tm, D), lambda i: (i, 0)),
                  pl.BlockSpec((1, 8, D), lambda i: (mi(i), 0, 0))],
        out_specs=pl.BlockSpec((tm, D), lambda i: (i, 0)),
        out_shape=jax.ShapeDtypeStruct((R, D), F32),
        scratch_shapes=[pltpu.VMEM((tm, 512), BF16), pltpu.VMEM((tm, 512), BF16)],
        compiler_params=_cp("arbitrary"),
        name="merge",
    )(yh, y_f, y_b, p, p, dx, ssm_nw, o_f, o_b, p, gdn_nw, p, p, p, w0, w1, w2, wo, x, mod)


def _swiglu_up_kernel(x_ref, nw_ref, mod_ref, wg_ref, wu_ref, o_ref, h_ref):
    @pl.when(pl.program_id(1) == 0)
    def _():
        h = _norm_mod(x_ref[...], nw_ref[...], mod_ref[0, 4:5, :], mod_ref[0, 3:4, :])
        h_ref[...] = h.astype(BF16)

    h = h_ref[...]
    g = _dot(h, wg_ref[...])
    u = _dot(h, wu_ref[...])
    o_ref[...] = (_silu(g) * u).astype(o_ref.dtype)


def swiglu_up(rw, x, nw, mod, wgu):
    R, D = x.shape
    tm = rw.tm
    tn = D_FF // 2
    nj = D_FF // tn
    mi = rw.mod_index(tm)
    return pl.pallas_call(
        _swiglu_up_kernel,
        grid=(R // tm, nj),
        in_specs=[
            pl.BlockSpec((tm, D), lambda i, j: (i, 0)),
            pl.BlockSpec((1, D), lambda i, j: (0, 0)),
            pl.BlockSpec((1, 8, D), lambda i, j: (mi(i), 0, 0)),
            pl.BlockSpec((D, tn), lambda i, j: (0, j)),
            pl.BlockSpec((D, tn), lambda i, j: (0, nj + j)),
        ],
        out_specs=pl.BlockSpec((tm, tn), lambda i, j: (i, j)),
        out_shape=jax.ShapeDtypeStruct((R, D_FF), BF16),
        scratch_shapes=[pltpu.VMEM((tm, D), BF16)],
        compiler_params=_cp("arbitrary", "arbitrary"),
        name="swiglu_up",
    )(x, nw.reshape(1, D), mod, wgu, wgu)


def _swiglu_down_kernel(a_ref, w_ref, x_ref, mod_ref, o_ref):
    o_ref[...] = x_ref[...] + mod_ref[0, 5:6, :] * _dot(a_ref[...], w_ref[...])


def swiglu_down(rw, a, w, x, mod):
    R, D = x.shape
    tm = min(rw.tm, 512)
    mi = rw.mod_index(tm)
    return pl.pallas_call(
        _swiglu_down_kernel,
        grid=(R // tm,),
        in_specs=[
            pl.BlockSpec((tm, D_FF), lambda i: (i, 0)),
            pl.BlockSpec((D_FF, D), lambda i: (0, 0)),
            pl.BlockSpec((tm, D), lambda i: (i, 0)),
            pl.BlockSpec((1, 8, D), lambda i: (mi(i), 0, 0)),
        ],
        out_specs=pl.BlockSpec((tm, D), lambda i: (i, 0)),
        out_shape=jax.ShapeDtypeStruct((R, D), F32),
        compiler_params=_cp("arbitrary"),
        name="swiglu_down",
    )(a, w, x, mod)


def _final_norm_kernel(x_ref, w_ref, o_ref):
    x = x_ref[...]
    ms = jnp.mean(x * x, axis=-1, keepdims=True)
    o_ref[...] = x * lax.rsqrt(ms + EPS) * w_ref[...]


def final_norm(rw, x, w):
    D = x.shape[1]
    tm = rw.tm
    n0 = rw.NC // tm
    nl = rw.B * rw.Ll
    return pl.pallas_call(
        _final_norm_kernel,
        grid=(nl // tm,),
        in_specs=[pl.BlockSpec((tm, D), lambda i: (n0 + i, 0)), pl.BlockSpec((1, D), lambda i: (0, 0))],
        out_specs=pl.BlockSpec((tm, D), lambda i: (i, 0)),
        out_shape=jax.ShapeDtypeStruct((nl, D), F32),
        compiler_params=_cp("arbitrary"),
        name="final_norm",
    )(x, w.reshape(1, D))


def _regroup_w_in(w_in):
    o_dt = 3072
    o_gdn = 3088
    o_a = o_gdn + 2048
    o_b = o_a + 8
    o_gate = o_gdn + 2064
    pieces = [
        w_in[..., 0:3072],
        w_in[..., o_gdn:o_gdn + 2048],
        w_in[..., o_gate:o_gate + 3072],
        w_in[..., o_dt:o_dt + 16],
        w_in[..., o_a:o_a + 4], w_in[..., o_b:o_b + 4],
        w_in[..., o_a + 4:o_a + 8], w_in[..., o_b + 4:o_b + 8],
        jnp.zeros(w_in.shape[:-1] + (N_IN_PAD - C_SM - 32,), w_in.dtype),
    ]
    return jnp.concatenate(pieces, axis=-1).astype(BF16)


def kernel(x, c, ctx, c_ctx, w_ada, b_ada, norm1_w, norm2_w, w_in, hy_conv_w, hy_conv_b, hy_w1, hy_b1, hy_w2, hy_b2, hy_w3, hy_freq, hy_bias, ssm_conv_w, ssm_conv_b, ssm_dt_bias, ssm_A_log, ssm_D, ssm_norm_w, gdn_conv_w, gdn_dt_bias, gdn_A_log, gdn_norm_w, w_hy_out, w_ssm_out, w_gdn_out, w_out, w_gate_up, w_down, final_norm_w):
    B, Ll, D = x.shape
    Lc = ctx.shape[1]
    depth = w_ada.shape[0]
    assert Lc == CONV_ROWS and D == D_MODEL and B <= 15
    rw = Rows(B, Lc, Ll)
    R, NC = rw.R, rw.NC

    xa = jnp.concatenate([ctx.reshape(B * Lc, D), x.reshape(B * Ll, D)], axis=0)

    svec = jnp.concatenate([c_ctx[None, :], c, jnp.zeros((15 - B, D), F32)], axis=0)
    mod_all = ada_modulation(svec, w_ada, b_ada)
    mod_all = jnp.pad(mod_all.reshape(depth, 16, 6, D), ((0, 0), (0, 0), (0, 2), (0, 0)))

    w_in_r = _regroup_w_in(w_in)
    fwd_l, inv_l = dft_tables(Ll)
    fwd_c, inv_c = dft_tables(Lc)
    feat_l, feat_c = hy_features(Ll), hy_features(Lc)

    for l in range(depth):
        mod = mod_all[l]
        par = _in_proj_params(hy_conv_w[l], hy_conv_b[l], ssm_conv_w[l], ssm_conv_b[l], gdn_conv_w[l],
                              ssm_dt_bias[l], gdn_dt_bias[l], gdn_A_log[l])
        p, sm = in_proj(rw, xa, norm1_w[l], mod, w_in_r[l], par)

        sm32_t = sm[:, :32].T
        dt_t = sm32_t[:16].reshape(2, 8, R)
        g_t = sm32_t[16:32].reshape(16, R // GDN_CHUNK, GDN_CHUNK).transpose(1, 0, 2)

        alx = jnp.repeat(ssm_A_log[l], SSM_HEAD_DIM, axis=-1).reshape(2, 1, 512)
        alc = ssm_A_log[l].reshape(2, 8, 1)
        y_f, y_b = ssd_scan(rw, p, sm, dt_t, alx, alc)
        dx = jnp.repeat(ssm_D[l], SSM_HEAD_DIM).reshape(1, 512)

        o_f, o_b = gdn_scan(rw, *gdn_prep(p, sm, g_t))

        hyu = p
        parts = []
        for (Bn, L, blk0, fwd, inv, feat) in ((B, Lc, 0, fwd_c, inv_c, feat_c),
                                              (B, Ll, NC // Ll, fwd_l, inv_l, feat_l)):
            if NC % L:
                raise ValueError("latent length must divide the context row count")
            filt = hy_filter(feat, hy_w1[l], hy_b1[l], hy_w2[l], hy_b2[l], hy_w3[l], hy_freq[l])
            kspec = matmul(fwd, filt, min(512, 2 * L), 512)
            z1 = long_conv(Bn, L, hyu, blk0, 0, hyu, blk0, 1, hy_bias[l, 0], fwd, inv, kspec, 0, F32)
            yy = long_conv(Bn, L, z1, 0, 0, hyu, blk0, 2, hy_bias[l, 1], fwd, inv, kspec, 1, BF16)
            parts.append(yy)
        y_hy = jnp.concatenate(parts, axis=0)

        xa = merge(rw, y_hy, y_f, y_b, dx, ssm_norm_w[l].reshape(1, 512),
                   o_f, o_b, jnp.tile(gdn_norm_w[l], GDN_HEADS).reshape(1, 512), p,
                   w_hy_out[l].astype(BF16), w_ssm_out[l].astype(BF16),
                   w_gdn_out[l].astype(BF16), w_out[l].astype(BF16), xa, mod)
        act = swiglu_up(rw, xa, norm2_w[l], mod, w_gate_up[l].astype(BF16))
        xa = swiglu_down(rw, act, w_down[l].astype(BF16), xa, mod)

    out = final_norm(rw, xa, final_norm_w)
    return out.reshape(B, Ll, D)
```

```python
import functools
import math

import jax
import jax.numpy as jnp
import numpy as np
from jax import lax
from jax.experimental import pallas as pl
from jax.experimental.pallas import tpu as pltpu

F32 = jnp.float32
BF16 = jnp.bfloat16
HI = lax.Precision.HIGHEST

EPS = 1e-6
D_MODEL = 1024
GRID_W = 64

HY_WIDTH = 512
HY_BANDS = 16
HY_EMB = 1 + 2 * HY_BANDS
HY_HIDDEN = 64
HY_SHORT_DECAY_PCT = 0.3
HY_LONG_DECAY_PCT = 1.5
HY_TARGET = 1e-2

SSM_HEADS = 8
SSM_HEAD_DIM = 64
SSM_WIDTH = 512
SSM_GROUPS = 2
SSM_HPG = 4
SSM_STATE = 128
SSM_CHUNK = 128
SSM_GW = SSM_HPG * SSM_HEAD_DIM

GDN_HEADS = 4
GDN_DK = 128
GDN_DV = 128
GDN_CHUNK = 64

D_FF = 2816

C_HY = 0
C_Z = 1536
C_XBC = 2048
C_QKV = 3072
C_GG = 4608
C_GATE = 5120
C_SM = 8192

CONV_ROWS = 256
FREQ_BLK = 512

VMEM_LIMIT = 56 * 1024 * 1024


def _cp(*sem, flags=None):
    return pltpu.CompilerParams(dimension_semantics=sem, vmem_limit_bytes=VMEM_LIMIT, flags=flags)


def _sigmoid(x):
    return 1.0 / (1.0 + jnp.exp(-x))


def _silu(x):
    return x * _sigmoid(x)


def _softplus(x):
    return jnp.maximum(x, 0.0) + jnp.log1p(jnp.exp(-jnp.abs(x)))


def _dot(a, b, precision=None):
    return jnp.dot(a, b, precision=precision, preferred_element_type=F32)


def _dot_nt(a, b):
    return lax.dot_general(a, b, (((1,), (1,)), ((), ())), preferred_element_type=F32)


def _dot_tn(a, b):
    return lax.dot_general(a, b, (((0,), (0,)), ((), ())), preferred_element_type=F32)


def _ada_kernel(s_ref, w_ref, b_ref, o_ref):
    s = _silu(s_ref[...])
    o_ref[0] = _dot(s, w_ref[0], HI) + b_ref[0]


def ada_modulation(svec, w_ada, b_ada):
    depth = w_ada.shape[0]
    D = D_MODEL
    return pl.pallas_call(
        _ada_kernel,
        grid=(depth, 6),
        in_specs=[
            pl.BlockSpec((16, D), lambda l, j: (0, 0)),
            pl.BlockSpec((1, D, D), lambda l, j: (l, 0, j)),
            pl.BlockSpec((1, 1, D), lambda l, j: (l, 0, j)),
        ],
        out_specs=pl.BlockSpec((1, 16, D), lambda l, j: (l, 0, j)),
        out_shape=jax.ShapeDtypeStruct((depth, 16, 6 * D), F32),
        compiler_params=_cp("arbitrary", "arbitrary"),
        name="ada",
    )(svec, w_ada, b_ada.reshape(depth, 1, 6 * D))


def _norm_mod(x, nw, scale, shift):
    ms = jnp.mean(x * x, axis=-1, keepdims=True)
    return (x * lax.rsqrt(ms + EPS) * nw) * (1.0 + scale) + shift


N_IN_PAD = C_SM + 128
IN_TN = N_IN_PAD // 5
MODE_RAW, MODE_CONV, MODE_CONV_SILU, MODE_CONV_SILU_L2, MODE_SMALL = range(5)


def _tile_mode(tile):
    col = tile * 128
    if col < C_Z:
        return MODE_CONV
    if col < C_XBC:
        return MODE_RAW
    if col < C_QKV:
        return MODE_CONV_SILU
    if col < C_QKV + 1024:
        return MODE_CONV_SILU_L2
    if col < C_GG:
        return MODE_CONV_SILU
    if col < C_SM:
        return MODE_RAW
    return MODE_SMALL
PAR_W0, PAR_W1, PAR_W2, PAR_BIAS, PAR_L2SCALE, PAR_SBIAS, PAR_SALOG, PAR_SKIND = range(8)


def _in_proj_kernel(x_ref, nw_ref, mod_ref, w_ref, par_ref, o_ref, sm_ref, h_ref, raw0_ref, raw1_ref, *, nctx_blk):
    j = pl.program_id(1)
    nj = N_IN_PAD // IN_TN
    raws = (raw0_ref, raw1_ref)

    @pl.when(j == 0)
    def _():
        h = _norm_mod(x_ref[...], nw_ref[...], mod_ref[0, 1:2, :], mod_ref[0, 0:1, :])
        h_ref[...] = h.astype(BF16)

    T = h_ref.shape[0]
    G = GRID_W
    per_ctx = CONV_ROWS // G
    is_latent = pl.program_id(0) >= nctx_blk
    sub = lax.broadcasted_iota(jnp.int32, (8, 128), 0)

    def raw_piece(src, g, c):
        return src[g * G:(g + 1) * G, c * 128:(c + 1) * 128]

    def conv(src, g, c):
        cs = slice(c * 128, (c + 1) * 128)
        x = raw_piece(src, g, c)
        zero = jnp.zeros((1, 128), F32)
        before = zero if g % per_ctx == 0 else jnp.where(is_latent, 0.0, src[g * G - 1:g * G, cs])
        after = zero if g % per_ctx == per_ctx - 1 else jnp.where(is_latent, 0.0, src[(g + 1) * G:(g + 1) * G + 1, cs])
        rp = pltpu.roll(x, 1, 0)
        rn = pltpu.roll(x, G - 1, 0)
        prev = jnp.concatenate([jnp.where(sub == 0, before, rp[0:8]), rp[8:]], axis=0)
        nxt = jnp.concatenate([rn[:G - 8], jnp.where(sub == 7, after, rn[G - 8:])], axis=0)
        return (prev * par_ref[PAR_W0:PAR_W0 + 1, cs] + x * par_ref[PAR_W1:PAR_W1 + 1, cs]
                + nxt * par_ref[PAR_W2:PAR_W2 + 1, cs] + par_ref[PAR_BIAS:PAR_BIAS + 1, cs])

    def conv_silu(src, g, c):
        return _silu(conv(src, g, c))

    def conv_silu_l2(src, g, c):
        y = _silu(conv(src, g, c))
        y = y * lax.rsqrt(jnp.sum(y * y, axis=-1, keepdims=True) + EPS)
        return y * par_ref[PAR_L2SCALE:PAR_L2SCALE + 1, c * 128:(c + 1) * 128]

    def small(src, g, c):
        cs = slice(c * 128, (c + 1) * 128)
        acc = raw_piece(src, g, c)
        kind = par_ref[PAR_SKIND:PAR_SKIND + 1, cs]
        sp = _softplus(acc + par_ref[PAR_SBIAS:PAR_SBIAS + 1, cs])
        dec = -jnp.exp(par_ref[PAR_SALOG:PAR_SALOG + 1, cs]) * sp
        return jnp.where(kind == 0.0, sp, jnp.where(kind == 1.0, dec, jnp.where(kind == 2.0, _sigmoid(acc), 0.0)))

    rows_mm = 256
    tiles = IN_TN // 128
    piece_fn = {MODE_RAW: raw_piece, MODE_CONV: conv, MODE_CONV_SILU: conv_silu,
                MODE_CONV_SILU_L2: conv_silu_l2, MODE_SMALL: small}

    def project(dst, r):
        rs = slice(r * rows_mm, (r + 1) * rows_mm)
        dst[rs, :] = _dot(h_ref[rs, :], w_ref[...])

    def finish(src, blk, g, c):
        mode = _tile_mode(blk * tiles + c)
        y = piece_fn[mode](src, g, c)
        if mode == MODE_SMALL:
            sm_ref[g * G:(g + 1) * G, :] = y
            y = jnp.zeros_like(y)
        o_ref[g * G:(g + 1) * G, c * 128:(c + 1) * 128] = y.astype(o_ref.dtype)

    for step in range(nj + 1):
        @pl.when(j == step)
        def _(step=step):
            blk = step - 1
            src, dst = raws[blk % 2], raws[step % 2]
            for r in range(T // rows_mm):
                if step < nj:
                    project(dst, r)
                if blk < 0:
                    continue
                for g in range(r * rows_mm // G, (r + 1) * rows_mm // G):
                    for c in range(tiles):
                        finish(src, blk, g, c)


class Rows:
    def __init__(self, B, Lc, Ll):
        self.B, self.Lc, self.Ll = B, Lc, Ll
        self.NC = B * Lc
        self.R = B * Lc + B * Ll
        assert self.NC % Ll == 0 or Ll % self.NC == 0
        tm = 1024
        while self.NC % tm or Ll % tm:
            tm //= 2
        self.tm = tm

    def mod_index(self, tm):
        nctx = self.NC // tm
        per = self.Ll // tm
        return lambda i: jnp.where(i < nctx, 0, 1 + (i - nctx) // per)


def in_proj(rw, x, nw, mod, w, par):
    R, D = x.shape
    N = w.shape[1]
    tm, tn = rw.tm, IN_TN
    nj = N // tn
    assert N == N_IN_PAD
    mi = rw.mod_index(tm)
    done = lambda j: jnp.maximum(j - 1, 0)
    return pl.pallas_call(
        functools.partial(_in_proj_kernel, nctx_blk=rw.NC // tm),
        grid=(R // tm, nj + 1),
        in_specs=[
            pl.BlockSpec((tm, D), lambda i, j: (i, 0)),
            pl.BlockSpec((1, D), lambda i, j: (0, 0)),
            pl.BlockSpec((1, 8, D), lambda i, j: (mi(i), 0, 0)),
            pl.BlockSpec((D, tn), lambda i, j: (0, jnp.minimum(j, nj - 1))),
            pl.BlockSpec((8, tn), lambda i, j: (0, done(j))),
        ],
        out_specs=[pl.BlockSpec((tm, tn), lambda i, j: (i, done(j))),
                   pl.BlockSpec((tm, 128), lambda i, j: (i, 0))],
        out_shape=[jax.ShapeDtypeStruct((R, N), BF16), jax.ShapeDtypeStruct((R, 128), F32)],
        scratch_shapes=[pltpu.VMEM((tm, D), BF16), pltpu.VMEM((tm, tn), F32), pltpu.VMEM((tm, tn), F32)],
        compiler_params=_cp("arbitrary", "arbitrary"),
        name="in_proj",
    )(x, nw.reshape(1, D), mod, w, par)


def _in_proj_params(hy_conv_w, hy_conv_b, ssm_conv_w, ssm_conv_b, gdn_conv_w, ssm_dt_bias, gdn_dt_bias, gdn_A_log):
    def row(pieces):
        out, pos = [], 0
        for off, a in pieces:
            out += [jnp.zeros((off - pos,), F32), a.astype(F32)]
            pos = off + a.shape[0]
        return jnp.concatenate(out + [jnp.zeros((N_IN_PAD - pos,), F32)])
    z4 = jnp.zeros((4,), F32)
    conv = [row([(C_HY, hy_conv_w[t]), (C_XBC, ssm_conv_w[t]), (C_QKV, gdn_conv_w[t])]) for t in range(3)]
    bias = row([(C_HY, hy_conv_b), (C_XBC, ssm_conv_b)])
    l2s = row([(C_QKV, jnp.full((512,), GDN_DK ** -0.5, F32)), (C_QKV + 512, jnp.ones((512,), F32))])
    sbias = row([(C_SM, jnp.concatenate([ssm_dt_bias.reshape(16), gdn_dt_bias[0], z4, gdn_dt_bias[1], z4]))])
    salog = row([(C_SM + 16, jnp.concatenate([gdn_A_log[0], z4, gdn_A_log[1], z4]))])
    kind = np.full((N_IN_PAD,), 3.0, np.float32)
    kind[C_SM:C_SM + 16] = 0.0
    kind[C_SM + 16:C_SM + 20] = 1.0
    kind[C_SM + 24:C_SM + 28] = 1.0
    kind[C_SM + 20:C_SM + 24] = 2.0
    kind[C_SM + 28:C_SM + 32] = 2.0
    return jnp.stack(conv + [bias, l2s, sbias, salog, jnp.asarray(kind)], axis=0)


def _hy_filter_kernel(z_ref, w1_ref, b1_ref, w2_ref, b2_ref, w3_ref, f0_ref, f1_ref, win_ref, o_ref, h_ref):
    @pl.when(pl.program_id(1) == 0)
    def _():
        h1 = jnp.sin(f0_ref[...] * (_dot(z_ref[...], w1_ref[...], HI) + b1_ref[...]))
        h_ref[...] = jnp.sin(f1_ref[...] * (_dot(h1, w2_ref[...], HI) + b2_ref[...]))

    h = _dot(h_ref[...], w3_ref[...], HI) * win_ref[...]
    tl = h.shape[0]
    row = lax.broadcasted_iota(jnp.int32, (tl, 1), 0) + pl.program_id(0) * tl
    drop = (row == 0) & (pl.program_id(1) % 2 == 1)
    o_ref[...] = jnp.where(drop, 0.0, h).astype(o_ref.dtype)


def hy_features(L):
    t = jnp.linspace(0.0, 1.0, L, dtype=F32)[:, None]
    w = 2.0 * math.pi * jnp.arange(L, dtype=F32)[:, None] / L
    f = jnp.linspace(1e-4, HY_BANDS - 1, HY_BANDS, dtype=F32)[None, :]
    z = jnp.concatenate([t, jnp.cos(f * w), -jnp.sin(f * w)], axis=-1)
    z = jnp.pad(z, ((0, 0), (0, 128 - HY_EMB)))
    min_decay = math.log(HY_TARGET) / HY_LONG_DECAY_PCT
    max_decay = math.log(HY_TARGET) / HY_SHORT_DECAY_PCT
    deltas = jnp.linspace(min_decay, max_decay, HY_WIDTH, dtype=F32)
    window = jnp.exp(-t * jnp.abs(deltas))
    return z, window


def hy_filter(feat, w1, b1, w2, b2, w3, freq):
    z, window = feat
    L = z.shape[0]
    H = HY_HIDDEN
    w1p = jnp.pad(w1, ((0, 128 - HY_EMB), (0, 128 - H)))
    w2p = jnp.pad(w2, ((0, 128 - H), (0, 128 - H)))
    w3p = jnp.pad(w3, ((0, 128 - H), (0, 0)))
    pad1 = lambda v: jnp.pad(v, (0, 128 - H)).reshape(1, 128)
    tl = 256
    full = lambda shape: pl.BlockSpec(shape, lambda i, j: (0, 0))
    return pl.pallas_call(
        _hy_filter_kernel,
        grid=(L // tl, 4),
        in_specs=[
            pl.BlockSpec((tl, 128), lambda i, j: (i, 0)),
            full((128, 128)), full((1, 128)), full((128, 128)), full((1, 128)),
            pl.BlockSpec((128, HY_WIDTH), lambda i, j: (0, j)),
            full((1, 128)), full((1, 128)),
            pl.BlockSpec((tl, HY_WIDTH), lambda i, j: (i, 0)),
        ],
        out_specs=pl.BlockSpec((tl, HY_WIDTH), lambda i, j: (i, j)),
        out_shape=jax.ShapeDtypeStruct((L, 4 * HY_WIDTH), BF16),
        scratch_shapes=[pltpu.VMEM((tl, 128), F32)],
        compiler_params=_cp("arbitrary", "arbitrary"),
        name="hy_filter",
    )(z, w1p, pad1(b1), w2p, pad1(b2), w3p, pad1(freq[0]), pad1(freq[1]), window)


def dft_tables(L):
    N = 2 * L
    f = jnp.arange(L, dtype=jnp.int32)[:, None]
    s = jnp.arange(L, dtype=jnp.int32)[None, :]
    ang = ((f * s) % N).astype(F32) * (2.0 * math.pi / N)
    c, sn = jnp.cos(ang), jnp.sin(ang)
    alt_s = (1 - 2 * (s % 2)).astype(F32)
    alt_t = (1 - 2 * (f % 2)).astype(F32)
    fwd = jnp.concatenate([c, jnp.where(f == 0, alt_s, -sn)], axis=0)
    wgt = jnp.where(s == 0, 1.0, 2.0) / N
    inv = jnp.concatenate([c * wgt, jnp.where(s == 0, alt_t / N, -sn * wgt)], axis=1)
    return fwd.astype(BF16), inv.astype(BF16)


def _matmul_kernel(a_ref, b_ref, o_ref):
    o_ref[...] = _dot(a_ref[...], b_ref[...])


def matmul(a, b, tm, tn):
    M, K = a.shape
    N = b.shape[1]
    return pl.pallas_call(
        _matmul_kernel,
        grid=(M // tm, N // tn),
        in_specs=[pl.BlockSpec((tm, K), lambda i, j: (i, 0)), pl.BlockSpec((K, tn), lambda i, j: (0, j))],
        out_specs=pl.BlockSpec((tm, tn), lambda i, j: (i, j)),
        out_shape=jax.ShapeDtypeStruct((M, N), F32),
        compiler_params=_cp("arbitrary", "arbitrary"),
        name="matmul",
    )(a, b)


def _long_conv_kernel(u_ref, g_ref, bias_ref, fr_ref, fi_ref, ic_ref, is_ref,
                      ar0_ref, ar1_ref, ai0_ref, ai1_ref, o_ref, ub_ref, acc_ref):
    f = pl.program_id(1)

    @pl.when(f == 0)
    def _():
        ub_ref[...] = u_ref[...].astype(BF16)
        acc_ref[...] = jnp.zeros_like(acc_ref)

    ub = ub_ref[...]
    ur = _dot(fr_ref[...], ub)
    ui = _dot(fi_ref[...], ub)
    kr = ar0_ref[...] + ar1_ref[...]
    nyq = (lax.broadcasted_iota(jnp.int32, (fr_ref.shape[0], 1), 0) == 0) & (f == 0)
    ki = jnp.where(nyq, ai0_ref[...] + ai1_ref[...], ai0_ref[...] - ai1_ref[...])
    pr = jnp.where(nyq, ur * kr, ur * kr - ui * ki)
    pi = jnp.where(nyq, ui * ki, ur * ki + ui * kr)
    acc_ref[...] += _dot(ic_ref[...], pr.astype(BF16)) + _dot(is_ref[...], pi.astype(BF16))

    @pl.when(f == pl.num_programs(1) - 1)
    def _():
        u = u_ref[...].astype(F32)
        o_ref[...] = (g_ref[...].astype(F32) * (acc_ref[...] + u * bias_ref[...])).astype(o_ref.dtype)


def long_conv(B, L, u, u_rb0, u_cb, gate, g_rb0, gate_cb, bias, fwd, inv, kspec, order, out_dtype):
    C = HY_WIDTH
    FB = min(FREQ_BLK, L)
    nfb = L // FB
    return pl.pallas_call(
        _long_conv_kernel,
        grid=(B, nfb),
        in_specs=[
            pl.BlockSpec((L, C), lambda b, f: (u_rb0 + b, u_cb)),
            pl.BlockSpec((L, C), lambda b, f: (g_rb0 + b, gate_cb)),
            pl.BlockSpec((1, C), lambda b, f: (0, 0)),
            pl.BlockSpec((FB, L), lambda b, f: (f, 0)),
            pl.BlockSpec((FB, L), lambda b, f: (nfb + f, 0)),
            pl.BlockSpec((L, FB), lambda b, f: (0, f)),
            pl.BlockSpec((L, FB), lambda b, f: (0, nfb + f)),
            pl.BlockSpec((FB, C), lambda b, f: (f, 2 * order)),
            pl.BlockSpec((FB, C), lambda b, f: (f, 2 * order + 1)),
            pl.BlockSpec((FB, C), lambda b, f: (nfb + f, 2 * order)),
            pl.BlockSpec((FB, C), lambda b, f: (nfb + f, 2 * order + 1)),
        ],
        out_specs=pl.BlockSpec((L, C), lambda b, f: (b, 0)),
        out_shape=jax.ShapeDtypeStruct((B * L, C), out_dtype),
        scratch_shapes=[pltpu.VMEM((L, C), BF16), pltpu.VMEM((L, C), F32)],
        compiler_params=_cp("arbitrary", "arbitrary"),
        name="long_conv",
    )(u, gate, bias.reshape(1, C), fwd, fwd, inv, inv, kspec, kspec, kspec, kspec)


def _scan_blocks(rw, rows):
    nbc, nbl, base = rw.Lc // rows, rw.Ll // rows, rw.NC // rows

    def make(d):
        def f(b, s):
            jc = s if d == 0 else nbc - 1 - s
            jl = (s - nbc) if d == 0 else nbl - 1 - (s - nbc)
            return jnp.where(s < nbc, b * nbc + jc, base + b * nbl + jl)
        return f

    return [make(0), make(1)], nbc + nbl


def _expand_lanes(x, base, n, width):
    rows = x.shape[0]
    per = 128 // width
    lane = lax.broadcasted_iota(jnp.int32, (rows, 128), 1)
    tiles = []
    for t in range(n // per):
        c0 = base + t * per
        tile = jnp.broadcast_to(x[:, c0:c0 + 1], (rows, 128))
        for i in range(1, per):
            tile = jnp.where(lane >= i * width, jnp.broadcast_to(x[:, c0 + i:c0 + i + 1], (rows, 128)), tile)
        tiles.append(tile)
    return jnp.concatenate(tiles, axis=1)


def _ssd_kernel(xf, bf, cf, smf, dtf, xb, bb, cb_, smb, dtb, alx_ref, alc_ref, of_ref, ob_ref, h_ref):
    Q = SSM_CHUNK
    GW = SSM_GW

    @pl.when(pl.program_id(1) == 0)
    def _():
        h_ref[...] = jnp.zeros_like(h_ref)

    row = lax.broadcasted_iota(jnp.int32, (Q, Q), 0)
    col = lax.broadcasted_iota(jnp.int32, (Q, Q), 1)
    lane_head = lax.broadcasted_iota(jnp.int32, (Q, GW), 1) // SSM_HEAD_DIM
    dirs = ((xf, bf, cf, smf, dtf, of_ref), (xb, bb, cb_, smb, dtb, ob_ref))
    jobs = []
    for d in range(2):
        x_ref, b_ref, c_ref, sm_ref, dt_ref, o_ref = dirs[d]
        keep = (col <= row) if d == 0 else (col >= row)
        tri = keep.astype(BF16)
        tri_t = ((row <= col) if d == 0 else (row >= col)).astype(BF16)
        sm = sm_ref[...]
        a_x = -jnp.exp(alx_ref[d])
        dtx = _expand_lanes(sm, 8 * d, SSM_HEADS, SSM_HEAD_DIM)
        cumx = _expand_lanes(_dot_01_lhs(tri, sm), 8 * d, SSM_HEADS, SSM_HEAD_DIM) * a_x
        cumr = _dot_01_rhs(dt_ref[0], tri_t) * (-jnp.exp(alc_ref[d]))
        last = Q - 1 if d == 0 else 0
        totx = cumx[last:last + 1, :]
        xd = x_ref[...].astype(F32) * dtx
        xdw = xd * jnp.exp(totx - cumx)
        ecum = jnp.exp(cumx)
        for g in range(SSM_GROUPS):
            gs = slice(g * GW, (g + 1) * GW)
            jobs.append(dict(d=d, g=g, gs=gs, keep=keep, cumx=cumx, cumr=cumr, o_ref=o_ref,
                             bg=b_ref[:, g * SSM_STATE:(g + 1) * SSM_STATE].astype(BF16),
                             cg=c_ref[:, g * SSM_STATE:(g + 1) * SSM_STATE].astype(BF16),
                             xdg=xd[:, gs], xdw=xdw[:, gs].astype(BF16), ecum=ecum[:, gs],
                             etot=jnp.exp(totx[:, gs])))
    for j in jobs:
        j["cb"] = _dot_nt(j["cg"], j["bg"])
        j["h"] = h_ref[j["d"], j["g"]]
    for j in jobs:
        ms, xs = [], []
        for e4 in range(SSM_HPG):
            e = j["g"] * SSM_HPG + e4
            diff = j["cumx"][:, e * SSM_HEAD_DIM:e * SSM_HEAD_DIM + 1] - j["cumr"][e:e + 1, :]
            ms.append((j["cb"] * jnp.where(j["keep"], jnp.exp(diff), 0.0)).astype(BF16))
            xs.append(jnp.where(lane_head == e4, j["xdg"], 0.0).astype(BF16))
        yd = _dot(jnp.concatenate(ms, axis=1), jnp.concatenate(xs, axis=0))
        y_off = _dot(j["cg"], j["h"].astype(BF16)) * j["ecum"]
        j["o_ref"][:, j["gs"]] = (yd + y_off).astype(BF16)
    for j in jobs:
        h_ref[j["d"], j["g"]] = j["h"] * j["etot"] + _dot_tn(j["bg"], j["xdw"])


def ssd_scan(rw, p, sm, dtT, alx, alc):
    Q = SSM_CHUNK
    blks, nsteps = _scan_blocks(rw, Q)
    R = p.shape[0]
    in_specs = []
    for d in range(2):
        f = blks[d]
        in_specs += [
            pl.BlockSpec((Q, 512), lambda b, s, f=f: (f(b, s), C_XBC // 512)),
            pl.BlockSpec((Q, 256), lambda b, s, f=f: (f(b, s), C_XBC // 256 + 2)),
            pl.BlockSpec((Q, 256), lambda b, s, f=f: (f(b, s), C_XBC // 256 + 3)),
            pl.BlockSpec((Q, 128), lambda b, s, f=f: (f(b, s), 0)),
            pl.BlockSpec((1, 8, Q), lambda b, s, f=f, d=d: (d, 0, f(b, s))),
        ]
    in_specs += [pl.BlockSpec((2, 1, 512), lambda b, s: (0, 0, 0)), pl.BlockSpec((2, 8, 1), lambda b, s: (0, 0, 0))]
    ops = (p, p, p, sm, dtT)
    return pl.pallas_call(
        _ssd_kernel,
        grid=(rw.B, nsteps),
        in_specs=in_specs,
        out_specs=[pl.BlockSpec((Q, 512), lambda b, s, f=blks[d]: (f(b, s), 0)) for d in range(2)],
        out_shape=[jax.ShapeDtypeStruct((R, 512), BF16)] * 2,
        scratch_shapes=[pltpu.VMEM((2, SSM_GROUPS, SSM_STATE, SSM_GW), F32)],
        compiler_params=_cp("arbitrary", "arbitrary"),
        name="ssd_scan",
    )(*ops, *ops, alx, alc)


def _split3(x):
    x1 = x.astype(BF16)
    r = x - x1.astype(F32)
    x2 = r.astype(BF16)
    x3 = (r - x2.astype(F32)).astype(BF16)
    return x1, x2, x3


def _dot_01_lhs(m01, x):
    x1, x2, x3 = _split3(x)
    return _dot(m01, x1) + _dot(m01, x2) + _dot(m01, x3)


def _dot_01_rhs(x, m01):
    x1, x2, x3 = _split3(x)
    return _dot(x1, m01) + _dot(x2, m01) + _dot(x3, m01)


GDN_ROWS = 256


def _gdn_prep_kernel(q_ref, k_ref, v_ref, sm_ref, gT_ref, u_ref, w_ref, qg_ref, kd_ref, qk_ref, egl_ref):
    C = GDN_CHUNK
    row = lax.broadcasted_iota(jnp.int32, (C, C), 0)
    col = lax.broadcasted_iota(jnp.int32, (C, C), 1)
    jobs = []
    levels = []
    for d in range(2):
        keep = (col <= row) if d == 0 else (col >= row)
        late, early = (row, col) if d == 0 else (col, row)
        levels.append([(((row ^ col) >> (j + 1)) == 0) & ((late & (1 << j)) != 0) & ((early & (1 << j)) == 0)
                       for j in range(6)])
        tri = keep.astype(BF16)
        tri_t = ((row <= col) if d == 0 else (row >= col)).astype(BF16)
        last = C - 1 if d == 0 else 0
        for c in range(GDN_ROWS // C):
            rows = slice(c * C, (c + 1) * C)
            smc = sm_ref[rows, :]
            cums = _dot_01_lhs(tri, smc)
            cumr = _dot_01_rhs(gT_ref[c, 8 * d:8 * d + 8, :], tri_t)
            tot = cums[last:last + 1, :]
            for h in range(GDN_HEADS):
                lg = 16 + 8 * d + h
                jobs.append(dict(d=d, c=c, h=h, rows=rows, hs=slice(h * 128, (h + 1) * 128), keep=keep,
                                 gc=cums[:, lg:lg + 1], beta=smc[:, lg + 4:lg + 5],
                                 gl=tot[:, lg:lg + 1], gr=cumr[h:h + 1, :]))
    for j in jobs:
        q = q_ref[j["rows"], j["hs"]].astype(F32)
        k = k_ref[j["rows"], j["hs"]].astype(F32)
        j["dec"] = jnp.where(j["keep"], jnp.exp(j["gc"] - j["gr"]), 0.0)
        kb = k * j["beta"]
        both = _dot_nt(jnp.concatenate([kb, q], axis=0).astype(BF16), k.astype(BF16))
        j["a"] = both[:C] * j["dec"]
        j["n"] = -jnp.where(levels[j["d"]][0], j["a"], 0.0)
        qk_ref[j["d"], j["c"], j["h"]] = (both[C:] * j["dec"]).astype(BF16)
    for lev in range(1, 6):
        for j in jobs:
            l = jnp.where(levels[j["d"]][lev], j["a"], 0.0)
            j["y"] = l + _dot(l.astype(BF16), j["n"].astype(BF16))
        for j in jobs:
            j["n"] = j["n"] - j["y"] - _dot(j["n"].astype(BF16), j["y"].astype(BF16))
    for j in jobs:
        d, rows, hs, gc, gl, beta = j["d"], j["rows"], j["hs"], j["gc"], j["gl"], j["beta"]
        q = q_ref[rows, hs].astype(F32)
        k = k_ref[rows, hs].astype(F32)
        eg = jnp.exp(gc)
        rhs = jnp.concatenate([v_ref[rows, hs].astype(F32) * beta, k * beta * eg], axis=1)
        sol = rhs + _dot(j["n"].astype(BF16), rhs.astype(BF16))
        u_ref[d, rows, hs] = sol[:, :GDN_DV].astype(BF16)
        w_ref[d, rows, hs] = sol[:, GDN_DV:].astype(BF16)
        qg_ref[d, rows, hs] = (q * eg).astype(BF16)
        kd_ref[d, rows, hs] = (k * jnp.exp(gl - gc)).astype(BF16)
        egl_ref[d, j["c"], :, hs] = jnp.broadcast_to(jnp.exp(gl), (8, 128))


def gdn_prep(p, sm, gT):
    R = p.shape[0]
    T, C = GDN_ROWS, GDN_CHUNK
    nc = T // C
    col = lambda k: pl.BlockSpec((T, 512), lambda i: (i, C_QKV // 512 + k))
    dirrow = pl.BlockSpec((2, T, 512), lambda i: (0, i, 0))
    return pl.pallas_call(
        _gdn_prep_kernel,
        grid=(R // T,),
        in_specs=[col(0), col(1), col(2),
                  pl.BlockSpec((T, 128), lambda i: (i, 0)),
                  pl.BlockSpec((nc, 16, C), lambda i: (i, 0, 0))],
        out_specs=[dirrow, dirrow, dirrow, dirrow,
                   pl.BlockSpec((2, nc, GDN_HEADS, C, C), lambda i: (0, i, 0, 0, 0)),
                   pl.BlockSpec((2, nc, 8, 512), lambda i: (0, i, 0, 0))],
        out_shape=[jax.ShapeDtypeStruct((2, R, 512), BF16),
                   jax.ShapeDtypeStruct((2, R, 512), BF16),
                   jax.ShapeDtypeStruct((2, R, 512), BF16),
                   jax.ShapeDtypeStruct((2, R, 512), BF16),
                   jax.ShapeDtypeStruct((2, R // C, GDN_HEADS, C, C), BF16),
                   jax.ShapeDtypeStruct((2, R // C, 8, 512), F32)],
        compiler_params=_cp("arbitrary"),
        name="gdn_prep",
    )(p, p, p, sm, gT)


def _gdn_scan_kernel(uf, wf, qgf, kdf, qkf, eglf, ub, wb, qgb, kdb, qkb, eglb, of_ref, ob_ref, s_ref):
    C = GDN_CHUNK
    nch = GDN_ROWS // C

    @pl.when(pl.program_id(1) == 0)
    def _():
        s_ref[...] = jnp.zeros_like(s_ref)

    dirs = ((uf, wf, qgf, kdf, qkf, eglf, of_ref), (ub, wb, qgb, kdb, qkb, eglb, ob_ref))
    chains = [(d, h) for d in range(2) for h in range(GDN_HEADS)]
    S = {ch: s_ref[ch[0], ch[1]] for ch in chains}
    for i in range(nch):
        Sb, vnb, rows_of, c_of = {}, {}, {}, {}
        for d, h in chains:
            c_of[d] = i if d == 0 else nch - 1 - i
            rows_of[d] = slice(c_of[d] * C, (c_of[d] + 1) * C)
        for d, h in chains:
            hs = slice(h * 128, (h + 1) * 128)
            Sb[d, h] = S[d, h].astype(BF16)
            v_new = dirs[d][0][0, rows_of[d], hs].astype(F32) - _dot(dirs[d][1][0, rows_of[d], hs], Sb[d, h])
            vnb[d, h] = v_new.astype(BF16)
        for d, h in chains:
            hs = slice(h * 128, (h + 1) * 128)
            u_ref, w_ref, qg_ref, kd_ref, qk_ref, egl_ref, o_ref = dirs[d]
            S[d, h] = S[d, h] * egl_ref[0, c_of[d], 0:1, hs] + _dot_tn(kd_ref[0, rows_of[d], hs], vnb[d, h])
        for d, h in chains:
            hs = slice(h * 128, (h + 1) * 128)
            u_ref, w_ref, qg_ref, kd_ref, qk_ref, egl_ref, o_ref = dirs[d]
            o_ref[rows_of[d], hs] = (_dot(qg_ref[0, rows_of[d], hs], Sb[d, h])
                                     + _dot(qk_ref[0, c_of[d], h], vnb[d, h])).astype(BF16)
    for ch in chains:
        s_ref[ch[0], ch[1]] = S[ch]


def gdn_scan(rw, u, w, qg, kd, qk, egl):
    T, C = GDN_ROWS, GDN_CHUNK
    nc = T // C
    R = u.shape[1]
    nbc, nbl, base = rw.Lc // T, rw.Ll // T, rw.NC // T

    def blk(d):
        def f(b, s):
            jc = s if d == 0 else nbc - 1 - s
            jl = (s - nbc) if d == 0 else nbl - 1 - (s - nbc)
            return jnp.where(s < nbc, b * nbc + jc, base + b * nbl + jl)
        return f

    in_specs = []
    for d in range(2):
        f = blk(d)
        rowspec = pl.BlockSpec((1, T, 512), lambda b, s, f=f, d=d: (d, f(b, s), 0))
        in_specs += [rowspec, rowspec, rowspec, rowspec,
                     pl.BlockSpec((1, nc, GDN_HEADS, C, C), lambda b, s, f=f, d=d: (d, f(b, s), 0, 0, 0)),
                     pl.BlockSpec((1, nc, 8, 512), lambda b, s, f=f, d=d: (d, f(b, s), 0, 0))]
    out_specs = [pl.BlockSpec((T, 512), lambda b, s, f=blk(d): (f(b, s), 0)) for d in range(2)]
    ops = (u, w, qg, kd, qk, egl)
    return pl.pallas_call(
        _gdn_scan_kernel,
        grid=(rw.B, nbc + nbl),
        in_specs=in_specs,
        out_specs=out_specs,
        out_shape=[jax.ShapeDtypeStruct((R, 512), BF16)] * 2,
        scratch_shapes=[pltpu.VMEM((2, GDN_HEADS, GDN_DK, GDN_DV), F32)],
        compiler_params=_cp("arbitrary", "arbitrary"),
        name="gdn_scan",
    )(*ops, *ops)


def _merge_kernel(yh_ref, sf_ref, sb_ref, sx_ref, sz_ref, dx_ref, snw_ref, gf_ref, gb_ref, gg_ref, gnw_ref,
                  g0_ref, g1_ref, g2_ref, w0_ref, w1_ref, w2_ref, wo_ref, x_ref, mod_ref, o_ref, ys_ref, yg_ref):
    tm = x_ref.shape[0]
    rp = 64
    for r in range(tm // rp):
        rs = slice(r * rp, (r + 1) * rp)
        y = (sf_ref[rs, :].astype(F32) + sb_ref[rs, :].astype(F32)
             + sx_ref[rs, :].astype(F32) * dx_ref[...])
        y = y * _silu(sz_ref[rs, :].astype(F32))
        parts = []
        for g in range(SSM_GROUPS):
            yg = y[:, g * SSM_GW:(g + 1) * SSM_GW]
            parts.append(yg * lax.rsqrt(jnp.mean(yg * yg, axis=-1, keepdims=True) + EPS))
        ys_ref[rs, :] = (jnp.concatenate(parts, axis=1) * snw_ref[...]).astype(BF16)
        o = gf_ref[rs, :].astype(F32) + gb_ref[rs, :].astype(F32)
        parts = []
        for h in range(GDN_HEADS):
            oh = o[:, h * 128:(h + 1) * 128]
            parts.append(oh * lax.rsqrt(jnp.mean(oh * oh, axis=-1, keepdims=True) + EPS))
        yg_ref[rs, :] = (jnp.concatenate(parts, axis=1) * gnw_ref[...]
                         * _silu(gg_ref[rs, :].astype(F32))).astype(BF16)
    m = (_sigmoid(g0_ref[...].astype(F32)) * _dot(yh_ref[...], w0_ref[...])
         + _sigmoid(g1_ref[...].astype(F32)) * _dot(ys_ref[...], w1_ref[...])
         + _sigmoid(g2_ref[...].astype(F32)) * _dot(yg_ref[...], w2_ref[...]))
    o_ref[...] = x_ref[...] + mod_ref[0, 2:3, :] * _dot(m.astype(BF16), wo_ref[...])


def merge(rw, yh, y_f, y_b, dx, ssm_nw, o_f, o_b, gdn_nw, p, w0, w1, w2, wo, x, mod):
    R, D = x.shape
    tm = min(rw.tm, 512)
    mi = rw.mod_index(tm)
    yspec = pl.BlockSpec((tm, 512), lambda i: (i, 0))
    pspec = lambda col: pl.BlockSpec((tm, 512), lambda i: (i, col // 512))
    vec = pl.BlockSpec((1, 512), lambda i: (0, 0))
    gspec = lambda k: pl.BlockSpec((tm, D), lambda i: (i, C_GATE // D + k))
    wspec = pl.BlockSpec((512, D), lambda i: (0, 0))
    return pl.pallas_call(
        _merge_kernel,
        grid=(R // tm,),
        in_specs=[yspec,
                  yspec, yspec, pspec(C_XBC), pspec(C_Z), vec, vec,
                  yspec, yspec, pspec(C_GG), vec,
                  gspec(0), gspec(1), gspec(2), wspec, wspec, wspec,
                  pl.BlockSpec((D, D), lambda i: (0, 0)),
                  pl.BlockSpec((tm, D), lambda i: (i, 0)),
                  pl.BlockSpec((1, 8, D), lambda i: (mi(i), 0, 0))],
        out_specs=pl.BlockSpec((tm, D), lambda i: (i, 0)),
        out_shape=jax.ShapeDtypeStruct((R, D), F32),
        scratch_shapes=[pltpu.VMEM((tm, 512), BF16), pltpu.VMEM((tm, 512), BF16)],
        compiler_params=_cp("arbitrary"),
        name="merge",
    )(yh, y_f, y_b, p, p, dx, ssm_nw, o_f, o_b, p, gdn_nw, p, p, p, w0, w1, w2, wo, x, mod)


def _swiglu_up_kernel(x_ref, nw_ref, mod_ref, wg_ref, wu_ref, o_ref, h_ref):
    @pl.when(pl.program_id(1) == 0)
    def _():
        h = _norm_mod(x_ref[...], nw_ref[...], mod_ref[0, 4:5, :], mod_ref[0, 3:4, :])
        h_ref[...] = h.astype(BF16)

    h = h_ref[...]
    g = _dot(h, wg_ref[...])
    u = _dot(h, wu_ref[...])
    o_ref[...] = (_silu(g) * u).astype(o_ref.dtype)


def swiglu_up(rw, x, nw, mod, wgu):
    R, D = x.shape
    tm = rw.tm
    tn = D_FF // 2
    nj = D_FF // tn
    mi = rw.mod_index(tm)
    return pl.pallas_call(
        _swiglu_up_kernel,
        grid=(R // tm, nj),
        in_specs=[
            pl.BlockSpec((tm, D), lambda i, j: (i, 0)),
            pl.BlockSpec((1, D), lambda i, j: (0, 0)),
            pl.BlockSpec((1, 8, D), lambda i, j: (mi(i), 0, 0)),
            pl.BlockSpec((D, tn), lambda i, j: (0, j)),
            pl.BlockSpec((D, tn), lambda i, j: (0, nj + j)),
        ],
        out_specs=pl.BlockSpec((tm, tn), lambda i, j: (i, j)),
        out_shape=jax.ShapeDtypeStruct((R, D_FF), BF16),
        scratch_shapes=[pltpu.VMEM((tm, D), BF16)],
        compiler_params=_cp("arbitrary", "arbitrary"),
        name="swiglu_up",
    )(x, nw.reshape(1, D), mod, wgu, wgu)


def _swiglu_down_kernel(a_ref, w_ref, x_ref, mod_ref, o_ref):
    o_ref[...] = x_ref[...] + mod_ref[0, 5:6, :] * _dot(a_ref[...], w_ref[...])


def swiglu_down(rw, a, w, x, mod):
    R, D = x.shape
    tm = min(rw.tm, 512)
    mi = rw.mod_index(tm)
    return pl.pallas_call(
        _swiglu_down_kernel,
        grid=(R // tm,),
        in_specs=[
            pl.BlockSpec((tm, D_FF), lambda i: (i, 0)),
            pl.BlockSpec((D_FF, D), lambda i: (0, 0)),
            pl.BlockSpec((tm, D), lambda i: (i, 0)),
            pl.BlockSpec((1, 8, D), lambda i: (mi(i), 0, 0)),
        ],
        out_specs=pl.BlockSpec((tm, D), lambda i: (i, 0)),
        out_shape=jax.ShapeDtypeStruct((R, D), F32),
        compiler_params=_cp("arbitrary"),
        name="swiglu_down",
    )(a, w, x, mod)


def _final_norm_kernel(x_ref, w_ref, o_ref):
    x = x_ref[...]
    ms = jnp.mean(x * x, axis=-1, keepdims=True)
    o_ref[...] = x * lax.rsqrt(ms + EPS) * w_ref[...]


def final_norm(rw, x, w):
    D = x.shape[1]
    tm = rw.tm
    n0 = rw.NC // tm
    nl = rw.B * rw.Ll
    return pl.pallas_call(
        _final_norm_kernel,
        grid=(nl // tm,),
        in_specs=[pl.BlockSpec((tm, D), lambda i: (n0 + i, 0)), pl.BlockSpec((1, D), lambda i: (0, 0))],
        out_specs=pl.BlockSpec((tm, D), lambda i: (i, 0)),
        out_shape=jax.ShapeDtypeStruct((nl, D), F32),
        compiler_params=_cp("arbitrary"),
        name="final_norm",
    )(x, w.reshape(1, D))


def _regroup_w_in(w_in):
    o_dt = 3072
    o_gdn = 3088
    o_a = o_gdn + 2048
    o_b = o_a + 8
    o_gate = o_gdn + 2064
    pieces = [
        w_in[..., 0:3072],
        w_in[..., o_gdn:o_gdn + 2048],
        w_in[..., o_gate:o_gate + 3072],
        w_in[..., o_dt:o_dt + 16],
        w_in[..., o_a:o_a + 4], w_in[..., o_b:o_b + 4],
        w_in[..., o_a + 4:o_a + 8], w_in[..., o_b + 4:o_b + 8],
        jnp.zeros(w_in.shape[:-1] + (N_IN_PAD - C_SM - 32,), w_in.dtype),
    ]
    return jnp.concatenate(pieces, axis=-1).astype(BF16)


def kernel(x, c, ctx, c_ctx, w_ada, b_ada, norm1_w, norm2_w, w_in, hy_conv_w, hy_conv_b, hy_w1, hy_b1, hy_w2, hy_b2, hy_w3, hy_freq, hy_bias, ssm_conv_w, ssm_conv_b, ssm_dt_bias, ssm_A_log, ssm_D, ssm_norm_w, gdn_conv_w, gdn_dt_bias, gdn_A_log, gdn_norm_w, w_hy_out, w_ssm_out, w_gdn_out, w_out, w_gate_up, w_down, final_norm_w):
    B, Ll, D = x.shape
    Lc = ctx.shape[1]
    depth = w_ada.shape[0]
    assert Lc == CONV_ROWS and D == D_MODEL and B <= 15
    rw = Rows(B, Lc, Ll)
    R, NC = rw.R, rw.NC

    xa = jnp.concatenate([ctx.reshape(B * Lc, D), x.reshape(B * Ll, D)], axis=0)

    svec = jnp.concatenate([c_ctx[None, :], c, jnp.zeros((15 - B, D), F32)], axis=0)
    mod_all = ada_modulation(svec, w_ada, b_ada)
    mod_all = jnp.pad(mod_all.reshape(depth, 16, 6, D), ((0, 0), (0, 0), (0, 2), (0, 0)))

    w_in_r = _regroup_w_in(w_in)
    fwd_l, inv_l = dft_tables(Ll)
    fwd_c, inv_c = dft_tables(Lc)
    feat_l, feat_c = hy_features(Ll), hy_features(Lc)

    for l in range(depth):
        mod = mod_all[l]
        par = _in_proj_params(hy_conv_w[l], hy_conv_b[l], ssm_conv_w[l], ssm_conv_b[l], gdn_conv_w[l],
                              ssm_dt_bias[l], gdn_dt_bias[l], gdn_A_log[l])
        p, sm = in_proj(rw, xa, norm1_w[l], mod, w_in_r[l], par)

        sm32_t = sm[:, :32].T
        dt_t = sm32_t[:16].reshape(2, 8, R)
        g_t = sm32_t[16:32].reshape(16, R // GDN_CHUNK, GDN_CHUNK).transpose(1, 0, 2)

        alx = jnp.repeat(ssm_A_log[l], SSM_HEAD_DIM, axis=-1).reshape(2, 1, 512)
        alc = ssm_A_log[l].reshape(2, 8, 1)
        y_f, y_b = ssd_scan(rw, p, sm, dt_t, alx, alc)
        dx = jnp.repeat(ssm_D[l], SSM_HEAD_DIM).reshape(1, 512)

        o_f, o_b = gdn_scan(rw, *gdn_prep(p, sm, g_t))

        hyu = p
        parts = []
        for (Bn, L, blk0, fwd, inv, feat) in ((B, Lc, 0, fwd_c, inv_c, feat_c),
                                              (B, Ll, NC // Ll, fwd_l, inv_l, feat_l)):
            if NC % L:
                raise ValueError("latent length must divide the context row count")
            filt = hy_filter(feat, hy_w1[l], hy_b1[l], hy_w2[l], hy_b2[l], hy_w3[l], hy_freq[l])
            kspec = matmul(fwd, filt, min(512, 2 * L), 512)
            z1 = long_conv(Bn, L, hyu, blk0, 0, hyu, blk0, 1, hy_bias[l, 0], fwd, inv, kspec, 0, F32)
            yy = long_conv(Bn, L, z1, 0, 0, hyu, blk0, 2, hy_bias[l, 1], fwd, inv, kspec, 1, BF16)
            parts.append(yy)
        y_hy = jnp.concatenate(parts, axis=0)

        xa = merge(rw, y_hy, y_f, y_b, dx, ssm_norm_w[l].reshape(1, 512),
                   o_f, o_b, jnp.tile(gdn_norm_w[l], GDN_HEADS).reshape(1, 512), p,
                   w_hy_out[l].astype(BF16), w_ssm_out[l].astype(BF16),
                   w_gdn_out[l].astype(BF16), w_out[l].astype(BF16), xa, mod)
        act = swiglu_up(rw, xa, norm2_w[l], mod, w_gate_up[l].astype(BF16))
        xa = swiglu_down(rw, act, w_down[l].astype(BF16), xa, mod)

    out = final_norm(rw, xa, final_norm_w)
    return out.reshape(B, Ll, D)
```

```python
import functools
import math

import jax
import jax.numpy as jnp
import numpy as np
from jax import lax
from jax.experimental import pallas as pl
from jax.experimental.pallas import tpu as pltpu

F32 = jnp.float32
BF16 = jnp.bfloat16
HI = lax.Precision.HIGHEST

EPS = 1e-6
D_MODEL = 1024
GRID_W = 64

HY_WIDTH = 512
HY_BANDS = 16
HY_EMB = 1 + 2 * HY_BANDS
HY_HIDDEN = 64
HY_SHORT_DECAY_PCT = 0.3
HY_LONG_DECAY_PCT = 1.5
HY_TARGET = 1e-2

SSM_HEADS = 8
SSM_HEAD_DIM = 64
SSM_WIDTH = 512
SSM_GROUPS = 2
SSM_HPG = 4
SSM_STATE = 128
SSM_CHUNK = 128
SSM_GW = SSM_HPG * SSM_HEAD_DIM

GDN_HEADS = 4
GDN_DK = 128
GDN_DV = 128
GDN_CHUNK = 64

D_FF = 2816

C_HY = 0
C_Z = 1536
C_XBC = 2048
C_QKV = 3072
C_GG = 4608
C_GATE = 5120
C_SM = 8192

CONV_ROWS = 256
FREQ_BLK = 512

VMEM_LIMIT = 56 * 1024 * 1024


def _cp(*sem, flags=None):
    return pltpu.CompilerParams(dimension_semantics=sem, vmem_limit_bytes=VMEM_LIMIT, flags=flags)


def _sigmoid(x):
    return 1.0 / (1.0 + jnp.exp(-x))


def _silu(x):
    return x * _sigmoid(x)


def _softplus(x):
    return jnp.maximum(x, 0.0) + jnp.log1p(jnp.exp(-jnp.abs(x)))


def _dot(a, b, precision=None):
    return jnp.dot(a, b, precision=precision, preferred_element_type=F32)


def _dot_nt(a, b):
    return lax.dot_general(a, b, (((1,), (1,)), ((), ())), preferred_element_type=F32)


def _dot_tn(a, b):
    return lax.dot_general(a, b, (((0,), (0,)), ((), ())), preferred_element_type=F32)


def _ada_kernel(s_ref, w_ref, b_ref, o_ref):
    s = _silu(s_ref[...])
    o_ref[0] = _dot(s, w_ref[0], HI) + b_ref[0]


def ada_modulation(svec, w_ada, b_ada):
    depth = w_ada.shape[0]
    D = D_MODEL
    return pl.pallas_call(
        _ada_kernel,
        grid=(depth, 6),
        in_specs=[
            pl.BlockSpec((16, D), lambda l, j: (0, 0)),
            pl.BlockSpec((1, D, D), lambda l, j: (l, 0, j)),
            pl.BlockSpec((1, 1, D), lambda l, j: (l, 0, j)),
        ],
        out_specs=pl.BlockSpec((1, 16, D), lambda l, j: (l, 0, j)),
        out_shape=jax.ShapeDtypeStruct((depth, 16, 6 * D), F32),
        compiler_params=_cp("arbitrary", "arbitrary"),
        name="ada",
    )(svec, w_ada, b_ada.reshape(depth, 1, 6 * D))


def _norm_mod(x, nw, scale, shift):
    ms = jnp.mean(x * x, axis=-1, keepdims=True)
    return (x * lax.rsqrt(ms + EPS) * nw) * (1.0 + scale) + shift


N_IN_PAD = C_SM + 128
IN_TN = N_IN_PAD // 5
MODE_RAW, MODE_CONV, MODE_CONV_SILU, MODE_CONV_SILU_L2, MODE_SMALL = range(5)


def _tile_mode(tile):
    col = tile * 128
    if col < C_Z:
        return MODE_CONV
    if col < C_XBC:
        return MODE_RAW
    if col < C_QKV:
        return MODE_CONV_SILU
    if col < C_QKV + 1024:
        return MODE_CONV_SILU_L2
    if col < C_GG:
        return MODE_CONV_SILU
    if col < C_SM:
        return MODE_RAW
    return MODE_SMALL
PAR_W0, PAR_W1, PAR_W2, PAR_BIAS, PAR_L2SCALE, PAR_SBIAS, PAR_SALOG, PAR_SKIND = range(8)


def _in_proj_kernel(x_ref, nw_ref, mod_ref, w_ref, par_ref, o_ref, sm_ref, h_ref, raw0_ref, raw1_ref, *, nctx_blk):
    j = pl.program_id(1)
    nj = N_IN_PAD // IN_TN
    raws = (raw0_ref, raw1_ref)

    @pl.when(j == 0)
    def _():
        h = _norm_mod(x_ref[...], nw_ref[...], mod_ref[0, 1:2, :], mod_ref[0, 0:1, :])
        h_ref[...] = h.astype(BF16)

    T = h_ref.shape[0]
    G = GRID_W
    per_ctx = CONV_ROWS // G
    is_latent = pl.program_id(0) >= nctx_blk
    sub = lax.broadcasted_iota(jnp.int32, (8, 128), 0)

    def raw_piece(src, g, c):
        return src[g * G:(g + 1) * G, c * 128:(c + 1) * 128]

    def conv(src, g, c):
        cs = slice(c * 128, (c + 1) * 128)
        x = raw_piece(src, g, c)
        zero = jnp.zeros((1, 128), F32)
        before = zero if g % per_ctx == 0 else jnp.where(is_latent, 0.0, src[g * G - 1:g * G, cs])
        after = zero if g % per_ctx == per_ctx - 1 else jnp.where(is_latent, 0.0, src[(g + 1) * G:(g + 1) * G + 1, cs])
        rp = pltpu.roll(x, 1, 0)
        rn = pltpu.roll(x, G - 1, 0)
        prev = jnp.concatenate([jnp.where(sub == 0, before, rp[0:8]), rp[8:]], axis=0)
        nxt = jnp.concatenate([rn[:G - 8], jnp.where(sub == 7, after, rn[G - 8:])], axis=0)
        return (prev * par_ref[PAR_W0:PAR_W0 + 1, cs] + x * par_ref[PAR_W1:PAR_W1 + 1, cs]
                + nxt * par_ref[PAR_W2:PAR_W2 + 1, cs] + par_ref[PAR_BIAS:PAR_BIAS + 1, cs])

    def conv_silu(src, g, c):
        return _silu(conv(src, g, c))

    def conv_silu_l2(src, g, c):
        y = _silu(conv(src, g, c))
        y = y * lax.rsqrt(jnp.sum(y * y, axis=-1, keepdims=True) + EPS)
        return y * par_ref[PAR_L2SCALE:PAR_L2SCALE + 1, c * 128:(c + 1) * 128]

    def small(src, g, c):
        cs = slice(c * 128, (c + 1) * 128)
        acc = raw_piece(src, g, c)
        kind = par_ref[PAR_SKIND:PAR_SKIND + 1, cs]
        sp = _softplus(acc + par_ref[PAR_SBIAS:PAR_SBIAS + 1, cs])
        dec = -jnp.exp(par_ref[PAR_SALOG:PAR_SALOG + 1, cs]) * sp
        return jnp.where(kind == 0.0, sp, jnp.where(kind == 1.0, dec, jnp.where(kind == 2.0, _sigmoid(acc), 0.0)))

    rows_mm = 256
    tiles = IN_TN // 128
    piece_fn = {MODE_RAW: raw_piece, MODE_CONV: conv, MODE_CONV_SILU: conv_silu,
                MODE_CONV_SILU_L2: conv_silu_l2, MODE_SMALL: small}

    def project(dst, r):
        rs = slice(r * rows_mm, (r + 1) * rows_mm)
        dst[rs, :] = _dot(h_ref[rs, :], w_ref[...])

    def finish(src, blk, g, c):
        mode = _tile_mode(blk * tiles + c)
        y = piece_fn[mode](src, g, c)
        if mode == MODE_SMALL:
            sm_ref[g * G:(g + 1) * G, :] = y
            y = jnp.zeros_like(y)
        o_ref[g * G:(g + 1) * G, c * 128:(c + 1) * 128] = y.astype(o_ref.dtype)

    for step in range(nj + 1):
        @pl.when(j == step)
        def _(step=step):
            blk = step - 1
            src, dst = raws[blk % 2], raws[step % 2]
            for r in range(T // rows_mm):
                if step < nj:
                    project(dst, r)
                if blk < 0:
                    continue
                for g in range(r * rows_mm // G, (r + 1) * rows_mm // G):
                    for c in range(tiles):
                        finish(src, blk, g, c)


class Rows:
    def __init__(self, B, Lc, Ll):
        self.B, self.Lc, self.Ll = B, Lc, Ll
        self.NC = B * Lc
        self.R = B * Lc + B * Ll
        assert self.NC % Ll == 0 or Ll % self.NC == 0
        tm = 1024
        while self.NC % tm or Ll % tm:
            tm //= 2
        self.tm = tm

    def mod_index(self, tm):
        nctx = self.NC // tm
        per = self.Ll // tm
        return lambda i: jnp.where(i < nctx, 0, 1 + (i - nctx) // per)


def in_proj(rw, x, nw, mod, w, par):
    R, D = x.shape
    N = w.shape[1]
    tm, tn = rw.tm, IN_TN
    nj = N // tn
    assert N == N_IN_PAD
    mi = rw.mod_index(tm)
    done = lambda j: jnp.maximum(j - 1, 0)
    return pl.pallas_call(
        functools.partial(_in_proj_kernel, nctx_blk=rw.NC // tm),
        grid=(R // tm, nj + 1),
        in_specs=[
            pl.BlockSpec((tm, D), lambda i, j: (i, 0)),
            pl.BlockSpec((1, D), lambda i, j: (0, 0)),
            pl.BlockSpec((1, 8, D), lambda i, j: (mi(i), 0, 0)),
            pl.BlockSpec((D, tn), lambda i, j: (0, jnp.minimum(j, nj - 1))),
            pl.BlockSpec((8, tn), lambda i, j: (0, done(j))),
        ],
        out_specs=[pl.BlockSpec((tm, tn), lambda i, j: (i, done(j))),
                   pl.BlockSpec((tm, 128), lambda i, j: (i, 0))],
        out_shape=[jax.ShapeDtypeStruct((R, N), BF16), jax.ShapeDtypeStruct((R, 128), F32)],
        scratch_shapes=[pltpu.VMEM((tm, D), BF16), pltpu.VMEM((tm, tn), F32), pltpu.VMEM((tm, tn), F32)],
        compiler_params=_cp("arbitrary", "arbitrary"),
        name="in_proj",
    )(x, nw.reshape(1, D), mod, w, par)


def _in_proj_params(hy_conv_w, hy_conv_b, ssm_conv_w, ssm_conv_b, gdn_conv_w, ssm_dt_bias, gdn_dt_bias, gdn_A_log):
    def row(pieces):
        out, pos = [], 0
        for off, a in pieces:
            out += [jnp.zeros((off - pos,), F32), a.astype(F32)]
            pos = off + a.shape[0]
        return jnp.concatenate(out + [jnp.zeros((N_IN_PAD - pos,), F32)])
    z4 = jnp.zeros((4,), F32)
    conv = [row([(C_HY, hy_conv_w[t]), (C_XBC, ssm_conv_w[t]), (C_QKV, gdn_conv_w[t])]) for t in range(3)]
    bias = row([(C_HY, hy_conv_b), (C_XBC, ssm_conv_b)])
    l2s = row([(C_QKV, jnp.full((512,), GDN_DK ** -0.5, F32)), (C_QKV + 512, jnp.ones((512,), F32))])
    sbias = row([(C_SM, jnp.concatenate([ssm_dt_bias.reshape(16), gdn_dt_bias[0], z4, gdn_dt_bias[1], z4]))])
    salog = row([(C_SM + 16, jnp.concatenate([gdn_A_log[0], z4, gdn_A_log[1], z4]))])
    kind = np.full((N_IN_PAD,), 3.0, np.float32)
    kind[C_SM:C_SM + 16] = 0.0
    kind[C_SM + 16:C_SM + 20] = 1.0
    kind[C_SM + 24:C_SM + 28] = 1.0
    kind[C_SM + 20:C_SM + 24] = 2.0
    kind[C_SM + 28:C_SM + 32] = 2.0
    return jnp.stack(conv + [bias, l2s, sbias, salog, jnp.asarray(kind)], axis=0)


def _hy_filter_kernel(z_ref, w1_ref, b1_ref, w2_ref, b2_ref, w3_ref, f0_ref, f1_ref, win_ref, o_ref, h_ref):
    @pl.when(pl.program_id(1) == 0)
    def _():
        h1 = jnp.sin(f0_ref[...] * (_dot(z_ref[...], w1_ref[...], HI) + b1_ref[...]))
        h_ref[...] = jnp.sin(f1_ref[...] * (_dot(h1, w2_ref[...], HI) + b2_ref[...]))

    h = _dot(h_ref[...], w3_ref[...], HI) * win_ref[...]
    tl = h.shape[0]
    row = lax.broadcasted_iota(jnp.int32, (tl, 1), 0) + pl.program_id(0) * tl
    drop = (row == 0) & (pl.program_id(1) % 2 == 1)
    o_ref[...] = jnp.where(drop, 0.0, h).astype(o_ref.dtype)


def hy_features(L):
    t = jnp.linspace(0.0, 1.0, L, dtype=F32)[:, None]
    w = 2.0 * math.pi * jnp.arange(L, dtype=F32)[:, None] / L
    f = jnp.linspace(1e-4, HY_BANDS - 1, HY_BANDS, dtype=F32)[None, :]
    z = jnp.concatenate([t, jnp.cos(f * w), -jnp.sin(f * w)], axis=-1)
    z = jnp.pad(z, ((0, 0), (0, 128 - HY_EMB)))
    min_decay = math.log(HY_TARGET) / HY_LONG_DECAY_PCT
    max_decay = math.log(HY_TARGET) / HY_SHORT_DECAY_PCT
    deltas = jnp.linspace(min_decay, max_decay, HY_WIDTH, dtype=F32)
    window = jnp.exp(-t * jnp.abs(deltas))
    return z, window


def hy_filter(feat, w1, b1, w2, b2, w3, freq):
    z, window = feat
    L = z.shape[0]
    H = HY_HIDDEN
    w1p = jnp.pad(w1, ((0, 128 - HY_EMB), (0, 128 - H)))
    w2p = jnp.pad(w2, ((0, 128 - H), (0, 128 - H)))
    w3p = jnp.pad(w3, ((0, 128 - H), (0, 0)))
    pad1 = lambda v: jnp.pad(v, (0, 128 - H)).reshape(1, 128)
    tl = 256
    full = lambda shape: pl.BlockSpec(shape, lambda i, j: (0, 0))
    return pl.pallas_call(
        _hy_filter_kernel,
        grid=(L // tl, 4),
        in_specs=[
            pl.BlockSpec((tl, 128), lambda i, j: (i, 0)),
            full((128, 128)), full((1, 128)), full((128, 128)), full((1, 128)),
            pl.BlockSpec((128, HY_WIDTH), lambda i, j: (0, j)),
            full((1, 128)), full((1, 128)),
            pl.BlockSpec((tl, HY_WIDTH), lambda i, j: (i, 0)),
        ],
        out_specs=pl.BlockSpec((tl, HY_WIDTH), lambda i, j: (i, j)),
        out_shape=jax.ShapeDtypeStruct((L, 4 * HY_WIDTH), BF16),
        scratch_shapes=[pltpu.VMEM((tl, 128), F32)],
        compiler_params=_cp("arbitrary", "arbitrary"),
        name="hy_filter",
    )(z, w1p, pad1(b1), w2p, pad1(b2), w3p, pad1(freq[0]), pad1(freq[1]), window)


def dft_tables(L):
    N = 2 * L
    f = np.arange(L, dtype=np.int64)[:, None]
    s = np.arange(L, dtype=np.int64)[None, :]
    ang = ((f * s) % N).astype(np.float64) * (2.0 * math.pi / N)
    c, sn = np.cos(ang), np.sin(ang)
    alt_s = (1 - 2 * (s % 2)).astype(np.float64)
    alt_t = (1 - 2 * (f % 2)).astype(np.float64)
    fwd = np.concatenate([c, np.where(f == 0, alt_s, -sn)], axis=0)
    wgt = np.where(s == 0, 1.0, 2.0) / N
    inv = np.concatenate([c * wgt, np.where(s == 0, alt_t / N, -sn * wgt)], axis=1)
    return jnp.asarray(fwd, dtype=BF16), jnp.asarray(inv, dtype=BF16)


def _matmul_kernel(a_ref, b_ref, o_ref):
    o_ref[...] = _dot(a_ref[...], b_ref[...])


def matmul(a, b, tm, tn):
    M, K = a.shape
    N = b.shape[1]
    return pl.pallas_call(
        _matmul_kernel,
        grid=(M // tm, N // tn),
        in_specs=[pl.BlockSpec((tm, K), lambda i, j: (i, 0)), pl.BlockSpec((K, tn), lambda i, j: (0, j))],
        out_specs=pl.BlockSpec((tm, tn), lambda i, j: (i, j)),
        out_shape=jax.ShapeDtypeStruct((M, N), F32),
        compiler_params=_cp("arbitrary", "arbitrary"),
        name="matmul",
    )(a, b)


def _long_conv_kernel(u_ref, g_ref, bias_ref, fr_ref, fi_ref, ic_ref, is_ref,
                      ar0_ref, ar1_ref, ai0_ref, ai1_ref, o_ref, ub_ref, acc_ref):
    f = pl.program_id(1)

    @pl.when(f == 0)
    def _():
        ub_ref[...] = u_ref[...].astype(BF16)
        acc_ref[...] = jnp.zeros_like(acc_ref)

    ub = ub_ref[...]
    ur = _dot(fr_ref[...], ub)
    ui = _dot(fi_ref[...], ub)
    kr = ar0_ref[...] + ar1_ref[...]
    nyq = (lax.broadcasted_iota(jnp.int32, (fr_ref.shape[0], 1), 0) == 0) & (f == 0)
    ki = jnp.where(nyq, ai0_ref[...] + ai1_ref[...], ai0_ref[...] - ai1_ref[...])
    pr = jnp.where(nyq, ur * kr, ur * kr - ui * ki)
    pi = jnp.where(nyq, ui * ki, ur * ki + ui * kr)
    acc_ref[...] += _dot(ic_ref[...], pr.astype(BF16)) + _dot(is_ref[...], pi.astype(BF16))

    @pl.when(f == pl.num_programs(1) - 1)
    def _():
        u = u_ref[...].astype(F32)
        o_ref[...] = (g_ref[...].astype(F32) * (acc_ref[...] + u * bias_ref[...])).astype(o_ref.dtype)


def long_conv(B, L, u, u_rb0, u_cb, gate, g_rb0, gate_cb, bias, fwd, inv, kspec, order, out_dtype):
    C = HY_WIDTH
    FB = min(FREQ_BLK, L)
    nfb = L // FB
    return pl.pallas_call(
        _long_conv_kernel,
        grid=(B, nfb),
        in_specs=[
            pl.BlockSpec((L, C), lambda b, f: (u_rb0 + b, u_cb)),
            pl.BlockSpec((L, C), lambda b, f: (g_rb0 + b, gate_cb)),
            pl.BlockSpec((1, C), lambda b, f: (0, 0)),
            pl.BlockSpec((FB, L), lambda b, f: (f, 0)),
            pl.BlockSpec((FB, L), lambda b, f: (nfb + f, 0)),
            pl.BlockSpec((L, FB), lambda b, f: (0, f)),
            pl.BlockSpec((L, FB), lambda b, f: (0, nfb + f)),
            pl.BlockSpec((FB, C), lambda b, f: (f, 2 * order)),
            pl.BlockSpec((FB, C), lambda b, f: (f, 2 * order + 1)),
            pl.BlockSpec((FB, C), lambda b, f: (nfb + f, 2 * order)),
            pl.BlockSpec((FB, C), lambda b, f: (nfb + f, 2 * order + 1)),
        ],
        out_specs=pl.BlockSpec((L, C), lambda b, f: (b, 0)),
        out_shape=jax.ShapeDtypeStruct((B * L, C), out_dtype),
        scratch_shapes=[pltpu.VMEM((L, C), BF16), pltpu.VMEM((L, C), F32)],
        compiler_params=_cp("arbitrary", "arbitrary"),
        name="long_conv",
    )(u, gate, bias.reshape(1, C), fwd, fwd, inv, inv, kspec, kspec, kspec, kspec)


def _scan_blocks(rw, rows):
    nbc, nbl, base = rw.Lc // rows, rw.Ll // rows, rw.NC // rows

    def make(d):
        def f(b, s):
            jc = s if d == 0 else nbc - 1 - s
            jl = (s - nbc) if d == 0 else nbl - 1 - (s - nbc)
            return jnp.where(s < nbc, b * nbc + jc, base + b * nbl + jl)
        return f

    return [make(0), make(1)], nbc + nbl


def _expand_lanes(x, base, n, width):
    rows = x.shape[0]
    per = 128 // width
    lane = lax.broadcasted_iota(jnp.int32, (rows, 128), 1)
    tiles = []
    for t in range(n // per):
        c0 = base + t * per
        tile = jnp.broadcast_to(x[:, c0:c0 + 1], (rows, 128))
        for i in range(1, per):
            tile = jnp.where(lane >= i * width, jnp.broadcast_to(x[:, c0 + i:c0 + i + 1], (rows, 128)), tile)
        tiles.append(tile)
    return jnp.concatenate(tiles, axis=1)


def _ssd_kernel(xf, bf, cf, smf, dtf, xb, bb, cb_, smb, dtb, alx_ref, alc_ref, of_ref, ob_ref, h_ref):
    Q = SSM_CHUNK
    GW = SSM_GW

    @pl.when(pl.program_id(1) == 0)
    def _():
        h_ref[...] = jnp.zeros_like(h_ref)

    row = lax.broadcasted_iota(jnp.int32, (Q, Q), 0)
    col = lax.broadcasted_iota(jnp.int32, (Q, Q), 1)
    lane_head = lax.broadcasted_iota(jnp.int32, (Q, GW), 1) // SSM_HEAD_DIM
    dirs = ((xf, bf, cf, smf, dtf, of_ref), (xb, bb, cb_, smb, dtb, ob_ref))
    jobs = []
    for d in range(2):
        x_ref, b_ref, c_ref, sm_ref, dt_ref, o_ref = dirs[d]
        keep = (col <= row) if d == 0 else (col >= row)
        tri = keep.astype(BF16)
        tri_t = ((row <= col) if d == 0 else (row >= col)).astype(BF16)
        sm = sm_ref[...]
        a_x = -jnp.exp(alx_ref[d])
        dtx = _expand_lanes(sm, 8 * d, SSM_HEADS, SSM_HEAD_DIM)
        cumx = _expand_lanes(_dot_01_lhs(tri, sm), 8 * d, SSM_HEADS, SSM_HEAD_DIM) * a_x
        cumr = _dot_01_rhs(dt_ref[0], tri_t) * (-jnp.exp(alc_ref[d]))
        last = Q - 1 if d == 0 else 0
        totx = cumx[last:last + 1, :]
        xd = x_ref[...].astype(F32) * dtx
        xdw = xd * jnp.exp(totx - cumx)
        ecum = jnp.exp(cumx)
        for g in range(SSM_GROUPS):
            gs = slice(g * GW, (g + 1) * GW)
            jobs.append(dict(d=d, g=g, gs=gs, keep=keep, cumx=cumx, cumr=cumr, o_ref=o_ref,
                             bg=b_ref[:, g * SSM_STATE:(g + 1) * SSM_STATE].astype(BF16),
                             cg=c_ref[:, g * SSM_STATE:(g + 1) * SSM_STATE].astype(BF16),
                             xdg=xd[:, gs], xdw=xdw[:, gs].astype(BF16), ecum=ecum[:, gs],
                             etot=jnp.exp(totx[:, gs])))
    for j in jobs:
        j["cb"] = _dot_nt(j["cg"], j["bg"])
        j["h"] = h_ref[j["d"], j["g"]]
    for j in jobs:
        ms, xs = [], []
        for e4 in range(SSM_HPG):
            e = j["g"] * SSM_HPG + e4
            diff = j["cumx"][:, e * SSM_HEAD_DIM:e * SSM_HEAD_DIM + 1] - j["cumr"][e:e + 1, :]
            ms.append((j["cb"] * jnp.where(j["keep"], jnp.exp(diff), 0.0)).astype(BF16))
            xs.append(jnp.where(lane_head == e4, j["xdg"], 0.0).astype(BF16))
        yd = _dot(jnp.concatenate(ms, axis=1), jnp.concatenate(xs, axis=0))
        y_off = _dot(j["cg"], j["h"].astype(BF16)) * j["ecum"]
        j["o_ref"][:, j["gs"]] = (yd + y_off).astype(BF16)
    for j in jobs:
        h_ref[j["d"], j["g"]] = j["h"] * j["etot"] + _dot_tn(j["bg"], j["xdw"])


def ssd_scan(rw, p, sm, dtT, alx, alc):
    Q = SSM_CHUNK
    blks, nsteps = _scan_blocks(rw, Q)
    R = p.shape[0]
    in_specs = []
    for d in range(2):
        f = blks[d]
        in_specs += [
            pl.BlockSpec((Q, 512), lambda b, s, f=f: (f(b, s), C_XBC // 512)),
            pl.BlockSpec((Q, 256), lambda b, s, f=f: (f(b, s), C_XBC // 256 + 2)),
            pl.BlockSpec((Q, 256), lambda b, s, f=f: (f(b, s), C_XBC // 256 + 3)),
            pl.BlockSpec((Q, 128), lambda b, s, f=f: (f(b, s), 0)),
            pl.BlockSpec((1, 8, Q), lambda b, s, f=f, d=d: (d, 0, f(b, s))),
        ]
    in_specs += [pl.BlockSpec((2, 1, 512), lambda b, s: (0, 0, 0)), pl.BlockSpec((2, 8, 1), lambda b, s: (0, 0, 0))]
    ops = (p, p, p, sm, dtT)
    return pl.pallas_call(
        _ssd_kernel,
        grid=(rw.B, nsteps),
        in_specs=in_specs,
        out_specs=[pl.BlockSpec((Q, 512), lambda b, s, f=blks[d]: (f(b, s), 0)) for d in range(2)],
        out_shape=[jax.ShapeDtypeStruct((R, 512), BF16)] * 2,
        scratch_shapes=[pltpu.VMEM((2, SSM_GROUPS, SSM_STATE, SSM_GW), F32)],
        compiler_params=_cp("arbitrary", "arbitrary"),
        name="ssd_scan",
    )(*ops, *ops, alx, alc)


def _split3(x):
    x1 = x.astype(BF16)
    r = x - x1.astype(F32)
    x2 = r.astype(BF16)
    x3 = (r - x2.astype(F32)).astype(BF16)
    return x1, x2, x3


def _dot_01_lhs(m01, x):
    x1, x2, x3 = _split3(x)
    return _dot(m01, x1) + _dot(m01, x2) + _dot(m01, x3)


def _dot_01_rhs(x, m01):
    x1, x2, x3 = _split3(x)
    return _dot(x1, m01) + _dot(x2, m01) + _dot(x3, m01)


GDN_ROWS = 256


def _gdn_prep_kernel(q_ref, k_ref, v_ref, sm_ref, gT_ref, u_ref, w_ref, qg_ref, kd_ref, qk_ref, egl_ref):
    C = GDN_CHUNK
    row = lax.broadcasted_iota(jnp.int32, (C, C), 0)
    col = lax.broadcasted_iota(jnp.int32, (C, C), 1)
    jobs = []
    levels = []
    for d in range(2):
        keep = (col <= row) if d == 0 else (col >= row)
        late, early = (row, col) if d == 0 else (col, row)
        levels.append([(((row ^ col) >> (j + 1)) == 0) & ((late & (1 << j)) != 0) & ((early & (1 << j)) == 0)
                       for j in range(6)])
        tri = keep.astype(BF16)
        tri_t = ((row <= col) if d == 0 else (row >= col)).astype(BF16)
        last = C - 1 if d == 0 else 0
        for c in range(GDN_ROWS // C):
            rows = slice(c * C, (c + 1) * C)
            smc = sm_ref[rows, :]
            cums = _dot_01_lhs(tri, smc)
            cumr = _dot_01_rhs(gT_ref[c, 8 * d:8 * d + 8, :], tri_t)
            tot = cums[last:last + 1, :]
            for h in range(GDN_HEADS):
                lg = 16 + 8 * d + h
                jobs.append(dict(d=d, c=c, h=h, rows=rows, hs=slice(h * 128, (h + 1) * 128), keep=keep,
                                 gc=cums[:, lg:lg + 1], beta=smc[:, lg + 4:lg + 5],
                                 gl=tot[:, lg:lg + 1], gr=cumr[h:h + 1, :]))
    for j in jobs:
        q = q_ref[j["rows"], j["hs"]].astype(F32)
        k = k_ref[j["rows"], j["hs"]].astype(F32)
        j["dec"] = jnp.where(j["keep"], jnp.exp(j["gc"] - j["gr"]), 0.0)
        kb = k * j["beta"]
        both = _dot_nt(jnp.concatenate([kb, q], axis=0).astype(BF16), k.astype(BF16))
        j["a"] = both[:C] * j["dec"]
        j["n"] = -jnp.where(levels[j["d"]][0], j["a"], 0.0)
        qk_ref[j["d"], j["c"], j["h"]] = (both[C:] * j["dec"]).astype(BF16)
    for lev in range(1, 6):
        for j in jobs:
            l = jnp.where(levels[j["d"]][lev], j["a"], 0.0)
            j["y"] = l + _dot(l.astype(BF16), j["n"].astype(BF16))
        for j in jobs:
            j["n"] = j["n"] - j["y"] - _dot(j["n"].astype(BF16), j["y"].astype(BF16))
    for j in jobs:
        d, rows, hs, gc, gl, beta = j["d"], j["rows"], j["hs"], j["gc"], j["gl"], j["beta"]
        q = q_ref[rows, hs].astype(F32)
        k = k_ref[rows, hs].astype(F32)
        eg = jnp.exp(gc)
        rhs = jnp.concatenate([v_ref[rows, hs].astype(F32) * beta, k * beta * eg], axis=1)
        sol = rhs + _dot(j["n"].astype(BF16), rhs.astype(BF16))
        u_ref[d, rows, hs] = sol[:, :GDN_DV].astype(BF16)
        w_ref[d, rows, hs] = sol[:, GDN_DV:].astype(BF16)
        qg_ref[d, rows, hs] = (q * eg).astype(BF16)
        kd_ref[d, rows, hs] = (k * jnp.exp(gl - gc)).astype(BF16)
        egl_ref[d, j["c"], :, hs] = jnp.broadcast_to(jnp.exp(gl), (8, 128))


def gdn_prep(p, sm, gT):
    R = p.shape[0]
    T, C = GDN_ROWS, GDN_CHUNK
    nc = T // C
    col = lambda k: pl.BlockSpec((T, 512), lambda i: (i, C_QKV // 512 + k))
    dirrow = pl.BlockSpec((2, T, 512), lambda i: (0, i, 0))
    return pl.pallas_call(
        _gdn_prep_kernel,
        grid=(R // T,),
        in_specs=[col(0), col(1), col(2),
                  pl.BlockSpec((T, 128), lambda i: (i, 0)),
                  pl.BlockSpec((nc, 16, C), lambda i: (i, 0, 0))],
        out_specs=[dirrow, dirrow, dirrow, dirrow,
                   pl.BlockSpec((2, nc, GDN_HEADS, C, C), lambda i: (0, i, 0, 0, 0)),
                   pl.BlockSpec((2, nc, 8, 512), lambda i: (0, i, 0, 0))],
        out_shape=[jax.ShapeDtypeStruct((2, R, 512), BF16),
                   jax.ShapeDtypeStruct((2, R, 512), BF16),
                   jax.ShapeDtypeStruct((2, R, 512), BF16),
                   jax.ShapeDtypeStruct((2, R, 512), BF16),
                   jax.ShapeDtypeStruct((2, R // C, GDN_HEADS, C, C), BF16),
                   jax.ShapeDtypeStruct((2, R // C, 8, 512), F32)],
        compiler_params=_cp("arbitrary"),
        name="gdn_prep",
    )(p, p, p, sm, gT)


def _gdn_scan_kernel(uf, wf, qgf, kdf, qkf, eglf, ub, wb, qgb, kdb, qkb, eglb, of_ref, ob_ref, s_ref):
    C = GDN_CHUNK
    nch = GDN_ROWS // C

    @pl.when(pl.program_id(1) == 0)
    def _():
        s_ref[...] = jnp.zeros_like(s_ref)

    dirs = ((uf, wf, qgf, kdf, qkf, eglf, of_ref), (ub, wb, qgb, kdb, qkb, eglb, ob_ref))
    chains = [(d, h) for d in range(2) for h in range(GDN_HEADS)]
    S = {ch: s_ref[ch[0], ch[1]] for ch in chains}
    for i in range(nch):
        Sb, vnb, rows_of, c_of = {}, {}, {}, {}
        for d, h in chains:
            c_of[d] = i if d == 0 else nch - 1 - i
            rows_of[d] = slice(c_of[d] * C, (c_of[d] + 1) * C)
        for d, h in chains:
            hs = slice(h * 128, (h + 1) * 128)
            Sb[d, h] = S[d, h].astype(BF16)
            v_new = dirs[d][0][0, rows_of[d], hs].astype(F32) - _dot(dirs[d][1][0, rows_of[d], hs], Sb[d, h])
            vnb[d, h] = v_new.astype(BF16)
        for d, h in chains:
            hs = slice(h * 128, (h + 1) * 128)
            u_ref, w_ref, qg_ref, kd_ref, qk_ref, egl_ref, o_ref = dirs[d]
            S[d, h] = S[d, h] * egl_ref[0, c_of[d], 0:1, hs] + _dot_tn(kd_ref[0, rows_of[d], hs], vnb[d, h])
        for d, h in chains:
            hs = slice(h * 128, (h + 1) * 128)
            u_ref, w_ref, qg_ref, kd_ref, qk_ref, egl_ref, o_ref = dirs[d]
            o_ref[rows_of[d], hs] = (_dot(qg_ref[0, rows_of[d], hs], Sb[d, h])
                                     + _dot(qk_ref[0, c_of[d], h], vnb[d, h])).astype(BF16)
    for ch in chains:
        s_ref[ch[0], ch[1]] = S[ch]


def gdn_scan(rw, u, w, qg, kd, qk, egl):
    T, C = GDN_ROWS, GDN_CHUNK
    nc = T // C
    R = u.shape[1]
    nbc, nbl, base = rw.Lc // T, rw.Ll // T, rw.NC // T

    def blk(d):
        def f(b, s):
            jc = s if d == 0 else nbc - 1 - s
            jl = (s - nbc) if d == 0 else nbl - 1 - (s - nbc)
            return jnp.where(s < nbc, b * nbc + jc, base + b * nbl + jl)
        return f

    in_specs = []
    for d in range(2):
        f = blk(d)
        rowspec = pl.BlockSpec((1, T, 512), lambda b, s, f=f, d=d: (d, f(b, s), 0))
        in_specs += [rowspec, rowspec, rowspec, rowspec,
                     pl.BlockSpec((1, nc, GDN_HEADS, C, C), lambda b, s, f=f, d=d: (d, f(b, s), 0, 0, 0)),
                     pl.BlockSpec((1, nc, 8, 512), lambda b, s, f=f, d=d: (d, f(b, s), 0, 0))]
    out_specs = [pl.BlockSpec((T, 512), lambda b, s, f=blk(d): (f(b, s), 0)) for d in range(2)]
    ops = (u, w, qg, kd, qk, egl)
    return pl.pallas_call(
        _gdn_scan_kernel,
        grid=(rw.B, nbc + nbl),
        in_specs=in_specs,
        out_specs=out_specs,
        out_shape=[jax.ShapeDtypeStruct((R, 512), BF16)] * 2,
        scratch_shapes=[pltpu.VMEM((2, GDN_HEADS, GDN_DK, GDN_DV), F32)],
        compiler_params=_cp("arbitrary", "arbitrary"),
        name="gdn_scan",
    )(*ops, *ops)


def _merge_kernel(yh_ref, sf_ref, sb_ref, sx_ref, sz_ref, dx_ref, snw_ref, gf_ref, gb_ref, gg_ref, gnw_ref,
                  g0_ref, g1_ref, g2_ref, w0_ref, w1_ref, w2_ref, wo_ref, x_ref, mod_ref, o_ref, ys_ref, yg_ref):
    tm = x_ref.shape[0]
    rp = 64
    for r in range(tm // rp):
        rs = slice(r * rp, (r + 1) * rp)
        y = (sf_ref[rs, :].astype(F32) + sb_ref[rs, :].astype(F32)
             + sx_ref[rs, :].astype(F32) * dx_ref[...])
        y = y * _silu(sz_ref[rs, :].astype(F32))
        parts = []
        for g in range(SSM_GROUPS):
            yg = y[:, g * SSM_GW:(g + 1) * SSM_GW]
            parts.append(yg * lax.rsqrt(jnp.mean(yg * yg, axis=-1, keepdims=True) + EPS))
        ys_ref[rs, :] = (jnp.concatenate(parts, axis=1) * snw_ref[...]).astype(BF16)
        o = gf_ref[rs, :].astype(F32) + gb_ref[rs, :].astype(F32)
        parts = []
        for h in range(GDN_HEADS):
            oh = o[:, h * 128:(h + 1) * 128]
            parts.append(oh * lax.rsqrt(jnp.mean(oh * oh, axis=-1, keepdims=True) + EPS))
        yg_ref[rs, :] = (jnp.concatenate(parts, axis=1) * gnw_ref[...]
                         * _silu(gg_ref[rs, :].astype(F32))).astype(BF16)
    m = (_sigmoid(g0_ref[...].astype(F32)) * _dot(yh_ref[...], w0_ref[...])
         + _sigmoid(g1_ref[...].astype(F32)) * _dot(ys_ref[...], w1_ref[...])
         + _sigmoid(g2_ref[...].astype(F32)) * _dot(yg_ref[...], w2_ref[...]))
    o_ref[...] = x_ref[...] + mod_ref[0, 2:3, :] * _dot(m.astype(BF16), wo_ref[...])


def merge(rw, yh, y_f, y_b, dx, ssm_nw, o_f, o_b, gdn_nw, p, w0, w1, w2, wo, x, mod):
    R, D = x.shape
    tm = min(rw.tm, 512)
    mi = rw.mod_index(tm)
    yspec = pl.BlockSpec((tm, 512), lambda i: (i, 0))
    pspec = lambda col: pl.BlockSpec((tm, 512), lambda i: (i, col // 512))
    vec = pl.BlockSpec((1, 512), lambda i: (0, 0))
    gspec = lambda k: pl.BlockSpec((tm, D), lambda i: (i, C_GATE // D + k))
    wspec = pl.BlockSpec((512, D), lambda i: (0, 0))
    return pl.pallas_call(
        _merge_kernel,
        grid=(R // tm,),
        in_specs=[yspec,
                  yspec, yspec, pspec(C_XBC), pspec(C_Z), vec, vec,
                  yspec, yspec, pspec(C_GG), vec,
                  gspec(0), gspec(1), gspec(2), wspec, wspec, wspec,
                  pl.BlockSpec((D, D), lambda i: (0, 0)),
                  pl.BlockSpec((tm, D), lambda i: (i, 0)),
                  pl.BlockSpec((1, 8, D), lambda i: (mi(i), 0, 0))],
        out_specs=pl.BlockSpec((tm, D), lambda i: (i, 0)),
        out_shape=jax.ShapeDtypeStruct((R, D), F32),
        scratch_shapes=[pltpu.VMEM((tm, 512), BF16), pltpu.VMEM((tm, 512), BF16)],
        compiler_params=_cp("arbitrary"),
        name="merge",
    )(yh, y_f, y_b, p, p, dx, ssm_nw, o_f, o_b, p, gdn_nw, p, p, p, w0, w1, w2, wo, x, mod)


def _swiglu_up_kernel(x_ref, nw_ref, mod_ref, wg_ref, wu_ref, o_ref, h_ref, g0_ref, g1_ref, u0_ref, u1_ref):
    @pl.when(pl.program_id(1) == 0)
    def _():
        h = _norm_mod(x_ref[...], nw_ref[...], mod_ref[0, 4:5, :], mod_ref[0, 3:4, :])
        h_ref[...] = h.astype(BF16)

    T, tn = o_ref.shape
    rows = g0_ref.shape[0]
    gs, us = (g0_ref, g1_ref), (u0_ref, u1_ref)

    def project(r):
        hh = h_ref[r * rows:(r + 1) * rows, :]
        gs[r % 2][...] = _dot(hh, wg_ref[...])
        us[r % 2][...] = _dot(hh, wu_ref[...])

    def finish(r):
        for q in range(rows // 64):
            for c in range(tn // 128):
                ps = (slice(q * 64, (q + 1) * 64), slice(c * 128, (c + 1) * 128))
                y = _silu(gs[r % 2][ps]) * us[r % 2][ps]
                o_ref[r * rows + q * 64:r * rows + (q + 1) * 64, ps[1]] = y.astype(o_ref.dtype)

    for r in range(T // rows):
        project(r)
        if r > 0:
            finish(r - 1)
    finish(T // rows - 1)


def swiglu_up(rw, x, nw, mod, wgu):
    R, D = x.shape
    tm = rw.tm
    tn = D_FF // 2
    nj = D_FF // tn
    mi = rw.mod_index(tm)
    return pl.pallas_call(
        _swiglu_up_kernel,
        grid=(R // tm, nj),
        in_specs=[
            pl.BlockSpec((tm, D), lambda i, j: (i, 0)),
            pl.BlockSpec((1, D), lambda i, j: (0, 0)),
            pl.BlockSpec((1, 8, D), lambda i, j: (mi(i), 0, 0)),
            pl.BlockSpec((D, tn), lambda i, j: (0, j)),
            pl.BlockSpec((D, tn), lambda i, j: (0, nj + j)),
        ],
        out_specs=pl.BlockSpec((tm, tn), lambda i, j: (i, j)),
        out_shape=jax.ShapeDtypeStruct((R, D_FF), BF16),
        scratch_shapes=[pltpu.VMEM((tm, D), BF16)] + [pltpu.VMEM((min(256, tm), tn), F32)] * 4,
        compiler_params=_cp("arbitrary", "arbitrary"),
        name="swiglu_up",
    )(x, nw.reshape(1, D), mod, wgu, wgu)


def _swiglu_down_kernel(a_ref, w_ref, x_ref, mod_ref, o_ref):
    o_ref[...] = x_ref[...] + mod_ref[0, 5:6, :] * _dot(a_ref[...], w_ref[...])


def swiglu_down(rw, a, w, x, mod):
    R, D = x.shape
    tm = min(rw.tm, 512)
    mi = rw.mod_index(tm)
    return pl.pallas_call(
        _swiglu_down_kernel,
        grid=(R // tm,),
        in_specs=[
            pl.BlockSpec((tm, D_FF), lambda i: (i, 0)),
            pl.BlockSpec((D_FF, D), lambda i: (0, 0)),
            pl.BlockSpec((tm, D), lambda i: (i, 0)),
            pl.BlockSpec((1, 8, D), lambda i: (mi(i), 0, 0)),
        ],
        out_specs=pl.BlockSpec((tm, D), lambda i: (i, 0)),
        out_shape=jax.ShapeDtypeStruct((R, D), F32),
        compiler_params=_cp("arbitrary"),
        name="swiglu_down",
    )(a, w, x, mod)


def _final_norm_kernel(x_ref, w_ref, o_ref):
    x = x_ref[...]
    ms = jnp.mean(x * x, axis=-1, keepdims=True)
    o_ref[...] = x * lax.rsqrt(ms + EPS) * w_ref[...]


def final_norm(rw, x, w):
    D = x.shape[1]
    tm = rw.tm
    n0 = rw.NC // tm
    nl = rw.B * rw.Ll
    return pl.pallas_call(
        _final_norm_kernel,
        grid=(nl // tm,),
        in_specs=[pl.BlockSpec((tm, D), lambda i: (n0 + i, 0)), pl.BlockSpec((1, D), lambda i: (0, 0))],
        out_specs=pl.BlockSpec((tm, D), lambda i: (i, 0)),
        out_shape=jax.ShapeDtypeStruct((nl, D), F32),
        compiler_params=_cp("arbitrary"),
        name="final_norm",
    )(x, w.reshape(1, D))


def _regroup_w_in(w_in):
    o_dt = 3072
    o_gdn = 3088
    o_a = o_gdn + 2048
    o_b = o_a + 8
    o_gate = o_gdn + 2064
    w_in = w_in.astype(BF16)
    pieces = [
        w_in[..., 0:3072],
        w_in[..., o_gdn:o_gdn + 2048],
        w_in[..., o_gate:o_gate + 3072],
        w_in[..., o_dt:o_dt + 16],
        w_in[..., o_a:o_a + 4], w_in[..., o_b:o_b + 4],
        w_in[..., o_a + 4:o_a + 8], w_in[..., o_b + 4:o_b + 8],
        jnp.zeros(w_in.shape[:-1] + (N_IN_PAD - C_SM - 32,), w_in.dtype),
    ]
    return jnp.concatenate(pieces, axis=-1)


def kernel(x, c, ctx, c_ctx, w_ada, b_ada, norm1_w, norm2_w, w_in, hy_conv_w, hy_conv_b, hy_w1, hy_b1, hy_w2, hy_b2, hy_w3, hy_freq, hy_bias, ssm_conv_w, ssm_conv_b, ssm_dt_bias, ssm_A_log, ssm_D, ssm_norm_w, gdn_conv_w, gdn_dt_bias, gdn_A_log, gdn_norm_w, w_hy_out, w_ssm_out, w_gdn_out, w_out, w_gate_up, w_down, final_norm_w):
    B, Ll, D = x.shape
    Lc = ctx.shape[1]
    depth = w_ada.shape[0]
    assert Lc == CONV_ROWS and D == D_MODEL and B <= 15
    rw = Rows(B, Lc, Ll)
    R, NC = rw.R, rw.NC

    xa = jnp.concatenate([ctx.reshape(B * Lc, D), x.reshape(B * Ll, D)], axis=0)

    svec = jnp.concatenate([c_ctx[None, :], c, jnp.zeros((15 - B, D), F32)], axis=0)
    mod_all = ada_modulation(svec, w_ada, b_ada)
    mod_all = jnp.pad(mod_all.reshape(depth, 16, 6, D), ((0, 0), (0, 0), (0, 2), (0, 0)))

    w_in_r = _regroup_w_in(w_in)
    fwd_l, inv_l = dft_tables(Ll)
    fwd_c, inv_c = dft_tables(Lc)
    feat_l, feat_c = hy_features(Ll), hy_features(Lc)

    for l in range(depth):
        mod = mod_all[l]
        par = _in_proj_params(hy_conv_w[l], hy_conv_b[l], ssm_conv_w[l], ssm_conv_b[l], gdn_conv_w[l],
                              ssm_dt_bias[l], gdn_dt_bias[l], gdn_A_log[l])
        p, sm = in_proj(rw, xa, norm1_w[l], mod, w_in_r[l], par)

        sm32_t = sm[:, :32].T
        dt_t = sm32_t[:16].reshape(2, 8, R)
        g_t = sm32_t[16:32].reshape(16, R // GDN_CHUNK, GDN_CHUNK).transpose(1, 0, 2)

        alx = jnp.repeat(ssm_A_log[l], SSM_HEAD_DIM, axis=-1).reshape(2, 1, 512)
        alc = ssm_A_log[l].reshape(2, 8, 1)
        y_f, y_b = ssd_scan(rw, p, sm, dt_t, alx, alc)
        dx = jnp.repeat(ssm_D[l], SSM_HEAD_DIM).reshape(1, 512)

        o_f, o_b = gdn_scan(rw, *gdn_prep(p, sm, g_t))

        hyu = p
        parts = []
        for (Bn, L, blk0, fwd, inv, feat) in ((B, Lc, 0, fwd_c, inv_c, feat_c),
                                              (B, Ll, NC // Ll, fwd_l, inv_l, feat_l)):
            if NC % L:
                raise ValueError("latent length must divide the context row count")
            filt = hy_filter(feat, hy_w1[l], hy_b1[l], hy_w2[l], hy_b2[l], hy_w3[l], hy_freq[l])
            kspec = matmul(fwd, filt, min(512, 2 * L), 512)
            z1 = long_conv(Bn, L, hyu, blk0, 0, hyu, blk0, 1, hy_bias[l, 0], fwd, inv, kspec, 0, F32)
            yy = long_conv(Bn, L, z1, 0, 0, hyu, blk0, 2, hy_bias[l, 1], fwd, inv, kspec, 1, BF16)
            parts.append(yy)
        y_hy = jnp.concatenate(parts, axis=0)

        xa = merge(rw, y_hy, y_f, y_b, dx, ssm_norm_w[l].reshape(1, 512),
                   o_f, o_b, jnp.tile(gdn_norm_w[l], GDN_HEADS).reshape(1, 512), p,
                   w_hy_out[l].astype(BF16), w_ssm_out[l].astype(BF16),
                   w_gdn_out[l].astype(BF16), w_out[l].astype(BF16), xa, mod)
        act = swiglu_up(rw, xa, norm2_w[l], mod, w_gate_up[l].astype(BF16))
        xa = swiglu_down(rw, act, w_down[l].astype(BF16), xa, mod)

    out = final_norm(rw, xa, final_norm_w)
    return out.reshape(B, Ll, D)
```

```python
import functools
import math

import jax
import jax.numpy as jnp
import numpy as np
from jax import lax
from jax.experimental import pallas as pl
from jax.experimental.pallas import tpu as pltpu

F32 = jnp.float32
BF16 = jnp.bfloat16
HI = lax.Precision.HIGHEST

EPS = 1e-6
D_MODEL = 1024
GRID_W = 64

HY_WIDTH = 512
HY_BANDS = 16
HY_EMB = 1 + 2 * HY_BANDS
HY_HIDDEN = 64
HY_SHORT_DECAY_PCT = 0.3
HY_LONG_DECAY_PCT = 1.5
HY_TARGET = 1e-2

SSM_HEADS = 8
SSM_HEAD_DIM = 64
SSM_WIDTH = 512
SSM_GROUPS = 2
SSM_HPG = 4
SSM_STATE = 128
SSM_CHUNK = 128
SSM_GW = SSM_HPG * SSM_HEAD_DIM

GDN_HEADS = 4
GDN_DK = 128
GDN_DV = 128
GDN_CHUNK = 64

D_FF = 2816

C_HY = 0
C_Z = 1536
C_XBC = 2048
C_QKV = 3072
C_GG = 4608
C_GATE = 5120
C_SM = 8192

CONV_ROWS = 256
FREQ_BLK = 512

VMEM_LIMIT = 56 * 1024 * 1024


def _cp(*sem, flags=None):
    return pltpu.CompilerParams(dimension_semantics=sem, vmem_limit_bytes=VMEM_LIMIT, flags=flags)


def _sigmoid(x):
    return 1.0 / (1.0 + jnp.exp(-x))


def _silu(x):
    return x * _sigmoid(x)


def _softplus(x):
    return jnp.maximum(x, 0.0) + jnp.log1p(jnp.exp(-jnp.abs(x)))


def _dot(a, b, precision=None):
    return jnp.dot(a, b, precision=precision, preferred_element_type=F32)


def _dot_nt(a, b):
    return lax.dot_general(a, b, (((1,), (1,)), ((), ())), preferred_element_type=F32)


def _dot_tn(a, b):
    return lax.dot_general(a, b, (((0,), (0,)), ((), ())), preferred_element_type=F32)


def _ada_kernel(s_ref, w_ref, b_ref, o_ref):
    s = _silu(s_ref[...])
    o_ref[0] = _dot(s, w_ref[0], HI) + b_ref[0]


def ada_modulation(svec, w_ada, b_ada):
    depth = w_ada.shape[0]
    D = D_MODEL
    return pl.pallas_call(
        _ada_kernel,
        grid=(depth, 6),
        in_specs=[
            pl.BlockSpec((16, D), lambda l, j: (0, 0)),
            pl.BlockSpec((1, D, D), lambda l, j: (l, 0, j)),
            pl.BlockSpec((1, 1, D), lambda l, j: (l, 0, j)),
        ],
        out_specs=pl.BlockSpec((1, 16, D), lambda l, j: (l, 0, j)),
        out_shape=jax.ShapeDtypeStruct((depth, 16, 6 * D), F32),
        compiler_params=_cp("arbitrary", "arbitrary"),
        name="ada",
    )(svec, w_ada, b_ada.reshape(depth, 1, 6 * D))


def _norm_mod(x, nw, scale, shift):
    ms = jnp.mean(x * x, axis=-1, keepdims=True)
    return (x * lax.rsqrt(ms + EPS) * nw) * (1.0 + scale) + shift


N_IN_PAD = C_SM + 128
IN_TN = N_IN_PAD // 5
MODE_RAW, MODE_CONV, MODE_CONV_SILU, MODE_CONV_SILU_L2, MODE_SMALL = range(5)


def _tile_mode(tile):
    col = tile * 128
    if col < C_Z:
        return MODE_CONV
    if col < C_XBC:
        return MODE_RAW
    if col < C_QKV:
        return MODE_CONV_SILU
    if col < C_QKV + 1024:
        return MODE_CONV_SILU_L2
    if col < C_GG:
        return MODE_CONV_SILU
    if col < C_SM:
        return MODE_RAW
    return MODE_SMALL
PAR_W0, PAR_W1, PAR_W2, PAR_BIAS, PAR_L2SCALE, PAR_SBIAS, PAR_SALOG, PAR_SKIND = range(8)


def _in_proj_kernel(xc_ref, xl_ref, nw_ref, mod_ref, w_ref, par_ref, o_ref, sm_ref, h_ref, raw0_ref, raw1_ref, *,
                    nctx_blk):
    j = pl.program_id(1)
    nj = N_IN_PAD // IN_TN
    raws = (raw0_ref, raw1_ref)

    @pl.when((j == 0) & (pl.program_id(0) < nctx_blk))
    def _():
        h = _norm_mod(xc_ref[...], nw_ref[...], mod_ref[0, 1:2, :], mod_ref[0, 0:1, :])
        h_ref[...] = h.astype(BF16)

    @pl.when((j == 0) & (pl.program_id(0) >= nctx_blk))
    def _():
        h = _norm_mod(xl_ref[...], nw_ref[...], mod_ref[0, 1:2, :], mod_ref[0, 0:1, :])
        h_ref[...] = h.astype(BF16)

    T = h_ref.shape[0]
    G = GRID_W
    per_ctx = CONV_ROWS // G
    is_latent = pl.program_id(0) >= nctx_blk
    sub = lax.broadcasted_iota(jnp.int32, (8, 128), 0)

    def raw_piece(src, g, c):
        return src[g * G:(g + 1) * G, c * 128:(c + 1) * 128]

    def conv(src, g, c):
        cs = slice(c * 128, (c + 1) * 128)
        x = raw_piece(src, g, c)
        zero = jnp.zeros((1, 128), F32)
        before = zero if g % per_ctx == 0 else jnp.where(is_latent, 0.0, src[g * G - 1:g * G, cs])
        after = zero if g % per_ctx == per_ctx - 1 else jnp.where(is_latent, 0.0, src[(g + 1) * G:(g + 1) * G + 1, cs])
        rp = pltpu.roll(x, 1, 0)
        rn = pltpu.roll(x, G - 1, 0)
        prev = jnp.concatenate([jnp.where(sub == 0, before, rp[0:8]), rp[8:]], axis=0)
        nxt = jnp.concatenate([rn[:G - 8], jnp.where(sub == 7, after, rn[G - 8:])], axis=0)
        return (prev * par_ref[PAR_W0:PAR_W0 + 1, cs] + x * par_ref[PAR_W1:PAR_W1 + 1, cs]
                + nxt * par_ref[PAR_W2:PAR_W2 + 1, cs] + par_ref[PAR_BIAS:PAR_BIAS + 1, cs])

    def conv_silu(src, g, c):
        return _silu(conv(src, g, c))

    def conv_silu_l2(src, g, c):
        y = _silu(conv(src, g, c))
        y = y * lax.rsqrt(jnp.sum(y * y, axis=-1, keepdims=True) + EPS)
        return y * par_ref[PAR_L2SCALE:PAR_L2SCALE + 1, c * 128:(c + 1) * 128]

    def small(src, g, c):
        cs = slice(c * 128, (c + 1) * 128)
        acc = raw_piece(src, g, c)
        kind = par_ref[PAR_SKIND:PAR_SKIND + 1, cs]
        sp = _softplus(acc + par_ref[PAR_SBIAS:PAR_SBIAS + 1, cs])
        dec = -jnp.exp(par_ref[PAR_SALOG:PAR_SALOG + 1, cs]) * sp
        return jnp.where(kind == 0.0, sp, jnp.where(kind == 1.0, dec, jnp.where(kind == 2.0, _sigmoid(acc), 0.0)))

    rows_mm = 256
    tiles = IN_TN // 128
    piece_fn = {MODE_RAW: raw_piece, MODE_CONV: conv, MODE_CONV_SILU: conv_silu,
                MODE_CONV_SILU_L2: conv_silu_l2, MODE_SMALL: small}

    def project(dst, r):
        rs = slice(r * rows_mm, (r + 1) * rows_mm)
        dst[rs, :] = _dot(h_ref[rs, :], w_ref[...])

    def finish(src, blk, g, c):
        mode = _tile_mode(blk * tiles + c)
        y = piece_fn[mode](src, g, c)
        if mode == MODE_SMALL:
            sm_ref[g * G:(g + 1) * G, :] = y
            y = jnp.zeros_like(y)
        o_ref[g * G:(g + 1) * G, c * 128:(c + 1) * 128] = y.astype(o_ref.dtype)

    for step in range(nj + 1):
        @pl.when(j == step)
        def _(step=step):
            blk = step - 1
            src, dst = raws[blk % 2], raws[step % 2]
            for r in range(T // rows_mm):
                if step < nj:
                    project(dst, r)
                if blk < 0:
                    continue
                for g in range(r * rows_mm // G, (r + 1) * rows_mm // G):
                    for c in range(tiles):
                        finish(src, blk, g, c)


class Rows:
    def __init__(self, B, Lc, Ll):
        self.B, self.Lc, self.Ll = B, Lc, Ll
        self.NC = B * Lc
        self.R = B * Lc + B * Ll
        assert self.NC % Ll == 0 or Ll % self.NC == 0
        tm = 1024
        while self.NC % tm or Ll % tm:
            tm //= 2
        self.tm = tm

    def mod_index(self, tm):
        nctx = self.NC // tm
        per = self.Ll // tm
        return lambda i: jnp.where(i < nctx, 0, 1 + (i - nctx) // per)


def _stream_specs(rw, tm, xs, ngrid):
    xc, xl = xs
    nctx = rw.NC // tm
    off = nctx if xl.shape[0] == rw.R else 0
    D = xc.shape[1]
    if ngrid == 1:
        return [pl.BlockSpec((tm, D), lambda i: (jnp.minimum(i, nctx - 1), 0)),
                pl.BlockSpec((tm, D), lambda i: (jnp.maximum(i - nctx, 0) + off, 0))]
    return [pl.BlockSpec((tm, D), lambda i, j: (jnp.minimum(i, nctx - 1), 0)),
            pl.BlockSpec((tm, D), lambda i, j: (jnp.maximum(i - nctx, 0) + off, 0))]


def in_proj(rw, xs, l, nw, mod, w, par):
    R = rw.R
    D = xs[0].shape[1]
    N = w.shape[2]
    tm, tn = rw.tm, IN_TN
    nj = N // tn
    assert N == N_IN_PAD
    mi = rw.mod_index(tm)
    done = lambda j: jnp.maximum(j - 1, 0)
    return pl.pallas_call(
        functools.partial(_in_proj_kernel, nctx_blk=rw.NC // tm),
        grid=(R // tm, nj + 1),
        in_specs=_stream_specs(rw, tm, xs, 2) + [
            pl.BlockSpec((None, 1, D), lambda i, j: (l, 0, 0)),
            pl.BlockSpec((None, 1, 8, D), lambda i, j: (l, mi(i), 0, 0)),
            pl.BlockSpec((None, D, tn), lambda i, j: (l, 0, jnp.minimum(j, nj - 1))),
            pl.BlockSpec((None, 8, tn), lambda i, j: (l, 0, done(j))),
        ],
        out_specs=[pl.BlockSpec((tm, tn), lambda i, j: (i, done(j))),
                   pl.BlockSpec((tm, 128), lambda i, j: (i, 0))],
        out_shape=[jax.ShapeDtypeStruct((R, N), BF16), jax.ShapeDtypeStruct((R, 128), F32)],
        scratch_shapes=[pltpu.VMEM((tm, D), BF16), pltpu.VMEM((tm, tn), F32), pltpu.VMEM((tm, tn), F32)],
        compiler_params=_cp("arbitrary", "arbitrary"),
        name="in_proj",
    )(xs[0], xs[1], nw, mod, w, par)


def _in_proj_params(hy_conv_w, hy_conv_b, ssm_conv_w, ssm_conv_b, gdn_conv_w, ssm_dt_bias, gdn_dt_bias, gdn_A_log):
    depth = hy_conv_w.shape[0]

    def row(pieces):
        out, pos = [], 0
        for off, a in pieces:
            out += [jnp.zeros((depth, off - pos), F32), a.astype(F32)]
            pos = off + a.shape[1]
        return jnp.concatenate(out + [jnp.zeros((depth, N_IN_PAD - pos), F32)], axis=1)
    z4 = jnp.zeros((depth, 4), F32)
    conv = [row([(C_HY, hy_conv_w[:, t]), (C_XBC, ssm_conv_w[:, t]), (C_QKV, gdn_conv_w[:, t])]) for t in range(3)]
    bias = row([(C_HY, hy_conv_b), (C_XBC, ssm_conv_b)])
    l2s = row([(C_QKV, jnp.full((depth, 512), GDN_DK ** -0.5, F32)), (C_QKV + 512, jnp.ones((depth, 512), F32))])
    sbias = row([(C_SM, jnp.concatenate([ssm_dt_bias.reshape(depth, 16), gdn_dt_bias[:, 0], z4,
                                         gdn_dt_bias[:, 1], z4], axis=1))])
    salog = row([(C_SM + 16, jnp.concatenate([gdn_A_log[:, 0], z4, gdn_A_log[:, 1], z4], axis=1))])
    kind = np.full((depth, N_IN_PAD), 3.0, np.float32)
    kind[:, C_SM:C_SM + 16] = 0.0
    kind[:, C_SM + 16:C_SM + 20] = 1.0
    kind[:, C_SM + 24:C_SM + 28] = 1.0
    kind[:, C_SM + 20:C_SM + 24] = 2.0
    kind[:, C_SM + 28:C_SM + 32] = 2.0
    return jnp.stack(conv + [bias, l2s, sbias, salog, jnp.asarray(kind)], axis=1)


def _hy_filter_kernel(z_ref, w1_ref, b1_ref, w2_ref, b2_ref, w3_ref, f0_ref, f1_ref, win_ref, o_ref, h_ref):
    @pl.when(pl.program_id(1) == 0)
    def _():
        h1 = jnp.sin(f0_ref[...] * (_dot(z_ref[...], w1_ref[...], HI) + b1_ref[...]))
        h_ref[...] = jnp.sin(f1_ref[...] * (_dot(h1, w2_ref[...], HI) + b2_ref[...]))

    h = _dot(h_ref[...], w3_ref[...], HI) * win_ref[...]
    tl = h.shape[0]
    row = lax.broadcasted_iota(jnp.int32, (tl, 1), 0) + pl.program_id(0) * tl
    drop = (row == 0) & (pl.program_id(1) % 2 == 1)
    o_ref[...] = jnp.where(drop, 0.0, h).astype(o_ref.dtype)


def hy_features(L):
    t = jnp.linspace(0.0, 1.0, L, dtype=F32)[:, None]
    w = 2.0 * math.pi * jnp.arange(L, dtype=F32)[:, None] / L
    f = jnp.linspace(1e-4, HY_BANDS - 1, HY_BANDS, dtype=F32)[None, :]
    z = jnp.concatenate([t, jnp.cos(f * w), -jnp.sin(f * w)], axis=-1)
    z = jnp.pad(z, ((0, 0), (0, 128 - HY_EMB)))
    min_decay = math.log(HY_TARGET) / HY_LONG_DECAY_PCT
    max_decay = math.log(HY_TARGET) / HY_SHORT_DECAY_PCT
    deltas = jnp.linspace(min_decay, max_decay, HY_WIDTH, dtype=F32)
    window = jnp.exp(-t * jnp.abs(deltas))
    return z, window


def hy_filter(feat, w1, b1, w2, b2, w3, freq):
    z, window = feat
    L = z.shape[0]
    H = HY_HIDDEN
    w1p = jnp.pad(w1, ((0, 128 - HY_EMB), (0, 128 - H)))
    w2p = jnp.pad(w2, ((0, 128 - H), (0, 128 - H)))
    w3p = jnp.pad(w3, ((0, 128 - H), (0, 0)))
    pad1 = lambda v: jnp.pad(v, (0, 128 - H)).reshape(1, 128)
    tl = 256
    full = lambda shape: pl.BlockSpec(shape, lambda i, j: (0, 0))
    return pl.pallas_call(
        _hy_filter_kernel,
        grid=(L // tl, 4),
        in_specs=[
            pl.BlockSpec((tl, 128), lambda i, j: (i, 0)),
            full((128, 128)), full((1, 128)), full((128, 128)), full((1, 128)),
            pl.BlockSpec((128, HY_WIDTH), lambda i, j: (0, j)),
            full((1, 128)), full((1, 128)),
            pl.BlockSpec((tl, HY_WIDTH), lambda i, j: (i, 0)),
        ],
        out_specs=pl.BlockSpec((tl, HY_WIDTH), lambda i, j: (i, j)),
        out_shape=jax.ShapeDtypeStruct((L, 4 * HY_WIDTH), BF16),
        scratch_shapes=[pltpu.VMEM((tl, 128), F32)],
        compiler_params=_cp("arbitrary", "arbitrary"),
        name="hy_filter",
    )(z, w1p, pad1(b1), w2p, pad1(b2), w3p, pad1(freq[0]), pad1(freq[1]), window)


def dft_tables(L):
    N = 2 * L
    f = np.arange(L, dtype=np.int64)[:, None]
    s = np.arange(L, dtype=np.int64)[None, :]
    ang = ((f * s) % N).astype(np.float64) * (2.0 * math.pi / N)
    c, sn = np.cos(ang), np.sin(ang)
    alt_s = (1 - 2 * (s % 2)).astype(np.float64)
    alt_t = (1 - 2 * (f % 2)).astype(np.float64)
    fwd = np.concatenate([c, np.where(f == 0, alt_s, -sn)], axis=0)
    wgt = np.where(s == 0, 1.0, 2.0) / N
    inv = np.concatenate([c * wgt, np.where(s == 0, alt_t / N, -sn * wgt)], axis=1)
    return jnp.asarray(fwd, dtype=BF16), jnp.asarray(inv, dtype=BF16)


def _matmul_kernel(a_ref, b_ref, o_ref):
    o_ref[...] = _dot(a_ref[...], b_ref[...])


def matmul(a, b, tm, tn):
    M, K = a.shape
    N = b.shape[1]
    return pl.pallas_call(
        _matmul_kernel,
        grid=(M // tm, N // tn),
        in_specs=[pl.BlockSpec((tm, K), lambda i, j: (i, 0)), pl.BlockSpec((K, tn), lambda i, j: (0, j))],
        out_specs=pl.BlockSpec((tm, tn), lambda i, j: (i, j)),
        out_shape=jax.ShapeDtypeStruct((M, N), F32),
        compiler_params=_cp("arbitrary", "arbitrary"),
        name="matmul",
    )(a, b)


def _long_conv_kernel(u_ref, g_ref, bias_ref, fr_ref, fi_ref, ic_ref, is_ref,
                      ar0_ref, ar1_ref, ai0_ref, ai1_ref, o_ref, ub_ref, acc_ref):
    f = pl.program_id(1)

    @pl.when(f == 0)
    def _():
        ub_ref[...] = u_ref[...].astype(BF16)
        acc_ref[...] = jnp.zeros_like(acc_ref)

    ub = ub_ref[...]
    ur = _dot(fr_ref[...], ub)
    ui = _dot(fi_ref[...], ub)
    kr = ar0_ref[...] + ar1_ref[...]
    nyq = (lax.broadcasted_iota(jnp.int32, (fr_ref.shape[0], 1), 0) == 0) & (f == 0)
    ki = jnp.where(nyq, ai0_ref[...] + ai1_ref[...], ai0_ref[...] - ai1_ref[...])
    pr = jnp.where(nyq, ur * kr, ur * kr - ui * ki)
    pi = jnp.where(nyq, ui * ki, ur * ki + ui * kr)
    acc_ref[...] += _dot(ic_ref[...], pr.astype(BF16)) + _dot(is_ref[...], pi.astype(BF16))

    @pl.when(f == pl.num_programs(1) - 1)
    def _():
        u = u_ref[...].astype(F32)
        o_ref[...] = (g_ref[...].astype(F32) * (acc_ref[...] + u * bias_ref[...])).astype(o_ref.dtype)


def long_conv(B, L, u, u_rb0, u_cb, gate, g_rb0, gate_cb, bias, fwd, inv, kspec, order, out_dtype):
    C = HY_WIDTH
    FB = min(FREQ_BLK, L)
    nfb = L // FB
    return pl.pallas_call(
        _long_conv_kernel,
        grid=(B, nfb),
        in_specs=[
            pl.BlockSpec((L, C), lambda b, f: (u_rb0 + b, u_cb)),
            pl.BlockSpec((L, C), lambda b, f: (g_rb0 + b, gate_cb)),
            pl.BlockSpec((1, C), lambda b, f: (0, 0)),
            pl.BlockSpec((FB, L), lambda b, f: (f, 0)),
            pl.BlockSpec((FB, L), lambda b, f: (nfb + f, 0)),
            pl.BlockSpec((L, FB), lambda b, f: (0, f)),
            pl.BlockSpec((L, FB), lambda b, f: (0, nfb + f)),
            pl.BlockSpec((FB, C), lambda b, f: (f, 2 * order)),
            pl.BlockSpec((FB, C), lambda b, f: (f, 2 * order + 1)),
            pl.BlockSpec((FB, C), lambda b, f: (nfb + f, 2 * order)),
            pl.BlockSpec((FB, C), lambda b, f: (nfb + f, 2 * order + 1)),
        ],
        out_specs=pl.BlockSpec((L, C), lambda b, f: (b, 0)),
        out_shape=jax.ShapeDtypeStruct((B * L, C), out_dtype),
        scratch_shapes=[pltpu.VMEM((L, C), BF16), pltpu.VMEM((L, C), F32)],
        compiler_params=_cp("arbitrary", "arbitrary"),
        name="long_conv",
    )(u, gate, bias.reshape(1, C), fwd, fwd, inv, inv, kspec, kspec, kspec, kspec)


def _scan_blocks(rw, rows):
    nbc, nbl, base = rw.Lc // rows, rw.Ll // rows, rw.NC // rows

    def make(d):
        def f(b, s):
            jc = s if d == 0 else nbc - 1 - s
            jl = (s - nbc) if d == 0 else nbl - 1 - (s - nbc)
            return jnp.where(s < nbc, b * nbc + jc, base + b * nbl + jl)
        return f

    return [make(0), make(1)], nbc + nbl


def _expand_lanes(x, base, n, width):
    rows = x.shape[0]
    per = 128 // width
    lane = lax.broadcasted_iota(jnp.int32, (rows, 128), 1)
    tiles = []
    for t in range(n // per):
        c0 = base + t * per
        tile = jnp.broadcast_to(x[:, c0:c0 + 1], (rows, 128))
        for i in range(1, per):
            tile = jnp.where(lane >= i * width, jnp.broadcast_to(x[:, c0 + i:c0 + i + 1], (rows, 128)), tile)
        tiles.append(tile)
    return jnp.concatenate(tiles, axis=1)


def _ssd_kernel(xf, bf, cf, smf, dtf, xb, bb, cb_, smb, dtb, alx_ref, alc_ref, of_ref, ob_ref, h_ref):
    Q = SSM_CHUNK
    GW = SSM_GW

    @pl.when(pl.program_id(1) == 0)
    def _():
        h_ref[...] = jnp.zeros_like(h_ref)

    row = lax.broadcasted_iota(jnp.int32, (Q, Q), 0)
    col = lax.broadcasted_iota(jnp.int32, (Q, Q), 1)
    lane_head = lax.broadcasted_iota(jnp.int32, (Q, GW), 1) // SSM_HEAD_DIM
    dirs = ((xf, bf, cf, smf, dtf, of_ref), (xb, bb, cb_, smb, dtb, ob_ref))
    jobs = []
    for d in range(2):
        x_ref, b_ref, c_ref, sm_ref, dt_ref, o_ref = dirs[d]
        keep = (col <= row) if d == 0 else (col >= row)
        tri = keep.astype(BF16)
        tri_t = ((row <= col) if d == 0 else (row >= col)).astype(BF16)
        sm = sm_ref[...]
        a_x = -jnp.exp(alx_ref[d])
        dtx = _expand_lanes(sm, 8 * d, SSM_HEADS, SSM_HEAD_DIM)
        cumx = _expand_lanes(_dot_01_lhs(tri, sm), 8 * d, SSM_HEADS, SSM_HEAD_DIM) * a_x
        cumr = _dot_01_rhs(dt_ref[0], tri_t) * (-jnp.exp(alc_ref[d]))
        last = Q - 1 if d == 0 else 0
        totx = cumx[last:last + 1, :]
        xd = x_ref[...].astype(F32) * dtx
        xdw = xd * jnp.exp(totx - cumx)
        ecum = jnp.exp(cumx)
        for g in range(SSM_GROUPS):
            gs = slice(g * GW, (g + 1) * GW)
            jobs.append(dict(d=d, g=g, gs=gs, keep=keep, cumx=cumx, cumr=cumr, o_ref=o_ref,
                             bg=b_ref[:, g * SSM_STATE:(g + 1) * SSM_STATE].astype(BF16),
                             cg=c_ref[:, g * SSM_STATE:(g + 1) * SSM_STATE].astype(BF16),
                             xdg=xd[:, gs], xdw=xdw[:, gs].astype(BF16), ecum=ecum[:, gs],
                             etot=jnp.exp(totx[:, gs])))
    for j in jobs:
        j["cb"] = _dot_nt(j["cg"], j["bg"])
        j["h"] = h_ref[j["d"], j["g"]]
    for j in jobs:
        ms, xs = [], []
        for e4 in range(SSM_HPG):
            e = j["g"] * SSM_HPG + e4
            diff = j["cumx"][:, e * SSM_HEAD_DIM:e * SSM_HEAD_DIM + 1] - j["cumr"][e:e + 1, :]
            ms.append((j["cb"] * jnp.where(j["keep"], jnp.exp(diff), 0.0)).astype(BF16))
            xs.append(jnp.where(lane_head == e4, j["xdg"], 0.0).astype(BF16))
        yd = _dot(jnp.concatenate(ms, axis=1), jnp.concatenate(xs, axis=0))
        y_off = _dot(j["cg"], j["h"].astype(BF16)) * j["ecum"]
        j["o_ref"][:, j["gs"]] = (yd + y_off).astype(BF16)
    for j in jobs:
        h_ref[j["d"], j["g"]] = j["h"] * j["etot"] + _dot_tn(j["bg"], j["xdw"])


def ssd_scan(rw, p, sm, dtT, alx, alc):
    Q = SSM_CHUNK
    blks, nsteps = _scan_blocks(rw, Q)
    R = p.shape[0]
    in_specs = []
    for d in range(2):
        f = blks[d]
        in_specs += [
            pl.BlockSpec((Q, 512), lambda b, s, f=f: (f(b, s), C_XBC // 512)),
            pl.BlockSpec((Q, 256), lambda b, s, f=f: (f(b, s), C_XBC // 256 + 2)),
            pl.BlockSpec((Q, 256), lambda b, s, f=f: (f(b, s), C_XBC // 256 + 3)),
            pl.BlockSpec((Q, 128), lambda b, s, f=f: (f(b, s), 0)),
            pl.BlockSpec((1, 8, Q), lambda b, s, f=f, d=d: (d, 0, f(b, s))),
        ]
    in_specs += [pl.BlockSpec((2, 1, 512), lambda b, s: (0, 0, 0)), pl.BlockSpec((2, 8, 1), lambda b, s: (0, 0, 0))]
    ops = (p, p, p, sm, dtT)
    return pl.pallas_call(
        _ssd_kernel,
        grid=(rw.B, nsteps),
        in_specs=in_specs,
        out_specs=[pl.BlockSpec((Q, 512), lambda b, s, f=blks[d]: (f(b, s), 0)) for d in range(2)],
        out_shape=[jax.ShapeDtypeStruct((R, 512), BF16)] * 2,
        scratch_shapes=[pltpu.VMEM((2, SSM_GROUPS, SSM_STATE, SSM_GW), F32)],
        compiler_params=_cp("arbitrary", "arbitrary"),
        name="ssd_scan",
    )(*ops, *ops, alx, alc)


def _split3(x):
    x1 = x.astype(BF16)
    r = x - x1.astype(F32)
    x2 = r.astype(BF16)
    x3 = (r - x2.astype(F32)).astype(BF16)
    return x1, x2, x3


def _dot_01_lhs(m01, x):
    x1, x2, x3 = _split3(x)
    return _dot(m01, x1) + _dot(m01, x2) + _dot(m01, x3)


def _dot_01_rhs(x, m01):
    x1, x2, x3 = _split3(x)
    return _dot(x1, m01) + _dot(x2, m01) + _dot(x3, m01)


GDN_ROWS = 256


def _gdn_prep_kernel(q_ref, k_ref, v_ref, sm_ref, gT_ref, u_ref, w_ref, qg_ref, kd_ref, qk_ref, egl_ref):
    C = GDN_CHUNK
    row = lax.broadcasted_iota(jnp.int32, (C, C), 0)
    col = lax.broadcasted_iota(jnp.int32, (C, C), 1)
    jobs = []
    levels = []
    for d in range(2):
        keep = (col <= row) if d == 0 else (col >= row)
        late, early = (row, col) if d == 0 else (col, row)
        levels.append([(((row ^ col) >> (j + 1)) == 0) & ((late & (1 << j)) != 0) & ((early & (1 << j)) == 0)
                       for j in range(6)])
        tri = keep.astype(BF16)
        tri_t = ((row <= col) if d == 0 else (row >= col)).astype(BF16)
        last = C - 1 if d == 0 else 0
        for c in range(GDN_ROWS // C):
            rows = slice(c * C, (c + 1) * C)
            smc = sm_ref[rows, :]
            cums = _dot_01_lhs(tri, smc)
            cumr = _dot_01_rhs(gT_ref[c, 8 * d:8 * d + 8, :], tri_t)
            tot = cums[last:last + 1, :]
            for h in range(GDN_HEADS):
                lg = 16 + 8 * d + h
                jobs.append(dict(d=d, c=c, h=h, rows=rows, hs=slice(h * 128, (h + 1) * 128), keep=keep,
                                 gc=cums[:, lg:lg + 1], beta=smc[:, lg + 4:lg + 5],
                                 gl=tot[:, lg:lg + 1], gr=cumr[h:h + 1, :]))
    for j in jobs:
        q = q_ref[j["rows"], j["hs"]].astype(F32)
        k = k_ref[j["rows"], j["hs"]].astype(F32)
        j["dec"] = jnp.where(j["keep"], jnp.exp(j["gc"] - j["gr"]), 0.0)
        kb = k * j["beta"]
        both = _dot_nt(jnp.concatenate([kb, q], axis=0).astype(BF16), k.astype(BF16))
        j["a"] = both[:C] * j["dec"]
        j["n"] = -jnp.where(levels[j["d"]][0], j["a"], 0.0)
        qk_ref[j["d"], j["c"], j["h"]] = (both[C:] * j["dec"]).astype(BF16)
    for lev in range(1, 6):
        for j in jobs:
            l = jnp.where(levels[j["d"]][lev], j["a"], 0.0)
            j["y"] = l + _dot(l.astype(BF16), j["n"].astype(BF16))
        for j in jobs:
            j["n"] = j["n"] - j["y"] - _dot(j["n"].astype(BF16), j["y"].astype(BF16))
    for j in jobs:
        d, rows, hs, gc, gl, beta = j["d"], j["rows"], j["hs"], j["gc"], j["gl"], j["beta"]
        q = q_ref[rows, hs].astype(F32)
        k = k_ref[rows, hs].astype(F32)
        eg = jnp.exp(gc)
        rhs = jnp.concatenate([v_ref[rows, hs].astype(F32) * beta, k * beta * eg], axis=1)
        sol = rhs + _dot(j["n"].astype(BF16), rhs.astype(BF16))
        u_ref[d, rows, hs] = sol[:, :GDN_DV].astype(BF16)
        w_ref[d, rows, hs] = sol[:, GDN_DV:].astype(BF16)
        qg_ref[d, rows, hs] = (q * eg).astype(BF16)
        kd_ref[d, rows, hs] = (k * jnp.exp(gl - gc)).astype(BF16)
        egl_ref[d, j["c"], :, hs] = jnp.broadcast_to(jnp.exp(gl), (8, 128))


def gdn_prep(p, sm, gT):
    R = p.shape[0]
    T, C = GDN_ROWS, GDN_CHUNK
    nc = T // C
    col = lambda k: pl.BlockSpec((T, 512), lambda i: (i, C_QKV // 512 + k))
    dirrow = pl.BlockSpec((2, T, 512), lambda i: (0, i, 0))
    return pl.pallas_call(
        _gdn_prep_kernel,
        grid=(R // T,),
        in_specs=[col(0), col(1), col(2),
                  pl.BlockSpec((T, 128), lambda i: (i, 0)),
                  pl.BlockSpec((nc, 16, C), lambda i: (i, 0, 0))],
        out_specs=[dirrow, dirrow, dirrow, dirrow,
                   pl.BlockSpec((2, nc, GDN_HEADS, C, C), lambda i: (0, i, 0, 0, 0)),
                   pl.BlockSpec((2, nc, 8, 512), lambda i: (0, i, 0, 0))],
        out_shape=[jax.ShapeDtypeStruct((2, R, 512), BF16),
                   jax.ShapeDtypeStruct((2, R, 512), BF16),
                   jax.ShapeDtypeStruct((2, R, 512), BF16),
                   jax.ShapeDtypeStruct((2, R, 512), BF16),
                   jax.ShapeDtypeStruct((2, R // C, GDN_HEADS, C, C), BF16),
                   jax.ShapeDtypeStruct((2, R // C, 8, 512), F32)],
        compiler_params=_cp("arbitrary"),
        name="gdn_prep",
    )(p, p, p, sm, gT)


def _gdn_scan_kernel(uf, wf, qgf, kdf, qkf, eglf, ub, wb, qgb, kdb, qkb, eglb, of_ref, ob_ref, s_ref):
    C = GDN_CHUNK
    nch = GDN_ROWS // C

    @pl.when(pl.program_id(1) == 0)
    def _():
        s_ref[...] = jnp.zeros_like(s_ref)

    dirs = ((uf, wf, qgf, kdf, qkf, eglf, of_ref), (ub, wb, qgb, kdb, qkb, eglb, ob_ref))
    chains = [(d, h) for d in range(2) for h in range(GDN_HEADS)]
    S = {ch: s_ref[ch[0], ch[1]] for ch in chains}
    for i in range(nch):
        Sb, vnb, rows_of, c_of = {}, {}, {}, {}
        for d, h in chains:
            c_of[d] = i if d == 0 else nch - 1 - i
            rows_of[d] = slice(c_of[d] * C, (c_of[d] + 1) * C)
        for d, h in chains:
            hs = slice(h * 128, (h + 1) * 128)
            Sb[d, h] = S[d, h].astype(BF16)
            v_new = dirs[d][0][0, rows_of[d], hs].astype(F32) - _dot(dirs[d][1][0, rows_of[d], hs], Sb[d, h])
            vnb[d, h] = v_new.astype(BF16)
        for d, h in chains:
            hs = slice(h * 128, (h + 1) * 128)
            u_ref, w_ref, qg_ref, kd_ref, qk_ref, egl_ref, o_ref = dirs[d]
            S[d, h] = S[d, h] * egl_ref[0, c_of[d], 0:1, hs] + _dot_tn(kd_ref[0, rows_of[d], hs], vnb[d, h])
        for d, h in chains:
            hs = slice(h * 128, (h + 1) * 128)
            u_ref, w_ref, qg_ref, kd_ref, qk_ref, egl_ref, o_ref = dirs[d]
            o_ref[rows_of[d], hs] = (_dot(qg_ref[0, rows_of[d], hs], Sb[d, h])
                                     + _dot(qk_ref[0, c_of[d], h], vnb[d, h])).astype(BF16)
    for ch in chains:
        s_ref[ch[0], ch[1]] = S[ch]


def gdn_scan(rw, u, w, qg, kd, qk, egl):
    T, C = GDN_ROWS, GDN_CHUNK
    nc = T // C
    R = u.shape[1]
    nbc, nbl, base = rw.Lc // T, rw.Ll // T, rw.NC // T

    def blk(d):
        def f(b, s):
            jc = s if d == 0 else nbc - 1 - s
            jl = (s - nbc) if d == 0 else nbl - 1 - (s - nbc)
            return jnp.where(s < nbc, b * nbc + jc, base + b * nbl + jl)
        return f

    in_specs = []
    for d in range(2):
        f = blk(d)
        rowspec = pl.BlockSpec((1, T, 512), lambda b, s, f=f, d=d: (d, f(b, s), 0))
        in_specs += [rowspec, rowspec, rowspec, rowspec,
                     pl.BlockSpec((1, nc, GDN_HEADS, C, C), lambda b, s, f=f, d=d: (d, f(b, s), 0, 0, 0)),
                     pl.BlockSpec((1, nc, 8, 512), lambda b, s, f=f, d=d: (d, f(b, s), 0, 0))]
    out_specs = [pl.BlockSpec((T, 512), lambda b, s, f=blk(d): (f(b, s), 0)) for d in range(2)]
    ops = (u, w, qg, kd, qk, egl)
    return pl.pallas_call(
        _gdn_scan_kernel,
        grid=(rw.B, nbc + nbl),
        in_specs=in_specs,
        out_specs=out_specs,
        out_shape=[jax.ShapeDtypeStruct((R, 512), BF16)] * 2,
        scratch_shapes=[pltpu.VMEM((2, GDN_HEADS, GDN_DK, GDN_DV), F32)],
        compiler_params=_cp("arbitrary", "arbitrary"),
        name="gdn_scan",
    )(*ops, *ops)


def _merge_kernel(yh_ref, sf_ref, sb_ref, sx_ref, sz_ref, dx_ref, snw_ref, gf_ref, gb_ref, gg_ref, gnw_ref,
                  g0_ref, g1_ref, g2_ref, w0_ref, w1_ref, w2_ref, wo_ref, xc_ref, xl_ref, mod_ref, o_ref,
                  ys_ref, yg_ref, *, nctx_blk):
    tm = xc_ref.shape[0]
    rp = 64
    for r in range(tm // rp):
        rs = slice(r * rp, (r + 1) * rp)
        y = (sf_ref[rs, :].astype(F32) + sb_ref[rs, :].astype(F32)
             + sx_ref[rs, :].astype(F32) * dx_ref[...])
        y = y * _silu(sz_ref[rs, :].astype(F32))
        parts = []
        for g in range(SSM_GROUPS):
            yg = y[:, g * SSM_GW:(g + 1) * SSM_GW]
            parts.append(yg * lax.rsqrt(jnp.mean(yg * yg, axis=-1, keepdims=True) + EPS))
        ys_ref[rs, :] = (jnp.concatenate(parts, axis=1) * snw_ref[...]).astype(BF16)
        o = gf_ref[rs, :].astype(F32) + gb_ref[rs, :].astype(F32)
        parts = []
        for h in range(GDN_HEADS):
            oh = o[:, h * 128:(h + 1) * 128]
            parts.append(oh * lax.rsqrt(jnp.mean(oh * oh, axis=-1, keepdims=True) + EPS))
        yg_ref[rs, :] = (jnp.concatenate(parts, axis=1) * gnw_ref[...]
                         * _silu(gg_ref[rs, :].astype(F32))).astype(BF16)
    m = (_sigmoid(g0_ref[...].astype(F32)) * _dot(yh_ref[...], w0_ref[...])
         + _sigmoid(g1_ref[...].astype(F32)) * _dot(ys_ref[...], w1_ref[...])
         + _sigmoid(g2_ref[...].astype(F32)) * _dot(yg_ref[...], w2_ref[...]))
    x = jnp.where(pl.program_id(0) < nctx_blk, xc_ref[...], xl_ref[...])
    o_ref[...] = x + mod_ref[0, 2:3, :] * _dot(m.astype(BF16), wo_ref[...])


def merge(rw, l, yh, y_f, y_b, dx, ssm_nw, o_f, o_b, gdn_nw, p, w0, w1, w2, wo, xs, mod):
    R = rw.R
    D = xs[0].shape[1]
    tm = min(rw.tm, 512)
    mi = rw.mod_index(tm)
    yspec = pl.BlockSpec((tm, 512), lambda i: (i, 0))
    pspec = lambda col: pl.BlockSpec((tm, 512), lambda i: (i, col // 512))
    vec = pl.BlockSpec((1, 512), lambda i: (0, 0))
    gspec = lambda k: pl.BlockSpec((tm, D), lambda i: (i, C_GATE // D + k))
    wspec = pl.BlockSpec((None, 512, D), lambda i: (l, 0, 0))
    return pl.pallas_call(
        functools.partial(_merge_kernel, nctx_blk=rw.NC // tm),
        grid=(R // tm,),
        in_specs=[yspec,
                  yspec, yspec, pspec(C_XBC), pspec(C_Z), vec, vec,
                  yspec, yspec, pspec(C_GG), vec,
                  gspec(0), gspec(1), gspec(2), wspec, wspec, wspec,
                  pl.BlockSpec((None, D, D), lambda i: (l, 0, 0))]
                 + _stream_specs(rw, tm, xs, 1)
                 + [pl.BlockSpec((None, 1, 8, D), lambda i: (l, mi(i), 0, 0))],
        out_specs=pl.BlockSpec((tm, D), lambda i: (i, 0)),
        out_shape=jax.ShapeDtypeStruct((R, D), F32),
        scratch_shapes=[pltpu.VMEM((tm, 512), BF16), pltpu.VMEM((tm, 512), BF16)],
        compiler_params=_cp("arbitrary"),
        name="merge",
    )(yh, y_f, y_b, p, p, dx, ssm_nw, o_f, o_b, p, gdn_nw, p, p, p, w0, w1, w2, wo, xs[0], xs[1], mod)


def _swiglu_up_kernel(x_ref, nw_ref, mod_ref, wg_ref, wu_ref, o_ref, h_ref, g0_ref, g1_ref, u0_ref, u1_ref):
    @pl.when(pl.program_id(1) == 0)
    def _():
        h = _norm_mod(x_ref[...], nw_ref[...], mod_ref[0, 4:5, :], mod_ref[0, 3:4, :])
        h_ref[...] = h.astype(BF16)

    T, tn = o_ref.shape
    rows = g0_ref.shape[0]
    gs, us = (g0_ref, g1_ref), (u0_ref, u1_ref)

    def project(r):
        hh = h_ref[r * rows:(r + 1) * rows, :]
        gs[r % 2][...] = _dot(hh, wg_ref[...])
        us[r % 2][...] = _dot(hh, wu_ref[...])

    def finish(r):
        for q in range(rows // 64):
            for c in range(tn // 128):
                ps = (slice(q * 64, (q + 1) * 64), slice(c * 128, (c + 1) * 128))
                y = _silu(gs[r % 2][ps]) * us[r % 2][ps]
                o_ref[r * rows + q * 64:r * rows + (q + 1) * 64, ps[1]] = y.astype(o_ref.dtype)

    for r in range(T // rows):
        project(r)
        if r > 0:
            finish(r - 1)
    finish(T // rows - 1)


def swiglu_up(rw, l, x, nw, mod, wgu):
    R, D = x.shape
    tm = rw.tm
    tn = D_FF // 2
    nj = D_FF // tn
    mi = rw.mod_index(tm)
    return pl.pallas_call(
        _swiglu_up_kernel,
        grid=(R // tm, nj),
        in_specs=[
            pl.BlockSpec((tm, D), lambda i, j: (i, 0)),
            pl.BlockSpec((None, 1, D), lambda i, j: (l, 0, 0)),
            pl.BlockSpec((None, 1, 8, D), lambda i, j: (l, mi(i), 0, 0)),
            pl.BlockSpec((None, D, tn), lambda i, j: (l, 0, j)),
            pl.BlockSpec((None, D, tn), lambda i, j: (l, 0, nj + j)),
        ],
        out_specs=pl.BlockSpec((tm, tn), lambda i, j: (i, j)),
        out_shape=jax.ShapeDtypeStruct((R, D_FF), BF16),
        scratch_shapes=[pltpu.VMEM((tm, D), BF16)] + [pltpu.VMEM((min(256, tm), tn), F32)] * 4,
        compiler_params=_cp("arbitrary", "arbitrary"),
        name="swiglu_up",
    )(x, nw, mod, wgu, wgu)


def _swiglu_down_kernel(a_ref, w_ref, x_ref, mod_ref, o_ref):
    o_ref[...] = x_ref[...] + mod_ref[0, 5:6, :] * _dot(a_ref[...], w_ref[...])


def swiglu_down(rw, l, a, w, x, mod):
    R, D = x.shape
    tm = min(rw.tm, 512)
    mi = rw.mod_index(tm)
    return pl.pallas_call(
        _swiglu_down_kernel,
        grid=(R // tm,),
        in_specs=[
            pl.BlockSpec((tm, D_FF), lambda i: (i, 0)),
            pl.BlockSpec((None, D_FF, D), lambda i: (l, 0, 0)),
            pl.BlockSpec((tm, D), lambda i: (i, 0)),
            pl.BlockSpec((None, 1, 8, D), lambda i: (l, mi(i), 0, 0)),
        ],
        out_specs=pl.BlockSpec((tm, D), lambda i: (i, 0)),
        out_shape=jax.ShapeDtypeStruct((R, D), F32),
        compiler_params=_cp("arbitrary"),
        name="swiglu_down",
    )(a, w, x, mod)


def _final_norm_kernel(x_ref, w_ref, o_ref):
    x = x_ref[...]
    ms = jnp.mean(x * x, axis=-1, keepdims=True)
    o_ref[...] = x * lax.rsqrt(ms + EPS) * w_ref[...]


def final_norm(rw, x, w):
    D = x.shape[1]
    tm = rw.tm
    n0 = rw.NC // tm
    nl = rw.B * rw.Ll
    return pl.pallas_call(
        _final_norm_kernel,
        grid=(nl // tm,),
        in_specs=[pl.BlockSpec((tm, D), lambda i: (n0 + i, 0)), pl.BlockSpec((1, D), lambda i: (0, 0))],
        out_specs=pl.BlockSpec((tm, D), lambda i: (i, 0)),
        out_shape=jax.ShapeDtypeStruct((nl, D), F32),
        compiler_params=_cp("arbitrary"),
        name="final_norm",
    )(x, w.reshape(1, D))


def _regroup_w_in(w_in):
    o_dt = 3072
    o_gdn = 3088
    o_a = o_gdn + 2048
    o_b = o_a + 8
    o_gate = o_gdn + 2064
    w_in = w_in.astype(BF16)
    pieces = [
        w_in[..., 0:3072],
        w_in[..., o_gdn:o_gdn + 2048],
        w_in[..., o_gate:o_gate + 3072],
        w_in[..., o_dt:o_dt + 16],
        w_in[..., o_a:o_a + 4], w_in[..., o_b:o_b + 4],
        w_in[..., o_a + 4:o_a + 8], w_in[..., o_b + 4:o_b + 8],
        jnp.zeros(w_in.shape[:-1] + (N_IN_PAD - C_SM - 32,), w_in.dtype),
    ]
    return jnp.concatenate(pieces, axis=-1)


def kernel(x, c, ctx, c_ctx, w_ada, b_ada, norm1_w, norm2_w, w_in, hy_conv_w, hy_conv_b, hy_w1, hy_b1, hy_w2, hy_b2, hy_w3, hy_freq, hy_bias, ssm_conv_w, ssm_conv_b, ssm_dt_bias, ssm_A_log, ssm_D, ssm_norm_w, gdn_conv_w, gdn_dt_bias, gdn_A_log, gdn_norm_w, w_hy_out, w_ssm_out, w_gdn_out, w_out, w_gate_up, w_down, final_norm_w):
    B, Ll, D = x.shape
    Lc = ctx.shape[1]
    depth = w_ada.shape[0]
    assert Lc == CONV_ROWS and D == D_MODEL and B <= 15
    rw = Rows(B, Lc, Ll)
    R, NC = rw.R, rw.NC

    xs = (ctx.reshape(B * Lc, D), x.reshape(B * Ll, D))

    svec = jnp.concatenate([c_ctx[None, :], c, jnp.zeros((15 - B, D), F32)], axis=0)
    mod = ada_modulation(svec, w_ada, b_ada)
    mod = jnp.pad(mod.reshape(depth, 16, 6, D), ((0, 0), (0, 0), (0, 2), (0, 0)))

    w_in_r = _regroup_w_in(w_in)
    par = _in_proj_params(hy_conv_w, hy_conv_b, ssm_conv_w, ssm_conv_b, gdn_conv_w, ssm_dt_bias, gdn_dt_bias,
                          gdn_A_log)
    norm1 = norm1_w.reshape(depth, 1, D)
    norm2 = norm2_w.reshape(depth, 1, D)
    w_hy_o, w_ssm_o, w_gdn_o, w_o = (w.astype(BF16) for w in (w_hy_out, w_ssm_out, w_gdn_out, w_out))
    w_gu, w_dn = w_gate_up.astype(BF16), w_down.astype(BF16)
    fwd_l, inv_l = dft_tables(Ll)
    fwd_c, inv_c = dft_tables(Lc)
    feat_l, feat_c = hy_features(Ll), hy_features(Lc)

    for l in range(depth):
        p, sm = in_proj(rw, xs, l, norm1, mod, w_in_r, par)

        sm32_t = sm[:, :32].T
        dt_t = sm32_t[:16].reshape(2, 8, R)
        g_t = sm32_t[16:32].reshape(16, R // GDN_CHUNK, GDN_CHUNK).transpose(1, 0, 2)

        alx = jnp.repeat(ssm_A_log[l], SSM_HEAD_DIM, axis=-1).reshape(2, 1, 512)
        alc = ssm_A_log[l].reshape(2, 8, 1)
        y_f, y_b = ssd_scan(rw, p, sm, dt_t, alx, alc)
        dx = jnp.repeat(ssm_D[l], SSM_HEAD_DIM).reshape(1, 512)

        o_f, o_b = gdn_scan(rw, *gdn_prep(p, sm, g_t))

        hyu = p
        parts = []
        for (Bn, L, blk0, fwd, inv, feat) in ((B, Lc, 0, fwd_c, inv_c, feat_c),
                                              (B, Ll, NC // Ll, fwd_l, inv_l, feat_l)):
            if NC % L:
                raise ValueError("latent length must divide the context row count")
            filt = hy_filter(feat, hy_w1[l], hy_b1[l], hy_w2[l], hy_b2[l], hy_w3[l], hy_freq[l])
            kspec = matmul(fwd, filt, min(512, 2 * L), 512)
            z1 = long_conv(Bn, L, hyu, blk0, 0, hyu, blk0, 1, hy_bias[l, 0], fwd, inv, kspec, 0, F32)
            yy = long_conv(Bn, L, z1, 0, 0, hyu, blk0, 2, hy_bias[l, 1], fwd, inv, kspec, 1, BF16)
            parts.append(yy)
        y_hy = jnp.concatenate(parts, axis=0)

        xa = merge(rw, l, y_hy, y_f, y_b, dx, ssm_norm_w[l].reshape(1, 512),
                   o_f, o_b, jnp.tile(gdn_norm_w[l], GDN_HEADS).reshape(1, 512), p,
                   w_hy_o, w_ssm_o, w_gdn_o, w_o, xs, mod)
        act = swiglu_up(rw, l, xa, norm2, mod, w_gu)
        xa = swiglu_down(rw, l, act, w_dn, xa, mod)
        xs = (xa, xa)

    out = final_norm(rw, xa, final_norm_w)
    return out.reshape(B, Ll, D)
```

```python
import functools
import math

import jax
import jax.numpy as jnp
import numpy as np
from jax import lax
from jax.experimental import pallas as pl
from jax.experimental.pallas import tpu as pltpu

F32 = jnp.float32
BF16 = jnp.bfloat16
HI = lax.Precision.HIGHEST

EPS = 1e-6
D_MODEL = 1024
GRID_W = 64

HY_WIDTH = 512
HY_BANDS = 16
HY_EMB = 1 + 2 * HY_BANDS
HY_HIDDEN = 64
HY_SHORT_DECAY_PCT = 0.3
HY_LONG_DECAY_PCT = 1.5
HY_TARGET = 1e-2

SSM_HEADS = 8
SSM_HEAD_DIM = 64
SSM_WIDTH = 512
SSM_GROUPS = 2
SSM_HPG = 4
SSM_STATE = 128
SSM_CHUNK = 128
SSM_GW = SSM_HPG * SSM_HEAD_DIM

GDN_HEADS = 4
GDN_DK = 128
GDN_DV = 128
GDN_CHUNK = 64

D_FF = 2816

C_HY = 0
C_Z = 1536
C_XBC = 2048
C_QKV = 3072
C_GG = 4608
C_GATE = 5120
C_SM = 8192

CONV_ROWS = 256
FREQ_BLK = 512

VMEM_LIMIT = 56 * 1024 * 1024


def _cp(*sem, flags=None):
    return pltpu.CompilerParams(dimension_semantics=sem, vmem_limit_bytes=VMEM_LIMIT, flags=flags)


def _sigmoid(x):
    return 1.0 / (1.0 + jnp.exp(-x))


def _silu(x):
    return x * _sigmoid(x)


def _softplus(x):
    return jnp.maximum(x, 0.0) + jnp.log1p(jnp.exp(-jnp.abs(x)))


def _dot(a, b, precision=None):
    return jnp.dot(a, b, precision=precision, preferred_element_type=F32)


def _dot_nt(a, b):
    return lax.dot_general(a, b, (((1,), (1,)), ((), ())), preferred_element_type=F32)


def _dot_tn(a, b):
    return lax.dot_general(a, b, (((0,), (0,)), ((), ())), preferred_element_type=F32)


def _ada_kernel(s_ref, w_ref, b_ref, o_ref):
    s = _silu(s_ref[...])
    o_ref[0] = _dot(s, w_ref[0], HI) + b_ref[0]


def ada_modulation(svec, w_ada, b_ada):
    depth = w_ada.shape[0]
    D = D_MODEL
    return pl.pallas_call(
        _ada_kernel,
        grid=(depth, 6),
        in_specs=[
            pl.BlockSpec((16, D), lambda l, j: (0, 0)),
            pl.BlockSpec((1, D, D), lambda l, j: (l, 0, j)),
            pl.BlockSpec((1, 1, D), lambda l, j: (l, 0, j)),
        ],
        out_specs=pl.BlockSpec((1, 16, D), lambda l, j: (l, 0, j)),
        out_shape=jax.ShapeDtypeStruct((depth, 16, 6 * D), F32),
        compiler_params=_cp("arbitrary", "arbitrary"),
        name="ada",
    )(svec, w_ada, b_ada.reshape(depth, 1, 6 * D))


def _norm_mod(x, nw, scale, shift):
    ms = jnp.mean(x * x, axis=-1, keepdims=True)
    return (x * lax.rsqrt(ms + EPS) * nw) * (1.0 + scale) + shift


N_IN_PAD = C_SM + 128
IN_TN = N_IN_PAD // 5
MODE_RAW, MODE_CONV, MODE_CONV_SILU, MODE_CONV_SILU_L2, MODE_SMALL = range(5)


def _tile_mode(tile):
    col = tile * 128
    if col < C_Z:
        return MODE_CONV
    if col < C_XBC:
        return MODE_RAW
    if col < C_QKV:
        return MODE_CONV_SILU
    if col < C_QKV + 1024:
        return MODE_CONV_SILU_L2
    if col < C_GG:
        return MODE_CONV_SILU
    if col < C_SM:
        return MODE_RAW
    return MODE_SMALL
PAR_W0, PAR_W1, PAR_W2, PAR_BIAS, PAR_L2SCALE, PAR_SBIAS, PAR_SALOG, PAR_SKIND = range(8)


def _in_proj_kernel(xc_ref, xl_ref, nw_ref, mod_ref, w_ref, par_ref, o_ref, sm_ref, h_ref, raw0_ref, raw1_ref, *,
                    nctx_blk):
    j = pl.program_id(1)
    nj = N_IN_PAD // IN_TN
    raws = (raw0_ref, raw1_ref)

    @pl.when((j == 0) & (pl.program_id(0) < nctx_blk))
    def _():
        h = _norm_mod(xc_ref[...], nw_ref[...], mod_ref[0, 1:2, :], mod_ref[0, 0:1, :])
        h_ref[...] = h.astype(BF16)

    @pl.when((j == 0) & (pl.program_id(0) >= nctx_blk))
    def _():
        h = _norm_mod(xl_ref[...], nw_ref[...], mod_ref[0, 1:2, :], mod_ref[0, 0:1, :])
        h_ref[...] = h.astype(BF16)

    T = h_ref.shape[0]
    G = GRID_W
    per_ctx = CONV_ROWS // G
    is_latent = pl.program_id(0) >= nctx_blk
    sub = lax.broadcasted_iota(jnp.int32, (8, 128), 0)

    def raw_piece(src, g, c):
        return src[g * G:(g + 1) * G, c * 128:(c + 1) * 128]

    def conv(src, g, c):
        cs = slice(c * 128, (c + 1) * 128)
        x = raw_piece(src, g, c)
        zero = jnp.zeros((1, 128), F32)
        before = zero if g % per_ctx == 0 else jnp.where(is_latent, 0.0, src[g * G - 1:g * G, cs])
        after = zero if g % per_ctx == per_ctx - 1 else jnp.where(is_latent, 0.0, src[(g + 1) * G:(g + 1) * G + 1, cs])
        rp = pltpu.roll(x, 1, 0)
        rn = pltpu.roll(x, G - 1, 0)
        prev = jnp.concatenate([jnp.where(sub == 0, before, rp[0:8]), rp[8:]], axis=0)
        nxt = jnp.concatenate([rn[:G - 8], jnp.where(sub == 7, after, rn[G - 8:])], axis=0)
        return (prev * par_ref[PAR_W0:PAR_W0 + 1, cs] + x * par_ref[PAR_W1:PAR_W1 + 1, cs]
                + nxt * par_ref[PAR_W2:PAR_W2 + 1, cs] + par_ref[PAR_BIAS:PAR_BIAS + 1, cs])

    def conv_silu(src, g, c):
        return _silu(conv(src, g, c))

    def conv_silu_l2(src, g, c):
        y = _silu(conv(src, g, c))
        y = y * lax.rsqrt(jnp.sum(y * y, axis=-1, keepdims=True) + EPS)
        return y * par_ref[PAR_L2SCALE:PAR_L2SCALE + 1, c * 128:(c + 1) * 128]

    def small(src, g, c):
        cs = slice(c * 128, (c + 1) * 128)
        acc = raw_piece(src, g, c)
        kind = par_ref[PAR_SKIND:PAR_SKIND + 1, cs]
        sp = _softplus(acc + par_ref[PAR_SBIAS:PAR_SBIAS + 1, cs])
        dec = -jnp.exp(par_ref[PAR_SALOG:PAR_SALOG + 1, cs]) * sp
        return jnp.where(kind == 0.0, sp, jnp.where(kind == 1.0, dec, jnp.where(kind == 2.0, _sigmoid(acc), 0.0)))

    rows_mm = 256
    tiles = IN_TN // 128
    piece_fn = {MODE_RAW: raw_piece, MODE_CONV: conv, MODE_CONV_SILU: conv_silu,
                MODE_CONV_SILU_L2: conv_silu_l2, MODE_SMALL: small}

    def project(dst, r):
        rs = slice(r * rows_mm, (r + 1) * rows_mm)
        dst[rs, :] = _dot(h_ref[rs, :], w_ref[...])

    def finish(src, blk, g, c):
        mode = _tile_mode(blk * tiles + c)
        y = piece_fn[mode](src, g, c)
        if mode == MODE_SMALL:
            sm_ref[g * G:(g + 1) * G, :] = y
            y = jnp.zeros_like(y)
        o_ref[g * G:(g + 1) * G, c * 128:(c + 1) * 128] = y.astype(o_ref.dtype)

    for step in range(nj + 1):
        @pl.when(j == step)
        def _(step=step):
            blk = step - 1
            src, dst = raws[blk % 2], raws[step % 2]
            for r in range(T // rows_mm):
                if step < nj:
                    project(dst, r)
                if blk < 0:
                    continue
                for g in range(r * rows_mm // G, (r + 1) * rows_mm // G):
                    for c in range(tiles):
                        finish(src, blk, g, c)


class Rows:
    def __init__(self, B, Lc, Ll):
        self.B, self.Lc, self.Ll = B, Lc, Ll
        self.NC = B * Lc
        self.R = B * Lc + B * Ll
        assert self.NC % Ll == 0 or Ll % self.NC == 0
        tm = 1024
        while self.NC % tm or Ll % tm:
            tm //= 2
        self.tm = tm

    def mod_index(self, tm):
        nctx = self.NC // tm
        per = self.Ll // tm
        return lambda i: jnp.where(i < nctx, 0, 1 + (i - nctx) // per)


def _stream_specs(rw, tm, xs, ngrid):
    xc, xl = xs
    nctx = rw.NC // tm
    off = nctx if xl.shape[0] == rw.R else 0
    D = xc.shape[1]
    if ngrid == 1:
        return [pl.BlockSpec((tm, D), lambda i: (jnp.minimum(i, nctx - 1), 0)),
                pl.BlockSpec((tm, D), lambda i: (jnp.maximum(i - nctx, 0) + off, 0))]
    return [pl.BlockSpec((tm, D), lambda i, j: (jnp.minimum(i, nctx - 1), 0)),
            pl.BlockSpec((tm, D), lambda i, j: (jnp.maximum(i - nctx, 0) + off, 0))]


def in_proj(rw, xs, l, nw, mod, w, par):
    R = rw.R
    D = xs[0].shape[1]
    N = w.shape[2]
    tm, tn = rw.tm, IN_TN
    nj = N // tn
    assert N == N_IN_PAD
    mi = rw.mod_index(tm)
    done = lambda j: jnp.maximum(j - 1, 0)
    return pl.pallas_call(
        functools.partial(_in_proj_kernel, nctx_blk=rw.NC // tm),
        grid=(R // tm, nj + 1),
        in_specs=_stream_specs(rw, tm, xs, 2) + [
            pl.BlockSpec((None, 1, D), lambda i, j: (l, 0, 0)),
            pl.BlockSpec((None, 1, 8, D), lambda i, j: (l, mi(i), 0, 0)),
            pl.BlockSpec((None, D, tn), lambda i, j: (l, 0, jnp.minimum(j, nj - 1))),
            pl.BlockSpec((None, 8, tn), lambda i, j: (l, 0, done(j))),
        ],
        out_specs=[pl.BlockSpec((tm, tn), lambda i, j: (i, done(j))),
                   pl.BlockSpec((tm, 128), lambda i, j: (i, 0))],
        out_shape=[jax.ShapeDtypeStruct((R, N), BF16), jax.ShapeDtypeStruct((R, 128), F32)],
        scratch_shapes=[pltpu.VMEM((tm, D), BF16), pltpu.VMEM((tm, tn), F32), pltpu.VMEM((tm, tn), F32)],
        compiler_params=_cp("arbitrary", "arbitrary"),
        name="in_proj",
    )(xs[0], xs[1], nw, mod, w, par)


def _in_proj_params(hy_conv_w, hy_conv_b, ssm_conv_w, ssm_conv_b, gdn_conv_w, ssm_dt_bias, gdn_dt_bias, gdn_A_log):
    depth = hy_conv_w.shape[0]

    def row(pieces):
        out, pos = [], 0
        for off, a in pieces:
            out += [jnp.zeros((depth, off - pos), F32), a.astype(F32)]
            pos = off + a.shape[1]
        return jnp.concatenate(out + [jnp.zeros((depth, N_IN_PAD - pos), F32)], axis=1)
    z4 = jnp.zeros((depth, 4), F32)
    conv = [row([(C_HY, hy_conv_w[:, t]), (C_XBC, ssm_conv_w[:, t]), (C_QKV, gdn_conv_w[:, t])]) for t in range(3)]
    bias = row([(C_HY, hy_conv_b), (C_XBC, ssm_conv_b)])
    l2s = row([(C_QKV, jnp.full((depth, 512), GDN_DK ** -0.5, F32)), (C_QKV + 512, jnp.ones((depth, 512), F32))])
    sbias = row([(C_SM, jnp.concatenate([ssm_dt_bias.reshape(depth, 16), gdn_dt_bias[:, 0], z4,
                                         gdn_dt_bias[:, 1], z4], axis=1))])
    salog = row([(C_SM + 16, jnp.concatenate([gdn_A_log[:, 0], z4, gdn_A_log[:, 1], z4], axis=1))])
    kind = np.full((depth, N_IN_PAD), 3.0, np.float32)
    kind[:, C_SM:C_SM + 16] = 0.0
    kind[:, C_SM + 16:C_SM + 20] = 1.0
    kind[:, C_SM + 24:C_SM + 28] = 1.0
    kind[:, C_SM + 20:C_SM + 24] = 2.0
    kind[:, C_SM + 28:C_SM + 32] = 2.0
    return jnp.stack(conv + [bias, l2s, sbias, salog, jnp.asarray(kind)], axis=1)


def _hy_filter_kernel(z_ref, w1_ref, b1_ref, w2_ref, b2_ref, w3_ref, f0_ref, f1_ref, win_ref, o_ref, h_ref):
    @pl.when(pl.program_id(1) == 0)
    def _():
        h1 = jnp.sin(f0_ref[...] * (_dot(z_ref[...], w1_ref[...], HI) + b1_ref[...]))
        h_ref[...] = jnp.sin(f1_ref[...] * (_dot(h1, w2_ref[...], HI) + b2_ref[...]))

    h = _dot(h_ref[...], w3_ref[...], HI) * win_ref[...]
    tl = h.shape[0]
    row = lax.broadcasted_iota(jnp.int32, (tl, 1), 0) + pl.program_id(0) * tl
    drop = (row == 0) & (pl.program_id(1) % 2 == 1)
    o_ref[...] = jnp.where(drop, 0.0, h).astype(o_ref.dtype)


def hy_features(L):
    t = jnp.linspace(0.0, 1.0, L, dtype=F32)[:, None]
    w = 2.0 * math.pi * jnp.arange(L, dtype=F32)[:, None] / L
    f = jnp.linspace(1e-4, HY_BANDS - 1, HY_BANDS, dtype=F32)[None, :]
    z = jnp.concatenate([t, jnp.cos(f * w), -jnp.sin(f * w)], axis=-1)
    z = jnp.pad(z, ((0, 0), (0, 128 - HY_EMB)))
    min_decay = math.log(HY_TARGET) / HY_LONG_DECAY_PCT
    max_decay = math.log(HY_TARGET) / HY_SHORT_DECAY_PCT
    deltas = jnp.linspace(min_decay, max_decay, HY_WIDTH, dtype=F32)
    window = jnp.exp(-t * jnp.abs(deltas))
    return z, window


def hy_filter(feat, w1, b1, w2, b2, w3, freq):
    z, window = feat
    L = z.shape[0]
    H = HY_HIDDEN
    w1p = jnp.pad(w1, ((0, 128 - HY_EMB), (0, 128 - H)))
    w2p = jnp.pad(w2, ((0, 128 - H), (0, 128 - H)))
    w3p = jnp.pad(w3, ((0, 128 - H), (0, 0)))
    pad1 = lambda v: jnp.pad(v, (0, 128 - H)).reshape(1, 128)
    tl = 256
    full = lambda shape: pl.BlockSpec(shape, lambda i, j: (0, 0))
    return pl.pallas_call(
        _hy_filter_kernel,
        grid=(L // tl, 4),
        in_specs=[
            pl.BlockSpec((tl, 128), lambda i, j: (i, 0)),
            full((128, 128)), full((1, 128)), full((128, 128)), full((1, 128)),
            pl.BlockSpec((128, HY_WIDTH), lambda i, j: (0, j)),
            full((1, 128)), full((1, 128)),
            pl.BlockSpec((tl, HY_WIDTH), lambda i, j: (i, 0)),
        ],
        out_specs=pl.BlockSpec((tl, HY_WIDTH), lambda i, j: (i, j)),
        out_shape=jax.ShapeDtypeStruct((L, 4 * HY_WIDTH), BF16),
        scratch_shapes=[pltpu.VMEM((tl, 128), F32)],
        compiler_params=_cp("arbitrary", "arbitrary"),
        name="hy_filter",
    )(z, w1p, pad1(b1), w2p, pad1(b2), w3p, pad1(freq[0]), pad1(freq[1]), window)


def dft_tables(L):
    N = 2 * L
    f = np.arange(L, dtype=np.int64)[:, None]
    s = np.arange(L, dtype=np.int64)[None, :]
    ang = ((f * s) % N).astype(np.float64) * (2.0 * math.pi / N)
    c, sn = np.cos(ang), np.sin(ang)
    alt_s = (1 - 2 * (s % 2)).astype(np.float64)
    alt_t = (1 - 2 * (f % 2)).astype(np.float64)
    fwd = np.concatenate([c, np.where(f == 0, alt_s, -sn)], axis=0)
    wgt = np.where(s == 0, 1.0, 2.0) / N
    inv = np.concatenate([c * wgt, np.where(s == 0, alt_t / N, -sn * wgt)], axis=1)
    return jnp.asarray(fwd, dtype=BF16), jnp.asarray(inv, dtype=BF16)


def _matmul_kernel(a_ref, b_ref, o_ref):
    o_ref[...] = _dot(a_ref[...], b_ref[...])


def matmul(a, b, tm, tn):
    M, K = a.shape
    N = b.shape[1]
    return pl.pallas_call(
        _matmul_kernel,
        grid=(M // tm, N // tn),
        in_specs=[pl.BlockSpec((tm, K), lambda i, j: (i, 0)), pl.BlockSpec((K, tn), lambda i, j: (0, j))],
        out_specs=pl.BlockSpec((tm, tn), lambda i, j: (i, j)),
        out_shape=jax.ShapeDtypeStruct((M, N), F32),
        compiler_params=_cp("arbitrary", "arbitrary"),
        name="matmul",
    )(a, b)


def _long_conv_kernel(u_ref, g_ref, bias_ref, fr_ref, fi_ref, ic_ref, is_ref,
                      ar0_ref, ar1_ref, ai0_ref, ai1_ref, o_ref, ub_ref, acc_ref):
    f = pl.program_id(1)

    @pl.when(f == 0)
    def _():
        ub_ref[...] = u_ref[...].astype(BF16)
        acc_ref[...] = jnp.zeros_like(acc_ref)

    ub = ub_ref[...]
    ur = _dot(fr_ref[...], ub)
    ui = _dot(fi_ref[...], ub)
    kr = ar0_ref[...] + ar1_ref[...]
    nyq = (lax.broadcasted_iota(jnp.int32, (fr_ref.shape[0], 1), 0) == 0) & (f == 0)
    ki = jnp.where(nyq, ai0_ref[...] + ai1_ref[...], ai0_ref[...] - ai1_ref[...])
    pr = jnp.where(nyq, ur * kr, ur * kr - ui * ki)
    pi = jnp.where(nyq, ui * ki, ur * ki + ui * kr)
    acc_ref[...] += _dot(ic_ref[...], pr.astype(BF16)) + _dot(is_ref[...], pi.astype(BF16))

    @pl.when(f == pl.num_programs(1) - 1)
    def _():
        u = u_ref[...].astype(F32)
        o_ref[...] = (g_ref[...].astype(F32) * (acc_ref[...] + u * bias_ref[...])).astype(o_ref.dtype)


def long_conv(B, L, u, u_rb0, u_cb, gate, g_rb0, gate_cb, bias, fwd, inv, kspec, order, out_dtype):
    C = HY_WIDTH
    FB = min(FREQ_BLK, L)
    nfb = L // FB
    return pl.pallas_call(
        _long_conv_kernel,
        grid=(B, nfb),
        in_specs=[
            pl.BlockSpec((L, C), lambda b, f: (u_rb0 + b, u_cb)),
            pl.BlockSpec((L, C), lambda b, f: (g_rb0 + b, gate_cb)),
            pl.BlockSpec((1, C), lambda b, f: (0, 0)),
            pl.BlockSpec((FB, L), lambda b, f: (f, 0)),
            pl.BlockSpec((FB, L), lambda b, f: (nfb + f, 0)),
            pl.BlockSpec((L, FB), lambda b, f: (0, f)),
            pl.BlockSpec((L, FB), lambda b, f: (0, nfb + f)),
            pl.BlockSpec((FB, C), lambda b, f: (f, 2 * order)),
            pl.BlockSpec((FB, C), lambda b, f: (f, 2 * order + 1)),
            pl.BlockSpec((FB, C), lambda b, f: (nfb + f, 2 * order)),
            pl.BlockSpec((FB, C), lambda b, f: (nfb + f, 2 * order + 1)),
        ],
        out_specs=pl.BlockSpec((L, C), lambda b, f: (b, 0)),
        out_shape=jax.ShapeDtypeStruct((B * L, C), out_dtype),
        scratch_shapes=[pltpu.VMEM((L, C), BF16), pltpu.VMEM((L, C), F32)],
        compiler_params=_cp("arbitrary", "arbitrary"),
        name="long_conv",
    )(u, gate, bias.reshape(1, C), fwd, fwd, inv, inv, kspec, kspec, kspec, kspec)


def _scan_blocks(rw, rows):
    nbc, nbl, base = rw.Lc // rows, rw.Ll // rows, rw.NC // rows

    def make(d):
        def f(b, s):
            jc = s if d == 0 else nbc - 1 - s
            jl = (s - nbc) if d == 0 else nbl - 1 - (s - nbc)
            return jnp.where(s < nbc, b * nbc + jc, base + b * nbl + jl)
        return f

    return [make(0), make(1)], nbc + nbl


def _expand_lanes(x, base, n, width):
    rows = x.shape[0]
    per = 128 // width
    lane = lax.broadcasted_iota(jnp.int32, (rows, 128), 1)
    tiles = []
    for t in range(n // per):
        c0 = base + t * per
        tile = jnp.broadcast_to(x[:, c0:c0 + 1], (rows, 128))
        for i in range(1, per):
            tile = jnp.where(lane >= i * width, jnp.broadcast_to(x[:, c0 + i:c0 + i + 1], (rows, 128)), tile)
        tiles.append(tile)
    return jnp.concatenate(tiles, axis=1)


def _ssd_kernel(xf, bf, cf, smf, dtf, xb, bb, cb_, smb, dtb, alx_ref, alc_ref, of_ref, ob_ref, h_ref):
    Q = SSM_CHUNK
    GW = SSM_GW

    @pl.when(pl.program_id(1) == 0)
    def _():
        h_ref[...] = jnp.zeros_like(h_ref)

    row = lax.broadcasted_iota(jnp.int32, (Q, Q), 0)
    col = lax.broadcasted_iota(jnp.int32, (Q, Q), 1)
    lane_head = lax.broadcasted_iota(jnp.int32, (Q, GW), 1) // SSM_HEAD_DIM
    dirs = ((xf, bf, cf, smf, dtf, of_ref), (xb, bb, cb_, smb, dtb, ob_ref))
    jobs = []
    for d in range(2):
        x_ref, b_ref, c_ref, sm_ref, dt_ref, o_ref = dirs[d]
        keep = (col <= row) if d == 0 else (col >= row)
        tri = keep.astype(BF16)
        tri_t = ((row <= col) if d == 0 else (row >= col)).astype(BF16)
        sm = sm_ref[...]
        a_x = -jnp.exp(alx_ref[d])
        dtx = _expand_lanes(sm, 8 * d, SSM_HEADS, SSM_HEAD_DIM)
        cumx = _expand_lanes(_dot_01_lhs(tri, sm), 8 * d, SSM_HEADS, SSM_HEAD_DIM) * a_x
        cumr = _dot_01_rhs(dt_ref[0], tri_t) * (-jnp.exp(alc_ref[d]))
        last = Q - 1 if d == 0 else 0
        totx = cumx[last:last + 1, :]
        xd = x_ref[...].astype(F32) * dtx
        xdw = xd * jnp.exp(totx - cumx)
        ecum = jnp.exp(cumx)
        for g in range(SSM_GROUPS):
            gs = slice(g * GW, (g + 1) * GW)
            jobs.append(dict(d=d, g=g, gs=gs, keep=keep, cumx=cumx, cumr=cumr, o_ref=o_ref,
                             bg=b_ref[:, g * SSM_STATE:(g + 1) * SSM_STATE].astype(BF16),
                             cg=c_ref[:, g * SSM_STATE:(g + 1) * SSM_STATE].astype(BF16),
                             xdg=xd[:, gs], xdw=xdw[:, gs].astype(BF16), ecum=ecum[:, gs],
                             etot=jnp.exp(totx[:, gs])))
    for j in jobs:
        j["cb"] = _dot_nt(j["cg"], j["bg"])
        j["h"] = h_ref[j["d"], j["g"]]
    for j in jobs:
        ms, xs = [], []
        for e4 in range(SSM_HPG):
            e = j["g"] * SSM_HPG + e4
            diff = j["cumx"][:, e * SSM_HEAD_DIM:e * SSM_HEAD_DIM + 1] - j["cumr"][e:e + 1, :]
            ms.append((j["cb"] * jnp.where(j["keep"], jnp.exp(diff), 0.0)).astype(BF16))
            xs.append(jnp.where(lane_head == e4, j["xdg"], 0.0).astype(BF16))
        yd = _dot(jnp.concatenate(ms, axis=1), jnp.concatenate(xs, axis=0))
        y_off = _dot(j["cg"], j["h"].astype(BF16)) * j["ecum"]
        j["o_ref"][:, j["gs"]] = (yd + y_off).astype(BF16)
    for j in jobs:
        h_ref[j["d"], j["g"]] = j["h"] * j["etot"] + _dot_tn(j["bg"], j["xdw"])


def ssd_scan(rw, p, sm, dtT, alx, alc):
    Q = SSM_CHUNK
    blks, nsteps = _scan_blocks(rw, Q)
    R = p.shape[0]
    in_specs = []
    for d in range(2):
        f = blks[d]
        in_specs += [
            pl.BlockSpec((Q, 512), lambda b, s, f=f: (f(b, s), C_XBC // 512)),
            pl.BlockSpec((Q, 256), lambda b, s, f=f: (f(b, s), C_XBC // 256 + 2)),
            pl.BlockSpec((Q, 256), lambda b, s, f=f: (f(b, s), C_XBC // 256 + 3)),
            pl.BlockSpec((Q, 128), lambda b, s, f=f: (f(b, s), 0)),
            pl.BlockSpec((1, 8, Q), lambda b, s, f=f, d=d: (d, 0, f(b, s))),
        ]
    in_specs += [pl.BlockSpec((2, 1, 512), lambda b, s: (0, 0, 0)), pl.BlockSpec((2, 8, 1), lambda b, s: (0, 0, 0))]
    ops = (p, p, p, sm, dtT)
    return pl.pallas_call(
        _ssd_kernel,
        grid=(rw.B, nsteps),
        in_specs=in_specs,
        out_specs=[pl.BlockSpec((Q, 512), lambda b, s, f=blks[d]: (f(b, s), 0)) for d in range(2)],
        out_shape=[jax.ShapeDtypeStruct((R, 512), BF16)] * 2,
        scratch_shapes=[pltpu.VMEM((2, SSM_GROUPS, SSM_STATE, SSM_GW), F32)],
        compiler_params=_cp("arbitrary", "arbitrary"),
        name="ssd_scan",
    )(*ops, *ops, alx, alc)


def _split3(x):
    x1 = x.astype(BF16)
    r = x - x1.astype(F32)
    x2 = r.astype(BF16)
    x3 = (r - x2.astype(F32)).astype(BF16)
    return x1, x2, x3


def _dot_01_lhs(m01, x):
    x1, x2, x3 = _split3(x)
    return _dot(m01, x1) + _dot(m01, x2) + _dot(m01, x3)


def _dot_01_rhs(x, m01):
    x1, x2, x3 = _split3(x)
    return _dot(x1, m01) + _dot(x2, m01) + _dot(x3, m01)


GDN_ROWS = 256


def _gdn_prep_kernel(q_ref, k_ref, v_ref, sm_ref, gT_ref, u_ref, w_ref, qg_ref, kd_ref, qk_ref, egl_ref):
    C = GDN_CHUNK
    row = lax.broadcasted_iota(jnp.int32, (C, C), 0)
    col = lax.broadcasted_iota(jnp.int32, (C, C), 1)
    jobs = []
    levels = []
    for d in range(2):
        keep = (col <= row) if d == 0 else (col >= row)
        late, early = (row, col) if d == 0 else (col, row)
        levels.append([(((row ^ col) >> (j + 1)) == 0) & ((late & (1 << j)) != 0) & ((early & (1 << j)) == 0)
                       for j in range(6)])
        tri = keep.astype(BF16)
        tri_t = ((row <= col) if d == 0 else (row >= col)).astype(BF16)
        last = C - 1 if d == 0 else 0
        for c in range(GDN_ROWS // C):
            rows = slice(c * C, (c + 1) * C)
            smc = sm_ref[rows, :]
            cums = _dot_01_lhs(tri, smc)
            cumr = _dot_01_rhs(gT_ref[c, 8 * d:8 * d + 8, :], tri_t)
            tot = cums[last:last + 1, :]
            for h in range(GDN_HEADS):
                lg = 16 + 8 * d + h
                jobs.append(dict(d=d, c=c, h=h, rows=rows, hs=slice(h * 128, (h + 1) * 128), keep=keep,
                                 gc=cums[:, lg:lg + 1], beta=smc[:, lg + 4:lg + 5],
                                 gl=tot[:, lg:lg + 1], gr=cumr[h:h + 1, :]))
    for j in jobs:
        q = q_ref[j["rows"], j["hs"]].astype(F32)
        k = k_ref[j["rows"], j["hs"]].astype(F32)
        j["dec"] = jnp.where(j["keep"], jnp.exp(j["gc"] - j["gr"]), 0.0)
        kb = k * j["beta"]
        both = _dot_nt(jnp.concatenate([kb, q], axis=0).astype(BF16), k.astype(BF16))
        j["a"] = both[:C] * j["dec"]
        j["n"] = -jnp.where(levels[j["d"]][0], j["a"], 0.0)
        qk_ref[j["d"], j["c"], j["h"]] = (both[C:] * j["dec"]).astype(BF16)
    for lev in range(1, 6):
        for j in jobs:
            l = jnp.where(levels[j["d"]][lev], j["a"], 0.0)
            j["y"] = l + _dot(l.astype(BF16), j["n"].astype(BF16))
        for j in jobs:
            j["n"] = j["n"] - j["y"] - _dot(j["n"].astype(BF16), j["y"].astype(BF16))
    for j in jobs:
        d, rows, hs, gc, gl, beta = j["d"], j["rows"], j["hs"], j["gc"], j["gl"], j["beta"]
        q = q_ref[rows, hs].astype(F32)
        k = k_ref[rows, hs].astype(F32)
        eg = jnp.exp(gc)
        rhs = jnp.concatenate([v_ref[rows, hs].astype(F32) * beta, k * beta * eg], axis=1)
        sol = rhs + _dot(j["n"].astype(BF16), rhs.astype(BF16))
        u_ref[d, rows, hs] = sol[:, :GDN_DV].astype(BF16)
        w_ref[d, rows, hs] = sol[:, GDN_DV:].astype(BF16)
        qg_ref[d, rows, hs] = (q * eg).astype(BF16)
        kd_ref[d, rows, hs] = (k * jnp.exp(gl - gc)).astype(BF16)
        egl_ref[d, j["c"], :, hs] = jnp.broadcast_to(jnp.exp(gl), (8, 128))


def gdn_prep(p, sm, gT):
    R = p.shape[0]
    T, C = GDN_ROWS, GDN_CHUNK
    nc = T // C
    col = lambda k: pl.BlockSpec((T, 512), lambda i: (i, C_QKV // 512 + k))
    dirrow = pl.BlockSpec((2, T, 512), lambda i: (0, i, 0))
    return pl.pallas_call(
        _gdn_prep_kernel,
        grid=(R // T,),
        in_specs=[col(0), col(1), col(2),
                  pl.BlockSpec((T, 128), lambda i: (i, 0)),
                  pl.BlockSpec((nc, 16, C), lambda i: (i, 0, 0))],
        out_specs=[dirrow, dirrow, dirrow, dirrow,
                   pl.BlockSpec((2, nc, GDN_HEADS, C, C), lambda i: (0, i, 0, 0, 0)),
                   pl.BlockSpec((2, nc, 8, 512), lambda i: (0, i, 0, 0))],
        out_shape=[jax.ShapeDtypeStruct((2, R, 512), BF16),
                   jax.ShapeDtypeStruct((2, R, 512), BF16),
                   jax.ShapeDtypeStruct((2, R, 512), BF16),
                   jax.ShapeDtypeStruct((2, R, 512), BF16),
                   jax.ShapeDtypeStruct((2, R // C, GDN_HEADS, C, C), BF16),
                   jax.ShapeDtypeStruct((2, R // C, 8, 512), F32)],
        compiler_params=_cp("arbitrary"),
        name="gdn_prep",
    )(p, p, p, sm, gT)


def _gdn_scan_kernel(uf, wf, qgf, kdf, qkf, eglf, ub, wb, qgb, kdb, qkb, eglb, of_ref, ob_ref, s_ref):
    C = GDN_CHUNK
    nch = GDN_ROWS // C

    @pl.when(pl.program_id(1) == 0)
    def _():
        s_ref[...] = jnp.zeros_like(s_ref)

    dirs = ((uf, wf, qgf, kdf, qkf, eglf, of_ref), (ub, wb, qgb, kdb, qkb, eglb, ob_ref))
    chains = [(d, h) for d in range(2) for h in range(GDN_HEADS)]
    S = {ch: s_ref[ch[0], ch[1]] for ch in chains}
    for i in range(nch):
        Sb, vnb, rows_of, c_of = {}, {}, {}, {}
        for d, h in chains:
            c_of[d] = i if d == 0 else nch - 1 - i
            rows_of[d] = slice(c_of[d] * C, (c_of[d] + 1) * C)
        for d, h in chains:
            hs = slice(h * 128, (h + 1) * 128)
            Sb[d, h] = S[d, h].astype(BF16)
            v_new = dirs[d][0][0, rows_of[d], hs].astype(F32) - _dot(dirs[d][1][0, rows_of[d], hs], Sb[d, h])
            vnb[d, h] = v_new.astype(BF16)
        for d, h in chains:
            hs = slice(h * 128, (h + 1) * 128)
            u_ref, w_ref, qg_ref, kd_ref, qk_ref, egl_ref, o_ref = dirs[d]
            S[d, h] = S[d, h] * egl_ref[0, c_of[d], 0:1, hs] + _dot_tn(kd_ref[0, rows_of[d], hs], vnb[d, h])
        for d, h in chains:
            hs = slice(h * 128, (h + 1) * 128)
            u_ref, w_ref, qg_ref, kd_ref, qk_ref, egl_ref, o_ref = dirs[d]
            o_ref[rows_of[d], hs] = (_dot(qg_ref[0, rows_of[d], hs], Sb[d, h])
                                     + _dot(qk_ref[0, c_of[d], h], vnb[d, h])).astype(BF16)
    for ch in chains:
        s_ref[ch[0], ch[1]] = S[ch]


def gdn_scan(rw, u, w, qg, kd, qk, egl):
    T, C = GDN_ROWS, GDN_CHUNK
    nc = T // C
    R = u.shape[1]
    nbc, nbl, base = rw.Lc // T, rw.Ll // T, rw.NC // T

    def blk(d):
        def f(b, s):
            jc = s if d == 0 else nbc - 1 - s
            jl = (s - nbc) if d == 0 else nbl - 1 - (s - nbc)
            return jnp.where(s < nbc, b * nbc + jc, base + b * nbl + jl)
        return f

    in_specs = []
    for d in range(2):
        f = blk(d)
        rowspec = pl.BlockSpec((1, T, 512), lambda b, s, f=f, d=d: (d, f(b, s), 0))
        in_specs += [rowspec, rowspec, rowspec, rowspec,
                     pl.BlockSpec((1, nc, GDN_HEADS, C, C), lambda b, s, f=f, d=d: (d, f(b, s), 0, 0, 0)),
                     pl.BlockSpec((1, nc, 8, 512), lambda b, s, f=f, d=d: (d, f(b, s), 0, 0))]
    out_specs = [pl.BlockSpec((T, 512), lambda b, s, f=blk(d): (f(b, s), 0)) for d in range(2)]
    ops = (u, w, qg, kd, qk, egl)
    return pl.pallas_call(
        _gdn_scan_kernel,
        grid=(rw.B, nbc + nbl),
        in_specs=in_specs,
        out_specs=out_specs,
        out_shape=[jax.ShapeDtypeStruct((R, 512), BF16)] * 2,
        scratch_shapes=[pltpu.VMEM((2, GDN_HEADS, GDN_DK, GDN_DV), F32)],
        compiler_params=_cp("arbitrary", "arbitrary"),
        name="gdn_scan",
    )(*ops, *ops)


def _merge_kernel(yhc_ref, yhl_ref, sf_ref, sb_ref, sx_ref, sz_ref, dx_ref, snw_ref, gf_ref, gb_ref, gg_ref, gnw_ref,
                  g0_ref, g1_ref, g2_ref, w0_ref, w1_ref, w2_ref, wo_ref, xc_ref, xl_ref, mod_ref, o_ref,
                  ys_ref, yg_ref, *, nctx_blk):
    tm = xc_ref.shape[0]
    rp = 64
    for r in range(tm // rp):
        rs = slice(r * rp, (r + 1) * rp)
        y = (sf_ref[rs, :].astype(F32) + sb_ref[rs, :].astype(F32)
             + sx_ref[rs, :].astype(F32) * dx_ref[...])
        y = y * _silu(sz_ref[rs, :].astype(F32))
        parts = []
        for g in range(SSM_GROUPS):
            yg = y[:, g * SSM_GW:(g + 1) * SSM_GW]
            parts.append(yg * lax.rsqrt(jnp.mean(yg * yg, axis=-1, keepdims=True) + EPS))
        ys_ref[rs, :] = (jnp.concatenate(parts, axis=1) * snw_ref[...]).astype(BF16)
        o = gf_ref[rs, :].astype(F32) + gb_ref[rs, :].astype(F32)
        parts = []
        for h in range(GDN_HEADS):
            oh = o[:, h * 128:(h + 1) * 128]
            parts.append(oh * lax.rsqrt(jnp.mean(oh * oh, axis=-1, keepdims=True) + EPS))
        yg_ref[rs, :] = (jnp.concatenate(parts, axis=1) * gnw_ref[...]
                         * _silu(gg_ref[rs, :].astype(F32))).astype(BF16)
    is_ctx = pl.program_id(0) < nctx_blk
    yh = jnp.where(is_ctx, yhc_ref[...], yhl_ref[...])
    m = (_sigmoid(g0_ref[...].astype(F32)) * _dot(yh, w0_ref[...])
         + _sigmoid(g1_ref[...].astype(F32)) * _dot(ys_ref[...], w1_ref[...])
         + _sigmoid(g2_ref[...].astype(F32)) * _dot(yg_ref[...], w2_ref[...]))
    x = jnp.where(is_ctx, xc_ref[...], xl_ref[...])
    o_ref[...] = x + mod_ref[0, 2:3, :] * _dot(m.astype(BF16), wo_ref[...])


def merge(rw, l, yh, y_f, y_b, dx, ssm_nw, o_f, o_b, gdn_nw, p, w0, w1, w2, wo, xs, mod):
    R = rw.R
    D = xs[0].shape[1]
    tm = min(rw.tm, 512)
    mi = rw.mod_index(tm)
    yspec = pl.BlockSpec((tm, 512), lambda i: (i, 0))
    pspec = lambda col: pl.BlockSpec((tm, 512), lambda i: (i, col // 512))
    vec = pl.BlockSpec((1, 512), lambda i: (0, 0))
    gspec = lambda k: pl.BlockSpec((tm, D), lambda i: (i, C_GATE // D + k))
    wspec = pl.BlockSpec((None, 512, D), lambda i: (l, 0, 0))
    return pl.pallas_call(
        functools.partial(_merge_kernel, nctx_blk=rw.NC // tm),
        grid=(R // tm,),
        in_specs=_stream_specs(rw, tm, yh, 1) + [
                  yspec, yspec, pspec(C_XBC), pspec(C_Z), vec, vec,
                  yspec, yspec, pspec(C_GG), vec,
                  gspec(0), gspec(1), gspec(2), wspec, wspec, wspec,
                  pl.BlockSpec((None, D, D), lambda i: (l, 0, 0))]
                 + _stream_specs(rw, tm, xs, 1)
                 + [pl.BlockSpec((None, 1, 8, D), lambda i: (l, mi(i), 0, 0))],
        out_specs=pl.BlockSpec((tm, D), lambda i: (i, 0)),
        out_shape=jax.ShapeDtypeStruct((R, D), F32),
        scratch_shapes=[pltpu.VMEM((tm, 512), BF16), pltpu.VMEM((tm, 512), BF16)],
        compiler_params=_cp("arbitrary"),
        name="merge",
    )(yh[0], yh[1], y_f, y_b, p, p, dx, ssm_nw, o_f, o_b, p, gdn_nw, p, p, p, w0, w1, w2, wo, xs[0], xs[1], mod)


def _swiglu_up_kernel(x_ref, nw_ref, mod_ref, wg_ref, wu_ref, o_ref, h_ref, g0_ref, g1_ref, u0_ref, u1_ref):
    @pl.when(pl.program_id(1) == 0)
    def _():
        h = _norm_mod(x_ref[...], nw_ref[...], mod_ref[0, 4:5, :], mod_ref[0, 3:4, :])
        h_ref[...] = h.astype(BF16)

    T, tn = o_ref.shape
    rows = g0_ref.shape[0]
    gs, us = (g0_ref, g1_ref), (u0_ref, u1_ref)

    def project(r):
        hh = h_ref[r * rows:(r + 1) * rows, :]
        gs[r % 2][...] = _dot(hh, wg_ref[...])
        us[r % 2][...] = _dot(hh, wu_ref[...])

    def finish(r):
        for q in range(rows // 64):
            for c in range(tn // 128):
                ps = (slice(q * 64, (q + 1) * 64), slice(c * 128, (c + 1) * 128))
                y = _silu(gs[r % 2][ps]) * us[r % 2][ps]
                o_ref[r * rows + q * 64:r * rows + (q + 1) * 64, ps[1]] = y.astype(o_ref.dtype)

    for r in range(T // rows):
        project(r)
        if r > 0:
            finish(r - 1)
    finish(T // rows - 1)


def swiglu_up(rw, l, x, nw, mod, wgu):
    R, D = x.shape
    tm = rw.tm
    tn = D_FF // 2
    nj = D_FF // tn
    mi = rw.mod_index(tm)
    return pl.pallas_call(
        _swiglu_up_kernel,
        grid=(R // tm, nj),
        in_specs=[
            pl.BlockSpec((tm, D), lambda i, j: (i, 0)),
            pl.BlockSpec((None, 1, D), lambda i, j: (l, 0, 0)),
            pl.BlockSpec((None, 1, 8, D), lambda i, j: (l, mi(i), 0, 0)),
            pl.BlockSpec((None, D, tn), lambda i, j: (l, 0, j)),
            pl.BlockSpec((None, D, tn), lambda i, j: (l, 0, nj + j)),
        ],
        out_specs=pl.BlockSpec((tm, tn), lambda i, j: (i, j)),
        out_shape=jax.ShapeDtypeStruct((R, D_FF), BF16),
        scratch_shapes=[pltpu.VMEM((tm, D), BF16)] + [pltpu.VMEM((min(256, tm), tn), F32)] * 4,
        compiler_params=_cp("arbitrary", "arbitrary"),
        name="swiglu_up",
    )(x, nw, mod, wgu, wgu)


def _swiglu_down_kernel(a_ref, w_ref, x_ref, mod_ref, o_ref):
    o_ref[...] = x_ref[...] + mod_ref[0, 5:6, :] * _dot(a_ref[...], w_ref[...])


def swiglu_down(rw, l, a, w, x, mod):
    R, D = x.shape
    tm = min(rw.tm, 512)
    mi = rw.mod_index(tm)
    return pl.pallas_call(
        _swiglu_down_kernel,
        grid=(R // tm,),
        in_specs=[
            pl.BlockSpec((tm, D_FF), lambda i: (i, 0)),
            pl.BlockSpec((None, D_FF, D), lambda i: (l, 0, 0)),
            pl.BlockSpec((tm, D), lambda i: (i, 0)),
            pl.BlockSpec((None, 1, 8, D), lambda i: (l, mi(i), 0, 0)),
        ],
        out_specs=pl.BlockSpec((tm, D), lambda i: (i, 0)),
        out_shape=jax.ShapeDtypeStruct((R, D), F32),
        compiler_params=_cp("arbitrary"),
        name="swiglu_down",
    )(a, w, x, mod)


def _final_norm_kernel(x_ref, w_ref, o_ref):
    x = x_ref[...]
    ms = jnp.mean(x * x, axis=-1, keepdims=True)
    o_ref[...] = x * lax.rsqrt(ms + EPS) * w_ref[...]


def final_norm(rw, x, w):
    D = x.shape[1]
    tm = rw.tm
    n0 = rw.NC // tm
    nl = rw.B * rw.Ll
    return pl.pallas_call(
        _final_norm_kernel,
        grid=(nl // tm,),
        in_specs=[pl.BlockSpec((tm, D), lambda i: (n0 + i, 0)), pl.BlockSpec((1, D), lambda i: (0, 0))],
        out_specs=pl.BlockSpec((tm, D), lambda i: (i, 0)),
        out_shape=jax.ShapeDtypeStruct((nl, D), F32),
        compiler_params=_cp("arbitrary"),
        name="final_norm",
    )(x, w.reshape(1, D))


def _regroup_w_in(w_in):
    o_dt = 3072
    o_gdn = 3088
    o_a = o_gdn + 2048
    o_b = o_a + 8
    o_gate = o_gdn + 2064
    wt = jnp.swapaxes(w_in, 1, 2).astype(BF16)
    pieces = [
        wt[:, 0:3072],
        wt[:, o_gdn:o_gdn + 2048],
        wt[:, o_gate:o_gate + 3072],
        wt[:, o_dt:o_dt + 16],
        wt[:, o_a:o_a + 4], wt[:, o_b:o_b + 4],
        wt[:, o_a + 4:o_a + 8], wt[:, o_b + 4:o_b + 8],
        jnp.zeros((wt.shape[0], N_IN_PAD - C_SM - 32, wt.shape[2]), wt.dtype),
    ]
    return jnp.swapaxes(jnp.concatenate(pieces, axis=1), 1, 2)


def kernel(x, c, ctx, c_ctx, w_ada, b_ada, norm1_w, norm2_w, w_in, hy_conv_w, hy_conv_b, hy_w1, hy_b1, hy_w2, hy_b2, hy_w3, hy_freq, hy_bias, ssm_conv_w, ssm_conv_b, ssm_dt_bias, ssm_A_log, ssm_D, ssm_norm_w, gdn_conv_w, gdn_dt_bias, gdn_A_log, gdn_norm_w, w_hy_out, w_ssm_out, w_gdn_out, w_out, w_gate_up, w_down, final_norm_w):
    B, Ll, D = x.shape
    Lc = ctx.shape[1]
    depth = w_ada.shape[0]
    assert Lc == CONV_ROWS and D == D_MODEL and B <= 15
    rw = Rows(B, Lc, Ll)
    R, NC = rw.R, rw.NC

    xs = (ctx.reshape(B * Lc, D), x.reshape(B * Ll, D))

    svec = jnp.concatenate([c_ctx[None, :], c, jnp.zeros((15 - B, D), F32)], axis=0)
    mod = ada_modulation(svec, w_ada, b_ada)
    mod = jnp.pad(mod.reshape(depth, 16, 6, D), ((0, 0), (0, 0), (0, 2), (0, 0)))

    w_in_r = _regroup_w_in(w_in)
    par = _in_proj_params(hy_conv_w, hy_conv_b, ssm_conv_w, ssm_conv_b, gdn_conv_w, ssm_dt_bias, gdn_dt_bias,
                          gdn_A_log)
    norm1 = norm1_w.reshape(depth, 1, D)
    norm2 = norm2_w.reshape(depth, 1, D)
    w_hy_o, w_ssm_o, w_gdn_o, w_o = (w.astype(BF16) for w in (w_hy_out, w_ssm_out, w_gdn_out, w_out))
    w_gu, w_dn = w_gate_up.astype(BF16), w_down.astype(BF16)
    fwd_l, inv_l = dft_tables(Ll)
    fwd_c, inv_c = dft_tables(Lc)
    feat_l, feat_c = hy_features(Ll), hy_features(Lc)

    for l in range(depth):
        p, sm = in_proj(rw, xs, l, norm1, mod, w_in_r, par)

        sm32_t = sm[:, :32].T
        dt_t = sm32_t[:16].reshape(2, 8, R)
        g_t = sm32_t[16:32].reshape(16, R // GDN_CHUNK, GDN_CHUNK).transpose(1, 0, 2)

        alx = jnp.repeat(ssm_A_log[l], SSM_HEAD_DIM, axis=-1).reshape(2, 1, 512)
        alc = ssm_A_log[l].reshape(2, 8, 1)
        y_f, y_b = ssd_scan(rw, p, sm, dt_t, alx, alc)
        dx = jnp.repeat(ssm_D[l], SSM_HEAD_DIM).reshape(1, 512)

        o_f, o_b = gdn_scan(rw, *gdn_prep(p, sm, g_t))

        hyu = p
        parts = []
        for (Bn, L, blk0, fwd, inv, feat) in ((B, Lc, 0, fwd_c, inv_c, feat_c),
                                              (B, Ll, NC // Ll, fwd_l, inv_l, feat_l)):
            if NC % L:
                raise ValueError("latent length must divide the context row count")
            filt = hy_filter(feat, hy_w1[l], hy_b1[l], hy_w2[l], hy_b2[l], hy_w3[l], hy_freq[l])
            kspec = matmul(fwd, filt, min(512, 2 * L), 512)
            z1 = long_conv(Bn, L, hyu, blk0, 0, hyu, blk0, 1, hy_bias[l, 0], fwd, inv, kspec, 0, F32)
            yy = long_conv(Bn, L, z1, 0, 0, hyu, blk0, 2, hy_bias[l, 1], fwd, inv, kspec, 1, BF16)
            parts.append(yy)
        y_hy = tuple(parts)

        xa = merge(rw, l, y_hy, y_f, y_b, dx, ssm_norm_w[l].reshape(1, 512),
                   o_f, o_b, jnp.tile(gdn_norm_w[l], GDN_HEADS).reshape(1, 512), p,
                   w_hy_o, w_ssm_o, w_gdn_o, w_o, xs, mod)
        act = swiglu_up(rw, l, xa, norm2, mod, w_gu)
        xa = swiglu_down(rw, l, act, w_dn, xa, mod)
        xs = (xa, xa)

    out = final_norm(rw, xa, final_norm_w)
    return out.reshape(B, Ll, D)
```

```python
import functools
import math

import jax
import jax.numpy as jnp
import numpy as np
from jax import lax
from jax.experimental import pallas as pl
from jax.experimental.pallas import tpu as pltpu

F32 = jnp.float32
BF16 = jnp.bfloat16
HI = lax.Precision.HIGHEST

EPS = 1e-6
D_MODEL = 1024
GRID_W = 64

HY_WIDTH = 512
HY_BANDS = 16
HY_EMB = 1 + 2 * HY_BANDS
HY_HIDDEN = 64
HY_SHORT_DECAY_PCT = 0.3
HY_LONG_DECAY_PCT = 1.5
HY_TARGET = 1e-2

SSM_HEADS = 8
SSM_HEAD_DIM = 64
SSM_WIDTH = 512
SSM_GROUPS = 2
SSM_HPG = 4
SSM_STATE = 128
SSM_CHUNK = 128
SSM_GW = SSM_HPG * SSM_HEAD_DIM

GDN_HEADS = 4
GDN_DK = 128
GDN_DV = 128
GDN_CHUNK = 64

D_FF = 2816

C_HY = 0
C_Z = 1536
C_XBC = 2048
C_QKV = 3072
C_GG = 4608
C_GATE = 5120
C_SM = 8192

CONV_ROWS = 256
FREQ_BLK = 256

VMEM_LIMIT = 56 * 1024 * 1024


def _cp(*sem, flags=None):
    return pltpu.CompilerParams(dimension_semantics=sem, vmem_limit_bytes=VMEM_LIMIT, flags=flags)


def _sigmoid(x):
    return 1.0 / (1.0 + jnp.exp(-x))


def _silu(x):
    return x * _sigmoid(x)


def _softplus(x):
    return jnp.maximum(x, 0.0) + jnp.log1p(jnp.exp(-jnp.abs(x)))


def _dot(a, b, precision=None):
    return jnp.dot(a, b, precision=precision, preferred_element_type=F32)


def _dot_nt(a, b):
    return lax.dot_general(a, b, (((1,), (1,)), ((), ())), preferred_element_type=F32)


def _dot_tn(a, b):
    return lax.dot_general(a, b, (((0,), (0,)), ((), ())), preferred_element_type=F32)


def _ada_kernel(s_ref, w_ref, b_ref, o_ref):
    s = _silu(s_ref[...])
    o_ref[0] = _dot(s, w_ref[0], HI) + b_ref[0]


def ada_modulation(svec, w_ada, b_ada):
    depth = w_ada.shape[0]
    D = D_MODEL
    return pl.pallas_call(
        _ada_kernel,
        grid=(depth, 6),
        in_specs=[
            pl.BlockSpec((16, D), lambda l, j: (0, 0)),
            pl.BlockSpec((1, D, D), lambda l, j: (l, 0, j)),
            pl.BlockSpec((1, 1, D), lambda l, j: (l, 0, j)),
        ],
        out_specs=pl.BlockSpec((1, 16, D), lambda l, j: (l, 0, j)),
        out_shape=jax.ShapeDtypeStruct((depth, 16, 6 * D), F32),
        compiler_params=_cp("arbitrary", "arbitrary"),
        name="ada",
    )(svec, w_ada, b_ada.reshape(depth, 1, 6 * D))


def _norm_mod(x, nw, scale, shift):
    ms = jnp.mean(x * x, axis=-1, keepdims=True)
    return (x * lax.rsqrt(ms + EPS) * nw) * (1.0 + scale) + shift


N_IN_PAD = C_SM + 128
IN_TN = N_IN_PAD // 5
MODE_RAW, MODE_CONV, MODE_CONV_SILU, MODE_CONV_SILU_L2, MODE_SMALL = range(5)


def _tile_mode(tile):
    col = tile * 128
    if col < C_Z:
        return MODE_CONV
    if col < C_XBC:
        return MODE_RAW
    if col < C_QKV:
        return MODE_CONV_SILU
    if col < C_QKV + 1024:
        return MODE_CONV_SILU_L2
    if col < C_GG:
        return MODE_CONV_SILU
    if col < C_SM:
        return MODE_RAW
    return MODE_SMALL
PAR_W0, PAR_W1, PAR_W2, PAR_BIAS, PAR_L2SCALE, PAR_SBIAS, PAR_SALOG, PAR_SKIND = range(8)


def _in_proj_kernel(xc_ref, xl_ref, nw_ref, mod_ref, w_ref, par_ref, o_ref, sm_ref, h_ref, raw0_ref, raw1_ref, *,
                    nctx_blk):
    j = pl.program_id(1)
    nj = N_IN_PAD // IN_TN
    raws = (raw0_ref, raw1_ref)

    @pl.when((j == 0) & (pl.program_id(0) < nctx_blk))
    def _():
        h = _norm_mod(xc_ref[...], nw_ref[...], mod_ref[0, 1:2, :], mod_ref[0, 0:1, :])
        h_ref[...] = h.astype(BF16)

    @pl.when((j == 0) & (pl.program_id(0) >= nctx_blk))
    def _():
        h = _norm_mod(xl_ref[...], nw_ref[...], mod_ref[0, 1:2, :], mod_ref[0, 0:1, :])
        h_ref[...] = h.astype(BF16)

    T = h_ref.shape[0]
    G = GRID_W
    per_ctx = CONV_ROWS // G
    is_latent = pl.program_id(0) >= nctx_blk
    sub = lax.broadcasted_iota(jnp.int32, (8, 128), 0)

    def raw_piece(src, g, c):
        return src[g * G:(g + 1) * G, c * 128:(c + 1) * 128]

    def conv(src, g, c):
        cs = slice(c * 128, (c + 1) * 128)
        x = raw_piece(src, g, c)
        zero = jnp.zeros((1, 128), F32)
        before = zero if g % per_ctx == 0 else jnp.where(is_latent, 0.0, src[g * G - 1:g * G, cs])
        after = zero if g % per_ctx == per_ctx - 1 else jnp.where(is_latent, 0.0, src[(g + 1) * G:(g + 1) * G + 1, cs])
        rp = pltpu.roll(x, 1, 0)
        rn = pltpu.roll(x, G - 1, 0)
        prev = jnp.concatenate([jnp.where(sub == 0, before, rp[0:8]), rp[8:]], axis=0)
        nxt = jnp.concatenate([rn[:G - 8], jnp.where(sub == 7, after, rn[G - 8:])], axis=0)
        return (prev * par_ref[PAR_W0:PAR_W0 + 1, cs] + x * par_ref[PAR_W1:PAR_W1 + 1, cs]
                + nxt * par_ref[PAR_W2:PAR_W2 + 1, cs] + par_ref[PAR_BIAS:PAR_BIAS + 1, cs])

    def conv_silu(src, g, c):
        return _silu(conv(src, g, c))

    def conv_silu_l2(src, g, c):
        y = _silu(conv(src, g, c))
        y = y * lax.rsqrt(jnp.sum(y * y, axis=-1, keepdims=True) + EPS)
        return y * par_ref[PAR_L2SCALE:PAR_L2SCALE + 1, c * 128:(c + 1) * 128]

    def small(src, g, c):
        cs = slice(c * 128, (c + 1) * 128)
        acc = raw_piece(src, g, c)
        kind = par_ref[PAR_SKIND:PAR_SKIND + 1, cs]
        sp = _softplus(acc + par_ref[PAR_SBIAS:PAR_SBIAS + 1, cs])
        dec = -jnp.exp(par_ref[PAR_SALOG:PAR_SALOG + 1, cs]) * sp
        return jnp.where(kind == 0.0, sp, jnp.where(kind == 1.0, dec, jnp.where(kind == 2.0, _sigmoid(acc), 0.0)))

    rows_mm = 256
    tiles = IN_TN // 128
    piece_fn = {MODE_RAW: raw_piece, MODE_CONV: conv, MODE_CONV_SILU: conv_silu,
                MODE_CONV_SILU_L2: conv_silu_l2, MODE_SMALL: small}

    def project(dst, r):
        rs = slice(r * rows_mm, (r + 1) * rows_mm)
        dst[rs, :] = _dot(h_ref[rs, :], w_ref[...])

    def finish(src, blk, g, c):
        mode = _tile_mode(blk * tiles + c)
        y = piece_fn[mode](src, g, c)
        if mode == MODE_SMALL:
            sm_ref[g * G:(g + 1) * G, :] = y
            y = jnp.zeros_like(y)
        o_ref[g * G:(g + 1) * G, c * 128:(c + 1) * 128] = y.astype(o_ref.dtype)

    for step in range(nj + 1):
        @pl.when(j == step)
        def _(step=step):
            blk = step - 1
            src, dst = raws[blk % 2], raws[step % 2]
            for r in range(T // rows_mm):
                if step < nj:
                    project(dst, r)
                if blk < 0:
                    continue
                for g in range(r * rows_mm // G, (r + 1) * rows_mm // G):
                    for c in range(tiles):
                        finish(src, blk, g, c)


class Rows:
    def __init__(self, B, Lc, Ll):
        self.B, self.Lc, self.Ll = B, Lc, Ll
        self.NC = B * Lc
        self.R = B * Lc + B * Ll
        assert self.NC % Ll == 0 or Ll % self.NC == 0
        tm = 1024
        while self.NC % tm or Ll % tm:
            tm //= 2
        self.tm = tm

    def mod_index(self, tm):
        nctx = self.NC // tm
        per = self.Ll // tm
        return lambda i: jnp.where(i < nctx, 0, 1 + (i - nctx) // per)


def _stream_specs(rw, tm, xs, ngrid):
    xc, xl = xs
    nctx = rw.NC // tm
    off = nctx if xl.shape[0] == rw.R else 0
    D = xc.shape[1]
    if ngrid == 1:
        return [pl.BlockSpec((tm, D), lambda i: (jnp.minimum(i, nctx - 1), 0)),
                pl.BlockSpec((tm, D), lambda i: (jnp.maximum(i - nctx, 0) + off, 0))]
    return [pl.BlockSpec((tm, D), lambda i, j: (jnp.minimum(i, nctx - 1), 0)),
            pl.BlockSpec((tm, D), lambda i, j: (jnp.maximum(i - nctx, 0) + off, 0))]


def in_proj(rw, xs, l, nw, mod, w, par):
    R = rw.R
    D = xs[0].shape[1]
    N = w.shape[2]
    tm, tn = rw.tm, IN_TN
    nj = N // tn
    assert N == N_IN_PAD
    mi = rw.mod_index(tm)
    done = lambda j: jnp.maximum(j - 1, 0)
    return pl.pallas_call(
        functools.partial(_in_proj_kernel, nctx_blk=rw.NC // tm),
        grid=(R // tm, nj + 1),
        in_specs=_stream_specs(rw, tm, xs, 2) + [
            pl.BlockSpec((None, 1, D), lambda i, j: (l, 0, 0)),
            pl.BlockSpec((None, 1, 8, D), lambda i, j: (l, mi(i), 0, 0)),
            pl.BlockSpec((None, D, tn), lambda i, j: (l, 0, jnp.minimum(j, nj - 1))),
            pl.BlockSpec((None, 8, tn), lambda i, j: (l, 0, done(j))),
        ],
        out_specs=[pl.BlockSpec((tm, tn), lambda i, j: (i, done(j))),
                   pl.BlockSpec((tm, 128), lambda i, j: (i, 0))],
        out_shape=[jax.ShapeDtypeStruct((R, N), BF16), jax.ShapeDtypeStruct((R, 128), F32)],
        scratch_shapes=[pltpu.VMEM((tm, D), BF16), pltpu.VMEM((tm, tn), F32), pltpu.VMEM((tm, tn), F32)],
        compiler_params=_cp("arbitrary", "arbitrary"),
        name="in_proj",
    )(xs[0], xs[1], nw, mod, w, par)


def _in_proj_params(hy_conv_w, hy_conv_b, ssm_conv_w, ssm_conv_b, gdn_conv_w, ssm_dt_bias, gdn_dt_bias, gdn_A_log):
    depth = hy_conv_w.shape[0]

    def row(pieces):
        out, pos = [], 0
        for off, a in pieces:
            out += [jnp.zeros((depth, off - pos), F32), a.astype(F32)]
            pos = off + a.shape[1]
        return jnp.concatenate(out + [jnp.zeros((depth, N_IN_PAD - pos), F32)], axis=1)
    z4 = jnp.zeros((depth, 4), F32)
    conv = [row([(C_HY, hy_conv_w[:, t]), (C_XBC, ssm_conv_w[:, t]), (C_QKV, gdn_conv_w[:, t])]) for t in range(3)]
    bias = row([(C_HY, hy_conv_b), (C_XBC, ssm_conv_b)])
    l2s = row([(C_QKV, jnp.full((depth, 512), GDN_DK ** -0.5, F32)), (C_QKV + 512, jnp.ones((depth, 512), F32))])
    sbias = row([(C_SM, jnp.concatenate([ssm_dt_bias.reshape(depth, 16), gdn_dt_bias[:, 0], z4,
                                         gdn_dt_bias[:, 1], z4], axis=1))])
    salog = row([(C_SM + 16, jnp.concatenate([gdn_A_log[:, 0], z4, gdn_A_log[:, 1], z4], axis=1))])
    kind = np.full((depth, N_IN_PAD), 3.0, np.float32)
    kind[:, C_SM:C_SM + 16] = 0.0
    kind[:, C_SM + 16:C_SM + 20] = 1.0
    kind[:, C_SM + 24:C_SM + 28] = 1.0
    kind[:, C_SM + 20:C_SM + 24] = 2.0
    kind[:, C_SM + 28:C_SM + 32] = 2.0
    return jnp.stack(conv + [bias, l2s, sbias, salog, jnp.asarray(kind)], axis=1)


def _hy_filter_kernel(z_ref, w1_ref, b1_ref, w2_ref, b2_ref, w3_ref, f0_ref, f1_ref, win_ref, o_ref, h_ref):
    @pl.when(pl.program_id(1) == 0)
    def _():
        h1 = jnp.sin(f0_ref[...] * (_dot(z_ref[...], w1_ref[...], HI) + b1_ref[...]))
        h_ref[...] = jnp.sin(f1_ref[...] * (_dot(h1, w2_ref[...], HI) + b2_ref[...]))

    h = _dot(h_ref[...], w3_ref[...], HI) * win_ref[...]
    tl = h.shape[0]
    row = lax.broadcasted_iota(jnp.int32, (tl, 1), 0) + pl.program_id(0) * tl
    drop = (row == 0) & (pl.program_id(1) % 2 == 1)
    o_ref[...] = jnp.where(drop, 0.0, h).astype(o_ref.dtype)


def hy_features(L):
    t = jnp.linspace(0.0, 1.0, L, dtype=F32)[:, None]
    w = 2.0 * math.pi * jnp.arange(L, dtype=F32)[:, None] / L
    f = jnp.linspace(1e-4, HY_BANDS - 1, HY_BANDS, dtype=F32)[None, :]
    z = jnp.concatenate([t, jnp.cos(f * w), -jnp.sin(f * w)], axis=-1)
    z = jnp.pad(z, ((0, 0), (0, 128 - HY_EMB)))
    min_decay = math.log(HY_TARGET) / HY_LONG_DECAY_PCT
    max_decay = math.log(HY_TARGET) / HY_SHORT_DECAY_PCT
    deltas = jnp.linspace(min_decay, max_decay, HY_WIDTH, dtype=F32)
    window = jnp.exp(-t * jnp.abs(deltas))
    return z, window


def hy_filter(feat, w1, b1, w2, b2, w3, freq):
    z, window = feat
    L = z.shape[0]
    H = HY_HIDDEN
    w1p = jnp.pad(w1, ((0, 128 - HY_EMB), (0, 128 - H)))
    w2p = jnp.pad(w2, ((0, 128 - H), (0, 128 - H)))
    w3p = jnp.pad(w3, ((0, 128 - H), (0, 0)))
    pad1 = lambda v: jnp.pad(v, (0, 128 - H)).reshape(1, 128)
    tl = 256
    full = lambda shape: pl.BlockSpec(shape, lambda i, j: (0, 0))
    return pl.pallas_call(
        _hy_filter_kernel,
        grid=(L // tl, 4),
        in_specs=[
            pl.BlockSpec((tl, 128), lambda i, j: (i, 0)),
            full((128, 128)), full((1, 128)), full((128, 128)), full((1, 128)),
            pl.BlockSpec((128, HY_WIDTH), lambda i, j: (0, j)),
            full((1, 128)), full((1, 128)),
            pl.BlockSpec((tl, HY_WIDTH), lambda i, j: (i, 0)),
        ],
        out_specs=pl.BlockSpec((tl, HY_WIDTH), lambda i, j: (i, j)),
        out_shape=jax.ShapeDtypeStruct((L, 4 * HY_WIDTH), BF16),
        scratch_shapes=[pltpu.VMEM((tl, 128), F32)],
        compiler_params=_cp("arbitrary", "arbitrary"),
        name="hy_filter",
    )(z, w1p, pad1(b1), w2p, pad1(b2), w3p, pad1(freq[0]), pad1(freq[1]), window)


def dft_table(L):
    N = 2 * L
    f = np.arange(L, dtype=np.int64)[:, None]
    s = np.arange(L, dtype=np.int64)[None, :]
    ang = ((f * s) % N).astype(np.float64) * (2.0 * math.pi / N)
    alt_s = (1 - 2 * (s % 2)).astype(np.float64)
    fwd = np.concatenate([np.cos(ang), np.where(f == 0, alt_s, -np.sin(ang))], axis=0)
    return jnp.asarray(fwd, dtype=BF16)


def dft_tables_split(L):
    N = 2 * L
    H = L // 2
    q = np.arange(H, dtype=np.int64)[:, None]
    m = np.arange(H, dtype=np.int64)[None, :]
    ang_e = ((q * 2 * m) % N).astype(np.float64) * (2.0 * math.pi / N)
    ang_o = ((q * (2 * m + 1)) % N).astype(np.float64) * (2.0 * math.pi / N)
    alt = (1 - 2 * (m % 2)).astype(np.float64)
    ce, co = np.cos(ang_e), np.cos(ang_o)
    se = np.where(q == 0, alt, -np.sin(ang_e))
    so = np.where(q == 0, -alt, -np.sin(ang_o))
    w = np.where(q == 0, 1.0, 2.0) / N
    ise = np.where(q == 0, 2.0 / N * alt, -np.sin(ang_e) * w)
    iso = np.where(q == 0, -2.0 / N * alt, -np.sin(ang_o) * w)
    fwd = np.stack([ce, co, se, so])
    inv = np.stack([(ce * w).T, ise.T, (co * w).T, iso.T])
    return jnp.asarray(fwd, dtype=BF16), jnp.asarray(inv, dtype=BF16)


def paired_spectrum(kspec, L):
    H = L // 2
    C = HY_WIDTH
    re, im = kspec[:L], kspec[L:]
    krs, kis = [], []
    for o in range(2):
        r0, r1 = re[:, 2 * o * C:(2 * o + 1) * C], re[:, (2 * o + 1) * C:(2 * o + 2) * C]
        i0, i1 = im[:, 2 * o * C:(2 * o + 1) * C], im[:, (2 * o + 1) * C:(2 * o + 2) * C]
        kr = r0 + r1
        ki = i0 - i1
        nyq = i0[0:1] + i1[0:1]
        krs.append(jnp.concatenate([kr[:H], nyq, kr[H + 1:][::-1]], axis=0))
        kis.append(jnp.concatenate([kr[H:H + 1], ki[1:H], ki[H:H + 1], ki[H + 1:][::-1]], axis=0))
    return jnp.concatenate(krs, axis=1), jnp.concatenate(kis, axis=1)


def _matmul_kernel(a_ref, b_ref, o_ref):
    o_ref[...] = _dot(a_ref[...], b_ref[...])


def matmul(a, b, tm, tn):
    M, K = a.shape
    N = b.shape[1]
    return pl.pallas_call(
        _matmul_kernel,
        grid=(M // tm, N // tn),
        in_specs=[pl.BlockSpec((tm, K), lambda i, j: (i, 0)), pl.BlockSpec((K, tn), lambda i, j: (0, j))],
        out_specs=pl.BlockSpec((tm, tn), lambda i, j: (i, j)),
        out_shape=jax.ShapeDtypeStruct((M, N), F32),
        compiler_params=_cp("arbitrary", "arbitrary"),
        name="matmul",
    )(a, b)


def _long_conv_kernel(u_ref, g_ref, bias_ref, fwd_ref, inv_ref, kr_ref, kr2_ref, ki_ref, ki2_ref, o_ref,
                      ue_ref, uo_ref, acce_ref, acco_ref, y_ref):
    f = pl.program_id(1)
    half = ue_ref.shape[0]

    lane_tiles = [slice(c * 128, (c + 1) * 128) for c in range(y_ref.shape[0])]

    @pl.when(f == 0)
    def _():
        for c, cs in enumerate(lane_tiles):
            y_c = y_ref.at[c]
            y_c[...] = u_ref[:, cs].astype(F32)
            ue_ref[:, cs] = y_c[pl.ds(0, half, stride=2), :].astype(BF16)
            uo_ref[:, cs] = y_c[pl.ds(1, half, stride=2), :].astype(BF16)
        acce_ref[...] = jnp.zeros_like(acce_ref)
        acco_ref[...] = jnp.zeros_like(acco_ref)

    ue, uo = ue_ref[...], uo_ref[...]
    ae, ao = _dot(fwd_ref[0], ue), _dot(fwd_ref[1], uo)
    be, bo = _dot(fwd_ref[2], ue), _dot(fwd_ref[3], uo)
    ur, ur2 = ae + ao, ae - ao
    ui, ui2 = be + bo, bo - be
    kr, kr2, ki, ki2 = kr_ref[...], kr2_ref[...], ki_ref[...], ki2_ref[...]
    pr, pi = ur * kr - ui * ki, ur * ki + ui * kr
    pr2, pi2 = ur2 * kr2 - ui2 * ki2, ur2 * ki2 + ui2 * kr2
    first = (lax.broadcasted_iota(jnp.int32, (fwd_ref.shape[1], 1), 0) == 0) & (f == 0)
    dc, ny = ur * kr, ur2 * kr2
    gr = jnp.where(first, dc + ny, pr + pr2)
    gi = jnp.where(first, be * ki - bo * ki2, pi - pi2)
    hr = jnp.where(first, dc - ny, pr - pr2)
    hi = jnp.where(first, be * ki2 + bo * ki, pi + pi2)
    acce_ref[...] += _dot(inv_ref[0], gr.astype(BF16)) + _dot(inv_ref[1], gi.astype(BF16))
    acco_ref[...] += _dot(inv_ref[2], hr.astype(BF16)) + _dot(inv_ref[3], hi.astype(BF16))

    @pl.when(f == pl.num_programs(1) - 1)
    def _():
        for c, cs in enumerate(lane_tiles):
            y_c = y_ref.at[c]
            y_c[pl.ds(0, half, stride=2), :] = acce_ref[:, cs]
            y_c[pl.ds(1, half, stride=2), :] = acco_ref[:, cs]
            u = u_ref[:, cs].astype(F32)
            o_ref[:, cs] = (g_ref[:, cs].astype(F32) * (y_c[...] + u * bias_ref[:, cs])).astype(o_ref.dtype)


def long_conv(B, L, u, u_rb0, u_cb, gate, g_rb0, gate_cb, bias, fwd, inv, kr, ki, order, out_dtype):
    C = HY_WIDTH
    H = L // 2
    FB = min(FREQ_BLK, H)
    nfb = H // FB
    kspec_lo = pl.BlockSpec((FB, C), lambda b, f: (f, order))
    kspec_hi = pl.BlockSpec((FB, C), lambda b, f: (nfb + f, order))
    return pl.pallas_call(
        _long_conv_kernel,
        grid=(B, nfb),
        in_specs=[
            pl.BlockSpec((L, C), lambda b, f: (u_rb0 + b, u_cb)),
            pl.BlockSpec((L, C), lambda b, f: (g_rb0 + b, gate_cb)),
            pl.BlockSpec((1, C), lambda b, f: (0, 0)),
            pl.BlockSpec((4, FB, H), lambda b, f: (0, f, 0)),
            pl.BlockSpec((4, H, FB), lambda b, f: (0, 0, f)),
            kspec_lo, kspec_hi, kspec_lo, kspec_hi,
        ],
        out_specs=pl.BlockSpec((L, C), lambda b, f: (b, 0)),
        out_shape=jax.ShapeDtypeStruct((B * L, C), out_dtype),
        scratch_shapes=[pltpu.VMEM((H, C), BF16), pltpu.VMEM((H, C), BF16),
                        pltpu.VMEM((H, C), F32), pltpu.VMEM((H, C), F32), pltpu.VMEM((C // 128, L, 128), F32)],
        compiler_params=_cp("arbitrary", "arbitrary"),
        name="long_conv",
    )(u, gate, bias.reshape(1, C), fwd, inv, kr, kr, ki, ki)


def _scan_blocks(rw, rows):
    nbc, nbl, base = rw.Lc // rows, rw.Ll // rows, rw.NC // rows

    def make(d):
        def f(b, s):
            jc = s if d == 0 else nbc - 1 - s
            jl = (s - nbc) if d == 0 else nbl - 1 - (s - nbc)
            return jnp.where(s < nbc, b * nbc + jc, base + b * nbl + jl)
        return f

    return [make(0), make(1)], nbc + nbl


def _expand_lanes(x, base, n, width):
    rows = x.shape[0]
    per = 128 // width
    lane = lax.broadcasted_iota(jnp.int32, (rows, 128), 1)
    tiles = []
    for t in range(n // per):
        c0 = base + t * per
        tile = jnp.broadcast_to(x[:, c0:c0 + 1], (rows, 128))
        for i in range(1, per):
            tile = jnp.where(lane >= i * width, jnp.broadcast_to(x[:, c0 + i:c0 + i + 1], (rows, 128)), tile)
        tiles.append(tile)
    return jnp.concatenate(tiles, axis=1)


def _ssd_kernel(xf, bf, cf, smf, dtf, xb, bb, cb_, smb, dtb, alx_ref, alc_ref, of_ref, ob_ref, h_ref):
    Q = SSM_CHUNK
    GW = SSM_GW

    @pl.when(pl.program_id(1) == 0)
    def _():
        h_ref[...] = jnp.zeros_like(h_ref)

    row = lax.broadcasted_iota(jnp.int32, (Q, Q), 0)
    col = lax.broadcasted_iota(jnp.int32, (Q, Q), 1)
    lane_head = lax.broadcasted_iota(jnp.int32, (Q, GW), 1) // SSM_HEAD_DIM
    dirs = ((xf, bf, cf, smf, dtf, of_ref), (xb, bb, cb_, smb, dtb, ob_ref))
    jobs = []
    for d in range(2):
        x_ref, b_ref, c_ref, sm_ref, dt_ref, o_ref = dirs[d]
        keep = (col <= row) if d == 0 else (col >= row)
        tri = keep.astype(BF16)
        tri_t = ((row <= col) if d == 0 else (row >= col)).astype(BF16)
        sm = sm_ref[...]
        a_x = -jnp.exp(alx_ref[d])
        dtx = _expand_lanes(sm, 8 * d, SSM_HEADS, SSM_HEAD_DIM)
        cumx = _expand_lanes(_dot_01_lhs(tri, sm), 8 * d, SSM_HEADS, SSM_HEAD_DIM) * a_x
        cumr = _dot_01_rhs(dt_ref[0], tri_t) * (-jnp.exp(alc_ref[d]))
        last = Q - 1 if d == 0 else 0
        totx = cumx[last:last + 1, :]
        xd = x_ref[...].astype(F32) * dtx
        xdw = xd * jnp.exp(totx - cumx)
        ecum = jnp.exp(cumx)
        for g in range(SSM_GROUPS):
            gs = slice(g * GW, (g + 1) * GW)
            jobs.append(dict(d=d, g=g, gs=gs, keep=keep, cumx=cumx, cumr=cumr, o_ref=o_ref,
                             bg=b_ref[:, g * SSM_STATE:(g + 1) * SSM_STATE].astype(BF16),
                             cg=c_ref[:, g * SSM_STATE:(g + 1) * SSM_STATE].astype(BF16),
                             xdg=xd[:, gs], xdw=xdw[:, gs].astype(BF16), ecum=ecum[:, gs],
                             etot=jnp.exp(totx[:, gs])))
    for j in jobs:
        j["cb"] = _dot_nt(j["cg"], j["bg"])
        j["h"] = h_ref[j["d"], j["g"]]
    for j in jobs:
        ms, xs = [], []
        for e4 in range(SSM_HPG):
            e = j["g"] * SSM_HPG + e4
            diff = j["cumx"][:, e * SSM_HEAD_DIM:e * SSM_HEAD_DIM + 1] - j["cumr"][e:e + 1, :]
            ms.append((j["cb"] * jnp.where(j["keep"], jnp.exp(diff), 0.0)).astype(BF16))
            xs.append(jnp.where(lane_head == e4, j["xdg"], 0.0).astype(BF16))
        yd = _dot(jnp.concatenate(ms, axis=1), jnp.concatenate(xs, axis=0))
        y_off = _dot(j["cg"], j["h"].astype(BF16)) * j["ecum"]
        j["o_ref"][:, j["gs"]] = (yd + y_off).astype(BF16)
    for j in jobs:
        h_ref[j["d"], j["g"]] = j["h"] * j["etot"] + _dot_tn(j["bg"], j["xdw"])


def ssd_scan(rw, p, sm, dtT, alx, alc):
    Q = SSM_CHUNK
    blks, nsteps = _scan_blocks(rw, Q)
    R = p.shape[0]
    in_specs = []
    for d in range(2):
        f = blks[d]
        in_specs += [
            pl.BlockSpec((Q, 512), lambda b, s, f=f: (f(b, s), C_XBC // 512)),
            pl.BlockSpec((Q, 256), lambda b, s, f=f: (f(b, s), C_XBC // 256 + 2)),
            pl.BlockSpec((Q, 256), lambda b, s, f=f: (f(b, s), C_XBC // 256 + 3)),
            pl.BlockSpec((Q, 128), lambda b, s, f=f: (f(b, s), 0)),
            pl.BlockSpec((1, 8, Q), lambda b, s, f=f, d=d: (d, 0, f(b, s))),
        ]
    in_specs += [pl.BlockSpec((2, 1, 512), lambda b, s: (0, 0, 0)), pl.BlockSpec((2, 8, 1), lambda b, s: (0, 0, 0))]
    ops = (p, p, p, sm, dtT)
    return pl.pallas_call(
        _ssd_kernel,
        grid=(rw.B, nsteps),
        in_specs=in_specs,
        out_specs=[pl.BlockSpec((Q, 512), lambda b, s, f=blks[d]: (f(b, s), 0)) for d in range(2)],
        out_shape=[jax.ShapeDtypeStruct((R, 512), BF16)] * 2,
        scratch_shapes=[pltpu.VMEM((2, SSM_GROUPS, SSM_STATE, SSM_GW), F32)],
        compiler_params=_cp("arbitrary", "arbitrary"),
        name="ssd_scan",
    )(*ops, *ops, alx, alc)


def _split3(x):
    x1 = x.astype(BF16)
    r = x - x1.astype(F32)
    x2 = r.astype(BF16)
    x3 = (r - x2.astype(F32)).astype(BF16)
    return x1, x2, x3


def _dot_01_lhs(m01, x):
    x1, x2, x3 = _split3(x)
    return _dot(m01, x1) + _dot(m01, x2) + _dot(m01, x3)


def _dot_01_rhs(x, m01):
    x1, x2, x3 = _split3(x)
    return _dot(x1, m01) + _dot(x2, m01) + _dot(x3, m01)


GDN_ROWS = 256


def _gdn_prep_kernel(q_ref, k_ref, v_ref, sm_ref, gT_ref, u_ref, w_ref, qg_ref, kd_ref, qk_ref, egl_ref):
    C = GDN_CHUNK
    row = lax.broadcasted_iota(jnp.int32, (C, C), 0)
    col = lax.broadcasted_iota(jnp.int32, (C, C), 1)
    jobs = []
    levels = []
    for d in range(2):
        keep = (col <= row) if d == 0 else (col >= row)
        late, early = (row, col) if d == 0 else (col, row)
        levels.append([(((row ^ col) >> (j + 1)) == 0) & ((late & (1 << j)) != 0) & ((early & (1 << j)) == 0)
                       for j in range(6)])
        tri = keep.astype(BF16)
        tri_t = ((row <= col) if d == 0 else (row >= col)).astype(BF16)
        last = C - 1 if d == 0 else 0
        for c in range(GDN_ROWS // C):
            rows = slice(c * C, (c + 1) * C)
            smc = sm_ref[rows, :]
            cums = _dot_01_lhs(tri, smc)
            cumr = _dot_01_rhs(gT_ref[c, 8 * d:8 * d + 8, :], tri_t)
            tot = cums[last:last + 1, :]
            for h in range(GDN_HEADS):
                lg = 16 + 8 * d + h
                jobs.append(dict(d=d, c=c, h=h, rows=rows, hs=slice(h * 128, (h + 1) * 128), keep=keep,
                                 gc=cums[:, lg:lg + 1], beta=smc[:, lg + 4:lg + 5],
                                 gl=tot[:, lg:lg + 1], gr=cumr[h:h + 1, :]))
    for j in jobs:
        q = q_ref[j["rows"], j["hs"]].astype(F32)
        k = k_ref[j["rows"], j["hs"]].astype(F32)
        j["dec"] = jnp.where(j["keep"], jnp.exp(j["gc"] - j["gr"]), 0.0)
        kb = k * j["beta"]
        both = _dot_nt(jnp.concatenate([kb, q], axis=0).astype(BF16), k.astype(BF16))
        j["a"] = both[:C] * j["dec"]
        j["n"] = -jnp.where(levels[j["d"]][0], j["a"], 0.0)
        qk_ref[j["d"], j["c"], j["h"]] = (both[C:] * j["dec"]).astype(BF16)
    for lev in range(1, 6):
        for j in jobs:
            l = jnp.where(levels[j["d"]][lev], j["a"], 0.0)
            j["y"] = l + _dot(l.astype(BF16), j["n"].astype(BF16))
        for j in jobs:
            j["n"] = j["n"] - j["y"] - _dot(j["n"].astype(BF16), j["y"].astype(BF16))
    for j in jobs:
        d, rows, hs, gc, gl, beta = j["d"], j["rows"], j["hs"], j["gc"], j["gl"], j["beta"]
        q = q_ref[rows, hs].astype(F32)
        k = k_ref[rows, hs].astype(F32)
        eg = jnp.exp(gc)
        rhs = jnp.concatenate([v_ref[rows, hs].astype(F32) * beta, k * beta * eg], axis=1)
        sol = rhs + _dot(j["n"].astype(BF16), rhs.astype(BF16))
        u_ref[d, rows, hs] = sol[:, :GDN_DV].astype(BF16)
        w_ref[d, rows, hs] = sol[:, GDN_DV:].astype(BF16)
        qg_ref[d, rows, hs] = (q * eg).astype(BF16)
        kd_ref[d, rows, hs] = (k * jnp.exp(gl - gc)).astype(BF16)
        egl_ref[d, j["c"], :, hs] = jnp.broadcast_to(jnp.exp(gl), (8, 128))


def gdn_prep(p, sm, gT):
    R = p.shape[0]
    T, C = GDN_ROWS, GDN_CHUNK
    nc = T // C
    col = lambda k: pl.BlockSpec((T, 512), lambda i: (i, C_QKV // 512 + k))
    dirrow = pl.BlockSpec((2, T, 512), lambda i: (0, i, 0))
    return pl.pallas_call(
        _gdn_prep_kernel,
        grid=(R // T,),
        in_specs=[col(0), col(1), col(2),
                  pl.BlockSpec((T, 128), lambda i: (i, 0)),
                  pl.BlockSpec((nc, 16, C), lambda i: (i, 0, 0))],
        out_specs=[dirrow, dirrow, dirrow, dirrow,
                   pl.BlockSpec((2, nc, GDN_HEADS, C, C), lambda i: (0, i, 0, 0, 0)),
                   pl.BlockSpec((2, nc, 8, 512), lambda i: (0, i, 0, 0))],
        out_shape=[jax.ShapeDtypeStruct((2, R, 512), BF16),
                   jax.ShapeDtypeStruct((2, R, 512), BF16),
                   jax.ShapeDtypeStruct((2, R, 512), BF16),
                   jax.ShapeDtypeStruct((2, R, 512), BF16),
                   jax.ShapeDtypeStruct((2, R // C, GDN_HEADS, C, C), BF16),
                   jax.ShapeDtypeStruct((2, R // C, 8, 512), F32)],
        compiler_params=_cp("arbitrary"),
        name="gdn_prep",
    )(p, p, p, sm, gT)


def _gdn_scan_kernel(uf, wf, qgf, kdf, qkf, eglf, ub, wb, qgb, kdb, qkb, eglb, of_ref, ob_ref, s_ref):
    C = GDN_CHUNK
    nch = GDN_ROWS // C

    @pl.when(pl.program_id(1) == 0)
    def _():
        s_ref[...] = jnp.zeros_like(s_ref)

    dirs = ((uf, wf, qgf, kdf, qkf, eglf, of_ref), (ub, wb, qgb, kdb, qkb, eglb, ob_ref))
    chains = [(d, h) for d in range(2) for h in range(GDN_HEADS)]
    S = {ch: s_ref[ch[0], ch[1]] for ch in chains}
    for i in range(nch):
        Sb, vnb, rows_of, c_of = {}, {}, {}, {}
        for d, h in chains:
            c_of[d] = i if d == 0 else nch - 1 - i
            rows_of[d] = slice(c_of[d] * C, (c_of[d] + 1) * C)
        for d, h in chains:
            hs = slice(h * 128, (h + 1) * 128)
            Sb[d, h] = S[d, h].astype(BF16)
            v_new = dirs[d][0][0, rows_of[d], hs].astype(F32) - _dot(dirs[d][1][0, rows_of[d], hs], Sb[d, h])
            vnb[d, h] = v_new.astype(BF16)
        for d, h in chains:
            hs = slice(h * 128, (h + 1) * 128)
            u_ref, w_ref, qg_ref, kd_ref, qk_ref, egl_ref, o_ref = dirs[d]
            S[d, h] = S[d, h] * egl_ref[0, c_of[d], 0:1, hs] + _dot_tn(kd_ref[0, rows_of[d], hs], vnb[d, h])
        for d, h in chains:
            hs = slice(h * 128, (h + 1) * 128)
            u_ref, w_ref, qg_ref, kd_ref, qk_ref, egl_ref, o_ref = dirs[d]
            o_ref[rows_of[d], hs] = (_dot(qg_ref[0, rows_of[d], hs], Sb[d, h])
                                     + _dot(qk_ref[0, c_of[d], h], vnb[d, h])).astype(BF16)
    for ch in chains:
        s_ref[ch[0], ch[1]] = S[ch]


def gdn_scan(rw, u, w, qg, kd, qk, egl):
    T, C = GDN_ROWS, GDN_CHUNK
    nc = T // C
    R = u.shape[1]
    nbc, nbl, base = rw.Lc // T, rw.Ll // T, rw.NC // T

    def blk(d):
        def f(b, s):
            jc = s if d == 0 else nbc - 1 - s
            jl = (s - nbc) if d == 0 else nbl - 1 - (s - nbc)
            return jnp.where(s < nbc, b * nbc + jc, base + b * nbl + jl)
        return f

    in_specs = []
    for d in range(2):
        f = blk(d)
        rowspec = pl.BlockSpec((1, T, 512), lambda b, s, f=f, d=d: (d, f(b, s), 0))
        in_specs += [rowspec, rowspec, rowspec, rowspec,
                     pl.BlockSpec((1, nc, GDN_HEADS, C, C), lambda b, s, f=f, d=d: (d, f(b, s), 0, 0, 0)),
                     pl.BlockSpec((1, nc, 8, 512), lambda b, s, f=f, d=d: (d, f(b, s), 0, 0))]
    out_specs = [pl.BlockSpec((T, 512), lambda b, s, f=blk(d): (f(b, s), 0)) for d in range(2)]
    ops = (u, w, qg, kd, qk, egl)
    return pl.pallas_call(
        _gdn_scan_kernel,
        grid=(rw.B, nbc + nbl),
        in_specs=in_specs,
        out_specs=out_specs,
        out_shape=[jax.ShapeDtypeStruct((R, 512), BF16)] * 2,
        scratch_shapes=[pltpu.VMEM((2, GDN_HEADS, GDN_DK, GDN_DV), F32)],
        compiler_params=_cp("arbitrary", "arbitrary"),
        name="gdn_scan",
    )(*ops, *ops)


def _merge_kernel(yhc_ref, yhl_ref, sf_ref, sb_ref, sx_ref, sz_ref, dx_ref, snw_ref, gf_ref, gb_ref, gg_ref, gnw_ref,
                  g0_ref, g1_ref, g2_ref, w0_ref, w1_ref, w2_ref, wo_ref, xc_ref, xl_ref, mod_ref, o_ref,
                  ys_ref, yg_ref, *, nctx_blk):
    tm = xc_ref.shape[0]
    rp = 64
    for r in range(tm // rp):
        rs = slice(r * rp, (r + 1) * rp)
        y = (sf_ref[rs, :].astype(F32) + sb_ref[rs, :].astype(F32)
             + sx_ref[rs, :].astype(F32) * dx_ref[...])
        y = y * _silu(sz_ref[rs, :].astype(F32))
        parts = []
        for g in range(SSM_GROUPS):
            yg = y[:, g * SSM_GW:(g + 1) * SSM_GW]
            parts.append(yg * lax.rsqrt(jnp.mean(yg * yg, axis=-1, keepdims=True) + EPS))
        ys_ref[rs, :] = (jnp.concatenate(parts, axis=1) * snw_ref[...]).astype(BF16)
        o = gf_ref[rs, :].astype(F32) + gb_ref[rs, :].astype(F32)
        parts = []
        for h in range(GDN_HEADS):
            oh = o[:, h * 128:(h + 1) * 128]
            parts.append(oh * lax.rsqrt(jnp.mean(oh * oh, axis=-1, keepdims=True) + EPS))
        yg_ref[rs, :] = (jnp.concatenate(parts, axis=1) * gnw_ref[...]
                         * _silu(gg_ref[rs, :].astype(F32))).astype(BF16)
    is_ctx = pl.program_id(0) < nctx_blk
    yh = jnp.where(is_ctx, yhc_ref[...], yhl_ref[...])
    m = (_sigmoid(g0_ref[...].astype(F32)) * _dot(yh, w0_ref[...])
         + _sigmoid(g1_ref[...].astype(F32)) * _dot(ys_ref[...], w1_ref[...])
         + _sigmoid(g2_ref[...].astype(F32)) * _dot(yg_ref[...], w2_ref[...]))
    x = jnp.where(is_ctx, xc_ref[...], xl_ref[...])
    o_ref[...] = x + mod_ref[0, 2:3, :] * _dot(m.astype(BF16), wo_ref[...])


def merge(rw, l, yh, y_f, y_b, dx, ssm_nw, o_f, o_b, gdn_nw, p, w0, w1, w2, wo, xs, mod):
    R = rw.R
    D = xs[0].shape[1]
    tm = min(rw.tm, 512)
    mi = rw.mod_index(tm)
    yspec = pl.BlockSpec((tm, 512), lambda i: (i, 0))
    pspec = lambda col: pl.BlockSpec((tm, 512), lambda i: (i, col // 512))
    vec = pl.BlockSpec((1, 512), lambda i: (0, 0))
    gspec = lambda k: pl.BlockSpec((tm, D), lambda i: (i, C_GATE // D + k))
    wspec = pl.BlockSpec((None, 512, D), lambda i: (l, 0, 0))
    return pl.pallas_call(
        functools.partial(_merge_kernel, nctx_blk=rw.NC // tm),
        grid=(R // tm,),
        in_specs=_stream_specs(rw, tm, yh, 1) + [
                  yspec, yspec, pspec(C_XBC), pspec(C_Z), vec, vec,
                  yspec, yspec, pspec(C_GG), vec,
                  gspec(0), gspec(1), gspec(2), wspec, wspec, wspec,
                  pl.BlockSpec((None, D, D), lambda i: (l, 0, 0))]
                 + _stream_specs(rw, tm, xs, 1)
                 + [pl.BlockSpec((None, 1, 8, D), lambda i: (l, mi(i), 0, 0))],
        out_specs=pl.BlockSpec((tm, D), lambda i: (i, 0)),
        out_shape=jax.ShapeDtypeStruct((R, D), F32),
        scratch_shapes=[pltpu.VMEM((tm, 512), BF16), pltpu.VMEM((tm, 512), BF16)],
        compiler_params=_cp("arbitrary"),
        name="merge",
    )(yh[0], yh[1], y_f, y_b, p, p, dx, ssm_nw, o_f, o_b, p, gdn_nw, p, p, p, w0, w1, w2, wo, xs[0], xs[1], mod)


def _swiglu_up_kernel(x_ref, nw_ref, mod_ref, wg_ref, wu_ref, o_ref, h_ref, g0_ref, g1_ref, u0_ref, u1_ref):
    @pl.when(pl.program_id(1) == 0)
    def _():
        h = _norm_mod(x_ref[...], nw_ref[...], mod_ref[0, 4:5, :], mod_ref[0, 3:4, :])
        h_ref[...] = h.astype(BF16)

    T, tn = o_ref.shape
    rows = g0_ref.shape[0]
    gs, us = (g0_ref, g1_ref), (u0_ref, u1_ref)

    def project(r):
        hh = h_ref[r * rows:(r + 1) * rows, :]
        gs[r % 2][...] = _dot(hh, wg_ref[...])
        us[r % 2][...] = _dot(hh, wu_ref[...])

    def finish(r):
        for q in range(rows // 64):
            for c in range(tn // 128):
                ps = (slice(q * 64, (q + 1) * 64), slice(c * 128, (c + 1) * 128))
                y = _silu(gs[r % 2][ps]) * us[r % 2][ps]
                o_ref[r * rows + q * 64:r * rows + (q + 1) * 64, ps[1]] = y.astype(o_ref.dtype)

    for r in range(T // rows):
        project(r)
        if r > 0:
            finish(r - 1)
    finish(T // rows - 1)


def swiglu_up(rw, l, x, nw, mod, wgu):
    R, D = x.shape
    tm = rw.tm
    tn = D_FF // 2
    nj = D_FF // tn
    mi = rw.mod_index(tm)
    return pl.pallas_call(
        _swiglu_up_kernel,
        grid=(R // tm, nj),
        in_specs=[
            pl.BlockSpec((tm, D), lambda i, j: (i, 0)),
            pl.BlockSpec((None, 1, D), lambda i, j: (l, 0, 0)),
            pl.BlockSpec((None, 1, 8, D), lambda i, j: (l, mi(i), 0, 0)),
            pl.BlockSpec((None, D, tn), lambda i, j: (l, 0, j)),
            pl.BlockSpec((None, D, tn), lambda i, j: (l, 0, nj + j)),
        ],
        out_specs=pl.BlockSpec((tm, tn), lambda i, j: (i, j)),
        out_shape=jax.ShapeDtypeStruct((R, D_FF), BF16),
        scratch_shapes=[pltpu.VMEM((tm, D), BF16)] + [pltpu.VMEM((min(256, tm), tn), F32)] * 4,
        compiler_params=_cp("arbitrary", "arbitrary"),
        name="swiglu_up",
    )(x, nw, mod, wgu, wgu)


def _swiglu_down_kernel(a_ref, w_ref, x_ref, mod_ref, o_ref):
    o_ref[...] = x_ref[...] + mod_ref[0, 5:6, :] * _dot(a_ref[...], w_ref[...])


def swiglu_down(rw, l, a, w, x, mod):
    R, D = x.shape
    tm = min(rw.tm, 512)
    mi = rw.mod_index(tm)
    return pl.pallas_call(
        _swiglu_down_kernel,
        grid=(R // tm,),
        in_specs=[
            pl.BlockSpec((tm, D_FF), lambda i: (i, 0)),
            pl.BlockSpec((None, D_FF, D), lambda i: (l, 0, 0)),
            pl.BlockSpec((tm, D), lambda i: (i, 0)),
            pl.BlockSpec((None, 1, 8, D), lambda i: (l, mi(i), 0, 0)),
        ],
        out_specs=pl.BlockSpec((tm, D), lambda i: (i, 0)),
        out_shape=jax.ShapeDtypeStruct((R, D), F32),
        compiler_params=_cp("arbitrary"),
        name="swiglu_down",
    )(a, w, x, mod)


def _final_norm_kernel(x_ref, w_ref, o_ref):
    x = x_ref[...]
    ms = jnp.mean(x * x, axis=-1, keepdims=True)
    o_ref[...] = x * lax.rsqrt(ms + EPS) * w_ref[...]


def final_norm(rw, x, w):
    D = x.shape[1]
    tm = rw.tm
    n0 = rw.NC // tm
    nl = rw.B * rw.Ll
    return pl.pallas_call(
        _final_norm_kernel,
        grid=(nl // tm,),
        in_specs=[pl.BlockSpec((tm, D), lambda i: (n0 + i, 0)), pl.BlockSpec((1, D), lambda i: (0, 0))],
        out_specs=pl.BlockSpec((tm, D), lambda i: (i, 0)),
        out_shape=jax.ShapeDtypeStruct((nl, D), F32),
        compiler_params=_cp("arbitrary"),
        name="final_norm",
    )(x, w.reshape(1, D))


def _regroup_w_in(w_in):
    o_dt = 3072
    o_gdn = 3088
    o_a = o_gdn + 2048
    o_b = o_a + 8
    o_gate = o_gdn + 2064
    wt = jnp.swapaxes(w_in, 1, 2).astype(BF16)
    pieces = [
        wt[:, 0:3072],
        wt[:, o_gdn:o_gdn + 2048],
        wt[:, o_gate:o_gate + 3072],
        wt[:, o_dt:o_dt + 16],
        wt[:, o_a:o_a + 4], wt[:, o_b:o_b + 4],
        wt[:, o_a + 4:o_a + 8], wt[:, o_b + 4:o_b + 8],
        jnp.zeros((wt.shape[0], N_IN_PAD - C_SM - 32, wt.shape[2]), wt.dtype),
    ]
    return jnp.swapaxes(jnp.concatenate(pieces, axis=1), 1, 2)


def kernel(x, c, ctx, c_ctx, w_ada, b_ada, norm1_w, norm2_w, w_in, hy_conv_w, hy_conv_b, hy_w1, hy_b1, hy_w2, hy_b2, hy_w3, hy_freq, hy_bias, ssm_conv_w, ssm_conv_b, ssm_dt_bias, ssm_A_log, ssm_D, ssm_norm_w, gdn_conv_w, gdn_dt_bias, gdn_A_log, gdn_norm_w, w_hy_out, w_ssm_out, w_gdn_out, w_out, w_gate_up, w_down, final_norm_w):
    B, Ll, D = x.shape
    Lc = ctx.shape[1]
    depth = w_ada.shape[0]
    assert Lc == CONV_ROWS and D == D_MODEL and B <= 15
    rw = Rows(B, Lc, Ll)
    R, NC = rw.R, rw.NC

    xs = (ctx.reshape(B * Lc, D), x.reshape(B * Ll, D))

    svec = jnp.concatenate([c_ctx[None, :], c, jnp.zeros((15 - B, D), F32)], axis=0)
    mod = ada_modulation(svec, w_ada, b_ada)
    mod = jnp.pad(mod.reshape(depth, 16, 6, D), ((0, 0), (0, 0), (0, 2), (0, 0)))

    w_in_r = _regroup_w_in(w_in)
    par = _in_proj_params(hy_conv_w, hy_conv_b, ssm_conv_w, ssm_conv_b, gdn_conv_w, ssm_dt_bias, gdn_dt_bias,
                          gdn_A_log)
    norm1 = norm1_w.reshape(depth, 1, D)
    norm2 = norm2_w.reshape(depth, 1, D)
    w_hy_o, w_ssm_o, w_gdn_o, w_o = (w.astype(BF16) for w in (w_hy_out, w_ssm_out, w_gdn_out, w_out))
    w_gu, w_dn = w_gate_up.astype(BF16), w_down.astype(BF16)
    dft_l = (dft_table(Ll),) + dft_tables_split(Ll)
    dft_c = (dft_table(Lc),) + dft_tables_split(Lc)
    feat_l, feat_c = hy_features(Ll), hy_features(Lc)

    for l in range(depth):
        p, sm = in_proj(rw, xs, l, norm1, mod, w_in_r, par)

        sm32_t = sm[:, :32].T
        dt_t = sm32_t[:16].reshape(2, 8, R)
        g_t = sm32_t[16:32].reshape(16, R // GDN_CHUNK, GDN_CHUNK).transpose(1, 0, 2)

        alx = jnp.repeat(ssm_A_log[l], SSM_HEAD_DIM, axis=-1).reshape(2, 1, 512)
        alc = ssm_A_log[l].reshape(2, 8, 1)
        y_f, y_b = ssd_scan(rw, p, sm, dt_t, alx, alc)
        dx = jnp.repeat(ssm_D[l], SSM_HEAD_DIM).reshape(1, 512)

        o_f, o_b = gdn_scan(rw, *gdn_prep(p, sm, g_t))

        hyu = p
        parts = []
        for (Bn, L, blk0, (full, fwd, inv), feat) in ((B, Lc, 0, dft_c, feat_c),
                                                      (B, Ll, NC // Ll, dft_l, feat_l)):
            if NC % L:
                raise ValueError("latent length must divide the context row count")
            filt = hy_filter(feat, hy_w1[l], hy_b1[l], hy_w2[l], hy_b2[l], hy_w3[l], hy_freq[l])
            kr, ki = paired_spectrum(matmul(full, filt, min(512, 2 * L), 512), L)
            z1 = long_conv(Bn, L, hyu, blk0, 0, hyu, blk0, 1, hy_bias[l, 0], fwd, inv, kr, ki, 0, F32)
            yy = long_conv(Bn, L, z1, 0, 0, hyu, blk0, 2, hy_bias[l, 1], fwd, inv, kr, ki, 1, BF16)
            parts.append(yy)
        y_hy = tuple(parts)

        xa = merge(rw, l, y_hy, y_f, y_b, dx, ssm_norm_w[l].reshape(1, 512),
                   o_f, o_b, jnp.tile(gdn_norm_w[l], GDN_HEADS).reshape(1, 512), p,
                   w_hy_o, w_ssm_o, w_gdn_o, w_o, xs, mod)
        act = swiglu_up(rw, l, xa, norm2, mod, w_gu)
        xa = swiglu_down(rw, l, act, w_dn, xa, mod)
        xs = (xa, xa)

    out = final_norm(rw, xa, final_norm_w)
    return out.reshape(B, Ll, D)
```

```python
import functools
import math

import jax
import jax.numpy as jnp
import numpy as np
from jax import lax
from jax.experimental import pallas as pl
from jax.experimental.pallas import tpu as pltpu

F32 = jnp.float32
BF16 = jnp.bfloat16
HI = lax.Precision.HIGHEST

EPS = 1e-6
D_MODEL = 1024
GRID_W = 64

HY_WIDTH = 512
HY_BANDS = 16
HY_EMB = 1 + 2 * HY_BANDS
HY_HIDDEN = 64
HY_SHORT_DECAY_PCT = 0.3
HY_LONG_DECAY_PCT = 1.5
HY_TARGET = 1e-2

SSM_HEADS = 8
SSM_HEAD_DIM = 64
SSM_WIDTH = 512
SSM_GROUPS = 2
SSM_HPG = 4
SSM_STATE = 128
SSM_CHUNK = 128
SSM_GW = SSM_HPG * SSM_HEAD_DIM

GDN_HEADS = 4
GDN_DK = 128
GDN_DV = 128
GDN_CHUNK = 64

D_FF = 2816

C_HY = 0
C_Z = 1536
C_XBC = 2048
C_QKV = 3072
C_GG = 4608
C_GATE = 5120
C_SM = 8192

CONV_ROWS = 256
FREQ_BLK = 256

VMEM_LIMIT = 56 * 1024 * 1024


def _cp(*sem, flags=None):
    return pltpu.CompilerParams(dimension_semantics=sem, vmem_limit_bytes=VMEM_LIMIT, flags=flags)


def _sigmoid(x):
    return 1.0 / (1.0 + jnp.exp(-x))


def _silu(x):
    return x * _sigmoid(x)


def _softplus(x):
    return jnp.maximum(x, 0.0) + jnp.log1p(jnp.exp(-jnp.abs(x)))


def _dot(a, b, precision=None):
    return jnp.dot(a, b, precision=precision, preferred_element_type=F32)


def _dot_nt(a, b):
    return lax.dot_general(a, b, (((1,), (1,)), ((), ())), preferred_element_type=F32)


def _dot_tn(a, b):
    return lax.dot_general(a, b, (((0,), (0,)), ((), ())), preferred_element_type=F32)


def _ada_kernel(s_ref, w_ref, b_ref, o_ref):
    s = _silu(s_ref[...])
    o_ref[0] = _dot(s, w_ref[0], HI) + b_ref[0]


def ada_modulation(svec, w_ada, b_ada):
    depth = w_ada.shape[0]
    D = D_MODEL
    return pl.pallas_call(
        _ada_kernel,
        grid=(depth, 6),
        in_specs=[
            pl.BlockSpec((16, D), lambda l, j: (0, 0)),
            pl.BlockSpec((1, D, D), lambda l, j: (l, 0, j)),
            pl.BlockSpec((1, 1, D), lambda l, j: (l, 0, j)),
        ],
        out_specs=pl.BlockSpec((1, 16, D), lambda l, j: (l, 0, j)),
        out_shape=jax.ShapeDtypeStruct((depth, 16, 6 * D), F32),
        compiler_params=_cp("arbitrary", "arbitrary"),
        name="ada",
    )(svec, w_ada, b_ada.reshape(depth, 1, 6 * D))


def _norm_mod(x, nw, scale, shift):
    ms = jnp.mean(x * x, axis=-1, keepdims=True)
    return (x * lax.rsqrt(ms + EPS) * nw) * (1.0 + scale) + shift


N_IN_PAD = C_SM + 128
IN_TN = N_IN_PAD // 5
MODE_RAW, MODE_CONV, MODE_CONV_SILU, MODE_CONV_SILU_L2, MODE_SMALL = range(5)


def _tile_mode(tile):
    col = tile * 128
    if col < C_Z:
        return MODE_CONV
    if col < C_XBC:
        return MODE_RAW
    if col < C_QKV:
        return MODE_CONV_SILU
    if col < C_QKV + 1024:
        return MODE_CONV_SILU_L2
    if col < C_GG:
        return MODE_CONV_SILU
    if col < C_SM:
        return MODE_RAW
    return MODE_SMALL
PAR_W0, PAR_W1, PAR_W2, PAR_BIAS, PAR_L2SCALE, PAR_SBIAS, PAR_SALOG, PAR_SKIND = range(8)


def _in_proj_kernel(xc_ref, xl_ref, nw_ref, mod_ref, w_ref, par_ref, o_ref, sm_ref, h_ref, raw0_ref, raw1_ref, *,
                    nctx_blk):
    j = pl.program_id(1)
    nj = N_IN_PAD // IN_TN
    raws = (raw0_ref, raw1_ref)

    @pl.when((j == 0) & (pl.program_id(0) < nctx_blk))
    def _():
        h = _norm_mod(xc_ref[...], nw_ref[...], mod_ref[0, 1:2, :], mod_ref[0, 0:1, :])
        h_ref[...] = h.astype(BF16)

    @pl.when((j == 0) & (pl.program_id(0) >= nctx_blk))
    def _():
        h = _norm_mod(xl_ref[...], nw_ref[...], mod_ref[0, 1:2, :], mod_ref[0, 0:1, :])
        h_ref[...] = h.astype(BF16)

    T = h_ref.shape[0]
    G = GRID_W
    per_ctx = CONV_ROWS // G
    is_latent = pl.program_id(0) >= nctx_blk
    sub = lax.broadcasted_iota(jnp.int32, (8, 128), 0)

    def raw_piece(src, g, c):
        return src[g * G:(g + 1) * G, c * 128:(c + 1) * 128]

    def conv(src, g, c):
        cs = slice(c * 128, (c + 1) * 128)
        x = raw_piece(src, g, c)
        zero = jnp.zeros((1, 128), F32)
        before = zero if g % per_ctx == 0 else jnp.where(is_latent, 0.0, src[g * G - 1:g * G, cs])
        after = zero if g % per_ctx == per_ctx - 1 else jnp.where(is_latent, 0.0, src[(g + 1) * G:(g + 1) * G + 1, cs])
        rp = pltpu.roll(x, 1, 0)
        rn = pltpu.roll(x, G - 1, 0)
        prev = jnp.concatenate([jnp.where(sub == 0, before, rp[0:8]), rp[8:]], axis=0)
        nxt = jnp.concatenate([rn[:G - 8], jnp.where(sub == 7, after, rn[G - 8:])], axis=0)
        return (prev * par_ref[PAR_W0:PAR_W0 + 1, cs] + x * par_ref[PAR_W1:PAR_W1 + 1, cs]
                + nxt * par_ref[PAR_W2:PAR_W2 + 1, cs] + par_ref[PAR_BIAS:PAR_BIAS + 1, cs])

    def conv_silu(src, g, c):
        return _silu(conv(src, g, c))

    def conv_silu_l2(src, g, c):
        y = _silu(conv(src, g, c))
        y = y * lax.rsqrt(jnp.sum(y * y, axis=-1, keepdims=True) + EPS)
        return y * par_ref[PAR_L2SCALE:PAR_L2SCALE + 1, c * 128:(c + 1) * 128]

    def small(src, g, c):
        cs = slice(c * 128, (c + 1) * 128)
        acc = raw_piece(src, g, c)
        kind = par_ref[PAR_SKIND:PAR_SKIND + 1, cs]
        sp = _softplus(acc + par_ref[PAR_SBIAS:PAR_SBIAS + 1, cs])
        dec = -jnp.exp(par_ref[PAR_SALOG:PAR_SALOG + 1, cs]) * sp
        return jnp.where(kind == 0.0, sp, jnp.where(kind == 1.0, dec, jnp.where(kind == 2.0, _sigmoid(acc), 0.0)))

    rows_mm = 256
    tiles = IN_TN // 128
    piece_fn = {MODE_RAW: raw_piece, MODE_CONV: conv, MODE_CONV_SILU: conv_silu,
                MODE_CONV_SILU_L2: conv_silu_l2, MODE_SMALL: small}

    def project(dst, r):
        rs = slice(r * rows_mm, (r + 1) * rows_mm)
        dst[rs, :] = _dot(h_ref[rs, :], w_ref[...])

    def finish(src, blk, g, c):
        mode = _tile_mode(blk * tiles + c)
        y = piece_fn[mode](src, g, c)
        if mode == MODE_SMALL:
            sm_ref[g * G:(g + 1) * G, :] = y
            y = jnp.zeros_like(y)
        o_ref[g * G:(g + 1) * G, c * 128:(c + 1) * 128] = y.astype(o_ref.dtype)

    for step in range(nj + 1):
        @pl.when(j == step)
        def _(step=step):
            blk = step - 1
            src, dst = raws[blk % 2], raws[step % 2]
            for r in range(T // rows_mm):
                if step < nj:
                    project(dst, r)
                if blk < 0:
                    continue
                for g in range(r * rows_mm // G, (r + 1) * rows_mm // G):
                    for c in range(tiles):
                        finish(src, blk, g, c)


class Rows:
    def __init__(self, B, Lc, Ll):
        self.B, self.Lc, self.Ll = B, Lc, Ll
        self.NC = B * Lc
        self.R = B * Lc + B * Ll
        assert self.NC % Ll == 0 or Ll % self.NC == 0
        tm = 1024
        while self.NC % tm or Ll % tm:
            tm //= 2
        self.tm = tm

    def mod_index(self, tm):
        nctx = self.NC // tm
        per = self.Ll // tm
        return lambda i: jnp.where(i < nctx, 0, 1 + (i - nctx) // per)


def _stream_specs(rw, tm, xs, ngrid):
    xc, xl = xs
    nctx = rw.NC // tm
    off = nctx if xl.shape[0] == rw.R else 0
    D = xc.shape[1]
    if ngrid == 1:
        return [pl.BlockSpec((tm, D), lambda i: (jnp.minimum(i, nctx - 1), 0)),
                pl.BlockSpec((tm, D), lambda i: (jnp.maximum(i - nctx, 0) + off, 0))]
    return [pl.BlockSpec((tm, D), lambda i, j: (jnp.minimum(i, nctx - 1), 0)),
            pl.BlockSpec((tm, D), lambda i, j: (jnp.maximum(i - nctx, 0) + off, 0))]


def in_proj(rw, xs, l, nw, mod, w, par):
    R = rw.R
    D = xs[0].shape[1]
    N = w.shape[2]
    tm, tn = rw.tm, IN_TN
    nj = N // tn
    assert N == N_IN_PAD
    mi = rw.mod_index(tm)
    done = lambda j: jnp.maximum(j - 1, 0)
    return pl.pallas_call(
        functools.partial(_in_proj_kernel, nctx_blk=rw.NC // tm),
        grid=(R // tm, nj + 1),
        in_specs=_stream_specs(rw, tm, xs, 2) + [
            pl.BlockSpec((None, 1, D), lambda i, j: (l, 0, 0)),
            pl.BlockSpec((None, 1, 8, D), lambda i, j: (l, mi(i), 0, 0)),
            pl.BlockSpec((None, D, tn), lambda i, j: (l, 0, jnp.minimum(j, nj - 1))),
            pl.BlockSpec((None, 8, tn), lambda i, j: (l, 0, done(j))),
        ],
        out_specs=[pl.BlockSpec((tm, tn), lambda i, j: (i, done(j))),
                   pl.BlockSpec((tm, 128), lambda i, j: (i, 0))],
        out_shape=[jax.ShapeDtypeStruct((R, N), BF16), jax.ShapeDtypeStruct((R, 128), F32)],
        scratch_shapes=[pltpu.VMEM((tm, D), BF16), pltpu.VMEM((tm, tn), F32), pltpu.VMEM((tm, tn), F32)],
        compiler_params=_cp("arbitrary", "arbitrary"),
        name="in_proj",
    )(xs[0], xs[1], nw, mod, w, par)


def _in_proj_params(hy_conv_w, hy_conv_b, ssm_conv_w, ssm_conv_b, gdn_conv_w, ssm_dt_bias, gdn_dt_bias, gdn_A_log):
    depth = hy_conv_w.shape[0]

    def row(pieces):
        out, pos = [], 0
        for off, a in pieces:
            out += [jnp.zeros((depth, off - pos), F32), a.astype(F32)]
            pos = off + a.shape[1]
        return jnp.concatenate(out + [jnp.zeros((depth, N_IN_PAD - pos), F32)], axis=1)
    z4 = jnp.zeros((depth, 4), F32)
    conv = [row([(C_HY, hy_conv_w[:, t]), (C_XBC, ssm_conv_w[:, t]), (C_QKV, gdn_conv_w[:, t])]) for t in range(3)]
    bias = row([(C_HY, hy_conv_b), (C_XBC, ssm_conv_b)])
    l2s = row([(C_QKV, jnp.full((depth, 512), GDN_DK ** -0.5, F32)), (C_QKV + 512, jnp.ones((depth, 512), F32))])
    sbias = row([(C_SM, jnp.concatenate([ssm_dt_bias.reshape(depth, 16), gdn_dt_bias[:, 0], z4,
                                         gdn_dt_bias[:, 1], z4], axis=1))])
    salog = row([(C_SM + 16, jnp.concatenate([gdn_A_log[:, 0], z4, gdn_A_log[:, 1], z4], axis=1))])
    kind = np.full((depth, N_IN_PAD), 3.0, np.float32)
    kind[:, C_SM:C_SM + 16] = 0.0
    kind[:, C_SM + 16:C_SM + 20] = 1.0
    kind[:, C_SM + 24:C_SM + 28] = 1.0
    kind[:, C_SM + 20:C_SM + 24] = 2.0
    kind[:, C_SM + 28:C_SM + 32] = 2.0
    return jnp.stack(conv + [bias, l2s, sbias, salog, jnp.asarray(kind)], axis=1)


def _hy_filter_kernel(z_ref, w1_ref, b1_ref, w2_ref, b2_ref, w3_ref, f0_ref, f1_ref, win_ref, o_ref, h_ref):
    @pl.when(pl.program_id(1) == 0)
    def _():
        h1 = jnp.sin(f0_ref[...] * (_dot(z_ref[...], w1_ref[...], HI) + b1_ref[...]))
        h_ref[...] = jnp.sin(f1_ref[...] * (_dot(h1, w2_ref[...], HI) + b2_ref[...]))

    h = _dot(h_ref[...], w3_ref[...], HI) * win_ref[...]
    tl = h.shape[0]
    row = lax.broadcasted_iota(jnp.int32, (tl, 1), 0) + pl.program_id(0) * tl
    drop = (row == 0) & (pl.program_id(1) % 2 == 1)
    o_ref[...] = jnp.where(drop, 0.0, h).astype(o_ref.dtype)


def hy_features(L):
    t = jnp.linspace(0.0, 1.0, L, dtype=F32)[:, None]
    w = 2.0 * math.pi * jnp.arange(L, dtype=F32)[:, None] / L
    f = jnp.linspace(1e-4, HY_BANDS - 1, HY_BANDS, dtype=F32)[None, :]
    z = jnp.concatenate([t, jnp.cos(f * w), -jnp.sin(f * w)], axis=-1)
    z = jnp.pad(z, ((0, 0), (0, 128 - HY_EMB)))
    min_decay = math.log(HY_TARGET) / HY_LONG_DECAY_PCT
    max_decay = math.log(HY_TARGET) / HY_SHORT_DECAY_PCT
    deltas = jnp.linspace(min_decay, max_decay, HY_WIDTH, dtype=F32)
    window = jnp.exp(-t * jnp.abs(deltas))
    return z, window


def hy_filter(feat, w1, b1, w2, b2, w3, freq):
    z, window = feat
    L = z.shape[0]
    H = HY_HIDDEN
    w1p = jnp.pad(w1, ((0, 128 - HY_EMB), (0, 128 - H)))
    w2p = jnp.pad(w2, ((0, 128 - H), (0, 128 - H)))
    w3p = jnp.pad(w3, ((0, 128 - H), (0, 0)))
    pad1 = lambda v: jnp.pad(v, (0, 128 - H)).reshape(1, 128)
    tl = 256
    full = lambda shape: pl.BlockSpec(shape, lambda i, j: (0, 0))
    return pl.pallas_call(
        _hy_filter_kernel,
        grid=(L // tl, 4),
        in_specs=[
            pl.BlockSpec((tl, 128), lambda i, j: (i, 0)),
            full((128, 128)), full((1, 128)), full((128, 128)), full((1, 128)),
            pl.BlockSpec((128, HY_WIDTH), lambda i, j: (0, j)),
            full((1, 128)), full((1, 128)),
            pl.BlockSpec((tl, HY_WIDTH), lambda i, j: (i, 0)),
        ],
        out_specs=pl.BlockSpec((tl, HY_WIDTH), lambda i, j: (i, j)),
        out_shape=jax.ShapeDtypeStruct((L, 4 * HY_WIDTH), BF16),
        scratch_shapes=[pltpu.VMEM((tl, 128), F32)],
        compiler_params=_cp("arbitrary", "arbitrary"),
        name="hy_filter",
    )(z, w1p, pad1(b1), w2p, pad1(b2), w3p, pad1(freq[0]), pad1(freq[1]), window)


def dft_table(L):
    N = 2 * L
    H = L // 2
    q = np.arange(L, dtype=np.int64)
    f = np.where(q < H, q, L + H - q)[:, None]
    s = np.arange(L, dtype=np.int64)[None, :]
    ang = ((f * s) % N).astype(np.float64) * (2.0 * math.pi / N)
    mid = ((H * s) % N).astype(np.float64) * (2.0 * math.pi / N)
    q = q[:, None]
    imag = np.where(q == 0, np.cos(mid), np.where(q == H, -np.sin(mid), -np.sin(ang)))
    return jnp.asarray(np.concatenate([np.cos(ang), imag], axis=0), dtype=BF16)


def dft_tables_split(L):
    N = 2 * L
    H = L // 2
    q = np.arange(H, dtype=np.int64)[:, None]
    m = np.arange(H, dtype=np.int64)[None, :]
    ang_e = ((q * 2 * m) % N).astype(np.float64) * (2.0 * math.pi / N)
    ang_o = ((q * (2 * m + 1)) % N).astype(np.float64) * (2.0 * math.pi / N)
    alt = (1 - 2 * (m % 2)).astype(np.float64)
    ce, co = np.cos(ang_e), np.cos(ang_o)
    se = np.where(q == 0, alt, -np.sin(ang_e))
    so = np.where(q == 0, -alt, -np.sin(ang_o))
    w = np.where(q == 0, 1.0, 2.0) / N
    ise = np.where(q == 0, 2.0 / N * alt, -np.sin(ang_e) * w)
    iso = np.where(q == 0, -2.0 / N * alt, -np.sin(ang_o) * w)
    fwd = np.stack([ce, co, se, so])
    inv = np.stack([(ce * w).T, ise.T, (co * w).T, iso.T])
    return jnp.asarray(fwd, dtype=BF16), jnp.asarray(inv, dtype=BF16)


def paired_spectrum(kspec, L):
    C = HY_WIDTH
    a = kspec.reshape(2, L, 2, 2, C)
    kr = a[0, :, :, 0] + a[0, :, :, 1]
    sign = jnp.where(jnp.arange(L)[:, None, None] == 0, 1.0, -1.0)
    ki = a[1, :, :, 0] + sign * a[1, :, :, 1]
    return kr.reshape(L, 2 * C), ki.reshape(L, 2 * C)


def _matmul_kernel(a_ref, b_ref, o_ref):
    o_ref[...] = _dot(a_ref[...], b_ref[...])


def matmul(a, b, tm, tn):
    M, K = a.shape
    N = b.shape[1]
    return pl.pallas_call(
        _matmul_kernel,
        grid=(M // tm, N // tn),
        in_specs=[pl.BlockSpec((tm, K), lambda i, j: (i, 0)), pl.BlockSpec((K, tn), lambda i, j: (0, j))],
        out_specs=pl.BlockSpec((tm, tn), lambda i, j: (i, j)),
        out_shape=jax.ShapeDtypeStruct((M, N), F32),
        compiler_params=_cp("arbitrary", "arbitrary"),
        name="matmul",
    )(a, b)


def _long_conv_kernel(u_ref, g_ref, bias_ref, fwd_ref, inv_ref, kr_ref, kr2_ref, ki_ref, ki2_ref, o_ref,
                      ue_ref, uo_ref, acce_ref, acco_ref, y_ref):
    f = pl.program_id(1)
    half = ue_ref.shape[0]

    lane_tiles = [slice(c * 128, (c + 1) * 128) for c in range(y_ref.shape[0])]

    @pl.when(f == 0)
    def _():
        for c, cs in enumerate(lane_tiles):
            y_c = y_ref.at[c]
            y_c[...] = u_ref[:, cs].astype(F32)
            ue_ref[:, cs] = y_c[pl.ds(0, half, stride=2), :].astype(BF16)
            uo_ref[:, cs] = y_c[pl.ds(1, half, stride=2), :].astype(BF16)
        acce_ref[...] = jnp.zeros_like(acce_ref)
        acco_ref[...] = jnp.zeros_like(acco_ref)

    ue, uo = ue_ref[...], uo_ref[...]
    ae, ao = _dot(fwd_ref[0], ue), _dot(fwd_ref[1], uo)
    be, bo = _dot(fwd_ref[2], ue), _dot(fwd_ref[3], uo)
    ur, ur2 = ae + ao, ae - ao
    ui, ui2 = be + bo, bo - be
    kr, kr2, ki, ki2 = kr_ref[...], kr2_ref[...], ki_ref[...], ki2_ref[...]
    pr, pi = ur * kr - ui * ki, ur * ki + ui * kr
    pr2, pi2 = ur2 * kr2 - ui2 * ki2, ur2 * ki2 + ui2 * kr2
    first = (lax.broadcasted_iota(jnp.int32, (fwd_ref.shape[1], 1), 0) == 0) & (f == 0)
    dc, ny = ur * kr, ur2 * kr2
    gr = jnp.where(first, dc + ny, pr + pr2)
    gi = jnp.where(first, be * ki - bo * ki2, pi - pi2)
    hr = jnp.where(first, dc - ny, pr - pr2)
    hi = jnp.where(first, be * ki2 + bo * ki, pi + pi2)
    acce_ref[...] += _dot(inv_ref[0], gr.astype(BF16)) + _dot(inv_ref[1], gi.astype(BF16))
    acco_ref[...] += _dot(inv_ref[2], hr.astype(BF16)) + _dot(inv_ref[3], hi.astype(BF16))

    @pl.when(f == pl.num_programs(1) - 1)
    def _():
        for c, cs in enumerate(lane_tiles):
            y_c = y_ref.at[c]
            y_c[pl.ds(0, half, stride=2), :] = acce_ref[:, cs]
            y_c[pl.ds(1, half, stride=2), :] = acco_ref[:, cs]
            u = u_ref[:, cs].astype(F32)
            o_ref[:, cs] = (g_ref[:, cs].astype(F32) * (y_c[...] + u * bias_ref[:, cs])).astype(o_ref.dtype)


def long_conv(B, L, u, u_rb0, u_cb, gate, g_rb0, gate_cb, bias, fwd, inv, kr, ki, order, out_dtype):
    C = HY_WIDTH
    H = L // 2
    FB = min(FREQ_BLK, H)
    nfb = H // FB
    kspec_lo = pl.BlockSpec((FB, C), lambda b, f: (f, order))
    kspec_hi = pl.BlockSpec((FB, C), lambda b, f: (nfb + f, order))
    return pl.pallas_call(
        _long_conv_kernel,
        grid=(B, nfb),
        in_specs=[
            pl.BlockSpec((L, C), lambda b, f: (u_rb0 + b, u_cb)),
            pl.BlockSpec((L, C), lambda b, f: (g_rb0 + b, gate_cb)),
            pl.BlockSpec((1, C), lambda b, f: (0, 0)),
            pl.BlockSpec((4, FB, H), lambda b, f: (0, f, 0)),
            pl.BlockSpec((4, H, FB), lambda b, f: (0, 0, f)),
            kspec_lo, kspec_hi, kspec_lo, kspec_hi,
        ],
        out_specs=pl.BlockSpec((L, C), lambda b, f: (b, 0)),
        out_shape=jax.ShapeDtypeStruct((B * L, C), out_dtype),
        scratch_shapes=[pltpu.VMEM((H, C), BF16), pltpu.VMEM((H, C), BF16),
                        pltpu.VMEM((H, C), F32), pltpu.VMEM((H, C), F32), pltpu.VMEM((C // 128, L, 128), F32)],
        compiler_params=_cp("arbitrary", "arbitrary"),
        name="long_conv",
    )(u, gate, bias.reshape(1, C), fwd, inv, kr, kr, ki, ki)


def _scan_blocks(rw, rows):
    nbc, nbl, base = rw.Lc // rows, rw.Ll // rows, rw.NC // rows

    def make(d):
        def f(b, s):
            jc = s if d == 0 else nbc - 1 - s
            jl = (s - nbc) if d == 0 else nbl - 1 - (s - nbc)
            return jnp.where(s < nbc, b * nbc + jc, base + b * nbl + jl)
        return f

    return [make(0), make(1)], nbc + nbl


def _expand_lanes(x, base, n, width):
    rows = x.shape[0]
    per = 128 // width
    lane = lax.broadcasted_iota(jnp.int32, (rows, 128), 1)
    tiles = []
    for t in range(n // per):
        c0 = base + t * per
        tile = jnp.broadcast_to(x[:, c0:c0 + 1], (rows, 128))
        for i in range(1, per):
            tile = jnp.where(lane >= i * width, jnp.broadcast_to(x[:, c0 + i:c0 + i + 1], (rows, 128)), tile)
        tiles.append(tile)
    return jnp.concatenate(tiles, axis=1)


def _ssd_kernel(xf, bf, cf, smf, dtf, xb, bb, cb_, smb, dtb, alx_ref, alc_ref, of_ref, ob_ref, h_ref):
    Q = SSM_CHUNK
    GW = SSM_GW

    @pl.when(pl.program_id(1) == 0)
    def _():
        h_ref[...] = jnp.zeros_like(h_ref)

    row = lax.broadcasted_iota(jnp.int32, (Q, Q), 0)
    col = lax.broadcasted_iota(jnp.int32, (Q, Q), 1)
    lane_head = lax.broadcasted_iota(jnp.int32, (Q, GW), 1) // SSM_HEAD_DIM
    dirs = ((xf, bf, cf, smf, dtf, of_ref), (xb, bb, cb_, smb, dtb, ob_ref))
    jobs = []
    for d in range(2):
        x_ref, b_ref, c_ref, sm_ref, dt_ref, o_ref = dirs[d]
        keep = (col <= row) if d == 0 else (col >= row)
        tri = keep.astype(BF16)
        tri_t = ((row <= col) if d == 0 else (row >= col)).astype(BF16)
        sm = sm_ref[...]
        a_x = -jnp.exp(alx_ref[d])
        dtx = _expand_lanes(sm, 8 * d, SSM_HEADS, SSM_HEAD_DIM)
        cumx = _expand_lanes(_dot_01_lhs(tri, sm), 8 * d, SSM_HEADS, SSM_HEAD_DIM) * a_x
        cumr = _dot_01_rhs(dt_ref[0], tri_t) * (-jnp.exp(alc_ref[d]))
        last = Q - 1 if d == 0 else 0
        totx = cumx[last:last + 1, :]
        xd = x_ref[...].astype(F32) * dtx
        xdw = xd * jnp.exp(totx - cumx)
        ecum = jnp.exp(cumx)
        for g in range(SSM_GROUPS):
            gs = slice(g * GW, (g + 1) * GW)
            jobs.append(dict(d=d, g=g, gs=gs, keep=keep, cumx=cumx, cumr=cumr, o_ref=o_ref,
                             bg=b_ref[:, g * SSM_STATE:(g + 1) * SSM_STATE].astype(BF16),
                             cg=c_ref[:, g * SSM_STATE:(g + 1) * SSM_STATE].astype(BF16),
                             xdg=xd[:, gs], xdw=xdw[:, gs].astype(BF16), ecum=ecum[:, gs],
                             etot=jnp.exp(totx[:, gs])))
    for j in jobs:
        j["cb"] = _dot_nt(j["cg"], j["bg"])
        j["h"] = h_ref[j["d"], j["g"]]
    for j in jobs:
        ms, xs = [], []
        for e4 in range(SSM_HPG):
            e = j["g"] * SSM_HPG + e4
            diff = j["cumx"][:, e * SSM_HEAD_DIM:e * SSM_HEAD_DIM + 1] - j["cumr"][e:e + 1, :]
            ms.append((j["cb"] * jnp.where(j["keep"], jnp.exp(diff), 0.0)).astype(BF16))
            xs.append(jnp.where(lane_head == e4, j["xdg"], 0.0).astype(BF16))
        yd = _dot(jnp.concatenate(ms, axis=1), jnp.concatenate(xs, axis=0))
        y_off = _dot(j["cg"], j["h"].astype(BF16)) * j["ecum"]
        j["o_ref"][:, j["gs"]] = (yd + y_off).astype(BF16)
    for j in jobs:
        h_ref[j["d"], j["g"]] = j["h"] * j["etot"] + _dot_tn(j["bg"], j["xdw"])


def ssd_scan(rw, p, sm, dtT, alx, alc):
    Q = SSM_CHUNK
    blks, nsteps = _scan_blocks(rw, Q)
    R = p.shape[0]
    in_specs = []
    for d in range(2):
        f = blks[d]
        in_specs += [
            pl.BlockSpec((Q, 512), lambda b, s, f=f: (f(b, s), C_XBC // 512)),
            pl.BlockSpec((Q, 256), lambda b, s, f=f: (f(b, s), C_XBC // 256 + 2)),
            pl.BlockSpec((Q, 256), lambda b, s, f=f: (f(b, s), C_XBC // 256 + 3)),
            pl.BlockSpec((Q, 128), lambda b, s, f=f: (f(b, s), 0)),
            pl.BlockSpec((1, 8, Q), lambda b, s, f=f, d=d: (d, 0, f(b, s))),
        ]
    in_specs += [pl.BlockSpec((2, 1, 512), lambda b, s: (0, 0, 0)), pl.BlockSpec((2, 8, 1), lambda b, s: (0, 0, 0))]
    ops = (p, p, p, sm, dtT)
    return pl.pallas_call(
        _ssd_kernel,
        grid=(rw.B, nsteps),
        in_specs=in_specs,
        out_specs=[pl.BlockSpec((Q, 512), lambda b, s, f=blks[d]: (f(b, s), 0)) for d in range(2)],
        out_shape=[jax.ShapeDtypeStruct((R, 512), BF16)] * 2,
        scratch_shapes=[pltpu.VMEM((2, SSM_GROUPS, SSM_STATE, SSM_GW), F32)],
        compiler_params=_cp("arbitrary", "arbitrary"),
        name="ssd_scan",
    )(*ops, *ops, alx, alc)


def _split3(x):
    x1 = x.astype(BF16)
    r = x - x1.astype(F32)
    x2 = r.astype(BF16)
    x3 = (r - x2.astype(F32)).astype(BF16)
    return x1, x2, x3


def _dot_01_lhs(m01, x):
    x1, x2, x3 = _split3(x)
    return _dot(m01, x1) + _dot(m01, x2) + _dot(m01, x3)


def _dot_01_rhs(x, m01):
    x1, x2, x3 = _split3(x)
    return _dot(x1, m01) + _dot(x2, m01) + _dot(x3, m01)


GDN_ROWS = 256


def _gdn_prep_kernel(q_ref, k_ref, v_ref, sm_ref, gT_ref, u_ref, w_ref, qg_ref, kd_ref, qk_ref, egl_ref):
    C = GDN_CHUNK
    row = lax.broadcasted_iota(jnp.int32, (C, C), 0)
    col = lax.broadcasted_iota(jnp.int32, (C, C), 1)
    jobs = []
    levels = []
    for d in range(2):
        keep = (col <= row) if d == 0 else (col >= row)
        late, early = (row, col) if d == 0 else (col, row)
        levels.append([(((row ^ col) >> (j + 1)) == 0) & ((late & (1 << j)) != 0) & ((early & (1 << j)) == 0)
                       for j in range(6)])
        tri = keep.astype(BF16)
        tri_t = ((row <= col) if d == 0 else (row >= col)).astype(BF16)
        last = C - 1 if d == 0 else 0
        for c in range(GDN_ROWS // C):
            rows = slice(c * C, (c + 1) * C)
            smc = sm_ref[rows, :]
            cums = _dot_01_lhs(tri, smc)
            cumr = _dot_01_rhs(gT_ref[c, 8 * d:8 * d + 8, :], tri_t)
            tot = cums[last:last + 1, :]
            for h in range(GDN_HEADS):
                lg = 16 + 8 * d + h
                jobs.append(dict(d=d, c=c, h=h, rows=rows, hs=slice(h * 128, (h + 1) * 128), keep=keep,
                                 gc=cums[:, lg:lg + 1], beta=smc[:, lg + 4:lg + 5],
                                 gl=tot[:, lg:lg + 1], gr=cumr[h:h + 1, :]))
    for j in jobs:
        q = q_ref[j["rows"], j["hs"]].astype(F32)
        k = k_ref[j["rows"], j["hs"]].astype(F32)
        j["dec"] = jnp.where(j["keep"], jnp.exp(j["gc"] - j["gr"]), 0.0)
        kb = k * j["beta"]
        both = _dot_nt(jnp.concatenate([kb, q], axis=0).astype(BF16), k.astype(BF16))
        j["a"] = both[:C] * j["dec"]
        j["n"] = -jnp.where(levels[j["d"]][0], j["a"], 0.0)
        qk_ref[j["d"], j["c"], j["h"]] = (both[C:] * j["dec"]).astype(BF16)
    for lev in range(1, 6):
        for j in jobs:
            l = jnp.where(levels[j["d"]][lev], j["a"], 0.0)
            j["y"] = l + _dot(l.astype(BF16), j["n"].astype(BF16))
        for j in jobs:
            j["n"] = j["n"] - j["y"] - _dot(j["n"].astype(BF16), j["y"].astype(BF16))
    for j in jobs:
        d, rows, hs, gc, gl, beta = j["d"], j["rows"], j["hs"], j["gc"], j["gl"], j["beta"]
        q = q_ref[rows, hs].astype(F32)
        k = k_ref[rows, hs].astype(F32)
        eg = jnp.exp(gc)
        rhs = jnp.concatenate([v_ref[rows, hs].astype(F32) * beta, k * beta * eg], axis=1)
        sol = rhs + _dot(j["n"].astype(BF16), rhs.astype(BF16))
        u_ref[d, rows, hs] = sol[:, :GDN_DV].astype(BF16)
        w_ref[d, rows, hs] = sol[:, GDN_DV:].astype(BF16)
        qg_ref[d, rows, hs] = (q * eg).astype(BF16)
        kd_ref[d, rows, hs] = (k * jnp.exp(gl - gc)).astype(BF16)
        egl_ref[d, j["c"], :, hs] = jnp.broadcast_to(jnp.exp(gl), (8, 128))


def gdn_prep(p, sm, gT):
    R = p.shape[0]
    T, C = GDN_ROWS, GDN_CHUNK
    nc = T // C
    col = lambda k: pl.BlockSpec((T, 512), lambda i: (i, C_QKV // 512 + k))
    dirrow = pl.BlockSpec((2, T, 512), lambda i: (0, i, 0))
    return pl.pallas_call(
        _gdn_prep_kernel,
        grid=(R // T,),
        in_specs=[col(0), col(1), col(2),
                  pl.BlockSpec((T, 128), lambda i: (i, 0)),
                  pl.BlockSpec((nc, 16, C), lambda i: (i, 0, 0))],
        out_specs=[dirrow, dirrow, dirrow, dirrow,
                   pl.BlockSpec((2, nc, GDN_HEADS, C, C), lambda i: (0, i, 0, 0, 0)),
                   pl.BlockSpec((2, nc, 8, 512), lambda i: (0, i, 0, 0))],
        out_shape=[jax.ShapeDtypeStruct((2, R, 512), BF16),
                   jax.ShapeDtypeStruct((2, R, 512), BF16),
                   jax.ShapeDtypeStruct((2, R, 512), BF16),
                   jax.ShapeDtypeStruct((2, R, 512), BF16),
                   jax.ShapeDtypeStruct((2, R // C, GDN_HEADS, C, C), BF16),
                   jax.ShapeDtypeStruct((2, R // C, 8, 512), F32)],
        compiler_params=_cp("arbitrary"),
        name="gdn_prep",
    )(p, p, p, sm, gT)


def _gdn_scan_kernel(uf, wf, qgf, kdf, qkf, eglf, ub, wb, qgb, kdb, qkb, eglb, of_ref, ob_ref, s_ref):
    C = GDN_CHUNK
    nch = GDN_ROWS // C

    @pl.when(pl.program_id(1) == 0)
    def _():
        s_ref[...] = jnp.zeros_like(s_ref)

    dirs = ((uf, wf, qgf, kdf, qkf, eglf, of_ref), (ub, wb, qgb, kdb, qkb, eglb, ob_ref))
    chains = [(d, h) for d in range(2) for h in range(GDN_HEADS)]
    S = {ch: s_ref[ch[0], ch[1]] for ch in chains}
    for i in range(nch):
        Sb, vnb, rows_of, c_of = {}, {}, {}, {}
        for d, h in chains:
            c_of[d] = i if d == 0 else nch - 1 - i
            rows_of[d] = slice(c_of[d] * C, (c_of[d] + 1) * C)
        for d, h in chains:
            hs = slice(h * 128, (h + 1) * 128)
            Sb[d, h] = S[d, h].astype(BF16)
            v_new = dirs[d][0][0, rows_of[d], hs].astype(F32) - _dot(dirs[d][1][0, rows_of[d], hs], Sb[d, h])
            vnb[d, h] = v_new.astype(BF16)
        for d, h in chains:
            hs = slice(h * 128, (h + 1) * 128)
            u_ref, w_ref, qg_ref, kd_ref, qk_ref, egl_ref, o_ref = dirs[d]
            S[d, h] = S[d, h] * egl_ref[0, c_of[d], 0:1, hs] + _dot_tn(kd_ref[0, rows_of[d], hs], vnb[d, h])
        for d, h in chains:
            hs = slice(h * 128, (h + 1) * 128)
            u_ref, w_ref, qg_ref, kd_ref, qk_ref, egl_ref, o_ref = dirs[d]
            o_ref[rows_of[d], hs] = (_dot(qg_ref[0, rows_of[d], hs], Sb[d, h])
                                     + _dot(qk_ref[0, c_of[d], h], vnb[d, h])).astype(BF16)
    for ch in chains:
        s_ref[ch[0], ch[1]] = S[ch]


def gdn_scan(rw, u, w, qg, kd, qk, egl):
    T, C = GDN_ROWS, GDN_CHUNK
    nc = T // C
    R = u.shape[1]
    nbc, nbl, base = rw.Lc // T, rw.Ll // T, rw.NC // T

    def blk(d):
        def f(b, s):
            jc = s if d == 0 else nbc - 1 - s
            jl = (s - nbc) if d == 0 else nbl - 1 - (s - nbc)
            return jnp.where(s < nbc, b * nbc + jc, base + b * nbl + jl)
        return f

    in_specs = []
    for d in range(2):
        f = blk(d)
        rowspec = pl.BlockSpec((1, T, 512), lambda b, s, f=f, d=d: (d, f(b, s), 0))
        in_specs += [rowspec, rowspec, rowspec, rowspec,
                     pl.BlockSpec((1, nc, GDN_HEADS, C, C), lambda b, s, f=f, d=d: (d, f(b, s), 0, 0, 0)),
                     pl.BlockSpec((1, nc, 8, 512), lambda b, s, f=f, d=d: (d, f(b, s), 0, 0))]
    out_specs = [pl.BlockSpec((T, 512), lambda b, s, f=blk(d): (f(b, s), 0)) for d in range(2)]
    ops = (u, w, qg, kd, qk, egl)
    return pl.pallas_call(
        _gdn_scan_kernel,
        grid=(rw.B, nbc + nbl),
        in_specs=in_specs,
        out_specs=out_specs,
        out_shape=[jax.ShapeDtypeStruct((R, 512), BF16)] * 2,
        scratch_shapes=[pltpu.VMEM((2, GDN_HEADS, GDN_DK, GDN_DV), F32)],
        compiler_params=_cp("arbitrary", "arbitrary"),
        name="gdn_scan",
    )(*ops, *ops)


def _merge_kernel(yhc_ref, yhl_ref, sf_ref, sb_ref, sx_ref, sz_ref, dx_ref, snw_ref, gf_ref, gb_ref, gg_ref, gnw_ref,
                  g0_ref, g1_ref, g2_ref, w0_ref, w1_ref, w2_ref, wo_ref, xc_ref, xl_ref, mod_ref, o_ref,
                  ys_ref, yg_ref, *, nctx_blk):
    tm = xc_ref.shape[0]
    rp = 64
    for r in range(tm // rp):
        rs = slice(r * rp, (r + 1) * rp)
        y = (sf_ref[rs, :].astype(F32) + sb_ref[rs, :].astype(F32)
             + sx_ref[rs, :].astype(F32) * dx_ref[...])
        y = y * _silu(sz_ref[rs, :].astype(F32))
        parts = []
        for g in range(SSM_GROUPS):
            yg = y[:, g * SSM_GW:(g + 1) * SSM_GW]
            parts.append(yg * lax.rsqrt(jnp.mean(yg * yg, axis=-1, keepdims=True) + EPS))
        ys_ref[rs, :] = (jnp.concatenate(parts, axis=1) * snw_ref[...]).astype(BF16)
        o = gf_ref[rs, :].astype(F32) + gb_ref[rs, :].astype(F32)
        parts = []
        for h in range(GDN_HEADS):
            oh = o[:, h * 128:(h + 1) * 128]
            parts.append(oh * lax.rsqrt(jnp.mean(oh * oh, axis=-1, keepdims=True) + EPS))
        yg_ref[rs, :] = (jnp.concatenate(parts, axis=1) * gnw_ref[...]
                         * _silu(gg_ref[rs, :].astype(F32))).astype(BF16)
    is_ctx = pl.program_id(0) < nctx_blk
    yh = jnp.where(is_ctx, yhc_ref[...], yhl_ref[...])
    m = (_sigmoid(g0_ref[...].astype(F32)) * _dot(yh, w0_ref[...])
         + _sigmoid(g1_ref[...].astype(F32)) * _dot(ys_ref[...], w1_ref[...])
         + _sigmoid(g2_ref[...].astype(F32)) * _dot(yg_ref[...], w2_ref[...]))
    x = jnp.where(is_ctx, xc_ref[...], xl_ref[...])
    o_ref[...] = x + mod_ref[0, 2:3, :] * _dot(m.astype(BF16), wo_ref[...])


def merge(rw, l, yh, y_f, y_b, dx, ssm_nw, o_f, o_b, gdn_nw, p, w0, w1, w2, wo, xs, mod):
    R = rw.R
    D = xs[0].shape[1]
    tm = min(rw.tm, 512)
    mi = rw.mod_index(tm)
    yspec = pl.BlockSpec((tm, 512), lambda i: (i, 0))
    pspec = lambda col: pl.BlockSpec((tm, 512), lambda i: (i, col // 512))
    vec = pl.BlockSpec((1, 512), lambda i: (0, 0))
    gspec = lambda k: pl.BlockSpec((tm, D), lambda i: (i, C_GATE // D + k))
    wspec = pl.BlockSpec((None, 512, D), lambda i: (l, 0, 0))
    return pl.pallas_call(
        functools.partial(_merge_kernel, nctx_blk=rw.NC // tm),
        grid=(R // tm,),
        in_specs=_stream_specs(rw, tm, yh, 1) + [
                  yspec, yspec, pspec(C_XBC), pspec(C_Z), vec, vec,
                  yspec, yspec, pspec(C_GG), vec,
                  gspec(0), gspec(1), gspec(2), wspec, wspec, wspec,
                  pl.BlockSpec((None, D, D), lambda i: (l, 0, 0))]
                 + _stream_specs(rw, tm, xs, 1)
                 + [pl.BlockSpec((None, 1, 8, D), lambda i: (l, mi(i), 0, 0))],
        out_specs=pl.BlockSpec((tm, D), lambda i: (i, 0)),
        out_shape=jax.ShapeDtypeStruct((R, D), F32),
        scratch_shapes=[pltpu.VMEM((tm, 512), BF16), pltpu.VMEM((tm, 512), BF16)],
        compiler_params=_cp("arbitrary"),
        name="merge",
    )(yh[0], yh[1], y_f, y_b, p, p, dx, ssm_nw, o_f, o_b, p, gdn_nw, p, p, p, w0, w1, w2, wo, xs[0], xs[1], mod)


def _swiglu_up_kernel(x_ref, nw_ref, mod_ref, wg_ref, wu_ref, o_ref, h_ref, g0_ref, g1_ref, u0_ref, u1_ref):
    @pl.when(pl.program_id(1) == 0)
    def _():
        h = _norm_mod(x_ref[...], nw_ref[...], mod_ref[0, 4:5, :], mod_ref[0, 3:4, :])
        h_ref[...] = h.astype(BF16)

    T, tn = o_ref.shape
    rows = g0_ref.shape[0]
    gs, us = (g0_ref, g1_ref), (u0_ref, u1_ref)

    def project(r):
        hh = h_ref[r * rows:(r + 1) * rows, :]
        gs[r % 2][...] = _dot(hh, wg_ref[...])
        us[r % 2][...] = _dot(hh, wu_ref[...])

    def finish(r):
        for q in range(rows // 64):
            for c in range(tn // 128):
                ps = (slice(q * 64, (q + 1) * 64), slice(c * 128, (c + 1) * 128))
                y = _silu(gs[r % 2][ps]) * us[r % 2][ps]
                o_ref[r * rows + q * 64:r * rows + (q + 1) * 64, ps[1]] = y.astype(o_ref.dtype)

    for r in range(T // rows):
        project(r)
        if r > 0:
            finish(r - 1)
    finish(T // rows - 1)


def swiglu_up(rw, l, x, nw, mod, wgu):
    R, D = x.shape
    tm = rw.tm
    tn = D_FF // 2
    nj = D_FF // tn
    mi = rw.mod_index(tm)
    return pl.pallas_call(
        _swiglu_up_kernel,
        grid=(R // tm, nj),
        in_specs=[
            pl.BlockSpec((tm, D), lambda i, j: (i, 0)),
            pl.BlockSpec((None, 1, D), lambda i, j: (l, 0, 0)),
            pl.BlockSpec((None, 1, 8, D), lambda i, j: (l, mi(i), 0, 0)),
            pl.BlockSpec((None, D, tn), lambda i, j: (l, 0, j)),
            pl.BlockSpec((None, D, tn), lambda i, j: (l, 0, nj + j)),
        ],
        out_specs=pl.BlockSpec((tm, tn), lambda i, j: (i, j)),
        out_shape=jax.ShapeDtypeStruct((R, D_FF), BF16),
        scratch_shapes=[pltpu.VMEM((tm, D), BF16)] + [pltpu.VMEM((min(256, tm), tn), F32)] * 4,
        compiler_params=_cp("arbitrary", "arbitrary"),
        name="swiglu_up",
    )(x, nw, mod, wgu, wgu)


def _swiglu_down_kernel(a_ref, w_ref, x_ref, mod_ref, o_ref):
    o_ref[...] = x_ref[...] + mod_ref[0, 5:6, :] * _dot(a_ref[...], w_ref[...])


def swiglu_down(rw, l, a, w, x, mod):
    R, D = x.shape
    tm = min(rw.tm, 512)
    mi = rw.mod_index(tm)
    return pl.pallas_call(
        _swiglu_down_kernel,
        grid=(R // tm,),
        in_specs=[
            pl.BlockSpec((tm, D_FF), lambda i: (i, 0)),
            pl.BlockSpec((None, D_FF, D), lambda i: (l, 0, 0)),
            pl.BlockSpec((tm, D), lambda i: (i, 0)),
            pl.BlockSpec((None, 1, 8, D), lambda i: (l, mi(i), 0, 0)),
        ],
        out_specs=pl.BlockSpec((tm, D), lambda i: (i, 0)),
        out_shape=jax.ShapeDtypeStruct((R, D), F32),
        compiler_params=_cp("arbitrary"),
        name="swiglu_down",
    )(a, w, x, mod)


def _final_norm_kernel(x_ref, w_ref, o_ref):
    x = x_ref[...]
    ms = jnp.mean(x * x, axis=-1, keepdims=True)
    o_ref[...] = x * lax.rsqrt(ms + EPS) * w_ref[...]


def final_norm(rw, x, w):
    D = x.shape[1]
    tm = rw.tm
    n0 = rw.NC // tm
    nl = rw.B * rw.Ll
    return pl.pallas_call(
        _final_norm_kernel,
        grid=(nl // tm,),
        in_specs=[pl.BlockSpec((tm, D), lambda i: (n0 + i, 0)), pl.BlockSpec((1, D), lambda i: (0, 0))],
        out_specs=pl.BlockSpec((tm, D), lambda i: (i, 0)),
        out_shape=jax.ShapeDtypeStruct((nl, D), F32),
        compiler_params=_cp("arbitrary"),
        name="final_norm",
    )(x, w.reshape(1, D))


def _regroup_w_in(w_in):
    o_dt = 3072
    o_gdn = 3088
    o_a = o_gdn + 2048
    o_b = o_a + 8
    o_gate = o_gdn + 2064
    wt = jnp.swapaxes(w_in, 1, 2).astype(BF16)
    pieces = [
        wt[:, 0:3072],
        wt[:, o_gdn:o_gdn + 2048],
        wt[:, o_gate:o_gate + 3072],
        wt[:, o_dt:o_dt + 16],
        wt[:, o_a:o_a + 4], wt[:, o_b:o_b + 4],
        wt[:, o_a + 4:o_a + 8], wt[:, o_b + 4:o_b + 8],
        jnp.zeros((wt.shape[0], N_IN_PAD - C_SM - 32, wt.shape[2]), wt.dtype),
    ]
    return jnp.swapaxes(jnp.concatenate(pieces, axis=1), 1, 2)


def kernel(x, c, ctx, c_ctx, w_ada, b_ada, norm1_w, norm2_w, w_in, hy_conv_w, hy_conv_b, hy_w1, hy_b1, hy_w2, hy_b2, hy_w3, hy_freq, hy_bias, ssm_conv_w, ssm_conv_b, ssm_dt_bias, ssm_A_log, ssm_D, ssm_norm_w, gdn_conv_w, gdn_dt_bias, gdn_A_log, gdn_norm_w, w_hy_out, w_ssm_out, w_gdn_out, w_out, w_gate_up, w_down, final_norm_w):
    B, Ll, D = x.shape
    Lc = ctx.shape[1]
    depth = w_ada.shape[0]
    assert Lc == CONV_ROWS and D == D_MODEL and B <= 15
    rw = Rows(B, Lc, Ll)
    R, NC = rw.R, rw.NC

    xs = (ctx.reshape(B * Lc, D), x.reshape(B * Ll, D))

    svec = jnp.concatenate([c_ctx[None, :], c, jnp.zeros((15 - B, D), F32)], axis=0)
    mod = ada_modulation(svec, w_ada, b_ada)
    mod = jnp.pad(mod.reshape(depth, 16, 6, D), ((0, 0), (0, 0), (0, 2), (0, 0)))

    w_in_r = _regroup_w_in(w_in)
    par = _in_proj_params(hy_conv_w, hy_conv_b, ssm_conv_w, ssm_conv_b, gdn_conv_w, ssm_dt_bias, gdn_dt_bias,
                          gdn_A_log)
    norm1 = norm1_w.reshape(depth, 1, D)
    norm2 = norm2_w.reshape(depth, 1, D)
    w_hy_o, w_ssm_o, w_gdn_o, w_o = (w.astype(BF16) for w in (w_hy_out, w_ssm_out, w_gdn_out, w_out))
    w_gu, w_dn = w_gate_up.astype(BF16), w_down.astype(BF16)
    dft_l = (dft_table(Ll),) + dft_tables_split(Ll)
    dft_c = (dft_table(Lc),) + dft_tables_split(Lc)
    feat_l, feat_c = hy_features(Ll), hy_features(Lc)

    for l in range(depth):
        p, sm = in_proj(rw, xs, l, norm1, mod, w_in_r, par)

        sm32_t = sm[:, :32].T
        dt_t = sm32_t[:16].reshape(2, 8, R)
        g_t = sm32_t[16:32].reshape(16, R // GDN_CHUNK, GDN_CHUNK).transpose(1, 0, 2)

        alx = jnp.repeat(ssm_A_log[l], SSM_HEAD_DIM, axis=-1).reshape(2, 1, 512)
        alc = ssm_A_log[l].reshape(2, 8, 1)
        y_f, y_b = ssd_scan(rw, p, sm, dt_t, alx, alc)
        dx = jnp.repeat(ssm_D[l], SSM_HEAD_DIM).reshape(1, 512)

        o_f, o_b = gdn_scan(rw, *gdn_prep(p, sm, g_t))

        hyu = p
        parts = []
        for (Bn, L, blk0, (full, fwd, inv), feat) in ((B, Lc, 0, dft_c, feat_c),
                                                      (B, Ll, NC // Ll, dft_l, feat_l)):
            if NC % L:
                raise ValueError("latent length must divide the context row count")
            filt = hy_filter(feat, hy_w1[l], hy_b1[l], hy_w2[l], hy_b2[l], hy_w3[l], hy_freq[l])
            kr, ki = paired_spectrum(matmul(full, filt, min(512, 2 * L), 512), L)
            z1 = long_conv(Bn, L, hyu, blk0, 0, hyu, blk0, 1, hy_bias[l, 0], fwd, inv, kr, ki, 0, F32)
            yy = long_conv(Bn, L, z1, 0, 0, hyu, blk0, 2, hy_bias[l, 1], fwd, inv, kr, ki, 1, BF16)
            parts.append(yy)
        y_hy = tuple(parts)

        xa = merge(rw, l, y_hy, y_f, y_b, dx, ssm_norm_w[l].reshape(1, 512),
                   o_f, o_b, jnp.tile(gdn_norm_w[l], GDN_HEADS).reshape(1, 512), p,
                   w_hy_o, w_ssm_o, w_gdn_o, w_o, xs, mod)
        act = swiglu_up(rw, l, xa, norm2, mod, w_gu)
        xa = swiglu_down(rw, l, act, w_dn, xa, mod)
        xs = (xa, xa)

    out = final_norm(rw, xa, final_norm_w)
    return out.reshape(B, Ll, D)
```

```python
import functools
import math

import jax
import jax.numpy as jnp
import numpy as np
from jax import lax
from jax.experimental import pallas as pl
from jax.experimental.pallas import tpu as pltpu

F32 = jnp.float32
BF16 = jnp.bfloat16
HI = lax.Precision.HIGHEST

EPS = 1e-6
D_MODEL = 1024
GRID_W = 64

HY_WIDTH = 512
HY_BANDS = 16
HY_EMB = 1 + 2 * HY_BANDS
HY_HIDDEN = 64
HY_SHORT_DECAY_PCT = 0.3
HY_LONG_DECAY_PCT = 1.5
HY_TARGET = 1e-2

SSM_HEADS = 8
SSM_HEAD_DIM = 64
SSM_WIDTH = 512
SSM_GROUPS = 2
SSM_HPG = 4
SSM_STATE = 128
SSM_CHUNK = 128
SSM_GW = SSM_HPG * SSM_HEAD_DIM

GDN_HEADS = 4
GDN_DK = 128
GDN_DV = 128
GDN_CHUNK = 64

D_FF = 2816

C_HY = 0
C_Z = 1536
C_XBC = 2048
C_QKV = 3072
C_GG = 4608
C_GATE = 5120
C_SM = 8192

CONV_ROWS = 256
FREQ_BLK = 256

VMEM_LIMIT = 56 * 1024 * 1024


def _cp(*sem, flags=None):
    return pltpu.CompilerParams(dimension_semantics=sem, vmem_limit_bytes=VMEM_LIMIT, flags=flags)


def _sigmoid(x):
    return 1.0 / (1.0 + jnp.exp(-x))


def _silu(x):
    return x * _sigmoid(x)


def _softplus(x):
    return jnp.maximum(x, 0.0) + jnp.log1p(jnp.exp(-jnp.abs(x)))


def _dot(a, b, precision=None):
    return jnp.dot(a, b, precision=precision, preferred_element_type=F32)


def _dot_nt(a, b):
    return lax.dot_general(a, b, (((1,), (1,)), ((), ())), preferred_element_type=F32)


def _dot_tn(a, b):
    return lax.dot_general(a, b, (((0,), (0,)), ((), ())), preferred_element_type=F32)


def _ada_kernel(s_ref, w_ref, b_ref, o_ref):
    s = _silu(s_ref[...])
    o_ref[0] = _dot(s, w_ref[0], HI) + b_ref[0]


def ada_modulation(svec, w_ada, b_ada):
    depth = w_ada.shape[0]
    D = D_MODEL
    return pl.pallas_call(
        _ada_kernel,
        grid=(depth, 6),
        in_specs=[
            pl.BlockSpec((16, D), lambda l, j: (0, 0)),
            pl.BlockSpec((1, D, D), lambda l, j: (l, 0, j)),
            pl.BlockSpec((1, 1, D), lambda l, j: (l, 0, j)),
        ],
        out_specs=pl.BlockSpec((1, 16, D), lambda l, j: (l, 0, j)),
        out_shape=jax.ShapeDtypeStruct((depth, 16, 6 * D), F32),
        compiler_params=_cp("arbitrary", "arbitrary"),
        name="ada",
    )(svec, w_ada, b_ada.reshape(depth, 1, 6 * D))


def _norm_mod(x, nw, scale, shift):
    ms = jnp.mean(x * x, axis=-1, keepdims=True)
    return (x * lax.rsqrt(ms + EPS) * nw) * (1.0 + scale) + shift


N_IN_PAD = C_SM + 128
IN_TN = N_IN_PAD // 5
MODE_RAW, MODE_CONV, MODE_CONV_SILU, MODE_CONV_SILU_L2, MODE_SMALL = range(5)


def _tile_mode(tile):
    col = tile * 128
    if col < C_Z:
        return MODE_CONV
    if col < C_XBC:
        return MODE_RAW
    if col < C_QKV:
        return MODE_CONV_SILU
    if col < C_QKV + 1024:
        return MODE_CONV_SILU_L2
    if col < C_GG:
        return MODE_CONV_SILU
    if col < C_SM:
        return MODE_RAW
    return MODE_SMALL
PAR_W0, PAR_W1, PAR_W2, PAR_BIAS, PAR_L2SCALE, PAR_SBIAS, PAR_SALOG, PAR_SKIND = range(8)


def _in_proj_kernel(xc_ref, xl_ref, nw_ref, mod_ref, w_ref, par_ref, o_ref, sm_ref, h_ref, raw0_ref, raw1_ref, *,
                    nctx_blk):
    j = pl.program_id(1)
    nj = N_IN_PAD // IN_TN
    raws = (raw0_ref, raw1_ref)

    @pl.when((j == 0) & (pl.program_id(0) < nctx_blk))
    def _():
        h = _norm_mod(xc_ref[...], nw_ref[...], mod_ref[0, 1:2, :], mod_ref[0, 0:1, :])
        h_ref[...] = h.astype(BF16)

    @pl.when((j == 0) & (pl.program_id(0) >= nctx_blk))
    def _():
        h = _norm_mod(xl_ref[...], nw_ref[...], mod_ref[0, 1:2, :], mod_ref[0, 0:1, :])
        h_ref[...] = h.astype(BF16)

    T = h_ref.shape[0]
    G = GRID_W
    per_ctx = CONV_ROWS // G
    is_latent = pl.program_id(0) >= nctx_blk
    sub = lax.broadcasted_iota(jnp.int32, (8, 128), 0)

    def raw_piece(src, g, c):
        return src[g * G:(g + 1) * G, c * 128:(c + 1) * 128]

    def conv(src, g, c):
        cs = slice(c * 128, (c + 1) * 128)
        x = raw_piece(src, g, c)
        zero = jnp.zeros((1, 128), F32)
        before = zero if g % per_ctx == 0 else jnp.where(is_latent, 0.0, src[g * G - 1:g * G, cs])
        after = zero if g % per_ctx == per_ctx - 1 else jnp.where(is_latent, 0.0, src[(g + 1) * G:(g + 1) * G + 1, cs])
        rp = pltpu.roll(x, 1, 0)
        rn = pltpu.roll(x, G - 1, 0)
        prev = jnp.concatenate([jnp.where(sub == 0, before, rp[0:8]), rp[8:]], axis=0)
        nxt = jnp.concatenate([rn[:G - 8], jnp.where(sub == 7, after, rn[G - 8:])], axis=0)
        return (prev * par_ref[PAR_W0:PAR_W0 + 1, cs] + x * par_ref[PAR_W1:PAR_W1 + 1, cs]
                + nxt * par_ref[PAR_W2:PAR_W2 + 1, cs] + par_ref[PAR_BIAS:PAR_BIAS + 1, cs])

    def conv_silu(src, g, c):
        return _silu(conv(src, g, c))

    def conv_silu_l2(src, g, c):
        y = _silu(conv(src, g, c))
        y = y * lax.rsqrt(jnp.sum(y * y, axis=-1, keepdims=True) + EPS)
        return y * par_ref[PAR_L2SCALE:PAR_L2SCALE + 1, c * 128:(c + 1) * 128]

    def small(src, g, c):
        cs = slice(c * 128, (c + 1) * 128)
        acc = raw_piece(src, g, c)
        kind = par_ref[PAR_SKIND:PAR_SKIND + 1, cs]
        sp = _softplus(acc + par_ref[PAR_SBIAS:PAR_SBIAS + 1, cs])
        dec = -jnp.exp(par_ref[PAR_SALOG:PAR_SALOG + 1, cs]) * sp
        return jnp.where(kind == 0.0, sp, jnp.where(kind == 1.0, dec, jnp.where(kind == 2.0, _sigmoid(acc), 0.0)))

    rows_mm = 256
    tiles = IN_TN // 128
    piece_fn = {MODE_RAW: raw_piece, MODE_CONV: conv, MODE_CONV_SILU: conv_silu,
                MODE_CONV_SILU_L2: conv_silu_l2, MODE_SMALL: small}

    def project(dst, r):
        rs = slice(r * rows_mm, (r + 1) * rows_mm)
        dst[rs, :] = _dot(h_ref[rs, :], w_ref[...])

    def finish(src, blk, g, c):
        mode = _tile_mode(blk * tiles + c)
        y = piece_fn[mode](src, g, c)
        if mode == MODE_SMALL:
            sm_ref[g * G:(g + 1) * G, :] = y
            y = jnp.zeros_like(y)
        o_ref[g * G:(g + 1) * G, c * 128:(c + 1) * 128] = y.astype(o_ref.dtype)

    for step in range(nj + 1):
        @pl.when(j == step)
        def _(step=step):
            blk = step - 1
            src, dst = raws[blk % 2], raws[step % 2]
            for r in range(T // rows_mm):
                if step < nj:
                    project(dst, r)
                if blk < 0:
                    continue
                for g in range(r * rows_mm // G, (r + 1) * rows_mm // G):
                    for c in range(tiles):
                        finish(src, blk, g, c)


class Rows:
    def __init__(self, B, Lc, Ll):
        self.B, self.Lc, self.Ll = B, Lc, Ll
        self.NC = B * Lc
        self.R = B * Lc + B * Ll
        assert self.NC % Ll == 0 or Ll % self.NC == 0
        tm = 1024
        while self.NC % tm or Ll % tm:
            tm //= 2
        self.tm = tm

    def mod_index(self, tm):
        nctx = self.NC // tm
        per = self.Ll // tm
        return lambda i: jnp.where(i < nctx, 0, 1 + (i - nctx) // per)


def _stream_specs(rw, tm, xs, ngrid):
    xc, xl = xs
    nctx = rw.NC // tm
    off = nctx if xl.shape[0] == rw.R else 0
    D = xc.shape[1]
    if ngrid == 1:
        return [pl.BlockSpec((tm, D), lambda i: (jnp.minimum(i, nctx - 1), 0)),
                pl.BlockSpec((tm, D), lambda i: (jnp.maximum(i - nctx, 0) + off, 0))]
    return [pl.BlockSpec((tm, D), lambda i, j: (jnp.minimum(i, nctx - 1), 0)),
            pl.BlockSpec((tm, D), lambda i, j: (jnp.maximum(i - nctx, 0) + off, 0))]


def in_proj(rw, xs, l, nw, mod, w, par):
    R = rw.R
    D = xs[0].shape[1]
    N = w.shape[2]
    tm, tn = rw.tm, IN_TN
    nj = N // tn
    assert N == N_IN_PAD
    mi = rw.mod_index(tm)
    done = lambda j: jnp.maximum(j - 1, 0)
    return pl.pallas_call(
        functools.partial(_in_proj_kernel, nctx_blk=rw.NC // tm),
        grid=(R // tm, nj + 1),
        in_specs=_stream_specs(rw, tm, xs, 2) + [
            pl.BlockSpec((None, 1, D), lambda i, j: (l, 0, 0)),
            pl.BlockSpec((None, 1, 8, D), lambda i, j: (l, mi(i), 0, 0)),
            pl.BlockSpec((None, D, tn), lambda i, j: (l, 0, jnp.minimum(j, nj - 1))),
            pl.BlockSpec((None, 8, tn), lambda i, j: (l, 0, done(j))),
        ],
        out_specs=[pl.BlockSpec((tm, tn), lambda i, j: (i, done(j))),
                   pl.BlockSpec((tm, 128), lambda i, j: (i, 0))],
        out_shape=[jax.ShapeDtypeStruct((R, N), BF16), jax.ShapeDtypeStruct((R, 128), F32)],
        scratch_shapes=[pltpu.VMEM((tm, D), BF16), pltpu.VMEM((tm, tn), F32), pltpu.VMEM((tm, tn), F32)],
        compiler_params=_cp("arbitrary", "arbitrary"),
        name="in_proj",
    )(xs[0], xs[1], nw, mod, w, par)


def _in_proj_params(hy_conv_w, hy_conv_b, ssm_conv_w, ssm_conv_b, gdn_conv_w, ssm_dt_bias, gdn_dt_bias, gdn_A_log):
    depth = hy_conv_w.shape[0]

    def row(pieces):
        out, pos = [], 0
        for off, a in pieces:
            out += [jnp.zeros((depth, off - pos), F32), a.astype(F32)]
            pos = off + a.shape[1]
        return jnp.concatenate(out + [jnp.zeros((depth, N_IN_PAD - pos), F32)], axis=1)
    z4 = jnp.zeros((depth, 4), F32)
    conv = [row([(C_HY, hy_conv_w[:, t]), (C_XBC, ssm_conv_w[:, t]), (C_QKV, gdn_conv_w[:, t])]) for t in range(3)]
    bias = row([(C_HY, hy_conv_b), (C_XBC, ssm_conv_b)])
    l2s = row([(C_QKV, jnp.full((depth, 512), GDN_DK ** -0.5, F32)), (C_QKV + 512, jnp.ones((depth, 512), F32))])
    sbias = row([(C_SM, jnp.concatenate([ssm_dt_bias.reshape(depth, 16), gdn_dt_bias[:, 0], z4,
                                         gdn_dt_bias[:, 1], z4], axis=1))])
    salog = row([(C_SM + 16, jnp.concatenate([gdn_A_log[:, 0], z4, gdn_A_log[:, 1], z4], axis=1))])
    kind = np.full((depth, N_IN_PAD), 3.0, np.float32)
    kind[:, C_SM:C_SM + 16] = 0.0
    kind[:, C_SM + 16:C_SM + 20] = 1.0
    kind[:, C_SM + 24:C_SM + 28] = 1.0
    kind[:, C_SM + 20:C_SM + 24] = 2.0
    kind[:, C_SM + 28:C_SM + 32] = 2.0
    return jnp.stack(conv + [bias, l2s, sbias, salog, jnp.asarray(kind)], axis=1)


def _hy_filter_kernel(z_ref, w1_ref, b1_ref, w2_ref, b2_ref, w3_ref, f0_ref, f1_ref, win_ref, o_ref, h_ref):
    @pl.when(pl.program_id(1) == 0)
    def _():
        h1 = jnp.sin(f0_ref[...] * (_dot(z_ref[...], w1_ref[...], HI) + b1_ref[...]))
        h_ref[...] = jnp.sin(f1_ref[...] * (_dot(h1, w2_ref[...], HI) + b2_ref[...]))

    h = _dot(h_ref[...], w3_ref[...], HI) * win_ref[...]
    tl = h.shape[0]
    row = lax.broadcasted_iota(jnp.int32, (tl, 1), 0) + pl.program_id(0) * tl
    drop = (row == 0) & (pl.program_id(1) % 2 == 1)
    o_ref[...] = jnp.where(drop, 0.0, h).astype(o_ref.dtype)


def hy_features(L):
    t = jnp.linspace(0.0, 1.0, L, dtype=F32)[:, None]
    w = 2.0 * math.pi * jnp.arange(L, dtype=F32)[:, None] / L
    f = jnp.linspace(1e-4, HY_BANDS - 1, HY_BANDS, dtype=F32)[None, :]
    z = jnp.concatenate([t, jnp.cos(f * w), -jnp.sin(f * w)], axis=-1)
    z = jnp.pad(z, ((0, 0), (0, 128 - HY_EMB)))
    min_decay = math.log(HY_TARGET) / HY_LONG_DECAY_PCT
    max_decay = math.log(HY_TARGET) / HY_SHORT_DECAY_PCT
    deltas = jnp.linspace(min_decay, max_decay, HY_WIDTH, dtype=F32)
    window = jnp.exp(-t * jnp.abs(deltas))
    return z, window


def hy_filter(feat, w1, b1, w2, b2, w3, freq):
    z, window = feat
    L = z.shape[0]
    H = HY_HIDDEN
    w1p = jnp.pad(w1, ((0, 128 - HY_EMB), (0, 128 - H)))
    w2p = jnp.pad(w2, ((0, 128 - H), (0, 128 - H)))
    w3p = jnp.pad(w3, ((0, 128 - H), (0, 0)))
    pad1 = lambda v: jnp.pad(v, (0, 128 - H)).reshape(1, 128)
    tl = 256
    full = lambda shape: pl.BlockSpec(shape, lambda i, j: (0, 0))
    return pl.pallas_call(
        _hy_filter_kernel,
        grid=(L // tl, 4),
        in_specs=[
            pl.BlockSpec((tl, 128), lambda i, j: (i, 0)),
            full((128, 128)), full((1, 128)), full((128, 128)), full((1, 128)),
            pl.BlockSpec((128, HY_WIDTH), lambda i, j: (0, j)),
            full((1, 128)), full((1, 128)),
            pl.BlockSpec((tl, HY_WIDTH), lambda i, j: (i, 0)),
        ],
        out_specs=pl.BlockSpec((tl, HY_WIDTH), lambda i, j: (i, j)),
        out_shape=jax.ShapeDtypeStruct((L, 4 * HY_WIDTH), BF16),
        scratch_shapes=[pltpu.VMEM((tl, 128), F32)],
        compiler_params=_cp("arbitrary", "arbitrary"),
        name="hy_filter",
    )(z, w1p, pad1(b1), w2p, pad1(b2), w3p, pad1(freq[0]), pad1(freq[1]), window)


def dft_table(L):
    N = 2 * L
    H = L // 2
    q = np.arange(L, dtype=np.int64)
    f = np.where(q < H, q, L + H - q)[:, None]
    s = np.arange(L, dtype=np.int64)[None, :]
    ang = ((f * s) % N).astype(np.float64) * (2.0 * math.pi / N)
    mid = ((H * s) % N).astype(np.float64) * (2.0 * math.pi / N)
    q = q[:, None]
    imag = np.where(q == 0, np.cos(mid), np.where(q == H, -np.sin(mid), -np.sin(ang)))
    return jnp.asarray(np.concatenate([np.cos(ang), imag], axis=0), dtype=BF16)


def dft_tables_split(L):
    N = 2 * L
    H = L // 2
    q = np.arange(H, dtype=np.int64)[:, None]
    m = np.arange(H, dtype=np.int64)[None, :]
    ang_e = ((q * 2 * m) % N).astype(np.float64) * (2.0 * math.pi / N)
    ang_o = ((q * (2 * m + 1)) % N).astype(np.float64) * (2.0 * math.pi / N)
    alt = (1 - 2 * (m % 2)).astype(np.float64)
    ce, co = np.cos(ang_e), np.cos(ang_o)
    se = np.where(q == 0, alt, -np.sin(ang_e))
    so = np.where(q == 0, -alt, -np.sin(ang_o))
    w = np.where(q == 0, 1.0, 2.0) / N
    ise = np.where(q == 0, 2.0 / N * alt, -np.sin(ang_e) * w)
    iso = np.where(q == 0, -2.0 / N * alt, -np.sin(ang_o) * w)
    fwd = np.stack([ce, co, se, so])
    inv = np.stack([(ce * w).T, ise.T, (co * w).T, iso.T])
    return jnp.asarray(fwd, dtype=BF16), jnp.asarray(inv, dtype=BF16)


def _matmul_kernel(a_ref, b_ref, o_ref):
    o_ref[...] = _dot(a_ref[...], b_ref[...])


def matmul(a, b, tm, tn):
    M, K = a.shape
    N = b.shape[1]
    return pl.pallas_call(
        _matmul_kernel,
        grid=(M // tm, N // tn),
        in_specs=[pl.BlockSpec((tm, K), lambda i, j: (i, 0)), pl.BlockSpec((K, tn), lambda i, j: (0, j))],
        out_specs=pl.BlockSpec((tm, tn), lambda i, j: (i, j)),
        out_shape=jax.ShapeDtypeStruct((M, N), F32),
        compiler_params=_cp("arbitrary", "arbitrary"),
        name="matmul",
    )(a, b)


def _long_conv_kernel(u_ref, g_ref, bias_ref, fwd_ref, inv_ref, ar0_ref, ar1_ref, ar0h_ref, ar1h_ref,
                      ai0_ref, ai1_ref, ai0h_ref, ai1h_ref, o_ref, ue_ref, uo_ref, acce_ref, acco_ref, y_ref):
    f = pl.program_id(1)
    half = ue_ref.shape[0]

    lane_tiles = [slice(c * 128, (c + 1) * 128) for c in range(y_ref.shape[0])]

    @pl.when(f == 0)
    def _():
        for c, cs in enumerate(lane_tiles):
            y_c = y_ref.at[c]
            y_c[...] = u_ref[:, cs].astype(F32)
            ue_ref[:, cs] = y_c[pl.ds(0, half, stride=2), :].astype(BF16)
            uo_ref[:, cs] = y_c[pl.ds(1, half, stride=2), :].astype(BF16)
        acce_ref[...] = jnp.zeros_like(acce_ref)
        acco_ref[...] = jnp.zeros_like(acco_ref)

    ue, uo = ue_ref[...], uo_ref[...]
    ae, ao = _dot(fwd_ref[0], ue), _dot(fwd_ref[1], uo)
    be, bo = _dot(fwd_ref[2], ue), _dot(fwd_ref[3], uo)
    ur, ur2 = ae + ao, ae - ao
    ui, ui2 = be + bo, bo - be
    first = (lax.broadcasted_iota(jnp.int32, (fwd_ref.shape[1], 1), 0) == 0) & (f == 0)
    kr, kr2 = ar0_ref[...] + ar1_ref[...], ar0h_ref[...] + ar1h_ref[...]
    ki = jnp.where(first, ai0_ref[...] + ai1_ref[...], ai0_ref[...] - ai1_ref[...])
    ki2 = ai0h_ref[...] - ai1h_ref[...]
    pr, pi = ur * kr - ui * ki, ur * ki + ui * kr
    pr2, pi2 = ur2 * kr2 - ui2 * ki2, ur2 * ki2 + ui2 * kr2
    dc, ny = ur * kr, ur2 * kr2
    gr = jnp.where(first, dc + ny, pr + pr2)
    gi = jnp.where(first, be * ki - bo * ki2, pi - pi2)
    hr = jnp.where(first, dc - ny, pr - pr2)
    hi = jnp.where(first, be * ki2 + bo * ki, pi + pi2)
    acce_ref[...] += _dot(inv_ref[0], gr.astype(BF16)) + _dot(inv_ref[1], gi.astype(BF16))
    acco_ref[...] += _dot(inv_ref[2], hr.astype(BF16)) + _dot(inv_ref[3], hi.astype(BF16))

    @pl.when(f == pl.num_programs(1) - 1)
    def _():
        for c, cs in enumerate(lane_tiles):
            y_c = y_ref.at[c]
            y_c[pl.ds(0, half, stride=2), :] = acce_ref[:, cs]
            y_c[pl.ds(1, half, stride=2), :] = acco_ref[:, cs]
            u = u_ref[:, cs].astype(F32)
            o_ref[:, cs] = (g_ref[:, cs].astype(F32) * (y_c[...] + u * bias_ref[:, cs])).astype(o_ref.dtype)


def long_conv(B, L, u, u_rb0, u_cb, gate, g_rb0, gate_cb, bias, fwd, inv, kspec, order, out_dtype):
    C = HY_WIDTH
    H = L // 2
    FB = min(FREQ_BLK, H)
    nfb = H // FB
    kblk = lambda part, d: pl.BlockSpec((FB, C), lambda b, f: (part * nfb + f, 2 * order + d))
    return pl.pallas_call(
        _long_conv_kernel,
        grid=(B, nfb),
        in_specs=[
            pl.BlockSpec((L, C), lambda b, f: (u_rb0 + b, u_cb)),
            pl.BlockSpec((L, C), lambda b, f: (g_rb0 + b, gate_cb)),
            pl.BlockSpec((1, C), lambda b, f: (0, 0)),
            pl.BlockSpec((4, FB, H), lambda b, f: (0, f, 0)),
            pl.BlockSpec((4, H, FB), lambda b, f: (0, 0, f)),
            kblk(0, 0), kblk(0, 1), kblk(1, 0), kblk(1, 1), kblk(2, 0), kblk(2, 1), kblk(3, 0), kblk(3, 1),
        ],
        out_specs=pl.BlockSpec((L, C), lambda b, f: (b, 0)),
        out_shape=jax.ShapeDtypeStruct((B * L, C), out_dtype),
        scratch_shapes=[pltpu.VMEM((H, C), BF16), pltpu.VMEM((H, C), BF16),
                        pltpu.VMEM((H, C), F32), pltpu.VMEM((H, C), F32), pltpu.VMEM((C // 128, L, 128), F32)],
        compiler_params=_cp("arbitrary", "arbitrary"),
        name="long_conv",
    )(u, gate, bias.reshape(1, C), fwd, inv, *([kspec] * 8))


def _scan_blocks(rw, rows):
    nbc, nbl, base = rw.Lc // rows, rw.Ll // rows, rw.NC // rows

    def make(d):
        def f(b, s):
            jc = s if d == 0 else nbc - 1 - s
            jl = (s - nbc) if d == 0 else nbl - 1 - (s - nbc)
            return jnp.where(s < nbc, b * nbc + jc, base + b * nbl + jl)
        return f

    return [make(0), make(1)], nbc + nbl


def _expand_lanes(x, base, n, width):
    rows = x.shape[0]
    per = 128 // width
    lane = lax.broadcasted_iota(jnp.int32, (rows, 128), 1)
    tiles = []
    for t in range(n // per):
        c0 = base + t * per
        tile = jnp.broadcast_to(x[:, c0:c0 + 1], (rows, 128))
        for i in range(1, per):
            tile = jnp.where(lane >= i * width, jnp.broadcast_to(x[:, c0 + i:c0 + i + 1], (rows, 128)), tile)
        tiles.append(tile)
    return jnp.concatenate(tiles, axis=1)


def _ssd_kernel(xf, bf, cf, smf, dtf, xb, bb, cb_, smb, dtb, alx_ref, alc_ref, of_ref, ob_ref, h_ref):
    Q = SSM_CHUNK
    GW = SSM_GW

    @pl.when(pl.program_id(1) == 0)
    def _():
        h_ref[...] = jnp.zeros_like(h_ref)

    row = lax.broadcasted_iota(jnp.int32, (Q, Q), 0)
    col = lax.broadcasted_iota(jnp.int32, (Q, Q), 1)
    lane_head = lax.broadcasted_iota(jnp.int32, (Q, GW), 1) // SSM_HEAD_DIM
    dirs = ((xf, bf, cf, smf, dtf, of_ref), (xb, bb, cb_, smb, dtb, ob_ref))
    jobs = []
    for d in range(2):
        x_ref, b_ref, c_ref, sm_ref, dt_ref, o_ref = dirs[d]
        keep = (col <= row) if d == 0 else (col >= row)
        tri = keep.astype(BF16)
        tri_t = ((row <= col) if d == 0 else (row >= col)).astype(BF16)
        sm = sm_ref[...]
        a_x = -jnp.exp(alx_ref[d])
        dtx = _expand_lanes(sm, 8 * d, SSM_HEADS, SSM_HEAD_DIM)
        cumx = _expand_lanes(_dot_01_lhs(tri, sm), 8 * d, SSM_HEADS, SSM_HEAD_DIM) * a_x
        cumr = _dot_01_rhs(dt_ref[0], tri_t) * (-jnp.exp(alc_ref[d]))
        last = Q - 1 if d == 0 else 0
        totx = cumx[last:last + 1, :]
        xd = x_ref[...].astype(F32) * dtx
        xdw = xd * jnp.exp(totx - cumx)
        ecum = jnp.exp(cumx)
        for g in range(SSM_GROUPS):
            gs = slice(g * GW, (g + 1) * GW)
            jobs.append(dict(d=d, g=g, gs=gs, keep=keep, cumx=cumx, cumr=cumr, o_ref=o_ref,
                             bg=b_ref[:, g * SSM_STATE:(g + 1) * SSM_STATE].astype(BF16),
                             cg=c_ref[:, g * SSM_STATE:(g + 1) * SSM_STATE].astype(BF16),
                             xdg=xd[:, gs], xdw=xdw[:, gs].astype(BF16), ecum=ecum[:, gs],
                             etot=jnp.exp(totx[:, gs])))
    for j in jobs:
        j["cb"] = _dot_nt(j["cg"], j["bg"])
        j["h"] = h_ref[j["d"], j["g"]]
    for j in jobs:
        ms, xs = [], []
        for e4 in range(SSM_HPG):
            e = j["g"] * SSM_HPG + e4
            diff = j["cumx"][:, e * SSM_HEAD_DIM:e * SSM_HEAD_DIM + 1] - j["cumr"][e:e + 1, :]
            ms.append((j["cb"] * jnp.where(j["keep"], jnp.exp(diff), 0.0)).astype(BF16))
            xs.append(jnp.where(lane_head == e4, j["xdg"], 0.0).astype(BF16))
        yd = _dot(jnp.concatenate(ms, axis=1), jnp.concatenate(xs, axis=0))
        y_off = _dot(j["cg"], j["h"].astype(BF16)) * j["ecum"]
        j["o_ref"][:, j["gs"]] = (yd + y_off).astype(BF16)
    for j in jobs:
        h_ref[j["d"], j["g"]] = j["h"] * j["etot"] + _dot_tn(j["bg"], j["xdw"])


def ssd_scan(rw, p, sm, dtT, alx, alc):
    Q = SSM_CHUNK
    blks, nsteps = _scan_blocks(rw, Q)
    R = p.shape[0]
    in_specs = []
    for d in range(2):
        f = blks[d]
        in_specs += [
            pl.BlockSpec((Q, 512), lambda b, s, f=f: (f(b, s), C_XBC // 512)),
            pl.BlockSpec((Q, 256), lambda b, s, f=f: (f(b, s), C_XBC // 256 + 2)),
            pl.BlockSpec((Q, 256), lambda b, s, f=f: (f(b, s), C_XBC // 256 + 3)),
            pl.BlockSpec((Q, 128), lambda b, s, f=f: (f(b, s), 0)),
            pl.BlockSpec((1, 8, Q), lambda b, s, f=f, d=d: (d, 0, f(b, s))),
        ]
    in_specs += [pl.BlockSpec((2, 1, 512), lambda b, s: (0, 0, 0)), pl.BlockSpec((2, 8, 1), lambda b, s: (0, 0, 0))]
    ops = (p, p, p, sm, dtT)
    return pl.pallas_call(
        _ssd_kernel,
        grid=(rw.B, nsteps),
        in_specs=in_specs,
        out_specs=[pl.BlockSpec((Q, 512), lambda b, s, f=blks[d]: (f(b, s), 0)) for d in range(2)],
        out_shape=[jax.ShapeDtypeStruct((R, 512), BF16)] * 2,
        scratch_shapes=[pltpu.VMEM((2, SSM_GROUPS, SSM_STATE, SSM_GW), F32)],
        compiler_params=_cp("arbitrary", "arbitrary"),
        name="ssd_scan",
    )(*ops, *ops, alx, alc)


def _split3(x):
    x1 = x.astype(BF16)
    r = x - x1.astype(F32)
    x2 = r.astype(BF16)
    x3 = (r - x2.astype(F32)).astype(BF16)
    return x1, x2, x3


def _dot_01_lhs(m01, x):
    x1, x2, x3 = _split3(x)
    return _dot(m01, x1) + _dot(m01, x2) + _dot(m01, x3)


def _dot_01_rhs(x, m01):
    x1, x2, x3 = _split3(x)
    return _dot(x1, m01) + _dot(x2, m01) + _dot(x3, m01)


GDN_ROWS = 256


def _gdn_prep_kernel(q_ref, k_ref, v_ref, sm_ref, gT_ref, u_ref, w_ref, qg_ref, kd_ref, qk_ref, egl_ref):
    C = GDN_CHUNK
    row = lax.broadcasted_iota(jnp.int32, (C, C), 0)
    col = lax.broadcasted_iota(jnp.int32, (C, C), 1)
    jobs = []
    levels = []
    for d in range(2):
        keep = (col <= row) if d == 0 else (col >= row)
        late, early = (row, col) if d == 0 else (col, row)
        levels.append([(((row ^ col) >> (j + 1)) == 0) & ((late & (1 << j)) != 0) & ((early & (1 << j)) == 0)
                       for j in range(6)])
        tri = keep.astype(BF16)
        tri_t = ((row <= col) if d == 0 else (row >= col)).astype(BF16)
        last = C - 1 if d == 0 else 0
        for c in range(GDN_ROWS // C):
            rows = slice(c * C, (c + 1) * C)
            smc = sm_ref[rows, :]
            cums = _dot_01_lhs(tri, smc)
            cumr = _dot_01_rhs(gT_ref[c, 8 * d:8 * d + 8, :], tri_t)
            tot = cums[last:last + 1, :]
            for h in range(GDN_HEADS):
                lg = 16 + 8 * d + h
                jobs.append(dict(d=d, c=c, h=h, rows=rows, hs=slice(h * 128, (h + 1) * 128), keep=keep,
                                 gc=cums[:, lg:lg + 1], beta=smc[:, lg + 4:lg + 5],
                                 gl=tot[:, lg:lg + 1], gr=cumr[h:h + 1, :]))
    for j in jobs:
        q = q_ref[j["rows"], j["hs"]].astype(F32)
        k = k_ref[j["rows"], j["hs"]].astype(F32)
        j["dec"] = jnp.where(j["keep"], jnp.exp(j["gc"] - j["gr"]), 0.0)
        kb = k * j["beta"]
        both = _dot_nt(jnp.concatenate([kb, q], axis=0).astype(BF16), k.astype(BF16))
        j["a"] = both[:C] * j["dec"]
        j["n"] = -jnp.where(levels[j["d"]][0], j["a"], 0.0)
        qk_ref[j["d"], j["c"], j["h"]] = (both[C:] * j["dec"]).astype(BF16)
    for lev in range(1, 6):
        for j in jobs:
            l = jnp.where(levels[j["d"]][lev], j["a"], 0.0)
            j["y"] = l + _dot(l.astype(BF16), j["n"].astype(BF16))
        for j in jobs:
            j["n"] = j["n"] - j["y"] - _dot(j["n"].astype(BF16), j["y"].astype(BF16))
    for j in jobs:
        d, rows, hs, gc, gl, beta = j["d"], j["rows"], j["hs"], j["gc"], j["gl"], j["beta"]
        q = q_ref[rows, hs].astype(F32)
        k = k_ref[rows, hs].astype(F32)
        eg = jnp.exp(gc)
        rhs = jnp.concatenate([v_ref[rows, hs].astype(F32) * beta, k * beta * eg], axis=1)
        sol = rhs + _dot(j["n"].astype(BF16), rhs.astype(BF16))
        u_ref[d, rows, hs] = sol[:, :GDN_DV].astype(BF16)
        w_ref[d, rows, hs] = sol[:, GDN_DV:].astype(BF16)
        qg_ref[d, rows, hs] = (q * eg).astype(BF16)
        kd_ref[d, rows, hs] = (k * jnp.exp(gl - gc)).astype(BF16)
        egl_ref[d, j["c"], :, hs] = jnp.broadcast_to(jnp.exp(gl), (8, 128))


def gdn_prep(p, sm, gT):
    R = p.shape[0]
    T, C = GDN_ROWS, GDN_CHUNK
    nc = T // C
    col = lambda k: pl.BlockSpec((T, 512), lambda i: (i, C_QKV // 512 + k))
    dirrow = pl.BlockSpec((2, T, 512), lambda i: (0, i, 0))
    return pl.pallas_call(
        _gdn_prep_kernel,
        grid=(R // T,),
        in_specs=[col(0), col(1), col(2),
                  pl.BlockSpec((T, 128), lambda i: (i, 0)),
                  pl.BlockSpec((nc, 16, C), lambda i: (i, 0, 0))],
        out_specs=[dirrow, dirrow, dirrow, dirrow,
                   pl.BlockSpec((2, nc, GDN_HEADS, C, C), lambda i: (0, i, 0, 0, 0)),
                   pl.BlockSpec((2, nc, 8, 512), lambda i: (0, i, 0, 0))],
        out_shape=[jax.ShapeDtypeStruct((2, R, 512), BF16),
                   jax.ShapeDtypeStruct((2, R, 512), BF16),
                   jax.ShapeDtypeStruct((2, R, 512), BF16),
                   jax.ShapeDtypeStruct((2, R, 512), BF16),
                   jax.ShapeDtypeStruct((2, R // C, GDN_HEADS, C, C), BF16),
                   jax.ShapeDtypeStruct((2, R // C, 8, 512), F32)],
        compiler_params=_cp("arbitrary"),
        name="gdn_prep",
    )(p, p, p, sm, gT)


def _gdn_scan_kernel(uf, wf, qgf, kdf, qkf, eglf, ub, wb, qgb, kdb, qkb, eglb, of_ref, ob_ref, s_ref):
    C = GDN_CHUNK
    nch = GDN_ROWS // C

    @pl.when(pl.program_id(1) == 0)
    def _():
        s_ref[...] = jnp.zeros_like(s_ref)

    dirs = ((uf, wf, qgf, kdf, qkf, eglf, of_ref), (ub, wb, qgb, kdb, qkb, eglb, ob_ref))
    chains = [(d, h) for d in range(2) for h in range(GDN_HEADS)]
    S = {ch: s_ref[ch[0], ch[1]] for ch in chains}
    for i in range(nch):
        Sb, vnb, rows_of, c_of = {}, {}, {}, {}
        for d, h in chains:
            c_of[d] = i if d == 0 else nch - 1 - i
            rows_of[d] = slice(c_of[d] * C, (c_of[d] + 1) * C)
        for d, h in chains:
            hs = slice(h * 128, (h + 1) * 128)
            Sb[d, h] = S[d, h].astype(BF16)
            v_new = dirs[d][0][0, rows_of[d], hs].astype(F32) - _dot(dirs[d][1][0, rows_of[d], hs], Sb[d, h])
            vnb[d, h] = v_new.astype(BF16)
        for d, h in chains:
            hs = slice(h * 128, (h + 1) * 128)
            u_ref, w_ref, qg_ref, kd_ref, qk_ref, egl_ref, o_ref = dirs[d]
            S[d, h] = S[d, h] * egl_ref[0, c_of[d], 0:1, hs] + _dot_tn(kd_ref[0, rows_of[d], hs], vnb[d, h])
        for d, h in chains:
            hs = slice(h * 128, (h + 1) * 128)
            u_ref, w_ref, qg_ref, kd_ref, qk_ref, egl_ref, o_ref = dirs[d]
            o_ref[rows_of[d], hs] = (_dot(qg_ref[0, rows_of[d], hs], Sb[d, h])
                                     + _dot(qk_ref[0, c_of[d], h], vnb[d, h])).astype(BF16)
    for ch in chains:
        s_ref[ch[0], ch[1]] = S[ch]


def gdn_scan(rw, u, w, qg, kd, qk, egl):
    T, C = GDN_ROWS, GDN_CHUNK
    nc = T // C
    R = u.shape[1]
    nbc, nbl, base = rw.Lc // T, rw.Ll // T, rw.NC // T

    def blk(d):
        def f(b, s):
            jc = s if d == 0 else nbc - 1 - s
            jl = (s - nbc) if d == 0 else nbl - 1 - (s - nbc)
            return jnp.where(s < nbc, b * nbc + jc, base + b * nbl + jl)
        return f

    in_specs = []
    for d in range(2):
        f = blk(d)
        rowspec = pl.BlockSpec((1, T, 512), lambda b, s, f=f, d=d: (d, f(b, s), 0))
        in_specs += [rowspec, rowspec, rowspec, rowspec,
                     pl.BlockSpec((1, nc, GDN_HEADS, C, C), lambda b, s, f=f, d=d: (d, f(b, s), 0, 0, 0)),
                     pl.BlockSpec((1, nc, 8, 512), lambda b, s, f=f, d=d: (d, f(b, s), 0, 0))]
    out_specs = [pl.BlockSpec((T, 512), lambda b, s, f=blk(d): (f(b, s), 0)) for d in range(2)]
    ops = (u, w, qg, kd, qk, egl)
    return pl.pallas_call(
        _gdn_scan_kernel,
        grid=(rw.B, nbc + nbl),
        in_specs=in_specs,
        out_specs=out_specs,
        out_shape=[jax.ShapeDtypeStruct((R, 512), BF16)] * 2,
        scratch_shapes=[pltpu.VMEM((2, GDN_HEADS, GDN_DK, GDN_DV), F32)],
        compiler_params=_cp("arbitrary", "arbitrary"),
        name="gdn_scan",
    )(*ops, *ops)


def _merge_kernel(yhc_ref, yhl_ref, sf_ref, sb_ref, sx_ref, sz_ref, dx_ref, snw_ref, gf_ref, gb_ref, gg_ref, gnw_ref,
                  g0_ref, g1_ref, g2_ref, w0_ref, w1_ref, w2_ref, wo_ref, xc_ref, xl_ref, mod_ref, o_ref,
                  ys_ref, yg_ref, *, nctx_blk):
    tm = xc_ref.shape[0]
    rp = 64
    for r in range(tm // rp):
        rs = slice(r * rp, (r + 1) * rp)
        y = (sf_ref[rs, :].astype(F32) + sb_ref[rs, :].astype(F32)
             + sx_ref[rs, :].astype(F32) * dx_ref[...])
        y = y * _silu(sz_ref[rs, :].astype(F32))
        parts = []
        for g in range(SSM_GROUPS):
            yg = y[:, g * SSM_GW:(g + 1) * SSM_GW]
            parts.append(yg * lax.rsqrt(jnp.mean(yg * yg, axis=-1, keepdims=True) + EPS))
        ys_ref[rs, :] = (jnp.concatenate(parts, axis=1) * snw_ref[...]).astype(BF16)
        o = gf_ref[rs, :].astype(F32) + gb_ref[rs, :].astype(F32)
        parts = []
        for h in range(GDN_HEADS):
            oh = o[:, h * 128:(h + 1) * 128]
            parts.append(oh * lax.rsqrt(jnp.mean(oh * oh, axis=-1, keepdims=True) + EPS))
        yg_ref[rs, :] = (jnp.concatenate(parts, axis=1) * gnw_ref[...]
                         * _silu(gg_ref[rs, :].astype(F32))).astype(BF16)
    is_ctx = pl.program_id(0) < nctx_blk
    yh = jnp.where(is_ctx, yhc_ref[...], yhl_ref[...])
    m = (_sigmoid(g0_ref[...].astype(F32)) * _dot(yh, w0_ref[...])
         + _sigmoid(g1_ref[...].astype(F32)) * _dot(ys_ref[...], w1_ref[...])
         + _sigmoid(g2_ref[...].astype(F32)) * _dot(yg_ref[...], w2_ref[...]))
    x = jnp.where(is_ctx, xc_ref[...], xl_ref[...])
    o_ref[...] = x + mod_ref[0, 2:3, :] * _dot(m.astype(BF16), wo_ref[...])


def merge(rw, l, yh, y_f, y_b, dx, ssm_nw, o_f, o_b, gdn_nw, p, w0, w1, w2, wo, xs, mod):
    R = rw.R
    D = xs[0].shape[1]
    tm = min(rw.tm, 512)
    mi = rw.mod_index(tm)
    yspec = pl.BlockSpec((tm, 512), lambda i: (i, 0))
    pspec = lambda col: pl.BlockSpec((tm, 512), lambda i: (i, col // 512))
    vec = pl.BlockSpec((1, 512), lambda i: (0, 0))
    gspec = lambda k: pl.BlockSpec((tm, D), lambda i: (i, C_GATE // D + k))
    wspec = pl.BlockSpec((None, 512, D), lambda i: (l, 0, 0))
    return pl.pallas_call(
        functools.partial(_merge_kernel, nctx_blk=rw.NC // tm),
        grid=(R // tm,),
        in_specs=_stream_specs(rw, tm, yh, 1) + [
                  yspec, yspec, pspec(C_XBC), pspec(C_Z), vec, vec,
                  yspec, yspec, pspec(C_GG), vec,
                  gspec(0), gspec(1), gspec(2), wspec, wspec, wspec,
                  pl.BlockSpec((None, D, D), lambda i: (l, 0, 0))]
                 + _stream_specs(rw, tm, xs, 1)
                 + [pl.BlockSpec((None, 1, 8, D), lambda i: (l, mi(i), 0, 0))],
        out_specs=pl.BlockSpec((tm, D), lambda i: (i, 0)),
        out_shape=jax.ShapeDtypeStruct((R, D), F32),
        scratch_shapes=[pltpu.VMEM((tm, 512), BF16), pltpu.VMEM((tm, 512), BF16)],
        compiler_params=_cp("arbitrary"),
        name="merge",
    )(yh[0], yh[1], y_f, y_b, p, p, dx, ssm_nw, o_f, o_b, p, gdn_nw, p, p, p, w0, w1, w2, wo, xs[0], xs[1], mod)


def _swiglu_up_kernel(x_ref, nw_ref, mod_ref, wg_ref, wu_ref, o_ref, h_ref, g0_ref, g1_ref, u0_ref, u1_ref):
    @pl.when(pl.program_id(1) == 0)
    def _():
        h = _norm_mod(x_ref[...], nw_ref[...], mod_ref[0, 4:5, :], mod_ref[0, 3:4, :])
        h_ref[...] = h.astype(BF16)

    T, tn = o_ref.shape
    rows = g0_ref.shape[0]
    gs, us = (g0_ref, g1_ref), (u0_ref, u1_ref)

    def project(r):
        hh = h_ref[r * rows:(r + 1) * rows, :]
        gs[r % 2][...] = _dot(hh, wg_ref[...])
        us[r % 2][...] = _dot(hh, wu_ref[...])

    def finish(r):
        for q in range(rows // 64):
            for c in range(tn // 128):
                ps = (slice(q * 64, (q + 1) * 64), slice(c * 128, (c + 1) * 128))
                y = _silu(gs[r % 2][ps]) * us[r % 2][ps]
                o_ref[r * rows + q * 64:r * rows + (q + 1) * 64, ps[1]] = y.astype(o_ref.dtype)

    for r in range(T // rows):
        project(r)
        if r > 0:
            finish(r - 1)
    finish(T // rows - 1)


def swiglu_up(rw, l, x, nw, mod, wgu):
    R, D = x.shape
    tm = rw.tm
    tn = D_FF // 2
    nj = D_FF // tn
    mi = rw.mod_index(tm)
    return pl.pallas_call(
        _swiglu_up_kernel,
        grid=(R // tm, nj),
        in_specs=[
            pl.BlockSpec((tm, D), lambda i, j: (i, 0)),
            pl.BlockSpec((None, 1, D), lambda i, j: (l, 0, 0)),
            pl.BlockSpec((None, 1, 8, D), lambda i, j: (l, mi(i), 0, 0)),
            pl.BlockSpec((None, D, tn), lambda i, j: (l, 0, j)),
            pl.BlockSpec((None, D, tn), lambda i, j: (l, 0, nj + j)),
        ],
        out_specs=pl.BlockSpec((tm, tn), lambda i, j: (i, j)),
        out_shape=jax.ShapeDtypeStruct((R, D_FF), BF16),
        scratch_shapes=[pltpu.VMEM((tm, D), BF16)] + [pltpu.VMEM((min(256, tm), tn), F32)] * 4,
        compiler_params=_cp("arbitrary", "arbitrary"),
        name="swiglu_up",
    )(x, nw, mod, wgu, wgu)


def _swiglu_down_kernel(a_ref, w_ref, x_ref, mod_ref, o_ref):
    o_ref[...] = x_ref[...] + mod_ref[0, 5:6, :] * _dot(a_ref[...], w_ref[...])


def swiglu_down(rw, l, a, w, x, mod):
    R, D = x.shape
    tm = min(rw.tm, 512)
    mi = rw.mod_index(tm)
    return pl.pallas_call(
        _swiglu_down_kernel,
        grid=(R // tm,),
        in_specs=[
            pl.BlockSpec((tm, D_FF), lambda i: (i, 0)),
            pl.BlockSpec((None, D_FF, D), lambda i: (l, 0, 0)),
            pl.BlockSpec((tm, D), lambda i: (i, 0)),
            pl.BlockSpec((None, 1, 8, D), lambda i: (l, mi(i), 0, 0)),
        ],
        out_specs=pl.BlockSpec((tm, D), lambda i: (i, 0)),
        out_shape=jax.ShapeDtypeStruct((R, D), F32),
        compiler_params=_cp("arbitrary"),
        name="swiglu_down",
    )(a, w, x, mod)


def _final_norm_kernel(x_ref, w_ref, o_ref):
    x = x_ref[...]
    ms = jnp.mean(x * x, axis=-1, keepdims=True)
    o_ref[...] = x * lax.rsqrt(ms + EPS) * w_ref[...]


def final_norm(rw, x, w):
    D = x.shape[1]
    tm = rw.tm
    n0 = rw.NC // tm
    nl = rw.B * rw.Ll
    return pl.pallas_call(
        _final_norm_kernel,
        grid=(nl // tm,),
        in_specs=[pl.BlockSpec((tm, D), lambda i: (n0 + i, 0)), pl.BlockSpec((1, D), lambda i: (0, 0))],
        out_specs=pl.BlockSpec((tm, D), lambda i: (i, 0)),
        out_shape=jax.ShapeDtypeStruct((nl, D), F32),
        compiler_params=_cp("arbitrary"),
        name="final_norm",
    )(x, w.reshape(1, D))


def _regroup_w_in(w_in):
    o_dt = 3072
    o_gdn = 3088
    o_a = o_gdn + 2048
    o_b = o_a + 8
    o_gate = o_gdn + 2064
    wt = jnp.swapaxes(w_in, 1, 2).astype(BF16)
    pieces = [
        wt[:, 0:3072],
        wt[:, o_gdn:o_gdn + 2048],
        wt[:, o_gate:o_gate + 3072],
        wt[:, o_dt:o_dt + 16],
        wt[:, o_a:o_a + 4], wt[:, o_b:o_b + 4],
        wt[:, o_a + 4:o_a + 8], wt[:, o_b + 4:o_b + 8],
        jnp.zeros((wt.shape[0], N_IN_PAD - C_SM - 32, wt.shape[2]), wt.dtype),
    ]
    return jnp.swapaxes(jnp.concatenate(pieces, axis=1), 1, 2)


def kernel(x, c, ctx, c_ctx, w_ada, b_ada, norm1_w, norm2_w, w_in, hy_conv_w, hy_conv_b, hy_w1, hy_b1, hy_w2, hy_b2, hy_w3, hy_freq, hy_bias, ssm_conv_w, ssm_conv_b, ssm_dt_bias, ssm_A_log, ssm_D, ssm_norm_w, gdn_conv_w, gdn_dt_bias, gdn_A_log, gdn_norm_w, w_hy_out, w_ssm_out, w_gdn_out, w_out, w_gate_up, w_down, final_norm_w):
    B, Ll, D = x.shape
    Lc = ctx.shape[1]
    depth = w_ada.shape[0]
    assert Lc == CONV_ROWS and D == D_MODEL and B <= 15
    rw = Rows(B, Lc, Ll)
    R, NC = rw.R, rw.NC

    xs = (ctx.reshape(B * Lc, D), x.reshape(B * Ll, D))

    svec = jnp.concatenate([c_ctx[None, :], c, jnp.zeros((15 - B, D), F32)], axis=0)
    mod = ada_modulation(svec, w_ada, b_ada)
    mod = jnp.pad(mod.reshape(depth, 16, 6, D), ((0, 0), (0, 0), (0, 2), (0, 0)))

    w_in_r = _regroup_w_in(w_in)
    par = _in_proj_params(hy_conv_w, hy_conv_b, ssm_conv_w, ssm_conv_b, gdn_conv_w, ssm_dt_bias, gdn_dt_bias,
                          gdn_A_log)
    norm1 = norm1_w.reshape(depth, 1, D)
    norm2 = norm2_w.reshape(depth, 1, D)
    w_hy_o, w_ssm_o, w_gdn_o, w_o = (w.astype(BF16) for w in (w_hy_out, w_ssm_out, w_gdn_out, w_out))
    w_gu, w_dn = w_gate_up.astype(BF16), w_down.astype(BF16)
    dft_l = (dft_table(Ll),) + dft_tables_split(Ll)
    dft_c = (dft_table(Lc),) + dft_tables_split(Lc)
    feat_l, feat_c = hy_features(Ll), hy_features(Lc)

    for l in range(depth):
        p, sm = in_proj(rw, xs, l, norm1, mod, w_in_r, par)

        sm32_t = sm[:, :32].T
        dt_t = sm32_t[:16].reshape(2, 8, R)
        g_t = sm32_t[16:32].reshape(16, R // GDN_CHUNK, GDN_CHUNK).transpose(1, 0, 2)

        alx = jnp.repeat(ssm_A_log[l], SSM_HEAD_DIM, axis=-1).reshape(2, 1, 512)
        alc = ssm_A_log[l].reshape(2, 8, 1)
        y_f, y_b = ssd_scan(rw, p, sm, dt_t, alx, alc)
        dx = jnp.repeat(ssm_D[l], SSM_HEAD_DIM).reshape(1, 512)

        o_f, o_b = gdn_scan(rw, *gdn_prep(p, sm, g_t))

        hyu = p
        parts = []
        for (Bn, L, blk0, (full, fwd, inv), feat) in ((B, Lc, 0, dft_c, feat_c),
                                                      (B, Ll, NC // Ll, dft_l, feat_l)):
            if NC % L:
                raise ValueError("latent length must divide the context row count")
            filt = hy_filter(feat, hy_w1[l], hy_b1[l], hy_w2[l], hy_b2[l], hy_w3[l], hy_freq[l])
            kspec = matmul(full, filt, min(512, 2 * L), 512)
            z1 = long_conv(Bn, L, hyu, blk0, 0, hyu, blk0, 1, hy_bias[l, 0], fwd, inv, kspec, 0, F32)
            yy = long_conv(Bn, L, z1, 0, 0, hyu, blk0, 2, hy_bias[l, 1], fwd, inv, kspec, 1, BF16)
            parts.append(yy)
        y_hy = tuple(parts)

        xa = merge(rw, l, y_hy, y_f, y_b, dx, ssm_norm_w[l].reshape(1, 512),
                   o_f, o_b, jnp.tile(gdn_norm_w[l], GDN_HEADS).reshape(1, 512), p,
                   w_hy_o, w_ssm_o, w_gdn_o, w_o, xs, mod)
        act = swiglu_up(rw, l, xa, norm2, mod, w_gu)
        xa = swiglu_down(rw, l, act, w_dn, xa, mod)
        xs = (xa, xa)

    out = final_norm(rw, xa, final_norm_w)
    return out.reshape(B, Ll, D)
```

```python
import functools
import math

import jax
import jax.numpy as jnp
import numpy as np
from jax import lax
from jax.experimental import pallas as pl
from jax.experimental.pallas import tpu as pltpu

F32 = jnp.float32
BF16 = jnp.bfloat16
HI = lax.Precision.HIGHEST

EPS = 1e-6
D_MODEL = 1024
GRID_W = 64

HY_WIDTH = 512
HY_BANDS = 16
HY_EMB = 1 + 2 * HY_BANDS
HY_HIDDEN = 64
HY_SHORT_DECAY_PCT = 0.3
HY_LONG_DECAY_PCT = 1.5
HY_TARGET = 1e-2

SSM_HEADS = 8
SSM_HEAD_DIM = 64
SSM_WIDTH = 512
SSM_GROUPS = 2
SSM_HPG = 4
SSM_STATE = 128
SSM_CHUNK = 128
SSM_GW = SSM_HPG * SSM_HEAD_DIM

GDN_HEADS = 4
GDN_DK = 128
GDN_DV = 128
GDN_CHUNK = 64

D_FF = 2816

C_HY = 0
C_Z = 1536
C_XBC = 2048
C_QKV = 3072
C_GG = 4608
C_GATE = 5120
C_SM = 8192

CONV_ROWS = 256
FREQ_BLK = 256

VMEM_LIMIT = 56 * 1024 * 1024


def _cp(*sem, flags=None):
    return pltpu.CompilerParams(dimension_semantics=sem, vmem_limit_bytes=VMEM_LIMIT, flags=flags)


def _sigmoid(x):
    return 1.0 / (1.0 + jnp.exp(-x))


def _silu(x):
    return x * _sigmoid(x)


def _softplus(x):
    return jnp.maximum(x, 0.0) + jnp.log1p(jnp.exp(-jnp.abs(x)))


def _dot(a, b, precision=None):
    return jnp.dot(a, b, precision=precision, preferred_element_type=F32)


def _dot_nt(a, b):
    return lax.dot_general(a, b, (((1,), (1,)), ((), ())), preferred_element_type=F32)


def _dot_tn(a, b):
    return lax.dot_general(a, b, (((0,), (0,)), ((), ())), preferred_element_type=F32)


def _ada_kernel(s_ref, w_ref, b_ref, o_ref):
    s = _silu(s_ref[...])
    o_ref[0] = _dot(s, w_ref[0], HI) + b_ref[0]


def ada_modulation(svec, w_ada, b_ada):
    depth = w_ada.shape[0]
    D = D_MODEL
    return pl.pallas_call(
        _ada_kernel,
        grid=(depth, 6),
        in_specs=[
            pl.BlockSpec((16, D), lambda l, j: (0, 0)),
            pl.BlockSpec((1, D, D), lambda l, j: (l, 0, j)),
            pl.BlockSpec((1, 1, D), lambda l, j: (l, 0, j)),
        ],
        out_specs=pl.BlockSpec((1, 16, D), lambda l, j: (l, 0, j)),
        out_shape=jax.ShapeDtypeStruct((depth, 16, 6 * D), F32),
        compiler_params=_cp("arbitrary", "arbitrary"),
        name="ada",
    )(svec, w_ada, b_ada.reshape(depth, 1, 6 * D))


def _norm_mod(x, nw, scale, shift):
    ms = jnp.mean(x * x, axis=-1, keepdims=True)
    return (x * lax.rsqrt(ms + EPS) * nw) * (1.0 + scale) + shift


IN_FLIGHT = 4
N_IN_PAD = C_SM + 128
IN_TN = N_IN_PAD // 5
MODE_RAW, MODE_CONV, MODE_CONV_SILU, MODE_CONV_SILU_L2, MODE_SMALL = range(5)


def _tile_mode(tile):
    col = tile * 128
    if col < C_Z:
        return MODE_CONV
    if col < C_XBC:
        return MODE_RAW
    if col < C_QKV:
        return MODE_CONV_SILU
    if col < C_QKV + 1024:
        return MODE_CONV_SILU_L2
    if col < C_GG:
        return MODE_CONV_SILU
    if col < C_SM:
        return MODE_RAW
    return MODE_SMALL
PAR_W0, PAR_W1, PAR_W2, PAR_BIAS, PAR_L2SCALE, PAR_SBIAS, PAR_SALOG, PAR_SKIND = range(8)


def _in_proj_kernel(xc_ref, xl_ref, nw_ref, mod_ref, w_ref, par_ref, o_ref, sm_ref, h_ref, raw0_ref, raw1_ref, *,
                    nctx_blk):
    j = pl.program_id(1)
    nj = N_IN_PAD // IN_TN
    raws = (raw0_ref, raw1_ref)

    @pl.when((j == 0) & (pl.program_id(0) < nctx_blk))
    def _():
        h = _norm_mod(xc_ref[...], nw_ref[...], mod_ref[0, 1:2, :], mod_ref[0, 0:1, :])
        h_ref[...] = h.astype(BF16)

    @pl.when((j == 0) & (pl.program_id(0) >= nctx_blk))
    def _():
        h = _norm_mod(xl_ref[...], nw_ref[...], mod_ref[0, 1:2, :], mod_ref[0, 0:1, :])
        h_ref[...] = h.astype(BF16)

    T = h_ref.shape[0]
    G = GRID_W
    per_ctx = CONV_ROWS // G
    is_latent = pl.program_id(0) >= nctx_blk
    sub = lax.broadcasted_iota(jnp.int32, (8, 128), 0)

    def raw_piece(src, g, c):
        return src[g * G:(g + 1) * G, c * 128:(c + 1) * 128]

    retired = []

    def retire(y):
        bits = pltpu.bitcast(y[0:8], jnp.int32)
        zero = lax.shift_right_logical(lax.shift_right_logical(bits, 16), 16)
        retired.append(jnp.tile(zero.astype(F32), (G // 8, 1)))

    def conv(src, g, c):
        cs = slice(c * 128, (c + 1) * 128)
        x = raw_piece(src, g, c)
        if len(retired) >= IN_FLIGHT:
            x = x + retired[-IN_FLIGHT]
        zero = jnp.zeros((1, 128), F32)
        before = zero if g % per_ctx == 0 else jnp.where(is_latent, 0.0, src[g * G - 1:g * G, cs])
        after = zero if g % per_ctx == per_ctx - 1 else jnp.where(is_latent, 0.0, src[(g + 1) * G:(g + 1) * G + 1, cs])
        rp = pltpu.roll(x, 1, 0)
        rn = pltpu.roll(x, G - 1, 0)
        prev = jnp.concatenate([jnp.where(sub == 0, before, rp[0:8]), rp[8:]], axis=0)
        nxt = jnp.concatenate([rn[:G - 8], jnp.where(sub == 7, after, rn[G - 8:])], axis=0)
        return (prev * par_ref[PAR_W0:PAR_W0 + 1, cs] + x * par_ref[PAR_W1:PAR_W1 + 1, cs]
                + nxt * par_ref[PAR_W2:PAR_W2 + 1, cs] + par_ref[PAR_BIAS:PAR_BIAS + 1, cs])

    def conv_silu(src, g, c):
        return _silu(conv(src, g, c))

    def conv_silu_l2(src, g, c):
        y = _silu(conv(src, g, c))
        y = y * lax.rsqrt(jnp.sum(y * y, axis=-1, keepdims=True) + EPS)
        return y * par_ref[PAR_L2SCALE:PAR_L2SCALE + 1, c * 128:(c + 1) * 128]

    def small(src, g, c):
        cs = slice(c * 128, (c + 1) * 128)
        acc = raw_piece(src, g, c)
        kind = par_ref[PAR_SKIND:PAR_SKIND + 1, cs]
        sp = _softplus(acc + par_ref[PAR_SBIAS:PAR_SBIAS + 1, cs])
        dec = -jnp.exp(par_ref[PAR_SALOG:PAR_SALOG + 1, cs]) * sp
        return jnp.where(kind == 0.0, sp, jnp.where(kind == 1.0, dec, jnp.where(kind == 2.0, _sigmoid(acc), 0.0)))

    rows_mm = 256
    tiles = IN_TN // 128
    piece_fn = {MODE_RAW: raw_piece, MODE_CONV: conv, MODE_CONV_SILU: conv_silu,
                MODE_CONV_SILU_L2: conv_silu_l2, MODE_SMALL: small}

    heavy_modes = (MODE_CONV_SILU, MODE_CONV_SILU_L2)
    col_slices = [slice(c0, min(c0 + 256, IN_TN)) for c0 in range(0, IN_TN, 256)]

    def project(dst, r, cs=slice(None)):
        rs = slice(r * rows_mm, (r + 1) * rows_mm)
        dst[rs, cs] = _dot(h_ref[rs, :], w_ref[:, cs])

    def finish(src, blk, g, c):
        mode = _tile_mode(blk * tiles + c)
        y = piece_fn[mode](src, g, c)
        if mode in heavy_modes:
            retire(y)
        if mode == MODE_SMALL:
            sm_ref[g * G:(g + 1) * G, :] = y
            y = jnp.zeros_like(y)
        o_ref[g * G:(g + 1) * G, c * 128:(c + 1) * 128] = y.astype(o_ref.dtype)

    for step in range(nj + 1):
        @pl.when(j == step)
        def _(step=step):
            blk = step - 1
            src, dst = raws[blk % 2], raws[step % 2]
            retired.clear()
            light = blk >= 0 and not any(_tile_mode(blk * tiles + c) in heavy_modes for c in range(tiles))
            for r in range(T // rows_mm):
                pieces = [] if blk < 0 else [(g, c) for g in range(r * rows_mm // G, (r + 1) * rows_mm // G)
                                             for c in range(tiles)]
                if step == nj:
                    slabs = []
                elif light:
                    slabs = col_slices
                else:
                    slabs = [slice(None)]
                per = -(-len(pieces) // max(len(slabs), 1))
                for n in range(max(len(slabs), 1)):
                    if n < len(slabs):
                        project(dst, r, slabs[n])
                    for g, c in pieces[n * per:(n + 1) * per]:
                        finish(src, blk, g, c)


class Rows:
    def __init__(self, B, Lc, Ll):
        self.B, self.Lc, self.Ll = B, Lc, Ll
        self.NC = B * Lc
        self.R = B * Lc + B * Ll
        assert self.NC % Ll == 0 or Ll % self.NC == 0
        tm = 1024
        while self.NC % tm or Ll % tm:
            tm //= 2
        self.tm = tm

    def mod_index(self, tm):
        nctx = self.NC // tm
        per = self.Ll // tm
        return lambda i: jnp.where(i < nctx, 0, 1 + (i - nctx) // per)


def _stream_specs(rw, tm, xs, ngrid):
    xc, xl = xs
    nctx = rw.NC // tm
    off = nctx if xl.shape[0] == rw.R else 0
    D = xc.shape[1]
    if ngrid == 1:
        return [pl.BlockSpec((tm, D), lambda i: (jnp.minimum(i, nctx - 1), 0)),
                pl.BlockSpec((tm, D), lambda i: (jnp.maximum(i - nctx, 0) + off, 0))]
    return [pl.BlockSpec((tm, D), lambda i, j: (jnp.minimum(i, nctx - 1), 0)),
            pl.BlockSpec((tm, D), lambda i, j: (jnp.maximum(i - nctx, 0) + off, 0))]


def in_proj(rw, xs, l, nw, mod, w, par):
    R = rw.R
    D = xs[0].shape[1]
    N = w.shape[2]
    tm, tn = rw.tm, IN_TN
    nj = N // tn
    assert N == N_IN_PAD
    mi = rw.mod_index(tm)
    done = lambda j: jnp.maximum(j - 1, 0)
    return pl.pallas_call(
        functools.partial(_in_proj_kernel, nctx_blk=rw.NC // tm),
        grid=(R // tm, nj + 1),
        in_specs=_stream_specs(rw, tm, xs, 2) + [
            pl.BlockSpec((None, 1, D), lambda i, j: (l, 0, 0)),
            pl.BlockSpec((None, 1, 8, D), lambda i, j: (l, mi(i), 0, 0)),
            pl.BlockSpec((None, D, tn), lambda i, j: (l, 0, jnp.minimum(j, nj - 1))),
            pl.BlockSpec((None, 8, tn), lambda i, j: (l, 0, done(j))),
        ],
        out_specs=[pl.BlockSpec((tm, tn), lambda i, j: (i, done(j))),
                   pl.BlockSpec((tm, 128), lambda i, j: (i, 0))],
        out_shape=[jax.ShapeDtypeStruct((R, N), BF16), jax.ShapeDtypeStruct((R, 128), F32)],
        scratch_shapes=[pltpu.VMEM((tm, D), BF16), pltpu.VMEM((tm, tn), F32), pltpu.VMEM((tm, tn), F32)],
        compiler_params=_cp("arbitrary", "arbitrary"),
        name="in_proj",
    )(xs[0], xs[1], nw, mod, w, par)


def _in_proj_params(hy_conv_w, hy_conv_b, ssm_conv_w, ssm_conv_b, gdn_conv_w, ssm_dt_bias, gdn_dt_bias, gdn_A_log):
    depth = hy_conv_w.shape[0]

    def row(pieces):
        out, pos = [], 0
        for off, a in pieces:
            out += [jnp.zeros((depth, off - pos), F32), a.astype(F32)]
            pos = off + a.shape[1]
        return jnp.concatenate(out + [jnp.zeros((depth, N_IN_PAD - pos), F32)], axis=1)
    z4 = jnp.zeros((depth, 4), F32)
    conv = [row([(C_HY, hy_conv_w[:, t]), (C_XBC, ssm_conv_w[:, t]), (C_QKV, gdn_conv_w[:, t])]) for t in range(3)]
    bias = row([(C_HY, hy_conv_b), (C_XBC, ssm_conv_b)])
    l2s = row([(C_QKV, jnp.full((depth, 512), GDN_DK ** -0.5, F32)), (C_QKV + 512, jnp.ones((depth, 512), F32))])
    sbias = row([(C_SM, jnp.concatenate([ssm_dt_bias.reshape(depth, 16), gdn_dt_bias[:, 0], z4,
                                         gdn_dt_bias[:, 1], z4], axis=1))])
    salog = row([(C_SM + 16, jnp.concatenate([gdn_A_log[:, 0], z4, gdn_A_log[:, 1], z4], axis=1))])
    kind = np.full((depth, N_IN_PAD), 3.0, np.float32)
    kind[:, C_SM:C_SM + 16] = 0.0
    kind[:, C_SM + 16:C_SM + 20] = 1.0
    kind[:, C_SM + 24:C_SM + 28] = 1.0
    kind[:, C_SM + 20:C_SM + 24] = 2.0
    kind[:, C_SM + 28:C_SM + 32] = 2.0
    return jnp.stack(conv + [bias, l2s, sbias, salog, jnp.asarray(kind)], axis=1)


def _hy_filter_kernel(z_ref, w1_ref, b1_ref, w2_ref, b2_ref, w3_ref, f0_ref, f1_ref, win_ref, o_ref, h_ref):
    @pl.when(pl.program_id(1) == 0)
    def _():
        h1 = jnp.sin(f0_ref[...] * (_dot(z_ref[...], w1_ref[...], HI) + b1_ref[...]))
        h_ref[...] = jnp.sin(f1_ref[...] * (_dot(h1, w2_ref[...], HI) + b2_ref[...]))

    h = _dot(h_ref[...], w3_ref[...], HI) * win_ref[...]
    tl = h.shape[0]
    row = lax.broadcasted_iota(jnp.int32, (tl, 1), 0) + pl.program_id(0) * tl
    drop = (row == 0) & (pl.program_id(1) % 2 == 1)
    o_ref[...] = jnp.where(drop, 0.0, h).astype(o_ref.dtype)


def hy_features(L):
    t = jnp.linspace(0.0, 1.0, L, dtype=F32)[:, None]
    w = 2.0 * math.pi * jnp.arange(L, dtype=F32)[:, None] / L
    f = jnp.linspace(1e-4, HY_BANDS - 1, HY_BANDS, dtype=F32)[None, :]
    z = jnp.concatenate([t, jnp.cos(f * w), -jnp.sin(f * w)], axis=-1)
    z = jnp.pad(z, ((0, 0), (0, 128 - HY_EMB)))
    min_decay = math.log(HY_TARGET) / HY_LONG_DECAY_PCT
    max_decay = math.log(HY_TARGET) / HY_SHORT_DECAY_PCT
    deltas = jnp.linspace(min_decay, max_decay, HY_WIDTH, dtype=F32)
    window = jnp.exp(-t * jnp.abs(deltas))
    return z, window


def hy_filter(feat, w1, b1, w2, b2, w3, freq):
    z, window = feat
    L = z.shape[0]
    H = HY_HIDDEN
    w1p = jnp.pad(w1, ((0, 128 - HY_EMB), (0, 128 - H)))
    w2p = jnp.pad(w2, ((0, 128 - H), (0, 128 - H)))
    w3p = jnp.pad(w3, ((0, 128 - H), (0, 0)))
    pad1 = lambda v: jnp.pad(v, (0, 128 - H)).reshape(1, 128)
    tl = 256
    full = lambda shape: pl.BlockSpec(shape, lambda i, j: (0, 0))
    return pl.pallas_call(
        _hy_filter_kernel,
        grid=(L // tl, 4),
        in_specs=[
            pl.BlockSpec((tl, 128), lambda i, j: (i, 0)),
            full((128, 128)), full((1, 128)), full((128, 128)), full((1, 128)),
            pl.BlockSpec((128, HY_WIDTH), lambda i, j: (0, j)),
            full((1, 128)), full((1, 128)),
            pl.BlockSpec((tl, HY_WIDTH), lambda i, j: (i, 0)),
        ],
        out_specs=pl.BlockSpec((tl, HY_WIDTH), lambda i, j: (i, j)),
        out_shape=jax.ShapeDtypeStruct((L, 4 * HY_WIDTH), BF16),
        scratch_shapes=[pltpu.VMEM((tl, 128), F32)],
        compiler_params=_cp("arbitrary", "arbitrary"),
        name="hy_filter",
    )(z, w1p, pad1(b1), w2p, pad1(b2), w3p, pad1(freq[0]), pad1(freq[1]), window)


def dft_table(L):
    N = 2 * L
    H = L // 2
    q = np.arange(L, dtype=np.int64)
    f = np.where(q < H, q, L + H - q)[:, None]
    s = np.arange(L, dtype=np.int64)[None, :]
    ang = ((f * s) % N).astype(np.float64) * (2.0 * math.pi / N)
    mid = ((H * s) % N).astype(np.float64) * (2.0 * math.pi / N)
    q = q[:, None]
    imag = np.where(q == 0, np.cos(mid), np.where(q == H, -np.sin(mid), -np.sin(ang)))
    return jnp.asarray(np.concatenate([np.cos(ang), imag], axis=0), dtype=BF16)


def dft_tables_split(L):
    N = 2 * L
    H = L // 2
    q = np.arange(H, dtype=np.int64)[:, None]
    m = np.arange(H, dtype=np.int64)[None, :]
    ang_e = ((q * 2 * m) % N).astype(np.float64) * (2.0 * math.pi / N)
    ang_o = ((q * (2 * m + 1)) % N).astype(np.float64) * (2.0 * math.pi / N)
    alt = (1 - 2 * (m % 2)).astype(np.float64)
    ce, co = np.cos(ang_e), np.cos(ang_o)
    se = np.where(q == 0, alt, -np.sin(ang_e))
    so = np.where(q == 0, -alt, -np.sin(ang_o))
    w = np.where(q == 0, 1.0, 2.0) / N
    ise = np.where(q == 0, 2.0 / N * alt, -np.sin(ang_e) * w)
    iso = np.where(q == 0, -2.0 / N * alt, -np.sin(ang_o) * w)
    fwd = np.stack([ce, co, se, so])
    inv = np.stack([(ce * w).T, ise.T, (co * w).T, iso.T])
    return jnp.asarray(fwd, dtype=BF16), jnp.asarray(inv, dtype=BF16)


def _matmul_kernel(a_ref, b_ref, o_ref):
    o_ref[...] = _dot(a_ref[...], b_ref[...])


def matmul(a, b, tm, tn):
    M, K = a.shape
    N = b.shape[1]
    return pl.pallas_call(
        _matmul_kernel,
        grid=(M // tm, N // tn),
        in_specs=[pl.BlockSpec((tm, K), lambda i, j: (i, 0)), pl.BlockSpec((K, tn), lambda i, j: (0, j))],
        out_specs=pl.BlockSpec((tm, tn), lambda i, j: (i, j)),
        out_shape=jax.ShapeDtypeStruct((M, N), F32),
        compiler_params=_cp("arbitrary", "arbitrary"),
        name="matmul",
    )(a, b)


def _long_conv_kernel(u_ref, g_ref, bias_ref, fwd_ref, inv_ref, ar0_ref, ar1_ref, ar0h_ref, ar1h_ref,
                      ai0_ref, ai1_ref, ai0h_ref, ai1h_ref, o_ref, ue_ref, uo_ref, acce_ref, acco_ref, y_ref):
    f = pl.program_id(1)
    half = ue_ref.shape[0]

    lane_tiles = [slice(c * 128, (c + 1) * 128) for c in range(y_ref.shape[0])]

    @pl.when(f == 0)
    def _():
        for c, cs in enumerate(lane_tiles):
            y_c = y_ref.at[c]
            y_c[...] = u_ref[:, cs].astype(F32)
            ue_ref[:, cs] = y_c[pl.ds(0, half, stride=2), :].astype(BF16)
            uo_ref[:, cs] = y_c[pl.ds(1, half, stride=2), :].astype(BF16)
        acce_ref[...] = jnp.zeros_like(acce_ref)
        acco_ref[...] = jnp.zeros_like(acco_ref)

    ue, uo = ue_ref[...], uo_ref[...]
    ae, ao = _dot(fwd_ref[0], ue), _dot(fwd_ref[1], uo)
    be, bo = _dot(fwd_ref[2], ue), _dot(fwd_ref[3], uo)
    ur, ur2 = ae + ao, ae - ao
    ui, ui2 = be + bo, bo - be
    first = (lax.broadcasted_iota(jnp.int32, (fwd_ref.shape[1], 1), 0) == 0) & (f == 0)
    kr, kr2 = ar0_ref[...] + ar1_ref[...], ar0h_ref[...] + ar1h_ref[...]
    ki = jnp.where(first, ai0_ref[...] + ai1_ref[...], ai0_ref[...] - ai1_ref[...])
    ki2 = ai0h_ref[...] - ai1h_ref[...]
    pr, pi = ur * kr - ui * ki, ur * ki + ui * kr
    pr2, pi2 = ur2 * kr2 - ui2 * ki2, ur2 * ki2 + ui2 * kr2
    dc, ny = ur * kr, ur2 * kr2
    gr = jnp.where(first, dc + ny, pr + pr2)
    gi = jnp.where(first, be * ki - bo * ki2, pi - pi2)
    hr = jnp.where(first, dc - ny, pr - pr2)
    hi = jnp.where(first, be * ki2 + bo * ki, pi + pi2)
    acce_ref[...] += _dot(inv_ref[0], gr.astype(BF16)) + _dot(inv_ref[1], gi.astype(BF16))
    acco_ref[...] += _dot(inv_ref[2], hr.astype(BF16)) + _dot(inv_ref[3], hi.astype(BF16))

    @pl.when(f == pl.num_programs(1) - 1)
    def _():
        for c, cs in enumerate(lane_tiles):
            y_c = y_ref.at[c]
            y_c[pl.ds(0, half, stride=2), :] = acce_ref[:, cs]
            y_c[pl.ds(1, half, stride=2), :] = acco_ref[:, cs]
            u = u_ref[:, cs].astype(F32)
            o_ref[:, cs] = (g_ref[:, cs].astype(F32) * (y_c[...] + u * bias_ref[:, cs])).astype(o_ref.dtype)


def long_conv(B, L, u, u_rb0, u_cb, gate, g_rb0, gate_cb, bias, fwd, inv, kspec, order, out_dtype):
    C = HY_WIDTH
    H = L // 2
    FB = min(FREQ_BLK, H)
    nfb = H // FB
    kblk = lambda part, d: pl.BlockSpec((FB, C), lambda b, f: (part * nfb + f, 2 * order + d))
    return pl.pallas_call(
        _long_conv_kernel,
        grid=(B, nfb),
        in_specs=[
            pl.BlockSpec((L, C), lambda b, f: (u_rb0 + b, u_cb)),
            pl.BlockSpec((L, C), lambda b, f: (g_rb0 + b, gate_cb)),
            pl.BlockSpec((1, C), lambda b, f: (0, 0)),
            pl.BlockSpec((4, FB, H), lambda b, f: (0, f, 0)),
            pl.BlockSpec((4, H, FB), lambda b, f: (0, 0, f)),
            kblk(0, 0), kblk(0, 1), kblk(1, 0), kblk(1, 1), kblk(2, 0), kblk(2, 1), kblk(3, 0), kblk(3, 1),
        ],
        out_specs=pl.BlockSpec((L, C), lambda b, f: (b, 0)),
        out_shape=jax.ShapeDtypeStruct((B * L, C), out_dtype),
        scratch_shapes=[pltpu.VMEM((H, C), BF16), pltpu.VMEM((H, C), BF16),
                        pltpu.VMEM((H, C), F32), pltpu.VMEM((H, C), F32), pltpu.VMEM((C // 128, L, 128), F32)],
        compiler_params=_cp("arbitrary", "arbitrary"),
        name="long_conv",
    )(u, gate, bias.reshape(1, C), fwd, inv, *([kspec] * 8))


def _scan_blocks(rw, rows):
    nbc, nbl, base = rw.Lc // rows, rw.Ll // rows, rw.NC // rows

    def make(d):
        def f(b, s):
            jc = s if d == 0 else nbc - 1 - s
            jl = (s - nbc) if d == 0 else nbl - 1 - (s - nbc)
            return jnp.where(s < nbc, b * nbc + jc, base + b * nbl + jl)
        return f

    return [make(0), make(1)], nbc + nbl


def _expand_lanes(x, base, n, width):
    rows = x.shape[0]
    per = 128 // width
    lane = lax.broadcasted_iota(jnp.int32, (rows, 128), 1)
    tiles = []
    for t in range(n // per):
        c0 = base + t * per
        tile = jnp.broadcast_to(x[:, c0:c0 + 1], (rows, 128))
        for i in range(1, per):
            tile = jnp.where(lane >= i * width, jnp.broadcast_to(x[:, c0 + i:c0 + i + 1], (rows, 128)), tile)
        tiles.append(tile)
    return jnp.concatenate(tiles, axis=1)


def _ssd_kernel(xf, bf, cf, smf, dtf, xb, bb, cb_, smb, dtb, alx_ref, alc_ref, of_ref, ob_ref, h_ref):
    Q = SSM_CHUNK
    GW = SSM_GW

    @pl.when(pl.program_id(1) == 0)
    def _():
        h_ref[...] = jnp.zeros_like(h_ref)

    row = lax.broadcasted_iota(jnp.int32, (Q, Q), 0)
    col = lax.broadcasted_iota(jnp.int32, (Q, Q), 1)
    lane_head = lax.broadcasted_iota(jnp.int32, (Q, GW), 1) // SSM_HEAD_DIM
    dirs = ((xf, bf, cf, smf, dtf, of_ref), (xb, bb, cb_, smb, dtb, ob_ref))
    jobs = []
    for d in range(2):
        x_ref, b_ref, c_ref, sm_ref, dt_ref, o_ref = dirs[d]
        keep = (col <= row) if d == 0 else (col >= row)
        tri = keep.astype(BF16)
        tri_t = ((row <= col) if d == 0 else (row >= col)).astype(BF16)
        sm = sm_ref[...]
        a_x = -jnp.exp(alx_ref[d])
        dtx = _expand_lanes(sm, 8 * d, SSM_HEADS, SSM_HEAD_DIM)
        cumx = _expand_lanes(_dot_01_lhs(tri, sm), 8 * d, SSM_HEADS, SSM_HEAD_DIM) * a_x
        cumr = _dot_01_rhs(dt_ref[0], tri_t) * (-jnp.exp(alc_ref[d]))
        last = Q - 1 if d == 0 else 0
        totx = cumx[last:last + 1, :]
        xd = x_ref[...].astype(F32) * dtx
        xdw = xd * jnp.exp(totx - cumx)
        ecum = jnp.exp(cumx)
        for g in range(SSM_GROUPS):
            gs = slice(g * GW, (g + 1) * GW)
            jobs.append(dict(d=d, g=g, gs=gs, keep=keep, cumx=cumx, cumr=cumr, o_ref=o_ref,
                             bg=b_ref[:, g * SSM_STATE:(g + 1) * SSM_STATE].astype(BF16),
                             cg=c_ref[:, g * SSM_STATE:(g + 1) * SSM_STATE].astype(BF16),
                             xdg=xd[:, gs], xdw=xdw[:, gs].astype(BF16), ecum=ecum[:, gs],
                             etot=jnp.exp(totx[:, gs])))
    for j in jobs:
        j["cb"] = _dot_nt(j["cg"], j["bg"])
        j["h"] = h_ref[j["d"], j["g"]]
    for j in jobs:
        ms, xs = [], []
        for e4 in range(SSM_HPG):
            e = j["g"] * SSM_HPG + e4
            diff = j["cumx"][:, e * SSM_HEAD_DIM:e * SSM_HEAD_DIM + 1] - j["cumr"][e:e + 1, :]
            ms.append((j["cb"] * jnp.where(j["keep"], jnp.exp(diff), 0.0)).astype(BF16))
            xs.append(jnp.where(lane_head == e4, j["xdg"], 0.0).astype(BF16))
        yd = _dot(jnp.concatenate(ms, axis=1), jnp.concatenate(xs, axis=0))
        y_off = _dot(j["cg"], j["h"].astype(BF16)) * j["ecum"]
        j["o_ref"][:, j["gs"]] = (yd + y_off).astype(BF16)
    for j in jobs:
        h_ref[j["d"], j["g"]] = j["h"] * j["etot"] + _dot_tn(j["bg"], j["xdw"])


def ssd_scan(rw, p, sm, dtT, alx, alc):
    Q = SSM_CHUNK
    blks, nsteps = _scan_blocks(rw, Q)
    R = p.shape[0]
    in_specs = []
    for d in range(2):
        f = blks[d]
        in_specs += [
            pl.BlockSpec((Q, 512), lambda b, s, f=f: (f(b, s), C_XBC // 512)),
            pl.BlockSpec((Q, 256), lambda b, s, f=f: (f(b, s), C_XBC // 256 + 2)),
            pl.BlockSpec((Q, 256), lambda b, s, f=f: (f(b, s), C_XBC // 256 + 3)),
            pl.BlockSpec((Q, 128), lambda b, s, f=f: (f(b, s), 0)),
            pl.BlockSpec((1, 8, Q), lambda b, s, f=f, d=d: (d, 0, f(b, s))),
        ]
    in_specs += [pl.BlockSpec((2, 1, 512), lambda b, s: (0, 0, 0)), pl.BlockSpec((2, 8, 1), lambda b, s: (0, 0, 0))]
    ops = (p, p, p, sm, dtT)
    return pl.pallas_call(
        _ssd_kernel,
        grid=(rw.B, nsteps),
        in_specs=in_specs,
        out_specs=[pl.BlockSpec((Q, 512), lambda b, s, f=blks[d]: (f(b, s), 0)) for d in range(2)],
        out_shape=[jax.ShapeDtypeStruct((R, 512), BF16)] * 2,
        scratch_shapes=[pltpu.VMEM((2, SSM_GROUPS, SSM_STATE, SSM_GW), F32)],
        compiler_params=_cp("arbitrary", "arbitrary"),
        name="ssd_scan",
    )(*ops, *ops, alx, alc)


def _split3(x):
    x1 = x.astype(BF16)
    r = x - x1.astype(F32)
    x2 = r.astype(BF16)
    x3 = (r - x2.astype(F32)).astype(BF16)
    return x1, x2, x3


def _dot_01_lhs(m01, x):
    x1, x2, x3 = _split3(x)
    return _dot(m01, x1) + _dot(m01, x2) + _dot(m01, x3)


def _dot_01_rhs(x, m01):
    x1, x2, x3 = _split3(x)
    return _dot(x1, m01) + _dot(x2, m01) + _dot(x3, m01)


GDN_ROWS = 256


def _gdn_prep_kernel(q_ref, k_ref, v_ref, sm_ref, gT_ref, u_ref, w_ref, qg_ref, kd_ref, qk_ref, egl_ref):
    C = GDN_CHUNK
    row = lax.broadcasted_iota(jnp.int32, (C, C), 0)
    col = lax.broadcasted_iota(jnp.int32, (C, C), 1)
    jobs = []
    levels = []
    for d in range(2):
        keep = (col <= row) if d == 0 else (col >= row)
        late, early = (row, col) if d == 0 else (col, row)
        levels.append([(((row ^ col) >> (j + 1)) == 0) & ((late & (1 << j)) != 0) & ((early & (1 << j)) == 0)
                       for j in range(6)])
        tri = keep.astype(BF16)
        tri_t = ((row <= col) if d == 0 else (row >= col)).astype(BF16)
        last = C - 1 if d == 0 else 0
        for c in range(GDN_ROWS // C):
            rows = slice(c * C, (c + 1) * C)
            smc = sm_ref[rows, :]
            cums = _dot_01_lhs(tri, smc)
            cumr = _dot_01_rhs(gT_ref[c, 8 * d:8 * d + 8, :], tri_t)
            tot = cums[last:last + 1, :]
            for h in range(GDN_HEADS):
                lg = 16 + 8 * d + h
                jobs.append(dict(d=d, c=c, h=h, rows=rows, hs=slice(h * 128, (h + 1) * 128), keep=keep,
                                 gc=cums[:, lg:lg + 1], beta=smc[:, lg + 4:lg + 5],
                                 gl=tot[:, lg:lg + 1], gr=cumr[h:h + 1, :]))
    for j in jobs:
        q = q_ref[j["rows"], j["hs"]].astype(F32)
        k = k_ref[j["rows"], j["hs"]].astype(F32)
        j["dec"] = jnp.where(j["keep"], jnp.exp(j["gc"] - j["gr"]), 0.0)
        kb = k * j["beta"]
        both = _dot_nt(jnp.concatenate([kb, q], axis=0).astype(BF16), k.astype(BF16))
        j["a"] = both[:C] * j["dec"]
        j["n"] = -jnp.where(levels[j["d"]][0], j["a"], 0.0)
        qk_ref[j["d"], j["c"], j["h"]] = (both[C:] * j["dec"]).astype(BF16)
    for lev in range(1, 6):
        for j in jobs:
            l = jnp.where(levels[j["d"]][lev], j["a"], 0.0)
            j["y"] = l + _dot(l.astype(BF16), j["n"].astype(BF16))
        for j in jobs:
            j["n"] = j["n"] - j["y"] - _dot(j["n"].astype(BF16), j["y"].astype(BF16))
    for j in jobs:
        d, rows, hs, gc, gl, beta = j["d"], j["rows"], j["hs"], j["gc"], j["gl"], j["beta"]
        q = q_ref[rows, hs].astype(F32)
        k = k_ref[rows, hs].astype(F32)
        eg = jnp.exp(gc)
        rhs = jnp.concatenate([v_ref[rows, hs].astype(F32) * beta, k * beta * eg], axis=1)
        sol = rhs + _dot(j["n"].astype(BF16), rhs.astype(BF16))
        u_ref[d, rows, hs] = sol[:, :GDN_DV].astype(BF16)
        w_ref[d, rows, hs] = sol[:, GDN_DV:].astype(BF16)
        qg_ref[d, rows, hs] = (q * eg).astype(BF16)
        kd_ref[d, rows, hs] = (k * jnp.exp(gl - gc)).astype(BF16)
        egl_ref[d, j["c"], :, hs] = jnp.broadcast_to(jnp.exp(gl), (8, 128))


def gdn_prep(p, sm, gT):
    R = p.shape[0]
    T, C = GDN_ROWS, GDN_CHUNK
    nc = T // C
    col = lambda k: pl.BlockSpec((T, 512), lambda i: (i, C_QKV // 512 + k))
    dirrow = pl.BlockSpec((2, T, 512), lambda i: (0, i, 0))
    return pl.pallas_call(
        _gdn_prep_kernel,
        grid=(R // T,),
        in_specs=[col(0), col(1), col(2),
                  pl.BlockSpec((T, 128), lambda i: (i, 0)),
                  pl.BlockSpec((nc, 16, C), lambda i: (i, 0, 0))],
        out_specs=[dirrow, dirrow, dirrow, dirrow,
                   pl.BlockSpec((2, nc, GDN_HEADS, C, C), lambda i: (0, i, 0, 0, 0)),
                   pl.BlockSpec((2, nc, 8, 512), lambda i: (0, i, 0, 0))],
        out_shape=[jax.ShapeDtypeStruct((2, R, 512), BF16),
                   jax.ShapeDtypeStruct((2, R, 512), BF16),
                   jax.ShapeDtypeStruct((2, R, 512), BF16),
                   jax.ShapeDtypeStruct((2, R, 512), BF16),
                   jax.ShapeDtypeStruct((2, R // C, GDN_HEADS, C, C), BF16),
                   jax.ShapeDtypeStruct((2, R // C, 8, 512), F32)],
        compiler_params=_cp("arbitrary"),
        name="gdn_prep",
    )(p, p, p, sm, gT)


def _gdn_scan_kernel(uf, wf, qgf, kdf, qkf, eglf, ub, wb, qgb, kdb, qkb, eglb, of_ref, ob_ref, s_ref):
    C = GDN_CHUNK
    nch = GDN_ROWS // C

    @pl.when(pl.program_id(1) == 0)
    def _():
        s_ref[...] = jnp.zeros_like(s_ref)

    dirs = ((uf, wf, qgf, kdf, qkf, eglf, of_ref), (ub, wb, qgb, kdb, qkb, eglb, ob_ref))
    chains = [(d, h) for d in range(2) for h in range(GDN_HEADS)]
    S = {ch: s_ref[ch[0], ch[1]] for ch in chains}
    for i in range(nch):
        Sb, vnb, rows_of, c_of = {}, {}, {}, {}
        for d, h in chains:
            c_of[d] = i if d == 0 else nch - 1 - i
            rows_of[d] = slice(c_of[d] * C, (c_of[d] + 1) * C)
        for d, h in chains:
            hs = slice(h * 128, (h + 1) * 128)
            Sb[d, h] = S[d, h].astype(BF16)
            v_new = dirs[d][0][0, rows_of[d], hs].astype(F32) - _dot(dirs[d][1][0, rows_of[d], hs], Sb[d, h])
            vnb[d, h] = v_new.astype(BF16)
        for d, h in chains:
            hs = slice(h * 128, (h + 1) * 128)
            u_ref, w_ref, qg_ref, kd_ref, qk_ref, egl_ref, o_ref = dirs[d]
            S[d, h] = S[d, h] * egl_ref[0, c_of[d], 0:1, hs] + _dot_tn(kd_ref[0, rows_of[d], hs], vnb[d, h])
        for d, h in chains:
            hs = slice(h * 128, (h + 1) * 128)
            u_ref, w_ref, qg_ref, kd_ref, qk_ref, egl_ref, o_ref = dirs[d]
            o_ref[rows_of[d], hs] = (_dot(qg_ref[0, rows_of[d], hs], Sb[d, h])
                                     + _dot(qk_ref[0, c_of[d], h], vnb[d, h])).astype(BF16)
    for ch in chains:
        s_ref[ch[0], ch[1]] = S[ch]


def gdn_scan(rw, u, w, qg, kd, qk, egl):
    T, C = GDN_ROWS, GDN_CHUNK
    nc = T // C
    R = u.shape[1]
    nbc, nbl, base = rw.Lc // T, rw.Ll // T, rw.NC // T

    def blk(d):
        def f(b, s):
            jc = s if d == 0 else nbc - 1 - s
            jl = (s - nbc) if d == 0 else nbl - 1 - (s - nbc)
            return jnp.where(s < nbc, b * nbc + jc, base + b * nbl + jl)
        return f

    in_specs = []
    for d in range(2):
        f = blk(d)
        rowspec = pl.BlockSpec((1, T, 512), lambda b, s, f=f, d=d: (d, f(b, s), 0))
        in_specs += [rowspec, rowspec, rowspec, rowspec,
                     pl.BlockSpec((1, nc, GDN_HEADS, C, C), lambda b, s, f=f, d=d: (d, f(b, s), 0, 0, 0)),
                     pl.BlockSpec((1, nc, 8, 512), lambda b, s, f=f, d=d: (d, f(b, s), 0, 0))]
    out_specs = [pl.BlockSpec((T, 512), lambda b, s, f=blk(d): (f(b, s), 0)) for d in range(2)]
    ops = (u, w, qg, kd, qk, egl)
    return pl.pallas_call(
        _gdn_scan_kernel,
        grid=(rw.B, nbc + nbl),
        in_specs=in_specs,
        out_specs=out_specs,
        out_shape=[jax.ShapeDtypeStruct((R, 512), BF16)] * 2,
        scratch_shapes=[pltpu.VMEM((2, GDN_HEADS, GDN_DK, GDN_DV), F32)],
        compiler_params=_cp("arbitrary", "arbitrary"),
        name="gdn_scan",
    )(*ops, *ops)


def _merge_kernel(yhc_ref, yhl_ref, sf_ref, sb_ref, sx_ref, sz_ref, dx_ref, snw_ref, gf_ref, gb_ref, gg_ref, gnw_ref,
                  g0_ref, g1_ref, g2_ref, w0_ref, w1_ref, w2_ref, wo_ref, xc_ref, xl_ref, mod_ref, o_ref,
                  ys_ref, yg_ref, *, nctx_blk):
    tm = xc_ref.shape[0]
    rp = 64
    for r in range(tm // rp):
        rs = slice(r * rp, (r + 1) * rp)
        y = (sf_ref[rs, :].astype(F32) + sb_ref[rs, :].astype(F32)
             + sx_ref[rs, :].astype(F32) * dx_ref[...])
        y = y * _silu(sz_ref[rs, :].astype(F32))
        parts = []
        for g in range(SSM_GROUPS):
            yg = y[:, g * SSM_GW:(g + 1) * SSM_GW]
            parts.append(yg * lax.rsqrt(jnp.mean(yg * yg, axis=-1, keepdims=True) + EPS))
        ys_ref[rs, :] = (jnp.concatenate(parts, axis=1) * snw_ref[...]).astype(BF16)
        o = gf_ref[rs, :].astype(F32) + gb_ref[rs, :].astype(F32)
        parts = []
        for h in range(GDN_HEADS):
            oh = o[:, h * 128:(h + 1) * 128]
            parts.append(oh * lax.rsqrt(jnp.mean(oh * oh, axis=-1, keepdims=True) + EPS))
        yg_ref[rs, :] = (jnp.concatenate(parts, axis=1) * gnw_ref[...]
                         * _silu(gg_ref[rs, :].astype(F32))).astype(BF16)
    is_ctx = pl.program_id(0) < nctx_blk
    yh = jnp.where(is_ctx, yhc_ref[...], yhl_ref[...])
    m = (_sigmoid(g0_ref[...].astype(F32)) * _dot(yh, w0_ref[...])
         + _sigmoid(g1_ref[...].astype(F32)) * _dot(ys_ref[...], w1_ref[...])
         + _sigmoid(g2_ref[...].astype(F32)) * _dot(yg_ref[...], w2_ref[...]))
    x = jnp.where(is_ctx, xc_ref[...], xl_ref[...])
    o_ref[...] = x + mod_ref[0, 2:3, :] * _dot(m.astype(BF16), wo_ref[...])


def merge(rw, l, yh, y_f, y_b, dx, ssm_nw, o_f, o_b, gdn_nw, p, w0, w1, w2, wo, xs, mod):
    R = rw.R
    D = xs[0].shape[1]
    tm = min(rw.tm, 512)
    mi = rw.mod_index(tm)
    yspec = pl.BlockSpec((tm, 512), lambda i: (i, 0))
    pspec = lambda col: pl.BlockSpec((tm, 512), lambda i: (i, col // 512))
    vec = pl.BlockSpec((1, 512), lambda i: (0, 0))
    gspec = lambda k: pl.BlockSpec((tm, D), lambda i: (i, C_GATE // D + k))
    wspec = pl.BlockSpec((None, 512, D), lambda i: (l, 0, 0))
    return pl.pallas_call(
        functools.partial(_merge_kernel, nctx_blk=rw.NC // tm),
        grid=(R // tm,),
        in_specs=_stream_specs(rw, tm, yh, 1) + [
                  yspec, yspec, pspec(C_XBC), pspec(C_Z), vec, vec,
                  yspec, yspec, pspec(C_GG), vec,
                  gspec(0), gspec(1), gspec(2), wspec, wspec, wspec,
                  pl.BlockSpec((None, D, D), lambda i: (l, 0, 0))]
                 + _stream_specs(rw, tm, xs, 1)
                 + [pl.BlockSpec((None, 1, 8, D), lambda i: (l, mi(i), 0, 0))],
        out_specs=pl.BlockSpec((tm, D), lambda i: (i, 0)),
        out_shape=jax.ShapeDtypeStruct((R, D), F32),
        scratch_shapes=[pltpu.VMEM((tm, 512), BF16), pltpu.VMEM((tm, 512), BF16)],
        compiler_params=_cp("arbitrary"),
        name="merge",
    )(yh[0], yh[1], y_f, y_b, p, p, dx, ssm_nw, o_f, o_b, p, gdn_nw, p, p, p, w0, w1, w2, wo, xs[0], xs[1], mod)


def _swiglu_up_kernel(x_ref, nw_ref, mod_ref, wg_ref, wu_ref, o_ref, h_ref, g0_ref, g1_ref, u0_ref, u1_ref):
    @pl.when(pl.program_id(1) == 0)
    def _():
        h = _norm_mod(x_ref[...], nw_ref[...], mod_ref[0, 4:5, :], mod_ref[0, 3:4, :])
        h_ref[...] = h.astype(BF16)

    T, tn = o_ref.shape
    rows = g0_ref.shape[0]
    gs, us = (g0_ref, g1_ref), (u0_ref, u1_ref)

    def project(r):
        hh = h_ref[r * rows:(r + 1) * rows, :]
        gs[r % 2][...] = _dot(hh, wg_ref[...])
        us[r % 2][...] = _dot(hh, wu_ref[...])

    def finish(r):
        for q in range(rows // 64):
            for c in range(tn // 128):
                ps = (slice(q * 64, (q + 1) * 64), slice(c * 128, (c + 1) * 128))
                y = _silu(gs[r % 2][ps]) * us[r % 2][ps]
                o_ref[r * rows + q * 64:r * rows + (q + 1) * 64, ps[1]] = y.astype(o_ref.dtype)

    for r in range(T // rows):
        project(r)
        if r > 0:
            finish(r - 1)
    finish(T // rows - 1)


def swiglu_up(rw, l, x, nw, mod, wgu):
    R, D = x.shape
    tm = rw.tm
    tn = D_FF // 2
    nj = D_FF // tn
    mi = rw.mod_index(tm)
    return pl.pallas_call(
        _swiglu_up_kernel,
        grid=(R // tm, nj),
        in_specs=[
            pl.BlockSpec((tm, D), lambda i, j: (i, 0)),
            pl.BlockSpec((None, 1, D), lambda i, j: (l, 0, 0)),
            pl.BlockSpec((None, 1, 8, D), lambda i, j: (l, mi(i), 0, 0)),
            pl.BlockSpec((None, D, tn), lambda i, j: (l, 0, j)),
            pl.BlockSpec((None, D, tn), lambda i, j: (l, 0, nj + j)),
        ],
        out_specs=pl.BlockSpec((tm, tn), lambda i, j: (i, j)),
        out_shape=jax.ShapeDtypeStruct((R, D_FF), BF16),
        scratch_shapes=[pltpu.VMEM((tm, D), BF16)] + [pltpu.VMEM((min(256, tm), tn), F32)] * 4,
        compiler_params=_cp("arbitrary", "arbitrary"),
        name="swiglu_up",
    )(x, nw, mod, wgu, wgu)


def _swiglu_down_kernel(a_ref, w_ref, x_ref, mod_ref, o_ref):
    o_ref[...] = x_ref[...] + mod_ref[0, 5:6, :] * _dot(a_ref[...], w_ref[...])


def swiglu_down(rw, l, a, w, x, mod):
    R, D = x.shape
    tm = min(rw.tm, 512)
    mi = rw.mod_index(tm)
    return pl.pallas_call(
        _swiglu_down_kernel,
        grid=(R // tm,),
        in_specs=[
            pl.BlockSpec((tm, D_FF), lambda i: (i, 0)),
            pl.BlockSpec((None, D_FF, D), lambda i: (l, 0, 0)),
            pl.BlockSpec((tm, D), lambda i: (i, 0)),
            pl.BlockSpec((None, 1, 8, D), lambda i: (l, mi(i), 0, 0)),
        ],
        out_specs=pl.BlockSpec((tm, D), lambda i: (i, 0)),
        out_shape=jax.ShapeDtypeStruct((R, D), F32),
        compiler_params=_cp("arbitrary"),
        name="swiglu_down",
    )(a, w, x, mod)


def _final_norm_kernel(x_ref, w_ref, o_ref):
    x = x_ref[...]
    ms = jnp.mean(x * x, axis=-1, keepdims=True)
    o_ref[...] = x * lax.rsqrt(ms + EPS) * w_ref[...]


def final_norm(rw, x, w):
    D = x.shape[1]
    tm = rw.tm
    n0 = rw.NC // tm
    nl = rw.B * rw.Ll
    return pl.pallas_call(
        _final_norm_kernel,
        grid=(nl // tm,),
        in_specs=[pl.BlockSpec((tm, D), lambda i: (n0 + i, 0)), pl.BlockSpec((1, D), lambda i: (0, 0))],
        out_specs=pl.BlockSpec((tm, D), lambda i: (i, 0)),
        out_shape=jax.ShapeDtypeStruct((nl, D), F32),
        compiler_params=_cp("arbitrary"),
        name="final_norm",
    )(x, w.reshape(1, D))


def _regroup_w_in(w_in):
    o_dt = 3072
    o_gdn = 3088
    o_a = o_gdn + 2048
    o_b = o_a + 8
    o_gate = o_gdn + 2064
    wt = jnp.swapaxes(w_in, 1, 2).astype(BF16)
    pieces = [
        wt[:, 0:3072],
        wt[:, o_gdn:o_gdn + 2048],
        wt[:, o_gate:o_gate + 3072],
        wt[:, o_dt:o_dt + 16],
        wt[:, o_a:o_a + 4], wt[:, o_b:o_b + 4],
        wt[:, o_a + 4:o_a + 8], wt[:, o_b + 4:o_b + 8],
        jnp.zeros((wt.shape[0], N_IN_PAD - C_SM - 32, wt.shape[2]), wt.dtype),
    ]
    return jnp.swapaxes(jnp.concatenate(pieces, axis=1), 1, 2)


def kernel(x, c, ctx, c_ctx, w_ada, b_ada, norm1_w, norm2_w, w_in, hy_conv_w, hy_conv_b, hy_w1, hy_b1, hy_w2, hy_b2, hy_w3, hy_freq, hy_bias, ssm_conv_w, ssm_conv_b, ssm_dt_bias, ssm_A_log, ssm_D, ssm_norm_w, gdn_conv_w, gdn_dt_bias, gdn_A_log, gdn_norm_w, w_hy_out, w_ssm_out, w_gdn_out, w_out, w_gate_up, w_down, final_norm_w):
    B, Ll, D = x.shape
    Lc = ctx.shape[1]
    depth = w_ada.shape[0]
    assert Lc == CONV_ROWS and D == D_MODEL and B <= 15
    rw = Rows(B, Lc, Ll)
    R, NC = rw.R, rw.NC

    xs = (ctx.reshape(B * Lc, D), x.reshape(B * Ll, D))

    svec = jnp.concatenate([c_ctx[None, :], c, jnp.zeros((15 - B, D), F32)], axis=0)
    mod = ada_modulation(svec, w_ada, b_ada)
    mod = jnp.pad(mod.reshape(depth, 16, 6, D), ((0, 0), (0, 0), (0, 2), (0, 0)))

    w_in_r = _regroup_w_in(w_in)
    par = _in_proj_params(hy_conv_w, hy_conv_b, ssm_conv_w, ssm_conv_b, gdn_conv_w, ssm_dt_bias, gdn_dt_bias,
                          gdn_A_log)
    norm1 = norm1_w.reshape(depth, 1, D)
    norm2 = norm2_w.reshape(depth, 1, D)
    w_hy_o, w_ssm_o, w_gdn_o, w_o = (w.astype(BF16) for w in (w_hy_out, w_ssm_out, w_gdn_out, w_out))
    w_gu, w_dn = w_gate_up.astype(BF16), w_down.astype(BF16)
    dft_l = (dft_table(Ll),) + dft_tables_split(Ll)
    dft_c = (dft_table(Lc),) + dft_tables_split(Lc)
    feat_l, feat_c = hy_features(Ll), hy_features(Lc)

    for l in range(depth):
        p, sm = in_proj(rw, xs, l, norm1, mod, w_in_r, par)

        sm32_t = sm[:, :32].T
        dt_t = sm32_t[:16].reshape(2, 8, R)
        g_t = sm32_t[16:32].reshape(16, R // GDN_CHUNK, GDN_CHUNK).transpose(1, 0, 2)

        alx = jnp.repeat(ssm_A_log[l], SSM_HEAD_DIM, axis=-1).reshape(2, 1, 512)
        alc = ssm_A_log[l].reshape(2, 8, 1)
        y_f, y_b = ssd_scan(rw, p, sm, dt_t, alx, alc)
        dx = jnp.repeat(ssm_D[l], SSM_HEAD_DIM).reshape(1, 512)

        o_f, o_b = gdn_scan(rw, *gdn_prep(p, sm, g_t))

        hyu = p
        parts = []
        for (Bn, L, blk0, (full, fwd, inv), feat) in ((B, Lc, 0, dft_c, feat_c),
                                                      (B, Ll, NC // Ll, dft_l, feat_l)):
            if NC % L:
                raise ValueError("latent length must divide the context row count")
            filt = hy_filter(feat, hy_w1[l], hy_b1[l], hy_w2[l], hy_b2[l], hy_w3[l], hy_freq[l])
            kspec = matmul(full, filt, min(512, 2 * L), 512)
            z1 = long_conv(Bn, L, hyu, blk0, 0, hyu, blk0, 1, hy_bias[l, 0], fwd, inv, kspec, 0, F32)
            yy = long_conv(Bn, L, z1, 0, 0, hyu, blk0, 2, hy_bias[l, 1], fwd, inv, kspec, 1, BF16)
            parts.append(yy)
        y_hy = tuple(parts)

        xa = merge(rw, l, y_hy, y_f, y_b, dx, ssm_norm_w[l].reshape(1, 512),
                   o_f, o_b, jnp.tile(gdn_norm_w[l], GDN_HEADS).reshape(1, 512), p,
                   w_hy_o, w_ssm_o, w_gdn_o, w_o, xs, mod)
        act = swiglu_up(rw, l, xa, norm2, mod, w_gu)
        xa = swiglu_down(rw, l, act, w_dn, xa, mod)
        xs = (xa, xa)

    out = final_norm(rw, xa, final_norm_w)
    return out.reshape(B, Ll, D)
```

```python
import functools
import math

import jax
import jax.numpy as jnp
import numpy as np
from jax import lax
from jax.experimental import pallas as pl
from jax.experimental.pallas import tpu as pltpu

F32 = jnp.float32
BF16 = jnp.bfloat16
HI = lax.Precision.HIGHEST

EPS = 1e-6
D_MODEL = 1024
GRID_W = 64

HY_WIDTH = 512
HY_BANDS = 16
HY_EMB = 1 + 2 * HY_BANDS
HY_HIDDEN = 64
HY_SHORT_DECAY_PCT = 0.3
HY_LONG_DECAY_PCT = 1.5
HY_TARGET = 1e-2

SSM_HEADS = 8
SSM_HEAD_DIM = 64
SSM_WIDTH = 512
SSM_GROUPS = 2
SSM_HPG = 4
SSM_STATE = 128
SSM_CHUNK = 128
SSM_GW = SSM_HPG * SSM_HEAD_DIM

GDN_HEADS = 4
GDN_DK = 128
GDN_DV = 128
GDN_CHUNK = 64

D_FF = 2816

C_HY = 0
C_Z = 1536
C_XBC = 2048
C_QKV = 3072
C_GG = 4608
C_GATE = 5120
C_SM = 8192

CONV_ROWS = 256
FREQ_BLK = 256

VMEM_LIMIT = 56 * 1024 * 1024


def _cp(*sem, flags=None):
    return pltpu.CompilerParams(dimension_semantics=sem, vmem_limit_bytes=VMEM_LIMIT, flags=flags)


def _sigmoid(x):
    return 1.0 / (1.0 + jnp.exp(-x))


def _silu(x):
    return x * _sigmoid(x)


def _softplus(x):
    return jnp.maximum(x, 0.0) + jnp.log1p(jnp.exp(-jnp.abs(x)))


def _dot(a, b, precision=None):
    return jnp.dot(a, b, precision=precision, preferred_element_type=F32)


def _dot_nt(a, b):
    return lax.dot_general(a, b, (((1,), (1,)), ((), ())), preferred_element_type=F32)


def _dot_tn(a, b):
    return lax.dot_general(a, b, (((0,), (0,)), ((), ())), preferred_element_type=F32)


def _ada_kernel(s_ref, w_ref, b_ref, o_ref):
    s = _silu(s_ref[...])
    o_ref[0] = _dot(s, w_ref[0], HI) + b_ref[0]


def ada_modulation(svec, w_ada, b_ada):
    depth = w_ada.shape[0]
    D = D_MODEL
    return pl.pallas_call(
        _ada_kernel,
        grid=(depth, 6),
        in_specs=[
            pl.BlockSpec((16, D), lambda l, j: (0, 0)),
            pl.BlockSpec((1, D, D), lambda l, j: (l, 0, j)),
            pl.BlockSpec((1, 1, D), lambda l, j: (l, 0, j)),
        ],
        out_specs=pl.BlockSpec((1, 16, D), lambda l, j: (l, 0, j)),
        out_shape=jax.ShapeDtypeStruct((depth, 16, 6 * D), F32),
        compiler_params=_cp("arbitrary", "arbitrary"),
        name="ada",
    )(svec, w_ada, b_ada.reshape(depth, 1, 6 * D))


def _norm_mod(x, nw, scale, shift):
    ms = jnp.mean(x * x, axis=-1, keepdims=True)
    return (x * lax.rsqrt(ms + EPS) * nw) * (1.0 + scale) + shift


IN_FLIGHT = 4
N_IN_PAD = C_SM + 128
IN_TN = N_IN_PAD // 5
MODE_RAW, MODE_CONV, MODE_CONV_SILU, MODE_CONV_SILU_L2, MODE_SMALL = range(5)


def _tile_mode(tile):
    col = tile * 128
    if col < C_Z:
        return MODE_CONV
    if col < C_XBC:
        return MODE_RAW
    if col < C_QKV:
        return MODE_CONV_SILU
    if col < C_QKV + 1024:
        return MODE_CONV_SILU_L2
    if col < C_GG:
        return MODE_CONV_SILU
    if col < C_SM:
        return MODE_RAW
    return MODE_SMALL
PAR_W0, PAR_W1, PAR_W2, PAR_BIAS, PAR_L2SCALE, PAR_SBIAS, PAR_SALOG, PAR_SKIND = range(8)


def _in_proj_kernel(xc_ref, xl_ref, nw_ref, mod_ref, w_ref, par_ref, o_ref, sm_ref, h_ref, raw0_ref, raw1_ref, *,
                    nctx_blk):
    j = pl.program_id(1)
    nj = N_IN_PAD // IN_TN
    raws = (raw0_ref, raw1_ref)

    @pl.when((j == 0) & (pl.program_id(0) < nctx_blk))
    def _():
        h = _norm_mod(xc_ref[...], nw_ref[...], mod_ref[0, 1:2, :], mod_ref[0, 0:1, :])
        h_ref[...] = h.astype(BF16)

    @pl.when((j == 0) & (pl.program_id(0) >= nctx_blk))
    def _():
        h = _norm_mod(xl_ref[...], nw_ref[...], mod_ref[0, 1:2, :], mod_ref[0, 0:1, :])
        h_ref[...] = h.astype(BF16)

    T = h_ref.shape[0]
    G = GRID_W
    per_ctx = CONV_ROWS // G
    is_latent = pl.program_id(0) >= nctx_blk
    sub = lax.broadcasted_iota(jnp.int32, (8, 128), 0)

    def raw_piece(src, g, c):
        return src[g * G:(g + 1) * G, c * 128:(c + 1) * 128]

    retired = []

    def retire(y):
        bits = pltpu.bitcast(y[0:8], jnp.int32)
        zero = lax.shift_right_logical(lax.shift_right_logical(bits, 16), 16)
        retired.append(jnp.tile(zero.astype(F32), (G // 8, 1)))

    def conv(src, g, c):
        cs = slice(c * 128, (c + 1) * 128)
        x = raw_piece(src, g, c)
        if len(retired) >= IN_FLIGHT:
            x = x + retired[-IN_FLIGHT]
        zero = jnp.zeros((1, 128), F32)
        before = zero if g % per_ctx == 0 else jnp.where(is_latent, 0.0, src[g * G - 1:g * G, cs])
        after = zero if g % per_ctx == per_ctx - 1 else jnp.where(is_latent, 0.0, src[(g + 1) * G:(g + 1) * G + 1, cs])
        rp = pltpu.roll(x, 1, 0)
        rn = pltpu.roll(x, G - 1, 0)
        prev = jnp.concatenate([jnp.where(sub == 0, before, rp[0:8]), rp[8:]], axis=0)
        nxt = jnp.concatenate([rn[:G - 8], jnp.where(sub == 7, after, rn[G - 8:])], axis=0)
        return (prev * par_ref[PAR_W0:PAR_W0 + 1, cs] + x * par_ref[PAR_W1:PAR_W1 + 1, cs]
                + nxt * par_ref[PAR_W2:PAR_W2 + 1, cs] + par_ref[PAR_BIAS:PAR_BIAS + 1, cs])

    def conv_silu(src, g, c):
        return _silu(conv(src, g, c))

    def conv_silu_l2(src, g, c):
        y = _silu(conv(src, g, c))
        y = y * lax.rsqrt(jnp.sum(y * y, axis=-1, keepdims=True) + EPS)
        return y * par_ref[PAR_L2SCALE:PAR_L2SCALE + 1, c * 128:(c + 1) * 128]

    def small(src, g, c):
        cs = slice(c * 128, (c + 1) * 128)
        acc = raw_piece(src, g, c)
        kind = par_ref[PAR_SKIND:PAR_SKIND + 1, cs]
        sp = _softplus(acc + par_ref[PAR_SBIAS:PAR_SBIAS + 1, cs])
        dec = -jnp.exp(par_ref[PAR_SALOG:PAR_SALOG + 1, cs]) * sp
        return jnp.where(kind == 0.0, sp, jnp.where(kind == 1.0, dec, jnp.where(kind == 2.0, _sigmoid(acc), 0.0)))

    rows_mm = 256
    tiles = IN_TN // 128
    piece_fn = {MODE_RAW: raw_piece, MODE_CONV: conv, MODE_CONV_SILU: conv_silu,
                MODE_CONV_SILU_L2: conv_silu_l2, MODE_SMALL: small}

    heavy_modes = (MODE_CONV_SILU, MODE_CONV_SILU_L2)
    col_slices = [slice(c0, min(c0 + 256, IN_TN)) for c0 in range(0, IN_TN, 256)]

    def project(dst, r, cs=slice(None)):
        rs = slice(r * rows_mm, (r + 1) * rows_mm)
        dst[rs, cs] = _dot(h_ref[rs, :], w_ref[:, cs])

    def finish(src, blk, g, c):
        mode = _tile_mode(blk * tiles + c)
        y = piece_fn[mode](src, g, c)
        if mode in heavy_modes:
            retire(y)
        if mode == MODE_SMALL:
            sm_ref[g * G:(g + 1) * G, :] = y
            y = jnp.zeros_like(y)
        o_ref[g * G:(g + 1) * G, c * 128:(c + 1) * 128] = y.astype(o_ref.dtype)

    for step in range(nj + 1):
        @pl.when(j == step)
        def _(step=step):
            blk = step - 1
            src, dst = raws[blk % 2], raws[step % 2]
            retired.clear()
            light = blk >= 0 and not any(_tile_mode(blk * tiles + c) in heavy_modes for c in range(tiles))
            for r in range(T // rows_mm):
                pieces = [] if blk < 0 else [(g, c) for g in range(r * rows_mm // G, (r + 1) * rows_mm // G)
                                             for c in range(tiles)]
                if step == nj:
                    slabs = []
                elif light:
                    slabs = col_slices
                else:
                    slabs = [slice(None)]
                per = -(-len(pieces) // max(len(slabs), 1))
                for n in range(max(len(slabs), 1)):
                    if n < len(slabs):
                        project(dst, r, slabs[n])
                    for g, c in pieces[n * per:(n + 1) * per]:
                        finish(src, blk, g, c)


class Rows:
    def __init__(self, B, Lc, Ll):
        self.B, self.Lc, self.Ll = B, Lc, Ll
        self.NC = B * Lc
        self.R = B * Lc + B * Ll
        assert self.NC % Ll == 0 or Ll % self.NC == 0
        tm = 1024
        while self.NC % tm or Ll % tm:
            tm //= 2
        self.tm = tm

    def mod_index(self, tm):
        nctx = self.NC // tm
        per = self.Ll // tm
        return lambda i: jnp.where(i < nctx, 0, 1 + (i - nctx) // per)


def _stream_specs(rw, tm, xs, ngrid):
    xc, xl = xs
    nctx = rw.NC // tm
    off = nctx if xl.shape[0] == rw.R else 0
    D = xc.shape[1]
    if ngrid == 1:
        return [pl.BlockSpec((tm, D), lambda i: (jnp.minimum(i, nctx - 1), 0)),
                pl.BlockSpec((tm, D), lambda i: (jnp.maximum(i - nctx, 0) + off, 0))]
    return [pl.BlockSpec((tm, D), lambda i, j: (jnp.minimum(i, nctx - 1), 0)),
            pl.BlockSpec((tm, D), lambda i, j: (jnp.maximum(i - nctx, 0) + off, 0))]


def in_proj(rw, xs, l, nw, mod, w, par):
    R = rw.R
    D = xs[0].shape[1]
    N = w.shape[2]
    tm, tn = rw.tm, IN_TN
    nj = N // tn
    assert N == N_IN_PAD
    mi = rw.mod_index(tm)
    done = lambda j: jnp.maximum(j - 1, 0)
    return pl.pallas_call(
        functools.partial(_in_proj_kernel, nctx_blk=rw.NC // tm),
        grid=(R // tm, nj + 1),
        in_specs=_stream_specs(rw, tm, xs, 2) + [
            pl.BlockSpec((None, 1, D), lambda i, j: (l, 0, 0)),
            pl.BlockSpec((None, 1, 8, D), lambda i, j: (l, mi(i), 0, 0)),
            pl.BlockSpec((None, D, tn), lambda i, j: (l, 0, jnp.minimum(j, nj - 1))),
            pl.BlockSpec((None, 8, tn), lambda i, j: (l, 0, done(j))),
        ],
        out_specs=[pl.BlockSpec((tm, tn), lambda i, j: (i, done(j))),
                   pl.BlockSpec((tm, 128), lambda i, j: (i, 0))],
        out_shape=[jax.ShapeDtypeStruct((R, N), BF16), jax.ShapeDtypeStruct((R, 128), F32)],
        scratch_shapes=[pltpu.VMEM((tm, D), BF16), pltpu.VMEM((tm, tn), F32), pltpu.VMEM((tm, tn), F32)],
        compiler_params=_cp("arbitrary", "arbitrary"),
        name="in_proj",
    )(xs[0], xs[1], nw, mod, w, par)


def _in_proj_params(hy_conv_w, hy_conv_b, ssm_conv_w, ssm_conv_b, gdn_conv_w, ssm_dt_bias, gdn_dt_bias, gdn_A_log):
    depth = hy_conv_w.shape[0]

    def row(pieces):
        out, pos = [], 0
        for off, a in pieces:
            out += [jnp.zeros((depth, off - pos), F32), a.astype(F32)]
            pos = off + a.shape[1]
        return jnp.concatenate(out + [jnp.zeros((depth, N_IN_PAD - pos), F32)], axis=1)
    z4 = jnp.zeros((depth, 4), F32)
    conv = [row([(C_HY, hy_conv_w[:, t]), (C_XBC, ssm_conv_w[:, t]), (C_QKV, gdn_conv_w[:, t])]) for t in range(3)]
    bias = row([(C_HY, hy_conv_b), (C_XBC, ssm_conv_b)])
    l2s = row([(C_QKV, jnp.full((depth, 512), GDN_DK ** -0.5, F32)), (C_QKV + 512, jnp.ones((depth, 512), F32))])
    sbias = row([(C_SM, jnp.concatenate([ssm_dt_bias.reshape(depth, 16), gdn_dt_bias[:, 0], z4,
                                         gdn_dt_bias[:, 1], z4], axis=1))])
    salog = row([(C_SM + 16, jnp.concatenate([gdn_A_log[:, 0], z4, gdn_A_log[:, 1], z4], axis=1))])
    kind = np.full((depth, N_IN_PAD), 3.0, np.float32)
    kind[:, C_SM:C_SM + 16] = 0.0
    kind[:, C_SM + 16:C_SM + 20] = 1.0
    kind[:, C_SM + 24:C_SM + 28] = 1.0
    kind[:, C_SM + 20:C_SM + 24] = 2.0
    kind[:, C_SM + 28:C_SM + 32] = 2.0
    return jnp.stack(conv + [bias, l2s, sbias, salog, jnp.asarray(kind)], axis=1)


def _hy_filter_kernel(z_ref, w1_ref, b1_ref, w2_ref, b2_ref, w3_ref, f0_ref, f1_ref, win_ref, o_ref, h_ref):
    @pl.when(pl.program_id(1) == 0)
    def _():
        h1 = jnp.sin(f0_ref[...] * (_dot(z_ref[...], w1_ref[...], HI) + b1_ref[...]))
        h_ref[...] = jnp.sin(f1_ref[...] * (_dot(h1, w2_ref[...], HI) + b2_ref[...]))

    h = _dot(h_ref[...], w3_ref[...], HI) * win_ref[...]
    tl = h.shape[0]
    row = lax.broadcasted_iota(jnp.int32, (tl, 1), 0) + pl.program_id(0) * tl
    drop = (row == 0) & (pl.program_id(1) % 2 == 1)
    o_ref[...] = jnp.where(drop, 0.0, h).astype(o_ref.dtype)


def hy_features(L):
    t = jnp.linspace(0.0, 1.0, L, dtype=F32)[:, None]
    w = 2.0 * math.pi * jnp.arange(L, dtype=F32)[:, None] / L
    f = jnp.linspace(1e-4, HY_BANDS - 1, HY_BANDS, dtype=F32)[None, :]
    z = jnp.concatenate([t, jnp.cos(f * w), -jnp.sin(f * w)], axis=-1)
    z = jnp.pad(z, ((0, 0), (0, 128 - HY_EMB)))
    min_decay = math.log(HY_TARGET) / HY_LONG_DECAY_PCT
    max_decay = math.log(HY_TARGET) / HY_SHORT_DECAY_PCT
    deltas = jnp.linspace(min_decay, max_decay, HY_WIDTH, dtype=F32)
    window = jnp.exp(-t * jnp.abs(deltas))
    return z, window


def hy_filter(feat, w1, b1, w2, b2, w3, freq):
    z, window = feat
    L = z.shape[0]
    H = HY_HIDDEN
    w1p = jnp.pad(w1, ((0, 128 - HY_EMB), (0, 128 - H)))
    w2p = jnp.pad(w2, ((0, 128 - H), (0, 128 - H)))
    w3p = jnp.pad(w3, ((0, 128 - H), (0, 0)))
    pad1 = lambda v: jnp.pad(v, (0, 128 - H)).reshape(1, 128)
    tl = 256
    full = lambda shape: pl.BlockSpec(shape, lambda i, j: (0, 0))
    return pl.pallas_call(
        _hy_filter_kernel,
        grid=(L // tl, 4),
        in_specs=[
            pl.BlockSpec((tl, 128), lambda i, j: (i, 0)),
            full((128, 128)), full((1, 128)), full((128, 128)), full((1, 128)),
            pl.BlockSpec((128, HY_WIDTH), lambda i, j: (0, j)),
            full((1, 128)), full((1, 128)),
            pl.BlockSpec((tl, HY_WIDTH), lambda i, j: (i, 0)),
        ],
        out_specs=pl.BlockSpec((tl, HY_WIDTH), lambda i, j: (i, j)),
        out_shape=jax.ShapeDtypeStruct((L, 4 * HY_WIDTH), BF16),
        scratch_shapes=[pltpu.VMEM((tl, 128), F32)],
        compiler_params=_cp("arbitrary", "arbitrary"),
        name="hy_filter",
    )(z, w1p, pad1(b1), w2p, pad1(b2), w3p, pad1(freq[0]), pad1(freq[1]), window)


def dft_table(L):
    N = 2 * L
    H = L // 2
    q = np.arange(L, dtype=np.int64)
    f = np.where(q < H, q, L + H - q)[:, None]
    s = np.arange(L, dtype=np.int64)[None, :]
    ang = ((f * s) % N).astype(np.float64) * (2.0 * math.pi / N)
    mid = ((H * s) % N).astype(np.float64) * (2.0 * math.pi / N)
    q = q[:, None]
    imag = np.where(q == 0, np.cos(mid), np.where(q == H, -np.sin(mid), -np.sin(ang)))
    return jnp.asarray(np.concatenate([np.cos(ang), imag], axis=0), dtype=BF16)


def dft_tables_split(L):
    N = 2 * L
    H = L // 2
    q = np.arange(H, dtype=np.int64)[:, None]
    m = np.arange(H, dtype=np.int64)[None, :]
    ang_e = ((q * 2 * m) % N).astype(np.float64) * (2.0 * math.pi / N)
    ang_o = ((q * (2 * m + 1)) % N).astype(np.float64) * (2.0 * math.pi / N)
    alt = (1 - 2 * (m % 2)).astype(np.float64)
    ce, co = np.cos(ang_e), np.cos(ang_o)
    se = np.where(q == 0, alt, -np.sin(ang_e))
    so = np.where(q == 0, -alt, -np.sin(ang_o))
    w = np.where(q == 0, 1.0, 2.0) / N
    ise = np.where(q == 0, 2.0 / N * alt, -np.sin(ang_e) * w)
    iso = np.where(q == 0, -2.0 / N * alt, -np.sin(ang_o) * w)
    fwd = np.stack([ce, co, se, so])
    inv = np.stack([(ce * w).T, ise.T, (co * w).T, iso.T])
    return jnp.asarray(fwd, dtype=BF16), jnp.asarray(inv, dtype=BF16)


def _matmul_kernel(a_ref, b_ref, o_ref):
    o_ref[...] = _dot(a_ref[...], b_ref[...])


def matmul(a, b, tm, tn):
    M, K = a.shape
    N = b.shape[1]
    return pl.pallas_call(
        _matmul_kernel,
        grid=(M // tm, N // tn),
        in_specs=[pl.BlockSpec((tm, K), lambda i, j: (i, 0)), pl.BlockSpec((K, tn), lambda i, j: (0, j))],
        out_specs=pl.BlockSpec((tm, tn), lambda i, j: (i, j)),
        out_shape=jax.ShapeDtypeStruct((M, N), F32),
        compiler_params=_cp("arbitrary", "arbitrary"),
        name="matmul",
    )(a, b)


def _long_conv_kernel(u_ref, g_ref, bias_ref, fwd_ref, inv_ref, ar0_ref, ar1_ref, ar0h_ref, ar1h_ref,
                      ai0_ref, ai1_ref, ai0h_ref, ai1h_ref, o_ref, ue_ref, uo_ref, acce_ref, acco_ref, y_ref):
    f = pl.program_id(1)
    half = ue_ref.shape[0]

    lane_tiles = [slice(c * 128, (c + 1) * 128) for c in range(y_ref.shape[0])]

    @pl.when(f == 0)
    def _():
        for c, cs in enumerate(lane_tiles):
            y_c = y_ref.at[c]
            y_c[...] = u_ref[:, cs].astype(F32)
            ue_ref[:, cs] = y_c[pl.ds(0, half, stride=2), :].astype(BF16)
            uo_ref[:, cs] = y_c[pl.ds(1, half, stride=2), :].astype(BF16)
        acce_ref[...] = jnp.zeros_like(acce_ref)
        acco_ref[...] = jnp.zeros_like(acco_ref)

    ue, uo = ue_ref[...], uo_ref[...]
    ae, ao = _dot(fwd_ref[0], ue), _dot(fwd_ref[1], uo)
    be, bo = _dot(fwd_ref[2], ue), _dot(fwd_ref[3], uo)
    ur, ur2 = ae + ao, ae - ao
    ui, ui2 = be + bo, bo - be
    first = (lax.broadcasted_iota(jnp.int32, (fwd_ref.shape[1], 1), 0) == 0) & (f == 0)
    kr, kr2 = ar0_ref[...] + ar1_ref[...], ar0h_ref[...] + ar1h_ref[...]
    ki = jnp.where(first, ai0_ref[...] + ai1_ref[...], ai0_ref[...] - ai1_ref[...])
    ki2 = ai0h_ref[...] - ai1h_ref[...]
    pr, pi = ur * kr - ui * ki, ur * ki + ui * kr
    pr2, pi2 = ur2 * kr2 - ui2 * ki2, ur2 * ki2 + ui2 * kr2
    dc, ny = ur * kr, ur2 * kr2
    gr = jnp.where(first, dc + ny, pr + pr2)
    gi = jnp.where(first, be * ki - bo * ki2, pi - pi2)
    hr = jnp.where(first, dc - ny, pr - pr2)
    hi = jnp.where(first, be * ki2 + bo * ki, pi + pi2)
    acce_ref[...] += _dot(inv_ref[0], gr.astype(BF16)) + _dot(inv_ref[1], gi.astype(BF16))
    acco_ref[...] += _dot(inv_ref[2], hr.astype(BF16)) + _dot(inv_ref[3], hi.astype(BF16))

    @pl.when(f == pl.num_programs(1) - 1)
    def _():
        for c, cs in enumerate(lane_tiles):
            y_c = y_ref.at[c]
            y_c[pl.ds(0, half, stride=2), :] = acce_ref[:, cs]
            y_c[pl.ds(1, half, stride=2), :] = acco_ref[:, cs]
            u = u_ref[:, cs].astype(F32)
            o_ref[:, cs] = (g_ref[:, cs].astype(F32) * (y_c[...] + u * bias_ref[:, cs])).astype(o_ref.dtype)


def long_conv(B, L, u, u_rb0, u_cb, gate, g_rb0, gate_cb, bias, fwd, inv, kspec, order, out_dtype):
    C = HY_WIDTH
    H = L // 2
    FB = min(FREQ_BLK, H)
    nfb = H // FB
    kblk = lambda part, d: pl.BlockSpec((FB, C), lambda b, f: (part * nfb + f, 2 * order + d))
    return pl.pallas_call(
        _long_conv_kernel,
        grid=(B, nfb),
        in_specs=[
            pl.BlockSpec((L, C), lambda b, f: (u_rb0 + b, u_cb)),
            pl.BlockSpec((L, C), lambda b, f: (g_rb0 + b, gate_cb)),
            pl.BlockSpec((1, C), lambda b, f: (0, 0)),
            pl.BlockSpec((4, FB, H), lambda b, f: (0, f, 0)),
            pl.BlockSpec((4, H, FB), lambda b, f: (0, 0, f)),
            kblk(0, 0), kblk(0, 1), kblk(1, 0), kblk(1, 1), kblk(2, 0), kblk(2, 1), kblk(3, 0), kblk(3, 1),
        ],
        out_specs=pl.BlockSpec((L, C), lambda b, f: (b, 0)),
        out_shape=jax.ShapeDtypeStruct((B * L, C), out_dtype),
        scratch_shapes=[pltpu.VMEM((H, C), BF16), pltpu.VMEM((H, C), BF16),
                        pltpu.VMEM((H, C), F32), pltpu.VMEM((H, C), F32), pltpu.VMEM((C // 128, L, 128), F32)],
        compiler_params=_cp("arbitrary", "arbitrary"),
        name="long_conv",
    )(u, gate, bias.reshape(1, C), fwd, inv, *([kspec] * 8))


def _scan_blocks(rw, rows):
    nbc, nbl, base = rw.Lc // rows, rw.Ll // rows, rw.NC // rows

    def make(d):
        def f(b, s):
            jc = s if d == 0 else nbc - 1 - s
            jl = (s - nbc) if d == 0 else nbl - 1 - (s - nbc)
            return jnp.where(s < nbc, b * nbc + jc, base + b * nbl + jl)
        return f

    return [make(0), make(1)], nbc + nbl


def _expand_lanes(x, base, n, width):
    rows = x.shape[0]
    per = 128 // width
    lane = lax.broadcasted_iota(jnp.int32, (rows, 128), 1)
    tiles = []
    for t in range(n // per):
        c0 = base + t * per
        tile = jnp.broadcast_to(x[:, c0:c0 + 1], (rows, 128))
        for i in range(1, per):
            tile = jnp.where(lane >= i * width, jnp.broadcast_to(x[:, c0 + i:c0 + i + 1], (rows, 128)), tile)
        tiles.append(tile)
    return jnp.concatenate(tiles, axis=1)


def _ssd_kernel(xf, bf, cf, smf, dtf, xb, bb, cb_, smb, dtb, alx_ref, alc_ref, of_ref, ob_ref, h_ref):
    Q = SSM_CHUNK
    GW = SSM_GW

    @pl.when(pl.program_id(1) == 0)
    def _():
        h_ref[...] = jnp.zeros_like(h_ref)

    row = lax.broadcasted_iota(jnp.int32, (Q, Q), 0)
    col = lax.broadcasted_iota(jnp.int32, (Q, Q), 1)
    lane_head = lax.broadcasted_iota(jnp.int32, (Q, GW), 1) // SSM_HEAD_DIM
    dirs = ((xf, bf, cf, smf, dtf, of_ref), (xb, bb, cb_, smb, dtb, ob_ref))
    jobs = []
    for d in range(2):
        x_ref, b_ref, c_ref, sm_ref, dt_ref, o_ref = dirs[d]
        keep = (col <= row) if d == 0 else (col >= row)
        tri = keep.astype(BF16)
        tri_t = ((row <= col) if d == 0 else (row >= col)).astype(BF16)
        sm = sm_ref[...]
        a_x = -jnp.exp(alx_ref[d])
        dtx = _expand_lanes(sm, 8 * d, SSM_HEADS, SSM_HEAD_DIM)
        cumx = _expand_lanes(_dot_01_lhs(tri, sm), 8 * d, SSM_HEADS, SSM_HEAD_DIM) * a_x
        cumr = _dot_01_rhs(dt_ref[0], tri_t) * (-jnp.exp(alc_ref[d]))
        last = Q - 1 if d == 0 else 0
        totx = cumx[last:last + 1, :]
        xd = x_ref[...].astype(F32) * dtx
        xdw = xd * jnp.exp(totx - cumx)
        ecum = jnp.exp(cumx)
        for g in range(SSM_GROUPS):
            gs = slice(g * GW, (g + 1) * GW)
            jobs.append(dict(d=d, g=g, gs=gs, keep=keep, cumx=cumx, cumr=cumr, o_ref=o_ref,
                             bg=b_ref[:, g * SSM_STATE:(g + 1) * SSM_STATE].astype(BF16),
                             cg=c_ref[:, g * SSM_STATE:(g + 1) * SSM_STATE].astype(BF16),
                             xdg=xd[:, gs], xdw=xdw[:, gs].astype(BF16), ecum=ecum[:, gs],
                             etot=jnp.exp(totx[:, gs])))
    for j in jobs:
        j["cb"] = _dot_nt(j["cg"], j["bg"])
        j["h"] = h_ref[j["d"], j["g"]]
    for j in jobs:
        ms, xs = [], []
        for e4 in range(SSM_HPG):
            e = j["g"] * SSM_HPG + e4
            diff = j["cumx"][:, e * SSM_HEAD_DIM:e * SSM_HEAD_DIM + 1] - j["cumr"][e:e + 1, :]
            ms.append((j["cb"] * jnp.where(j["keep"], jnp.exp(diff), 0.0)).astype(BF16))
            xs.append(jnp.where(lane_head == e4, j["xdg"], 0.0).astype(BF16))
        yd = _dot(jnp.concatenate(ms, axis=1), jnp.concatenate(xs, axis=0))
        y_off = _dot(j["cg"], j["h"].astype(BF16)) * j["ecum"]
        j["o_ref"][:, j["gs"]] = (yd + y_off).astype(BF16)
    for j in jobs:
        h_ref[j["d"], j["g"]] = j["h"] * j["etot"] + _dot_tn(j["bg"], j["xdw"])


def ssd_scan(rw, p, sm, dtT, alx, alc):
    Q = SSM_CHUNK
    blks, nsteps = _scan_blocks(rw, Q)
    R = p.shape[0]
    in_specs = []
    for d in range(2):
        f = blks[d]
        in_specs += [
            pl.BlockSpec((Q, 512), lambda b, s, f=f: (f(b, s), C_XBC // 512)),
            pl.BlockSpec((Q, 256), lambda b, s, f=f: (f(b, s), C_XBC // 256 + 2)),
            pl.BlockSpec((Q, 256), lambda b, s, f=f: (f(b, s), C_XBC // 256 + 3)),
            pl.BlockSpec((Q, 128), lambda b, s, f=f: (f(b, s), 0)),
            pl.BlockSpec((1, 8, Q), lambda b, s, f=f, d=d: (d, 0, f(b, s))),
        ]
    in_specs += [pl.BlockSpec((2, 1, 512), lambda b, s: (0, 0, 0)), pl.BlockSpec((2, 8, 1), lambda b, s: (0, 0, 0))]
    ops = (p, p, p, sm, dtT)
    return pl.pallas_call(
        _ssd_kernel,
        grid=(rw.B, nsteps),
        in_specs=in_specs,
        out_specs=[pl.BlockSpec((Q, 512), lambda b, s, f=blks[d]: (f(b, s), 0)) for d in range(2)],
        out_shape=[jax.ShapeDtypeStruct((R, 512), BF16)] * 2,
        scratch_shapes=[pltpu.VMEM((2, SSM_GROUPS, SSM_STATE, SSM_GW), F32)],
        compiler_params=_cp("arbitrary", "arbitrary"),
        name="ssd_scan",
    )(*ops, *ops, alx, alc)


def _split3(x):
    x1 = x.astype(BF16)
    r = x - x1.astype(F32)
    x2 = r.astype(BF16)
    x3 = (r - x2.astype(F32)).astype(BF16)
    return x1, x2, x3


def _dot_01_lhs(m01, x):
    x1, x2, x3 = _split3(x)
    return _dot(m01, x1) + _dot(m01, x2) + _dot(m01, x3)


def _dot_01_rhs(x, m01):
    x1, x2, x3 = _split3(x)
    return _dot(x1, m01) + _dot(x2, m01) + _dot(x3, m01)


GDN_ROWS = 256


def _gdn_prep_kernel(q_ref, k_ref, v_ref, sm_ref, gT_ref, u_ref, w_ref, qg_ref, kd_ref, qk_ref, egl_ref):
    C = GDN_CHUNK
    row = lax.broadcasted_iota(jnp.int32, (C, C), 0)
    col = lax.broadcasted_iota(jnp.int32, (C, C), 1)
    jobs = []
    levels = []
    for d in range(2):
        keep = (col <= row) if d == 0 else (col >= row)
        late, early = (row, col) if d == 0 else (col, row)
        levels.append([(((row ^ col) >> (j + 1)) == 0) & ((late & (1 << j)) != 0) & ((early & (1 << j)) == 0)
                       for j in range(6)])
        tri = keep.astype(BF16)
        tri_t = ((row <= col) if d == 0 else (row >= col)).astype(BF16)
        last = C - 1 if d == 0 else 0
        for c in range(GDN_ROWS // C):
            rows = slice(c * C, (c + 1) * C)
            smc = sm_ref[rows, :]
            cums = _dot_01_lhs(tri, smc)
            cumr = _dot_01_rhs(gT_ref[c, 8 * d:8 * d + 8, :], tri_t)
            tot = cums[last:last + 1, :]
            for h in range(GDN_HEADS):
                lg = 16 + 8 * d + h
                jobs.append(dict(d=d, c=c, h=h, rows=rows, hs=slice(h * 128, (h + 1) * 128), keep=keep,
                                 gc=cums[:, lg:lg + 1], beta=smc[:, lg + 4:lg + 5],
                                 gl=tot[:, lg:lg + 1], gr=cumr[h:h + 1, :]))
    for j in jobs:
        q = q_ref[j["rows"], j["hs"]].astype(F32)
        k = k_ref[j["rows"], j["hs"]].astype(F32)
        j["dec"] = jnp.where(j["keep"], jnp.exp(j["gc"] - j["gr"]), 0.0)
        kb = k * j["beta"]
        both = _dot_nt(jnp.concatenate([kb, q], axis=0).astype(BF16), k.astype(BF16))
        j["a"] = both[:C] * j["dec"]
        j["n"] = -jnp.where(levels[j["d"]][0], j["a"], 0.0)
        qk_ref[j["d"], j["c"], j["h"]] = (both[C:] * j["dec"]).astype(BF16)
    for lev in range(1, 6):
        for j in jobs:
            l = jnp.where(levels[j["d"]][lev], j["a"], 0.0)
            j["y"] = l + _dot(l.astype(BF16), j["n"].astype(BF16))
        for j in jobs:
            j["n"] = j["n"] - j["y"] - _dot(j["n"].astype(BF16), j["y"].astype(BF16))
    for j in jobs:
        d, rows, hs, gc, gl, beta = j["d"], j["rows"], j["hs"], j["gc"], j["gl"], j["beta"]
        q = q_ref[rows, hs].astype(F32)
        k = k_ref[rows, hs].astype(F32)
        eg = jnp.exp(gc)
        rhs = jnp.concatenate([v_ref[rows, hs].astype(F32) * beta, k * beta * eg], axis=1)
        sol = rhs + _dot(j["n"].astype(BF16), rhs.astype(BF16))
        u_ref[d, rows, hs] = sol[:, :GDN_DV].astype(BF16)
        w_ref[d, rows, hs] = sol[:, GDN_DV:].astype(BF16)
        qg_ref[d, rows, hs] = (q * eg).astype(BF16)
        kd_ref[d, rows, hs] = (k * jnp.exp(gl - gc)).astype(BF16)
        egl_ref[d, j["c"], :, hs] = jnp.broadcast_to(jnp.exp(gl), (8, 128))


def gdn_prep(p, sm, gT):
    R = p.shape[0]
    T, C = GDN_ROWS, GDN_CHUNK
    nc = T // C
    col = lambda k: pl.BlockSpec((T, 512), lambda i: (i, C_QKV // 512 + k))
    dirrow = pl.BlockSpec((2, T, 512), lambda i: (0, i, 0))
    return pl.pallas_call(
        _gdn_prep_kernel,
        grid=(R // T,),
        in_specs=[col(0), col(1), col(2),
                  pl.BlockSpec((T, 128), lambda i: (i, 0)),
                  pl.BlockSpec((nc, 16, C), lambda i: (i, 0, 0))],
        out_specs=[dirrow, dirrow, dirrow, dirrow,
                   pl.BlockSpec((2, nc, GDN_HEADS, C, C), lambda i: (0, i, 0, 0, 0)),
                   pl.BlockSpec((2, nc, 8, 512), lambda i: (0, i, 0, 0))],
        out_shape=[jax.ShapeDtypeStruct((2, R, 512), BF16),
                   jax.ShapeDtypeStruct((2, R, 512), BF16),
                   jax.ShapeDtypeStruct((2, R, 512), BF16),
                   jax.ShapeDtypeStruct((2, R, 512), BF16),
                   jax.ShapeDtypeStruct((2, R // C, GDN_HEADS, C, C), BF16),
                   jax.ShapeDtypeStruct((2, R // C, 8, 512), F32)],
        compiler_params=_cp("arbitrary"),
        name="gdn_prep",
    )(p, p, p, sm, gT)


def _gdn_scan_kernel(uf, wf, qgf, kdf, qkf, eglf, ub, wb, qgb, kdb, qkb, eglb, of_ref, ob_ref, s_ref):
    C = GDN_CHUNK
    nch = GDN_ROWS // C

    @pl.when(pl.program_id(1) == 0)
    def _():
        s_ref[...] = jnp.zeros_like(s_ref)

    dirs = ((uf, wf, qgf, kdf, qkf, eglf, of_ref), (ub, wb, qgb, kdb, qkb, eglb, ob_ref))
    chains = [(d, h) for d in range(2) for h in range(GDN_HEADS)]
    S = {ch: s_ref[ch[0], ch[1]] for ch in chains}
    for i in range(nch):
        Sb, vnb, rows_of, c_of = {}, {}, {}, {}
        for d, h in chains:
            c_of[d] = i if d == 0 else nch - 1 - i
            rows_of[d] = slice(c_of[d] * C, (c_of[d] + 1) * C)
        for d, h in chains:
            hs = slice(h * 128, (h + 1) * 128)
            Sb[d, h] = S[d, h].astype(BF16)
            v_new = dirs[d][0][0, rows_of[d], hs].astype(F32) - _dot(dirs[d][1][0, rows_of[d], hs], Sb[d, h])
            vnb[d, h] = v_new.astype(BF16)
        for d, h in chains:
            hs = slice(h * 128, (h + 1) * 128)
            u_ref, w_ref, qg_ref, kd_ref, qk_ref, egl_ref, o_ref = dirs[d]
            S[d, h] = S[d, h] * egl_ref[0, c_of[d], 0:1, hs] + _dot_tn(kd_ref[0, rows_of[d], hs], vnb[d, h])
        for d, h in chains:
            hs = slice(h * 128, (h + 1) * 128)
            u_ref, w_ref, qg_ref, kd_ref, qk_ref, egl_ref, o_ref = dirs[d]
            o_ref[rows_of[d], hs] = (_dot(qg_ref[0, rows_of[d], hs], Sb[d, h])
                                     + _dot(qk_ref[0, c_of[d], h], vnb[d, h])).astype(BF16)
    for ch in chains:
        s_ref[ch[0], ch[1]] = S[ch]


def gdn_scan(rw, u, w, qg, kd, qk, egl):
    T, C = GDN_ROWS, GDN_CHUNK
    nc = T // C
    R = u.shape[1]
    nbc, nbl, base = rw.Lc // T, rw.Ll // T, rw.NC // T

    def blk(d):
        def f(b, s):
            jc = s if d == 0 else nbc - 1 - s
            jl = (s - nbc) if d == 0 else nbl - 1 - (s - nbc)
            return jnp.where(s < nbc, b * nbc + jc, base + b * nbl + jl)
        return f

    in_specs = []
    for d in range(2):
        f = blk(d)
        rowspec = pl.BlockSpec((1, T, 512), lambda b, s, f=f, d=d: (d, f(b, s), 0))
        in_specs += [rowspec, rowspec, rowspec, rowspec,
                     pl.BlockSpec((1, nc, GDN_HEADS, C, C), lambda b, s, f=f, d=d: (d, f(b, s), 0, 0, 0)),
                     pl.BlockSpec((1, nc, 8, 512), lambda b, s, f=f, d=d: (d, f(b, s), 0, 0))]
    out_specs = [pl.BlockSpec((T, 512), lambda b, s, f=blk(d): (f(b, s), 0)) for d in range(2)]
    ops = (u, w, qg, kd, qk, egl)
    return pl.pallas_call(
        _gdn_scan_kernel,
        grid=(rw.B, nbc + nbl),
        in_specs=in_specs,
        out_specs=out_specs,
        out_shape=[jax.ShapeDtypeStruct((R, 512), BF16)] * 2,
        scratch_shapes=[pltpu.VMEM((2, GDN_HEADS, GDN_DK, GDN_DV), F32)],
        compiler_params=_cp("arbitrary", "arbitrary"),
        name="gdn_scan",
    )(*ops, *ops)


def _merge_kernel(yhc_ref, yhl_ref, sf_ref, sb_ref, sx_ref, sz_ref, dx_ref, snw_ref, gf_ref, gb_ref, gg_ref, gnw_ref,
                  g0_ref, g1_ref, g2_ref, w0_ref, w1_ref, w2_ref, wo_ref, xc_ref, xl_ref, mod_ref, o_ref,
                  ys_ref, yg_ref, *, nctx_blk):
    tm = xc_ref.shape[0]
    rp = 64
    for r in range(tm // rp):
        rs = slice(r * rp, (r + 1) * rp)
        y = (sf_ref[rs, :].astype(F32) + sb_ref[rs, :].astype(F32)
             + sx_ref[rs, :].astype(F32) * dx_ref[...])
        y = y * _silu(sz_ref[rs, :].astype(F32))
        parts = []
        for g in range(SSM_GROUPS):
            yg = y[:, g * SSM_GW:(g + 1) * SSM_GW]
            parts.append(yg * lax.rsqrt(jnp.mean(yg * yg, axis=-1, keepdims=True) + EPS))
        ys_ref[rs, :] = (jnp.concatenate(parts, axis=1) * snw_ref[...]).astype(BF16)
        o = gf_ref[rs, :].astype(F32) + gb_ref[rs, :].astype(F32)
        parts = []
        for h in range(GDN_HEADS):
            oh = o[:, h * 128:(h + 1) * 128]
            parts.append(oh * lax.rsqrt(jnp.mean(oh * oh, axis=-1, keepdims=True) + EPS))
        yg_ref[rs, :] = (jnp.concatenate(parts, axis=1) * gnw_ref[...]
                         * _silu(gg_ref[rs, :].astype(F32))).astype(BF16)
    is_ctx = pl.program_id(0) < nctx_blk
    yh = jnp.where(is_ctx, yhc_ref[...], yhl_ref[...])
    m = (_sigmoid(g0_ref[...].astype(F32)) * _dot(yh, w0_ref[...])
         + _sigmoid(g1_ref[...].astype(F32)) * _dot(ys_ref[...], w1_ref[...])
         + _sigmoid(g2_ref[...].astype(F32)) * _dot(yg_ref[...], w2_ref[...]))
    x = jnp.where(is_ctx, xc_ref[...], xl_ref[...])
    o_ref[...] = x + mod_ref[0, 2:3, :] * _dot(m.astype(BF16), wo_ref[...])


def merge(rw, l, yh, y_f, y_b, dx, ssm_nw, o_f, o_b, gdn_nw, p, w0, w1, w2, wo, xs, mod):
    R = rw.R
    D = xs[0].shape[1]
    tm = min(rw.tm, 512)
    mi = rw.mod_index(tm)
    yspec = pl.BlockSpec((tm, 512), lambda i: (i, 0))
    pspec = lambda col: pl.BlockSpec((tm, 512), lambda i: (i, col // 512))
    vec = pl.BlockSpec((1, 512), lambda i: (0, 0))
    gspec = lambda k: pl.BlockSpec((tm, D), lambda i: (i, C_GATE // D + k))
    wspec = pl.BlockSpec((None, 512, D), lambda i: (l, 0, 0))
    return pl.pallas_call(
        functools.partial(_merge_kernel, nctx_blk=rw.NC // tm),
        grid=(R // tm,),
        in_specs=_stream_specs(rw, tm, yh, 1) + [
                  yspec, yspec, pspec(C_XBC), pspec(C_Z), vec, vec,
                  yspec, yspec, pspec(C_GG), vec,
                  gspec(0), gspec(1), gspec(2), wspec, wspec, wspec,
                  pl.BlockSpec((None, D, D), lambda i: (l, 0, 0))]
                 + _stream_specs(rw, tm, xs, 1)
                 + [pl.BlockSpec((None, 1, 8, D), lambda i: (l, mi(i), 0, 0))],
        out_specs=pl.BlockSpec((tm, D), lambda i: (i, 0)),
        out_shape=jax.ShapeDtypeStruct((R, D), F32),
        scratch_shapes=[pltpu.VMEM((tm, 512), BF16), pltpu.VMEM((tm, 512), BF16)],
        compiler_params=_cp("arbitrary"),
        name="merge",
    )(yh[0], yh[1], y_f, y_b, p, p, dx, ssm_nw, o_f, o_b, p, gdn_nw, p, p, p, w0, w1, w2, wo, xs[0], xs[1], mod)


def _swiglu_up_kernel(x_ref, nw_ref, mod_ref, wg_ref, wu_ref, o_ref, h_ref, g0_ref, g1_ref, u0_ref, u1_ref):
    @pl.when(pl.program_id(1) == 0)
    def _():
        h = _norm_mod(x_ref[...], nw_ref[...], mod_ref[0, 4:5, :], mod_ref[0, 3:4, :])
        h_ref[...] = h.astype(BF16)

    T, tn = o_ref.shape
    rows = g0_ref.shape[0]
    gs, us = (g0_ref, g1_ref), (u0_ref, u1_ref)

    def project(r):
        hh = h_ref[r * rows:(r + 1) * rows, :]
        gs[r % 2][...] = _dot(hh, wg_ref[...])
        us[r % 2][...] = _dot(hh, wu_ref[...])

    def finish(r):
        for q in range(rows // 64):
            for c in range(tn // 128):
                ps = (slice(q * 64, (q + 1) * 64), slice(c * 128, (c + 1) * 128))
                y = _silu(gs[r % 2][ps]) * us[r % 2][ps]
                o_ref[r * rows + q * 64:r * rows + (q + 1) * 64, ps[1]] = y.astype(o_ref.dtype)

    for r in range(T // rows):
        project(r)
        if r > 0:
            finish(r - 1)
    finish(T // rows - 1)


def swiglu_up(rw, l, x, nw, mod, wgu):
    R, D = x.shape
    tm = min(rw.tm, 512)
    tn = D_FF
    nj = D_FF // tn
    mi = rw.mod_index(tm)
    return pl.pallas_call(
        _swiglu_up_kernel,
        grid=(R // tm, nj),
        in_specs=[
            pl.BlockSpec((tm, D), lambda i, j: (i, 0)),
            pl.BlockSpec((None, 1, D), lambda i, j: (l, 0, 0)),
            pl.BlockSpec((None, 1, 8, D), lambda i, j: (l, mi(i), 0, 0)),
            pl.BlockSpec((None, D, tn), lambda i, j: (l, 0, j)),
            pl.BlockSpec((None, D, tn), lambda i, j: (l, 0, nj + j)),
        ],
        out_specs=pl.BlockSpec((tm, tn), lambda i, j: (i, j)),
        out_shape=jax.ShapeDtypeStruct((R, D_FF), BF16),
        scratch_shapes=[pltpu.VMEM((tm, D), BF16)] + [pltpu.VMEM((min(256, tm), tn), F32)] * 4,
        compiler_params=_cp("arbitrary", "arbitrary"),
        name="swiglu_up",
    )(x, nw, mod, wgu, wgu)


def _swiglu_down_kernel(a_ref, w_ref, x_ref, mod_ref, o_ref):
    o_ref[...] = x_ref[...] + mod_ref[0, 5:6, :] * _dot(a_ref[...], w_ref[...])


def swiglu_down(rw, l, a, w, x, mod):
    R, D = x.shape
    tm = min(rw.tm, 512)
    mi = rw.mod_index(tm)
    return pl.pallas_call(
        _swiglu_down_kernel,
        grid=(R // tm,),
        in_specs=[
            pl.BlockSpec((tm, D_FF), lambda i: (i, 0)),
            pl.BlockSpec((None, D_FF, D), lambda i: (l, 0, 0)),
            pl.BlockSpec((tm, D), lambda i: (i, 0)),
            pl.BlockSpec((None, 1, 8, D), lambda i: (l, mi(i), 0, 0)),
        ],
        out_specs=pl.BlockSpec((tm, D), lambda i: (i, 0)),
        out_shape=jax.ShapeDtypeStruct((R, D), F32),
        compiler_params=_cp("arbitrary"),
        name="swiglu_down",
    )(a, w, x, mod)


def _final_norm_kernel(x_ref, w_ref, o_ref):
    x = x_ref[...]
    ms = jnp.mean(x * x, axis=-1, keepdims=True)
    o_ref[...] = x * lax.rsqrt(ms + EPS) * w_ref[...]


def final_norm(rw, x, w):
    D = x.shape[1]
    tm = rw.tm
    n0 = rw.NC // tm
    nl = rw.B * rw.Ll
    return pl.pallas_call(
        _final_norm_kernel,
        grid=(nl // tm,),
        in_specs=[pl.BlockSpec((tm, D), lambda i: (n0 + i, 0)), pl.BlockSpec((1, D), lambda i: (0, 0))],
        out_specs=pl.BlockSpec((tm, D), lambda i: (i, 0)),
        out_shape=jax.ShapeDtypeStruct((nl, D), F32),
        compiler_params=_cp("arbitrary"),
        name="final_norm",
    )(x, w.reshape(1, D))


def _regroup_w_in(w_in):
    o_dt = 3072
    o_gdn = 3088
    o_a = o_gdn + 2048
    o_b = o_a + 8
    o_gate = o_gdn + 2064
    wt = jnp.swapaxes(w_in, 1, 2).astype(BF16)
    pieces = [
        wt[:, 0:3072],
        wt[:, o_gdn:o_gdn + 2048],
        wt[:, o_gate:o_gate + 3072],
        wt[:, o_dt:o_dt + 16],
        wt[:, o_a:o_a + 4], wt[:, o_b:o_b + 4],
        wt[:, o_a + 4:o_a + 8], wt[:, o_b + 4:o_b + 8],
        jnp.zeros((wt.shape[0], N_IN_PAD - C_SM - 32, wt.shape[2]), wt.dtype),
    ]
    return jnp.swapaxes(jnp.concatenate(pieces, axis=1), 1, 2)


def kernel(x, c, ctx, c_ctx, w_ada, b_ada, norm1_w, norm2_w, w_in, hy_conv_w, hy_conv_b, hy_w1, hy_b1, hy_w2, hy_b2, hy_w3, hy_freq, hy_bias, ssm_conv_w, ssm_conv_b, ssm_dt_bias, ssm_A_log, ssm_D, ssm_norm_w, gdn_conv_w, gdn_dt_bias, gdn_A_log, gdn_norm_w, w_hy_out, w_ssm_out, w_gdn_out, w_out, w_gate_up, w_down, final_norm_w):
    B, Ll, D = x.shape
    Lc = ctx.shape[1]
    depth = w_ada.shape[0]
    assert Lc == CONV_ROWS and D == D_MODEL and B <= 15
    rw = Rows(B, Lc, Ll)
    R, NC = rw.R, rw.NC

    xs = (ctx.reshape(B * Lc, D), x.reshape(B * Ll, D))

    svec = jnp.concatenate([c_ctx[None, :], c, jnp.zeros((15 - B, D), F32)], axis=0)
    mod = ada_modulation(svec, w_ada, b_ada)
    mod = jnp.pad(mod.reshape(depth, 16, 6, D), ((0, 0), (0, 0), (0, 2), (0, 0)))

    w_in_r = _regroup_w_in(w_in)
    par = _in_proj_params(hy_conv_w, hy_conv_b, ssm_conv_w, ssm_conv_b, gdn_conv_w, ssm_dt_bias, gdn_dt_bias,
                          gdn_A_log)
    norm1 = norm1_w.reshape(depth, 1, D)
    norm2 = norm2_w.reshape(depth, 1, D)
    w_hy_o, w_ssm_o, w_gdn_o, w_o = (w.astype(BF16) for w in (w_hy_out, w_ssm_out, w_gdn_out, w_out))
    w_gu, w_dn = w_gate_up.astype(BF16), w_down.astype(BF16)
    dft_l = (dft_table(Ll),) + dft_tables_split(Ll)
    dft_c = (dft_table(Lc),) + dft_tables_split(Lc)
    feat_l, feat_c = hy_features(Ll), hy_features(Lc)

    for l in range(depth):
        p, sm = in_proj(rw, xs, l, norm1, mod, w_in_r, par)

        sm32_t = sm[:, :32].T
        dt_t = sm32_t[:16].reshape(2, 8, R)
        g_t = sm32_t[16:32].reshape(16, R // GDN_CHUNK, GDN_CHUNK).transpose(1, 0, 2)

        alx = jnp.repeat(ssm_A_log[l], SSM_HEAD_DIM, axis=-1).reshape(2, 1, 512)
        alc = ssm_A_log[l].reshape(2, 8, 1)
        y_f, y_b = ssd_scan(rw, p, sm, dt_t, alx, alc)
        dx = jnp.repeat(ssm_D[l], SSM_HEAD_DIM).reshape(1, 512)

        o_f, o_b = gdn_scan(rw, *gdn_prep(p, sm, g_t))

        hyu = p
        parts = []
        for (Bn, L, blk0, (full, fwd, inv), feat) in ((B, Lc, 0, dft_c, feat_c),
                                                      (B, Ll, NC // Ll, dft_l, feat_l)):
            if NC % L:
                raise ValueError("latent length must divide the context row count")
            filt = hy_filter(feat, hy_w1[l], hy_b1[l], hy_w2[l], hy_b2[l], hy_w3[l], hy_freq[l])
            kspec = matmul(full, filt, min(512, 2 * L), 512)
            z1 = long_conv(Bn, L, hyu, blk0, 0, hyu, blk0, 1, hy_bias[l, 0], fwd, inv, kspec, 0, F32)
            yy = long_conv(Bn, L, z1, 0, 0, hyu, blk0, 2, hy_bias[l, 1], fwd, inv, kspec, 1, BF16)
            parts.append(yy)
        y_hy = tuple(parts)

        xa = merge(rw, l, y_hy, y_f, y_b, dx, ssm_norm_w[l].reshape(1, 512),
                   o_f, o_b, jnp.tile(gdn_norm_w[l], GDN_HEADS).reshape(1, 512), p,
                   w_hy_o, w_ssm_o, w_gdn_o, w_o, xs, mod)
        act = swiglu_up(rw, l, xa, norm2, mod, w_gu)
        xa = swiglu_down(rw, l, act, w_dn, xa, mod)
        xs = (xa, xa)

    out = final_norm(rw, xa, final_norm_w)
    return out.reshape(B, Ll, D)
```

```python
import functools
import math

import jax
import jax.numpy as jnp
import numpy as np
from jax import lax
from jax.experimental import pallas as pl
from jax.experimental.pallas import tpu as pltpu

F32 = jnp.float32
BF16 = jnp.bfloat16
HI = lax.Precision.HIGHEST

EPS = 1e-6
D_MODEL = 1024
GRID_W = 64

HY_WIDTH = 512
HY_BANDS = 16
HY_EMB = 1 + 2 * HY_BANDS
HY_HIDDEN = 64
HY_SHORT_DECAY_PCT = 0.3
HY_LONG_DECAY_PCT = 1.5
HY_TARGET = 1e-2

SSM_HEADS = 8
SSM_HEAD_DIM = 64
SSM_WIDTH = 512
SSM_GROUPS = 2
SSM_HPG = 4
SSM_STATE = 128
SSM_CHUNK = 128
SSM_GW = SSM_HPG * SSM_HEAD_DIM

GDN_HEADS = 4
GDN_DK = 128
GDN_DV = 128
GDN_CHUNK = 64

D_FF = 2816

C_HY = 0
C_Z = 1536
C_XBC = 2048
C_QKV = 3072
C_GG = 4608
C_GATE = 5120
C_SM = 8192

CONV_ROWS = 256
FREQ_BLK = 256

VMEM_LIMIT = 56 * 1024 * 1024


def _cp(*sem, flags=None):
    return pltpu.CompilerParams(dimension_semantics=sem, vmem_limit_bytes=VMEM_LIMIT, flags=flags)


def _sigmoid(x):
    return 1.0 / (1.0 + jnp.exp(-x))


def _silu(x):
    return x * _sigmoid(x)


def _softplus(x):
    return jnp.maximum(x, 0.0) + jnp.log1p(jnp.exp(-jnp.abs(x)))


def _dot(a, b, precision=None):
    return jnp.dot(a, b, precision=precision, preferred_element_type=F32)


def _dot_nt(a, b):
    return lax.dot_general(a, b, (((1,), (1,)), ((), ())), preferred_element_type=F32)


def _dot_tn(a, b):
    return lax.dot_general(a, b, (((0,), (0,)), ((), ())), preferred_element_type=F32)


def _ada_kernel(s_ref, w_ref, b_ref, o_ref):
    s = _silu(s_ref[...])
    o_ref[0] = _dot(s, w_ref[0], HI) + b_ref[0]


def ada_modulation(svec, w_ada, b_ada):
    depth = w_ada.shape[0]
    D = D_MODEL
    return pl.pallas_call(
        _ada_kernel,
        grid=(depth, 6),
        in_specs=[
            pl.BlockSpec((16, D), lambda l, j: (0, 0)),
            pl.BlockSpec((1, D, D), lambda l, j: (l, 0, j)),
            pl.BlockSpec((1, 1, D), lambda l, j: (l, 0, j)),
        ],
        out_specs=pl.BlockSpec((1, 16, D), lambda l, j: (l, 0, j)),
        out_shape=jax.ShapeDtypeStruct((depth, 16, 6 * D), F32),
        compiler_params=_cp("arbitrary", "arbitrary"),
        name="ada",
    )(svec, w_ada, b_ada.reshape(depth, 1, 6 * D))


def _norm_mod(x, nw, scale, shift):
    ms = jnp.mean(x * x, axis=-1, keepdims=True)
    return (x * lax.rsqrt(ms + EPS) * nw) * (1.0 + scale) + shift


IN_FLIGHT = 4
N_IN_PAD = C_SM + 128
IN_TN = N_IN_PAD // 5
MODE_RAW, MODE_CONV, MODE_CONV_SILU, MODE_CONV_SILU_L2, MODE_SMALL = range(5)


def _tile_mode(tile):
    col = tile * 128
    if col < C_Z:
        return MODE_CONV
    if col < C_XBC:
        return MODE_RAW
    if col < C_QKV:
        return MODE_CONV_SILU
    if col < C_QKV + 1024:
        return MODE_CONV_SILU_L2
    if col < C_GG:
        return MODE_CONV_SILU
    if col < C_SM:
        return MODE_RAW
    return MODE_SMALL
PAR_W0, PAR_W1, PAR_W2, PAR_BIAS, PAR_L2SCALE, PAR_SBIAS, PAR_SALOG, PAR_SKIND = range(8)


def _in_proj_kernel(xc_ref, xl_ref, nw_ref, mod_ref, w_ref, par_ref, o_ref, sm_ref, h_ref, raw0_ref, raw1_ref, *,
                    nctx_blk):
    j = pl.program_id(1)
    nj = N_IN_PAD // IN_TN
    raws = (raw0_ref, raw1_ref)

    @pl.when((j == 0) & (pl.program_id(0) < nctx_blk))
    def _():
        h = _norm_mod(xc_ref[...], nw_ref[...], mod_ref[0, 1:2, :], mod_ref[0, 0:1, :])
        h_ref[...] = h.astype(BF16)

    @pl.when((j == 0) & (pl.program_id(0) >= nctx_blk))
    def _():
        h = _norm_mod(xl_ref[...], nw_ref[...], mod_ref[0, 1:2, :], mod_ref[0, 0:1, :])
        h_ref[...] = h.astype(BF16)

    T = h_ref.shape[0]
    G = GRID_W
    per_ctx = CONV_ROWS // G
    is_latent = pl.program_id(0) >= nctx_blk
    sub = lax.broadcasted_iota(jnp.int32, (8, 128), 0)

    def raw_piece(src, g, c):
        return src[g * G:(g + 1) * G, c * 128:(c + 1) * 128]

    retired = []

    def retire(y):
        bits = pltpu.bitcast(y[0:8], jnp.int32)
        zero = lax.shift_right_logical(lax.shift_right_logical(bits, 16), 16)
        retired.append(jnp.tile(zero.astype(F32), (G // 8, 1)))

    def conv(src, g, c):
        cs = slice(c * 128, (c + 1) * 128)
        x = raw_piece(src, g, c)
        if len(retired) >= IN_FLIGHT:
            x = x + retired[-IN_FLIGHT]
        zero = jnp.zeros((1, 128), F32)
        before = zero if g % per_ctx == 0 else jnp.where(is_latent, 0.0, src[g * G - 1:g * G, cs])
        after = zero if g % per_ctx == per_ctx - 1 else jnp.where(is_latent, 0.0, src[(g + 1) * G:(g + 1) * G + 1, cs])
        rp = pltpu.roll(x, 1, 0)
        rn = pltpu.roll(x, G - 1, 0)
        prev = jnp.concatenate([jnp.where(sub == 0, before, rp[0:8]), rp[8:]], axis=0)
        nxt = jnp.concatenate([rn[:G - 8], jnp.where(sub == 7, after, rn[G - 8:])], axis=0)
        return (prev * par_ref[PAR_W0:PAR_W0 + 1, cs] + x * par_ref[PAR_W1:PAR_W1 + 1, cs]
                + nxt * par_ref[PAR_W2:PAR_W2 + 1, cs] + par_ref[PAR_BIAS:PAR_BIAS + 1, cs])

    def conv_silu(src, g, c):
        return _silu(conv(src, g, c))

    def conv_silu_l2(src, g, c):
        y = _silu(conv(src, g, c))
        y = y * lax.rsqrt(jnp.sum(y * y, axis=-1, keepdims=True) + EPS)
        return y * par_ref[PAR_L2SCALE:PAR_L2SCALE + 1, c * 128:(c + 1) * 128]

    def small(src, g, c):
        cs = slice(c * 128, (c + 1) * 128)
        acc = raw_piece(src, g, c)
        kind = par_ref[PAR_SKIND:PAR_SKIND + 1, cs]
        sp = _softplus(acc + par_ref[PAR_SBIAS:PAR_SBIAS + 1, cs])
        dec = -jnp.exp(par_ref[PAR_SALOG:PAR_SALOG + 1, cs]) * sp
        return jnp.where(kind == 0.0, sp, jnp.where(kind == 1.0, dec, jnp.where(kind == 2.0, _sigmoid(acc), 0.0)))

    rows_mm = 256
    tiles = IN_TN // 128
    piece_fn = {MODE_RAW: raw_piece, MODE_CONV: conv, MODE_CONV_SILU: conv_silu,
                MODE_CONV_SILU_L2: conv_silu_l2, MODE_SMALL: small}

    heavy_modes = (MODE_CONV_SILU, MODE_CONV_SILU_L2)
    col_slices = [slice(c0, min(c0 + 256, IN_TN)) for c0 in range(0, IN_TN, 256)]

    def project(dst, r, cs=slice(None)):
        rs = slice(r * rows_mm, (r + 1) * rows_mm)
        dst[rs, cs] = _dot(h_ref[rs, :], w_ref[:, cs])

    def finish(src, blk, g, c):
        mode = _tile_mode(blk * tiles + c)
        y = piece_fn[mode](src, g, c)
        if mode in heavy_modes:
            retire(y)
        if mode == MODE_SMALL:
            sm_ref[g * G:(g + 1) * G, :] = y
            y = jnp.zeros_like(y)
        o_ref[g * G:(g + 1) * G, c * 128:(c + 1) * 128] = y.astype(o_ref.dtype)

    for step in range(nj + 1):
        @pl.when(j == step)
        def _(step=step):
            blk = step - 1
            src, dst = raws[blk % 2], raws[step % 2]
            retired.clear()
            light = blk >= 0 and not any(_tile_mode(blk * tiles + c) in heavy_modes for c in range(tiles))
            for r in range(T // rows_mm):
                pieces = [] if blk < 0 else [(g, c) for g in range(r * rows_mm // G, (r + 1) * rows_mm // G)
                                             for c in range(tiles)]
                if step == nj:
                    slabs = []
                elif light:
                    slabs = col_slices
                else:
                    slabs = [slice(None)]
                per = -(-len(pieces) // max(len(slabs), 1))
                for n in range(max(len(slabs), 1)):
                    if n < len(slabs):
                        project(dst, r, slabs[n])
                    for g, c in pieces[n * per:(n + 1) * per]:
                        finish(src, blk, g, c)


class Rows:
    def __init__(self, B, Lc, Ll):
        self.B, self.Lc, self.Ll = B, Lc, Ll
        self.NC = B * Lc
        self.R = B * Lc + B * Ll
        assert self.NC % Ll == 0 or Ll % self.NC == 0
        tm = 1024
        while self.NC % tm or Ll % tm:
            tm //= 2
        self.tm = tm

    def mod_index(self, tm):
        nctx = self.NC // tm
        per = self.Ll // tm
        return lambda i: jnp.where(i < nctx, 0, 1 + (i - nctx) // per)


def _stream_specs(rw, tm, xs, ngrid, skip=0):
    xc, xl = xs
    nctx = rw.NC // tm
    off = nctx if xl.shape[0] == rw.R else 0
    D = xc.shape[1]
    if ngrid == 1:
        return [pl.BlockSpec((tm, D), lambda i: (jnp.minimum(i + skip, nctx - 1), 0)),
                pl.BlockSpec((tm, D), lambda i: (jnp.maximum(i + skip - nctx, 0) + off, 0))]
    return [pl.BlockSpec((tm, D), lambda i, j: (jnp.minimum(i + skip, nctx - 1), 0)),
            pl.BlockSpec((tm, D), lambda i, j: (jnp.maximum(i + skip - nctx, 0) + off, 0))]


def in_proj(rw, xs, l, nw, mod, w, par):
    R = rw.R
    D = xs[0].shape[1]
    N = w.shape[2]
    tm, tn = rw.tm, IN_TN
    nj = N // tn
    assert N == N_IN_PAD
    mi = rw.mod_index(tm)
    done = lambda j: jnp.maximum(j - 1, 0)
    return pl.pallas_call(
        functools.partial(_in_proj_kernel, nctx_blk=rw.NC // tm),
        grid=(R // tm, nj + 1),
        in_specs=_stream_specs(rw, tm, xs, 2) + [
            pl.BlockSpec((None, 1, D), lambda i, j: (l, 0, 0)),
            pl.BlockSpec((None, 1, 8, D), lambda i, j: (l, mi(i), 0, 0)),
            pl.BlockSpec((None, D, tn), lambda i, j: (l, 0, jnp.minimum(j, nj - 1))),
            pl.BlockSpec((None, 8, tn), lambda i, j: (l, 0, done(j))),
        ],
        out_specs=[pl.BlockSpec((tm, tn), lambda i, j: (i, done(j))),
                   pl.BlockSpec((tm, 128), lambda i, j: (i, 0))],
        out_shape=[jax.ShapeDtypeStruct((R, N), BF16), jax.ShapeDtypeStruct((R, 128), F32)],
        scratch_shapes=[pltpu.VMEM((tm, D), BF16), pltpu.VMEM((tm, tn), F32), pltpu.VMEM((tm, tn), F32)],
        compiler_params=_cp("arbitrary", "arbitrary"),
        name="in_proj",
    )(xs[0], xs[1], nw, mod, w, par)


def _in_proj_params(hy_conv_w, hy_conv_b, ssm_conv_w, ssm_conv_b, gdn_conv_w, ssm_dt_bias, gdn_dt_bias, gdn_A_log):
    depth = hy_conv_w.shape[0]

    def row(pieces):
        out, pos = [], 0
        for off, a in pieces:
            out += [jnp.zeros((depth, off - pos), F32), a.astype(F32)]
            pos = off + a.shape[1]
        return jnp.concatenate(out + [jnp.zeros((depth, N_IN_PAD - pos), F32)], axis=1)
    z4 = jnp.zeros((depth, 4), F32)
    conv = [row([(C_HY, hy_conv_w[:, t]), (C_XBC, ssm_conv_w[:, t]), (C_QKV, gdn_conv_w[:, t])]) for t in range(3)]
    bias = row([(C_HY, hy_conv_b), (C_XBC, ssm_conv_b)])
    l2s = row([(C_QKV, jnp.full((depth, 512), GDN_DK ** -0.5, F32)), (C_QKV + 512, jnp.ones((depth, 512), F32))])
    sbias = row([(C_SM, jnp.concatenate([ssm_dt_bias.reshape(depth, 16), gdn_dt_bias[:, 0], z4,
                                         gdn_dt_bias[:, 1], z4], axis=1))])
    salog = row([(C_SM + 16, jnp.concatenate([gdn_A_log[:, 0], z4, gdn_A_log[:, 1], z4], axis=1))])
    kind = np.full((depth, N_IN_PAD), 3.0, np.float32)
    kind[:, C_SM:C_SM + 16] = 0.0
    kind[:, C_SM + 16:C_SM + 20] = 1.0
    kind[:, C_SM + 24:C_SM + 28] = 1.0
    kind[:, C_SM + 20:C_SM + 24] = 2.0
    kind[:, C_SM + 28:C_SM + 32] = 2.0
    return jnp.stack(conv + [bias, l2s, sbias, salog, jnp.asarray(kind)], axis=1)


def _hy_filter_kernel(z_ref, w1_ref, b1_ref, w2_ref, b2_ref, w3_ref, f0_ref, f1_ref, win_ref, o_ref, h_ref):
    @pl.when(pl.program_id(1) == 0)
    def _():
        h1 = jnp.sin(f0_ref[...] * (_dot(z_ref[...], w1_ref[...], HI) + b1_ref[...]))
        h_ref[...] = jnp.sin(f1_ref[...] * (_dot(h1, w2_ref[...], HI) + b2_ref[...]))

    a1 = h_ref[...].astype(BF16)
    a2 = (h_ref[...] - a1.astype(F32)).astype(BF16)
    b1 = w3_ref[...].astype(BF16)
    b2 = (w3_ref[...] - b1.astype(F32)).astype(BF16)
    h = (_dot(a1, b1) + _dot(a1, b2) + _dot(a2, b1)) * win_ref[...]
    tl = h.shape[0]
    row = lax.broadcasted_iota(jnp.int32, (tl, 1), 0) + pl.program_id(0) * tl
    drop = (row == 0) & (pl.program_id(1) % 2 == 1)
    o_ref[...] = jnp.where(drop, 0.0, h).astype(o_ref.dtype)


def hy_features(L):
    t = jnp.linspace(0.0, 1.0, L, dtype=F32)[:, None]
    w = 2.0 * math.pi * jnp.arange(L, dtype=F32)[:, None] / L
    f = jnp.linspace(1e-4, HY_BANDS - 1, HY_BANDS, dtype=F32)[None, :]
    z = jnp.concatenate([t, jnp.cos(f * w), -jnp.sin(f * w)], axis=-1)
    z = jnp.pad(z, ((0, 0), (0, 128 - HY_EMB)))
    min_decay = math.log(HY_TARGET) / HY_LONG_DECAY_PCT
    max_decay = math.log(HY_TARGET) / HY_SHORT_DECAY_PCT
    deltas = jnp.linspace(min_decay, max_decay, HY_WIDTH, dtype=F32)
    window = jnp.exp(-t * jnp.abs(deltas))
    return z, window


def hy_filter(feat, w1, b1, w2, b2, w3, freq):
    z, window = feat
    L = z.shape[0]
    H = HY_HIDDEN
    w1p = jnp.pad(w1, ((0, 128 - HY_EMB), (0, 128 - H)))
    w2p = jnp.pad(w2, ((0, 128 - H), (0, 128 - H)))
    w3p = jnp.pad(w3, ((0, 128 - H), (0, 0)))
    pad1 = lambda v: jnp.pad(v, (0, 128 - H)).reshape(1, 128)
    tl = 256
    full = lambda shape: pl.BlockSpec(shape, lambda i, j: (0, 0))
    return pl.pallas_call(
        _hy_filter_kernel,
        grid=(L // tl, 4),
        in_specs=[
            pl.BlockSpec((tl, 128), lambda i, j: (i, 0)),
            full((128, 128)), full((1, 128)), full((128, 128)), full((1, 128)),
            pl.BlockSpec((128, HY_WIDTH), lambda i, j: (0, j)),
            full((1, 128)), full((1, 128)),
            pl.BlockSpec((tl, HY_WIDTH), lambda i, j: (i, 0)),
        ],
        out_specs=pl.BlockSpec((tl, HY_WIDTH), lambda i, j: (i, j)),
        out_shape=jax.ShapeDtypeStruct((L, 4 * HY_WIDTH), BF16),
        scratch_shapes=[pltpu.VMEM((tl, 128), F32)],
        compiler_params=_cp("arbitrary", "arbitrary"),
        name="hy_filter",
    )(z, w1p, pad1(b1), w2p, pad1(b2), w3p, pad1(freq[0]), pad1(freq[1]), window)


def dft_table(L):
    N = 2 * L
    H = L // 2
    q = np.arange(L, dtype=np.int64)
    f = np.where(q < H, q, L + H - q)[:, None]
    s = np.arange(L, dtype=np.int64)[None, :]
    ang = ((f * s) % N).astype(np.float64) * (2.0 * math.pi / N)
    mid = ((H * s) % N).astype(np.float64) * (2.0 * math.pi / N)
    q = q[:, None]
    imag = np.where(q == 0, np.cos(mid), np.where(q == H, -np.sin(mid), -np.sin(ang)))
    return jnp.asarray(np.concatenate([np.cos(ang), imag], axis=0), dtype=BF16)


def dft_tables_split(L):
    N = 2 * L
    H = L // 2
    q = np.arange(H, dtype=np.int64)[:, None]
    m = np.arange(H, dtype=np.int64)[None, :]
    ang_e = ((q * 2 * m) % N).astype(np.float64) * (2.0 * math.pi / N)
    ang_o = ((q * (2 * m + 1)) % N).astype(np.float64) * (2.0 * math.pi / N)
    alt = (1 - 2 * (m % 2)).astype(np.float64)
    ce, co = np.cos(ang_e), np.cos(ang_o)
    se = np.where(q == 0, alt, -np.sin(ang_e))
    so = np.where(q == 0, -alt, -np.sin(ang_o))
    w = np.where(q == 0, 1.0, 2.0) / N
    ise = np.where(q == 0, 2.0 / N * alt, -np.sin(ang_e) * w)
    iso = np.where(q == 0, -2.0 / N * alt, -np.sin(ang_o) * w)
    fwd = np.stack([ce, co, se, so])
    inv = np.stack([(ce * w).T, ise.T, (co * w).T, iso.T])
    return jnp.asarray(fwd, dtype=BF16), jnp.asarray(inv, dtype=BF16)


def _matmul_kernel(a_ref, b_ref, o_ref):
    o_ref[...] = _dot(a_ref[...], b_ref[...])


def matmul(a, b, tm, tn):
    M, K = a.shape
    N = b.shape[1]
    return pl.pallas_call(
        _matmul_kernel,
        grid=(M // tm, N // tn),
        in_specs=[pl.BlockSpec((tm, K), lambda i, j: (i, 0)), pl.BlockSpec((K, tn), lambda i, j: (0, j))],
        out_specs=pl.BlockSpec((tm, tn), lambda i, j: (i, j)),
        out_shape=jax.ShapeDtypeStruct((M, N), F32),
        compiler_params=_cp("arbitrary", "arbitrary"),
        name="matmul",
    )(a, b)


def _long_conv_kernel(u_ref, g_ref, bias_ref, fwd_ref, inv_ref, ar0_ref, ar1_ref, ar0h_ref, ar1h_ref,
                      ai0_ref, ai1_ref, ai0h_ref, ai1h_ref, o_ref, ue_ref, uo_ref, acce_ref, acco_ref, y_ref):
    f = pl.program_id(1)
    half = ue_ref.shape[0]

    lane_tiles = [slice(c * 128, (c + 1) * 128) for c in range(y_ref.shape[0])]

    @pl.when(f == 0)
    def _():
        for c, cs in enumerate(lane_tiles):
            y_c = y_ref.at[c]
            y_c[...] = u_ref[:, cs].astype(F32)
            ue_ref[:, cs] = y_c[pl.ds(0, half, stride=2), :].astype(BF16)
            uo_ref[:, cs] = y_c[pl.ds(1, half, stride=2), :].astype(BF16)
        acce_ref[...] = jnp.zeros_like(acce_ref)
        acco_ref[...] = jnp.zeros_like(acco_ref)

    ue, uo = ue_ref[...], uo_ref[...]
    ae, ao = _dot(fwd_ref[0], ue), _dot(fwd_ref[1], uo)
    be, bo = _dot(fwd_ref[2], ue), _dot(fwd_ref[3], uo)
    ur, ur2 = ae + ao, ae - ao
    ui, ui2 = be + bo, bo - be
    first = (lax.broadcasted_iota(jnp.int32, (fwd_ref.shape[1], 1), 0) == 0) & (f == 0)
    kr, kr2 = ar0_ref[...] + ar1_ref[...], ar0h_ref[...] + ar1h_ref[...]
    ki = jnp.where(first, ai0_ref[...] + ai1_ref[...], ai0_ref[...] - ai1_ref[...])
    ki2 = ai0h_ref[...] - ai1h_ref[...]
    pr, pi = ur * kr - ui * ki, ur * ki + ui * kr
    pr2, pi2 = ur2 * kr2 - ui2 * ki2, ur2 * ki2 + ui2 * kr2
    dc, ny = ur * kr, ur2 * kr2
    gr = jnp.where(first, dc + ny, pr + pr2)
    gi = jnp.where(first, be * ki - bo * ki2, pi - pi2)
    hr = jnp.where(first, dc - ny, pr - pr2)
    hi = jnp.where(first, be * ki2 + bo * ki, pi + pi2)
    acce_ref[...] += _dot(inv_ref[0], gr.astype(BF16)) + _dot(inv_ref[1], gi.astype(BF16))
    acco_ref[...] += _dot(inv_ref[2], hr.astype(BF16)) + _dot(inv_ref[3], hi.astype(BF16))

    @pl.when(f == pl.num_programs(1) - 1)
    def _():
        for c, cs in enumerate(lane_tiles):
            y_c = y_ref.at[c]
            y_c[pl.ds(0, half, stride=2), :] = acce_ref[:, cs]
            y_c[pl.ds(1, half, stride=2), :] = acco_ref[:, cs]
            u = u_ref[:, cs].astype(F32)
            o_ref[:, cs] = (g_ref[:, cs].astype(F32) * (y_c[...] + u * bias_ref[:, cs])).astype(o_ref.dtype)


def long_conv(B, L, u, u_rb0, u_cb, gate, g_rb0, gate_cb, bias, fwd, inv, kspec, order, out_dtype):
    C = HY_WIDTH
    H = L // 2
    FB = min(FREQ_BLK, H)
    nfb = H // FB
    kblk = lambda part, d: pl.BlockSpec((FB, C), lambda b, f: (part * nfb + f, 2 * order + d))
    return pl.pallas_call(
        _long_conv_kernel,
        grid=(B, nfb),
        in_specs=[
            pl.BlockSpec((L, C), lambda b, f: (u_rb0 + b, u_cb)),
            pl.BlockSpec((L, C), lambda b, f: (g_rb0 + b, gate_cb)),
            pl.BlockSpec((1, C), lambda b, f: (0, 0)),
            pl.BlockSpec((4, FB, H), lambda b, f: (0, f, 0)),
            pl.BlockSpec((4, H, FB), lambda b, f: (0, 0, f)),
            kblk(0, 0), kblk(0, 1), kblk(1, 0), kblk(1, 1), kblk(2, 0), kblk(2, 1), kblk(3, 0), kblk(3, 1),
        ],
        out_specs=pl.BlockSpec((L, C), lambda b, f: (b, 0)),
        out_shape=jax.ShapeDtypeStruct((B * L, C), out_dtype),
        scratch_shapes=[pltpu.VMEM((H, C), BF16), pltpu.VMEM((H, C), BF16),
                        pltpu.VMEM((H, C), F32), pltpu.VMEM((H, C), F32), pltpu.VMEM((C // 128, L, 128), F32)],
        compiler_params=_cp("arbitrary", "arbitrary"),
        name="long_conv",
    )(u, gate, bias.reshape(1, C), fwd, inv, *([kspec] * 8))


def _scan_blocks(rw, rows):
    nbc, nbl, base = rw.Lc // rows, rw.Ll // rows, rw.NC // rows

    def make(d):
        def f(b, s):
            jc = s if d == 0 else nbc - 1 - s
            jl = (s - nbc) if d == 0 else nbl - 1 - (s - nbc)
            return jnp.where(s < nbc, b * nbc + jc, base + b * nbl + jl)
        return f

    return [make(0), make(1)], nbc + nbl


def _expand_lanes(x, base, n, width):
    rows = x.shape[0]
    per = 128 // width
    lane = lax.broadcasted_iota(jnp.int32, (rows, 128), 1)
    tiles = []
    for t in range(n // per):
        c0 = base + t * per
        tile = jnp.broadcast_to(x[:, c0:c0 + 1], (rows, 128))
        for i in range(1, per):
            tile = jnp.where(lane >= i * width, jnp.broadcast_to(x[:, c0 + i:c0 + i + 1], (rows, 128)), tile)
        tiles.append(tile)
    return jnp.concatenate(tiles, axis=1)


def _ssd_kernel(xf, bf, cf, smf, dtf, xb, bb, cb_, smb, dtb, alx_ref, alc_ref, of_ref, ob_ref, h_ref):
    Q = SSM_CHUNK
    GW = SSM_GW

    @pl.when(pl.program_id(1) == 0)
    def _():
        h_ref[...] = jnp.zeros_like(h_ref)

    row = lax.broadcasted_iota(jnp.int32, (Q, Q), 0)
    col = lax.broadcasted_iota(jnp.int32, (Q, Q), 1)
    lane_head = lax.broadcasted_iota(jnp.int32, (Q, GW), 1) // SSM_HEAD_DIM
    dirs = ((xf, bf, cf, smf, dtf, of_ref), (xb, bb, cb_, smb, dtb, ob_ref))
    jobs = []
    for d in range(2):
        x_ref, b_ref, c_ref, sm_ref, dt_ref, o_ref = dirs[d]
        keep = (col <= row) if d == 0 else (col >= row)
        tri = keep.astype(BF16)
        tri_t = ((row <= col) if d == 0 else (row >= col)).astype(BF16)
        sm = sm_ref[...]
        a_x = -jnp.exp(alx_ref[d])
        dtx = _expand_lanes(sm, 8 * d, SSM_HEADS, SSM_HEAD_DIM)
        cumx = _expand_lanes(_dot_01_lhs(tri, sm), 8 * d, SSM_HEADS, SSM_HEAD_DIM) * a_x
        cumr = _dot_01_rhs(dt_ref[0], tri_t) * (-jnp.exp(alc_ref[d]))
        last = Q - 1 if d == 0 else 0
        totx = cumx[last:last + 1, :]
        xd = x_ref[...].astype(F32) * dtx
        xdw = xd * jnp.exp(totx - cumx)
        ecum = jnp.exp(cumx)
        for g in range(SSM_GROUPS):
            gs = slice(g * GW, (g + 1) * GW)
            jobs.append(dict(d=d, g=g, gs=gs, keep=keep, cumx=cumx, cumr=cumr, o_ref=o_ref,
                             bg=b_ref[:, g * SSM_STATE:(g + 1) * SSM_STATE].astype(BF16),
                             cg=c_ref[:, g * SSM_STATE:(g + 1) * SSM_STATE].astype(BF16),
                             xdg=xd[:, gs], xdw=xdw[:, gs].astype(BF16), ecum=ecum[:, gs],
                             etot=jnp.exp(totx[:, gs])))
    for j in jobs:
        j["cb"] = _dot_nt(j["cg"], j["bg"])
        j["h"] = h_ref[j["d"], j["g"]]
    for j in jobs:
        ms, xs = [], []
        for e4 in range(SSM_HPG):
            e = j["g"] * SSM_HPG + e4
            diff = j["cumx"][:, e * SSM_HEAD_DIM:e * SSM_HEAD_DIM + 1] - j["cumr"][e:e + 1, :]
            ms.append((j["cb"] * jnp.where(j["keep"], jnp.exp(diff), 0.0)).astype(BF16))
            xs.append(jnp.where(lane_head == e4, j["xdg"], 0.0).astype(BF16))
        yd = _dot(jnp.concatenate(ms, axis=1), jnp.concatenate(xs, axis=0))
        y_off = _dot(j["cg"], j["h"].astype(BF16)) * j["ecum"]
        j["o_ref"][:, j["gs"]] = (yd + y_off).astype(BF16)
    for j in jobs:
        h_ref[j["d"], j["g"]] = j["h"] * j["etot"] + _dot_tn(j["bg"], j["xdw"])


def ssd_scan(rw, p, sm, dtT, alx, alc):
    Q = SSM_CHUNK
    blks, nsteps = _scan_blocks(rw, Q)
    R = p.shape[0]
    in_specs = []
    for d in range(2):
        f = blks[d]
        in_specs += [
            pl.BlockSpec((Q, 512), lambda b, s, f=f: (f(b, s), C_XBC // 512)),
            pl.BlockSpec((Q, 256), lambda b, s, f=f: (f(b, s), C_XBC // 256 + 2)),
            pl.BlockSpec((Q, 256), lambda b, s, f=f: (f(b, s), C_XBC // 256 + 3)),
            pl.BlockSpec((Q, 128), lambda b, s, f=f: (f(b, s), 0)),
            pl.BlockSpec((1, 8, Q), lambda b, s, f=f, d=d: (d, 0, f(b, s))),
        ]
    in_specs += [pl.BlockSpec((2, 1, 512), lambda b, s: (0, 0, 0)), pl.BlockSpec((2, 8, 1), lambda b, s: (0, 0, 0))]
    ops = (p, p, p, sm, dtT)
    return pl.pallas_call(
        _ssd_kernel,
        grid=(rw.B, nsteps),
        in_specs=in_specs,
        out_specs=[pl.BlockSpec((Q, 512), lambda b, s, f=blks[d]: (f(b, s), 0)) for d in range(2)],
        out_shape=[jax.ShapeDtypeStruct((R, 512), BF16)] * 2,
        scratch_shapes=[pltpu.VMEM((2, SSM_GROUPS, SSM_STATE, SSM_GW), F32)],
        compiler_params=_cp("arbitrary", "arbitrary"),
        name="ssd_scan",
    )(*ops, *ops, alx, alc)


def _split3(x):
    x1 = x.astype(BF16)
    r = x - x1.astype(F32)
    x2 = r.astype(BF16)
    x3 = (r - x2.astype(F32)).astype(BF16)
    return x1, x2, x3


def _dot_01_lhs(m01, x):
    x1, x2, x3 = _split3(x)
    return _dot(m01, x1) + _dot(m01, x2) + _dot(m01, x3)


def _dot_01_rhs(x, m01):
    x1, x2, x3 = _split3(x)
    return _dot(x1, m01) + _dot(x2, m01) + _dot(x3, m01)


GDN_ROWS = 256


def _gdn_prep_kernel(q_ref, k_ref, v_ref, sm_ref, gT_ref, u_ref, w_ref, qg_ref, kd_ref, qk_ref, egl_ref):
    C = GDN_CHUNK
    row = lax.broadcasted_iota(jnp.int32, (C, C), 0)
    col = lax.broadcasted_iota(jnp.int32, (C, C), 1)
    jobs = []
    levels = []
    for d in range(2):
        keep = (col <= row) if d == 0 else (col >= row)
        late, early = (row, col) if d == 0 else (col, row)
        levels.append([(((row ^ col) >> (j + 1)) == 0) & ((late & (1 << j)) != 0) & ((early & (1 << j)) == 0)
                       for j in range(6)])
        tri = keep.astype(BF16)
        tri_t = ((row <= col) if d == 0 else (row >= col)).astype(BF16)
        last = C - 1 if d == 0 else 0
        for c in range(GDN_ROWS // C):
            rows = slice(c * C, (c + 1) * C)
            smc = sm_ref[rows, :]
            cums = _dot_01_lhs(tri, smc)
            cumr = _dot_01_rhs(gT_ref[c, 8 * d:8 * d + 8, :], tri_t)
            tot = cums[last:last + 1, :]
            for h in range(GDN_HEADS):
                lg = 16 + 8 * d + h
                jobs.append(dict(d=d, c=c, h=h, rows=rows, hs=slice(h * 128, (h + 1) * 128), keep=keep,
                                 gc=cums[:, lg:lg + 1], beta=smc[:, lg + 4:lg + 5],
                                 gl=tot[:, lg:lg + 1], gr=cumr[h:h + 1, :]))
    for j in jobs:
        q = q_ref[j["rows"], j["hs"]].astype(F32)
        k = k_ref[j["rows"], j["hs"]].astype(F32)
        j["dec"] = jnp.where(j["keep"], jnp.exp(j["gc"] - j["gr"]), 0.0)
        kb = k * j["beta"]
        both = _dot_nt(jnp.concatenate([kb, q], axis=0).astype(BF16), k.astype(BF16))
        j["a"] = both[:C] * j["dec"]
        j["n"] = -jnp.where(levels[j["d"]][0], j["a"], 0.0)
        qk_ref[j["d"], j["c"], j["h"]] = (both[C:] * j["dec"]).astype(BF16)
    for lev in range(1, 6):
        for j in jobs:
            l = jnp.where(levels[j["d"]][lev], j["a"], 0.0)
            j["y"] = l + _dot(l.astype(BF16), j["n"].astype(BF16))
        for j in jobs:
            j["n"] = j["n"] - j["y"] - _dot(j["n"].astype(BF16), j["y"].astype(BF16))
    for j in jobs:
        d, rows, hs, gc, gl, beta = j["d"], j["rows"], j["hs"], j["gc"], j["gl"], j["beta"]
        q = q_ref[rows, hs].astype(F32)
        k = k_ref[rows, hs].astype(F32)
        eg = jnp.exp(gc)
        rhs = jnp.concatenate([v_ref[rows, hs].astype(F32) * beta, k * beta * eg], axis=1)
        sol = rhs + _dot(j["n"].astype(BF16), rhs.astype(BF16))
        u_ref[d, rows, hs] = sol[:, :GDN_DV].astype(BF16)
        w_ref[d, rows, hs] = sol[:, GDN_DV:].astype(BF16)
        qg_ref[d, rows, hs] = (q * eg).astype(BF16)
        kd_ref[d, rows, hs] = (k * jnp.exp(gl - gc)).astype(BF16)
        egl_ref[d, j["c"], :, hs] = jnp.broadcast_to(jnp.exp(gl), (8, 128))


def gdn_prep(p, sm, gT):
    R = p.shape[0]
    T, C = GDN_ROWS, GDN_CHUNK
    nc = T // C
    col = lambda k: pl.BlockSpec((T, 512), lambda i: (i, C_QKV // 512 + k))
    dirrow = pl.BlockSpec((2, T, 512), lambda i: (0, i, 0))
    return pl.pallas_call(
        _gdn_prep_kernel,
        grid=(R // T,),
        in_specs=[col(0), col(1), col(2),
                  pl.BlockSpec((T, 128), lambda i: (i, 0)),
                  pl.BlockSpec((nc, 16, C), lambda i: (i, 0, 0))],
        out_specs=[dirrow, dirrow, dirrow, dirrow,
                   pl.BlockSpec((2, nc, GDN_HEADS, C, C), lambda i: (0, i, 0, 0, 0)),
                   pl.BlockSpec((2, nc, 8, 512), lambda i: (0, i, 0, 0))],
        out_shape=[jax.ShapeDtypeStruct((2, R, 512), BF16),
                   jax.ShapeDtypeStruct((2, R, 512), BF16),
                   jax.ShapeDtypeStruct((2, R, 512), BF16),
                   jax.ShapeDtypeStruct((2, R, 512), BF16),
                   jax.ShapeDtypeStruct((2, R // C, GDN_HEADS, C, C), BF16),
                   jax.ShapeDtypeStruct((2, R // C, 8, 512), F32)],
        compiler_params=_cp("arbitrary"),
        name="gdn_prep",
    )(p, p, p, sm, gT)


def _gdn_scan_kernel(uf, wf, qgf, kdf, qkf, eglf, ub, wb, qgb, kdb, qkb, eglb, of_ref, ob_ref, s_ref):
    C = GDN_CHUNK
    nch = GDN_ROWS // C

    @pl.when(pl.program_id(1) == 0)
    def _():
        s_ref[...] = jnp.zeros_like(s_ref)

    dirs = ((uf, wf, qgf, kdf, qkf, eglf, of_ref), (ub, wb, qgb, kdb, qkb, eglb, ob_ref))
    chains = [(d, h) for d in range(2) for h in range(GDN_HEADS)]
    S = {ch: s_ref[ch[0], ch[1]] for ch in chains}
    for i in range(nch):
        Sb, vnb, rows_of, c_of = {}, {}, {}, {}
        for d, h in chains:
            c_of[d] = i if d == 0 else nch - 1 - i
            rows_of[d] = slice(c_of[d] * C, (c_of[d] + 1) * C)
        for d, h in chains:
            hs = slice(h * 128, (h + 1) * 128)
            Sb[d, h] = S[d, h].astype(BF16)
            v_new = dirs[d][0][0, rows_of[d], hs].astype(F32) - _dot(dirs[d][1][0, rows_of[d], hs], Sb[d, h])
            vnb[d, h] = v_new.astype(BF16)
        for d, h in chains:
            hs = slice(h * 128, (h + 1) * 128)
            u_ref, w_ref, qg_ref, kd_ref, qk_ref, egl_ref, o_ref = dirs[d]
            S[d, h] = S[d, h] * egl_ref[0, c_of[d], 0:1, hs] + _dot_tn(kd_ref[0, rows_of[d], hs], vnb[d, h])
        for d, h in chains:
            hs = slice(h * 128, (h + 1) * 128)
            u_ref, w_ref, qg_ref, kd_ref, qk_ref, egl_ref, o_ref = dirs[d]
            o_ref[rows_of[d], hs] = (_dot(qg_ref[0, rows_of[d], hs], Sb[d, h])
                                     + _dot(qk_ref[0, c_of[d], h], vnb[d, h])).astype(BF16)
    for ch in chains:
        s_ref[ch[0], ch[1]] = S[ch]


def gdn_scan(rw, u, w, qg, kd, qk, egl):
    T, C = GDN_ROWS, GDN_CHUNK
    nc = T // C
    R = u.shape[1]
    nbc, nbl, base = rw.Lc // T, rw.Ll // T, rw.NC // T

    def blk(d):
        def f(b, s):
            jc = s if d == 0 else nbc - 1 - s
            jl = (s - nbc) if d == 0 else nbl - 1 - (s - nbc)
            return jnp.where(s < nbc, b * nbc + jc, base + b * nbl + jl)
        return f

    in_specs = []
    for d in range(2):
        f = blk(d)
        rowspec = pl.BlockSpec((1, T, 512), lambda b, s, f=f, d=d: (d, f(b, s), 0))
        in_specs += [rowspec, rowspec, rowspec, rowspec,
                     pl.BlockSpec((1, nc, GDN_HEADS, C, C), lambda b, s, f=f, d=d: (d, f(b, s), 0, 0, 0)),
                     pl.BlockSpec((1, nc, 8, 512), lambda b, s, f=f, d=d: (d, f(b, s), 0, 0))]
    out_specs = [pl.BlockSpec((T, 512), lambda b, s, f=blk(d): (f(b, s), 0)) for d in range(2)]
    ops = (u, w, qg, kd, qk, egl)
    return pl.pallas_call(
        _gdn_scan_kernel,
        grid=(rw.B, nbc + nbl),
        in_specs=in_specs,
        out_specs=out_specs,
        out_shape=[jax.ShapeDtypeStruct((R, 512), BF16)] * 2,
        scratch_shapes=[pltpu.VMEM((2, GDN_HEADS, GDN_DK, GDN_DV), F32)],
        compiler_params=_cp("arbitrary", "arbitrary"),
        name="gdn_scan",
    )(*ops, *ops)


def _merge_kernel(yhc_ref, yhl_ref, sf_ref, sb_ref, sx_ref, sz_ref, dx_ref, snw_ref, gf_ref, gb_ref, gg_ref, gnw_ref,
                  g0_ref, g1_ref, g2_ref, w0_ref, w1_ref, w2_ref, wo_ref, xc_ref, xl_ref, mod_ref, o_ref,
                  ys_ref, yg_ref, *, nctx_blk):
    tm = xc_ref.shape[0]
    rp = 64
    for r in range(tm // rp):
        rs = slice(r * rp, (r + 1) * rp)
        y = (sf_ref[rs, :].astype(F32) + sb_ref[rs, :].astype(F32)
             + sx_ref[rs, :].astype(F32) * dx_ref[...])
        y = y * _silu(sz_ref[rs, :].astype(F32))
        parts = []
        for g in range(SSM_GROUPS):
            yg = y[:, g * SSM_GW:(g + 1) * SSM_GW]
            parts.append(yg * lax.rsqrt(jnp.mean(yg * yg, axis=-1, keepdims=True) + EPS))
        ys_ref[rs, :] = (jnp.concatenate(parts, axis=1) * snw_ref[...]).astype(BF16)
        o = gf_ref[rs, :].astype(F32) + gb_ref[rs, :].astype(F32)
        parts = []
        for h in range(GDN_HEADS):
            oh = o[:, h * 128:(h + 1) * 128]
            parts.append(oh * lax.rsqrt(jnp.mean(oh * oh, axis=-1, keepdims=True) + EPS))
        yg_ref[rs, :] = (jnp.concatenate(parts, axis=1) * gnw_ref[...]
                         * _silu(gg_ref[rs, :].astype(F32))).astype(BF16)
    is_ctx = pl.program_id(0) < nctx_blk
    yh = jnp.where(is_ctx, yhc_ref[...], yhl_ref[...])
    m = (_sigmoid(g0_ref[...].astype(F32)) * _dot(yh, w0_ref[...])
         + _sigmoid(g1_ref[...].astype(F32)) * _dot(ys_ref[...], w1_ref[...])
         + _sigmoid(g2_ref[...].astype(F32)) * _dot(yg_ref[...], w2_ref[...]))
    x = jnp.where(is_ctx, xc_ref[...], xl_ref[...])
    o_ref[...] = x + mod_ref[0, 2:3, :] * _dot(m.astype(BF16), wo_ref[...])


def merge(rw, l, yh, y_f, y_b, dx, ssm_nw, o_f, o_b, gdn_nw, p, w0, w1, w2, wo, xs, mod, skip_ctx=False):
    R = rw.R
    D = xs[0].shape[1]
    tm = min(rw.tm, 512)
    mi = rw.mod_index(tm)
    skip = rw.NC // tm if skip_ctx else 0
    yspec = pl.BlockSpec((tm, 512), lambda i: (i + skip, 0))
    pspec = lambda col: pl.BlockSpec((tm, 512), lambda i: (i + skip, col // 512))
    vec = pl.BlockSpec((1, 512), lambda i: (0, 0))
    gspec = lambda k: pl.BlockSpec((tm, D), lambda i: (i + skip, C_GATE // D + k))
    wspec = pl.BlockSpec((None, 512, D), lambda i: (l, 0, 0))
    return pl.pallas_call(
        functools.partial(_merge_kernel, nctx_blk=rw.NC // tm - skip),
        grid=(R // tm - skip,),
        in_specs=_stream_specs(rw, tm, yh, 1, skip) + [
                  yspec, yspec, pspec(C_XBC), pspec(C_Z), vec, vec,
                  yspec, yspec, pspec(C_GG), vec,
                  gspec(0), gspec(1), gspec(2), wspec, wspec, wspec,
                  pl.BlockSpec((None, D, D), lambda i: (l, 0, 0))]
                 + _stream_specs(rw, tm, xs, 1, skip)
                 + [pl.BlockSpec((None, 1, 8, D), lambda i: (l, mi(i + skip), 0, 0))],
        out_specs=pl.BlockSpec((tm, D), lambda i: (i + skip, 0)),
        out_shape=jax.ShapeDtypeStruct((R, D), F32),
        scratch_shapes=[pltpu.VMEM((tm, 512), BF16), pltpu.VMEM((tm, 512), BF16)],
        compiler_params=_cp("arbitrary"),
        name="merge",
    )(yh[0], yh[1], y_f, y_b, p, p, dx, ssm_nw, o_f, o_b, p, gdn_nw, p, p, p, w0, w1, w2, wo, xs[0], xs[1], mod)


def _swiglu_up_kernel(x_ref, nw_ref, mod_ref, wg_ref, wu_ref, o_ref, h_ref, g0_ref, g1_ref, u0_ref, u1_ref):
    @pl.when(pl.program_id(1) == 0)
    def _():
        h = _norm_mod(x_ref[...], nw_ref[...], mod_ref[0, 4:5, :], mod_ref[0, 3:4, :])
        h_ref[...] = h.astype(BF16)

    T, tn = o_ref.shape
    rows = g0_ref.shape[0]
    gs, us = (g0_ref, g1_ref), (u0_ref, u1_ref)

    def project(r):
        hh = h_ref[r * rows:(r + 1) * rows, :]
        gs[r % 2][...] = _dot(hh, wg_ref[...])
        us[r % 2][...] = _dot(hh, wu_ref[...])

    def finish(r):
        for q in range(rows // 64):
            for c in range(tn // 128):
                ps = (slice(q * 64, (q + 1) * 64), slice(c * 128, (c + 1) * 128))
                y = _silu(gs[r % 2][ps]) * us[r % 2][ps]
                o_ref[r * rows + q * 64:r * rows + (q + 1) * 64, ps[1]] = y.astype(o_ref.dtype)

    for r in range(T // rows):
        project(r)
        if r > 0:
            finish(r - 1)
    finish(T // rows - 1)


def swiglu_up(rw, l, x, nw, mod, wgu, skip_ctx=False):
    R, D = x.shape
    tm = min(rw.tm, 512)
    tn = D_FF
    nj = D_FF // tn
    mi = rw.mod_index(tm)
    skip = rw.NC // tm if skip_ctx else 0
    return pl.pallas_call(
        _swiglu_up_kernel,
        grid=(R // tm - skip, nj),
        in_specs=[
            pl.BlockSpec((tm, D), lambda i, j: (i + skip, 0)),
            pl.BlockSpec((None, 1, D), lambda i, j: (l, 0, 0)),
            pl.BlockSpec((None, 1, 8, D), lambda i, j: (l, mi(i + skip), 0, 0)),
            pl.BlockSpec((None, D, tn), lambda i, j: (l, 0, j)),
            pl.BlockSpec((None, D, tn), lambda i, j: (l, 0, nj + j)),
        ],
        out_specs=pl.BlockSpec((tm, tn), lambda i, j: (i + skip, j)),
        out_shape=jax.ShapeDtypeStruct((R, D_FF), BF16),
        scratch_shapes=[pltpu.VMEM((tm, D), BF16)] + [pltpu.VMEM((min(256, tm), tn), F32)] * 4,
        compiler_params=_cp("arbitrary", "arbitrary"),
        name="swiglu_up",
    )(x, nw, mod, wgu, wgu)


def _swiglu_down_kernel(a_ref, w_ref, x_ref, mod_ref, o_ref):
    o_ref[...] = x_ref[...] + mod_ref[0, 5:6, :] * _dot(a_ref[...], w_ref[...])


def swiglu_down(rw, l, a, w, x, mod, skip_ctx=False):
    R, D = x.shape
    tm = min(rw.tm, 512)
    mi = rw.mod_index(tm)
    skip = rw.NC // tm if skip_ctx else 0
    return pl.pallas_call(
        _swiglu_down_kernel,
        grid=(R // tm - skip,),
        in_specs=[
            pl.BlockSpec((tm, D_FF), lambda i: (i + skip, 0)),
            pl.BlockSpec((None, D_FF, D), lambda i: (l, 0, 0)),
            pl.BlockSpec((tm, D), lambda i: (i + skip, 0)),
            pl.BlockSpec((None, 1, 8, D), lambda i: (l, mi(i + skip), 0, 0)),
        ],
        out_specs=pl.BlockSpec((tm, D), lambda i: (i + skip, 0)),
        out_shape=jax.ShapeDtypeStruct((R, D), F32),
        compiler_params=_cp("arbitrary"),
        name="swiglu_down",
    )(a, w, x, mod)


def _final_norm_kernel(x_ref, w_ref, o_ref):
    x = x_ref[...]
    ms = jnp.mean(x * x, axis=-1, keepdims=True)
    o_ref[...] = x * lax.rsqrt(ms + EPS) * w_ref[...]


def final_norm(rw, x, w):
    D = x.shape[1]
    tm = rw.tm
    n0 = rw.NC // tm
    nl = rw.B * rw.Ll
    return pl.pallas_call(
        _final_norm_kernel,
        grid=(nl // tm,),
        in_specs=[pl.BlockSpec((tm, D), lambda i: (n0 + i, 0)), pl.BlockSpec((1, D), lambda i: (0, 0))],
        out_specs=pl.BlockSpec((tm, D), lambda i: (i, 0)),
        out_shape=jax.ShapeDtypeStruct((nl, D), F32),
        compiler_params=_cp("arbitrary"),
        name="final_norm",
    )(x, w.reshape(1, D))


def _regroup_w_in(w_in):
    o_dt = 3072
    o_gdn = 3088
    o_a = o_gdn + 2048
    o_b = o_a + 8
    o_gate = o_gdn + 2064
    wt = jnp.swapaxes(w_in, 1, 2).astype(BF16)
    pieces = [
        wt[:, 0:3072],
        wt[:, o_gdn:o_gdn + 2048],
        wt[:, o_gate:o_gate + 3072],
        wt[:, o_dt:o_dt + 16],
        wt[:, o_a:o_a + 4], wt[:, o_b:o_b + 4],
        wt[:, o_a + 4:o_a + 8], wt[:, o_b + 4:o_b + 8],
        jnp.zeros((wt.shape[0], N_IN_PAD - C_SM - 32, wt.shape[2]), wt.dtype),
    ]
    return jnp.swapaxes(jnp.concatenate(pieces, axis=1), 1, 2)


def kernel(x, c, ctx, c_ctx, w_ada, b_ada, norm1_w, norm2_w, w_in, hy_conv_w, hy_conv_b, hy_w1, hy_b1, hy_w2, hy_b2, hy_w3, hy_freq, hy_bias, ssm_conv_w, ssm_conv_b, ssm_dt_bias, ssm_A_log, ssm_D, ssm_norm_w, gdn_conv_w, gdn_dt_bias, gdn_A_log, gdn_norm_w, w_hy_out, w_ssm_out, w_gdn_out, w_out, w_gate_up, w_down, final_norm_w):
    B, Ll, D = x.shape
    Lc = ctx.shape[1]
    depth = w_ada.shape[0]
    assert Lc == CONV_ROWS and D == D_MODEL and B <= 15
    rw = Rows(B, Lc, Ll)
    R, NC = rw.R, rw.NC

    xs = (ctx.reshape(B * Lc, D), x.reshape(B * Ll, D))

    svec = jnp.concatenate([c_ctx[None, :], c, jnp.zeros((15 - B, D), F32)], axis=0)
    mod = ada_modulation(svec, w_ada, b_ada)
    mod = jnp.pad(mod.reshape(depth, 16, 6, D), ((0, 0), (0, 0), (0, 2), (0, 0)))

    w_in_r = _regroup_w_in(w_in)
    par = _in_proj_params(hy_conv_w, hy_conv_b, ssm_conv_w, ssm_conv_b, gdn_conv_w, ssm_dt_bias, gdn_dt_bias,
                          gdn_A_log)
    norm1 = norm1_w.reshape(depth, 1, D)
    norm2 = norm2_w.reshape(depth, 1, D)
    w_hy_o, w_ssm_o, w_gdn_o, w_o = (w.astype(BF16) for w in (w_hy_out, w_ssm_out, w_gdn_out, w_out))
    w_gu, w_dn = w_gate_up.astype(BF16), w_down.astype(BF16)
    dft_l = (dft_table(Ll),) + dft_tables_split(Ll)
    dft_c = (dft_table(Lc),) + dft_tables_split(Lc)
    feat_l, feat_c = hy_features(Ll), hy_features(Lc)

    for l in range(depth):
        p, sm = in_proj(rw, xs, l, norm1, mod, w_in_r, par)

        sm32_t = sm[:, :32].T
        dt_t = sm32_t[:16].reshape(2, 8, R)
        g_t = sm32_t[16:32].reshape(16, R // GDN_CHUNK, GDN_CHUNK).transpose(1, 0, 2)

        alx = jnp.repeat(ssm_A_log[l], SSM_HEAD_DIM, axis=-1).reshape(2, 1, 512)
        alc = ssm_A_log[l].reshape(2, 8, 1)
        y_f, y_b = ssd_scan(rw, p, sm, dt_t, alx, alc)
        dx = jnp.repeat(ssm_D[l], SSM_HEAD_DIM).reshape(1, 512)

        o_f, o_b = gdn_scan(rw, *gdn_prep(p, sm, g_t))

        last = l == depth - 1
        hyu = p
        parts = []
        for (Bn, L, blk0, (full, fwd, inv), feat) in ((B, Lc, 0, dft_c, feat_c),
                                                      (B, Ll, NC // Ll, dft_l, feat_l)):
            if NC % L:
                raise ValueError("latent length must divide the context row count")
            if last and blk0 == 0:
                parts.append(None)
                continue
            filt = hy_filter(feat, hy_w1[l], hy_b1[l], hy_w2[l], hy_b2[l], hy_w3[l], hy_freq[l])
            kspec = matmul(full, filt, min(512, 2 * L), 512)
            z1 = long_conv(Bn, L, hyu, blk0, 0, hyu, blk0, 1, hy_bias[l, 0], fwd, inv, kspec, 0, F32)
            yy = long_conv(Bn, L, z1, 0, 0, hyu, blk0, 2, hy_bias[l, 1], fwd, inv, kspec, 1, BF16)
            parts.append(yy)
        y_hy = (parts[1] if last else parts[0], parts[1])

        xa = merge(rw, l, y_hy, y_f, y_b, dx, ssm_norm_w[l].reshape(1, 512),
                   o_f, o_b, jnp.tile(gdn_norm_w[l], GDN_HEADS).reshape(1, 512), p,
                   w_hy_o, w_ssm_o, w_gdn_o, w_o, xs, mod, skip_ctx=last)
        act = swiglu_up(rw, l, xa, norm2, mod, w_gu, skip_ctx=last)
        xa = swiglu_down(rw, l, act, w_dn, xa, mod, skip_ctx=last)
        xs = (xa, xa)

    out = final_norm(rw, xa, final_norm_w)
    return out.reshape(B, Ll, D)
```

```python
import functools
import math

import jax
import jax.numpy as jnp
import numpy as np
from jax import lax
from jax.experimental import pallas as pl
from jax.experimental.pallas import tpu as pltpu

F32 = jnp.float32
BF16 = jnp.bfloat16
HI = lax.Precision.HIGHEST

EPS = 1e-6
D_MODEL = 1024
GRID_W = 64

HY_WIDTH = 512
HY_BANDS = 16
HY_EMB = 1 + 2 * HY_BANDS
HY_HIDDEN = 64
HY_SHORT_DECAY_PCT = 0.3
HY_LONG_DECAY_PCT = 1.5
HY_TARGET = 1e-2

SSM_HEADS = 8
SSM_HEAD_DIM = 64
SSM_WIDTH = 512
SSM_GROUPS = 2
SSM_HPG = 4
SSM_STATE = 128
SSM_CHUNK = 128
SSM_GW = SSM_HPG * SSM_HEAD_DIM

GDN_HEADS = 4
GDN_DK = 128
GDN_DV = 128
GDN_CHUNK = 64

D_FF = 2816

C_HY = 0
C_Z = 1536
C_XBC = 2048
C_QKV = 3072
C_GG = 4608
C_GATE = 5120
C_SM = 8192

CONV_ROWS = 256
FREQ_BLK = 256

VMEM_LIMIT = 56 * 1024 * 1024


def _cp(*sem, flags=None):
    return pltpu.CompilerParams(dimension_semantics=sem, vmem_limit_bytes=VMEM_LIMIT, flags=flags)


def _sigmoid(x):
    return 1.0 / (1.0 + jnp.exp(-x))


def _silu(x):
    return x * _sigmoid(x)


def _softplus(x):
    return jnp.maximum(x, 0.0) + jnp.log1p(jnp.exp(-jnp.abs(x)))


def _dot(a, b, precision=None):
    return jnp.dot(a, b, precision=precision, preferred_element_type=F32)


def _dot_nt(a, b):
    return lax.dot_general(a, b, (((1,), (1,)), ((), ())), preferred_element_type=F32)


def _dot_tn(a, b):
    return lax.dot_general(a, b, (((0,), (0,)), ((), ())), preferred_element_type=F32)


def _ada_kernel(s_ref, w_ref, b_ref, o_ref):
    s = _silu(s_ref[...])
    o_ref[0] = _dot(s, w_ref[0], HI) + b_ref[0]


def ada_modulation(svec, w_ada, b_ada):
    depth = w_ada.shape[0]
    D = D_MODEL
    return pl.pallas_call(
        _ada_kernel,
        grid=(depth, 6),
        in_specs=[
            pl.BlockSpec((16, D), lambda l, j: (0, 0)),
            pl.BlockSpec((1, D, D), lambda l, j: (l, 0, j)),
            pl.BlockSpec((1, 1, D), lambda l, j: (l, 0, j)),
        ],
        out_specs=pl.BlockSpec((1, 16, D), lambda l, j: (l, 0, j)),
        out_shape=jax.ShapeDtypeStruct((depth, 16, 6 * D), F32),
        compiler_params=_cp("arbitrary", "arbitrary"),
        name="ada",
    )(svec, w_ada, b_ada.reshape(depth, 1, 6 * D))


def _norm_mod(x, nw, scale, shift):
    ms = jnp.mean(x * x, axis=-1, keepdims=True)
    return (x * lax.rsqrt(ms + EPS) * nw) * (1.0 + scale) + shift


IN_FLIGHT = 4
N_IN_PAD = C_SM + 128
IN_TN = N_IN_PAD // 5
MODE_RAW, MODE_CONV, MODE_CONV_SILU, MODE_CONV_SILU_L2, MODE_SMALL = range(5)


def _tile_mode(tile):
    col = tile * 128
    if col < C_Z:
        return MODE_CONV
    if col < C_XBC:
        return MODE_RAW
    if col < C_QKV:
        return MODE_CONV_SILU
    if col < C_QKV + 1024:
        return MODE_CONV_SILU_L2
    if col < C_GG:
        return MODE_CONV_SILU
    if col < C_SM:
        return MODE_RAW
    return MODE_SMALL
PAR_W0, PAR_W1, PAR_W2, PAR_BIAS, PAR_L2SCALE, PAR_SBIAS, PAR_SALOG, PAR_SKIND = range(8)


def _in_proj_kernel(xc_ref, xl_ref, nw_ref, mod_ref, w_ref, par_ref, o_ref, sm_ref, h_ref, raw0_ref, raw1_ref, *,
                    nctx_blk):
    j = pl.program_id(1)
    nj = N_IN_PAD // IN_TN
    raws = (raw0_ref, raw1_ref)

    @pl.when((j == 0) & (pl.program_id(0) < nctx_blk))
    def _():
        h = _norm_mod(xc_ref[...], nw_ref[...], mod_ref[0, 1:2, :], mod_ref[0, 0:1, :])
        h_ref[...] = h.astype(BF16)

    @pl.when((j == 0) & (pl.program_id(0) >= nctx_blk))
    def _():
        h = _norm_mod(xl_ref[...], nw_ref[...], mod_ref[0, 1:2, :], mod_ref[0, 0:1, :])
        h_ref[...] = h.astype(BF16)

    T = h_ref.shape[0]
    G = GRID_W
    per_ctx = CONV_ROWS // G
    is_latent = pl.program_id(0) >= nctx_blk
    sub = lax.broadcasted_iota(jnp.int32, (8, 128), 0)

    def raw_piece(src, g, c):
        return src[g * G:(g + 1) * G, c * 128:(c + 1) * 128]

    retired = []

    def retire(y):
        bits = pltpu.bitcast(y[0:8], jnp.int32)
        zero = lax.shift_right_logical(lax.shift_right_logical(bits, 16), 16)
        retired.append(jnp.tile(zero.astype(F32), (G // 8, 1)))

    def conv(src, g, c):
        cs = slice(c * 128, (c + 1) * 128)
        x = raw_piece(src, g, c)
        if len(retired) >= IN_FLIGHT:
            x = x + retired[-IN_FLIGHT]
        zero = jnp.zeros((1, 128), F32)
        before = zero if g % per_ctx == 0 else jnp.where(is_latent, 0.0, src[g * G - 1:g * G, cs])
        after = zero if g % per_ctx == per_ctx - 1 else jnp.where(is_latent, 0.0, src[(g + 1) * G:(g + 1) * G + 1, cs])
        rp = pltpu.roll(x, 1, 0)
        rn = pltpu.roll(x, G - 1, 0)
        prev = jnp.concatenate([jnp.where(sub == 0, before, rp[0:8]), rp[8:]], axis=0)
        nxt = jnp.concatenate([rn[:G - 8], jnp.where(sub == 7, after, rn[G - 8:])], axis=0)
        return (prev * par_ref[PAR_W0:PAR_W0 + 1, cs] + x * par_ref[PAR_W1:PAR_W1 + 1, cs]
                + nxt * par_ref[PAR_W2:PAR_W2 + 1, cs] + par_ref[PAR_BIAS:PAR_BIAS + 1, cs])

    def conv_silu(src, g, c):
        return _silu(conv(src, g, c))

    def conv_silu_l2(src, g, c):
        y = _silu(conv(src, g, c))
        y = y * lax.rsqrt(jnp.sum(y * y, axis=-1, keepdims=True) + EPS)
        return y * par_ref[PAR_L2SCALE:PAR_L2SCALE + 1, c * 128:(c + 1) * 128]

    def small(src, g, c):
        cs = slice(c * 128, (c + 1) * 128)
        acc = raw_piece(src, g, c)
        kind = par_ref[PAR_SKIND:PAR_SKIND + 1, cs]
        sp = _softplus(acc + par_ref[PAR_SBIAS:PAR_SBIAS + 1, cs])
        dec = -jnp.exp(par_ref[PAR_SALOG:PAR_SALOG + 1, cs]) * sp
        return jnp.where(kind == 0.0, sp, jnp.where(kind == 1.0, dec, jnp.where(kind == 2.0, _sigmoid(acc), 0.0)))

    rows_mm = 256
    tiles = IN_TN // 128
    piece_fn = {MODE_RAW: raw_piece, MODE_CONV: conv, MODE_CONV_SILU: conv_silu,
                MODE_CONV_SILU_L2: conv_silu_l2, MODE_SMALL: small}

    heavy_modes = (MODE_CONV_SILU, MODE_CONV_SILU_L2)
    col_slices = [slice(c0, min(c0 + 256, IN_TN)) for c0 in range(0, IN_TN, 256)]

    def project(dst, r, cs=slice(None)):
        rs = slice(r * rows_mm, (r + 1) * rows_mm)
        dst[rs, cs] = _dot(h_ref[rs, :], w_ref[:, cs])

    def finish(src, blk, g, c):
        mode = _tile_mode(blk * tiles + c)
        y = piece_fn[mode](src, g, c)
        if mode in heavy_modes:
            retire(y)
        if mode == MODE_SMALL:
            sm_ref[g * G:(g + 1) * G, :] = y
            y = jnp.zeros_like(y)
        o_ref[g * G:(g + 1) * G, c * 128:(c + 1) * 128] = y.astype(o_ref.dtype)

    for step in range(nj + 1):
        @pl.when(j == step)
        def _(step=step):
            blk = step - 1
            src, dst = raws[blk % 2], raws[step % 2]
            retired.clear()
            light = blk >= 0 and not any(_tile_mode(blk * tiles + c) in heavy_modes for c in range(tiles))
            for r in range(T // rows_mm):
                pieces = [] if blk < 0 else [(g, c) for g in range(r * rows_mm // G, (r + 1) * rows_mm // G)
                                             for c in range(tiles)]
                if step == nj:
                    slabs = []
                elif light:
                    slabs = col_slices
                else:
                    slabs = [slice(None)]
                per = -(-len(pieces) // max(len(slabs), 1))
                for n in range(max(len(slabs), 1)):
                    if n < len(slabs):
                        project(dst, r, slabs[n])
                    for g, c in pieces[n * per:(n + 1) * per]:
                        finish(src, blk, g, c)


class Rows:
    def __init__(self, B, Lc, Ll):
        self.B, self.Lc, self.Ll = B, Lc, Ll
        self.NC = B * Lc
        self.R = B * Lc + B * Ll
        assert self.NC % Ll == 0 or Ll % self.NC == 0
        tm = 1024
        while self.NC % tm or Ll % tm:
            tm //= 2
        self.tm = tm

    def mod_index(self, tm):
        nctx = self.NC // tm
        per = self.Ll // tm
        return lambda i: jnp.where(i < nctx, 0, 1 + (i - nctx) // per)


def _stream_specs(rw, tm, xs, ngrid, skip=0):
    xc, xl = xs
    nctx = rw.NC // tm
    off = nctx if xl.shape[0] == rw.R else 0
    D = xc.shape[1]
    if ngrid == 1:
        return [pl.BlockSpec((tm, D), lambda i: (jnp.minimum(i + skip, nctx - 1), 0)),
                pl.BlockSpec((tm, D), lambda i: (jnp.maximum(i + skip - nctx, 0) + off, 0))]
    return [pl.BlockSpec((tm, D), lambda i, j: (jnp.minimum(i + skip, nctx - 1), 0)),
            pl.BlockSpec((tm, D), lambda i, j: (jnp.maximum(i + skip - nctx, 0) + off, 0))]


def in_proj(rw, xs, l, nw, mod, w, par):
    R = rw.R
    D = xs[0].shape[1]
    N = w.shape[2]
    tm, tn = rw.tm, IN_TN
    nj = N // tn
    assert N == N_IN_PAD
    mi = rw.mod_index(tm)
    done = lambda j: jnp.maximum(j - 1, 0)
    return pl.pallas_call(
        functools.partial(_in_proj_kernel, nctx_blk=rw.NC // tm),
        grid=(R // tm, nj + 1),
        in_specs=_stream_specs(rw, tm, xs, 2) + [
            pl.BlockSpec((None, 1, D), lambda i, j: (l, 0, 0)),
            pl.BlockSpec((None, 1, 8, D), lambda i, j: (l, mi(i), 0, 0)),
            pl.BlockSpec((None, D, tn), lambda i, j: (l, 0, jnp.minimum(j, nj - 1))),
            pl.BlockSpec((None, 8, tn), lambda i, j: (l, 0, done(j))),
        ],
        out_specs=[pl.BlockSpec((tm, tn), lambda i, j: (i, done(j))),
                   pl.BlockSpec((tm, 128), lambda i, j: (i, 0))],
        out_shape=[jax.ShapeDtypeStruct((R, N), BF16), jax.ShapeDtypeStruct((R, 128), F32)],
        scratch_shapes=[pltpu.VMEM((tm, D), BF16), pltpu.VMEM((tm, tn), F32), pltpu.VMEM((tm, tn), F32)],
        compiler_params=_cp("arbitrary", "arbitrary"),
        name="in_proj",
    )(xs[0], xs[1], nw, mod, w, par)


def _in_proj_params(hy_conv_w, hy_conv_b, ssm_conv_w, ssm_conv_b, gdn_conv_w, ssm_dt_bias, gdn_dt_bias, gdn_A_log):
    depth = hy_conv_w.shape[0]

    def row(pieces):
        out, pos = [], 0
        for off, a in pieces:
            out += [jnp.zeros((depth, off - pos), F32), a.astype(F32)]
            pos = off + a.shape[1]
        return jnp.concatenate(out + [jnp.zeros((depth, N_IN_PAD - pos), F32)], axis=1)
    z4 = jnp.zeros((depth, 4), F32)
    conv = [row([(C_HY, hy_conv_w[:, t]), (C_XBC, ssm_conv_w[:, t]), (C_QKV, gdn_conv_w[:, t])]) for t in range(3)]
    bias = row([(C_HY, hy_conv_b), (C_XBC, ssm_conv_b)])
    l2s = row([(C_QKV, jnp.full((depth, 512), GDN_DK ** -0.5, F32)), (C_QKV + 512, jnp.ones((depth, 512), F32))])
    sbias = row([(C_SM, jnp.concatenate([ssm_dt_bias.reshape(depth, 16), gdn_dt_bias[:, 0], z4,
                                         gdn_dt_bias[:, 1], z4], axis=1))])
    salog = row([(C_SM + 16, jnp.concatenate([gdn_A_log[:, 0], z4, gdn_A_log[:, 1], z4], axis=1))])
    kind = np.full((depth, N_IN_PAD), 3.0, np.float32)
    kind[:, C_SM:C_SM + 16] = 0.0
    kind[:, C_SM + 16:C_SM + 20] = 1.0
    kind[:, C_SM + 24:C_SM + 28] = 1.0
    kind[:, C_SM + 20:C_SM + 24] = 2.0
    kind[:, C_SM + 28:C_SM + 32] = 2.0
    return jnp.stack(conv + [bias, l2s, sbias, salog, jnp.asarray(kind)], axis=1)


def _hy_filter_kernel(z_ref, w1_ref, b1_ref, w2_ref, b2_ref, w3_ref, f0_ref, f1_ref, win_ref, oe_ref, oo_ref,
                      h_ref, split_ref):
    @pl.when(pl.program_id(1) == 0)
    def _():
        h1 = jnp.sin(f0_ref[...] * (_dot(z_ref[...], w1_ref[...], HI) + b1_ref[...]))
        h_ref[...] = jnp.sin(f1_ref[...] * (_dot(h1, w2_ref[...], HI) + b2_ref[...]))

    a1 = h_ref[...].astype(BF16)
    a2 = (h_ref[...] - a1.astype(F32)).astype(BF16)
    b1 = w3_ref[...].astype(BF16)
    b2 = (w3_ref[...] - b1.astype(F32)).astype(BF16)
    h = (_dot(a1, b1) + _dot(a1, b2) + _dot(a2, b1)) * win_ref[...]
    tl = h.shape[0]
    row = lax.broadcasted_iota(jnp.int32, (tl, 1), 0) + pl.program_id(0) * tl
    drop = (row == 0) & (pl.program_id(1) % 2 == 1)
    h = jnp.where(drop, 0.0, h)
    for c in range(h.shape[1] // 128):
        cs = slice(c * 128, (c + 1) * 128)
        s_c = split_ref.at[c]
        s_c[...] = h[:, cs]
        oe_ref[:, cs] = s_c[pl.ds(0, tl // 2, stride=2), :].astype(oe_ref.dtype)
        oo_ref[:, cs] = s_c[pl.ds(1, tl // 2, stride=2), :].astype(oo_ref.dtype)


def hy_features(L):
    t = jnp.linspace(0.0, 1.0, L, dtype=F32)[:, None]
    w = 2.0 * math.pi * jnp.arange(L, dtype=F32)[:, None] / L
    f = jnp.linspace(1e-4, HY_BANDS - 1, HY_BANDS, dtype=F32)[None, :]
    z = jnp.concatenate([t, jnp.cos(f * w), -jnp.sin(f * w)], axis=-1)
    z = jnp.pad(z, ((0, 0), (0, 128 - HY_EMB)))
    min_decay = math.log(HY_TARGET) / HY_LONG_DECAY_PCT
    max_decay = math.log(HY_TARGET) / HY_SHORT_DECAY_PCT
    deltas = jnp.linspace(min_decay, max_decay, HY_WIDTH, dtype=F32)
    window = jnp.exp(-t * jnp.abs(deltas))
    return z, window


def hy_filter(feat, w1, b1, w2, b2, w3, freq):
    z, window = feat
    L = z.shape[0]
    H = HY_HIDDEN
    w1p = jnp.pad(w1, ((0, 128 - HY_EMB), (0, 128 - H)))
    w2p = jnp.pad(w2, ((0, 128 - H), (0, 128 - H)))
    w3p = jnp.pad(w3, ((0, 128 - H), (0, 0)))
    pad1 = lambda v: jnp.pad(v, (0, 128 - H)).reshape(1, 128)
    tl = 256
    full = lambda shape: pl.BlockSpec(shape, lambda i, j: (0, 0))
    return pl.pallas_call(
        _hy_filter_kernel,
        grid=(L // tl, 4),
        in_specs=[
            pl.BlockSpec((tl, 128), lambda i, j: (i, 0)),
            full((128, 128)), full((1, 128)), full((128, 128)), full((1, 128)),
            pl.BlockSpec((128, HY_WIDTH), lambda i, j: (0, j)),
            full((1, 128)), full((1, 128)),
            pl.BlockSpec((tl, HY_WIDTH), lambda i, j: (i, 0)),
        ],
        out_specs=[pl.BlockSpec((tl // 2, HY_WIDTH), lambda i, j: (i, j))] * 2,
        out_shape=[jax.ShapeDtypeStruct((L // 2, 4 * HY_WIDTH), BF16)] * 2,
        scratch_shapes=[pltpu.VMEM((tl, 128), F32), pltpu.VMEM((HY_WIDTH // 128, tl, 128), F32)],
        compiler_params=_cp("arbitrary", "arbitrary"),
        name="hy_filter",
    )(z, w1p, pad1(b1), w2p, pad1(b2), w3p, pad1(freq[0]), pad1(freq[1]), window)


def dft_tables_split(L):
    N = 2 * L
    H = L // 2
    q = np.arange(H, dtype=np.int64)[:, None]
    m = np.arange(H, dtype=np.int64)[None, :]
    ang_e = ((q * 2 * m) % N).astype(np.float64) * (2.0 * math.pi / N)
    ang_o = ((q * (2 * m + 1)) % N).astype(np.float64) * (2.0 * math.pi / N)
    alt = (1 - 2 * (m % 2)).astype(np.float64)
    ce, co = np.cos(ang_e), np.cos(ang_o)
    se = np.where(q == 0, alt, -np.sin(ang_e))
    so = np.where(q == 0, -alt, -np.sin(ang_o))
    w = np.where(q == 0, 1.0, 2.0) / N
    ise = np.where(q == 0, 2.0 / N * alt, -np.sin(ang_e) * w)
    iso = np.where(q == 0, -2.0 / N * alt, -np.sin(ang_o) * w)
    fwd = np.stack([ce, co, se, so])
    inv = np.stack([(ce * w).T, ise.T, (co * w).T, iso.T])
    return jnp.asarray(fwd, dtype=BF16), jnp.asarray(inv, dtype=BF16)


def _filter_spectrum_kernel(fwd_ref, he_ref, ho_ref, rlo_ref, rhi_ref, ilo_ref, ihi_ref):
    he, ho = he_ref[...], ho_ref[...]
    ae, ao = _dot(fwd_ref[0], he), _dot(fwd_ref[1], ho)
    be, bo = _dot(fwd_ref[2], he), _dot(fwd_ref[3], ho)
    rlo_ref[...] = ae + ao
    rhi_ref[...] = ae - ao
    first = (lax.broadcasted_iota(jnp.int32, (fwd_ref.shape[1], 1), 0) == 0) & (pl.program_id(0) == 0)
    ilo_ref[...] = jnp.where(first, be, be + bo)
    ihi_ref[...] = jnp.where(first, bo, bo - be)


def filter_spectrum(fwd, taps_even, taps_odd):
    H, N = taps_even.shape
    C = HY_WIDTH
    FB = min(FREQ_BLK, H)
    tap = pl.BlockSpec((H, C), lambda f, j: (0, j))
    out = pl.BlockSpec((FB, C), lambda f, j: (f, j))
    return pl.pallas_call(
        _filter_spectrum_kernel,
        grid=(H // FB, N // C),
        in_specs=[pl.BlockSpec((4, FB, H), lambda f, j: (0, f, 0)), tap, tap],
        out_specs=[out] * 4,
        out_shape=[jax.ShapeDtypeStruct((H, N), F32)] * 4,
        compiler_params=_cp("arbitrary", "arbitrary"),
        name="filter_spectrum",
    )(fwd, taps_even, taps_odd)


def _long_conv_kernel(u_ref, g_ref, bias_ref, fwd_ref, inv_ref, ar0_ref, ar1_ref, ar0h_ref, ar1h_ref,
                      ai0_ref, ai1_ref, ai0h_ref, ai1h_ref, o_ref, ue_ref, uo_ref, acce_ref, acco_ref, y_ref):
    f = pl.program_id(1)
    half = ue_ref.shape[0]

    lane_tiles = [slice(c * 128, (c + 1) * 128) for c in range(y_ref.shape[0])]

    @pl.when(f == 0)
    def _():
        for c, cs in enumerate(lane_tiles):
            y_c = y_ref.at[c]
            y_c[...] = u_ref[:, cs].astype(F32)
            ue_ref[:, cs] = y_c[pl.ds(0, half, stride=2), :].astype(BF16)
            uo_ref[:, cs] = y_c[pl.ds(1, half, stride=2), :].astype(BF16)
        acce_ref[...] = jnp.zeros_like(acce_ref)
        acco_ref[...] = jnp.zeros_like(acco_ref)

    ue, uo = ue_ref[...], uo_ref[...]
    ae, ao = _dot(fwd_ref[0], ue), _dot(fwd_ref[1], uo)
    be, bo = _dot(fwd_ref[2], ue), _dot(fwd_ref[3], uo)
    ur, ur2 = ae + ao, ae - ao
    ui, ui2 = be + bo, bo - be
    first = (lax.broadcasted_iota(jnp.int32, (fwd_ref.shape[1], 1), 0) == 0) & (f == 0)
    kr, kr2 = ar0_ref[...] + ar1_ref[...], ar0h_ref[...] + ar1h_ref[...]
    ki = jnp.where(first, ai0_ref[...] + ai1_ref[...], ai0_ref[...] - ai1_ref[...])
    ki2 = ai0h_ref[...] - ai1h_ref[...]
    pr, pi = ur * kr - ui * ki, ur * ki + ui * kr
    pr2, pi2 = ur2 * kr2 - ui2 * ki2, ur2 * ki2 + ui2 * kr2
    dc, ny = ur * kr, ur2 * kr2
    gr = jnp.where(first, dc + ny, pr + pr2)
    gi = jnp.where(first, be * ki - bo * ki2, pi - pi2)
    hr = jnp.where(first, dc - ny, pr - pr2)
    hi = jnp.where(first, be * ki2 + bo * ki, pi + pi2)
    acce_ref[...] += _dot(inv_ref[0], gr.astype(BF16)) + _dot(inv_ref[1], gi.astype(BF16))
    acco_ref[...] += _dot(inv_ref[2], hr.astype(BF16)) + _dot(inv_ref[3], hi.astype(BF16))

    @pl.when(f == pl.num_programs(1) - 1)
    def _():
        for c, cs in enumerate(lane_tiles):
            y_c = y_ref.at[c]
            y_c[pl.ds(0, half, stride=2), :] = acce_ref[:, cs]
            y_c[pl.ds(1, half, stride=2), :] = acco_ref[:, cs]
            u = u_ref[:, cs].astype(F32)
            o_ref[:, cs] = (g_ref[:, cs].astype(F32) * (y_c[...] + u * bias_ref[:, cs])).astype(o_ref.dtype)


def long_conv(B, L, u, u_rb0, u_cb, gate, g_rb0, gate_cb, bias, fwd, inv, kspec, order, out_dtype):
    C = HY_WIDTH
    H = L // 2
    FB = min(FREQ_BLK, H)
    nfb = H // FB
    kblk = lambda part, d: pl.BlockSpec((FB, C), lambda b, f: (f, 2 * order + d))
    kops = [kspec[part] for part in range(4) for _ in range(2)]
    return pl.pallas_call(
        _long_conv_kernel,
        grid=(B, nfb),
        in_specs=[
            pl.BlockSpec((L, C), lambda b, f: (u_rb0 + b, u_cb)),
            pl.BlockSpec((L, C), lambda b, f: (g_rb0 + b, gate_cb)),
            pl.BlockSpec((1, C), lambda b, f: (0, 0)),
            pl.BlockSpec((4, FB, H), lambda b, f: (0, f, 0)),
            pl.BlockSpec((4, H, FB), lambda b, f: (0, 0, f)),
            kblk(0, 0), kblk(0, 1), kblk(1, 0), kblk(1, 1), kblk(2, 0), kblk(2, 1), kblk(3, 0), kblk(3, 1),
        ],
        out_specs=pl.BlockSpec((L, C), lambda b, f: (b, 0)),
        out_shape=jax.ShapeDtypeStruct((B * L, C), out_dtype),
        scratch_shapes=[pltpu.VMEM((H, C), BF16), pltpu.VMEM((H, C), BF16),
                        pltpu.VMEM((H, C), F32), pltpu.VMEM((H, C), F32), pltpu.VMEM((C // 128, L, 128), F32)],
        compiler_params=_cp("arbitrary", "arbitrary"),
        name="long_conv",
    )(u, gate, bias.reshape(1, C), fwd, inv, *kops)


def _scan_blocks(rw, rows):
    nbc, nbl, base = rw.Lc // rows, rw.Ll // rows, rw.NC // rows

    def make(d):
        def f(b, s):
            jc = s if d == 0 else nbc - 1 - s
            jl = (s - nbc) if d == 0 else nbl - 1 - (s - nbc)
            return jnp.where(s < nbc, b * nbc + jc, base + b * nbl + jl)
        return f

    return [make(0), make(1)], nbc + nbl


def _expand_lanes(x, base, n, width):
    rows = x.shape[0]
    per = 128 // width
    lane = lax.broadcasted_iota(jnp.int32, (rows, 128), 1)
    tiles = []
    for t in range(n // per):
        c0 = base + t * per
        tile = jnp.broadcast_to(x[:, c0:c0 + 1], (rows, 128))
        for i in range(1, per):
            tile = jnp.where(lane >= i * width, jnp.broadcast_to(x[:, c0 + i:c0 + i + 1], (rows, 128)), tile)
        tiles.append(tile)
    return jnp.concatenate(tiles, axis=1)


def _ssd_kernel(xf, bf, cf, smf, dtf, xb, bb, cb_, smb, dtb, alx_ref, alc_ref, of_ref, ob_ref, h_ref):
    Q = SSM_CHUNK
    GW = SSM_GW

    @pl.when(pl.program_id(1) == 0)
    def _():
        h_ref[...] = jnp.zeros_like(h_ref)

    row = lax.broadcasted_iota(jnp.int32, (Q, Q), 0)
    col = lax.broadcasted_iota(jnp.int32, (Q, Q), 1)
    lane_head = lax.broadcasted_iota(jnp.int32, (Q, GW), 1) // SSM_HEAD_DIM
    dirs = ((xf, bf, cf, smf, dtf, of_ref), (xb, bb, cb_, smb, dtb, ob_ref))
    jobs = []
    for d in range(2):
        x_ref, b_ref, c_ref, sm_ref, dt_ref, o_ref = dirs[d]
        keep = (col <= row) if d == 0 else (col >= row)
        tri = keep.astype(BF16)
        tri_t = ((row <= col) if d == 0 else (row >= col)).astype(BF16)
        sm = sm_ref[...]
        a_x = -jnp.exp(alx_ref[d])
        dtx = _expand_lanes(sm, 8 * d, SSM_HEADS, SSM_HEAD_DIM)
        cumx = _expand_lanes(_dot_01_lhs(tri, sm), 8 * d, SSM_HEADS, SSM_HEAD_DIM) * a_x
        cumr = _dot_01_rhs(dt_ref[0], tri_t) * (-jnp.exp(alc_ref[d]))
        last = Q - 1 if d == 0 else 0
        totx = cumx[last:last + 1, :]
        xd = x_ref[...].astype(F32) * dtx
        xdw = xd * jnp.exp(totx - cumx)
        ecum = jnp.exp(cumx)
        for g in range(SSM_GROUPS):
            gs = slice(g * GW, (g + 1) * GW)
            jobs.append(dict(d=d, g=g, gs=gs, keep=keep, cumx=cumx, cumr=cumr, o_ref=o_ref,
                             bg=b_ref[:, g * SSM_STATE:(g + 1) * SSM_STATE].astype(BF16),
                             cg=c_ref[:, g * SSM_STATE:(g + 1) * SSM_STATE].astype(BF16),
                             xdg=xd[:, gs], xdw=xdw[:, gs].astype(BF16), ecum=ecum[:, gs],
                             etot=jnp.exp(totx[:, gs])))
    for j in jobs:
        j["cb"] = _dot_nt(j["cg"], j["bg"])
        j["h"] = h_ref[j["d"], j["g"]]
    for j in jobs:
        ms, xs = [], []
        for e4 in range(SSM_HPG):
            e = j["g"] * SSM_HPG + e4
            diff = j["cumx"][:, e * SSM_HEAD_DIM:e * SSM_HEAD_DIM + 1] - j["cumr"][e:e + 1, :]
            ms.append((j["cb"] * jnp.where(j["keep"], jnp.exp(diff), 0.0)).astype(BF16))
            xs.append(jnp.where(lane_head == e4, j["xdg"], 0.0).astype(BF16))
        yd = _dot(jnp.concatenate(ms, axis=1), jnp.concatenate(xs, axis=0))
        y_off = _dot(j["cg"], j["h"].astype(BF16)) * j["ecum"]
        j["o_ref"][:, j["gs"]] = (yd + y_off).astype(BF16)
    for j in jobs:
        h_ref[j["d"], j["g"]] = j["h"] * j["etot"] + _dot_tn(j["bg"], j["xdw"])


def ssd_scan(rw, p, sm, dtT, alx, alc):
    Q = SSM_CHUNK
    blks, nsteps = _scan_blocks(rw, Q)
    R = p.shape[0]
    in_specs = []
    for d in range(2):
        f = blks[d]
        in_specs += [
            pl.BlockSpec((Q, 512), lambda b, s, f=f: (f(b, s), C_XBC // 512)),
            pl.BlockSpec((Q, 256), lambda b, s, f=f: (f(b, s), C_XBC // 256 + 2)),
            pl.BlockSpec((Q, 256), lambda b, s, f=f: (f(b, s), C_XBC // 256 + 3)),
            pl.BlockSpec((Q, 128), lambda b, s, f=f: (f(b, s), 0)),
            pl.BlockSpec((1, 8, Q), lambda b, s, f=f, d=d: (d, 0, f(b, s))),
        ]
    in_specs += [pl.BlockSpec((2, 1, 512), lambda b, s: (0, 0, 0)), pl.BlockSpec((2, 8, 1), lambda b, s: (0, 0, 0))]
    ops = (p, p, p, sm, dtT)
    return pl.pallas_call(
        _ssd_kernel,
        grid=(rw.B, nsteps),
        in_specs=in_specs,
        out_specs=[pl.BlockSpec((Q, 512), lambda b, s, f=blks[d]: (f(b, s), 0)) for d in range(2)],
        out_shape=[jax.ShapeDtypeStruct((R, 512), BF16)] * 2,
        scratch_shapes=[pltpu.VMEM((2, SSM_GROUPS, SSM_STATE, SSM_GW), F32)],
        compiler_params=_cp("arbitrary", "arbitrary"),
        name="ssd_scan",
    )(*ops, *ops, alx, alc)


def _split3(x):
    x1 = x.astype(BF16)
    r = x - x1.astype(F32)
    x2 = r.astype(BF16)
    x3 = (r - x2.astype(F32)).astype(BF16)
    return x1, x2, x3


def _dot_01_lhs(m01, x):
    x1, x2, x3 = _split3(x)
    return _dot(m01, x1) + _dot(m01, x2) + _dot(m01, x3)


def _dot_01_rhs(x, m01):
    x1, x2, x3 = _split3(x)
    return _dot(x1, m01) + _dot(x2, m01) + _dot(x3, m01)


GDN_ROWS = 256


def _gdn_prep_kernel(q_ref, k_ref, v_ref, sm_ref, gT_ref, u_ref, w_ref, qg_ref, kd_ref, qk_ref, egl_ref):
    C = GDN_CHUNK
    row = lax.broadcasted_iota(jnp.int32, (C, C), 0)
    col = lax.broadcasted_iota(jnp.int32, (C, C), 1)
    jobs = []
    levels = []
    for d in range(2):
        keep = (col <= row) if d == 0 else (col >= row)
        late, early = (row, col) if d == 0 else (col, row)
        levels.append([(((row ^ col) >> (j + 1)) == 0) & ((late & (1 << j)) != 0) & ((early & (1 << j)) == 0)
                       for j in range(6)])
        tri = keep.astype(BF16)
        tri_t = ((row <= col) if d == 0 else (row >= col)).astype(BF16)
        last = C - 1 if d == 0 else 0
        for c in range(GDN_ROWS // C):
            rows = slice(c * C, (c + 1) * C)
            smc = sm_ref[rows, :]
            cums = _dot_01_lhs(tri, smc)
            cumr = _dot_01_rhs(gT_ref[c, 8 * d:8 * d + 8, :], tri_t)
            tot = cums[last:last + 1, :]
            for h in range(GDN_HEADS):
                lg = 16 + 8 * d + h
                jobs.append(dict(d=d, c=c, h=h, rows=rows, hs=slice(h * 128, (h + 1) * 128), keep=keep,
                                 gc=cums[:, lg:lg + 1], beta=smc[:, lg + 4:lg + 5],
                                 gl=tot[:, lg:lg + 1], gr=cumr[h:h + 1, :]))
    for j in jobs:
        q = q_ref[j["rows"], j["hs"]].astype(F32)
        k = k_ref[j["rows"], j["hs"]].astype(F32)
        j["dec"] = jnp.where(j["keep"], jnp.exp(j["gc"] - j["gr"]), 0.0)
        kb = k * j["beta"]
        both = _dot_nt(jnp.concatenate([kb, q], axis=0).astype(BF16), k.astype(BF16))
        j["a"] = both[:C] * j["dec"]
        j["n"] = -jnp.where(levels[j["d"]][0], j["a"], 0.0)
        qk_ref[j["d"], j["c"], j["h"]] = (both[C:] * j["dec"]).astype(BF16)
    for lev in range(1, 6):
        for j in jobs:
            l = jnp.where(levels[j["d"]][lev], j["a"], 0.0)
            j["y"] = l + _dot(l.astype(BF16), j["n"].astype(BF16))
        for j in jobs:
            j["n"] = j["n"] - j["y"] - _dot(j["n"].astype(BF16), j["y"].astype(BF16))
    for j in jobs:
        d, rows, hs, gc, gl, beta = j["d"], j["rows"], j["hs"], j["gc"], j["gl"], j["beta"]
        q = q_ref[rows, hs].astype(F32)
        k = k_ref[rows, hs].astype(F32)
        eg = jnp.exp(gc)
        rhs = jnp.concatenate([v_ref[rows, hs].astype(F32) * beta, k * beta * eg], axis=1)
        sol = rhs + _dot(j["n"].astype(BF16), rhs.astype(BF16))
        u_ref[d, rows, hs] = sol[:, :GDN_DV].astype(BF16)
        w_ref[d, rows, hs] = sol[:, GDN_DV:].astype(BF16)
        qg_ref[d, rows, hs] = (q * eg).astype(BF16)
        kd_ref[d, rows, hs] = (k * jnp.exp(gl - gc)).astype(BF16)
        egl_ref[d, j["c"], :, hs] = jnp.broadcast_to(jnp.exp(gl), (8, 128))


def gdn_prep(p, sm, gT):
    R = p.shape[0]
    T, C = GDN_ROWS, GDN_CHUNK
    nc = T // C
    col = lambda k: pl.BlockSpec((T, 512), lambda i: (i, C_QKV // 512 + k))
    dirrow = pl.BlockSpec((2, T, 512), lambda i: (0, i, 0))
    return pl.pallas_call(
        _gdn_prep_kernel,
        grid=(R // T,),
        in_specs=[col(0), col(1), col(2),
                  pl.BlockSpec((T, 128), lambda i: (i, 0)),
                  pl.BlockSpec((nc, 16, C), lambda i: (i, 0, 0))],
        out_specs=[dirrow, dirrow, dirrow, dirrow,
                   pl.BlockSpec((2, nc, GDN_HEADS, C, C), lambda i: (0, i, 0, 0, 0)),
                   pl.BlockSpec((2, nc, 8, 512), lambda i: (0, i, 0, 0))],
        out_shape=[jax.ShapeDtypeStruct((2, R, 512), BF16),
                   jax.ShapeDtypeStruct((2, R, 512), BF16),
                   jax.ShapeDtypeStruct((2, R, 512), BF16),
                   jax.ShapeDtypeStruct((2, R, 512), BF16),
                   jax.ShapeDtypeStruct((2, R // C, GDN_HEADS, C, C), BF16),
                   jax.ShapeDtypeStruct((2, R // C, 8, 512), F32)],
        compiler_params=_cp("arbitrary"),
        name="gdn_prep",
    )(p, p, p, sm, gT)


def _gdn_scan_kernel(uf, wf, qgf, kdf, qkf, eglf, ub, wb, qgb, kdb, qkb, eglb, of_ref, ob_ref, s_ref):
    C = GDN_CHUNK
    nch = GDN_ROWS // C

    @pl.when(pl.program_id(1) == 0)
    def _():
        s_ref[...] = jnp.zeros_like(s_ref)

    dirs = ((uf, wf, qgf, kdf, qkf, eglf, of_ref), (ub, wb, qgb, kdb, qkb, eglb, ob_ref))
    chains = [(d, h) for d in range(2) for h in range(GDN_HEADS)]
    S = {ch: s_ref[ch[0], ch[1]] for ch in chains}
    for i in range(nch):
        Sb, vnb, rows_of, c_of = {}, {}, {}, {}
        for d, h in chains:
            c_of[d] = i if d == 0 else nch - 1 - i
            rows_of[d] = slice(c_of[d] * C, (c_of[d] + 1) * C)
        for d, h in chains:
            hs = slice(h * 128, (h + 1) * 128)
            Sb[d, h] = S[d, h].astype(BF16)
            v_new = dirs[d][0][0, rows_of[d], hs].astype(F32) - _dot(dirs[d][1][0, rows_of[d], hs], Sb[d, h])
            vnb[d, h] = v_new.astype(BF16)
        for d, h in chains:
            hs = slice(h * 128, (h + 1) * 128)
            u_ref, w_ref, qg_ref, kd_ref, qk_ref, egl_ref, o_ref = dirs[d]
            S[d, h] = S[d, h] * egl_ref[0, c_of[d], 0:1, hs] + _dot_tn(kd_ref[0, rows_of[d], hs], vnb[d, h])
        for d, h in chains:
            hs = slice(h * 128, (h + 1) * 128)
            u_ref, w_ref, qg_ref, kd_ref, qk_ref, egl_ref, o_ref = dirs[d]
            o_ref[rows_of[d], hs] = (_dot(qg_ref[0, rows_of[d], hs], Sb[d, h])
                                     + _dot(qk_ref[0, c_of[d], h], vnb[d, h])).astype(BF16)
    for ch in chains:
        s_ref[ch[0], ch[1]] = S[ch]


def gdn_scan(rw, u, w, qg, kd, qk, egl):
    T, C = GDN_ROWS, GDN_CHUNK
    nc = T // C
    R = u.shape[1]
    nbc, nbl, base = rw.Lc // T, rw.Ll // T, rw.NC // T

    def blk(d):
        def f(b, s):
            jc = s if d == 0 else nbc - 1 - s
            jl = (s - nbc) if d == 0 else nbl - 1 - (s - nbc)
            return jnp.where(s < nbc, b * nbc + jc, base + b * nbl + jl)
        return f

    in_specs = []
    for d in range(2):
        f = blk(d)
        rowspec = pl.BlockSpec((1, T, 512), lambda b, s, f=f, d=d: (d, f(b, s), 0))
        in_specs += [rowspec, rowspec, rowspec, rowspec,
                     pl.BlockSpec((1, nc, GDN_HEADS, C, C), lambda b, s, f=f, d=d: (d, f(b, s), 0, 0, 0)),
                     pl.BlockSpec((1, nc, 8, 512), lambda b, s, f=f, d=d: (d, f(b, s), 0, 0))]
    out_specs = [pl.BlockSpec((T, 512), lambda b, s, f=blk(d): (f(b, s), 0)) for d in range(2)]
    ops = (u, w, qg, kd, qk, egl)
    return pl.pallas_call(
        _gdn_scan_kernel,
        grid=(rw.B, nbc + nbl),
        in_specs=in_specs,
        out_specs=out_specs,
        out_shape=[jax.ShapeDtypeStruct((R, 512), BF16)] * 2,
        scratch_shapes=[pltpu.VMEM((2, GDN_HEADS, GDN_DK, GDN_DV), F32)],
        compiler_params=_cp("arbitrary", "arbitrary"),
        name="gdn_scan",
    )(*ops, *ops)


def _merge_kernel(yhc_ref, yhl_ref, sf_ref, sb_ref, sx_ref, sz_ref, dx_ref, snw_ref, gf_ref, gb_ref, gg_ref, gnw_ref,
                  g0_ref, g1_ref, g2_ref, w0_ref, w1_ref, w2_ref, wo_ref, xc_ref, xl_ref, mod_ref, o_ref,
                  ys_ref, yg_ref, *, nctx_blk):
    tm = xc_ref.shape[0]
    rp = 64
    for r in range(tm // rp):
        rs = slice(r * rp, (r + 1) * rp)
        y = (sf_ref[rs, :].astype(F32) + sb_ref[rs, :].astype(F32)
             + sx_ref[rs, :].astype(F32) * dx_ref[...])
        y = y * _silu(sz_ref[rs, :].astype(F32))
        parts = []
        for g in range(SSM_GROUPS):
            yg = y[:, g * SSM_GW:(g + 1) * SSM_GW]
            parts.append(yg * lax.rsqrt(jnp.mean(yg * yg, axis=-1, keepdims=True) + EPS))
        ys_ref[rs, :] = (jnp.concatenate(parts, axis=1) * snw_ref[...]).astype(BF16)
        o = gf_ref[rs, :].astype(F32) + gb_ref[rs, :].astype(F32)
        parts = []
        for h in range(GDN_HEADS):
            oh = o[:, h * 128:(h + 1) * 128]
            parts.append(oh * lax.rsqrt(jnp.mean(oh * oh, axis=-1, keepdims=True) + EPS))
        yg_ref[rs, :] = (jnp.concatenate(parts, axis=1) * gnw_ref[...]
                         * _silu(gg_ref[rs, :].astype(F32))).astype(BF16)
    is_ctx = pl.program_id(0) < nctx_blk
    yh = jnp.where(is_ctx, yhc_ref[...], yhl_ref[...])
    m = (_sigmoid(g0_ref[...].astype(F32)) * _dot(yh, w0_ref[...])
         + _sigmoid(g1_ref[...].astype(F32)) * _dot(ys_ref[...], w1_ref[...])
         + _sigmoid(g2_ref[...].astype(F32)) * _dot(yg_ref[...], w2_ref[...]))
    x = jnp.where(is_ctx, xc_ref[...], xl_ref[...])
    o_ref[...] = x + mod_ref[0, 2:3, :] * _dot(m.astype(BF16), wo_ref[...])


def merge(rw, l, yh, y_f, y_b, dx, ssm_nw, o_f, o_b, gdn_nw, p, w0, w1, w2, wo, xs, mod, skip_ctx=False):
    R = rw.R
    D = xs[0].shape[1]
    tm = min(rw.tm, 512)
    mi = rw.mod_index(tm)
    skip = rw.NC // tm if skip_ctx else 0
    yspec = pl.BlockSpec((tm, 512), lambda i: (i + skip, 0))
    pspec = lambda col: pl.BlockSpec((tm, 512), lambda i: (i + skip, col // 512))
    vec = pl.BlockSpec((1, 512), lambda i: (0, 0))
    gspec = lambda k: pl.BlockSpec((tm, D), lambda i: (i + skip, C_GATE // D + k))
    wspec = pl.BlockSpec((None, 512, D), lambda i: (l, 0, 0))
    return pl.pallas_call(
        functools.partial(_merge_kernel, nctx_blk=rw.NC // tm - skip),
        grid=(R // tm - skip,),
        in_specs=_stream_specs(rw, tm, yh, 1, skip) + [
                  yspec, yspec, pspec(C_XBC), pspec(C_Z), vec, vec,
                  yspec, yspec, pspec(C_GG), vec,
                  gspec(0), gspec(1), gspec(2), wspec, wspec, wspec,
                  pl.BlockSpec((None, D, D), lambda i: (l, 0, 0))]
                 + _stream_specs(rw, tm, xs, 1, skip)
                 + [pl.BlockSpec((None, 1, 8, D), lambda i: (l, mi(i + skip), 0, 0))],
        out_specs=pl.BlockSpec((tm, D), lambda i: (i + skip, 0)),
        out_shape=jax.ShapeDtypeStruct((R, D), F32),
        scratch_shapes=[pltpu.VMEM((tm, 512), BF16), pltpu.VMEM((tm, 512), BF16)],
        compiler_params=_cp("arbitrary"),
        name="merge",
    )(yh[0], yh[1], y_f, y_b, p, p, dx, ssm_nw, o_f, o_b, p, gdn_nw, p, p, p, w0, w1, w2, wo, xs[0], xs[1], mod)


def _swiglu_up_kernel(x_ref, nw_ref, mod_ref, wg_ref, wu_ref, o_ref, h_ref, g0_ref, g1_ref, u0_ref, u1_ref):
    @pl.when(pl.program_id(1) == 0)
    def _():
        h = _norm_mod(x_ref[...], nw_ref[...], mod_ref[0, 4:5, :], mod_ref[0, 3:4, :])
        h_ref[...] = h.astype(BF16)

    T, tn = o_ref.shape
    rows = g0_ref.shape[0]
    gs, us = (g0_ref, g1_ref), (u0_ref, u1_ref)

    def project(r):
        hh = h_ref[r * rows:(r + 1) * rows, :]
        gs[r % 2][...] = _dot(hh, wg_ref[...])
        us[r % 2][...] = _dot(hh, wu_ref[...])

    def finish(r):
        for q in range(rows // 64):
            for c in range(tn // 128):
                ps = (slice(q * 64, (q + 1) * 64), slice(c * 128, (c + 1) * 128))
                y = _silu(gs[r % 2][ps]) * us[r % 2][ps]
                o_ref[r * rows + q * 64:r * rows + (q + 1) * 64, ps[1]] = y.astype(o_ref.dtype)

    for r in range(T // rows):
        project(r)
        if r > 0:
            finish(r - 1)
    finish(T // rows - 1)


def swiglu_up(rw, l, x, nw, mod, wgu, skip_ctx=False):
    R, D = x.shape
    tm = min(rw.tm, 512)
    tn = D_FF
    nj = D_FF // tn
    mi = rw.mod_index(tm)
    skip = rw.NC // tm if skip_ctx else 0
    return pl.pallas_call(
        _swiglu_up_kernel,
        grid=(R // tm - skip, nj),
        in_specs=[
            pl.BlockSpec((tm, D), lambda i, j: (i + skip, 0)),
            pl.BlockSpec((None, 1, D), lambda i, j: (l, 0, 0)),
            pl.BlockSpec((None, 1, 8, D), lambda i, j: (l, mi(i + skip), 0, 0)),
            pl.BlockSpec((None, D, tn), lambda i, j: (l, 0, j)),
            pl.BlockSpec((None, D, tn), lambda i, j: (l, 0, nj + j)),
        ],
        out_specs=pl.BlockSpec((tm, tn), lambda i, j: (i + skip, j)),
        out_shape=jax.ShapeDtypeStruct((R, D_FF), BF16),
        scratch_shapes=[pltpu.VMEM((tm, D), BF16)] + [pltpu.VMEM((min(256, tm), tn), F32)] * 4,
        compiler_params=_cp("arbitrary", "arbitrary"),
        name="swiglu_up",
    )(x, nw, mod, wgu, wgu)


def _swiglu_down_kernel(a_ref, w_ref, x_ref, mod_ref, o_ref):
    o_ref[...] = x_ref[...] + mod_ref[0, 5:6, :] * _dot(a_ref[...], w_ref[...])


def swiglu_down(rw, l, a, w, x, mod, skip_ctx=False):
    R, D = x.shape
    tm = min(rw.tm, 512)
    mi = rw.mod_index(tm)
    skip = rw.NC // tm if skip_ctx else 0
    return pl.pallas_call(
        _swiglu_down_kernel,
        grid=(R // tm - skip,),
        in_specs=[
            pl.BlockSpec((tm, D_FF), lambda i: (i + skip, 0)),
            pl.BlockSpec((None, D_FF, D), lambda i: (l, 0, 0)),
            pl.BlockSpec((tm, D), lambda i: (i + skip, 0)),
            pl.BlockSpec((None, 1, 8, D), lambda i: (l, mi(i + skip), 0, 0)),
        ],
        out_specs=pl.BlockSpec((tm, D), lambda i: (i + skip, 0)),
        out_shape=jax.ShapeDtypeStruct((R, D), F32),
        compiler_params=_cp("arbitrary"),
        name="swiglu_down",
    )(a, w, x, mod)


def _final_norm_kernel(x_ref, w_ref, o_ref):
    x = x_ref[...]
    ms = jnp.mean(x * x, axis=-1, keepdims=True)
    o_ref[...] = x * lax.rsqrt(ms + EPS) * w_ref[...]


def final_norm(rw, x, w):
    D = x.shape[1]
    tm = rw.tm
    n0 = rw.NC // tm
    nl = rw.B * rw.Ll
    return pl.pallas_call(
        _final_norm_kernel,
        grid=(nl // tm,),
        in_specs=[pl.BlockSpec((tm, D), lambda i: (n0 + i, 0)), pl.BlockSpec((1, D), lambda i: (0, 0))],
        out_specs=pl.BlockSpec((tm, D), lambda i: (i, 0)),
        out_shape=jax.ShapeDtypeStruct((nl, D), F32),
        compiler_params=_cp("arbitrary"),
        name="final_norm",
    )(x, w.reshape(1, D))


def _regroup_w_in(w_in):
    o_dt = 3072
    o_gdn = 3088
    o_a = o_gdn + 2048
    o_b = o_a + 8
    o_gate = o_gdn + 2064
    wt = jnp.swapaxes(w_in, 1, 2).astype(BF16)
    pieces = [
        wt[:, 0:3072],
        wt[:, o_gdn:o_gdn + 2048],
        wt[:, o_gate:o_gate + 3072],
        wt[:, o_dt:o_dt + 16],
        wt[:, o_a:o_a + 4], wt[:, o_b:o_b + 4],
        wt[:, o_a + 4:o_a + 8], wt[:, o_b + 4:o_b + 8],
        jnp.zeros((wt.shape[0], N_IN_PAD - C_SM - 32, wt.shape[2]), wt.dtype),
    ]
    return jnp.swapaxes(jnp.concatenate(pieces, axis=1), 1, 2)


def kernel(x, c, ctx, c_ctx, w_ada, b_ada, norm1_w, norm2_w, w_in, hy_conv_w, hy_conv_b, hy_w1, hy_b1, hy_w2, hy_b2, hy_w3, hy_freq, hy_bias, ssm_conv_w, ssm_conv_b, ssm_dt_bias, ssm_A_log, ssm_D, ssm_norm_w, gdn_conv_w, gdn_dt_bias, gdn_A_log, gdn_norm_w, w_hy_out, w_ssm_out, w_gdn_out, w_out, w_gate_up, w_down, final_norm_w):
    B, Ll, D = x.shape
    Lc = ctx.shape[1]
    depth = w_ada.shape[0]
    assert Lc == CONV_ROWS and D == D_MODEL and B <= 15
    rw = Rows(B, Lc, Ll)
    R, NC = rw.R, rw.NC

    xs = (ctx.reshape(B * Lc, D), x.reshape(B * Ll, D))

    svec = jnp.concatenate([c_ctx[None, :], c, jnp.zeros((15 - B, D), F32)], axis=0)
    mod = ada_modulation(svec, w_ada, b_ada)
    mod = jnp.pad(mod.reshape(depth, 16, 6, D), ((0, 0), (0, 0), (0, 2), (0, 0)))

    w_in_r = _regroup_w_in(w_in)
    par = _in_proj_params(hy_conv_w, hy_conv_b, ssm_conv_w, ssm_conv_b, gdn_conv_w, ssm_dt_bias, gdn_dt_bias,
                          gdn_A_log)
    norm1 = norm1_w.reshape(depth, 1, D)
    norm2 = norm2_w.reshape(depth, 1, D)
    w_hy_o, w_ssm_o, w_gdn_o, w_o = (w.astype(BF16) for w in (w_hy_out, w_ssm_out, w_gdn_out, w_out))
    w_gu, w_dn = w_gate_up.astype(BF16), w_down.astype(BF16)
    dft_l = dft_tables_split(Ll)
    dft_c = dft_tables_split(Lc)
    feat_l, feat_c = hy_features(Ll), hy_features(Lc)

    for l in range(depth):
        p, sm = in_proj(rw, xs, l, norm1, mod, w_in_r, par)

        sm32_t = sm[:, :32].T
        dt_t = sm32_t[:16].reshape(2, 8, R)
        g_t = sm32_t[16:32].reshape(16, R // GDN_CHUNK, GDN_CHUNK).transpose(1, 0, 2)

        alx = jnp.repeat(ssm_A_log[l], SSM_HEAD_DIM, axis=-1).reshape(2, 1, 512)
        alc = ssm_A_log[l].reshape(2, 8, 1)
        y_f, y_b = ssd_scan(rw, p, sm, dt_t, alx, alc)
        dx = jnp.repeat(ssm_D[l], SSM_HEAD_DIM).reshape(1, 512)

        o_f, o_b = gdn_scan(rw, *gdn_prep(p, sm, g_t))

        last = l == depth - 1
        hyu = p
        parts = []
        for (Bn, L, blk0, (fwd, inv), feat) in ((B, Lc, 0, dft_c, feat_c), (B, Ll, NC // Ll, dft_l, feat_l)):
            if NC % L:
                raise ValueError("latent length must divide the context row count")
            if last and blk0 == 0:
                parts.append(None)
                continue
            filt = hy_filter(feat, hy_w1[l], hy_b1[l], hy_w2[l], hy_b2[l], hy_w3[l], hy_freq[l])
            kspec = filter_spectrum(fwd, *filt)
            z1 = long_conv(Bn, L, hyu, blk0, 0, hyu, blk0, 1, hy_bias[l, 0], fwd, inv, kspec, 0, F32)
            yy = long_conv(Bn, L, z1, 0, 0, hyu, blk0, 2, hy_bias[l, 1], fwd, inv, kspec, 1, BF16)
            parts.append(yy)
        y_hy = (parts[1] if last else parts[0], parts[1])

        xa = merge(rw, l, y_hy, y_f, y_b, dx, ssm_norm_w[l].reshape(1, 512),
                   o_f, o_b, jnp.tile(gdn_norm_w[l], GDN_HEADS).reshape(1, 512), p,
                   w_hy_o, w_ssm_o, w_gdn_o, w_o, xs, mod, skip_ctx=last)
        act = swiglu_up(rw, l, xa, norm2, mod, w_gu, skip_ctx=last)
        xa = swiglu_down(rw, l, act, w_dn, xa, mod, skip_ctx=last)
        xs = (xa, xa)

    out = final_norm(rw, xa, final_norm_w)
    return out.reshape(B, Ll, D)
```

```python
import functools
import math

import jax
import jax.numpy as jnp
import numpy as np
from jax import lax
from jax.experimental import pallas as pl
from jax.experimental.pallas import tpu as pltpu

F32 = jnp.float32
BF16 = jnp.bfloat16
HI = lax.Precision.HIGHEST

EPS = 1e-6
D_MODEL = 1024
GRID_W = 64

HY_WIDTH = 512
HY_BANDS = 16
HY_EMB = 1 + 2 * HY_BANDS
HY_HIDDEN = 64
HY_SHORT_DECAY_PCT = 0.3
HY_LONG_DECAY_PCT = 1.5
HY_TARGET = 1e-2

SSM_HEADS = 8
SSM_HEAD_DIM = 64
SSM_WIDTH = 512
SSM_GROUPS = 2
SSM_HPG = 4
SSM_STATE = 128
SSM_CHUNK = 128
SSM_GW = SSM_HPG * SSM_HEAD_DIM

GDN_HEADS = 4
GDN_DK = 128
GDN_DV = 128
GDN_CHUNK = 64

D_FF = 2816

C_HY = 0
C_Z = 1536
C_XBC = 2048
C_QKV = 3072
C_GG = 4608
C_GATE = 5120
C_SM = 8192

CONV_ROWS = 256
FREQ_BLK = 256

VMEM_LIMIT = 56 * 1024 * 1024


def _cp(*sem, flags=None):
    return pltpu.CompilerParams(dimension_semantics=sem, vmem_limit_bytes=VMEM_LIMIT, flags=flags)


def _sigmoid(x):
    return 1.0 / (1.0 + jnp.exp(-x))


def _silu(x):
    return x * _sigmoid(x)


def _softplus(x):
    return jnp.maximum(x, 0.0) + jnp.log1p(jnp.exp(-jnp.abs(x)))


def _dot(a, b, precision=None):
    return jnp.dot(a, b, precision=precision, preferred_element_type=F32)


def _dot_nt(a, b):
    return lax.dot_general(a, b, (((1,), (1,)), ((), ())), preferred_element_type=F32)


def _dot_tn(a, b):
    return lax.dot_general(a, b, (((0,), (0,)), ((), ())), preferred_element_type=F32)


def _ada_kernel(s_ref, w_ref, b_ref, o_ref):
    s = _silu(s_ref[...])
    o_ref[0] = _dot(s, w_ref[0], HI) + b_ref[0]


def ada_modulation(svec, w_ada, b_ada):
    depth = w_ada.shape[0]
    D = D_MODEL
    return pl.pallas_call(
        _ada_kernel,
        grid=(depth, 6),
        in_specs=[
            pl.BlockSpec((16, D), lambda l, j: (0, 0)),
            pl.BlockSpec((1, D, D), lambda l, j: (l, 0, j)),
            pl.BlockSpec((1, 1, D), lambda l, j: (l, 0, j)),
        ],
        out_specs=pl.BlockSpec((1, 16, D), lambda l, j: (l, 0, j)),
        out_shape=jax.ShapeDtypeStruct((depth, 16, 6 * D), F32),
        compiler_params=_cp("arbitrary", "arbitrary"),
        name="ada",
    )(svec, w_ada, b_ada.reshape(depth, 1, 6 * D))


def _norm_mod(x, nw, scale, shift):
    ms = jnp.mean(x * x, axis=-1, keepdims=True)
    return (x * lax.rsqrt(ms + EPS) * nw) * (1.0 + scale) + shift


IN_FLIGHT = 4
N_IN_PAD = C_SM + 128
IN_TN = N_IN_PAD // 5
MODE_RAW, MODE_CONV, MODE_CONV_SILU, MODE_CONV_SILU_L2, MODE_SMALL = range(5)


def _tile_mode(tile):
    col = tile * 128
    if col < C_Z:
        return MODE_CONV
    if col < C_XBC:
        return MODE_RAW
    if col < C_QKV:
        return MODE_CONV_SILU
    if col < C_QKV + 1024:
        return MODE_CONV_SILU_L2
    if col < C_GG:
        return MODE_CONV_SILU
    if col < C_SM:
        return MODE_RAW
    return MODE_SMALL
PAR_W0, PAR_W1, PAR_W2, PAR_BIAS, PAR_L2SCALE, PAR_SBIAS, PAR_SALOG, PAR_SKIND = range(8)


def _in_proj_kernel(xc_ref, xl_ref, nw_ref, mod_ref, w_ref, par_ref, o_ref, sm_ref, h_ref, raw0_ref, raw1_ref, *,
                    nctx_blk):
    j = pl.program_id(1)
    nj = N_IN_PAD // IN_TN
    raws = (raw0_ref, raw1_ref)

    @pl.when((j == 0) & (pl.program_id(0) < nctx_blk))
    def _():
        h = _norm_mod(xc_ref[...], nw_ref[...], mod_ref[0, 1:2, :], mod_ref[0, 0:1, :])
        h_ref[...] = h.astype(BF16)

    @pl.when((j == 0) & (pl.program_id(0) >= nctx_blk))
    def _():
        h = _norm_mod(xl_ref[...], nw_ref[...], mod_ref[0, 1:2, :], mod_ref[0, 0:1, :])
        h_ref[...] = h.astype(BF16)

    T = h_ref.shape[0]
    G = GRID_W
    per_ctx = CONV_ROWS // G
    is_latent = pl.program_id(0) >= nctx_blk
    sub = lax.broadcasted_iota(jnp.int32, (8, 128), 0)

    def raw_piece(src, g, c):
        return src[g * G:(g + 1) * G, c * 128:(c + 1) * 128]

    retired = []

    def retire(y):
        bits = pltpu.bitcast(y[0:8], jnp.int32)
        zero = lax.shift_right_logical(lax.shift_right_logical(bits, 16), 16)
        retired.append(jnp.tile(zero.astype(F32), (G // 8, 1)))

    def conv(src, g, c):
        cs = slice(c * 128, (c + 1) * 128)
        x = raw_piece(src, g, c)
        if len(retired) >= IN_FLIGHT:
            x = x + retired[-IN_FLIGHT]
        zero = jnp.zeros((1, 128), F32)
        before = zero if g % per_ctx == 0 else jnp.where(is_latent, 0.0, src[g * G - 1:g * G, cs])
        after = zero if g % per_ctx == per_ctx - 1 else jnp.where(is_latent, 0.0, src[(g + 1) * G:(g + 1) * G + 1, cs])
        rp = pltpu.roll(x, 1, 0)
        rn = pltpu.roll(x, G - 1, 0)
        prev = jnp.concatenate([jnp.where(sub == 0, before, rp[0:8]), rp[8:]], axis=0)
        nxt = jnp.concatenate([rn[:G - 8], jnp.where(sub == 7, after, rn[G - 8:])], axis=0)
        return (prev * par_ref[PAR_W0:PAR_W0 + 1, cs] + x * par_ref[PAR_W1:PAR_W1 + 1, cs]
                + nxt * par_ref[PAR_W2:PAR_W2 + 1, cs] + par_ref[PAR_BIAS:PAR_BIAS + 1, cs])

    def conv_silu(src, g, c):
        return _silu(conv(src, g, c))

    def conv_silu_l2(src, g, c):
        y = _silu(conv(src, g, c))
        y = y * lax.rsqrt(jnp.sum(y * y, axis=-1, keepdims=True) + EPS)
        return y * par_ref[PAR_L2SCALE:PAR_L2SCALE + 1, c * 128:(c + 1) * 128]

    def small(src, g, c):
        cs = slice(c * 128, (c + 1) * 128)
        acc = raw_piece(src, g, c)
        kind = par_ref[PAR_SKIND:PAR_SKIND + 1, cs]
        sp = _softplus(acc + par_ref[PAR_SBIAS:PAR_SBIAS + 1, cs])
        dec = -jnp.exp(par_ref[PAR_SALOG:PAR_SALOG + 1, cs]) * sp
        return jnp.where(kind == 0.0, sp, jnp.where(kind == 1.0, dec, jnp.where(kind == 2.0, _sigmoid(acc), 0.0)))

    rows_mm = 256
    tiles = IN_TN // 128
    piece_fn = {MODE_RAW: raw_piece, MODE_CONV: conv, MODE_CONV_SILU: conv_silu,
                MODE_CONV_SILU_L2: conv_silu_l2, MODE_SMALL: small}

    heavy_modes = (MODE_CONV_SILU, MODE_CONV_SILU_L2)
    col_slices = [slice(c0, min(c0 + 256, IN_TN)) for c0 in range(0, IN_TN, 256)]

    def project(dst, r, cs=slice(None)):
        rs = slice(r * rows_mm, (r + 1) * rows_mm)
        dst[rs, cs] = _dot(h_ref[rs, :], w_ref[:, cs])

    def finish(src, blk, g, c):
        mode = _tile_mode(blk * tiles + c)
        y = piece_fn[mode](src, g, c)
        if mode in heavy_modes:
            retire(y)
        if mode == MODE_SMALL:
            sm_ref[g * G:(g + 1) * G, :] = y
            y = jnp.zeros_like(y)
        o_ref[g * G:(g + 1) * G, c * 128:(c + 1) * 128] = y.astype(o_ref.dtype)

    for step in range(nj + 1):
        @pl.when(j == step)
        def _(step=step):
            blk = step - 1
            src, dst = raws[blk % 2], raws[step % 2]
            retired.clear()
            light = blk >= 0 and not any(_tile_mode(blk * tiles + c) in heavy_modes for c in range(tiles))
            for r in range(T // rows_mm):
                pieces = [] if blk < 0 else [(g, c) for g in range(r * rows_mm // G, (r + 1) * rows_mm // G)
                                             for c in range(tiles)]
                if step == nj:
                    slabs = []
                elif light:
                    slabs = col_slices
                else:
                    slabs = [slice(None)]
                per = -(-len(pieces) // max(len(slabs), 1))
                for n in range(max(len(slabs), 1)):
                    if n < len(slabs):
                        project(dst, r, slabs[n])
                    for g, c in pieces[n * per:(n + 1) * per]:
                        finish(src, blk, g, c)


class Rows:
    def __init__(self, B, Lc, Ll):
        self.B, self.Lc, self.Ll = B, Lc, Ll
        self.NC = B * Lc
        self.R = B * Lc + B * Ll
        assert self.NC % Ll == 0 or Ll % self.NC == 0
        tm = 1024
        while self.NC % tm or Ll % tm:
            tm //= 2
        self.tm = tm

    def mod_index(self, tm):
        nctx = self.NC // tm
        per = self.Ll // tm
        return lambda i: jnp.where(i < nctx, 0, 1 + (i - nctx) // per)


def _stream_specs(rw, tm, xs, ngrid, skip=0):
    xc, xl = xs
    nctx = rw.NC // tm
    off = nctx if xl.shape[0] == rw.R else 0
    D = xc.shape[1]
    if ngrid == 1:
        return [pl.BlockSpec((tm, D), lambda i: (jnp.minimum(i + skip, nctx - 1), 0)),
                pl.BlockSpec((tm, D), lambda i: (jnp.maximum(i + skip - nctx, 0) + off, 0))]
    return [pl.BlockSpec((tm, D), lambda i, j: (jnp.minimum(i + skip, nctx - 1), 0)),
            pl.BlockSpec((tm, D), lambda i, j: (jnp.maximum(i + skip - nctx, 0) + off, 0))]


def in_proj(rw, xs, l, nw, mod, w, par):
    R = rw.R
    D = xs[0].shape[1]
    N = w.shape[2]
    tm, tn = rw.tm, IN_TN
    nj = N // tn
    assert N == N_IN_PAD
    mi = rw.mod_index(tm)
    done = lambda j: jnp.maximum(j - 1, 0)
    return pl.pallas_call(
        functools.partial(_in_proj_kernel, nctx_blk=rw.NC // tm),
        grid=(R // tm, nj + 1),
        in_specs=_stream_specs(rw, tm, xs, 2) + [
            pl.BlockSpec((None, 1, D), lambda i, j: (l, 0, 0)),
            pl.BlockSpec((None, 1, 8, D), lambda i, j: (l, mi(i), 0, 0)),
            pl.BlockSpec((None, D, tn), lambda i, j: (l, 0, jnp.minimum(j, nj - 1))),
            pl.BlockSpec((None, 8, tn), lambda i, j: (l, 0, done(j))),
        ],
        out_specs=[pl.BlockSpec((tm, tn), lambda i, j: (i, done(j))),
                   pl.BlockSpec((tm, 128), lambda i, j: (i, 0))],
        out_shape=[jax.ShapeDtypeStruct((R, N), BF16), jax.ShapeDtypeStruct((R, 128), F32)],
        scratch_shapes=[pltpu.VMEM((tm, D), BF16), pltpu.VMEM((tm, tn), F32), pltpu.VMEM((tm, tn), F32)],
        compiler_params=_cp("arbitrary", "arbitrary"),
        name="in_proj",
    )(xs[0], xs[1], nw, mod, w, par)


def _in_proj_params(hy_conv_w, hy_conv_b, ssm_conv_w, ssm_conv_b, gdn_conv_w, ssm_dt_bias, gdn_dt_bias, gdn_A_log):
    depth = hy_conv_w.shape[0]

    def row(pieces):
        out, pos = [], 0
        for off, a in pieces:
            out += [jnp.zeros((depth, off - pos), F32), a.astype(F32)]
            pos = off + a.shape[1]
        return jnp.concatenate(out + [jnp.zeros((depth, N_IN_PAD - pos), F32)], axis=1)
    z4 = jnp.zeros((depth, 4), F32)
    conv = [row([(C_HY, hy_conv_w[:, t]), (C_XBC, ssm_conv_w[:, t]), (C_QKV, gdn_conv_w[:, t])]) for t in range(3)]
    bias = row([(C_HY, hy_conv_b), (C_XBC, ssm_conv_b)])
    l2s = row([(C_QKV, jnp.full((depth, 512), GDN_DK ** -0.5, F32)), (C_QKV + 512, jnp.ones((depth, 512), F32))])
    sbias = row([(C_SM, jnp.concatenate([ssm_dt_bias.reshape(depth, 16), gdn_dt_bias[:, 0], z4,
                                         gdn_dt_bias[:, 1], z4], axis=1))])
    salog = row([(C_SM + 16, jnp.concatenate([gdn_A_log[:, 0], z4, gdn_A_log[:, 1], z4], axis=1))])
    kind = np.full((depth, N_IN_PAD), 3.0, np.float32)
    kind[:, C_SM:C_SM + 16] = 0.0
    kind[:, C_SM + 16:C_SM + 20] = 1.0
    kind[:, C_SM + 24:C_SM + 28] = 1.0
    kind[:, C_SM + 20:C_SM + 24] = 2.0
    kind[:, C_SM + 28:C_SM + 32] = 2.0
    return jnp.stack(conv + [bias, l2s, sbias, salog, jnp.asarray(kind)], axis=1)


def _hy_filter_kernel(z_ref, w1_ref, b1_ref, w2_ref, b2_ref, w3_ref, f0_ref, f1_ref, win_ref, oe_ref, oo_ref,
                      h_ref, split_ref):
    @pl.when(pl.program_id(1) == 0)
    def _():
        h1 = jnp.sin(f0_ref[...] * (_dot(z_ref[...], w1_ref[...], HI) + b1_ref[...]))
        h_ref[...] = jnp.sin(f1_ref[...] * (_dot(h1, w2_ref[...], HI) + b2_ref[...]))

    a1 = h_ref[...].astype(BF16)
    a2 = (h_ref[...] - a1.astype(F32)).astype(BF16)
    b1 = w3_ref[...].astype(BF16)
    b2 = (w3_ref[...] - b1.astype(F32)).astype(BF16)
    h = (_dot(a1, b1) + _dot(a1, b2) + _dot(a2, b1)) * win_ref[...]
    tl = h.shape[0]
    row = lax.broadcasted_iota(jnp.int32, (tl, 1), 0) + pl.program_id(0) * tl
    drop = (row == 0) & (pl.program_id(1) % 2 == 1)
    h = jnp.where(drop, 0.0, h)
    for c in range(h.shape[1] // 128):
        cs = slice(c * 128, (c + 1) * 128)
        s_c = split_ref.at[c]
        s_c[...] = h[:, cs]
        oe_ref[:, cs] = s_c[pl.ds(0, tl // 2, stride=2), :].astype(oe_ref.dtype)
        oo_ref[:, cs] = s_c[pl.ds(1, tl // 2, stride=2), :].astype(oo_ref.dtype)


def hy_features(L):
    t = jnp.linspace(0.0, 1.0, L, dtype=F32)[:, None]
    w = 2.0 * math.pi * jnp.arange(L, dtype=F32)[:, None] / L
    f = jnp.linspace(1e-4, HY_BANDS - 1, HY_BANDS, dtype=F32)[None, :]
    z = jnp.concatenate([t, jnp.cos(f * w), -jnp.sin(f * w)], axis=-1)
    z = jnp.pad(z, ((0, 0), (0, 128 - HY_EMB)))
    min_decay = math.log(HY_TARGET) / HY_LONG_DECAY_PCT
    max_decay = math.log(HY_TARGET) / HY_SHORT_DECAY_PCT
    deltas = jnp.linspace(min_decay, max_decay, HY_WIDTH, dtype=F32)
    window = jnp.exp(-t * jnp.abs(deltas))
    return z, window


def hy_filter(feat, w1, b1, w2, b2, w3, freq):
    z, window = feat
    L = z.shape[0]
    H = HY_HIDDEN
    w1p = jnp.pad(w1, ((0, 128 - HY_EMB), (0, 128 - H)))
    w2p = jnp.pad(w2, ((0, 128 - H), (0, 128 - H)))
    w3p = jnp.pad(w3, ((0, 128 - H), (0, 0)))
    pad1 = lambda v: jnp.pad(v, (0, 128 - H)).reshape(1, 128)
    tl = 256
    full = lambda shape: pl.BlockSpec(shape, lambda i, j: (0, 0))
    return pl.pallas_call(
        _hy_filter_kernel,
        grid=(L // tl, 4),
        in_specs=[
            pl.BlockSpec((tl, 128), lambda i, j: (i, 0)),
            full((128, 128)), full((1, 128)), full((128, 128)), full((1, 128)),
            pl.BlockSpec((128, HY_WIDTH), lambda i, j: (0, j)),
            full((1, 128)), full((1, 128)),
            pl.BlockSpec((tl, HY_WIDTH), lambda i, j: (i, 0)),
        ],
        out_specs=[pl.BlockSpec((tl // 2, HY_WIDTH), lambda i, j: (i, j))] * 2,
        out_shape=[jax.ShapeDtypeStruct((L // 2, 4 * HY_WIDTH), BF16)] * 2,
        scratch_shapes=[pltpu.VMEM((tl, 128), F32), pltpu.VMEM((HY_WIDTH // 128, tl, 128), F32)],
        compiler_params=_cp("arbitrary", "arbitrary"),
        name="hy_filter",
    )(z, w1p, pad1(b1), w2p, pad1(b2), w3p, pad1(freq[0]), pad1(freq[1]), window)


def dft_tables_split(L):
    N = 2 * L
    H = L // 2
    q = np.arange(H, dtype=np.int64)[:, None]
    m = np.arange(H, dtype=np.int64)[None, :]
    ang_e = ((q * 2 * m) % N).astype(np.float64) * (2.0 * math.pi / N)
    ang_o = ((q * (2 * m + 1)) % N).astype(np.float64) * (2.0 * math.pi / N)
    alt = (1 - 2 * (m % 2)).astype(np.float64)
    ce, co = np.cos(ang_e), np.cos(ang_o)
    se = np.where(q == 0, alt, -np.sin(ang_e))
    so = np.where(q == 0, -alt, -np.sin(ang_o))
    w = np.where(q == 0, 1.0, 2.0) / N
    ise = np.where(q == 0, 2.0 / N * alt, -np.sin(ang_e) * w)
    iso = np.where(q == 0, -2.0 / N * alt, -np.sin(ang_o) * w)
    fwd = np.stack([ce, co, se, so])
    inv = np.stack([(ce * w).T, ise.T, (co * w).T, iso.T])
    return jnp.asarray(fwd, dtype=BF16), jnp.asarray(inv, dtype=BF16)


def _filter_spectrum_kernel(fwd_ref, he_ref, ho_ref, rlo_ref, rhi_ref, ilo_ref, ihi_ref):
    he, ho = he_ref[...], ho_ref[...]
    ae, ao = _dot(fwd_ref[0], he), _dot(fwd_ref[1], ho)
    be, bo = _dot(fwd_ref[2], he), _dot(fwd_ref[3], ho)
    rlo_ref[...] = ae + ao
    rhi_ref[...] = ae - ao
    first = (lax.broadcasted_iota(jnp.int32, (fwd_ref.shape[1], 1), 0) == 0) & (pl.program_id(0) == 0)
    ilo_ref[...] = jnp.where(first, be, be + bo)
    ihi_ref[...] = jnp.where(first, bo, bo - be)


def filter_spectrum(fwd, taps_even, taps_odd):
    H, N = taps_even.shape
    C = HY_WIDTH
    FB = min(FREQ_BLK, H)
    tap = pl.BlockSpec((H, C), lambda f, j: (0, j))
    out = pl.BlockSpec((FB, C), lambda f, j: (f, j))
    return pl.pallas_call(
        _filter_spectrum_kernel,
        grid=(H // FB, N // C),
        in_specs=[pl.BlockSpec((4, FB, H), lambda f, j: (0, f, 0)), tap, tap],
        out_specs=[out] * 4,
        out_shape=[jax.ShapeDtypeStruct((H, N), F32)] * 4,
        compiler_params=_cp("arbitrary", "arbitrary"),
        name="filter_spectrum",
    )(fwd, taps_even, taps_odd)


def _long_conv_kernel(u_ref, g_ref, bias_ref, fwd_ref, inv_ref, ar0_ref, ar1_ref, ar0h_ref, ar1h_ref,
                      ai0_ref, ai1_ref, ai0h_ref, ai1h_ref, o_ref, ue_ref, uo_ref, acce_ref, acco_ref, y_ref):
    f = pl.program_id(1)
    half = ue_ref.shape[0]

    lane_tiles = [slice(c * 128, (c + 1) * 128) for c in range(y_ref.shape[0])]

    @pl.when(f == 0)
    def _():
        for c, cs in enumerate(lane_tiles):
            y_c = y_ref.at[c]
            y_c[...] = u_ref[:, cs].astype(F32)
            ue_ref[:, cs] = y_c[pl.ds(0, half, stride=2), :].astype(BF16)
            uo_ref[:, cs] = y_c[pl.ds(1, half, stride=2), :].astype(BF16)
        acce_ref[...] = jnp.zeros_like(acce_ref)
        acco_ref[...] = jnp.zeros_like(acco_ref)

    ue, uo = ue_ref[...], uo_ref[...]
    ae, ao = _dot(fwd_ref[0], ue), _dot(fwd_ref[1], uo)
    be, bo = _dot(fwd_ref[2], ue), _dot(fwd_ref[3], uo)
    ur, ur2 = ae + ao, ae - ao
    ui, ui2 = be + bo, bo - be
    first = (lax.broadcasted_iota(jnp.int32, (fwd_ref.shape[1], 1), 0) == 0) & (f == 0)
    kr, kr2 = ar0_ref[...] + ar1_ref[...], ar0h_ref[...] + ar1h_ref[...]
    ki = jnp.where(first, ai0_ref[...] + ai1_ref[...], ai0_ref[...] - ai1_ref[...])
    ki2 = ai0h_ref[...] - ai1h_ref[...]
    pr, pi = ur * kr - ui * ki, ur * ki + ui * kr
    pr2, pi2 = ur2 * kr2 - ui2 * ki2, ur2 * ki2 + ui2 * kr2
    dc, ny = ur * kr, ur2 * kr2
    gr = jnp.where(first, dc + ny, pr + pr2)
    gi = jnp.where(first, be * ki - bo * ki2, pi - pi2)
    hr = jnp.where(first, dc - ny, pr - pr2)
    hi = jnp.where(first, be * ki2 + bo * ki, pi + pi2)
    acce_ref[...] += _dot(inv_ref[0], gr.astype(BF16)) + _dot(inv_ref[1], gi.astype(BF16))
    acco_ref[...] += _dot(inv_ref[2], hr.astype(BF16)) + _dot(inv_ref[3], hi.astype(BF16))

    @pl.when(f == pl.num_programs(1) - 1)
    def _():
        for c, cs in enumerate(lane_tiles):
            y_c = y_ref.at[c]
            y_c[pl.ds(0, half, stride=2), :] = acce_ref[:, cs]
            y_c[pl.ds(1, half, stride=2), :] = acco_ref[:, cs]
            u = u_ref[:, cs].astype(F32)
            o_ref[:, cs] = (g_ref[:, cs].astype(F32) * (y_c[...] + u * bias_ref[:, cs])).astype(o_ref.dtype)


def long_conv(B, L, u, u_rb0, u_cb, gate, g_rb0, gate_cb, bias, fwd, inv, kspec, order, out_dtype):
    C = HY_WIDTH
    H = L // 2
    FB = min(FREQ_BLK, H)
    nfb = H // FB
    kblk = lambda part, d: pl.BlockSpec((FB, C), lambda b, f: (f, 2 * order + d))
    kops = [kspec[part] for part in range(4) for _ in range(2)]
    return pl.pallas_call(
        _long_conv_kernel,
        grid=(B, nfb),
        in_specs=[
            pl.BlockSpec((L, C), lambda b, f: (u_rb0 + b, u_cb)),
            pl.BlockSpec((L, C), lambda b, f: (g_rb0 + b, gate_cb)),
            pl.BlockSpec((1, C), lambda b, f: (0, 0)),
            pl.BlockSpec((4, FB, H), lambda b, f: (0, f, 0)),
            pl.BlockSpec((4, H, FB), lambda b, f: (0, 0, f)),
            kblk(0, 0), kblk(0, 1), kblk(1, 0), kblk(1, 1), kblk(2, 0), kblk(2, 1), kblk(3, 0), kblk(3, 1),
        ],
        out_specs=pl.BlockSpec((L, C), lambda b, f: (b, 0)),
        out_shape=jax.ShapeDtypeStruct((B * L, C), out_dtype),
        scratch_shapes=[pltpu.VMEM((H, C), BF16), pltpu.VMEM((H, C), BF16),
                        pltpu.VMEM((H, C), F32), pltpu.VMEM((H, C), F32), pltpu.VMEM((C // 128, L, 128), F32)],
        compiler_params=_cp("arbitrary", "arbitrary"),
        name="long_conv",
    )(u, gate, bias.reshape(1, C), fwd, inv, *kops)


def _scan_blocks(rw, rows):
    nbc, nbl, base = rw.Lc // rows, rw.Ll // rows, rw.NC // rows

    def make(d):
        def f(b, s):
            jc = s if d == 0 else nbc - 1 - s
            jl = (s - nbc) if d == 0 else nbl - 1 - (s - nbc)
            return jnp.where(s < nbc, b * nbc + jc, base + b * nbl + jl)
        return f

    return [make(0), make(1)], nbc + nbl


def _expand_lanes(x, base, n, width):
    rows = x.shape[0]
    per = 128 // width
    lane = lax.broadcasted_iota(jnp.int32, (rows, 128), 1)
    tiles = []
    for t in range(n // per):
        c0 = base + t * per
        tile = jnp.broadcast_to(x[:, c0:c0 + 1], (rows, 128))
        for i in range(1, per):
            tile = jnp.where(lane >= i * width, jnp.broadcast_to(x[:, c0 + i:c0 + i + 1], (rows, 128)), tile)
        tiles.append(tile)
    return jnp.concatenate(tiles, axis=1)


def _ssd_kernel(xf, bf, cf, smf, dtf, xb, bb, cb_, smb, dtb, alx_ref, alc_ref, of_ref, ob_ref, h_ref):
    Q = SSM_CHUNK
    GW = SSM_GW

    @pl.when(pl.program_id(1) == 0)
    def _():
        h_ref[...] = jnp.zeros_like(h_ref)

    row = lax.broadcasted_iota(jnp.int32, (Q, Q), 0)
    col = lax.broadcasted_iota(jnp.int32, (Q, Q), 1)
    lane_head = lax.broadcasted_iota(jnp.int32, (Q, GW), 1) // SSM_HEAD_DIM
    dirs = ((xf, bf, cf, smf, dtf, of_ref), (xb, bb, cb_, smb, dtb, ob_ref))
    jobs = []
    for d in range(2):
        x_ref, b_ref, c_ref, sm_ref, dt_ref, o_ref = dirs[d]
        keep = (col <= row) if d == 0 else (col >= row)
        tri = keep.astype(BF16)
        tri_t = ((row <= col) if d == 0 else (row >= col)).astype(BF16)
        sm = sm_ref[...]
        a_x = -jnp.exp(alx_ref[d])
        dtx = _expand_lanes(sm, 8 * d, SSM_HEADS, SSM_HEAD_DIM)
        cumx = _expand_lanes(_dot_01_lhs(tri, sm), 8 * d, SSM_HEADS, SSM_HEAD_DIM) * a_x
        cumr = _dot_01_rhs(dt_ref[0], tri_t) * (-jnp.exp(alc_ref[d]))
        last = Q - 1 if d == 0 else 0
        totx = cumx[last:last + 1, :]
        xd = x_ref[...].astype(F32) * dtx
        xdw = xd * jnp.exp(totx - cumx)
        ecum = jnp.exp(cumx)
        for g in range(SSM_GROUPS):
            gs = slice(g * GW, (g + 1) * GW)
            jobs.append(dict(d=d, g=g, gs=gs, keep=keep, cumx=cumx, cumr=cumr, o_ref=o_ref,
                             bg=b_ref[:, g * SSM_STATE:(g + 1) * SSM_STATE].astype(BF16),
                             cg=c_ref[:, g * SSM_STATE:(g + 1) * SSM_STATE].astype(BF16),
                             xdg=xd[:, gs], xdw=xdw[:, gs].astype(BF16), ecum=ecum[:, gs],
                             etot=jnp.exp(totx[:, gs])))
    for j in jobs:
        j["cb"] = _dot_nt(j["cg"], j["bg"])
        j["h"] = h_ref[j["d"], j["g"]]
    for j in jobs:
        ms, xs = [], []
        for e4 in range(SSM_HPG):
            e = j["g"] * SSM_HPG + e4
            diff = j["cumx"][:, e * SSM_HEAD_DIM:e * SSM_HEAD_DIM + 1] - j["cumr"][e:e + 1, :]
            ms.append((j["cb"] * jnp.where(j["keep"], jnp.exp(diff), 0.0)).astype(BF16))
            xs.append(jnp.where(lane_head == e4, j["xdg"], 0.0).astype(BF16))
        yd = _dot(jnp.concatenate(ms, axis=1), jnp.concatenate(xs, axis=0))
        y_off = _dot(j["cg"], j["h"].astype(BF16)) * j["ecum"]
        j["o_ref"][:, j["gs"]] = (yd + y_off).astype(BF16)
    for j in jobs:
        h_ref[j["d"], j["g"]] = j["h"] * j["etot"] + _dot_tn(j["bg"], j["xdw"])


def ssd_scan(rw, p, sm, dtT, alx, alc):
    Q = SSM_CHUNK
    blks, nsteps = _scan_blocks(rw, Q)
    R = p.shape[0]
    in_specs = []
    for d in range(2):
        f = blks[d]
        in_specs += [
            pl.BlockSpec((Q, 512), lambda b, s, f=f: (f(b, s), C_XBC // 512)),
            pl.BlockSpec((Q, 256), lambda b, s, f=f: (f(b, s), C_XBC // 256 + 2)),
            pl.BlockSpec((Q, 256), lambda b, s, f=f: (f(b, s), C_XBC // 256 + 3)),
            pl.BlockSpec((Q, 128), lambda b, s, f=f: (f(b, s), 0)),
            pl.BlockSpec((1, 8, Q), lambda b, s, f=f, d=d: (d, 0, f(b, s))),
        ]
    in_specs += [pl.BlockSpec((2, 1, 512), lambda b, s: (0, 0, 0)), pl.BlockSpec((2, 8, 1), lambda b, s: (0, 0, 0))]
    ops = (p, p, p, sm, dtT)
    return pl.pallas_call(
        _ssd_kernel,
        grid=(rw.B, nsteps),
        in_specs=in_specs,
        out_specs=[pl.BlockSpec((Q, 512), lambda b, s, f=blks[d]: (f(b, s), 0)) for d in range(2)],
        out_shape=[jax.ShapeDtypeStruct((R, 512), BF16)] * 2,
        scratch_shapes=[pltpu.VMEM((2, SSM_GROUPS, SSM_STATE, SSM_GW), F32)],
        compiler_params=_cp("arbitrary", "arbitrary"),
        name="ssd_scan",
    )(*ops, *ops, alx, alc)


def _split3(x):
    x1 = x.astype(BF16)
    r = x - x1.astype(F32)
    x2 = r.astype(BF16)
    x3 = (r - x2.astype(F32)).astype(BF16)
    return x1, x2, x3


def _dot_01_lhs(m01, x):
    x1, x2, x3 = _split3(x)
    return _dot(m01, x1) + _dot(m01, x2) + _dot(m01, x3)


def _dot_01_rhs(x, m01):
    x1, x2, x3 = _split3(x)
    return _dot(x1, m01) + _dot(x2, m01) + _dot(x3, m01)


GDN_ROWS = 256


def _gdn_prep_kernel(q_ref, k_ref, v_ref, sm_ref, gT_ref, u_ref, w_ref, qg_ref, kd_ref, qk_ref, egl_ref):
    C = GDN_CHUNK
    row = lax.broadcasted_iota(jnp.int32, (C, C), 0)
    col = lax.broadcasted_iota(jnp.int32, (C, C), 1)
    lane2 = lax.broadcasted_iota(jnp.int32, (C, 2 * C), 1)
    row2 = lax.broadcasted_iota(jnp.int32, (C, 2 * C), 0)
    col2 = lane2 & (C - 1)
    left = lane2 < C
    lane_k = lax.broadcasted_iota(jnp.int32, (C, 2 * GDN_DK), 1) < GDN_DK
    jobs = []
    for d in range(2):
        keep = (col <= row) if d == 0 else (col >= row)
        tri = keep.astype(BF16)
        tri_t2 = ((row2 <= col2) if d == 0 else (row2 >= col2)).astype(BF16)
        last = C - 1 if d == 0 else 0
        for c in range(GDN_ROWS // C):
            rows = slice(c * C, (c + 1) * C)
            smc = sm_ref[rows, :]
            cums = _dot_01_lhs(tri, smc)
            cumr2 = _dot_01_rhs(gT_ref[c, 8 * d:8 * d + 8, :], tri_t2)
            tot = cums[last:last + 1, :]
            for h in range(GDN_HEADS):
                lg = 16 + 8 * d + h
                jobs.append(dict(d=d, c=c, h=h, rows=rows, hs=slice(h * 128, (h + 1) * 128), keep=keep,
                                 gc=cums[:, lg:lg + 1], beta=smc[:, lg + 4:lg + 5],
                                 gl=tot[:, lg:lg + 1], gr=cumr2[h:h + 1, :C], gr2=cumr2[h:h + 1, :]))

    def block_diag(x):
        return jnp.concatenate([jnp.where(left, x, 0.0), jnp.where(left, 0.0, x)], axis=0).astype(BF16)

    pairs = []
    for i in range(0, len(jobs), 2):
        j0, j1 = jobs[i], jobs[i + 1]
        d, rows = j0["d"], j0["rows"]
        ps = slice(j0["hs"].start, j1["hs"].stop)
        keep2 = (col2 <= row2) if d == 0 else (col2 >= row2)
        late2, early2 = (row2, col2) if d == 0 else (col2, row2)
        lev2 = [(((row2 ^ col2) >> (t + 1)) == 0) & ((late2 & (1 << t)) != 0) & ((early2 & (1 << t)) == 0)
                for t in range(6)]
        gcp = jnp.where(left, j0["gc"], j1["gc"])
        grp = jnp.where(left[0:1], j0["gr2"], j1["gr2"])
        decp = jnp.where(keep2, jnp.exp(gcp - grp), 0.0)
        kp = k_ref[rows, ps].astype(F32)
        kbp = kp * jnp.where(lane_k, j0["beta"], j1["beta"])
        kstack = jnp.concatenate([jnp.where(lane_k, kp, 0.0), jnp.where(lane_k, 0.0, kp)], axis=0).astype(BF16)
        a = _dot_nt(kbp.astype(BF16), kstack) * decp
        pairs.append(dict(j0=j0, j1=j1, a=a, lev=lev2, n=-jnp.where(lev2[0], a, 0.0)))
    for j in jobs:
        q = q_ref[j["rows"], j["hs"]].astype(BF16)
        k = k_ref[j["rows"], j["hs"]].astype(BF16)
        dec = jnp.where(j["keep"], jnp.exp(j["gc"] - j["gr"]), 0.0)
        qk_ref[j["d"], j["c"], j["h"]] = (_dot_nt(q, k) * dec).astype(BF16)
    for lev in range(1, 6):
        for pr in pairs:
            l = jnp.where(pr["lev"][lev], pr["a"], 0.0)
            pr["y"] = l + _dot(l.astype(BF16), block_diag(pr["n"]))
        for pr in pairs:
            pr["n"] = pr["n"] - pr["y"] - _dot(pr["n"].astype(BF16), block_diag(pr["y"]))
    for pr in pairs:
        rhs_pair = []
        for j in (pr["j0"], pr["j1"]):
            rows, hs, beta = j["rows"], j["hs"], j["beta"]
            k = k_ref[rows, hs].astype(F32)
            j["eg"] = jnp.exp(j["gc"])
            j["rhs"] = jnp.concatenate([v_ref[rows, hs].astype(F32) * beta, k * beta * j["eg"]], axis=1)
            rhs_pair.append(j["rhs"])
        rstack = jnp.concatenate(rhs_pair, axis=0).astype(BF16)
        pr["j0"]["cor"] = _dot(jnp.where(left, pr["n"], 0.0).astype(BF16), rstack)
        pr["j1"]["cor"] = _dot(jnp.where(left, 0.0, pr["n"]).astype(BF16), rstack)
    for j in jobs:
        d, rows, hs, gc, gl, eg = j["d"], j["rows"], j["hs"], j["gc"], j["gl"], j["eg"]
        q = q_ref[rows, hs].astype(F32)
        k = k_ref[rows, hs].astype(F32)
        sol = j["rhs"] + j["cor"]
        u_ref[d, rows, hs] = sol[:, :GDN_DV].astype(BF16)
        w_ref[d, rows, hs] = sol[:, GDN_DV:].astype(BF16)
        qg_ref[d, rows, hs] = (q * eg).astype(BF16)
        kd_ref[d, rows, hs] = (k * jnp.exp(gl - gc)).astype(BF16)
        egl_ref[d, j["c"], :, hs] = jnp.broadcast_to(jnp.exp(gl), (8, 128))


def gdn_prep(p, sm, gT):
    R = p.shape[0]
    T, C = GDN_ROWS, GDN_CHUNK
    nc = T // C
    col = lambda k: pl.BlockSpec((T, 512), lambda i: (i, C_QKV // 512 + k))
    dirrow = pl.BlockSpec((2, T, 512), lambda i: (0, i, 0))
    return pl.pallas_call(
        _gdn_prep_kernel,
        grid=(R // T,),
        in_specs=[col(0), col(1), col(2),
                  pl.BlockSpec((T, 128), lambda i: (i, 0)),
                  pl.BlockSpec((nc, 16, C), lambda i: (i, 0, 0))],
        out_specs=[dirrow, dirrow, dirrow, dirrow,
                   pl.BlockSpec((2, nc, GDN_HEADS, C, C), lambda i: (0, i, 0, 0, 0)),
                   pl.BlockSpec((2, nc, 8, 512), lambda i: (0, i, 0, 0))],
        out_shape=[jax.ShapeDtypeStruct((2, R, 512), BF16),
                   jax.ShapeDtypeStruct((2, R, 512), BF16),
                   jax.ShapeDtypeStruct((2, R, 512), BF16),
                   jax.ShapeDtypeStruct((2, R, 512), BF16),
                   jax.ShapeDtypeStruct((2, R // C, GDN_HEADS, C, C), BF16),
                   jax.ShapeDtypeStruct((2, R // C, 8, 512), F32)],
        compiler_params=_cp("arbitrary"),
        name="gdn_prep",
    )(p, p, p, sm, gT)


def _gdn_scan_kernel(uf, wf, qgf, kdf, qkf, eglf, ub, wb, qgb, kdb, qkb, eglb, of_ref, ob_ref, s_ref):
    C = GDN_CHUNK
    nch = GDN_ROWS // C

    @pl.when(pl.program_id(1) == 0)
    def _():
        s_ref[...] = jnp.zeros_like(s_ref)

    dirs = ((uf, wf, qgf, kdf, qkf, eglf, of_ref), (ub, wb, qgb, kdb, qkb, eglb, ob_ref))
    chains = [(d, h) for d in range(2) for h in range(GDN_HEADS)]
    S = {ch: s_ref[ch[0], ch[1]] for ch in chains}
    for i in range(nch):
        Sb, vnb, rows_of, c_of = {}, {}, {}, {}
        for d, h in chains:
            c_of[d] = i if d == 0 else nch - 1 - i
            rows_of[d] = slice(c_of[d] * C, (c_of[d] + 1) * C)
        for d, h in chains:
            hs = slice(h * 128, (h + 1) * 128)
            Sb[d, h] = S[d, h].astype(BF16)
            v_new = dirs[d][0][0, rows_of[d], hs].astype(F32) - _dot(dirs[d][1][0, rows_of[d], hs], Sb[d, h])
            vnb[d, h] = v_new.astype(BF16)
        for d, h in chains:
            hs = slice(h * 128, (h + 1) * 128)
            u_ref, w_ref, qg_ref, kd_ref, qk_ref, egl_ref, o_ref = dirs[d]
            S[d, h] = S[d, h] * egl_ref[0, c_of[d], 0:1, hs] + _dot_tn(kd_ref[0, rows_of[d], hs], vnb[d, h])
        for d, h in chains:
            hs = slice(h * 128, (h + 1) * 128)
            u_ref, w_ref, qg_ref, kd_ref, qk_ref, egl_ref, o_ref = dirs[d]
            o_ref[rows_of[d], hs] = (_dot(qg_ref[0, rows_of[d], hs], Sb[d, h])
                                     + _dot(qk_ref[0, c_of[d], h], vnb[d, h])).astype(BF16)
    for ch in chains:
        s_ref[ch[0], ch[1]] = S[ch]


def gdn_scan(rw, u, w, qg, kd, qk, egl):
    T, C = GDN_ROWS, GDN_CHUNK
    nc = T // C
    R = u.shape[1]
    nbc, nbl, base = rw.Lc // T, rw.Ll // T, rw.NC // T

    def blk(d):
        def f(b, s):
            jc = s if d == 0 else nbc - 1 - s
            jl = (s - nbc) if d == 0 else nbl - 1 - (s - nbc)
            return jnp.where(s < nbc, b * nbc + jc, base + b * nbl + jl)
        return f

    in_specs = []
    for d in range(2):
        f = blk(d)
        rowspec = pl.BlockSpec((1, T, 512), lambda b, s, f=f, d=d: (d, f(b, s), 0))
        in_specs += [rowspec, rowspec, rowspec, rowspec,
                     pl.BlockSpec((1, nc, GDN_HEADS, C, C), lambda b, s, f=f, d=d: (d, f(b, s), 0, 0, 0)),
                     pl.BlockSpec((1, nc, 8, 512), lambda b, s, f=f, d=d: (d, f(b, s), 0, 0))]
    out_specs = [pl.BlockSpec((T, 512), lambda b, s, f=blk(d): (f(b, s), 0)) for d in range(2)]
    ops = (u, w, qg, kd, qk, egl)
    return pl.pallas_call(
        _gdn_scan_kernel,
        grid=(rw.B, nbc + nbl),
        in_specs=in_specs,
        out_specs=out_specs,
        out_shape=[jax.ShapeDtypeStruct((R, 512), BF16)] * 2,
        scratch_shapes=[pltpu.VMEM((2, GDN_HEADS, GDN_DK, GDN_DV), F32)],
        compiler_params=_cp("arbitrary", "arbitrary"),
        name="gdn_scan",
    )(*ops, *ops)


def _merge_kernel(yhc_ref, yhl_ref, sf_ref, sb_ref, sx_ref, sz_ref, dx_ref, snw_ref, gf_ref, gb_ref, gg_ref, gnw_ref,
                  g0_ref, g1_ref, g2_ref, w0_ref, w1_ref, w2_ref, wo_ref, xc_ref, xl_ref, mod_ref, o_ref,
                  ys_ref, yg_ref, *, nctx_blk):
    tm = xc_ref.shape[0]
    rp = 64
    for r in range(tm // rp):
        rs = slice(r * rp, (r + 1) * rp)
        y = (sf_ref[rs, :].astype(F32) + sb_ref[rs, :].astype(F32)
             + sx_ref[rs, :].astype(F32) * dx_ref[...])
        y = y * _silu(sz_ref[rs, :].astype(F32))
        parts = []
        for g in range(SSM_GROUPS):
            yg = y[:, g * SSM_GW:(g + 1) * SSM_GW]
            parts.append(yg * lax.rsqrt(jnp.mean(yg * yg, axis=-1, keepdims=True) + EPS))
        ys_ref[rs, :] = (jnp.concatenate(parts, axis=1) * snw_ref[...]).astype(BF16)
        o = gf_ref[rs, :].astype(F32) + gb_ref[rs, :].astype(F32)
        parts = []
        for h in range(GDN_HEADS):
            oh = o[:, h * 128:(h + 1) * 128]
            parts.append(oh * lax.rsqrt(jnp.mean(oh * oh, axis=-1, keepdims=True) + EPS))
        yg_ref[rs, :] = (jnp.concatenate(parts, axis=1) * gnw_ref[...]
                         * _silu(gg_ref[rs, :].astype(F32))).astype(BF16)
    is_ctx = pl.program_id(0) < nctx_blk
    yh = jnp.where(is_ctx, yhc_ref[...], yhl_ref[...])
    m = (_sigmoid(g0_ref[...].astype(F32)) * _dot(yh, w0_ref[...])
         + _sigmoid(g1_ref[...].astype(F32)) * _dot(ys_ref[...], w1_ref[...])
         + _sigmoid(g2_ref[...].astype(F32)) * _dot(yg_ref[...], w2_ref[...]))
    x = jnp.where(is_ctx, xc_ref[...], xl_ref[...])
    o_ref[...] = x + mod_ref[0, 2:3, :] * _dot(m.astype(BF16), wo_ref[...])


def merge(rw, l, yh, y_f, y_b, dx, ssm_nw, o_f, o_b, gdn_nw, p, w0, w1, w2, wo, xs, mod, skip_ctx=False):
    R = rw.R
    D = xs[0].shape[1]
    tm = min(rw.tm, 512)
    mi = rw.mod_index(tm)
    skip = rw.NC // tm if skip_ctx else 0
    yspec = pl.BlockSpec((tm, 512), lambda i: (i + skip, 0))
    pspec = lambda col: pl.BlockSpec((tm, 512), lambda i: (i + skip, col // 512))
    vec = pl.BlockSpec((1, 512), lambda i: (0, 0))
    gspec = lambda k: pl.BlockSpec((tm, D), lambda i: (i + skip, C_GATE // D + k))
    wspec = pl.BlockSpec((None, 512, D), lambda i: (l, 0, 0))
    return pl.pallas_call(
        functools.partial(_merge_kernel, nctx_blk=rw.NC // tm - skip),
        grid=(R // tm - skip,),
        in_specs=_stream_specs(rw, tm, yh, 1, skip) + [
                  yspec, yspec, pspec(C_XBC), pspec(C_Z), vec, vec,
                  yspec, yspec, pspec(C_GG), vec,
                  gspec(0), gspec(1), gspec(2), wspec, wspec, wspec,
                  pl.BlockSpec((None, D, D), lambda i: (l, 0, 0))]
                 + _stream_specs(rw, tm, xs, 1, skip)
                 + [pl.BlockSpec((None, 1, 8, D), lambda i: (l, mi(i + skip), 0, 0))],
        out_specs=pl.BlockSpec((tm, D), lambda i: (i + skip, 0)),
        out_shape=jax.ShapeDtypeStruct((R, D), F32),
        scratch_shapes=[pltpu.VMEM((tm, 512), BF16), pltpu.VMEM((tm, 512), BF16)],
        compiler_params=_cp("arbitrary"),
        name="merge",
    )(yh[0], yh[1], y_f, y_b, p, p, dx, ssm_nw, o_f, o_b, p, gdn_nw, p, p, p, w0, w1, w2, wo, xs[0], xs[1], mod)


def _swiglu_up_kernel(x_ref, nw_ref, mod_ref, wg_ref, wu_ref, o_ref, h_ref, g0_ref, g1_ref, u0_ref, u1_ref):
    @pl.when(pl.program_id(1) == 0)
    def _():
        h = _norm_mod(x_ref[...], nw_ref[...], mod_ref[0, 4:5, :], mod_ref[0, 3:4, :])
        h_ref[...] = h.astype(BF16)

    T, tn = o_ref.shape
    rows = g0_ref.shape[0]
    gs, us = (g0_ref, g1_ref), (u0_ref, u1_ref)

    def project(r):
        hh = h_ref[r * rows:(r + 1) * rows, :]
        gs[r % 2][...] = _dot(hh, wg_ref[...])
        us[r % 2][...] = _dot(hh, wu_ref[...])

    def finish(r):
        for q in range(rows // 64):
            for c in range(tn // 128):
                ps = (slice(q * 64, (q + 1) * 64), slice(c * 128, (c + 1) * 128))
                y = _silu(gs[r % 2][ps]) * us[r % 2][ps]
                o_ref[r * rows + q * 64:r * rows + (q + 1) * 64, ps[1]] = y.astype(o_ref.dtype)

    for r in range(T // rows):
        project(r)
        if r > 0:
            finish(r - 1)
    finish(T // rows - 1)


def swiglu_up(rw, l, x, nw, mod, wgu, skip_ctx=False):
    R, D = x.shape
    tm = min(rw.tm, 512)
    tn = D_FF
    nj = D_FF // tn
    mi = rw.mod_index(tm)
    skip = rw.NC // tm if skip_ctx else 0
    return pl.pallas_call(
        _swiglu_up_kernel,
        grid=(R // tm - skip, nj),
        in_specs=[
            pl.BlockSpec((tm, D), lambda i, j: (i + skip, 0)),
            pl.BlockSpec((None, 1, D), lambda i, j: (l, 0, 0)),
            pl.BlockSpec((None, 1, 8, D), lambda i, j: (l, mi(i + skip), 0, 0)),
            pl.BlockSpec((None, D, tn), lambda i, j: (l, 0, j)),
            pl.BlockSpec((None, D, tn), lambda i, j: (l, 0, nj + j)),
        ],
        out_specs=pl.BlockSpec((tm, tn), lambda i, j: (i + skip, j)),
        out_shape=jax.ShapeDtypeStruct((R, D_FF), BF16),
        scratch_shapes=[pltpu.VMEM((tm, D), BF16)] + [pltpu.VMEM((min(256, tm), tn), F32)] * 4,
        compiler_params=_cp("arbitrary", "arbitrary"),
        name="swiglu_up",
    )(x, nw, mod, wgu, wgu)


def _swiglu_down_kernel(a_ref, w_ref, x_ref, mod_ref, o_ref):
    o_ref[...] = x_ref[...] + mod_ref[0, 5:6, :] * _dot(a_ref[...], w_ref[...])


def swiglu_down(rw, l, a, w, x, mod, skip_ctx=False):
    R, D = x.shape
    tm = min(rw.tm, 512)
    mi = rw.mod_index(tm)
    skip = rw.NC // tm if skip_ctx else 0
    return pl.pallas_call(
        _swiglu_down_kernel,
        grid=(R // tm - skip,),
        in_specs=[
            pl.BlockSpec((tm, D_FF), lambda i: (i + skip, 0)),
            pl.BlockSpec((None, D_FF, D), lambda i: (l, 0, 0)),
            pl.BlockSpec((tm, D), lambda i: (i + skip, 0)),
            pl.BlockSpec((None, 1, 8, D), lambda i: (l, mi(i + skip), 0, 0)),
        ],
        out_specs=pl.BlockSpec((tm, D), lambda i: (i + skip, 0)),
        out_shape=jax.ShapeDtypeStruct((R, D), F32),
        compiler_params=_cp("arbitrary"),
        name="swiglu_down",
    )(a, w, x, mod)


def _final_norm_kernel(x_ref, w_ref, o_ref):
    x = x_ref[...]
    ms = jnp.mean(x * x, axis=-1, keepdims=True)
    o_ref[...] = x * lax.rsqrt(ms + EPS) * w_ref[...]


def final_norm(rw, x, w):
    D = x.shape[1]
    tm = rw.tm
    n0 = rw.NC // tm
    nl = rw.B * rw.Ll
    return pl.pallas_call(
        _final_norm_kernel,
        grid=(nl // tm,),
        in_specs=[pl.BlockSpec((tm, D), lambda i: (n0 + i, 0)), pl.BlockSpec((1, D), lambda i: (0, 0))],
        out_specs=pl.BlockSpec((tm, D), lambda i: (i, 0)),
        out_shape=jax.ShapeDtypeStruct((nl, D), F32),
        compiler_params=_cp("arbitrary"),
        name="final_norm",
    )(x, w.reshape(1, D))


def _regroup_w_in(w_in):
    o_dt = 3072
    o_gdn = 3088
    o_a = o_gdn + 2048
    o_b = o_a + 8
    o_gate = o_gdn + 2064
    wt = jnp.swapaxes(w_in, 1, 2).astype(BF16)
    pieces = [
        wt[:, 0:3072],
        wt[:, o_gdn:o_gdn + 2048],
        wt[:, o_gate:o_gate + 3072],
        wt[:, o_dt:o_dt + 16],
        wt[:, o_a:o_a + 4], wt[:, o_b:o_b + 4],
        wt[:, o_a + 4:o_a + 8], wt[:, o_b + 4:o_b + 8],
        jnp.zeros((wt.shape[0], N_IN_PAD - C_SM - 32, wt.shape[2]), wt.dtype),
    ]
    return jnp.swapaxes(jnp.concatenate(pieces, axis=1), 1, 2)


def kernel(x, c, ctx, c_ctx, w_ada, b_ada, norm1_w, norm2_w, w_in, hy_conv_w, hy_conv_b, hy_w1, hy_b1, hy_w2, hy_b2, hy_w3, hy_freq, hy_bias, ssm_conv_w, ssm_conv_b, ssm_dt_bias, ssm_A_log, ssm_D, ssm_norm_w, gdn_conv_w, gdn_dt_bias, gdn_A_log, gdn_norm_w, w_hy_out, w_ssm_out, w_gdn_out, w_out, w_gate_up, w_down, final_norm_w):
    B, Ll, D = x.shape
    Lc = ctx.shape[1]
    depth = w_ada.shape[0]
    assert Lc == CONV_ROWS and D == D_MODEL and B <= 15
    rw = Rows(B, Lc, Ll)
    R, NC = rw.R, rw.NC

    xs = (ctx.reshape(B * Lc, D), x.reshape(B * Ll, D))

    svec = jnp.concatenate([c_ctx[None, :], c, jnp.zeros((15 - B, D), F32)], axis=0)
    mod = ada_modulation(svec, w_ada, b_ada)
    mod = jnp.pad(mod.reshape(depth, 16, 6, D), ((0, 0), (0, 0), (0, 2), (0, 0)))

    w_in_r = _regroup_w_in(w_in)
    par = _in_proj_params(hy_conv_w, hy_conv_b, ssm_conv_w, ssm_conv_b, gdn_conv_w, ssm_dt_bias, gdn_dt_bias,
                          gdn_A_log)
    norm1 = norm1_w.reshape(depth, 1, D)
    norm2 = norm2_w.reshape(depth, 1, D)
    w_hy_o, w_ssm_o, w_gdn_o, w_o = (w.astype(BF16) for w in (w_hy_out, w_ssm_out, w_gdn_out, w_out))
    w_gu, w_dn = w_gate_up.astype(BF16), w_down.astype(BF16)
    dft_l = dft_tables_split(Ll)
    dft_c = dft_tables_split(Lc)
    feat_l, feat_c = hy_features(Ll), hy_features(Lc)

    for l in range(depth):
        p, sm = in_proj(rw, xs, l, norm1, mod, w_in_r, par)

        sm32_t = sm[:, :32].T
        dt_t = sm32_t[:16].reshape(2, 8, R)
        g_t = sm32_t[16:32].reshape(16, R // GDN_CHUNK, GDN_CHUNK).transpose(1, 0, 2)

        alx = jnp.repeat(ssm_A_log[l], SSM_HEAD_DIM, axis=-1).reshape(2, 1, 512)
        alc = ssm_A_log[l].reshape(2, 8, 1)
        y_f, y_b = ssd_scan(rw, p, sm, dt_t, alx, alc)
        dx = jnp.repeat(ssm_D[l], SSM_HEAD_DIM).reshape(1, 512)

        o_f, o_b = gdn_scan(rw, *gdn_prep(p, sm, g_t))

        last = l == depth - 1
        hyu = p
        parts = []
        for (Bn, L, blk0, (fwd, inv), feat) in ((B, Lc, 0, dft_c, feat_c), (B, Ll, NC // Ll, dft_l, feat_l)):
            if NC % L:
                raise ValueError("latent length must divide the context row count")
            if last and blk0 == 0:
                parts.append(None)
                continue
            filt = hy_filter(feat, hy_w1[l], hy_b1[l], hy_w2[l], hy_b2[l], hy_w3[l], hy_freq[l])
            kspec = filter_spectrum(fwd, *filt)
            z1 = long_conv(Bn, L, hyu, blk0, 0, hyu, blk0, 1, hy_bias[l, 0], fwd, inv, kspec, 0, F32)
            yy = long_conv(Bn, L, z1, 0, 0, hyu, blk0, 2, hy_bias[l, 1], fwd, inv, kspec, 1, BF16)
            parts.append(yy)
        y_hy = (parts[1] if last else parts[0], parts[1])

        xa = merge(rw, l, y_hy, y_f, y_b, dx, ssm_norm_w[l].reshape(1, 512),
                   o_f, o_b, jnp.tile(gdn_norm_w[l], GDN_HEADS).reshape(1, 512), p,
                   w_hy_o, w_ssm_o, w_gdn_o, w_o, xs, mod, skip_ctx=last)
        act = swiglu_up(rw, l, xa, norm2, mod, w_gu, skip_ctx=last)
        xa = swiglu_down(rw, l, act, w_dn, xa, mod, skip_ctx=last)
        xs = (xa, xa)

    out = final_norm(rw, xa, final_norm_w)
    return out.reshape(B, Ll, D)
```

```python
import functools
import math

import jax
import jax.numpy as jnp
import numpy as np
from jax import lax
from jax.experimental import pallas as pl
from jax.experimental.pallas import tpu as pltpu

F32 = jnp.float32
BF16 = jnp.bfloat16
HI = lax.Precision.HIGHEST

EPS = 1e-6
D_MODEL = 1024
GRID_W = 64

HY_WIDTH = 512
HY_BANDS = 16
HY_EMB = 1 + 2 * HY_BANDS
HY_HIDDEN = 64
HY_SHORT_DECAY_PCT = 0.3
HY_LONG_DECAY_PCT = 1.5
HY_TARGET = 1e-2

SSM_HEADS = 8
SSM_HEAD_DIM = 64
SSM_WIDTH = 512
SSM_GROUPS = 2
SSM_HPG = 4
SSM_STATE = 128
SSM_CHUNK = 128
SSM_GW = SSM_HPG * SSM_HEAD_DIM

GDN_HEADS = 4
GDN_DK = 128
GDN_DV = 128
GDN_CHUNK = 64

D_FF = 2816

C_HY = 0
C_Z = 1536
C_XBC = 2048
C_QKV = 3072
C_GG = 4608
C_GATE = 5120
C_SM = 8192

CONV_ROWS = 256
FREQ_BLK = 256

VMEM_LIMIT = 56 * 1024 * 1024


def _cp(*sem, flags=None):
    return pltpu.CompilerParams(dimension_semantics=sem, vmem_limit_bytes=VMEM_LIMIT, flags=flags)


def _sigmoid(x):
    return 1.0 / (1.0 + jnp.exp(-x))


def _silu(x):
    return x * _sigmoid(x)


def _softplus(x):
    return jnp.maximum(x, 0.0) + jnp.log1p(jnp.exp(-jnp.abs(x)))


def _dot(a, b, precision=None):
    return jnp.dot(a, b, precision=precision, preferred_element_type=F32)


def _dot_nt(a, b):
    return lax.dot_general(a, b, (((1,), (1,)), ((), ())), preferred_element_type=F32)


def _dot_tn(a, b):
    return lax.dot_general(a, b, (((0,), (0,)), ((), ())), preferred_element_type=F32)


def _ada_kernel(s_ref, w_ref, b_ref, o_ref):
    s = _silu(s_ref[...])
    o_ref[0] = _dot(s, w_ref[0], HI) + b_ref[0]


def ada_modulation(svec, w_ada, b_ada):
    depth = w_ada.shape[0]
    D = D_MODEL
    return pl.pallas_call(
        _ada_kernel,
        grid=(depth, 6),
        in_specs=[
            pl.BlockSpec((16, D), lambda l, j: (0, 0)),
            pl.BlockSpec((1, D, D), lambda l, j: (l, 0, j)),
            pl.BlockSpec((1, 1, D), lambda l, j: (l, 0, j)),
        ],
        out_specs=pl.BlockSpec((1, 16, D), lambda l, j: (l, 0, j)),
        out_shape=jax.ShapeDtypeStruct((depth, 16, 6 * D), F32),
        compiler_params=_cp("arbitrary", "arbitrary"),
        name="ada",
    )(svec, w_ada, b_ada.reshape(depth, 1, 6 * D))


def _norm_mod(x, nw, scale, shift):
    ms = jnp.mean(x * x, axis=-1, keepdims=True)
    return (x * lax.rsqrt(ms + EPS) * nw) * (1.0 + scale) + shift


IN_FLIGHT = 6
N_IN_PAD = C_SM + 128
IN_TN = N_IN_PAD // 5
MODE_RAW, MODE_CONV, MODE_CONV_SILU, MODE_CONV_SILU_L2, MODE_SMALL = range(5)


def _tile_mode(tile):
    col = tile * 128
    if col < C_Z:
        return MODE_CONV
    if col < C_XBC:
        return MODE_RAW
    if col < C_QKV:
        return MODE_CONV_SILU
    if col < C_QKV + 1024:
        return MODE_CONV_SILU_L2
    if col < C_GG:
        return MODE_CONV_SILU
    if col < C_SM:
        return MODE_RAW
    return MODE_SMALL
PAR_W0, PAR_W1, PAR_W2, PAR_BIAS, PAR_L2SCALE, PAR_SBIAS, PAR_SALOG, PAR_SKIND = range(8)


def _in_proj_kernel(xc_ref, xl_ref, nw_ref, mod_ref, w_ref, par_ref, o_ref, sm_ref, h_ref, raw0_ref, raw1_ref, *,
                    nctx_blk):
    j = pl.program_id(1)
    nj = N_IN_PAD // IN_TN
    raws = (raw0_ref, raw1_ref)

    @pl.when((j == 0) & (pl.program_id(0) < nctx_blk))
    def _():
        h = _norm_mod(xc_ref[...], nw_ref[...], mod_ref[0, 1:2, :], mod_ref[0, 0:1, :])
        h_ref[...] = h.astype(BF16)

    @pl.when((j == 0) & (pl.program_id(0) >= nctx_blk))
    def _():
        h = _norm_mod(xl_ref[...], nw_ref[...], mod_ref[0, 1:2, :], mod_ref[0, 0:1, :])
        h_ref[...] = h.astype(BF16)

    T = h_ref.shape[0]
    G = GRID_W
    per_ctx = CONV_ROWS // G
    is_latent = pl.program_id(0) >= nctx_blk
    sub = lax.broadcasted_iota(jnp.int32, (8, 128), 0)

    def raw_piece(src, g, c):
        return src[g * G:(g + 1) * G, c * 128:(c + 1) * 128]

    retired = []

    def retire(y):
        bits = pltpu.bitcast(y[0:8], jnp.int32)
        zero = lax.shift_right_logical(lax.shift_right_logical(bits, 16), 16)
        retired.append(jnp.tile(zero.astype(F32), (G // 8, 1)))

    def conv(src, g, c):
        cs = slice(c * 128, (c + 1) * 128)
        x = raw_piece(src, g, c)
        if len(retired) >= IN_FLIGHT:
            x = x + retired[-IN_FLIGHT]
        zero = jnp.zeros((1, 128), F32)
        before = zero if g % per_ctx == 0 else jnp.where(is_latent, 0.0, src[g * G - 1:g * G, cs])
        after = zero if g % per_ctx == per_ctx - 1 else jnp.where(is_latent, 0.0, src[(g + 1) * G:(g + 1) * G + 1, cs])
        rp = pltpu.roll(x, 1, 0)
        rn = pltpu.roll(x, G - 1, 0)
        prev = jnp.concatenate([jnp.where(sub == 0, before, rp[0:8]), rp[8:]], axis=0)
        nxt = jnp.concatenate([rn[:G - 8], jnp.where(sub == 7, after, rn[G - 8:])], axis=0)
        return (prev * par_ref[PAR_W0:PAR_W0 + 1, cs] + x * par_ref[PAR_W1:PAR_W1 + 1, cs]
                + nxt * par_ref[PAR_W2:PAR_W2 + 1, cs] + par_ref[PAR_BIAS:PAR_BIAS + 1, cs])

    def conv_silu(src, g, c):
        return _silu(conv(src, g, c))

    def conv_silu_l2(src, g, c):
        y = _silu(conv(src, g, c))
        y = y * lax.rsqrt(jnp.sum(y * y, axis=-1, keepdims=True) + EPS)
        return y * par_ref[PAR_L2SCALE:PAR_L2SCALE + 1, c * 128:(c + 1) * 128]

    def small(src, g, c):
        cs = slice(c * 128, (c + 1) * 128)
        acc = raw_piece(src, g, c)
        kind = par_ref[PAR_SKIND:PAR_SKIND + 1, cs]
        sp = _softplus(acc + par_ref[PAR_SBIAS:PAR_SBIAS + 1, cs])
        dec = -jnp.exp(par_ref[PAR_SALOG:PAR_SALOG + 1, cs]) * sp
        return jnp.where(kind == 0.0, sp, jnp.where(kind == 1.0, dec, jnp.where(kind == 2.0, _sigmoid(acc), 0.0)))

    rows_mm = 256
    tiles = IN_TN // 128
    piece_fn = {MODE_RAW: raw_piece, MODE_CONV: conv, MODE_CONV_SILU: conv_silu,
                MODE_CONV_SILU_L2: conv_silu_l2, MODE_SMALL: small}

    heavy_modes = (MODE_CONV_SILU, MODE_CONV_SILU_L2)
    col_slices = [slice(c0, min(c0 + 256, IN_TN)) for c0 in range(0, IN_TN, 256)]

    def project(dst, r, cs=slice(None)):
        rs = slice(r * rows_mm, (r + 1) * rows_mm)
        dst[rs, cs] = _dot(h_ref[rs, :], w_ref[:, cs])

    def finish(src, blk, g, c):
        mode = _tile_mode(blk * tiles + c)
        y = piece_fn[mode](src, g, c)
        if mode in heavy_modes:
            retire(y)
        if mode == MODE_SMALL:
            sm_ref[g * G:(g + 1) * G, :] = y
            y = jnp.zeros_like(y)
        o_ref[g * G:(g + 1) * G, c * 128:(c + 1) * 128] = y.astype(o_ref.dtype)

    for step in range(nj + 1):
        @pl.when(j == step)
        def _(step=step):
            blk = step - 1
            src, dst = raws[blk % 2], raws[step % 2]
            retired.clear()
            light = blk >= 0 and not any(_tile_mode(blk * tiles + c) in heavy_modes for c in range(tiles))
            for r in range(T // rows_mm):
                pieces = [] if blk < 0 else [(g, c) for g in range(r * rows_mm // G, (r + 1) * rows_mm // G)
                                             for c in range(tiles)]
                if step == nj:
                    slabs = []
                elif light:
                    slabs = col_slices
                else:
                    slabs = [slice(None)]
                per = -(-len(pieces) // max(len(slabs), 1))
                for n in range(max(len(slabs), 1)):
                    if n < len(slabs):
                        project(dst, r, slabs[n])
                    for g, c in pieces[n * per:(n + 1) * per]:
                        finish(src, blk, g, c)


class Rows:
    def __init__(self, B, Lc, Ll):
        self.B, self.Lc, self.Ll = B, Lc, Ll
        self.NC = B * Lc
        self.R = B * Lc + B * Ll
        assert self.NC % Ll == 0 or Ll % self.NC == 0
        tm = 1024
        while self.NC % tm or Ll % tm:
            tm //= 2
        self.tm = tm

    def mod_index(self, tm):
        nctx = self.NC // tm
        per = self.Ll // tm
        return lambda i: jnp.where(i < nctx, 0, 1 + (i - nctx) // per)


def _stream_specs(rw, tm, xs, ngrid, skip=0):
    xc, xl = xs
    nctx = rw.NC // tm
    off = nctx if xl.shape[0] == rw.R else 0
    D = xc.shape[1]
    if ngrid == 1:
        return [pl.BlockSpec((tm, D), lambda i: (jnp.minimum(i + skip, nctx - 1), 0)),
                pl.BlockSpec((tm, D), lambda i: (jnp.maximum(i + skip - nctx, 0) + off, 0))]
    return [pl.BlockSpec((tm, D), lambda i, j: (jnp.minimum(i + skip, nctx - 1), 0)),
            pl.BlockSpec((tm, D), lambda i, j: (jnp.maximum(i + skip - nctx, 0) + off, 0))]


def in_proj(rw, xs, l, nw, mod, w, par):
    R = rw.R
    D = xs[0].shape[1]
    N = w.shape[2]
    tm, tn = rw.tm, IN_TN
    nj = N // tn
    assert N == N_IN_PAD
    mi = rw.mod_index(tm)
    done = lambda j: jnp.maximum(j - 1, 0)
    return pl.pallas_call(
        functools.partial(_in_proj_kernel, nctx_blk=rw.NC // tm),
        grid=(R // tm, nj + 1),
        in_specs=_stream_specs(rw, tm, xs, 2) + [
            pl.BlockSpec((None, 1, D), lambda i, j: (l, 0, 0)),
            pl.BlockSpec((None, 1, 8, D), lambda i, j: (l, mi(i), 0, 0)),
            pl.BlockSpec((None, D, tn), lambda i, j: (l, 0, jnp.minimum(j, nj - 1))),
            pl.BlockSpec((None, 8, tn), lambda i, j: (l, 0, done(j))),
        ],
        out_specs=[pl.BlockSpec((tm, tn), lambda i, j: (i, done(j))),
                   pl.BlockSpec((tm, 128), lambda i, j: (i, 0))],
        out_shape=[jax.ShapeDtypeStruct((R, N), BF16), jax.ShapeDtypeStruct((R, 128), F32)],
        scratch_shapes=[pltpu.VMEM((tm, D), BF16), pltpu.VMEM((tm, tn), F32), pltpu.VMEM((tm, tn), F32)],
        compiler_params=_cp("arbitrary", "arbitrary"),
        name="in_proj",
    )(xs[0], xs[1], nw, mod, w, par)


def _in_proj_params(hy_conv_w, hy_conv_b, ssm_conv_w, ssm_conv_b, gdn_conv_w, ssm_dt_bias, gdn_dt_bias, gdn_A_log):
    depth = hy_conv_w.shape[0]

    def row(pieces):
        out, pos = [], 0
        for off, a in pieces:
            out += [jnp.zeros((depth, off - pos), F32), a.astype(F32)]
            pos = off + a.shape[1]
        return jnp.concatenate(out + [jnp.zeros((depth, N_IN_PAD - pos), F32)], axis=1)
    z4 = jnp.zeros((depth, 4), F32)
    conv = [row([(C_HY, hy_conv_w[:, t]), (C_XBC, ssm_conv_w[:, t]), (C_QKV, gdn_conv_w[:, t])]) for t in range(3)]
    bias = row([(C_HY, hy_conv_b), (C_XBC, ssm_conv_b)])
    l2s = row([(C_QKV, jnp.full((depth, 512), GDN_DK ** -0.5, F32)), (C_QKV + 512, jnp.ones((depth, 512), F32))])
    sbias = row([(C_SM, jnp.concatenate([ssm_dt_bias.reshape(depth, 16), gdn_dt_bias[:, 0], z4,
                                         gdn_dt_bias[:, 1], z4], axis=1))])
    salog = row([(C_SM + 16, jnp.concatenate([gdn_A_log[:, 0], z4, gdn_A_log[:, 1], z4], axis=1))])
    kind = np.full((depth, N_IN_PAD), 3.0, np.float32)
    kind[:, C_SM:C_SM + 16] = 0.0
    kind[:, C_SM + 16:C_SM + 20] = 1.0
    kind[:, C_SM + 24:C_SM + 28] = 1.0
    kind[:, C_SM + 20:C_SM + 24] = 2.0
    kind[:, C_SM + 28:C_SM + 32] = 2.0
    return jnp.stack(conv + [bias, l2s, sbias, salog, jnp.asarray(kind)], axis=1)


def _hy_filter_kernel(z_ref, w1_ref, b1_ref, w2_ref, b2_ref, w3_ref, f0_ref, f1_ref, win_ref, oe_ref, oo_ref,
                      h_ref, split_ref):
    @pl.when(pl.program_id(1) == 0)
    def _():
        h1 = jnp.sin(f0_ref[...] * (_dot(z_ref[...], w1_ref[...], HI) + b1_ref[...]))
        h_ref[...] = jnp.sin(f1_ref[...] * (_dot(h1, w2_ref[...], HI) + b2_ref[...]))

    a1 = h_ref[...].astype(BF16)
    a2 = (h_ref[...] - a1.astype(F32)).astype(BF16)
    b1 = w3_ref[...].astype(BF16)
    b2 = (w3_ref[...] - b1.astype(F32)).astype(BF16)
    h = (_dot(a1, b1) + _dot(a1, b2) + _dot(a2, b1)) * win_ref[...]
    tl = h.shape[0]
    row = lax.broadcasted_iota(jnp.int32, (tl, 1), 0) + pl.program_id(0) * tl
    drop = (row == 0) & (pl.program_id(1) % 2 == 1)
    h = jnp.where(drop, 0.0, h)
    for c in range(h.shape[1] // 128):
        cs = slice(c * 128, (c + 1) * 128)
        s_c = split_ref.at[c]
        s_c[...] = h[:, cs]
        oe_ref[:, cs] = s_c[pl.ds(0, tl // 2, stride=2), :].astype(oe_ref.dtype)
        oo_ref[:, cs] = s_c[pl.ds(1, tl // 2, stride=2), :].astype(oo_ref.dtype)


def hy_features(L):
    t = jnp.linspace(0.0, 1.0, L, dtype=F32)[:, None]
    w = 2.0 * math.pi * jnp.arange(L, dtype=F32)[:, None] / L
    f = jnp.linspace(1e-4, HY_BANDS - 1, HY_BANDS, dtype=F32)[None, :]
    z = jnp.concatenate([t, jnp.cos(f * w), -jnp.sin(f * w)], axis=-1)
    z = jnp.pad(z, ((0, 0), (0, 128 - HY_EMB)))
    min_decay = math.log(HY_TARGET) / HY_LONG_DECAY_PCT
    max_decay = math.log(HY_TARGET) / HY_SHORT_DECAY_PCT
    deltas = jnp.linspace(min_decay, max_decay, HY_WIDTH, dtype=F32)
    window = jnp.exp(-t * jnp.abs(deltas))
    return z, window


def hy_filter(feat, w1, b1, w2, b2, w3, freq):
    z, window = feat
    L = z.shape[0]
    H = HY_HIDDEN
    w1p = jnp.pad(w1, ((0, 128 - HY_EMB), (0, 128 - H)))
    w2p = jnp.pad(w2, ((0, 128 - H), (0, 128 - H)))
    w3p = jnp.pad(w3, ((0, 128 - H), (0, 0)))
    pad1 = lambda v: jnp.pad(v, (0, 128 - H)).reshape(1, 128)
    tl = 256
    full = lambda shape: pl.BlockSpec(shape, lambda i, j: (0, 0))
    return pl.pallas_call(
        _hy_filter_kernel,
        grid=(L // tl, 4),
        in_specs=[
            pl.BlockSpec((tl, 128), lambda i, j: (i, 0)),
            full((128, 128)), full((1, 128)), full((128, 128)), full((1, 128)),
            pl.BlockSpec((128, HY_WIDTH), lambda i, j: (0, j)),
            full((1, 128)), full((1, 128)),
            pl.BlockSpec((tl, HY_WIDTH), lambda i, j: (i, 0)),
        ],
        out_specs=[pl.BlockSpec((tl // 2, HY_WIDTH), lambda i, j: (i, j))] * 2,
        out_shape=[jax.ShapeDtypeStruct((L // 2, 4 * HY_WIDTH), BF16)] * 2,
        scratch_shapes=[pltpu.VMEM((tl, 128), F32), pltpu.VMEM((HY_WIDTH // 128, tl, 128), F32)],
        compiler_params=_cp("arbitrary", "arbitrary"),
        name="hy_filter",
    )(z, w1p, pad1(b1), w2p, pad1(b2), w3p, pad1(freq[0]), pad1(freq[1]), window)


def dft_tables_split(L):
    N = 2 * L
    H = L // 2
    q = np.arange(H, dtype=np.int64)[:, None]
    m = np.arange(H, dtype=np.int64)[None, :]
    ang_e = ((q * 2 * m) % N).astype(np.float64) * (2.0 * math.pi / N)
    ang_o = ((q * (2 * m + 1)) % N).astype(np.float64) * (2.0 * math.pi / N)
    alt = (1 - 2 * (m % 2)).astype(np.float64)
    ce, co = np.cos(ang_e), np.cos(ang_o)
    se = np.where(q == 0, alt, -np.sin(ang_e))
    so = np.where(q == 0, -alt, -np.sin(ang_o))
    w = np.where(q == 0, 1.0, 2.0) / N
    ise = np.where(q == 0, 2.0 / N * alt, -np.sin(ang_e) * w)
    iso = np.where(q == 0, -2.0 / N * alt, -np.sin(ang_o) * w)
    fwd = np.stack([ce, co, se, so])
    inv = np.stack([(ce * w).T, ise.T, (co * w).T, iso.T])
    return jnp.asarray(fwd, dtype=BF16), jnp.asarray(inv, dtype=BF16)


def _filter_spectrum_kernel(fwd_ref, he_ref, ho_ref, rlo_ref, rhi_ref, ilo_ref, ihi_ref):
    he, ho = he_ref[...], ho_ref[...]
    ae, ao = _dot(fwd_ref[0], he), _dot(fwd_ref[1], ho)
    be, bo = _dot(fwd_ref[2], he), _dot(fwd_ref[3], ho)
    rlo_ref[...] = ae + ao
    rhi_ref[...] = ae - ao
    first = (lax.broadcasted_iota(jnp.int32, (fwd_ref.shape[1], 1), 0) == 0) & (pl.program_id(0) == 0)
    ilo_ref[...] = jnp.where(first, be, be + bo)
    ihi_ref[...] = jnp.where(first, bo, bo - be)


def filter_spectrum(fwd, taps_even, taps_odd):
    H, N = taps_even.shape
    C = HY_WIDTH
    FB = min(FREQ_BLK, H)
    tap = pl.BlockSpec((H, C), lambda f, j: (0, j))
    out = pl.BlockSpec((FB, C), lambda f, j: (f, j))
    return pl.pallas_call(
        _filter_spectrum_kernel,
        grid=(H // FB, N // C),
        in_specs=[pl.BlockSpec((4, FB, H), lambda f, j: (0, f, 0)), tap, tap],
        out_specs=[out] * 4,
        out_shape=[jax.ShapeDtypeStruct((H, N), F32)] * 4,
        compiler_params=_cp("arbitrary", "arbitrary"),
        name="filter_spectrum",
    )(fwd, taps_even, taps_odd)


def _long_conv_kernel(u_ref, g_ref, bias_ref, fwd_ref, inv_ref, ar0_ref, ar1_ref, ar0h_ref, ar1h_ref,
                      ai0_ref, ai1_ref, ai0h_ref, ai1h_ref, o_ref, ue_ref, uo_ref, acce_ref, acco_ref, y_ref):
    f = pl.program_id(1)
    half = ue_ref.shape[0]

    lane_tiles = [slice(c * 128, (c + 1) * 128) for c in range(y_ref.shape[0])]

    @pl.when(f == 0)
    def _():
        for c, cs in enumerate(lane_tiles):
            y_c = y_ref.at[c]
            y_c[...] = u_ref[:, cs].astype(F32)
            ue_ref[:, cs] = y_c[pl.ds(0, half, stride=2), :].astype(BF16)
            uo_ref[:, cs] = y_c[pl.ds(1, half, stride=2), :].astype(BF16)
        acce_ref[...] = jnp.zeros_like(acce_ref)
        acco_ref[...] = jnp.zeros_like(acco_ref)

    ue, uo = ue_ref[...], uo_ref[...]
    ae, ao = _dot(fwd_ref[0], ue), _dot(fwd_ref[1], uo)
    be, bo = _dot(fwd_ref[2], ue), _dot(fwd_ref[3], uo)
    ur, ur2 = ae + ao, ae - ao
    ui, ui2 = be + bo, bo - be
    first = (lax.broadcasted_iota(jnp.int32, (fwd_ref.shape[1], 1), 0) == 0) & (f == 0)
    kr, kr2 = ar0_ref[...] + ar1_ref[...], ar0h_ref[...] + ar1h_ref[...]
    ki = jnp.where(first, ai0_ref[...] + ai1_ref[...], ai0_ref[...] - ai1_ref[...])
    ki2 = ai0h_ref[...] - ai1h_ref[...]
    pr, pi = ur * kr - ui * ki, ur * ki + ui * kr
    pr2, pi2 = ur2 * kr2 - ui2 * ki2, ur2 * ki2 + ui2 * kr2
    dc, ny = ur * kr, ur2 * kr2
    gr = jnp.where(first, dc + ny, pr + pr2)
    gi = jnp.where(first, be * ki - bo * ki2, pi - pi2)
    hr = jnp.where(first, dc - ny, pr - pr2)
    hi = jnp.where(first, be * ki2 + bo * ki, pi + pi2)
    acce_ref[...] += _dot(inv_ref[0], gr.astype(BF16)) + _dot(inv_ref[1], gi.astype(BF16))
    acco_ref[...] += _dot(inv_ref[2], hr.astype(BF16)) + _dot(inv_ref[3], hi.astype(BF16))

    @pl.when(f == pl.num_programs(1) - 1)
    def _():
        for c, cs in enumerate(lane_tiles):
            y_c = y_ref.at[c]
            y_c[pl.ds(0, half, stride=2), :] = acce_ref[:, cs]
            y_c[pl.ds(1, half, stride=2), :] = acco_ref[:, cs]
            u = u_ref[:, cs].astype(F32)
            o_ref[:, cs] = (g_ref[:, cs].astype(F32) * (y_c[...] + u * bias_ref[:, cs])).astype(o_ref.dtype)


def long_conv(B, L, u, u_rb0, u_cb, gate, g_rb0, gate_cb, bias, fwd, inv, kspec, order, out_dtype):
    C = HY_WIDTH
    H = L // 2
    FB = min(FREQ_BLK, H)
    nfb = H // FB
    kblk = lambda part, d: pl.BlockSpec((FB, C), lambda b, f: (f, 2 * order + d))
    kops = [kspec[part] for part in range(4) for _ in range(2)]
    return pl.pallas_call(
        _long_conv_kernel,
        grid=(B, nfb),
        in_specs=[
            pl.BlockSpec((L, C), lambda b, f: (u_rb0 + b, u_cb)),
            pl.BlockSpec((L, C), lambda b, f: (g_rb0 + b, gate_cb)),
            pl.BlockSpec((1, C), lambda b, f: (0, 0)),
            pl.BlockSpec((4, FB, H), lambda b, f: (0, f, 0)),
            pl.BlockSpec((4, H, FB), lambda b, f: (0, 0, f)),
            kblk(0, 0), kblk(0, 1), kblk(1, 0), kblk(1, 1), kblk(2, 0), kblk(2, 1), kblk(3, 0), kblk(3, 1),
        ],
        out_specs=pl.BlockSpec((L, C), lambda b, f: (b, 0)),
        out_shape=jax.ShapeDtypeStruct((B * L, C), out_dtype),
        scratch_shapes=[pltpu.VMEM((H, C), BF16), pltpu.VMEM((H, C), BF16),
                        pltpu.VMEM((H, C), F32), pltpu.VMEM((H, C), F32), pltpu.VMEM((C // 128, L, 128), F32)],
        compiler_params=_cp("arbitrary", "arbitrary"),
        name="long_conv",
    )(u, gate, bias.reshape(1, C), fwd, inv, *kops)


def _scan_blocks(rw, rows):
    nbc, nbl, base = rw.Lc // rows, rw.Ll // rows, rw.NC // rows

    def make(d):
        def f(b, s):
            jc = s if d == 0 else nbc - 1 - s
            jl = (s - nbc) if d == 0 else nbl - 1 - (s - nbc)
            return jnp.where(s < nbc, b * nbc + jc, base + b * nbl + jl)
        return f

    return [make(0), make(1)], nbc + nbl


def _expand_lanes(x, base, n, width):
    rows = x.shape[0]
    per = 128 // width
    lane = lax.broadcasted_iota(jnp.int32, (rows, 128), 1)
    tiles = []
    for t in range(n // per):
        c0 = base + t * per
        tile = jnp.broadcast_to(x[:, c0:c0 + 1], (rows, 128))
        for i in range(1, per):
            tile = jnp.where(lane >= i * width, jnp.broadcast_to(x[:, c0 + i:c0 + i + 1], (rows, 128)), tile)
        tiles.append(tile)
    return jnp.concatenate(tiles, axis=1)


def _ssd_kernel(xf, bf, cf, smf, dtf, xb, bb, cb_, smb, dtb, alx_ref, alc_ref, of_ref, ob_ref, h_ref):
    Q = SSM_CHUNK
    GW = SSM_GW

    @pl.when(pl.program_id(1) == 0)
    def _():
        h_ref[...] = jnp.zeros_like(h_ref)

    row = lax.broadcasted_iota(jnp.int32, (Q, Q), 0)
    col = lax.broadcasted_iota(jnp.int32, (Q, Q), 1)
    lane_head = lax.broadcasted_iota(jnp.int32, (Q, GW), 1) // SSM_HEAD_DIM
    dirs = ((xf, bf, cf, smf, dtf, of_ref), (xb, bb, cb_, smb, dtb, ob_ref))
    jobs = []
    for d in range(2):
        x_ref, b_ref, c_ref, sm_ref, dt_ref, o_ref = dirs[d]
        keep = (col <= row) if d == 0 else (col >= row)
        tri = keep.astype(BF16)
        tri_t = ((row <= col) if d == 0 else (row >= col)).astype(BF16)
        sm = sm_ref[...]
        a_x = -jnp.exp(alx_ref[d])
        dtx = _expand_lanes(sm, 8 * d, SSM_HEADS, SSM_HEAD_DIM)
        cumx = _expand_lanes(_dot_01_lhs(tri, sm), 8 * d, SSM_HEADS, SSM_HEAD_DIM) * a_x
        cumr = _dot_01_rhs(dt_ref[0], tri_t) * (-jnp.exp(alc_ref[d]))
        last = Q - 1 if d == 0 else 0
        totx = cumx[last:last + 1, :]
        xd = x_ref[...].astype(F32) * dtx
        xdw = xd * jnp.exp(totx - cumx)
        ecum = jnp.exp(cumx)
        for g in range(SSM_GROUPS):
            gs = slice(g * GW, (g + 1) * GW)
            jobs.append(dict(d=d, g=g, gs=gs, keep=keep, cumx=cumx, cumr=cumr, o_ref=o_ref,
                             bg=b_ref[:, g * SSM_STATE:(g + 1) * SSM_STATE].astype(BF16),
                             cg=c_ref[:, g * SSM_STATE:(g + 1) * SSM_STATE].astype(BF16),
                             xdg=xd[:, gs], xdw=xdw[:, gs].astype(BF16), ecum=ecum[:, gs],
                             etot=jnp.exp(totx[:, gs])))
    for j in jobs:
        j["cb"] = _dot_nt(j["cg"], j["bg"])
        j["h"] = h_ref[j["d"], j["g"]]
    for j in jobs:
        ms, xs = [], []
        for e4 in range(SSM_HPG):
            e = j["g"] * SSM_HPG + e4
            diff = j["cumx"][:, e * SSM_HEAD_DIM:e * SSM_HEAD_DIM + 1] - j["cumr"][e:e + 1, :]
            ms.append((j["cb"] * jnp.where(j["keep"], jnp.exp(diff), 0.0)).astype(BF16))
            xs.append(jnp.where(lane_head == e4, j["xdg"], 0.0).astype(BF16))
        yd = _dot(jnp.concatenate(ms, axis=1), jnp.concatenate(xs, axis=0))
        y_off = _dot(j["cg"], j["h"].astype(BF16)) * j["ecum"]
        j["o_ref"][:, j["gs"]] = (yd + y_off).astype(BF16)
    for j in jobs:
        h_ref[j["d"], j["g"]] = j["h"] * j["etot"] + _dot_tn(j["bg"], j["xdw"])


def ssd_scan(rw, p, sm, dtT, alx, alc):
    Q = SSM_CHUNK
    blks, nsteps = _scan_blocks(rw, Q)
    R = p.shape[0]
    in_specs = []
    for d in range(2):
        f = blks[d]
        in_specs += [
            pl.BlockSpec((Q, 512), lambda b, s, f=f: (f(b, s), C_XBC // 512)),
            pl.BlockSpec((Q, 256), lambda b, s, f=f: (f(b, s), C_XBC // 256 + 2)),
            pl.BlockSpec((Q, 256), lambda b, s, f=f: (f(b, s), C_XBC // 256 + 3)),
            pl.BlockSpec((Q, 128), lambda b, s, f=f: (f(b, s), 0)),
            pl.BlockSpec((1, 8, Q), lambda b, s, f=f, d=d: (d, 0, f(b, s))),
        ]
    in_specs += [pl.BlockSpec((2, 1, 512), lambda b, s: (0, 0, 0)), pl.BlockSpec((2, 8, 1), lambda b, s: (0, 0, 0))]
    ops = (p, p, p, sm, dtT)
    return pl.pallas_call(
        _ssd_kernel,
        grid=(rw.B, nsteps),
        in_specs=in_specs,
        out_specs=[pl.BlockSpec((Q, 512), lambda b, s, f=blks[d]: (f(b, s), 0)) for d in range(2)],
        out_shape=[jax.ShapeDtypeStruct((R, 512), BF16)] * 2,
        scratch_shapes=[pltpu.VMEM((2, SSM_GROUPS, SSM_STATE, SSM_GW), F32)],
        compiler_params=_cp("arbitrary", "arbitrary"),
        name="ssd_scan",
    )(*ops, *ops, alx, alc)


def _split3(x):
    x1 = x.astype(BF16)
    r = x - x1.astype(F32)
    x2 = r.astype(BF16)
    x3 = (r - x2.astype(F32)).astype(BF16)
    return x1, x2, x3


def _dot_01_lhs(m01, x):
    x1, x2, x3 = _split3(x)
    return _dot(m01, x1) + _dot(m01, x2) + _dot(m01, x3)


def _dot_01_rhs(x, m01):
    x1, x2, x3 = _split3(x)
    return _dot(x1, m01) + _dot(x2, m01) + _dot(x3, m01)


GDN_ROWS = 256


def _gdn_prep_kernel(q_ref, k_ref, v_ref, sm_ref, gT_ref, u_ref, w_ref, qg_ref, kd_ref, qk_ref, egl_ref):
    C = GDN_CHUNK
    row = lax.broadcasted_iota(jnp.int32, (C, C), 0)
    col = lax.broadcasted_iota(jnp.int32, (C, C), 1)
    lane2 = lax.broadcasted_iota(jnp.int32, (C, 2 * C), 1)
    row2 = lax.broadcasted_iota(jnp.int32, (C, 2 * C), 0)
    col2 = lane2 & (C - 1)
    left = lane2 < C
    lane_k = lax.broadcasted_iota(jnp.int32, (C, 2 * GDN_DK), 1) < GDN_DK
    jobs = []
    for d in range(2):
        keep = (col <= row) if d == 0 else (col >= row)
        tri = keep.astype(BF16)
        tri_t2 = ((row2 <= col2) if d == 0 else (row2 >= col2)).astype(BF16)
        last = C - 1 if d == 0 else 0
        for c in range(GDN_ROWS // C):
            rows = slice(c * C, (c + 1) * C)
            smc = sm_ref[rows, :]
            cums = _dot_01_lhs(tri, smc)
            cumr2 = _dot_01_rhs(gT_ref[c, 8 * d:8 * d + 8, :], tri_t2)
            tot = cums[last:last + 1, :]
            for h in range(GDN_HEADS):
                lg = 16 + 8 * d + h
                jobs.append(dict(d=d, c=c, h=h, rows=rows, hs=slice(h * 128, (h + 1) * 128), keep=keep,
                                 gc=cums[:, lg:lg + 1], beta=smc[:, lg + 4:lg + 5],
                                 gl=tot[:, lg:lg + 1], gr=cumr2[h:h + 1, :C], gr2=cumr2[h:h + 1, :]))

    def block_diag(x):
        return jnp.concatenate([jnp.where(left, x, 0.0), jnp.where(left, 0.0, x)], axis=0).astype(BF16)

    pairs = []
    for i in range(0, len(jobs), 2):
        j0, j1 = jobs[i], jobs[i + 1]
        d, rows = j0["d"], j0["rows"]
        ps = slice(j0["hs"].start, j1["hs"].stop)
        keep2 = (col2 <= row2) if d == 0 else (col2 >= row2)
        late2, early2 = (row2, col2) if d == 0 else (col2, row2)
        lev2 = [(((row2 ^ col2) >> (t + 1)) == 0) & ((late2 & (1 << t)) != 0) & ((early2 & (1 << t)) == 0)
                for t in range(6)]
        gcp = jnp.where(left, j0["gc"], j1["gc"])
        grp = jnp.where(left[0:1], j0["gr2"], j1["gr2"])
        decp = jnp.where(keep2, jnp.exp(gcp - grp), 0.0)
        kp = k_ref[rows, ps].astype(F32)
        kbp = kp * jnp.where(lane_k, j0["beta"], j1["beta"])
        kstack = jnp.concatenate([jnp.where(lane_k, kp, 0.0), jnp.where(lane_k, 0.0, kp)], axis=0).astype(BF16)
        a = _dot_nt(kbp.astype(BF16), kstack) * decp
        pairs.append(dict(j0=j0, j1=j1, a=a, lev=lev2, n=-jnp.where(lev2[0], a, 0.0)))
    for j in jobs:
        q = q_ref[j["rows"], j["hs"]].astype(BF16)
        k = k_ref[j["rows"], j["hs"]].astype(BF16)
        dec = jnp.where(j["keep"], jnp.exp(j["gc"] - j["gr"]), 0.0)
        qk_ref[j["d"], j["c"], j["h"]] = (_dot_nt(q, k) * dec).astype(BF16)
    for lev in range(1, 6):
        for pr in pairs:
            l = jnp.where(pr["lev"][lev], pr["a"], 0.0)
            pr["y"] = l + _dot(l.astype(BF16), block_diag(pr["n"]))
        for pr in pairs:
            pr["n"] = pr["n"] - pr["y"] - _dot(pr["n"].astype(BF16), block_diag(pr["y"]))
    for pr in pairs:
        rhs_pair = []
        for j in (pr["j0"], pr["j1"]):
            rows, hs, beta = j["rows"], j["hs"], j["beta"]
            k = k_ref[rows, hs].astype(F32)
            j["eg"] = jnp.exp(j["gc"])
            j["rhs"] = jnp.concatenate([v_ref[rows, hs].astype(F32) * beta, k * beta * j["eg"]], axis=1)
            rhs_pair.append(j["rhs"])
        rstack = jnp.concatenate(rhs_pair, axis=0).astype(BF16)
        pr["j0"]["cor"] = _dot(jnp.where(left, pr["n"], 0.0).astype(BF16), rstack)
        pr["j1"]["cor"] = _dot(jnp.where(left, 0.0, pr["n"]).astype(BF16), rstack)
    for j in jobs:
        d, rows, hs, gc, gl, eg = j["d"], j["rows"], j["hs"], j["gc"], j["gl"], j["eg"]
        q = q_ref[rows, hs].astype(F32)
        k = k_ref[rows, hs].astype(F32)
        sol = j["rhs"] + j["cor"]
        u_ref[d, rows, hs] = sol[:, :GDN_DV].astype(BF16)
        w_ref[d, rows, hs] = sol[:, GDN_DV:].astype(BF16)
        qg_ref[d, rows, hs] = (q * eg).astype(BF16)
        kd_ref[d, rows, hs] = (k * jnp.exp(gl - gc)).astype(BF16)
        egl_ref[d, j["c"], :, hs] = jnp.broadcast_to(jnp.exp(gl), (8, 128))


def gdn_prep(p, sm, gT):
    R = p.shape[0]
    T, C = GDN_ROWS, GDN_CHUNK
    nc = T // C
    col = lambda k: pl.BlockSpec((T, 512), lambda i: (i, C_QKV // 512 + k))
    dirrow = pl.BlockSpec((2, T, 512), lambda i: (0, i, 0))
    return pl.pallas_call(
        _gdn_prep_kernel,
        grid=(R // T,),
        in_specs=[col(0), col(1), col(2),
                  pl.BlockSpec((T, 128), lambda i: (i, 0)),
                  pl.BlockSpec((nc, 16, C), lambda i: (i, 0, 0))],
        out_specs=[dirrow, dirrow, dirrow, dirrow,
                   pl.BlockSpec((2, nc, GDN_HEADS, C, C), lambda i: (0, i, 0, 0, 0)),
                   pl.BlockSpec((2, nc, 8, 512), lambda i: (0, i, 0, 0))],
        out_shape=[jax.ShapeDtypeStruct((2, R, 512), BF16),
                   jax.ShapeDtypeStruct((2, R, 512), BF16),
                   jax.ShapeDtypeStruct((2, R, 512), BF16),
                   jax.ShapeDtypeStruct((2, R, 512), BF16),
                   jax.ShapeDtypeStruct((2, R // C, GDN_HEADS, C, C), BF16),
                   jax.ShapeDtypeStruct((2, R // C, 8, 512), F32)],
        compiler_params=_cp("arbitrary"),
        name="gdn_prep",
    )(p, p, p, sm, gT)


def _gdn_scan_kernel(uf, wf, qgf, kdf, qkf, eglf, ub, wb, qgb, kdb, qkb, eglb, of_ref, ob_ref, s_ref):
    C = GDN_CHUNK
    nch = GDN_ROWS // C

    @pl.when(pl.program_id(1) == 0)
    def _():
        s_ref[...] = jnp.zeros_like(s_ref)

    dirs = ((uf, wf, qgf, kdf, qkf, eglf, of_ref), (ub, wb, qgb, kdb, qkb, eglb, ob_ref))
    chains = [(d, h) for d in range(2) for h in range(GDN_HEADS)]
    S = {ch: s_ref[ch[0], ch[1]] for ch in chains}
    for i in range(nch):
        Sb, vnb, rows_of, c_of = {}, {}, {}, {}
        for d, h in chains:
            c_of[d] = i if d == 0 else nch - 1 - i
            rows_of[d] = slice(c_of[d] * C, (c_of[d] + 1) * C)
        for d, h in chains:
            hs = slice(h * 128, (h + 1) * 128)
            Sb[d, h] = S[d, h].astype(BF16)
            v_new = dirs[d][0][0, rows_of[d], hs].astype(F32) - _dot(dirs[d][1][0, rows_of[d], hs], Sb[d, h])
            vnb[d, h] = v_new.astype(BF16)
        for d, h in chains:
            hs = slice(h * 128, (h + 1) * 128)
            u_ref, w_ref, qg_ref, kd_ref, qk_ref, egl_ref, o_ref = dirs[d]
            S[d, h] = S[d, h] * egl_ref[0, c_of[d], 0:1, hs] + _dot_tn(kd_ref[0, rows_of[d], hs], vnb[d, h])
        for d, h in chains:
            hs = slice(h * 128, (h + 1) * 128)
            u_ref, w_ref, qg_ref, kd_ref, qk_ref, egl_ref, o_ref = dirs[d]
            o_ref[rows_of[d], hs] = (_dot(qg_ref[0, rows_of[d], hs], Sb[d, h])
                                     + _dot(qk_ref[0, c_of[d], h], vnb[d, h])).astype(BF16)
    for ch in chains:
        s_ref[ch[0], ch[1]] = S[ch]


def gdn_scan(rw, u, w, qg, kd, qk, egl):
    T, C = GDN_ROWS, GDN_CHUNK
    nc = T // C
    R = u.shape[1]
    nbc, nbl, base = rw.Lc // T, rw.Ll // T, rw.NC // T

    def blk(d):
        def f(b, s):
            jc = s if d == 0 else nbc - 1 - s
            jl = (s - nbc) if d == 0 else nbl - 1 - (s - nbc)
            return jnp.where(s < nbc, b * nbc + jc, base + b * nbl + jl)
        return f

    in_specs = []
    for d in range(2):
        f = blk(d)
        rowspec = pl.BlockSpec((1, T, 512), lambda b, s, f=f, d=d: (d, f(b, s), 0))
        in_specs += [rowspec, rowspec, rowspec, rowspec,
                     pl.BlockSpec((1, nc, GDN_HEADS, C, C), lambda b, s, f=f, d=d: (d, f(b, s), 0, 0, 0)),
                     pl.BlockSpec((1, nc, 8, 512), lambda b, s, f=f, d=d: (d, f(b, s), 0, 0))]
    out_specs = [pl.BlockSpec((T, 512), lambda b, s, f=blk(d): (f(b, s), 0)) for d in range(2)]
    ops = (u, w, qg, kd, qk, egl)
    return pl.pallas_call(
        _gdn_scan_kernel,
        grid=(rw.B, nbc + nbl),
        in_specs=in_specs,
        out_specs=out_specs,
        out_shape=[jax.ShapeDtypeStruct((R, 512), BF16)] * 2,
        scratch_shapes=[pltpu.VMEM((2, GDN_HEADS, GDN_DK, GDN_DV), F32)],
        compiler_params=_cp("arbitrary", "arbitrary"),
        name="gdn_scan",
    )(*ops, *ops)


def _merge_kernel(yhc_ref, yhl_ref, sf_ref, sb_ref, sx_ref, sz_ref, dx_ref, snw_ref, gf_ref, gb_ref, gg_ref, gnw_ref,
                  g0_ref, g1_ref, g2_ref, w0_ref, w1_ref, w2_ref, wo_ref, xc_ref, xl_ref, mod_ref, o_ref,
                  ys_ref, yg_ref, *, nctx_blk):
    tm = xc_ref.shape[0]
    rp = 64
    for r in range(tm // rp):
        rs = slice(r * rp, (r + 1) * rp)
        y = (sf_ref[rs, :].astype(F32) + sb_ref[rs, :].astype(F32)
             + sx_ref[rs, :].astype(F32) * dx_ref[...])
        y = y * _silu(sz_ref[rs, :].astype(F32))
        parts = []
        for g in range(SSM_GROUPS):
            yg = y[:, g * SSM_GW:(g + 1) * SSM_GW]
            parts.append(yg * lax.rsqrt(jnp.mean(yg * yg, axis=-1, keepdims=True) + EPS))
        ys_ref[rs, :] = (jnp.concatenate(parts, axis=1) * snw_ref[...]).astype(BF16)
        o = gf_ref[rs, :].astype(F32) + gb_ref[rs, :].astype(F32)
        parts = []
        for h in range(GDN_HEADS):
            oh = o[:, h * 128:(h + 1) * 128]
            parts.append(oh * lax.rsqrt(jnp.mean(oh * oh, axis=-1, keepdims=True) + EPS))
        yg_ref[rs, :] = (jnp.concatenate(parts, axis=1) * gnw_ref[...]
                         * _silu(gg_ref[rs, :].astype(F32))).astype(BF16)
    is_ctx = pl.program_id(0) < nctx_blk
    yh = jnp.where(is_ctx, yhc_ref[...], yhl_ref[...])
    m = (_sigmoid(g0_ref[...].astype(F32)) * _dot(yh, w0_ref[...])
         + _sigmoid(g1_ref[...].astype(F32)) * _dot(ys_ref[...], w1_ref[...])
         + _sigmoid(g2_ref[...].astype(F32)) * _dot(yg_ref[...], w2_ref[...]))
    x = jnp.where(is_ctx, xc_ref[...], xl_ref[...])
    o_ref[...] = x + mod_ref[0, 2:3, :] * _dot(m.astype(BF16), wo_ref[...])


def merge(rw, l, yh, y_f, y_b, dx, ssm_nw, o_f, o_b, gdn_nw, p, w0, w1, w2, wo, xs, mod, skip_ctx=False):
    R = rw.R
    D = xs[0].shape[1]
    tm = min(rw.tm, 512)
    mi = rw.mod_index(tm)
    skip = rw.NC // tm if skip_ctx else 0
    yspec = pl.BlockSpec((tm, 512), lambda i: (i + skip, 0))
    pspec = lambda col: pl.BlockSpec((tm, 512), lambda i: (i + skip, col // 512))
    vec = pl.BlockSpec((1, 512), lambda i: (0, 0))
    gspec = lambda k: pl.BlockSpec((tm, D), lambda i: (i + skip, C_GATE // D + k))
    wspec = pl.BlockSpec((None, 512, D), lambda i: (l, 0, 0))
    return pl.pallas_call(
        functools.partial(_merge_kernel, nctx_blk=rw.NC // tm - skip),
        grid=(R // tm - skip,),
        in_specs=_stream_specs(rw, tm, yh, 1, skip) + [
                  yspec, yspec, pspec(C_XBC), pspec(C_Z), vec, vec,
                  yspec, yspec, pspec(C_GG), vec,
                  gspec(0), gspec(1), gspec(2), wspec, wspec, wspec,
                  pl.BlockSpec((None, D, D), lambda i: (l, 0, 0))]
                 + _stream_specs(rw, tm, xs, 1, skip)
                 + [pl.BlockSpec((None, 1, 8, D), lambda i: (l, mi(i + skip), 0, 0))],
        out_specs=pl.BlockSpec((tm, D), lambda i: (i + skip, 0)),
        out_shape=jax.ShapeDtypeStruct((R, D), F32),
        scratch_shapes=[pltpu.VMEM((tm, 512), BF16), pltpu.VMEM((tm, 512), BF16)],
        compiler_params=_cp("arbitrary"),
        name="merge",
    )(yh[0], yh[1], y_f, y_b, p, p, dx, ssm_nw, o_f, o_b, p, gdn_nw, p, p, p, w0, w1, w2, wo, xs[0], xs[1], mod)


def _swiglu_up_kernel(x_ref, nw_ref, mod_ref, wg_ref, wu_ref, o_ref, h_ref, g0_ref, g1_ref, u0_ref, u1_ref):
    @pl.when(pl.program_id(1) == 0)
    def _():
        h = _norm_mod(x_ref[...], nw_ref[...], mod_ref[0, 4:5, :], mod_ref[0, 3:4, :])
        h_ref[...] = h.astype(BF16)

    T, tn = o_ref.shape
    rows = g0_ref.shape[0]
    gs, us = (g0_ref, g1_ref), (u0_ref, u1_ref)

    def project(r):
        hh = h_ref[r * rows:(r + 1) * rows, :]
        gs[r % 2][...] = _dot(hh, wg_ref[...])
        us[r % 2][...] = _dot(hh, wu_ref[...])

    def finish(r):
        for q in range(rows // 64):
            for c in range(tn // 128):
                ps = (slice(q * 64, (q + 1) * 64), slice(c * 128, (c + 1) * 128))
                y = _silu(gs[r % 2][ps]) * us[r % 2][ps]
                o_ref[r * rows + q * 64:r * rows + (q + 1) * 64, ps[1]] = y.astype(o_ref.dtype)

    for r in range(T // rows):
        project(r)
        if r > 0:
            finish(r - 1)
    finish(T // rows - 1)


def swiglu_up(rw, l, x, nw, mod, wgu, skip_ctx=False):
    R, D = x.shape
    tm = min(rw.tm, 512)
    tn = D_FF
    nj = D_FF // tn
    mi = rw.mod_index(tm)
    skip = rw.NC // tm if skip_ctx else 0
    return pl.pallas_call(
        _swiglu_up_kernel,
        grid=(R // tm - skip, nj),
        in_specs=[
            pl.BlockSpec((tm, D), lambda i, j: (i + skip, 0)),
            pl.BlockSpec((None, 1, D), lambda i, j: (l, 0, 0)),
            pl.BlockSpec((None, 1, 8, D), lambda i, j: (l, mi(i + skip), 0, 0)),
            pl.BlockSpec((None, D, tn), lambda i, j: (l, 0, j)),
            pl.BlockSpec((None, D, tn), lambda i, j: (l, 0, nj + j)),
        ],
        out_specs=pl.BlockSpec((tm, tn), lambda i, j: (i + skip, j)),
        out_shape=jax.ShapeDtypeStruct((R, D_FF), BF16),
        scratch_shapes=[pltpu.VMEM((tm, D), BF16)] + [pltpu.VMEM((min(256, tm), tn), F32)] * 4,
        compiler_params=_cp("arbitrary", "arbitrary"),
        name="swiglu_up",
    )(x, nw, mod, wgu, wgu)


def _swiglu_down_kernel(a_ref, w_ref, x_ref, mod_ref, o_ref):
    o_ref[...] = x_ref[...] + mod_ref[0, 5:6, :] * _dot(a_ref[...], w_ref[...])


def swiglu_down(rw, l, a, w, x, mod, skip_ctx=False):
    R, D = x.shape
    tm = min(rw.tm, 512)
    mi = rw.mod_index(tm)
    skip = rw.NC // tm if skip_ctx else 0
    return pl.pallas_call(
        _swiglu_down_kernel,
        grid=(R // tm - skip,),
        in_specs=[
            pl.BlockSpec((tm, D_FF), lambda i: (i + skip, 0)),
            pl.BlockSpec((None, D_FF, D), lambda i: (l, 0, 0)),
            pl.BlockSpec((tm, D), lambda i: (i + skip, 0)),
            pl.BlockSpec((None, 1, 8, D), lambda i: (l, mi(i + skip), 0, 0)),
        ],
        out_specs=pl.BlockSpec((tm, D), lambda i: (i + skip, 0)),
        out_shape=jax.ShapeDtypeStruct((R, D), F32),
        compiler_params=_cp("arbitrary"),
        name="swiglu_down",
    )(a, w, x, mod)


def _final_norm_kernel(x_ref, w_ref, o_ref):
    x = x_ref[...]
    ms = jnp.mean(x * x, axis=-1, keepdims=True)
    o_ref[...] = x * lax.rsqrt(ms + EPS) * w_ref[...]


def final_norm(rw, x, w):
    D = x.shape[1]
    tm = rw.tm
    n0 = rw.NC // tm
    nl = rw.B * rw.Ll
    return pl.pallas_call(
        _final_norm_kernel,
        grid=(nl // tm,),
        in_specs=[pl.BlockSpec((tm, D), lambda i: (n0 + i, 0)), pl.BlockSpec((1, D), lambda i: (0, 0))],
        out_specs=pl.BlockSpec((tm, D), lambda i: (i, 0)),
        out_shape=jax.ShapeDtypeStruct((nl, D), F32),
        compiler_params=_cp("arbitrary"),
        name="final_norm",
    )(x, w.reshape(1, D))


def _regroup_w_in(w_in):
    o_dt = 3072
    o_gdn = 3088
    o_a = o_gdn + 2048
    o_b = o_a + 8
    o_gate = o_gdn + 2064
    wt = jnp.swapaxes(w_in, 1, 2).astype(BF16)
    pieces = [
        wt[:, 0:3072],
        wt[:, o_gdn:o_gdn + 2048],
        wt[:, o_gate:o_gate + 3072],
        wt[:, o_dt:o_dt + 16],
        wt[:, o_a:o_a + 4], wt[:, o_b:o_b + 4],
        wt[:, o_a + 4:o_a + 8], wt[:, o_b + 4:o_b + 8],
        jnp.zeros((wt.shape[0], N_IN_PAD - C_SM - 32, wt.shape[2]), wt.dtype),
    ]
    return jnp.swapaxes(jnp.concatenate(pieces, axis=1), 1, 2)


def kernel(x, c, ctx, c_ctx, w_ada, b_ada, norm1_w, norm2_w, w_in, hy_conv_w, hy_conv_b, hy_w1, hy_b1, hy_w2, hy_b2, hy_w3, hy_freq, hy_bias, ssm_conv_w, ssm_conv_b, ssm_dt_bias, ssm_A_log, ssm_D, ssm_norm_w, gdn_conv_w, gdn_dt_bias, gdn_A_log, gdn_norm_w, w_hy_out, w_ssm_out, w_gdn_out, w_out, w_gate_up, w_down, final_norm_w):
    B, Ll, D = x.shape
    Lc = ctx.shape[1]
    depth = w_ada.shape[0]
    assert Lc == CONV_ROWS and D == D_MODEL and B <= 15
    rw = Rows(B, Lc, Ll)
    R, NC = rw.R, rw.NC

    xs = (ctx.reshape(B * Lc, D), x.reshape(B * Ll, D))

    svec = jnp.concatenate([c_ctx[None, :], c, jnp.zeros((15 - B, D), F32)], axis=0)
    mod = ada_modulation(svec, w_ada, b_ada)
    mod = jnp.pad(mod.reshape(depth, 16, 6, D), ((0, 0), (0, 0), (0, 2), (0, 0)))

    w_in_r = _regroup_w_in(w_in)
    par = _in_proj_params(hy_conv_w, hy_conv_b, ssm_conv_w, ssm_conv_b, gdn_conv_w, ssm_dt_bias, gdn_dt_bias,
                          gdn_A_log)
    norm1 = norm1_w.reshape(depth, 1, D)
    norm2 = norm2_w.reshape(depth, 1, D)
    w_hy_o, w_ssm_o, w_gdn_o, w_o = (w.astype(BF16) for w in (w_hy_out, w_ssm_out, w_gdn_out, w_out))
    w_gu, w_dn = w_gate_up.astype(BF16), w_down.astype(BF16)
    dft_l = dft_tables_split(Ll)
    dft_c = dft_tables_split(Lc)
    feat_l, feat_c = hy_features(Ll), hy_features(Lc)

    for l in range(depth):
        p, sm = in_proj(rw, xs, l, norm1, mod, w_in_r, par)

        sm32_t = sm[:, :32].T
        dt_t = sm32_t[:16].reshape(2, 8, R)
        g_t = sm32_t[16:32].reshape(16, R // GDN_CHUNK, GDN_CHUNK).transpose(1, 0, 2)

        alx = jnp.repeat(ssm_A_log[l], SSM_HEAD_DIM, axis=-1).reshape(2, 1, 512)
        alc = ssm_A_log[l].reshape(2, 8, 1)
        y_f, y_b = ssd_scan(rw, p, sm, dt_t, alx, alc)
        dx = jnp.repeat(ssm_D[l], SSM_HEAD_DIM).reshape(1, 512)

        o_f, o_b = gdn_scan(rw, *gdn_prep(p, sm, g_t))

        last = l == depth - 1
        hyu = p
        parts = []
        for (Bn, L, blk0, (fwd, inv), feat) in ((B, Lc, 0, dft_c, feat_c), (B, Ll, NC // Ll, dft_l, feat_l)):
            if NC % L:
                raise ValueError("latent length must divide the context row count")
            if last and blk0 == 0:
                parts.append(None)
                continue
            filt = hy_filter(feat, hy_w1[l], hy_b1[l], hy_w2[l], hy_b2[l], hy_w3[l], hy_freq[l])
            kspec = filter_spectrum(fwd, *filt)
            z1 = long_conv(Bn, L, hyu, blk0, 0, hyu, blk0, 1, hy_bias[l, 0], fwd, inv, kspec, 0, F32)
            yy = long_conv(Bn, L, z1, 0, 0, hyu, blk0, 2, hy_bias[l, 1], fwd, inv, kspec, 1, BF16)
            parts.append(yy)
        y_hy = (parts[1] if last else parts[0], parts[1])

        xa = merge(rw, l, y_hy, y_f, y_b, dx, ssm_norm_w[l].reshape(1, 512),
                   o_f, o_b, jnp.tile(gdn_norm_w[l], GDN_HEADS).reshape(1, 512), p,
                   w_hy_o, w_ssm_o, w_gdn_o, w_o, xs, mod, skip_ctx=last)
        act = swiglu_up(rw, l, xa, norm2, mod, w_gu, skip_ctx=last)
        xa = swiglu_down(rw, l, act, w_dn, xa, mod, skip_ctx=last)
        xs = (xa, xa)

    out = final_norm(rw, xa, final_norm_w)
    return out.reshape(B, Ll, D)
```

```python
import functools
import math

import jax
import jax.numpy as jnp
import numpy as np
from jax import lax
from jax.experimental import pallas as pl
from jax.experimental.pallas import tpu as pltpu

F32 = jnp.float32
BF16 = jnp.bfloat16
HI = lax.Precision.HIGHEST

EPS = 1e-6
D_MODEL = 1024
GRID_W = 64

HY_WIDTH = 512
HY_BANDS = 16
HY_EMB = 1 + 2 * HY_BANDS
HY_HIDDEN = 64
HY_SHORT_DECAY_PCT = 0.3
HY_LONG_DECAY_PCT = 1.5
HY_TARGET = 1e-2

SSM_HEADS = 8
SSM_HEAD_DIM = 64
SSM_WIDTH = 512
SSM_GROUPS = 2
SSM_HPG = 4
SSM_STATE = 128
SSM_CHUNK = 128
SSM_GW = SSM_HPG * SSM_HEAD_DIM

GDN_HEADS = 4
GDN_DK = 128
GDN_DV = 128
GDN_CHUNK = 64

D_FF = 2816

C_HY = 0
C_Z = 1536
C_XBC = 2048
C_QKV = 3072
C_GG = 4608
C_GATE = 5120
C_SM = 8192

CONV_ROWS = 256
FREQ_BLK = 256

VMEM_LIMIT = 56 * 1024 * 1024


def _cp(*sem, flags=None):
    return pltpu.CompilerParams(dimension_semantics=sem, vmem_limit_bytes=VMEM_LIMIT, flags=flags)


def _sigmoid(x):
    return 1.0 / (1.0 + jnp.exp(-x))


def _silu(x):
    return x * _sigmoid(x)


def _softplus(x):
    return jnp.maximum(x, 0.0) + jnp.log1p(jnp.exp(-jnp.abs(x)))


def _dot(a, b, precision=None):
    return jnp.dot(a, b, precision=precision, preferred_element_type=F32)


def _dot_nt(a, b):
    return lax.dot_general(a, b, (((1,), (1,)), ((), ())), preferred_element_type=F32)


def _dot_tn(a, b):
    return lax.dot_general(a, b, (((0,), (0,)), ((), ())), preferred_element_type=F32)


def _ada_kernel(s_ref, w_ref, b_ref, o_ref):
    s = _silu(s_ref[...])
    o_ref[0] = _dot(s, w_ref[0], HI) + b_ref[0]


def ada_modulation(svec, w_ada, b_ada):
    depth = w_ada.shape[0]
    D = D_MODEL
    return pl.pallas_call(
        _ada_kernel,
        grid=(depth, 6),
        in_specs=[
            pl.BlockSpec((16, D), lambda l, j: (0, 0)),
            pl.BlockSpec((1, D, D), lambda l, j: (l, 0, j)),
            pl.BlockSpec((1, 1, D), lambda l, j: (l, 0, j)),
        ],
        out_specs=pl.BlockSpec((1, 16, D), lambda l, j: (l, 0, j)),
        out_shape=jax.ShapeDtypeStruct((depth, 16, 6 * D), F32),
        compiler_params=_cp("arbitrary", "arbitrary"),
        name="ada",
    )(svec, w_ada, b_ada.reshape(depth, 1, 6 * D))


def _norm_mod(x, nw, scale, shift):
    ms = jnp.mean(x * x, axis=-1, keepdims=True)
    return (x * lax.rsqrt(ms + EPS) * nw) * (1.0 + scale) + shift


IN_FLIGHT = 3
N_IN_PAD = C_SM + 128
IN_TN = N_IN_PAD // 5
MODE_RAW, MODE_CONV, MODE_CONV_SILU, MODE_CONV_SILU_L2, MODE_SMALL = range(5)


def _tile_mode(tile):
    col = tile * 128
    if col < C_Z:
        return MODE_CONV
    if col < C_XBC:
        return MODE_RAW
    if col < C_QKV:
        return MODE_CONV_SILU
    if col < C_QKV + 1024:
        return MODE_CONV_SILU_L2
    if col < C_GG:
        return MODE_CONV_SILU
    if col < C_SM:
        return MODE_RAW
    return MODE_SMALL
PAR_W0, PAR_W1, PAR_W2, PAR_BIAS, PAR_L2SCALE, PAR_SBIAS, PAR_SALOG, PAR_SKIND = range(8)


def _in_proj_kernel(xc_ref, xl_ref, nw_ref, mod_ref, w_ref, par_ref, o_ref, sm_ref, h_ref, raw0_ref, raw1_ref, *,
                    nctx_blk):
    j = pl.program_id(1)
    nj = N_IN_PAD // IN_TN
    raws = (raw0_ref, raw1_ref)

    @pl.when((j == 0) & (pl.program_id(0) < nctx_blk))
    def _():
        h = _norm_mod(xc_ref[...], nw_ref[...], mod_ref[0, 1:2, :], mod_ref[0, 0:1, :])
        h_ref[...] = h.astype(BF16)

    @pl.when((j == 0) & (pl.program_id(0) >= nctx_blk))
    def _():
        h = _norm_mod(xl_ref[...], nw_ref[...], mod_ref[0, 1:2, :], mod_ref[0, 0:1, :])
        h_ref[...] = h.astype(BF16)

    T = h_ref.shape[0]
    G = GRID_W
    per_ctx = CONV_ROWS // G
    is_latent = pl.program_id(0) >= nctx_blk
    sub = lax.broadcasted_iota(jnp.int32, (8, 128), 0)

    def raw_piece(src, g, c):
        return src[g * G:(g + 1) * G, c * 128:(c + 1) * 128]

    retired = []

    def retire(y):
        bits = pltpu.bitcast(y[0:8], jnp.int32)
        zero = lax.shift_right_logical(lax.shift_right_logical(bits, 16), 16)
        retired.append(jnp.tile(zero.astype(F32), (G // 8, 1)))

    def conv(src, g, c):
        cs = slice(c * 128, (c + 1) * 128)
        x = raw_piece(src, g, c)
        if len(retired) >= IN_FLIGHT:
            x = x + retired[-IN_FLIGHT]
        zero = jnp.zeros((1, 128), F32)
        before = zero if g % per_ctx == 0 else jnp.where(is_latent, 0.0, src[g * G - 1:g * G, cs])
        after = zero if g % per_ctx == per_ctx - 1 else jnp.where(is_latent, 0.0, src[(g + 1) * G:(g + 1) * G + 1, cs])
        rp = pltpu.roll(x, 1, 0)
        rn = pltpu.roll(x, G - 1, 0)
        prev = jnp.concatenate([jnp.where(sub == 0, before, rp[0:8]), rp[8:]], axis=0)
        nxt = jnp.concatenate([rn[:G - 8], jnp.where(sub == 7, after, rn[G - 8:])], axis=0)
        return (prev * par_ref[PAR_W0:PAR_W0 + 1, cs] + x * par_ref[PAR_W1:PAR_W1 + 1, cs]
                + nxt * par_ref[PAR_W2:PAR_W2 + 1, cs] + par_ref[PAR_BIAS:PAR_BIAS + 1, cs])

    def conv_silu(src, g, c):
        return _silu(conv(src, g, c))

    def conv_silu_l2(src, g, c):
        y = _silu(conv(src, g, c))
        y = y * lax.rsqrt(jnp.sum(y * y, axis=-1, keepdims=True) + EPS)
        return y * par_ref[PAR_L2SCALE:PAR_L2SCALE + 1, c * 128:(c + 1) * 128]

    def small(src, g, c):
        cs = slice(c * 128, (c + 1) * 128)
        acc = raw_piece(src, g, c)
        kind = par_ref[PAR_SKIND:PAR_SKIND + 1, cs]
        sp = _softplus(acc + par_ref[PAR_SBIAS:PAR_SBIAS + 1, cs])
        dec = -jnp.exp(par_ref[PAR_SALOG:PAR_SALOG + 1, cs]) * sp
        return jnp.where(kind == 0.0, sp, jnp.where(kind == 1.0, dec, jnp.where(kind == 2.0, _sigmoid(acc), 0.0)))

    rows_mm = 256
    tiles = IN_TN // 128
    piece_fn = {MODE_RAW: raw_piece, MODE_CONV: conv, MODE_CONV_SILU: conv_silu,
                MODE_CONV_SILU_L2: conv_silu_l2, MODE_SMALL: small}

    heavy_modes = (MODE_CONV_SILU, MODE_CONV_SILU_L2)
    col_slices = [slice(c0, min(c0 + 256, IN_TN)) for c0 in range(0, IN_TN, 256)]

    def project(dst, r, cs=slice(None)):
        rs = slice(r * rows_mm, (r + 1) * rows_mm)
        dst[rs, cs] = _dot(h_ref[rs, :], w_ref[:, cs])

    def finish(src, blk, g, c):
        mode = _tile_mode(blk * tiles + c)
        y = piece_fn[mode](src, g, c)
        if mode in heavy_modes:
            retire(y)
        if mode == MODE_SMALL:
            sm_ref[g * G:(g + 1) * G, :] = y
            y = jnp.zeros_like(y)
        o_ref[g * G:(g + 1) * G, c * 128:(c + 1) * 128] = y.astype(o_ref.dtype)

    for step in range(nj + 1):
        @pl.when(j == step)
        def _(step=step):
            blk = step - 1
            src, dst = raws[blk % 2], raws[step % 2]
            retired.clear()
            light = blk >= 0 and not any(_tile_mode(blk * tiles + c) in heavy_modes for c in range(tiles))
            for r in range(T // rows_mm):
                pieces = [] if blk < 0 else [(g, c) for g in range(r * rows_mm // G, (r + 1) * rows_mm // G)
                                             for c in range(tiles)]
                if step == nj:
                    slabs = []
                elif light:
                    slabs = col_slices
                else:
                    slabs = [slice(None)]
                per = -(-len(pieces) // max(len(slabs), 1))
                for n in range(max(len(slabs), 1)):
                    if n < len(slabs):
                        project(dst, r, slabs[n])
                    for g, c in pieces[n * per:(n + 1) * per]:
                        finish(src, blk, g, c)


class Rows:
    def __init__(self, B, Lc, Ll):
        self.B, self.Lc, self.Ll = B, Lc, Ll
        self.NC = B * Lc
        self.R = B * Lc + B * Ll
        assert self.NC % Ll == 0 or Ll % self.NC == 0
        tm = 1024
        while self.NC % tm or Ll % tm:
            tm //= 2
        self.tm = tm

    def mod_index(self, tm):
        nctx = self.NC // tm
        per = self.Ll // tm
        return lambda i: jnp.where(i < nctx, 0, 1 + (i - nctx) // per)


def _stream_specs(rw, tm, xs, ngrid, skip=0):
    xc, xl = xs
    nctx = rw.NC // tm
    off = nctx if xl.shape[0] == rw.R else 0
    D = xc.shape[1]
    if ngrid == 1:
        return [pl.BlockSpec((tm, D), lambda i: (jnp.minimum(i + skip, nctx - 1), 0)),
                pl.BlockSpec((tm, D), lambda i: (jnp.maximum(i + skip - nctx, 0) + off, 0))]
    return [pl.BlockSpec((tm, D), lambda i, j: (jnp.minimum(i + skip, nctx - 1), 0)),
            pl.BlockSpec((tm, D), lambda i, j: (jnp.maximum(i + skip - nctx, 0) + off, 0))]


def in_proj(rw, xs, l, nw, mod, w, par):
    R = rw.R
    D = xs[0].shape[1]
    N = w.shape[2]
    tm, tn = rw.tm, IN_TN
    nj = N // tn
    assert N == N_IN_PAD
    mi = rw.mod_index(tm)
    done = lambda j: jnp.maximum(j - 1, 0)
    return pl.pallas_call(
        functools.partial(_in_proj_kernel, nctx_blk=rw.NC // tm),
        grid=(R // tm, nj + 1),
        in_specs=_stream_specs(rw, tm, xs, 2) + [
            pl.BlockSpec((None, 1, D), lambda i, j: (l, 0, 0)),
            pl.BlockSpec((None, 1, 8, D), lambda i, j: (l, mi(i), 0, 0)),
            pl.BlockSpec((None, D, tn), lambda i, j: (l, 0, jnp.minimum(j, nj - 1))),
            pl.BlockSpec((None, 8, tn), lambda i, j: (l, 0, done(j))),
        ],
        out_specs=[pl.BlockSpec((tm, tn), lambda i, j: (i, done(j))),
                   pl.BlockSpec((tm, 128), lambda i, j: (i, 0))],
        out_shape=[jax.ShapeDtypeStruct((R, N), BF16), jax.ShapeDtypeStruct((R, 128), F32)],
        scratch_shapes=[pltpu.VMEM((tm, D), BF16), pltpu.VMEM((tm, tn), F32), pltpu.VMEM((tm, tn), F32)],
        compiler_params=_cp("arbitrary", "arbitrary"),
        name="in_proj",
    )(xs[0], xs[1], nw, mod, w, par)


def _in_proj_params(hy_conv_w, hy_conv_b, ssm_conv_w, ssm_conv_b, gdn_conv_w, ssm_dt_bias, gdn_dt_bias, gdn_A_log):
    depth = hy_conv_w.shape[0]

    def row(pieces):
        out, pos = [], 0
        for off, a in pieces:
            out += [jnp.zeros((depth, off - pos), F32), a.astype(F32)]
            pos = off + a.shape[1]
        return jnp.concatenate(out + [jnp.zeros((depth, N_IN_PAD - pos), F32)], axis=1)
    z4 = jnp.zeros((depth, 4), F32)
    conv = [row([(C_HY, hy_conv_w[:, t]), (C_XBC, ssm_conv_w[:, t]), (C_QKV, gdn_conv_w[:, t])]) for t in range(3)]
    bias = row([(C_HY, hy_conv_b), (C_XBC, ssm_conv_b)])
    l2s = row([(C_QKV, jnp.full((depth, 512), GDN_DK ** -0.5, F32)), (C_QKV + 512, jnp.ones((depth, 512), F32))])
    sbias = row([(C_SM, jnp.concatenate([ssm_dt_bias.reshape(depth, 16), gdn_dt_bias[:, 0], z4,
                                         gdn_dt_bias[:, 1], z4], axis=1))])
    salog = row([(C_SM + 16, jnp.concatenate([gdn_A_log[:, 0], z4, gdn_A_log[:, 1], z4], axis=1))])
    kind = np.full((depth, N_IN_PAD), 3.0, np.float32)
    kind[:, C_SM:C_SM + 16] = 0.0
    kind[:, C_SM + 16:C_SM + 20] = 1.0
    kind[:, C_SM + 24:C_SM + 28] = 1.0
    kind[:, C_SM + 20:C_SM + 24] = 2.0
    kind[:, C_SM + 28:C_SM + 32] = 2.0
    return jnp.stack(conv + [bias, l2s, sbias, salog, jnp.asarray(kind)], axis=1)


def _hy_filter_kernel(z_ref, w1_ref, b1_ref, w2_ref, b2_ref, w3_ref, f0_ref, f1_ref, win_ref, oe_ref, oo_ref,
                      h_ref, split_ref):
    @pl.when(pl.program_id(1) == 0)
    def _():
        h1 = jnp.sin(f0_ref[...] * (_dot(z_ref[...], w1_ref[...], HI) + b1_ref[...]))
        h_ref[...] = jnp.sin(f1_ref[...] * (_dot(h1, w2_ref[...], HI) + b2_ref[...]))

    a1 = h_ref[...].astype(BF16)
    a2 = (h_ref[...] - a1.astype(F32)).astype(BF16)
    b1 = w3_ref[...].astype(BF16)
    b2 = (w3_ref[...] - b1.astype(F32)).astype(BF16)
    h = (_dot(a1, b1) + _dot(a1, b2) + _dot(a2, b1)) * win_ref[...]
    tl = h.shape[0]
    row = lax.broadcasted_iota(jnp.int32, (tl, 1), 0) + pl.program_id(0) * tl
    drop = (row == 0) & (pl.program_id(1) % 2 == 1)
    h = jnp.where(drop, 0.0, h)
    for c in range(h.shape[1] // 128):
        cs = slice(c * 128, (c + 1) * 128)
        s_c = split_ref.at[c]
        s_c[...] = h[:, cs]
        oe_ref[:, cs] = s_c[pl.ds(0, tl // 2, stride=2), :].astype(oe_ref.dtype)
        oo_ref[:, cs] = s_c[pl.ds(1, tl // 2, stride=2), :].astype(oo_ref.dtype)


def hy_features(L):
    t = jnp.linspace(0.0, 1.0, L, dtype=F32)[:, None]
    w = 2.0 * math.pi * jnp.arange(L, dtype=F32)[:, None] / L
    f = jnp.linspace(1e-4, HY_BANDS - 1, HY_BANDS, dtype=F32)[None, :]
    z = jnp.concatenate([t, jnp.cos(f * w), -jnp.sin(f * w)], axis=-1)
    z = jnp.pad(z, ((0, 0), (0, 128 - HY_EMB)))
    min_decay = math.log(HY_TARGET) / HY_LONG_DECAY_PCT
    max_decay = math.log(HY_TARGET) / HY_SHORT_DECAY_PCT
    deltas = jnp.linspace(min_decay, max_decay, HY_WIDTH, dtype=F32)
    window = jnp.exp(-t * jnp.abs(deltas))
    return z, window


def hy_filter(feat, w1, b1, w2, b2, w3, freq):
    z, window = feat
    L = z.shape[0]
    H = HY_HIDDEN
    w1p = jnp.pad(w1, ((0, 128 - HY_EMB), (0, 128 - H)))
    w2p = jnp.pad(w2, ((0, 128 - H), (0, 128 - H)))
    w3p = jnp.pad(w3, ((0, 128 - H), (0, 0)))
    pad1 = lambda v: jnp.pad(v, (0, 128 - H)).reshape(1, 128)
    tl = 256
    full = lambda shape: pl.BlockSpec(shape, lambda i, j: (0, 0))
    return pl.pallas_call(
        _hy_filter_kernel,
        grid=(L // tl, 4),
        in_specs=[
            pl.BlockSpec((tl, 128), lambda i, j: (i, 0)),
            full((128, 128)), full((1, 128)), full((128, 128)), full((1, 128)),
            pl.BlockSpec((128, HY_WIDTH), lambda i, j: (0, j)),
            full((1, 128)), full((1, 128)),
            pl.BlockSpec((tl, HY_WIDTH), lambda i, j: (i, 0)),
        ],
        out_specs=[pl.BlockSpec((tl // 2, HY_WIDTH), lambda i, j: (i, j))] * 2,
        out_shape=[jax.ShapeDtypeStruct((L // 2, 4 * HY_WIDTH), BF16)] * 2,
        scratch_shapes=[pltpu.VMEM((tl, 128), F32), pltpu.VMEM((HY_WIDTH // 128, tl, 128), F32)],
        compiler_params=_cp("arbitrary", "arbitrary"),
        name="hy_filter",
    )(z, w1p, pad1(b1), w2p, pad1(b2), w3p, pad1(freq[0]), pad1(freq[1]), window)


def dft_tables_split(L):
    N = 2 * L
    H = L // 2
    q = np.arange(H, dtype=np.int64)[:, None]
    m = np.arange(H, dtype=np.int64)[None, :]
    ang_e = ((q * 2 * m) % N).astype(np.float64) * (2.0 * math.pi / N)
    ang_o = ((q * (2 * m + 1)) % N).astype(np.float64) * (2.0 * math.pi / N)
    alt = (1 - 2 * (m % 2)).astype(np.float64)
    ce, co = np.cos(ang_e), np.cos(ang_o)
    se = np.where(q == 0, alt, -np.sin(ang_e))
    so = np.where(q == 0, -alt, -np.sin(ang_o))
    w = np.where(q == 0, 1.0, 2.0) / N
    ise = np.where(q == 0, 2.0 / N * alt, -np.sin(ang_e) * w)
    iso = np.where(q == 0, -2.0 / N * alt, -np.sin(ang_o) * w)
    fwd = np.stack([ce, co, se, so])
    inv = np.stack([(ce * w).T, ise.T, (co * w).T, iso.T])
    return jnp.asarray(fwd, dtype=BF16), jnp.asarray(inv, dtype=BF16)


def _filter_spectrum_kernel(fwd_ref, he_ref, ho_ref, rlo_ref, rhi_ref, ilo_ref, ihi_ref):
    he, ho = he_ref[...], ho_ref[...]
    ae, ao = _dot(fwd_ref[0], he), _dot(fwd_ref[1], ho)
    be, bo = _dot(fwd_ref[2], he), _dot(fwd_ref[3], ho)
    rlo_ref[...] = ae + ao
    rhi_ref[...] = ae - ao
    first = (lax.broadcasted_iota(jnp.int32, (fwd_ref.shape[1], 1), 0) == 0) & (pl.program_id(0) == 0)
    ilo_ref[...] = jnp.where(first, be, be + bo)
    ihi_ref[...] = jnp.where(first, bo, bo - be)


def filter_spectrum(fwd, taps_even, taps_odd):
    H, N = taps_even.shape
    C = HY_WIDTH
    FB = min(FREQ_BLK, H)
    tap = pl.BlockSpec((H, C), lambda f, j: (0, j))
    out = pl.BlockSpec((FB, C), lambda f, j: (f, j))
    return pl.pallas_call(
        _filter_spectrum_kernel,
        grid=(H // FB, N // C),
        in_specs=[pl.BlockSpec((4, FB, H), lambda f, j: (0, f, 0)), tap, tap],
        out_specs=[out] * 4,
        out_shape=[jax.ShapeDtypeStruct((H, N), F32)] * 4,
        compiler_params=_cp("arbitrary", "arbitrary"),
        name="filter_spectrum",
    )(fwd, taps_even, taps_odd)


def _long_conv_kernel(u_ref, g_ref, bias_ref, fwd_ref, inv_ref, ar0_ref, ar1_ref, ar0h_ref, ar1h_ref,
                      ai0_ref, ai1_ref, ai0h_ref, ai1h_ref, o_ref, ue_ref, uo_ref, acce_ref, acco_ref, y_ref):
    f = pl.program_id(1)
    half = ue_ref.shape[0]

    lane_tiles = [slice(c * 128, (c + 1) * 128) for c in range(y_ref.shape[0])]

    @pl.when(f == 0)
    def _():
        for c, cs in enumerate(lane_tiles):
            y_c = y_ref.at[c]
            y_c[...] = u_ref[:, cs].astype(F32)
            ue_ref[:, cs] = y_c[pl.ds(0, half, stride=2), :].astype(BF16)
            uo_ref[:, cs] = y_c[pl.ds(1, half, stride=2), :].astype(BF16)
        acce_ref[...] = jnp.zeros_like(acce_ref)
        acco_ref[...] = jnp.zeros_like(acco_ref)

    ue, uo = ue_ref[...], uo_ref[...]
    ae, ao = _dot(fwd_ref[0], ue), _dot(fwd_ref[1], uo)
    be, bo = _dot(fwd_ref[2], ue), _dot(fwd_ref[3], uo)
    ur, ur2 = ae + ao, ae - ao
    ui, ui2 = be + bo, bo - be
    first = (lax.broadcasted_iota(jnp.int32, (fwd_ref.shape[1], 1), 0) == 0) & (f == 0)
    kr, kr2 = ar0_ref[...] + ar1_ref[...], ar0h_ref[...] + ar1h_ref[...]
    ki = jnp.where(first, ai0_ref[...] + ai1_ref[...], ai0_ref[...] - ai1_ref[...])
    ki2 = ai0h_ref[...] - ai1h_ref[...]
    pr, pi = ur * kr - ui * ki, ur * ki + ui * kr
    pr2, pi2 = ur2 * kr2 - ui2 * ki2, ur2 * ki2 + ui2 * kr2
    dc, ny = ur * kr, ur2 * kr2
    gr = jnp.where(first, dc + ny, pr + pr2)
    gi = jnp.where(first, be * ki - bo * ki2, pi - pi2)
    hr = jnp.where(first, dc - ny, pr - pr2)
    hi = jnp.where(first, be * ki2 + bo * ki, pi + pi2)
    acce_ref[...] += _dot(inv_ref[0], gr.astype(BF16)) + _dot(inv_ref[1], gi.astype(BF16))
    acco_ref[...] += _dot(inv_ref[2], hr.astype(BF16)) + _dot(inv_ref[3], hi.astype(BF16))

    @pl.when(f == pl.num_programs(1) - 1)
    def _():
        for c, cs in enumerate(lane_tiles):
            y_c = y_ref.at[c]
            y_c[pl.ds(0, half, stride=2), :] = acce_ref[:, cs]
            y_c[pl.ds(1, half, stride=2), :] = acco_ref[:, cs]
            u = u_ref[:, cs].astype(F32)
            o_ref[:, cs] = (g_ref[:, cs].astype(F32) * (y_c[...] + u * bias_ref[:, cs])).astype(o_ref.dtype)


def long_conv(B, L, u, u_rb0, u_cb, gate, g_rb0, gate_cb, bias, fwd, inv, kspec, order, out_dtype):
    C = HY_WIDTH
    H = L // 2
    FB = min(FREQ_BLK, H)
    nfb = H // FB
    kblk = lambda part, d: pl.BlockSpec((FB, C), lambda b, f: (f, 2 * order + d))
    kops = [kspec[part] for part in range(4) for _ in range(2)]
    return pl.pallas_call(
        _long_conv_kernel,
        grid=(B, nfb),
        in_specs=[
            pl.BlockSpec((L, C), lambda b, f: (u_rb0 + b, u_cb)),
            pl.BlockSpec((L, C), lambda b, f: (g_rb0 + b, gate_cb)),
            pl.BlockSpec((1, C), lambda b, f: (0, 0)),
            pl.BlockSpec((4, FB, H), lambda b, f: (0, f, 0)),
            pl.BlockSpec((4, H, FB), lambda b, f: (0, 0, f)),
            kblk(0, 0), kblk(0, 1), kblk(1, 0), kblk(1, 1), kblk(2, 0), kblk(2, 1), kblk(3, 0), kblk(3, 1),
        ],
        out_specs=pl.BlockSpec((L, C), lambda b, f: (b, 0)),
        out_shape=jax.ShapeDtypeStruct((B * L, C), out_dtype),
        scratch_shapes=[pltpu.VMEM((H, C), BF16), pltpu.VMEM((H, C), BF16),
                        pltpu.VMEM((H, C), F32), pltpu.VMEM((H, C), F32), pltpu.VMEM((C // 128, L, 128), F32)],
        compiler_params=_cp("arbitrary", "arbitrary"),
        name="long_conv",
    )(u, gate, bias.reshape(1, C), fwd, inv, *kops)


def _scan_blocks(rw, rows):
    nbc, nbl, base = rw.Lc // rows, rw.Ll // rows, rw.NC // rows

    def make(d):
        def f(b, s):
            jc = s if d == 0 else nbc - 1 - s
            jl = (s - nbc) if d == 0 else nbl - 1 - (s - nbc)
            return jnp.where(s < nbc, b * nbc + jc, base + b * nbl + jl)
        return f

    return [make(0), make(1)], nbc + nbl


def _expand_lanes(x, base, n, width):
    rows = x.shape[0]
    per = 128 // width
    lane = lax.broadcasted_iota(jnp.int32, (rows, 128), 1)
    tiles = []
    for t in range(n // per):
        c0 = base + t * per
        tile = jnp.broadcast_to(x[:, c0:c0 + 1], (rows, 128))
        for i in range(1, per):
            tile = jnp.where(lane >= i * width, jnp.broadcast_to(x[:, c0 + i:c0 + i + 1], (rows, 128)), tile)
        tiles.append(tile)
    return jnp.concatenate(tiles, axis=1)


def _ssd_kernel(xf, bf, cf, smf, dtf, xb, bb, cb_, smb, dtb, alx_ref, alc_ref, of_ref, ob_ref, h_ref):
    Q = SSM_CHUNK
    GW = SSM_GW

    @pl.when(pl.program_id(1) == 0)
    def _():
        h_ref[...] = jnp.zeros_like(h_ref)

    row = lax.broadcasted_iota(jnp.int32, (Q, Q), 0)
    col = lax.broadcasted_iota(jnp.int32, (Q, Q), 1)
    lane_head = lax.broadcasted_iota(jnp.int32, (Q, GW), 1) // SSM_HEAD_DIM
    dirs = ((xf, bf, cf, smf, dtf, of_ref), (xb, bb, cb_, smb, dtb, ob_ref))
    jobs = []
    for d in range(2):
        x_ref, b_ref, c_ref, sm_ref, dt_ref, o_ref = dirs[d]
        keep = (col <= row) if d == 0 else (col >= row)
        tri = keep.astype(BF16)
        tri_t = ((row <= col) if d == 0 else (row >= col)).astype(BF16)
        sm = sm_ref[...]
        a_x = -jnp.exp(alx_ref[d])
        dtx = _expand_lanes(sm, 8 * d, SSM_HEADS, SSM_HEAD_DIM)
        cumx = _expand_lanes(_dot_01_lhs(tri, sm), 8 * d, SSM_HEADS, SSM_HEAD_DIM) * a_x
        cumr = _dot_01_rhs(dt_ref[0], tri_t) * (-jnp.exp(alc_ref[d]))
        last = Q - 1 if d == 0 else 0
        totx = cumx[last:last + 1, :]
        xd = x_ref[...].astype(F32) * dtx
        xdw = xd * jnp.exp(totx - cumx)
        ecum = jnp.exp(cumx)
        for g in range(SSM_GROUPS):
            gs = slice(g * GW, (g + 1) * GW)
            jobs.append(dict(d=d, g=g, gs=gs, keep=keep, cumx=cumx, cumr=cumr, o_ref=o_ref,
                             bg=b_ref[:, g * SSM_STATE:(g + 1) * SSM_STATE].astype(BF16),
                             cg=c_ref[:, g * SSM_STATE:(g + 1) * SSM_STATE].astype(BF16),
                             xdg=xd[:, gs], xdw=xdw[:, gs].astype(BF16), ecum=ecum[:, gs],
                             etot=jnp.exp(totx[:, gs])))
    for j in jobs:
        j["cb"] = _dot_nt(j["cg"], j["bg"])
        j["h"] = h_ref[j["d"], j["g"]]
    for j in jobs:
        ms, xs = [], []
        for e4 in range(SSM_HPG):
            e = j["g"] * SSM_HPG + e4
            diff = j["cumx"][:, e * SSM_HEAD_DIM:e * SSM_HEAD_DIM + 1] - j["cumr"][e:e + 1, :]
            ms.append((j["cb"] * jnp.where(j["keep"], jnp.exp(diff), 0.0)).astype(BF16))
            xs.append(jnp.where(lane_head == e4, j["xdg"], 0.0).astype(BF16))
        yd = _dot(jnp.concatenate(ms, axis=1), jnp.concatenate(xs, axis=0))
        y_off = _dot(j["cg"], j["h"].astype(BF16)) * j["ecum"]
        j["o_ref"][:, j["gs"]] = (yd + y_off).astype(BF16)
    for j in jobs:
        h_ref[j["d"], j["g"]] = j["h"] * j["etot"] + _dot_tn(j["bg"], j["xdw"])


def ssd_scan(rw, p, sm, dtT, alx, alc):
    Q = SSM_CHUNK
    blks, nsteps = _scan_blocks(rw, Q)
    R = p.shape[0]
    in_specs = []
    for d in range(2):
        f = blks[d]
        in_specs += [
            pl.BlockSpec((Q, 512), lambda b, s, f=f: (f(b, s), C_XBC // 512)),
            pl.BlockSpec((Q, 256), lambda b, s, f=f: (f(b, s), C_XBC // 256 + 2)),
            pl.BlockSpec((Q, 256), lambda b, s, f=f: (f(b, s), C_XBC // 256 + 3)),
            pl.BlockSpec((Q, 128), lambda b, s, f=f: (f(b, s), 0)),
            pl.BlockSpec((1, 8, Q), lambda b, s, f=f, d=d: (d, 0, f(b, s))),
        ]
    in_specs += [pl.BlockSpec((2, 1, 512), lambda b, s: (0, 0, 0)), pl.BlockSpec((2, 8, 1), lambda b, s: (0, 0, 0))]
    ops = (p, p, p, sm, dtT)
    return pl.pallas_call(
        _ssd_kernel,
        grid=(rw.B, nsteps),
        in_specs=in_specs,
        out_specs=[pl.BlockSpec((Q, 512), lambda b, s, f=blks[d]: (f(b, s), 0)) for d in range(2)],
        out_shape=[jax.ShapeDtypeStruct((R, 512), BF16)] * 2,
        scratch_shapes=[pltpu.VMEM((2, SSM_GROUPS, SSM_STATE, SSM_GW), F32)],
        compiler_params=_cp("arbitrary", "arbitrary"),
        name="ssd_scan",
    )(*ops, *ops, alx, alc)


def _split3(x):
    x1 = x.astype(BF16)
    r = x - x1.astype(F32)
    x2 = r.astype(BF16)
    x3 = (r - x2.astype(F32)).astype(BF16)
    return x1, x2, x3


def _dot_01_lhs(m01, x):
    x1, x2, x3 = _split3(x)
    return _dot(m01, x1) + _dot(m01, x2) + _dot(m01, x3)


def _dot_01_rhs(x, m01):
    x1, x2, x3 = _split3(x)
    return _dot(x1, m01) + _dot(x2, m01) + _dot(x3, m01)


GDN_ROWS = 256


def _gdn_prep_kernel(q_ref, k_ref, v_ref, sm_ref, gT_ref, u_ref, w_ref, qg_ref, kd_ref, qk_ref, egl_ref):
    C = GDN_CHUNK
    row = lax.broadcasted_iota(jnp.int32, (C, C), 0)
    col = lax.broadcasted_iota(jnp.int32, (C, C), 1)
    lane2 = lax.broadcasted_iota(jnp.int32, (C, 2 * C), 1)
    row2 = lax.broadcasted_iota(jnp.int32, (C, 2 * C), 0)
    col2 = lane2 & (C - 1)
    left = lane2 < C
    lane_k = lax.broadcasted_iota(jnp.int32, (C, 2 * GDN_DK), 1) < GDN_DK
    jobs = []
    for d in range(2):
        keep = (col <= row) if d == 0 else (col >= row)
        tri = keep.astype(BF16)
        tri_t2 = ((row2 <= col2) if d == 0 else (row2 >= col2)).astype(BF16)
        last = C - 1 if d == 0 else 0
        for c in range(GDN_ROWS // C):
            rows = slice(c * C, (c + 1) * C)
            smc = sm_ref[rows, :]
            cums = _dot_01_lhs(tri, smc)
            cumr2 = _dot_01_rhs(gT_ref[c, 8 * d:8 * d + 8, :], tri_t2)
            tot = cums[last:last + 1, :]
            for h in range(GDN_HEADS):
                lg = 16 + 8 * d + h
                jobs.append(dict(d=d, c=c, h=h, rows=rows, hs=slice(h * 128, (h + 1) * 128), keep=keep,
                                 gc=cums[:, lg:lg + 1], beta=smc[:, lg + 4:lg + 5],
                                 gl=tot[:, lg:lg + 1], gr=cumr2[h:h + 1, :C], gr2=cumr2[h:h + 1, :]))

    def block_diag(x):
        return jnp.concatenate([jnp.where(left, x, 0.0), jnp.where(left, 0.0, x)], axis=0).astype(BF16)

    pairs = []
    for i in range(0, len(jobs), 2):
        j0, j1 = jobs[i], jobs[i + 1]
        d, rows = j0["d"], j0["rows"]
        ps = slice(j0["hs"].start, j1["hs"].stop)
        keep2 = (col2 <= row2) if d == 0 else (col2 >= row2)
        late2, early2 = (row2, col2) if d == 0 else (col2, row2)
        lev2 = [(((row2 ^ col2) >> (t + 1)) == 0) & ((late2 & (1 << t)) != 0) & ((early2 & (1 << t)) == 0)
                for t in range(6)]
        gcp = jnp.where(left, j0["gc"], j1["gc"])
        grp = jnp.where(left[0:1], j0["gr2"], j1["gr2"])
        decp = jnp.where(keep2, jnp.exp(gcp - grp), 0.0)
        kp = k_ref[rows, ps].astype(F32)
        kbp = kp * jnp.where(lane_k, j0["beta"], j1["beta"])
        kstack = jnp.concatenate([jnp.where(lane_k, kp, 0.0), jnp.where(lane_k, 0.0, kp)], axis=0).astype(BF16)
        a = _dot_nt(kbp.astype(BF16), kstack) * decp
        pairs.append(dict(j0=j0, j1=j1, a=a, lev=lev2, n=-jnp.where(lev2[0], a, 0.0)))
    for j in jobs:
        q = q_ref[j["rows"], j["hs"]].astype(BF16)
        k = k_ref[j["rows"], j["hs"]].astype(BF16)
        dec = jnp.where(j["keep"], jnp.exp(j["gc"] - j["gr"]), 0.0)
        qk_ref[j["d"], j["c"], j["h"]] = (_dot_nt(q, k) * dec).astype(BF16)
    for lev in range(1, 6):
        for pr in pairs:
            l = jnp.where(pr["lev"][lev], pr["a"], 0.0)
            pr["y"] = l + _dot(l.astype(BF16), block_diag(pr["n"]))
        for pr in pairs:
            pr["n"] = pr["n"] - pr["y"] - _dot(pr["n"].astype(BF16), block_diag(pr["y"]))
    for pr in pairs:
        rhs_pair = []
        for j in (pr["j0"], pr["j1"]):
            rows, hs, beta = j["rows"], j["hs"], j["beta"]
            k = k_ref[rows, hs].astype(F32)
            j["eg"] = jnp.exp(j["gc"])
            j["rhs"] = jnp.concatenate([v_ref[rows, hs].astype(F32) * beta, k * beta * j["eg"]], axis=1)
            rhs_pair.append(j["rhs"])
        rstack = jnp.concatenate(rhs_pair, axis=0).astype(BF16)
        pr["j0"]["cor"] = _dot(jnp.where(left, pr["n"], 0.0).astype(BF16), rstack)
        pr["j1"]["cor"] = _dot(jnp.where(left, 0.0, pr["n"]).astype(BF16), rstack)
    for j in jobs:
        d, rows, hs, gc, gl, eg = j["d"], j["rows"], j["hs"], j["gc"], j["gl"], j["eg"]
        q = q_ref[rows, hs].astype(F32)
        k = k_ref[rows, hs].astype(F32)
        sol = j["rhs"] + j["cor"]
        u_ref[d, rows, hs] = sol[:, :GDN_DV].astype(BF16)
        w_ref[d, rows, hs] = sol[:, GDN_DV:].astype(BF16)
        qg_ref[d, rows, hs] = (q * eg).astype(BF16)
        kd_ref[d, rows, hs] = (k * jnp.exp(gl - gc)).astype(BF16)
        egl_ref[d, j["c"], :, hs] = jnp.broadcast_to(jnp.exp(gl), (8, 128))


def gdn_prep(p, sm, gT):
    R = p.shape[0]
    T, C = GDN_ROWS, GDN_CHUNK
    nc = T // C
    col = lambda k: pl.BlockSpec((T, 512), lambda i: (i, C_QKV // 512 + k))
    dirrow = pl.BlockSpec((2, T, 512), lambda i: (0, i, 0))
    return pl.pallas_call(
        _gdn_prep_kernel,
        grid=(R // T,),
        in_specs=[col(0), col(1), col(2),
                  pl.BlockSpec((T, 128), lambda i: (i, 0)),
                  pl.BlockSpec((nc, 16, C), lambda i: (i, 0, 0))],
        out_specs=[dirrow, dirrow, dirrow, dirrow,
                   pl.BlockSpec((2, nc, GDN_HEADS, C, C), lambda i: (0, i, 0, 0, 0)),
                   pl.BlockSpec((2, nc, 8, 512), lambda i: (0, i, 0, 0))],
        out_shape=[jax.ShapeDtypeStruct((2, R, 512), BF16),
                   jax.ShapeDtypeStruct((2, R, 512), BF16),
                   jax.ShapeDtypeStruct((2, R, 512), BF16),
                   jax.ShapeDtypeStruct((2, R, 512), BF16),
                   jax.ShapeDtypeStruct((2, R // C, GDN_HEADS, C, C), BF16),
                   jax.ShapeDtypeStruct((2, R // C, 8, 512), F32)],
        compiler_params=_cp("arbitrary"),
        name="gdn_prep",
    )(p, p, p, sm, gT)


def _gdn_scan_kernel(uf, wf, qgf, kdf, qkf, eglf, ub, wb, qgb, kdb, qkb, eglb, of_ref, ob_ref, s_ref):
    C = GDN_CHUNK
    nch = GDN_ROWS // C

    @pl.when(pl.program_id(1) == 0)
    def _():
        s_ref[...] = jnp.zeros_like(s_ref)

    dirs = ((uf, wf, qgf, kdf, qkf, eglf, of_ref), (ub, wb, qgb, kdb, qkb, eglb, ob_ref))
    chains = [(d, h) for d in range(2) for h in range(GDN_HEADS)]
    S = {ch: s_ref[ch[0], ch[1]] for ch in chains}
    for i in range(nch):
        Sb, vnb, rows_of, c_of = {}, {}, {}, {}
        for d, h in chains:
            c_of[d] = i if d == 0 else nch - 1 - i
            rows_of[d] = slice(c_of[d] * C, (c_of[d] + 1) * C)
        for d, h in chains:
            hs = slice(h * 128, (h + 1) * 128)
            Sb[d, h] = S[d, h].astype(BF16)
            v_new = dirs[d][0][0, rows_of[d], hs].astype(F32) - _dot(dirs[d][1][0, rows_of[d], hs], Sb[d, h])
            vnb[d, h] = v_new.astype(BF16)
        for d, h in chains:
            hs = slice(h * 128, (h + 1) * 128)
            u_ref, w_ref, qg_ref, kd_ref, qk_ref, egl_ref, o_ref = dirs[d]
            S[d, h] = S[d, h] * egl_ref[0, c_of[d], 0:1, hs] + _dot_tn(kd_ref[0, rows_of[d], hs], vnb[d, h])
        for d, h in chains:
            hs = slice(h * 128, (h + 1) * 128)
            u_ref, w_ref, qg_ref, kd_ref, qk_ref, egl_ref, o_ref = dirs[d]
            o_ref[rows_of[d], hs] = (_dot(qg_ref[0, rows_of[d], hs], Sb[d, h])
                                     + _dot(qk_ref[0, c_of[d], h], vnb[d, h])).astype(BF16)
    for ch in chains:
        s_ref[ch[0], ch[1]] = S[ch]


def gdn_scan(rw, u, w, qg, kd, qk, egl):
    T, C = GDN_ROWS, GDN_CHUNK
    nc = T // C
    R = u.shape[1]
    nbc, nbl, base = rw.Lc // T, rw.Ll // T, rw.NC // T

    def blk(d):
        def f(b, s):
            jc = s if d == 0 else nbc - 1 - s
            jl = (s - nbc) if d == 0 else nbl - 1 - (s - nbc)
            return jnp.where(s < nbc, b * nbc + jc, base + b * nbl + jl)
        return f

    in_specs = []
    for d in range(2):
        f = blk(d)
        rowspec = pl.BlockSpec((1, T, 512), lambda b, s, f=f, d=d: (d, f(b, s), 0))
        in_specs += [rowspec, rowspec, rowspec, rowspec,
                     pl.BlockSpec((1, nc, GDN_HEADS, C, C), lambda b, s, f=f, d=d: (d, f(b, s), 0, 0, 0)),
                     pl.BlockSpec((1, nc, 8, 512), lambda b, s, f=f, d=d: (d, f(b, s), 0, 0))]
    out_specs = [pl.BlockSpec((T, 512), lambda b, s, f=blk(d): (f(b, s), 0)) for d in range(2)]
    ops = (u, w, qg, kd, qk, egl)
    return pl.pallas_call(
        _gdn_scan_kernel,
        grid=(rw.B, nbc + nbl),
        in_specs=in_specs,
        out_specs=out_specs,
        out_shape=[jax.ShapeDtypeStruct((R, 512), BF16)] * 2,
        scratch_shapes=[pltpu.VMEM((2, GDN_HEADS, GDN_DK, GDN_DV), F32)],
        compiler_params=_cp("arbitrary", "arbitrary"),
        name="gdn_scan",
    )(*ops, *ops)


def _merge_kernel(yhc_ref, yhl_ref, sf_ref, sb_ref, sx_ref, sz_ref, dx_ref, snw_ref, gf_ref, gb_ref, gg_ref, gnw_ref,
                  g0_ref, g1_ref, g2_ref, w0_ref, w1_ref, w2_ref, wo_ref, xc_ref, xl_ref, mod_ref, o_ref,
                  ys_ref, yg_ref, *, nctx_blk):
    tm = xc_ref.shape[0]
    rp = 64
    for r in range(tm // rp):
        rs = slice(r * rp, (r + 1) * rp)
        y = (sf_ref[rs, :].astype(F32) + sb_ref[rs, :].astype(F32)
             + sx_ref[rs, :].astype(F32) * dx_ref[...])
        y = y * _silu(sz_ref[rs, :].astype(F32))
        parts = []
        for g in range(SSM_GROUPS):
            yg = y[:, g * SSM_GW:(g + 1) * SSM_GW]
            parts.append(yg * lax.rsqrt(jnp.mean(yg * yg, axis=-1, keepdims=True) + EPS))
        ys_ref[rs, :] = (jnp.concatenate(parts, axis=1) * snw_ref[...]).astype(BF16)
        o = gf_ref[rs, :].astype(F32) + gb_ref[rs, :].astype(F32)
        parts = []
        for h in range(GDN_HEADS):
            oh = o[:, h * 128:(h + 1) * 128]
            parts.append(oh * lax.rsqrt(jnp.mean(oh * oh, axis=-1, keepdims=True) + EPS))
        yg_ref[rs, :] = (jnp.concatenate(parts, axis=1) * gnw_ref[...]
                         * _silu(gg_ref[rs, :].astype(F32))).astype(BF16)
    is_ctx = pl.program_id(0) < nctx_blk
    yh = jnp.where(is_ctx, yhc_ref[...], yhl_ref[...])
    m = (_sigmoid(g0_ref[...].astype(F32)) * _dot(yh, w0_ref[...])
         + _sigmoid(g1_ref[...].astype(F32)) * _dot(ys_ref[...], w1_ref[...])
         + _sigmoid(g2_ref[...].astype(F32)) * _dot(yg_ref[...], w2_ref[...]))
    x = jnp.where(is_ctx, xc_ref[...], xl_ref[...])
    o_ref[...] = x + mod_ref[0, 2:3, :] * _dot(m.astype(BF16), wo_ref[...])


def merge(rw, l, yh, y_f, y_b, dx, ssm_nw, o_f, o_b, gdn_nw, p, w0, w1, w2, wo, xs, mod, skip_ctx=False):
    R = rw.R
    D = xs[0].shape[1]
    tm = min(rw.tm, 512)
    mi = rw.mod_index(tm)
    skip = rw.NC // tm if skip_ctx else 0
    yspec = pl.BlockSpec((tm, 512), lambda i: (i + skip, 0))
    pspec = lambda col: pl.BlockSpec((tm, 512), lambda i: (i + skip, col // 512))
    vec = pl.BlockSpec((1, 512), lambda i: (0, 0))
    gspec = lambda k: pl.BlockSpec((tm, D), lambda i: (i + skip, C_GATE // D + k))
    wspec = pl.BlockSpec((None, 512, D), lambda i: (l, 0, 0))
    return pl.pallas_call(
        functools.partial(_merge_kernel, nctx_blk=rw.NC // tm - skip),
        grid=(R // tm - skip,),
        in_specs=_stream_specs(rw, tm, yh, 1, skip) + [
                  yspec, yspec, pspec(C_XBC), pspec(C_Z), vec, vec,
                  yspec, yspec, pspec(C_GG), vec,
                  gspec(0), gspec(1), gspec(2), wspec, wspec, wspec,
                  pl.BlockSpec((None, D, D), lambda i: (l, 0, 0))]
                 + _stream_specs(rw, tm, xs, 1, skip)
                 + [pl.BlockSpec((None, 1, 8, D), lambda i: (l, mi(i + skip), 0, 0))],
        out_specs=pl.BlockSpec((tm, D), lambda i: (i + skip, 0)),
        out_shape=jax.ShapeDtypeStruct((R, D), F32),
        scratch_shapes=[pltpu.VMEM((tm, 512), BF16), pltpu.VMEM((tm, 512), BF16)],
        compiler_params=_cp("arbitrary"),
        name="merge",
    )(yh[0], yh[1], y_f, y_b, p, p, dx, ssm_nw, o_f, o_b, p, gdn_nw, p, p, p, w0, w1, w2, wo, xs[0], xs[1], mod)


def _swiglu_up_kernel(x_ref, nw_ref, mod_ref, wg_ref, wu_ref, o_ref, h_ref, g0_ref, g1_ref, u0_ref, u1_ref):
    @pl.when(pl.program_id(1) == 0)
    def _():
        h = _norm_mod(x_ref[...], nw_ref[...], mod_ref[0, 4:5, :], mod_ref[0, 3:4, :])
        h_ref[...] = h.astype(BF16)

    T, tn = o_ref.shape
    rows = g0_ref.shape[0]
    gs, us = (g0_ref, g1_ref), (u0_ref, u1_ref)

    def project(r):
        hh = h_ref[r * rows:(r + 1) * rows, :]
        gs[r % 2][...] = _dot(hh, wg_ref[...])
        us[r % 2][...] = _dot(hh, wu_ref[...])

    def finish(r):
        for q in range(rows // 64):
            for c in range(tn // 128):
                ps = (slice(q * 64, (q + 1) * 64), slice(c * 128, (c + 1) * 128))
                y = _silu(gs[r % 2][ps]) * us[r % 2][ps]
                o_ref[r * rows + q * 64:r * rows + (q + 1) * 64, ps[1]] = y.astype(o_ref.dtype)

    for r in range(T // rows):
        project(r)
        if r > 0:
            finish(r - 1)
    finish(T // rows - 1)


def swiglu_up(rw, l, x, nw, mod, wgu, skip_ctx=False):
    R, D = x.shape
    tm = min(rw.tm, 512)
    tn = D_FF
    nj = D_FF // tn
    mi = rw.mod_index(tm)
    skip = rw.NC // tm if skip_ctx else 0
    return pl.pallas_call(
        _swiglu_up_kernel,
        grid=(R // tm - skip, nj),
        in_specs=[
            pl.BlockSpec((tm, D), lambda i, j: (i + skip, 0)),
            pl.BlockSpec((None, 1, D), lambda i, j: (l, 0, 0)),
            pl.BlockSpec((None, 1, 8, D), lambda i, j: (l, mi(i + skip), 0, 0)),
            pl.BlockSpec((None, D, tn), lambda i, j: (l, 0, j)),
            pl.BlockSpec((None, D, tn), lambda i, j: (l, 0, nj + j)),
        ],
        out_specs=pl.BlockSpec((tm, tn), lambda i, j: (i + skip, j)),
        out_shape=jax.ShapeDtypeStruct((R, D_FF), BF16),
        scratch_shapes=[pltpu.VMEM((tm, D), BF16)] + [pltpu.VMEM((min(256, tm), tn), F32)] * 4,
        compiler_params=_cp("arbitrary", "arbitrary"),
        name="swiglu_up",
    )(x, nw, mod, wgu, wgu)


def _swiglu_down_kernel(a_ref, w_ref, x_ref, mod_ref, o_ref):
    o_ref[...] = x_ref[...] + mod_ref[0, 5:6, :] * _dot(a_ref[...], w_ref[...])


def swiglu_down(rw, l, a, w, x, mod, skip_ctx=False):
    R, D = x.shape
    tm = min(rw.tm, 512)
    mi = rw.mod_index(tm)
    skip = rw.NC // tm if skip_ctx else 0
    return pl.pallas_call(
        _swiglu_down_kernel,
        grid=(R // tm - skip,),
        in_specs=[
            pl.BlockSpec((tm, D_FF), lambda i: (i + skip, 0)),
            pl.BlockSpec((None, D_FF, D), lambda i: (l, 0, 0)),
            pl.BlockSpec((tm, D), lambda i: (i + skip, 0)),
            pl.BlockSpec((None, 1, 8, D), lambda i: (l, mi(i + skip), 0, 0)),
        ],
        out_specs=pl.BlockSpec((tm, D), lambda i: (i + skip, 0)),
        out_shape=jax.ShapeDtypeStruct((R, D), F32),
        compiler_params=_cp("arbitrary"),
        name="swiglu_down",
    )(a, w, x, mod)


def _final_norm_kernel(x_ref, w_ref, o_ref):
    x = x_ref[...]
    ms = jnp.mean(x * x, axis=-1, keepdims=True)
    o_ref[...] = x * lax.rsqrt(ms + EPS) * w_ref[...]


def final_norm(rw, x, w):
    D = x.shape[1]
    tm = rw.tm
    n0 = rw.NC // tm
    nl = rw.B * rw.Ll
    return pl.pallas_call(
        _final_norm_kernel,
        grid=(nl // tm,),
        in_specs=[pl.BlockSpec((tm, D), lambda i: (n0 + i, 0)), pl.BlockSpec((1, D), lambda i: (0, 0))],
        out_specs=pl.BlockSpec((tm, D), lambda i: (i, 0)),
        out_shape=jax.ShapeDtypeStruct((nl, D), F32),
        compiler_params=_cp("arbitrary"),
        name="final_norm",
    )(x, w.reshape(1, D))


def _regroup_w_in(w_in):
    o_dt = 3072
    o_gdn = 3088
    o_a = o_gdn + 2048
    o_b = o_a + 8
    o_gate = o_gdn + 2064
    wt = jnp.swapaxes(w_in, 1, 2).astype(BF16)
    pieces = [
        wt[:, 0:3072],
        wt[:, o_gdn:o_gdn + 2048],
        wt[:, o_gate:o_gate + 3072],
        wt[:, o_dt:o_dt + 16],
        wt[:, o_a:o_a + 4], wt[:, o_b:o_b + 4],
        wt[:, o_a + 4:o_a + 8], wt[:, o_b + 4:o_b + 8],
        jnp.zeros((wt.shape[0], N_IN_PAD - C_SM - 32, wt.shape[2]), wt.dtype),
    ]
    return jnp.swapaxes(jnp.concatenate(pieces, axis=1), 1, 2)


def kernel(x, c, ctx, c_ctx, w_ada, b_ada, norm1_w, norm2_w, w_in, hy_conv_w, hy_conv_b, hy_w1, hy_b1, hy_w2, hy_b2, hy_w3, hy_freq, hy_bias, ssm_conv_w, ssm_conv_b, ssm_dt_bias, ssm_A_log, ssm_D, ssm_norm_w, gdn_conv_w, gdn_dt_bias, gdn_A_log, gdn_norm_w, w_hy_out, w_ssm_out, w_gdn_out, w_out, w_gate_up, w_down, final_norm_w):
    B, Ll, D = x.shape
    Lc = ctx.shape[1]
    depth = w_ada.shape[0]
    assert Lc == CONV_ROWS and D == D_MODEL and B <= 15
    rw = Rows(B, Lc, Ll)
    R, NC = rw.R, rw.NC

    xs = (ctx.reshape(B * Lc, D), x.reshape(B * Ll, D))

    svec = jnp.concatenate([c_ctx[None, :], c, jnp.zeros((15 - B, D), F32)], axis=0)
    mod = ada_modulation(svec, w_ada, b_ada)
    mod = jnp.pad(mod.reshape(depth, 16, 6, D), ((0, 0), (0, 0), (0, 2), (0, 0)))

    w_in_r = _regroup_w_in(w_in)
    par = _in_proj_params(hy_conv_w, hy_conv_b, ssm_conv_w, ssm_conv_b, gdn_conv_w, ssm_dt_bias, gdn_dt_bias,
                          gdn_A_log)
    norm1 = norm1_w.reshape(depth, 1, D)
    norm2 = norm2_w.reshape(depth, 1, D)
    w_hy_o, w_ssm_o, w_gdn_o, w_o = (w.astype(BF16) for w in (w_hy_out, w_ssm_out, w_gdn_out, w_out))
    w_gu, w_dn = w_gate_up.astype(BF16), w_down.astype(BF16)
    dft_l = dft_tables_split(Ll)
    dft_c = dft_tables_split(Lc)
    feat_l, feat_c = hy_features(Ll), hy_features(Lc)

    for l in range(depth):
        p, sm = in_proj(rw, xs, l, norm1, mod, w_in_r, par)

        sm32_t = sm[:, :32].T
        dt_t = sm32_t[:16].reshape(2, 8, R)
        g_t = sm32_t[16:32].reshape(16, R // GDN_CHUNK, GDN_CHUNK).transpose(1, 0, 2)

        alx = jnp.repeat(ssm_A_log[l], SSM_HEAD_DIM, axis=-1).reshape(2, 1, 512)
        alc = ssm_A_log[l].reshape(2, 8, 1)
        y_f, y_b = ssd_scan(rw, p, sm, dt_t, alx, alc)
        dx = jnp.repeat(ssm_D[l], SSM_HEAD_DIM).reshape(1, 512)

        o_f, o_b = gdn_scan(rw, *gdn_prep(p, sm, g_t))

        last = l == depth - 1
        hyu = p
        parts = []
        for (Bn, L, blk0, (fwd, inv), feat) in ((B, Lc, 0, dft_c, feat_c), (B, Ll, NC // Ll, dft_l, feat_l)):
            if NC % L:
                raise ValueError("latent length must divide the context row count")
            if last and blk0 == 0:
                parts.append(None)
                continue
            filt = hy_filter(feat, hy_w1[l], hy_b1[l], hy_w2[l], hy_b2[l], hy_w3[l], hy_freq[l])
            kspec = filter_spectrum(fwd, *filt)
            z1 = long_conv(Bn, L, hyu, blk0, 0, hyu, blk0, 1, hy_bias[l, 0], fwd, inv, kspec, 0, F32)
            yy = long_conv(Bn, L, z1, 0, 0, hyu, blk0, 2, hy_bias[l, 1], fwd, inv, kspec, 1, BF16)
            parts.append(yy)
        y_hy = (parts[1] if last else parts[0], parts[1])

        xa = merge(rw, l, y_hy, y_f, y_b, dx, ssm_norm_w[l].reshape(1, 512),
                   o_f, o_b, jnp.tile(gdn_norm_w[l], GDN_HEADS).reshape(1, 512), p,
                   w_hy_o, w_ssm_o, w_gdn_o, w_o, xs, mod, skip_ctx=last)
        act = swiglu_up(rw, l, xa, norm2, mod, w_gu, skip_ctx=last)
        xa = swiglu_down(rw, l, act, w_dn, xa, mod, skip_ctx=last)
        xs = (xa, xa)

    out = final_norm(rw, xa, final_norm_w)
    return out.reshape(B, Ll, D)
```

```python
import functools
import math

import jax
import jax.numpy as jnp
import numpy as np
from jax import lax
from jax.experimental import pallas as pl
from jax.experimental.pallas import tpu as pltpu

F32 = jnp.float32
BF16 = jnp.bfloat16
HI = lax.Precision.HIGHEST

EPS = 1e-6
D_MODEL = 1024
GRID_W = 64

HY_WIDTH = 512
HY_BANDS = 16
HY_EMB = 1 + 2 * HY_BANDS
HY_HIDDEN = 64
HY_SHORT_DECAY_PCT = 0.3
HY_LONG_DECAY_PCT = 1.5
HY_TARGET = 1e-2

SSM_HEADS = 8
SSM_HEAD_DIM = 64
SSM_WIDTH = 512
SSM_GROUPS = 2
SSM_HPG = 4
SSM_STATE = 128
SSM_CHUNK = 128
SSM_GW = SSM_HPG * SSM_HEAD_DIM

GDN_HEADS = 4
GDN_DK = 128
GDN_DV = 128
GDN_CHUNK = 64

D_FF = 2816

C_HY = 0
C_Z = 1536
C_XBC = 2048
C_QKV = 3072
C_GG = 4608
C_GATE = 5120
C_SM = 8192

CONV_ROWS = 256
FREQ_BLK = 256

VMEM_LIMIT = 56 * 1024 * 1024


def _cp(*sem):
    return pltpu.CompilerParams(dimension_semantics=sem, vmem_limit_bytes=VMEM_LIMIT)


def _sigmoid(x):
    return 1.0 / (1.0 + jnp.exp(-x))


def _silu(x):
    return x * _sigmoid(x)


def _softplus(x):
    return jnp.maximum(x, 0.0) + jnp.log1p(jnp.exp(-jnp.abs(x)))


def _dot(a, b, precision=None):
    return jnp.dot(a, b, precision=precision, preferred_element_type=F32)


def _dot_nt(a, b):
    return lax.dot_general(a, b, (((1,), (1,)), ((), ())), preferred_element_type=F32)


def _dot_tn(a, b):
    return lax.dot_general(a, b, (((0,), (0,)), ((), ())), preferred_element_type=F32)


def _ada_kernel(s_ref, w_ref, b_ref, o_ref):
    s = _silu(s_ref[...])
    o_ref[0] = _dot(s, w_ref[0], HI) + b_ref[0]


def ada_modulation(svec, w_ada, b_ada):
    depth = w_ada.shape[0]
    D = D_MODEL
    return pl.pallas_call(
        _ada_kernel,
        grid=(depth, 6),
        in_specs=[
            pl.BlockSpec((16, D), lambda l, j: (0, 0)),
            pl.BlockSpec((1, D, D), lambda l, j: (l, 0, j)),
            pl.BlockSpec((1, 1, D), lambda l, j: (l, 0, j)),
        ],
        out_specs=pl.BlockSpec((1, 16, D), lambda l, j: (l, 0, j)),
        out_shape=jax.ShapeDtypeStruct((depth, 16, 6 * D), F32),
        compiler_params=_cp("arbitrary", "arbitrary"),
        name="ada",
    )(svec, w_ada, b_ada.reshape(depth, 1, 6 * D))


def _norm_mod(x, nw, scale, shift):
    ms = jnp.mean(x * x, axis=-1, keepdims=True)
    return (x * lax.rsqrt(ms + EPS) * nw) * (1.0 + scale) + shift


IN_FLIGHT = 2
N_IN_PAD = C_SM + 128
IN_TN = N_IN_PAD // 5
MODE_RAW, MODE_CONV, MODE_CONV_SILU, MODE_CONV_SILU_L2, MODE_SMALL = range(5)


def _tile_mode(tile):
    col = tile * 128
    if col < C_Z:
        return MODE_CONV
    if col < C_XBC:
        return MODE_RAW
    if col < C_QKV:
        return MODE_CONV_SILU
    if col < C_QKV + 1024:
        return MODE_CONV_SILU_L2
    if col < C_GG:
        return MODE_CONV_SILU
    if col < C_SM:
        return MODE_RAW
    return MODE_SMALL
PAR_W0, PAR_W1, PAR_W2, PAR_BIAS, PAR_L2SCALE, PAR_SBIAS, PAR_SALOG, PAR_SKIND = range(8)


def _in_proj_kernel(xc_ref, xl_ref, nw_ref, mod_ref, w_ref, par_ref, o_ref, sm_ref, h_ref, raw0_ref, raw1_ref, *,
                    nctx_blk):
    j = pl.program_id(1)
    nj = N_IN_PAD // IN_TN
    raws = (raw0_ref, raw1_ref)

    @pl.when((j == 0) & (pl.program_id(0) < nctx_blk))
    def _():
        h = _norm_mod(xc_ref[...], nw_ref[...], mod_ref[0, 1:2, :], mod_ref[0, 0:1, :])
        h_ref[...] = h.astype(BF16)

    @pl.when((j == 0) & (pl.program_id(0) >= nctx_blk))
    def _():
        h = _norm_mod(xl_ref[...], nw_ref[...], mod_ref[0, 1:2, :], mod_ref[0, 0:1, :])
        h_ref[...] = h.astype(BF16)

    T = h_ref.shape[0]
    G = GRID_W
    per_ctx = CONV_ROWS // G
    is_latent = pl.program_id(0) >= nctx_blk
    sub = lax.broadcasted_iota(jnp.int32, (8, 128), 0)

    def raw_piece(src, g, c):
        return src[g * G:(g + 1) * G, c * 128:(c + 1) * 128]

    retired = []

    def retire(y):
        bits = pltpu.bitcast(y[0:8], jnp.int32)
        zero = lax.shift_right_logical(lax.shift_right_logical(bits, 16), 16)
        retired.append(jnp.tile(zero.astype(F32), (G // 8, 1)))

    def conv(src, g, c):
        cs = slice(c * 128, (c + 1) * 128)
        x = raw_piece(src, g, c)
        if len(retired) >= IN_FLIGHT:
            x = x + retired[-IN_FLIGHT]
        zero = jnp.zeros((1, 128), F32)
        before = zero if g % per_ctx == 0 else jnp.where(is_latent, 0.0, src[g * G - 1:g * G, cs])
        after = zero if g % per_ctx == per_ctx - 1 else jnp.where(is_latent, 0.0, src[(g + 1) * G:(g + 1) * G + 1, cs])
        rp = pltpu.roll(x, 1, 0)
        rn = pltpu.roll(x, G - 1, 0)
        prev = jnp.concatenate([jnp.where(sub == 0, before, rp[0:8]), rp[8:]], axis=0)
        nxt = jnp.concatenate([rn[:G - 8], jnp.where(sub == 7, after, rn[G - 8:])], axis=0)
        return (prev * par_ref[PAR_W0:PAR_W0 + 1, cs] + x * par_ref[PAR_W1:PAR_W1 + 1, cs]
                + nxt * par_ref[PAR_W2:PAR_W2 + 1, cs] + par_ref[PAR_BIAS:PAR_BIAS + 1, cs])

    def conv_silu(src, g, c):
        return _silu(conv(src, g, c))

    def conv_silu_l2(src, g, c):
        y = _silu(conv(src, g, c))
        y = y * lax.rsqrt(jnp.sum(y * y, axis=-1, keepdims=True) + EPS)
        return y * par_ref[PAR_L2SCALE:PAR_L2SCALE + 1, c * 128:(c + 1) * 128]

    def small(src, g, c):
        cs = slice(c * 128, (c + 1) * 128)
        acc = raw_piece(src, g, c)
        kind = par_ref[PAR_SKIND:PAR_SKIND + 1, cs]
        sp = _softplus(acc + par_ref[PAR_SBIAS:PAR_SBIAS + 1, cs])
        dec = -jnp.exp(par_ref[PAR_SALOG:PAR_SALOG + 1, cs]) * sp
        return jnp.where(kind == 0.0, sp, jnp.where(kind == 1.0, dec, jnp.where(kind == 2.0, _sigmoid(acc), 0.0)))

    rows_mm = 256
    tiles = IN_TN // 128
    piece_fn = {MODE_RAW: raw_piece, MODE_CONV: conv, MODE_CONV_SILU: conv_silu,
                MODE_CONV_SILU_L2: conv_silu_l2, MODE_SMALL: small}

    heavy_modes = (MODE_CONV_SILU, MODE_CONV_SILU_L2)
    col_slices = [slice(c0, min(c0 + 256, IN_TN)) for c0 in range(0, IN_TN, 256)]

    def project(dst, r, cs=slice(None)):
        rs = slice(r * rows_mm, (r + 1) * rows_mm)
        dst[rs, cs] = _dot(h_ref[rs, :], w_ref[:, cs])

    def finish(src, blk, g, c):
        mode = _tile_mode(blk * tiles + c)
        y = piece_fn[mode](src, g, c)
        if mode in heavy_modes:
            retire(y)
        if mode == MODE_SMALL:
            sm_ref[g * G:(g + 1) * G, :] = y
            y = jnp.zeros_like(y)
        o_ref[g * G:(g + 1) * G, c * 128:(c + 1) * 128] = y.astype(o_ref.dtype)

    for step in range(nj + 1):
        @pl.when(j == step)
        def _(step=step):
            blk = step - 1
            src, dst = raws[blk % 2], raws[step % 2]
            retired.clear()
            light = blk >= 0 and not any(_tile_mode(blk * tiles + c) in heavy_modes for c in range(tiles))
            for r in range(T // rows_mm):
                pieces = [] if blk < 0 else [(g, c) for g in range(r * rows_mm // G, (r + 1) * rows_mm // G)
                                             for c in range(tiles)]
                if step == nj:
                    slabs = []
                elif light:
                    slabs = col_slices
                else:
                    slabs = [slice(None)]
                per = -(-len(pieces) // max(len(slabs), 1))
                for n in range(max(len(slabs), 1)):
                    if n < len(slabs):
                        project(dst, r, slabs[n])
                    for g, c in pieces[n * per:(n + 1) * per]:
                        finish(src, blk, g, c)


class Rows:
    def __init__(self, B, Lc, Ll):
        self.B, self.Lc, self.Ll = B, Lc, Ll
        self.NC = B * Lc
        self.R = B * Lc + B * Ll
        assert self.NC % Ll == 0 or Ll % self.NC == 0
        tm = 1024
        while self.NC % tm or Ll % tm:
            tm //= 2
        self.tm = tm

    def mod_index(self, tm):
        nctx = self.NC // tm
        per = self.Ll // tm
        return lambda i: jnp.where(i < nctx, 0, 1 + (i - nctx) // per)


def _stream_specs(rw, tm, xs, ngrid, skip=0):
    xc, xl = xs
    nctx = rw.NC // tm
    off = nctx if xl.shape[0] == rw.R else 0
    D = xc.shape[1]
    if ngrid == 1:
        return [pl.BlockSpec((tm, D), lambda i: (jnp.minimum(i + skip, nctx - 1), 0)),
                pl.BlockSpec((tm, D), lambda i: (jnp.maximum(i + skip - nctx, 0) + off, 0))]
    return [pl.BlockSpec((tm, D), lambda i, j: (jnp.minimum(i + skip, nctx - 1), 0)),
            pl.BlockSpec((tm, D), lambda i, j: (jnp.maximum(i + skip - nctx, 0) + off, 0))]


def in_proj(rw, xs, l, nw, mod, w, par):
    R = rw.R
    D = xs[0].shape[1]
    N = w.shape[2]
    tm, tn = rw.tm, IN_TN
    nj = N // tn
    assert N == N_IN_PAD
    mi = rw.mod_index(tm)
    done = lambda j: jnp.maximum(j - 1, 0)
    return pl.pallas_call(
        functools.partial(_in_proj_kernel, nctx_blk=rw.NC // tm),
        grid=(R // tm, nj + 1),
        in_specs=_stream_specs(rw, tm, xs, 2) + [
            pl.BlockSpec((None, 1, D), lambda i, j: (l, 0, 0)),
            pl.BlockSpec((None, 1, 8, D), lambda i, j: (l, mi(i), 0, 0)),
            pl.BlockSpec((None, D, tn), lambda i, j: (l, 0, jnp.minimum(j, nj - 1))),
            pl.BlockSpec((None, 8, tn), lambda i, j: (l, 0, done(j))),
        ],
        out_specs=[pl.BlockSpec((tm, tn), lambda i, j: (i, done(j))),
                   pl.BlockSpec((tm, 128), lambda i, j: (i, 0))],
        out_shape=[jax.ShapeDtypeStruct((R, N), BF16), jax.ShapeDtypeStruct((R, 128), F32)],
        scratch_shapes=[pltpu.VMEM((tm, D), BF16), pltpu.VMEM((tm, tn), F32), pltpu.VMEM((tm, tn), F32)],
        compiler_params=_cp("arbitrary", "arbitrary"),
        name="in_proj",
    )(xs[0], xs[1], nw, mod, w, par)


def _in_proj_params(hy_conv_w, hy_conv_b, ssm_conv_w, ssm_conv_b, gdn_conv_w, ssm_dt_bias, gdn_dt_bias, gdn_A_log):
    depth = hy_conv_w.shape[0]

    def row(pieces):
        out, pos = [], 0
        for off, a in pieces:
            out += [jnp.zeros((depth, off - pos), F32), a.astype(F32)]
            pos = off + a.shape[1]
        return jnp.concatenate(out + [jnp.zeros((depth, N_IN_PAD - pos), F32)], axis=1)
    z4 = jnp.zeros((depth, 4), F32)
    conv = [row([(C_HY, hy_conv_w[:, t]), (C_XBC, ssm_conv_w[:, t]), (C_QKV, gdn_conv_w[:, t])]) for t in range(3)]
    bias = row([(C_HY, hy_conv_b), (C_XBC, ssm_conv_b)])
    l2s = row([(C_QKV, jnp.full((depth, 512), GDN_DK ** -0.5, F32)), (C_QKV + 512, jnp.ones((depth, 512), F32))])
    sbias = row([(C_SM, jnp.concatenate([ssm_dt_bias.reshape(depth, 16), gdn_dt_bias[:, 0], z4,
                                         gdn_dt_bias[:, 1], z4], axis=1))])
    salog = row([(C_SM + 16, jnp.concatenate([gdn_A_log[:, 0], z4, gdn_A_log[:, 1], z4], axis=1))])
    kind = np.full((depth, N_IN_PAD), 3.0, np.float32)
    kind[:, C_SM:C_SM + 16] = 0.0
    kind[:, C_SM + 16:C_SM + 20] = 1.0
    kind[:, C_SM + 24:C_SM + 28] = 1.0
    kind[:, C_SM + 20:C_SM + 24] = 2.0
    kind[:, C_SM + 28:C_SM + 32] = 2.0
    return jnp.stack(conv + [bias, l2s, sbias, salog, jnp.asarray(kind)], axis=1)


def _hy_filter_kernel(z_ref, w1_ref, b1_ref, w2_ref, b2_ref, w3_ref, f0_ref, f1_ref, win_ref, oe_ref, oo_ref,
                      h_ref, split_ref):
    @pl.when(pl.program_id(1) == 0)
    def _():
        h1 = jnp.sin(f0_ref[...] * (_dot(z_ref[...], w1_ref[...], HI) + b1_ref[...]))
        h_ref[...] = jnp.sin(f1_ref[...] * (_dot(h1, w2_ref[...], HI) + b2_ref[...]))

    a1 = h_ref[...].astype(BF16)
    a2 = (h_ref[...] - a1.astype(F32)).astype(BF16)
    b1 = w3_ref[...].astype(BF16)
    b2 = (w3_ref[...] - b1.astype(F32)).astype(BF16)
    h = (_dot(a1, b1) + _dot(a1, b2) + _dot(a2, b1)) * win_ref[...]
    tl = h.shape[0]
    row = lax.broadcasted_iota(jnp.int32, (tl, 1), 0) + pl.program_id(0) * tl
    drop = (row == 0) & (pl.program_id(1) % 2 == 1)
    h = jnp.where(drop, 0.0, h)
    for c in range(h.shape[1] // 128):
        cs = slice(c * 128, (c + 1) * 128)
        s_c = split_ref.at[c]
        s_c[...] = h[:, cs]
        oe_ref[:, cs] = s_c[pl.ds(0, tl // 2, stride=2), :].astype(oe_ref.dtype)
        oo_ref[:, cs] = s_c[pl.ds(1, tl // 2, stride=2), :].astype(oo_ref.dtype)


def hy_features(L):
    t = jnp.linspace(0.0, 1.0, L, dtype=F32)[:, None]
    w = 2.0 * math.pi * jnp.arange(L, dtype=F32)[:, None] / L
    f = jnp.linspace(1e-4, HY_BANDS - 1, HY_BANDS, dtype=F32)[None, :]
    z = jnp.concatenate([t, jnp.cos(f * w), -jnp.sin(f * w)], axis=-1)
    z = jnp.pad(z, ((0, 0), (0, 128 - HY_EMB)))
    min_decay = math.log(HY_TARGET) / HY_LONG_DECAY_PCT
    max_decay = math.log(HY_TARGET) / HY_SHORT_DECAY_PCT
    deltas = jnp.linspace(min_decay, max_decay, HY_WIDTH, dtype=F32)
    window = jnp.exp(-t * jnp.abs(deltas))
    return z, window


def hy_filter(feat, w1, b1, w2, b2, w3, freq):
    z, window = feat
    L = z.shape[0]
    H = HY_HIDDEN
    w1p = jnp.pad(w1, ((0, 128 - HY_EMB), (0, 128 - H)))
    w2p = jnp.pad(w2, ((0, 128 - H), (0, 128 - H)))
    w3p = jnp.pad(w3, ((0, 128 - H), (0, 0)))
    pad1 = lambda v: jnp.pad(v, (0, 128 - H)).reshape(1, 128)
    tl = 256
    full = lambda shape: pl.BlockSpec(shape, lambda i, j: (0, 0))
    return pl.pallas_call(
        _hy_filter_kernel,
        grid=(L // tl, 4),
        in_specs=[
            pl.BlockSpec((tl, 128), lambda i, j: (i, 0)),
            full((128, 128)), full((1, 128)), full((128, 128)), full((1, 128)),
            pl.BlockSpec((128, HY_WIDTH), lambda i, j: (0, j)),
            full((1, 128)), full((1, 128)),
            pl.BlockSpec((tl, HY_WIDTH), lambda i, j: (i, 0)),
        ],
        out_specs=[pl.BlockSpec((tl // 2, HY_WIDTH), lambda i, j: (i, j))] * 2,
        out_shape=[jax.ShapeDtypeStruct((L // 2, 4 * HY_WIDTH), BF16)] * 2,
        scratch_shapes=[pltpu.VMEM((tl, 128), F32), pltpu.VMEM((HY_WIDTH // 128, tl, 128), F32)],
        compiler_params=_cp("arbitrary", "arbitrary"),
        name="hy_filter",
    )(z, w1p, pad1(b1), w2p, pad1(b2), w3p, pad1(freq[0]), pad1(freq[1]), window)


def dft_tables_split(L):
    N = 2 * L
    H = L // 2
    q = np.arange(H, dtype=np.int64)[:, None]
    m = np.arange(H, dtype=np.int64)[None, :]
    ang_e = ((q * 2 * m) % N).astype(np.float64) * (2.0 * math.pi / N)
    ang_o = ((q * (2 * m + 1)) % N).astype(np.float64) * (2.0 * math.pi / N)
    alt = (1 - 2 * (m % 2)).astype(np.float64)
    ce, co = np.cos(ang_e), np.cos(ang_o)
    se = np.where(q == 0, alt, -np.sin(ang_e))
    so = np.where(q == 0, -alt, -np.sin(ang_o))
    w = np.where(q == 0, 1.0, 2.0) / N
    ise = np.where(q == 0, 2.0 / N * alt, -np.sin(ang_e) * w)
    iso = np.where(q == 0, -2.0 / N * alt, -np.sin(ang_o) * w)
    fwd = np.stack([ce, co, se, so])
    inv = np.stack([(ce * w).T, ise.T, (co * w).T, iso.T])
    return jnp.asarray(fwd, dtype=BF16), jnp.asarray(inv, dtype=BF16)


def _filter_spectrum_kernel(fwd_ref, he_ref, ho_ref, rlo_ref, rhi_ref, ilo_ref, ihi_ref):
    he, ho = he_ref[...], ho_ref[...]
    ae, ao = _dot(fwd_ref[0], he), _dot(fwd_ref[1], ho)
    be, bo = _dot(fwd_ref[2], he), _dot(fwd_ref[3], ho)
    rlo_ref[...] = ae + ao
    rhi_ref[...] = ae - ao
    first = (lax.broadcasted_iota(jnp.int32, (fwd_ref.shape[1], 1), 0) == 0) & (pl.program_id(0) == 0)
    ilo_ref[...] = jnp.where(first, be, be + bo)
    ihi_ref[...] = jnp.where(first, bo, bo - be)


def filter_spectrum(fwd, taps_even, taps_odd):
    H, N = taps_even.shape
    C = HY_WIDTH
    FB = min(FREQ_BLK, H)
    tap = pl.BlockSpec((H, C), lambda f, j: (0, j))
    out = pl.BlockSpec((FB, C), lambda f, j: (f, j))
    return pl.pallas_call(
        _filter_spectrum_kernel,
        grid=(H // FB, N // C),
        in_specs=[pl.BlockSpec((4, FB, H), lambda f, j: (0, f, 0)), tap, tap],
        out_specs=[out] * 4,
        out_shape=[jax.ShapeDtypeStruct((H, N), F32)] * 4,
        compiler_params=_cp("arbitrary", "arbitrary"),
        name="filter_spectrum",
    )(fwd, taps_even, taps_odd)


def _long_conv_kernel(u_ref, g_ref, bias_ref, fwd_ref, inv_ref, ar0_ref, ar1_ref, ar0h_ref, ar1h_ref,
                      ai0_ref, ai1_ref, ai0h_ref, ai1h_ref, o_ref, ue_ref, uo_ref, acce_ref, acco_ref, y_ref):
    f = pl.program_id(1)
    half = ue_ref.shape[0]

    lane_tiles = [slice(c * 128, (c + 1) * 128) for c in range(y_ref.shape[0])]

    @pl.when(f == 0)
    def _():
        for c, cs in enumerate(lane_tiles):
            y_c = y_ref.at[c]
            y_c[...] = u_ref[:, cs].astype(F32)
            ue_ref[:, cs] = y_c[pl.ds(0, half, stride=2), :].astype(BF16)
            uo_ref[:, cs] = y_c[pl.ds(1, half, stride=2), :].astype(BF16)
        acce_ref[...] = jnp.zeros_like(acce_ref)
        acco_ref[...] = jnp.zeros_like(acco_ref)

    ue, uo = ue_ref[...], uo_ref[...]
    ae, ao = _dot(fwd_ref[0], ue), _dot(fwd_ref[1], uo)
    be, bo = _dot(fwd_ref[2], ue), _dot(fwd_ref[3], uo)
    ur, ur2 = ae + ao, ae - ao
    ui, ui2 = be + bo, bo - be
    first = (lax.broadcasted_iota(jnp.int32, (fwd_ref.shape[1], 1), 0) == 0) & (f == 0)
    kr, kr2 = ar0_ref[...] + ar1_ref[...], ar0h_ref[...] + ar1h_ref[...]
    ki = jnp.where(first, ai0_ref[...] + ai1_ref[...], ai0_ref[...] - ai1_ref[...])
    ki2 = ai0h_ref[...] - ai1h_ref[...]
    pr, pi = ur * kr - ui * ki, ur * ki + ui * kr
    pr2, pi2 = ur2 * kr2 - ui2 * ki2, ur2 * ki2 + ui2 * kr2
    dc, ny = ur * kr, ur2 * kr2
    gr = jnp.where(first, dc + ny, pr + pr2)
    gi = jnp.where(first, be * ki - bo * ki2, pi - pi2)
    hr = jnp.where(first, dc - ny, pr - pr2)
    hi = jnp.where(first, be * ki2 + bo * ki, pi + pi2)
    acce_ref[...] += _dot(inv_ref[0], gr.astype(BF16)) + _dot(inv_ref[1], gi.astype(BF16))
    acco_ref[...] += _dot(inv_ref[2], hr.astype(BF16)) + _dot(inv_ref[3], hi.astype(BF16))

    @pl.when(f == pl.num_programs(1) - 1)
    def _():
        for c, cs in enumerate(lane_tiles):
            y_c = y_ref.at[c]
            y_c[pl.ds(0, half, stride=2), :] = acce_ref[:, cs]
            y_c[pl.ds(1, half, stride=2), :] = acco_ref[:, cs]
            u = u_ref[:, cs].astype(F32)
            o_ref[:, cs] = (g_ref[:, cs].astype(F32) * (y_c[...] + u * bias_ref[:, cs])).astype(o_ref.dtype)


def long_conv(B, L, u, u_rb0, u_cb, gate, g_rb0, gate_cb, bias, fwd, inv, kspec, order, out_dtype):
    C = HY_WIDTH
    H = L // 2
    FB = min(FREQ_BLK, H)
    nfb = H // FB
    kblk = lambda part, d: pl.BlockSpec((FB, C), lambda b, f: (f, 2 * order + d))
    kops = [kspec[part] for part in range(4) for _ in range(2)]
    return pl.pallas_call(
        _long_conv_kernel,
        grid=(B, nfb),
        in_specs=[
            pl.BlockSpec((L, C), lambda b, f: (u_rb0 + b, u_cb)),
            pl.BlockSpec((L, C), lambda b, f: (g_rb0 + b, gate_cb)),
            pl.BlockSpec((1, C), lambda b, f: (0, 0)),
            pl.BlockSpec((4, FB, H), lambda b, f: (0, f, 0)),
            pl.BlockSpec((4, H, FB), lambda b, f: (0, 0, f)),
            kblk(0, 0), kblk(0, 1), kblk(1, 0), kblk(1, 1), kblk(2, 0), kblk(2, 1), kblk(3, 0), kblk(3, 1),
        ],
        out_specs=pl.BlockSpec((L, C), lambda b, f: (b, 0)),
        out_shape=jax.ShapeDtypeStruct((B * L, C), out_dtype),
        scratch_shapes=[pltpu.VMEM((H, C), BF16), pltpu.VMEM((H, C), BF16),
                        pltpu.VMEM((H, C), F32), pltpu.VMEM((H, C), F32), pltpu.VMEM((C // 128, L, 128), F32)],
        compiler_params=_cp("arbitrary", "arbitrary"),
        name="long_conv",
    )(u, gate, bias.reshape(1, C), fwd, inv, *kops)


def _scan_blocks(rw, rows):
    nbc, nbl, base = rw.Lc // rows, rw.Ll // rows, rw.NC // rows

    def make(d):
        def f(b, s):
            jc = s if d == 0 else nbc - 1 - s
            jl = (s - nbc) if d == 0 else nbl - 1 - (s - nbc)
            return jnp.where(s < nbc, b * nbc + jc, base + b * nbl + jl)
        return f

    return [make(0), make(1)], nbc + nbl


def _expand_lanes(x, base, n, width):
    rows = x.shape[0]
    per = 128 // width
    lane = lax.broadcasted_iota(jnp.int32, (rows, 128), 1)
    tiles = []
    for t in range(n // per):
        c0 = base + t * per
        tile = jnp.broadcast_to(x[:, c0:c0 + 1], (rows, 128))
        for i in range(1, per):
            tile = jnp.where(lane >= i * width, jnp.broadcast_to(x[:, c0 + i:c0 + i + 1], (rows, 128)), tile)
        tiles.append(tile)
    return jnp.concatenate(tiles, axis=1)


def _ssd_kernel(xf, bf, cf, smf, dtf, xb, bb, cb_, smb, dtb, alx_ref, alc_ref, of_ref, ob_ref, h_ref):
    Q = SSM_CHUNK
    GW = SSM_GW

    @pl.when(pl.program_id(1) == 0)
    def _():
        h_ref[...] = jnp.zeros_like(h_ref)

    row = lax.broadcasted_iota(jnp.int32, (Q, Q), 0)
    col = lax.broadcasted_iota(jnp.int32, (Q, Q), 1)
    lane_head = lax.broadcasted_iota(jnp.int32, (Q, GW), 1) // SSM_HEAD_DIM
    dirs = ((xf, bf, cf, smf, dtf, of_ref), (xb, bb, cb_, smb, dtb, ob_ref))
    jobs = []
    for d in range(2):
        x_ref, b_ref, c_ref, sm_ref, dt_ref, o_ref = dirs[d]
        keep = (col <= row) if d == 0 else (col >= row)
        tri = keep.astype(BF16)
        tri_t = ((row <= col) if d == 0 else (row >= col)).astype(BF16)
        sm = sm_ref[...]
        a_x = -jnp.exp(alx_ref[d])
        dtx = _expand_lanes(sm, 8 * d, SSM_HEADS, SSM_HEAD_DIM)
        cumx = _expand_lanes(_dot_01_lhs(tri, sm), 8 * d, SSM_HEADS, SSM_HEAD_DIM) * a_x
        cumr = _dot_01_rhs(dt_ref[0], tri_t) * (-jnp.exp(alc_ref[d]))
        last = Q - 1 if d == 0 else 0
        totx = cumx[last:last + 1, :]
        xd = x_ref[...].astype(F32) * dtx
        xdw = xd * jnp.exp(totx - cumx)
        ecum = jnp.exp(cumx)
        for g in range(SSM_GROUPS):
            gs = slice(g * GW, (g + 1) * GW)
            jobs.append(dict(d=d, g=g, gs=gs, keep=keep, cumx=cumx, cumr=cumr, o_ref=o_ref,
                             bg=b_ref[:, g * SSM_STATE:(g + 1) * SSM_STATE].astype(BF16),
                             cg=c_ref[:, g * SSM_STATE:(g + 1) * SSM_STATE].astype(BF16),
                             xdg=xd[:, gs], xdw=xdw[:, gs].astype(BF16), ecum=ecum[:, gs],
                             etot=jnp.exp(totx[:, gs])))
    for j in jobs:
        j["cb"] = _dot_nt(j["cg"], j["bg"])
        j["h"] = h_ref[j["d"], j["g"]]
    for j in jobs:
        ms, xs = [], []
        for e4 in range(SSM_HPG):
            e = j["g"] * SSM_HPG + e4
            diff = j["cumx"][:, e * SSM_HEAD_DIM:e * SSM_HEAD_DIM + 1] - j["cumr"][e:e + 1, :]
            ms.append((j["cb"] * jnp.where(j["keep"], jnp.exp(diff), 0.0)).astype(BF16))
            xs.append(jnp.where(lane_head == e4, j["xdg"], 0.0).astype(BF16))
        yd = _dot(jnp.concatenate(ms, axis=1), jnp.concatenate(xs, axis=0))
        y_off = _dot(j["cg"], j["h"].astype(BF16)) * j["ecum"]
        j["o_ref"][:, j["gs"]] = (yd + y_off).astype(BF16)
    for j in jobs:
        h_ref[j["d"], j["g"]] = j["h"] * j["etot"] + _dot_tn(j["bg"], j["xdw"])


def ssd_scan(rw, p, sm, dtT, alx, alc):
    Q = SSM_CHUNK
    blks, nsteps = _scan_blocks(rw, Q)
    R = p.shape[0]
    in_specs = []
    for d in range(2):
        f = blks[d]
        in_specs += [
            pl.BlockSpec((Q, 512), lambda b, s, f=f: (f(b, s), C_XBC // 512)),
            pl.BlockSpec((Q, 256), lambda b, s, f=f: (f(b, s), C_XBC // 256 + 2)),
            pl.BlockSpec((Q, 256), lambda b, s, f=f: (f(b, s), C_XBC // 256 + 3)),
            pl.BlockSpec((Q, 128), lambda b, s, f=f: (f(b, s), 0)),
            pl.BlockSpec((1, 8, Q), lambda b, s, f=f, d=d: (d, 0, f(b, s))),
        ]
    in_specs += [pl.BlockSpec((2, 1, 512), lambda b, s: (0, 0, 0)), pl.BlockSpec((2, 8, 1), lambda b, s: (0, 0, 0))]
    ops = (p, p, p, sm, dtT)
    return pl.pallas_call(
        _ssd_kernel,
        grid=(rw.B, nsteps),
        in_specs=in_specs,
        out_specs=[pl.BlockSpec((Q, 512), lambda b, s, f=blks[d]: (f(b, s), 0)) for d in range(2)],
        out_shape=[jax.ShapeDtypeStruct((R, 512), BF16)] * 2,
        scratch_shapes=[pltpu.VMEM((2, SSM_GROUPS, SSM_STATE, SSM_GW), F32)],
        compiler_params=_cp("arbitrary", "arbitrary"),
        name="ssd_scan",
    )(*ops, *ops, alx, alc)


def _split3(x):
    x1 = x.astype(BF16)
    r = x - x1.astype(F32)
    x2 = r.astype(BF16)
    x3 = (r - x2.astype(F32)).astype(BF16)
    return x1, x2, x3


def _dot_01_lhs(m01, x):
    x1, x2, x3 = _split3(x)
    return _dot(m01, x1) + _dot(m01, x2) + _dot(m01, x3)


def _dot_01_rhs(x, m01):
    x1, x2, x3 = _split3(x)
    return _dot(x1, m01) + _dot(x2, m01) + _dot(x3, m01)


GDN_ROWS = 256


def _gdn_prep_kernel(q_ref, k_ref, v_ref, sm_ref, gT_ref, u_ref, w_ref, qg_ref, kd_ref, qk_ref, egl_ref):
    C = GDN_CHUNK
    row = lax.broadcasted_iota(jnp.int32, (C, C), 0)
    col = lax.broadcasted_iota(jnp.int32, (C, C), 1)
    lane2 = lax.broadcasted_iota(jnp.int32, (C, 2 * C), 1)
    row2 = lax.broadcasted_iota(jnp.int32, (C, 2 * C), 0)
    col2 = lane2 & (C - 1)
    left = lane2 < C
    lane_k = lax.broadcasted_iota(jnp.int32, (C, 2 * GDN_DK), 1) < GDN_DK
    jobs = []
    for d in range(2):
        keep = (col <= row) if d == 0 else (col >= row)
        tri = keep.astype(BF16)
        tri_t2 = ((row2 <= col2) if d == 0 else (row2 >= col2)).astype(BF16)
        last = C - 1 if d == 0 else 0
        for c in range(GDN_ROWS // C):
            rows = slice(c * C, (c + 1) * C)
            smc = sm_ref[rows, :]
            cums = _dot_01_lhs(tri, smc)
            cumr2 = _dot_01_rhs(gT_ref[c, 8 * d:8 * d + 8, :], tri_t2)
            tot = cums[last:last + 1, :]
            for h in range(GDN_HEADS):
                lg = 16 + 8 * d + h
                jobs.append(dict(d=d, c=c, h=h, rows=rows, hs=slice(h * 128, (h + 1) * 128), keep=keep,
                                 gc=cums[:, lg:lg + 1], beta=smc[:, lg + 4:lg + 5],
                                 gl=tot[:, lg:lg + 1], gr=cumr2[h:h + 1, :C], gr2=cumr2[h:h + 1, :]))

    def block_diag(x):
        return jnp.concatenate([jnp.where(left, x, 0.0), jnp.where(left, 0.0, x)], axis=0).astype(BF16)

    pairs = []
    for i in range(0, len(jobs), 2):
        j0, j1 = jobs[i], jobs[i + 1]
        d, rows = j0["d"], j0["rows"]
        ps = slice(j0["hs"].start, j1["hs"].stop)
        keep2 = (col2 <= row2) if d == 0 else (col2 >= row2)
        late2, early2 = (row2, col2) if d == 0 else (col2, row2)
        lev2 = [(((row2 ^ col2) >> (t + 1)) == 0) & ((late2 & (1 << t)) != 0) & ((early2 & (1 << t)) == 0)
                for t in range(6)]
        gcp = jnp.where(left, j0["gc"], j1["gc"])
        grp = jnp.where(left[0:1], j0["gr2"], j1["gr2"])
        decp = jnp.where(keep2, jnp.exp(gcp - grp), 0.0)
        kp = k_ref[rows, ps].astype(F32)
        kbp = kp * jnp.where(lane_k, j0["beta"], j1["beta"])
        kstack = jnp.concatenate([jnp.where(lane_k, kp, 0.0), jnp.where(lane_k, 0.0, kp)], axis=0).astype(BF16)
        a = _dot_nt(kbp.astype(BF16), kstack) * decp
        pairs.append(dict(j0=j0, j1=j1, a=a, lev=lev2, n=-jnp.where(lev2[0], a, 0.0)))
    for j in jobs:
        q = q_ref[j["rows"], j["hs"]].astype(BF16)
        k = k_ref[j["rows"], j["hs"]].astype(BF16)
        dec = jnp.where(j["keep"], jnp.exp(j["gc"] - j["gr"]), 0.0)
        qk_ref[j["d"], j["c"], j["h"]] = (_dot_nt(q, k) * dec).astype(BF16)
    for lev in range(1, 6):
        for pr in pairs:
            l = jnp.where(pr["lev"][lev], pr["a"], 0.0)
            pr["y"] = l + _dot(l.astype(BF16), block_diag(pr["n"]))
        for pr in pairs:
            pr["n"] = pr["n"] - pr["y"] - _dot(pr["n"].astype(BF16), block_diag(pr["y"]))
    for pr in pairs:
        rhs_pair = []
        for j in (pr["j0"], pr["j1"]):
            rows, hs, beta = j["rows"], j["hs"], j["beta"]
            k = k_ref[rows, hs].astype(F32)
            j["eg"] = jnp.exp(j["gc"])
            j["rhs"] = jnp.concatenate([v_ref[rows, hs].astype(F32) * beta, k * beta * j["eg"]], axis=1)
            rhs_pair.append(j["rhs"])
        rstack = jnp.concatenate(rhs_pair, axis=0).astype(BF16)
        pr["j0"]["cor"] = _dot(jnp.where(left, pr["n"], 0.0).astype(BF16), rstack)
        pr["j1"]["cor"] = _dot(jnp.where(left, 0.0, pr["n"]).astype(BF16), rstack)
    for j in jobs:
        d, rows, hs, gc, gl, eg = j["d"], j["rows"], j["hs"], j["gc"], j["gl"], j["eg"]
        q = q_ref[rows, hs].astype(F32)
        k = k_ref[rows, hs].astype(F32)
        sol = j["rhs"] + j["cor"]
        u_ref[d, rows, hs] = sol[:, :GDN_DV].astype(BF16)
        w_ref[d, rows, hs] = sol[:, GDN_DV:].astype(BF16)
        qg_ref[d, rows, hs] = (q * eg).astype(BF16)
        kd_ref[d, rows, hs] = (k * jnp.exp(gl - gc)).astype(BF16)
        egl_ref[d, j["c"], :, hs] = jnp.broadcast_to(jnp.exp(gl), (8, 128))


def gdn_prep(p, sm, gT):
    R = p.shape[0]
    T, C = GDN_ROWS, GDN_CHUNK
    nc = T // C
    col = lambda k: pl.BlockSpec((T, 512), lambda i: (i, C_QKV // 512 + k))
    dirrow = pl.BlockSpec((2, T, 512), lambda i: (0, i, 0))
    return pl.pallas_call(
        _gdn_prep_kernel,
        grid=(R // T,),
        in_specs=[col(0), col(1), col(2),
                  pl.BlockSpec((T, 128), lambda i: (i, 0)),
                  pl.BlockSpec((nc, 16, C), lambda i: (i, 0, 0))],
        out_specs=[dirrow, dirrow, dirrow, dirrow,
                   pl.BlockSpec((2, nc, GDN_HEADS, C, C), lambda i: (0, i, 0, 0, 0)),
                   pl.BlockSpec((2, nc, 8, 512), lambda i: (0, i, 0, 0))],
        out_shape=[jax.ShapeDtypeStruct((2, R, 512), BF16),
                   jax.ShapeDtypeStruct((2, R, 512), BF16),
                   jax.ShapeDtypeStruct((2, R, 512), BF16),
                   jax.ShapeDtypeStruct((2, R, 512), BF16),
                   jax.ShapeDtypeStruct((2, R // C, GDN_HEADS, C, C), BF16),
                   jax.ShapeDtypeStruct((2, R // C, 8, 512), F32)],
        compiler_params=_cp("arbitrary"),
        name="gdn_prep",
    )(p, p, p, sm, gT)


def _gdn_scan_kernel(uf, wf, qgf, kdf, qkf, eglf, ub, wb, qgb, kdb, qkb, eglb, of_ref, ob_ref, s_ref):
    C = GDN_CHUNK
    nch = GDN_ROWS // C

    @pl.when(pl.program_id(1) == 0)
    def _():
        s_ref[...] = jnp.zeros_like(s_ref)

    dirs = ((uf, wf, qgf, kdf, qkf, eglf, of_ref), (ub, wb, qgb, kdb, qkb, eglb, ob_ref))
    chains = [(d, h) for d in range(2) for h in range(GDN_HEADS)]
    S = {ch: s_ref[ch[0], ch[1]] for ch in chains}
    for i in range(nch):
        Sb, vnb, rows_of, c_of = {}, {}, {}, {}
        for d, h in chains:
            c_of[d] = i if d == 0 else nch - 1 - i
            rows_of[d] = slice(c_of[d] * C, (c_of[d] + 1) * C)
        for d, h in chains:
            hs = slice(h * 128, (h + 1) * 128)
            Sb[d, h] = S[d, h].astype(BF16)
            v_new = dirs[d][0][0, rows_of[d], hs].astype(F32) - _dot(dirs[d][1][0, rows_of[d], hs], Sb[d, h])
            vnb[d, h] = v_new.astype(BF16)
        for d, h in chains:
            hs = slice(h * 128, (h + 1) * 128)
            u_ref, w_ref, qg_ref, kd_ref, qk_ref, egl_ref, o_ref = dirs[d]
            S[d, h] = S[d, h] * egl_ref[0, c_of[d], 0:1, hs] + _dot_tn(kd_ref[0, rows_of[d], hs], vnb[d, h])
        for d, h in chains:
            hs = slice(h * 128, (h + 1) * 128)
            u_ref, w_ref, qg_ref, kd_ref, qk_ref, egl_ref, o_ref = dirs[d]
            o_ref[rows_of[d], hs] = (_dot(qg_ref[0, rows_of[d], hs], Sb[d, h])
                                     + _dot(qk_ref[0, c_of[d], h], vnb[d, h])).astype(BF16)
    for ch in chains:
        s_ref[ch[0], ch[1]] = S[ch]


def gdn_scan(rw, u, w, qg, kd, qk, egl):
    T, C = GDN_ROWS, GDN_CHUNK
    nc = T // C
    R = u.shape[1]
    nbc, nbl, base = rw.Lc // T, rw.Ll // T, rw.NC // T

    def blk(d):
        def f(b, s):
            jc = s if d == 0 else nbc - 1 - s
            jl = (s - nbc) if d == 0 else nbl - 1 - (s - nbc)
            return jnp.where(s < nbc, b * nbc + jc, base + b * nbl + jl)
        return f

    in_specs = []
    for d in range(2):
        f = blk(d)
        rowspec = pl.BlockSpec((1, T, 512), lambda b, s, f=f, d=d: (d, f(b, s), 0))
        in_specs += [rowspec, rowspec, rowspec, rowspec,
                     pl.BlockSpec((1, nc, GDN_HEADS, C, C), lambda b, s, f=f, d=d: (d, f(b, s), 0, 0, 0)),
                     pl.BlockSpec((1, nc, 8, 512), lambda b, s, f=f, d=d: (d, f(b, s), 0, 0))]
    out_specs = [pl.BlockSpec((T, 512), lambda b, s, f=blk(d): (f(b, s), 0)) for d in range(2)]
    ops = (u, w, qg, kd, qk, egl)
    return pl.pallas_call(
        _gdn_scan_kernel,
        grid=(rw.B, nbc + nbl),
        in_specs=in_specs,
        out_specs=out_specs,
        out_shape=[jax.ShapeDtypeStruct((R, 512), BF16)] * 2,
        scratch_shapes=[pltpu.VMEM((2, GDN_HEADS, GDN_DK, GDN_DV), F32)],
        compiler_params=_cp("arbitrary", "arbitrary"),
        name="gdn_scan",
    )(*ops, *ops)


def _merge_kernel(yhc_ref, yhl_ref, sf_ref, sb_ref, sx_ref, sz_ref, dx_ref, snw_ref, gf_ref, gb_ref, gg_ref, gnw_ref,
                  g0_ref, g1_ref, g2_ref, w0_ref, w1_ref, w2_ref, wo_ref, xc_ref, xl_ref, mod_ref, o_ref,
                  ys_ref, yg_ref, *, nctx_blk):
    tm = xc_ref.shape[0]
    rp = 64
    for r in range(tm // rp):
        rs = slice(r * rp, (r + 1) * rp)
        y = (sf_ref[rs, :].astype(F32) + sb_ref[rs, :].astype(F32)
             + sx_ref[rs, :].astype(F32) * dx_ref[...])
        y = y * _silu(sz_ref[rs, :].astype(F32))
        parts = []
        for g in range(SSM_GROUPS):
            yg = y[:, g * SSM_GW:(g + 1) * SSM_GW]
            parts.append(yg * lax.rsqrt(jnp.mean(yg * yg, axis=-1, keepdims=True) + EPS))
        ys_ref[rs, :] = (jnp.concatenate(parts, axis=1) * snw_ref[...]).astype(BF16)
        o = gf_ref[rs, :].astype(F32) + gb_ref[rs, :].astype(F32)
        parts = []
        for h in range(GDN_HEADS):
            oh = o[:, h * 128:(h + 1) * 128]
            parts.append(oh * lax.rsqrt(jnp.mean(oh * oh, axis=-1, keepdims=True) + EPS))
        yg_ref[rs, :] = (jnp.concatenate(parts, axis=1) * gnw_ref[...]
                         * _silu(gg_ref[rs, :].astype(F32))).astype(BF16)
    is_ctx = pl.program_id(0) < nctx_blk
    yh = jnp.where(is_ctx, yhc_ref[...], yhl_ref[...])
    m = (_sigmoid(g0_ref[...].astype(F32)) * _dot(yh, w0_ref[...])
         + _sigmoid(g1_ref[...].astype(F32)) * _dot(ys_ref[...], w1_ref[...])
         + _sigmoid(g2_ref[...].astype(F32)) * _dot(yg_ref[...], w2_ref[...]))
    x = jnp.where(is_ctx, xc_ref[...], xl_ref[...])
    o_ref[...] = x + mod_ref[0, 2:3, :] * _dot(m.astype(BF16), wo_ref[...])


def merge(rw, l, yh, y_f, y_b, dx, ssm_nw, o_f, o_b, gdn_nw, p, w0, w1, w2, wo, xs, mod, skip_ctx=False):
    R = rw.R
    D = xs[0].shape[1]
    tm = min(rw.tm, 512)
    mi = rw.mod_index(tm)
    skip = rw.NC // tm if skip_ctx else 0
    yspec = pl.BlockSpec((tm, 512), lambda i: (i + skip, 0))
    pspec = lambda col: pl.BlockSpec((tm, 512), lambda i: (i + skip, col // 512))
    vec = pl.BlockSpec((1, 512), lambda i: (0, 0))
    gspec = lambda k: pl.BlockSpec((tm, D), lambda i: (i + skip, C_GATE // D + k))
    wspec = pl.BlockSpec((None, 512, D), lambda i: (l, 0, 0))
    return pl.pallas_call(
        functools.partial(_merge_kernel, nctx_blk=rw.NC // tm - skip),
        grid=(R // tm - skip,),
        in_specs=_stream_specs(rw, tm, yh, 1, skip) + [
                  yspec, yspec, pspec(C_XBC), pspec(C_Z), vec, vec,
                  yspec, yspec, pspec(C_GG), vec,
                  gspec(0), gspec(1), gspec(2), wspec, wspec, wspec,
                  pl.BlockSpec((None, D, D), lambda i: (l, 0, 0))]
                 + _stream_specs(rw, tm, xs, 1, skip)
                 + [pl.BlockSpec((None, 1, 8, D), lambda i: (l, mi(i + skip), 0, 0))],
        out_specs=pl.BlockSpec((tm, D), lambda i: (i + skip, 0)),
        out_shape=jax.ShapeDtypeStruct((R, D), F32),
        scratch_shapes=[pltpu.VMEM((tm, 512), BF16), pltpu.VMEM((tm, 512), BF16)],
        compiler_params=_cp("arbitrary"),
        name="merge",
    )(yh[0], yh[1], y_f, y_b, p, p, dx, ssm_nw, o_f, o_b, p, gdn_nw, p, p, p, w0, w1, w2, wo, xs[0], xs[1], mod)


def _swiglu_up_kernel(x_ref, nw_ref, mod_ref, wg_ref, wu_ref, o_ref, h_ref, g0_ref, g1_ref, u0_ref, u1_ref):
    @pl.when(pl.program_id(1) == 0)
    def _():
        h = _norm_mod(x_ref[...], nw_ref[...], mod_ref[0, 4:5, :], mod_ref[0, 3:4, :])
        h_ref[...] = h.astype(BF16)

    T, tn = o_ref.shape
    rows = g0_ref.shape[0]
    gs, us = (g0_ref, g1_ref), (u0_ref, u1_ref)

    def project(r):
        hh = h_ref[r * rows:(r + 1) * rows, :]
        gs[r % 2][...] = _dot(hh, wg_ref[...])
        us[r % 2][...] = _dot(hh, wu_ref[...])

    def finish(r):
        for q in range(rows // 64):
            for c in range(tn // 128):
                ps = (slice(q * 64, (q + 1) * 64), slice(c * 128, (c + 1) * 128))
                y = _silu(gs[r % 2][ps]) * us[r % 2][ps]
                o_ref[r * rows + q * 64:r * rows + (q + 1) * 64, ps[1]] = y.astype(o_ref.dtype)

    for r in range(T // rows):
        project(r)
        if r > 0:
            finish(r - 1)
    finish(T // rows - 1)


def swiglu_up(rw, l, x, nw, mod, wgu, skip_ctx=False):
    R, D = x.shape
    tm = min(rw.tm, 512)
    tn = D_FF
    nj = D_FF // tn
    mi = rw.mod_index(tm)
    skip = rw.NC // tm if skip_ctx else 0
    return pl.pallas_call(
        _swiglu_up_kernel,
        grid=(R // tm - skip, nj),
        in_specs=[
            pl.BlockSpec((tm, D), lambda i, j: (i + skip, 0)),
            pl.BlockSpec((None, 1, D), lambda i, j: (l, 0, 0)),
            pl.BlockSpec((None, 1, 8, D), lambda i, j: (l, mi(i + skip), 0, 0)),
            pl.BlockSpec((None, D, tn), lambda i, j: (l, 0, j)),
            pl.BlockSpec((None, D, tn), lambda i, j: (l, 0, nj + j)),
        ],
        out_specs=pl.BlockSpec((tm, tn), lambda i, j: (i + skip, j)),
        out_shape=jax.ShapeDtypeStruct((R, D_FF), BF16),
        scratch_shapes=[pltpu.VMEM((tm, D), BF16)] + [pltpu.VMEM((min(256, tm), tn), F32)] * 4,
        compiler_params=_cp("arbitrary", "arbitrary"),
        name="swiglu_up",
    )(x, nw, mod, wgu, wgu)


def _swiglu_down_kernel(a_ref, w_ref, x_ref, mod_ref, fw_ref, o_ref, *, final):
    y = x_ref[...] + mod_ref[0, 5:6, :] * _dot(a_ref[...], w_ref[...])
    if final:
        y = y * lax.rsqrt(jnp.mean(y * y, axis=-1, keepdims=True) + EPS) * fw_ref[...]
    o_ref[...] = y


def swiglu_down(rw, l, a, w, x, mod, final_w, final=False):
    R, D = x.shape
    tm = min(rw.tm, 512)
    mi = rw.mod_index(tm)
    skip = rw.NC // tm if final else 0
    return pl.pallas_call(
        functools.partial(_swiglu_down_kernel, final=final),
        grid=(R // tm - skip,),
        in_specs=[
            pl.BlockSpec((tm, D_FF), lambda i: (i + skip, 0)),
            pl.BlockSpec((None, D_FF, D), lambda i: (l, 0, 0)),
            pl.BlockSpec((tm, D), lambda i: (i + skip, 0)),
            pl.BlockSpec((None, 1, 8, D), lambda i: (l, mi(i + skip), 0, 0)),
            pl.BlockSpec((1, D), lambda i: (0, 0)),
        ],
        out_specs=pl.BlockSpec((tm, D), lambda i: (i, 0)),
        out_shape=jax.ShapeDtypeStruct((R - skip * tm, D), F32),
        compiler_params=_cp("arbitrary"),
        name="swiglu_down",
    )(a, w, x, mod, final_w.reshape(1, D))


def _regroup_w_in(w_in):
    o_dt = 3072
    o_gdn = 3088
    o_a = o_gdn + 2048
    o_b = o_a + 8
    o_gate = o_gdn + 2064
    wt = jnp.swapaxes(w_in, 1, 2).astype(BF16)
    pieces = [
        wt[:, 0:3072],
        wt[:, o_gdn:o_gdn + 2048],
        wt[:, o_gate:o_gate + 3072],
        wt[:, o_dt:o_dt + 16],
        wt[:, o_a:o_a + 4], wt[:, o_b:o_b + 4],
        wt[:, o_a + 4:o_a + 8], wt[:, o_b + 4:o_b + 8],
        jnp.zeros((wt.shape[0], N_IN_PAD - C_SM - 32, wt.shape[2]), wt.dtype),
    ]
    return jnp.swapaxes(jnp.concatenate(pieces, axis=1), 1, 2)


def kernel(x, c, ctx, c_ctx, w_ada, b_ada, norm1_w, norm2_w, w_in, hy_conv_w, hy_conv_b, hy_w1, hy_b1, hy_w2, hy_b2, hy_w3, hy_freq, hy_bias, ssm_conv_w, ssm_conv_b, ssm_dt_bias, ssm_A_log, ssm_D, ssm_norm_w, gdn_conv_w, gdn_dt_bias, gdn_A_log, gdn_norm_w, w_hy_out, w_ssm_out, w_gdn_out, w_out, w_gate_up, w_down, final_norm_w):
    B, Ll, D = x.shape
    Lc = ctx.shape[1]
    depth = w_ada.shape[0]
    assert Lc == CONV_ROWS and D == D_MODEL and B <= 15
    rw = Rows(B, Lc, Ll)
    R, NC = rw.R, rw.NC

    xs = (ctx.reshape(B * Lc, D), x.reshape(B * Ll, D))

    svec = jnp.concatenate([c_ctx[None, :], c, jnp.zeros((15 - B, D), F32)], axis=0)
    mod = ada_modulation(svec, w_ada, b_ada)
    mod = jnp.pad(mod.reshape(depth, 16, 6, D), ((0, 0), (0, 0), (0, 2), (0, 0)))

    w_in_r = _regroup_w_in(w_in)
    par = _in_proj_params(hy_conv_w, hy_conv_b, ssm_conv_w, ssm_conv_b, gdn_conv_w, ssm_dt_bias, gdn_dt_bias,
                          gdn_A_log)
    norm1 = norm1_w.reshape(depth, 1, D)
    norm2 = norm2_w.reshape(depth, 1, D)
    w_hy_o, w_ssm_o, w_gdn_o, w_o = (w.astype(BF16) for w in (w_hy_out, w_ssm_out, w_gdn_out, w_out))
    w_gu, w_dn = w_gate_up.astype(BF16), w_down.astype(BF16)
    dft_l = dft_tables_split(Ll)
    dft_c = dft_tables_split(Lc)
    feat_l, feat_c = hy_features(Ll), hy_features(Lc)

    for l in range(depth):
        p, sm = in_proj(rw, xs, l, norm1, mod, w_in_r, par)

        sm32_t = sm[:, :32].T
        dt_t = sm32_t[:16].reshape(2, 8, R)
        g_t = sm32_t[16:32].reshape(16, R // GDN_CHUNK, GDN_CHUNK).transpose(1, 0, 2)

        alx = jnp.repeat(ssm_A_log[l], SSM_HEAD_DIM, axis=-1).reshape(2, 1, 512)
        alc = ssm_A_log[l].reshape(2, 8, 1)
        y_f, y_b = ssd_scan(rw, p, sm, dt_t, alx, alc)
        dx = jnp.repeat(ssm_D[l], SSM_HEAD_DIM).reshape(1, 512)

        o_f, o_b = gdn_scan(rw, *gdn_prep(p, sm, g_t))

        last = l == depth - 1
        hyu = p
        parts = []
        for (Bn, L, blk0, (fwd, inv), feat) in ((B, Lc, 0, dft_c, feat_c), (B, Ll, NC // Ll, dft_l, feat_l)):
            if NC % L:
                raise ValueError("latent length must divide the context row count")
            if last and blk0 == 0:
                parts.append(None)
                continue
            filt = hy_filter(feat, hy_w1[l], hy_b1[l], hy_w2[l], hy_b2[l], hy_w3[l], hy_freq[l])
            kspec = filter_spectrum(fwd, *filt)
            z1 = long_conv(Bn, L, hyu, blk0, 0, hyu, blk0, 1, hy_bias[l, 0], fwd, inv, kspec, 0, F32)
            yy = long_conv(Bn, L, z1, 0, 0, hyu, blk0, 2, hy_bias[l, 1], fwd, inv, kspec, 1, BF16)
            parts.append(yy)
        y_hy = (parts[1] if last else parts[0], parts[1])

        xa = merge(rw, l, y_hy, y_f, y_b, dx, ssm_norm_w[l].reshape(1, 512),
                   o_f, o_b, jnp.tile(gdn_norm_w[l], GDN_HEADS).reshape(1, 512), p,
                   w_hy_o, w_ssm_o, w_gdn_o, w_o, xs, mod, skip_ctx=last)
        act = swiglu_up(rw, l, xa, norm2, mod, w_gu, skip_ctx=last)
        xa = swiglu_down(rw, l, act, w_dn, xa, mod, final_norm_w, final=last)
        xs = (xa, xa)

    return xa.reshape(B, Ll, D)
```

```python
import functools
import math

import jax
import jax.numpy as jnp
import numpy as np
from jax import lax
from jax.experimental import pallas as pl
from jax.experimental.pallas import tpu as pltpu

F32 = jnp.float32
BF16 = jnp.bfloat16
HI = lax.Precision.HIGHEST

EPS = 1e-6
D_MODEL = 1024
GRID_W = 64

HY_WIDTH = 512
HY_BANDS = 16
HY_EMB = 1 + 2 * HY_BANDS
HY_HIDDEN = 64
HY_SHORT_DECAY_PCT = 0.3
HY_LONG_DECAY_PCT = 1.5
HY_TARGET = 1e-2

SSM_HEADS = 8
SSM_HEAD_DIM = 64
SSM_WIDTH = 512
SSM_GROUPS = 2
SSM_HPG = 4
SSM_STATE = 128
SSM_CHUNK = 128
SSM_GW = SSM_HPG * SSM_HEAD_DIM

GDN_HEADS = 4
GDN_DK = 128
GDN_DV = 128
GDN_CHUNK = 64

D_FF = 2816

C_HY = 0
C_Z = 1536
C_XBC = 2048
C_QKV = 3072
C_GG = 4608
C_GATE = 5120
C_SM = 8192

CONV_ROWS = 256
FREQ_BLK = 256

VMEM_LIMIT = 56 * 1024 * 1024


def _cp(*sem):
    return pltpu.CompilerParams(dimension_semantics=sem, vmem_limit_bytes=VMEM_LIMIT)


def _sigmoid(x):
    return 1.0 / (1.0 + jnp.exp(-x))


def _silu(x):
    return x * _sigmoid(x)


def _softplus(x):
    return jnp.maximum(x, 0.0) + jnp.log1p(jnp.exp(-jnp.abs(x)))


def _dot(a, b, precision=None):
    return jnp.dot(a, b, precision=precision, preferred_element_type=F32)


def _dot_nt(a, b):
    return lax.dot_general(a, b, (((1,), (1,)), ((), ())), preferred_element_type=F32)


def _dot_tn(a, b):
    return lax.dot_general(a, b, (((0,), (0,)), ((), ())), preferred_element_type=F32)


def _ada_kernel(s_ref, w_ref, b_ref, o_ref):
    s = _silu(s_ref[...])
    o_ref[0] = _dot(s, w_ref[0], HI) + b_ref[0]


def ada_modulation(svec, w_ada, b_ada):
    depth = w_ada.shape[0]
    D = D_MODEL
    return pl.pallas_call(
        _ada_kernel,
        grid=(depth, 6),
        in_specs=[
            pl.BlockSpec((16, D), lambda l, j: (0, 0)),
            pl.BlockSpec((1, D, D), lambda l, j: (l, 0, j)),
            pl.BlockSpec((1, 1, D), lambda l, j: (l, 0, j)),
        ],
        out_specs=pl.BlockSpec((1, 16, D), lambda l, j: (l, 0, j)),
        out_shape=jax.ShapeDtypeStruct((depth, 16, 6 * D), F32),
        compiler_params=_cp("arbitrary", "arbitrary"),
        name="ada",
    )(svec, w_ada, b_ada.reshape(depth, 1, 6 * D))


def _norm_mod(x, nw, scale, shift):
    ms = jnp.mean(x * x, axis=-1, keepdims=True)
    return (x * lax.rsqrt(ms + EPS) * nw) * (1.0 + scale) + shift


IN_FLIGHT = 3
N_IN_PAD = C_SM + 128
IN_TN = N_IN_PAD // 5
MODE_RAW, MODE_CONV, MODE_CONV_SILU, MODE_CONV_SILU_L2, MODE_SMALL = range(5)


def _tile_mode(tile):
    col = tile * 128
    if col < C_Z:
        return MODE_CONV
    if col < C_XBC:
        return MODE_RAW
    if col < C_QKV:
        return MODE_CONV_SILU
    if col < C_QKV + 1024:
        return MODE_CONV_SILU_L2
    if col < C_GG:
        return MODE_CONV_SILU
    if col < C_SM:
        return MODE_RAW
    return MODE_SMALL
PAR_W0, PAR_W1, PAR_W2, PAR_BIAS, PAR_L2SCALE, PAR_SBIAS, PAR_SALOG, PAR_SKIND = range(8)


def _in_proj_kernel(xc_ref, xl_ref, nw_ref, mod_ref, w_ref, par_ref, o_ref, sm_ref, h_ref, raw0_ref, raw1_ref, *,
                    nctx_blk):
    j = pl.program_id(1)
    nj = N_IN_PAD // IN_TN
    raws = (raw0_ref, raw1_ref)

    @pl.when((j == 0) & (pl.program_id(0) < nctx_blk))
    def _():
        h = _norm_mod(xc_ref[...], nw_ref[...], mod_ref[0, 1:2, :], mod_ref[0, 0:1, :])
        h_ref[...] = h.astype(BF16)

    @pl.when((j == 0) & (pl.program_id(0) >= nctx_blk))
    def _():
        h = _norm_mod(xl_ref[...], nw_ref[...], mod_ref[0, 1:2, :], mod_ref[0, 0:1, :])
        h_ref[...] = h.astype(BF16)

    T = h_ref.shape[0]
    G = GRID_W
    per_ctx = CONV_ROWS // G
    is_latent = pl.program_id(0) >= nctx_blk
    sub = lax.broadcasted_iota(jnp.int32, (8, 128), 0)

    def raw_piece(src, g, c):
        return src[g * G:(g + 1) * G, c * 128:(c + 1) * 128]

    retired = []

    def retire(y):
        bits = pltpu.bitcast(y[0:8], jnp.int32)
        zero = lax.shift_right_logical(lax.shift_right_logical(bits, 16), 16)
        retired.append(jnp.tile(zero.astype(F32), (G // 8, 1)))

    def conv(src, g, c):
        cs = slice(c * 128, (c + 1) * 128)
        x = raw_piece(src, g, c)
        if len(retired) >= IN_FLIGHT:
            x = x + retired[-IN_FLIGHT]
        zero = jnp.zeros((1, 128), F32)
        before = zero if g % per_ctx == 0 else jnp.where(is_latent, 0.0, src[g * G - 1:g * G, cs])
        after = zero if g % per_ctx == per_ctx - 1 else jnp.where(is_latent, 0.0, src[(g + 1) * G:(g + 1) * G + 1, cs])
        rp = pltpu.roll(x, 1, 0)
        rn = pltpu.roll(x, G - 1, 0)
        prev = jnp.concatenate([jnp.where(sub == 0, before, rp[0:8]), rp[8:]], axis=0)
        nxt = jnp.concatenate([rn[:G - 8], jnp.where(sub == 7, after, rn[G - 8:])], axis=0)
        return (prev * par_ref[PAR_W0:PAR_W0 + 1, cs] + x * par_ref[PAR_W1:PAR_W1 + 1, cs]
                + nxt * par_ref[PAR_W2:PAR_W2 + 1, cs] + par_ref[PAR_BIAS:PAR_BIAS + 1, cs])

    def conv_silu(src, g, c):
        return _silu(conv(src, g, c))

    def conv_silu_l2(src, g, c):
        y = _silu(conv(src, g, c))
        y = y * lax.rsqrt(jnp.sum(y * y, axis=-1, keepdims=True) + EPS)
        return y * par_ref[PAR_L2SCALE:PAR_L2SCALE + 1, c * 128:(c + 1) * 128]

    def small(src, g, c):
        cs = slice(c * 128, (c + 1) * 128)
        acc = raw_piece(src, g, c)
        kind = par_ref[PAR_SKIND:PAR_SKIND + 1, cs]
        sp = _softplus(acc + par_ref[PAR_SBIAS:PAR_SBIAS + 1, cs])
        dec = -jnp.exp(par_ref[PAR_SALOG:PAR_SALOG + 1, cs]) * sp
        return jnp.where(kind == 0.0, sp, jnp.where(kind == 1.0, dec, jnp.where(kind == 2.0, _sigmoid(acc), 0.0)))

    rows_mm = 256
    tiles = IN_TN // 128
    piece_fn = {MODE_RAW: raw_piece, MODE_CONV: conv, MODE_CONV_SILU: conv_silu,
                MODE_CONV_SILU_L2: conv_silu_l2, MODE_SMALL: small}

    heavy_modes = (MODE_CONV_SILU, MODE_CONV_SILU_L2)
    col_slices = [slice(c0, min(c0 + 256, IN_TN)) for c0 in range(0, IN_TN, 256)]

    def project(dst, r, cs=slice(None)):
        rs = slice(r * rows_mm, (r + 1) * rows_mm)
        dst[rs, cs] = _dot(h_ref[rs, :], w_ref[:, cs])

    def finish(src, blk, g, c):
        mode = _tile_mode(blk * tiles + c)
        y = piece_fn[mode](src, g, c)
        if mode in heavy_modes:
            retire(y)
        if mode == MODE_SMALL:
            sm_ref[g * G:(g + 1) * G, :] = y
            y = jnp.zeros_like(y)
        o_ref[g * G:(g + 1) * G, c * 128:(c + 1) * 128] = y.astype(o_ref.dtype)

    for step in range(nj + 1):
        @pl.when(j == step)
        def _(step=step):
            blk = step - 1
            src, dst = raws[blk % 2], raws[step % 2]
            retired.clear()
            light = blk >= 0 and not any(_tile_mode(blk * tiles + c) in heavy_modes for c in range(tiles))
            for r in range(T // rows_mm):
                pieces = [] if blk < 0 else [(g, c) for g in range(r * rows_mm // G, (r + 1) * rows_mm // G)
                                             for c in range(tiles)]
                if step == nj:
                    slabs = []
                elif light:
                    slabs = col_slices
                else:
                    slabs = [slice(None)]
                per = -(-len(pieces) // max(len(slabs), 1))
                for n in range(max(len(slabs), 1)):
                    if n < len(slabs):
                        project(dst, r, slabs[n])
                    for g, c in pieces[n * per:(n + 1) * per]:
                        finish(src, blk, g, c)


class Rows:
    def __init__(self, B, Lc, Ll):
        self.B, self.Lc, self.Ll = B, Lc, Ll
        self.NC = B * Lc
        self.R = B * Lc + B * Ll
        assert self.NC % Ll == 0 or Ll % self.NC == 0
        tm = 1024
        while self.NC % tm or Ll % tm:
            tm //= 2
        self.tm = tm

    def mod_index(self, tm):
        nctx = self.NC // tm
        per = self.Ll // tm
        return lambda i: jnp.where(i < nctx, 0, 1 + (i - nctx) // per)


def _stream_specs(rw, tm, xs, ngrid, skip=0):
    xc, xl = xs
    nctx = rw.NC // tm
    off = nctx if xl.shape[0] == rw.R else 0
    D = xc.shape[1]
    if ngrid == 1:
        return [pl.BlockSpec((tm, D), lambda i: (jnp.minimum(i + skip, nctx - 1), 0)),
                pl.BlockSpec((tm, D), lambda i: (jnp.maximum(i + skip - nctx, 0) + off, 0))]
    return [pl.BlockSpec((tm, D), lambda i, j: (jnp.minimum(i + skip, nctx - 1), 0)),
            pl.BlockSpec((tm, D), lambda i, j: (jnp.maximum(i + skip - nctx, 0) + off, 0))]


def in_proj(rw, xs, l, nw, mod, w, par):
    R = rw.R
    D = xs[0].shape[1]
    N = w.shape[2]
    tm, tn = rw.tm, IN_TN
    nj = N // tn
    assert N == N_IN_PAD
    mi = rw.mod_index(tm)
    done = lambda j: jnp.maximum(j - 1, 0)
    return pl.pallas_call(
        functools.partial(_in_proj_kernel, nctx_blk=rw.NC // tm),
        grid=(R // tm, nj + 1),
        in_specs=_stream_specs(rw, tm, xs, 2) + [
            pl.BlockSpec((None, 1, D), lambda i, j: (l, 0, 0)),
            pl.BlockSpec((None, 1, 8, D), lambda i, j: (l, mi(i), 0, 0)),
            pl.BlockSpec((None, D, tn), lambda i, j: (l, 0, jnp.minimum(j, nj - 1))),
            pl.BlockSpec((None, 8, tn), lambda i, j: (l, 0, done(j))),
        ],
        out_specs=[pl.BlockSpec((tm, tn), lambda i, j: (i, done(j))),
                   pl.BlockSpec((tm, 128), lambda i, j: (i, 0))],
        out_shape=[jax.ShapeDtypeStruct((R, N), BF16), jax.ShapeDtypeStruct((R, 128), F32)],
        scratch_shapes=[pltpu.VMEM((tm, D), BF16), pltpu.VMEM((tm, tn), F32), pltpu.VMEM((tm, tn), F32)],
        compiler_params=_cp("arbitrary", "arbitrary"),
        name="in_proj",
    )(xs[0], xs[1], nw, mod, w, par)


def _in_proj_params(hy_conv_w, hy_conv_b, ssm_conv_w, ssm_conv_b, gdn_conv_w, ssm_dt_bias, gdn_dt_bias, gdn_A_log):
    depth = hy_conv_w.shape[0]

    def row(pieces):
        out, pos = [], 0
        for off, a in pieces:
            out += [jnp.zeros((depth, off - pos), F32), a.astype(F32)]
            pos = off + a.shape[1]
        return jnp.concatenate(out + [jnp.zeros((depth, N_IN_PAD - pos), F32)], axis=1)
    z4 = jnp.zeros((depth, 4), F32)
    conv = [row([(C_HY, hy_conv_w[:, t]), (C_XBC, ssm_conv_w[:, t]), (C_QKV, gdn_conv_w[:, t])]) for t in range(3)]
    bias = row([(C_HY, hy_conv_b), (C_XBC, ssm_conv_b)])
    l2s = row([(C_QKV, jnp.full((depth, 512), GDN_DK ** -0.5, F32)), (C_QKV + 512, jnp.ones((depth, 512), F32))])
    sbias = row([(C_SM, jnp.concatenate([ssm_dt_bias.reshape(depth, 16), gdn_dt_bias[:, 0], z4,
                                         gdn_dt_bias[:, 1], z4], axis=1))])
    salog = row([(C_SM + 16, jnp.concatenate([gdn_A_log[:, 0], z4, gdn_A_log[:, 1], z4], axis=1))])
    kind = np.full((depth, N_IN_PAD), 3.0, np.float32)
    kind[:, C_SM:C_SM + 16] = 0.0
    kind[:, C_SM + 16:C_SM + 20] = 1.0
    kind[:, C_SM + 24:C_SM + 28] = 1.0
    kind[:, C_SM + 20:C_SM + 24] = 2.0
    kind[:, C_SM + 28:C_SM + 32] = 2.0
    return jnp.stack(conv + [bias, l2s, sbias, salog, jnp.asarray(kind)], axis=1)


def _hy_filter_kernel(z_ref, w1_ref, b1_ref, w2_ref, b2_ref, w3_ref, f0_ref, f1_ref, win_ref, oe_ref, oo_ref,
                      h_ref, split_ref):
    @pl.when(pl.program_id(1) == 0)
    def _():
        h1 = jnp.sin(f0_ref[...] * (_dot(z_ref[...], w1_ref[...], HI) + b1_ref[...]))
        h_ref[...] = jnp.sin(f1_ref[...] * (_dot(h1, w2_ref[...], HI) + b2_ref[...]))

    a1 = h_ref[...].astype(BF16)
    a2 = (h_ref[...] - a1.astype(F32)).astype(BF16)
    b1 = w3_ref[...].astype(BF16)
    b2 = (w3_ref[...] - b1.astype(F32)).astype(BF16)
    h = (_dot(a1, b1) + _dot(a1, b2) + _dot(a2, b1)) * win_ref[...]
    tl = h.shape[0]
    row = lax.broadcasted_iota(jnp.int32, (tl, 1), 0) + pl.program_id(0) * tl
    drop = (row == 0) & (pl.program_id(1) % 2 == 1)
    h = jnp.where(drop, 0.0, h)
    for c in range(h.shape[1] // 128):
        cs = slice(c * 128, (c + 1) * 128)
        s_c = split_ref.at[c]
        s_c[...] = h[:, cs]
        oe_ref[:, cs] = s_c[pl.ds(0, tl // 2, stride=2), :].astype(oe_ref.dtype)
        oo_ref[:, cs] = s_c[pl.ds(1, tl // 2, stride=2), :].astype(oo_ref.dtype)


def hy_features(L):
    t = jnp.linspace(0.0, 1.0, L, dtype=F32)[:, None]
    w = 2.0 * math.pi * jnp.arange(L, dtype=F32)[:, None] / L
    f = jnp.linspace(1e-4, HY_BANDS - 1, HY_BANDS, dtype=F32)[None, :]
    z = jnp.concatenate([t, jnp.cos(f * w), -jnp.sin(f * w)], axis=-1)
    z = jnp.pad(z, ((0, 0), (0, 128 - HY_EMB)))
    min_decay = math.log(HY_TARGET) / HY_LONG_DECAY_PCT
    max_decay = math.log(HY_TARGET) / HY_SHORT_DECAY_PCT
    deltas = jnp.linspace(min_decay, max_decay, HY_WIDTH, dtype=F32)
    window = jnp.exp(-t * jnp.abs(deltas))
    return z, window


def hy_filter(feat, w1, b1, w2, b2, w3, freq):
    z, window = feat
    L = z.shape[0]
    H = HY_HIDDEN
    w1p = jnp.pad(w1, ((0, 128 - HY_EMB), (0, 128 - H)))
    w2p = jnp.pad(w2, ((0, 128 - H), (0, 128 - H)))
    w3p = jnp.pad(w3, ((0, 128 - H), (0, 0)))
    pad1 = lambda v: jnp.pad(v, (0, 128 - H)).reshape(1, 128)
    tl = 256
    full = lambda shape: pl.BlockSpec(shape, lambda i, j: (0, 0))
    return pl.pallas_call(
        _hy_filter_kernel,
        grid=(L // tl, 4),
        in_specs=[
            pl.BlockSpec((tl, 128), lambda i, j: (i, 0)),
            full((128, 128)), full((1, 128)), full((128, 128)), full((1, 128)),
            pl.BlockSpec((128, HY_WIDTH), lambda i, j: (0, j)),
            full((1, 128)), full((1, 128)),
            pl.BlockSpec((tl, HY_WIDTH), lambda i, j: (i, 0)),
        ],
        out_specs=[pl.BlockSpec((tl // 2, HY_WIDTH), lambda i, j: (i, j))] * 2,
        out_shape=[jax.ShapeDtypeStruct((L // 2, 4 * HY_WIDTH), BF16)] * 2,
        scratch_shapes=[pltpu.VMEM((tl, 128), F32), pltpu.VMEM((HY_WIDTH // 128, tl, 128), F32)],
        compiler_params=_cp("arbitrary", "arbitrary"),
        name="hy_filter",
    )(z, w1p, pad1(b1), w2p, pad1(b2), w3p, pad1(freq[0]), pad1(freq[1]), window)


def dft_tables_split(L):
    N = 2 * L
    H = L // 2
    q = np.arange(H, dtype=np.int64)[:, None]
    m = np.arange(H, dtype=np.int64)[None, :]
    ang_e = ((q * 2 * m) % N).astype(np.float64) * (2.0 * math.pi / N)
    ang_o = ((q * (2 * m + 1)) % N).astype(np.float64) * (2.0 * math.pi / N)
    alt = (1 - 2 * (m % 2)).astype(np.float64)
    ce, co = np.cos(ang_e), np.cos(ang_o)
    se = np.where(q == 0, alt, -np.sin(ang_e))
    so = np.where(q == 0, -alt, -np.sin(ang_o))
    w = np.where(q == 0, 1.0, 2.0) / N
    ise = np.where(q == 0, 2.0 / N * alt, -np.sin(ang_e) * w)
    iso = np.where(q == 0, -2.0 / N * alt, -np.sin(ang_o) * w)
    fwd = np.stack([ce, co, se, so])
    inv = np.stack([(ce * w).T, ise.T, (co * w).T, iso.T])
    return jnp.asarray(fwd, dtype=BF16), jnp.asarray(inv, dtype=BF16)


def _filter_spectrum_kernel(fwd_ref, he_ref, ho_ref, rlo_ref, rhi_ref, ilo_ref, ihi_ref):
    he, ho = he_ref[...], ho_ref[...]
    ae, ao = _dot(fwd_ref[0], he), _dot(fwd_ref[1], ho)
    be, bo = _dot(fwd_ref[2], he), _dot(fwd_ref[3], ho)
    rlo_ref[...] = ae + ao
    rhi_ref[...] = ae - ao
    first = (lax.broadcasted_iota(jnp.int32, (fwd_ref.shape[1], 1), 0) == 0) & (pl.program_id(0) == 0)
    ilo_ref[...] = jnp.where(first, be, be + bo)
    ihi_ref[...] = jnp.where(first, bo, bo - be)


def filter_spectrum(fwd, taps_even, taps_odd):
    H, N = taps_even.shape
    C = HY_WIDTH
    FB = min(FREQ_BLK, H)
    tap = pl.BlockSpec((H, C), lambda f, j: (0, j))
    out = pl.BlockSpec((FB, C), lambda f, j: (f, j))
    return pl.pallas_call(
        _filter_spectrum_kernel,
        grid=(H // FB, N // C),
        in_specs=[pl.BlockSpec((4, FB, H), lambda f, j: (0, f, 0)), tap, tap],
        out_specs=[out] * 4,
        out_shape=[jax.ShapeDtypeStruct((H, N), F32)] * 4,
        compiler_params=_cp("arbitrary", "arbitrary"),
        name="filter_spectrum",
    )(fwd, taps_even, taps_odd)


def _long_conv_kernel(u_ref, g_ref, bias_ref, fwd_ref, inv_ref, ar0_ref, ar1_ref, ar0h_ref, ar1h_ref,
                      ai0_ref, ai1_ref, ai0h_ref, ai1h_ref, o_ref, ue_ref, uo_ref, acce_ref, acco_ref, y_ref):
    f = pl.program_id(1)
    half = ue_ref.shape[0]

    lane_tiles = [slice(c * 128, (c + 1) * 128) for c in range(y_ref.shape[0])]

    @pl.when(f == 0)
    def _():
        for c, cs in enumerate(lane_tiles):
            y_c = y_ref.at[c]
            y_c[...] = u_ref[:, cs].astype(F32)
            ue_ref[:, cs] = y_c[pl.ds(0, half, stride=2), :].astype(BF16)
            uo_ref[:, cs] = y_c[pl.ds(1, half, stride=2), :].astype(BF16)
        acce_ref[...] = jnp.zeros_like(acce_ref)
        acco_ref[...] = jnp.zeros_like(acco_ref)

    ue, uo = ue_ref[...], uo_ref[...]
    ae, ao = _dot(fwd_ref[0], ue), _dot(fwd_ref[1], uo)
    be, bo = _dot(fwd_ref[2], ue), _dot(fwd_ref[3], uo)
    ur, ur2 = ae + ao, ae - ao
    ui, ui2 = be + bo, bo - be
    first = (lax.broadcasted_iota(jnp.int32, (fwd_ref.shape[1], 1), 0) == 0) & (f == 0)
    kr, kr2 = ar0_ref[...] + ar1_ref[...], ar0h_ref[...] + ar1h_ref[...]
    ki = jnp.where(first, ai0_ref[...] + ai1_ref[...], ai0_ref[...] - ai1_ref[...])
    ki2 = ai0h_ref[...] - ai1h_ref[...]
    pr, pi = ur * kr - ui * ki, ur * ki + ui * kr
    pr2, pi2 = ur2 * kr2 - ui2 * ki2, ur2 * ki2 + ui2 * kr2
    dc, ny = ur * kr, ur2 * kr2
    gr = jnp.where(first, dc + ny, pr + pr2)
    gi = jnp.where(first, be * ki - bo * ki2, pi - pi2)
    hr = jnp.where(first, dc - ny, pr - pr2)
    hi = jnp.where(first, be * ki2 + bo * ki, pi + pi2)
    acce_ref[...] += _dot(inv_ref[0], gr.astype(BF16)) + _dot(inv_ref[1], gi.astype(BF16))
    acco_ref[...] += _dot(inv_ref[2], hr.astype(BF16)) + _dot(inv_ref[3], hi.astype(BF16))

    @pl.when(f == pl.num_programs(1) - 1)
    def _():
        for c, cs in enumerate(lane_tiles):
            y_c = y_ref.at[c]
            y_c[pl.ds(0, half, stride=2), :] = acce_ref[:, cs]
            y_c[pl.ds(1, half, stride=2), :] = acco_ref[:, cs]
            u = u_ref[:, cs].astype(F32)
            o_ref[:, cs] = (g_ref[:, cs].astype(F32) * (y_c[...] + u * bias_ref[:, cs])).astype(o_ref.dtype)


def long_conv(B, L, u, u_rb0, u_cb, gate, g_rb0, gate_cb, bias, fwd, inv, kspec, order, out_dtype):
    C = HY_WIDTH
    H = L // 2
    FB = min(FREQ_BLK, H)
    nfb = H // FB
    kblk = lambda part, d: pl.BlockSpec((FB, C), lambda b, f: (f, 2 * order + d))
    kops = [kspec[part] for part in range(4) for _ in range(2)]
    return pl.pallas_call(
        _long_conv_kernel,
        grid=(B, nfb),
        in_specs=[
            pl.BlockSpec((L, C), lambda b, f: (u_rb0 + b, u_cb)),
            pl.BlockSpec((L, C), lambda b, f: (g_rb0 + b, gate_cb)),
            pl.BlockSpec((1, C), lambda b, f: (0, 0)),
            pl.BlockSpec((4, FB, H), lambda b, f: (0, f, 0)),
            pl.BlockSpec((4, H, FB), lambda b, f: (0, 0, f)),
            kblk(0, 0), kblk(0, 1), kblk(1, 0), kblk(1, 1), kblk(2, 0), kblk(2, 1), kblk(3, 0), kblk(3, 1),
        ],
        out_specs=pl.BlockSpec((L, C), lambda b, f: (b, 0)),
        out_shape=jax.ShapeDtypeStruct((B * L, C), out_dtype),
        scratch_shapes=[pltpu.VMEM((H, C), BF16), pltpu.VMEM((H, C), BF16),
                        pltpu.VMEM((H, C), F32), pltpu.VMEM((H, C), F32), pltpu.VMEM((C // 128, L, 128), F32)],
        compiler_params=_cp("arbitrary", "arbitrary"),
        name="long_conv",
    )(u, gate, bias.reshape(1, C), fwd, inv, *kops)


def _scan_blocks(rw, rows):
    nbc, nbl, base = rw.Lc // rows, rw.Ll // rows, rw.NC // rows

    def make(d):
        def f(b, s):
            jc = s if d == 0 else nbc - 1 - s
            jl = (s - nbc) if d == 0 else nbl - 1 - (s - nbc)
            return jnp.where(s < nbc, b * nbc + jc, base + b * nbl + jl)
        return f

    return [make(0), make(1)], nbc + nbl


def _expand_lanes(x, base, n, width):
    rows = x.shape[0]
    per = 128 // width
    lane = lax.broadcasted_iota(jnp.int32, (rows, 128), 1)
    tiles = []
    for t in range(n // per):
        c0 = base + t * per
        tile = jnp.broadcast_to(x[:, c0:c0 + 1], (rows, 128))
        for i in range(1, per):
            tile = jnp.where(lane >= i * width, jnp.broadcast_to(x[:, c0 + i:c0 + i + 1], (rows, 128)), tile)
        tiles.append(tile)
    return jnp.concatenate(tiles, axis=1)


def _ssd_kernel(xf, bf, cf, smf, dtf, xb, bb, cb_, smb, dtb, alx_ref, alc_ref, of_ref, ob_ref, h_ref):
    Q = SSM_CHUNK
    GW = SSM_GW

    @pl.when(pl.program_id(1) == 0)
    def _():
        h_ref[...] = jnp.zeros_like(h_ref)

    row = lax.broadcasted_iota(jnp.int32, (Q, Q), 0)
    col = lax.broadcasted_iota(jnp.int32, (Q, Q), 1)
    lane_head = lax.broadcasted_iota(jnp.int32, (Q, GW), 1) // SSM_HEAD_DIM
    dirs = ((xf, bf, cf, smf, dtf, of_ref), (xb, bb, cb_, smb, dtb, ob_ref))
    jobs = []
    for d in range(2):
        x_ref, b_ref, c_ref, sm_ref, dt_ref, o_ref = dirs[d]
        keep = (col <= row) if d == 0 else (col >= row)
        tri = keep.astype(BF16)
        tri_t = ((row <= col) if d == 0 else (row >= col)).astype(BF16)
        sm = sm_ref[...]
        a_x = -jnp.exp(alx_ref[d])
        dtx = _expand_lanes(sm, 8 * d, SSM_HEADS, SSM_HEAD_DIM)
        cumx = _expand_lanes(_dot_01_lhs(tri, sm), 8 * d, SSM_HEADS, SSM_HEAD_DIM) * a_x
        cumr = _dot_01_rhs(dt_ref[0], tri_t) * (-jnp.exp(alc_ref[d]))
        last = Q - 1 if d == 0 else 0
        totx = cumx[last:last + 1, :]
        xd = x_ref[...].astype(F32) * dtx
        xdw = xd * jnp.exp(totx - cumx)
        ecum = jnp.exp(cumx)
        for g in range(SSM_GROUPS):
            gs = slice(g * GW, (g + 1) * GW)
            jobs.append(dict(d=d, g=g, gs=gs, keep=keep, cumx=cumx, cumr=cumr, o_ref=o_ref,
                             bg=b_ref[:, g * SSM_STATE:(g + 1) * SSM_STATE].astype(BF16),
                             cg=c_ref[:, g * SSM_STATE:(g + 1) * SSM_STATE].astype(BF16),
                             xdg=xd[:, gs], xdw=xdw[:, gs].astype(BF16), ecum=ecum[:, gs],
                             etot=jnp.exp(totx[:, gs])))
    for j in jobs:
        j["cb"] = _dot_nt(j["cg"], j["bg"])
        j["h"] = h_ref[j["d"], j["g"]]
    for j in jobs:
        ms, xs = [], []
        for e4 in range(SSM_HPG):
            e = j["g"] * SSM_HPG + e4
            diff = j["cumx"][:, e * SSM_HEAD_DIM:e * SSM_HEAD_DIM + 1] - j["cumr"][e:e + 1, :]
            ms.append((j["cb"] * jnp.where(j["keep"], jnp.exp(diff), 0.0)).astype(BF16))
            xs.append(jnp.where(lane_head == e4, j["xdg"], 0.0).astype(BF16))
        yd = _dot(jnp.concatenate(ms, axis=1), jnp.concatenate(xs, axis=0))
        y_off = _dot(j["cg"], j["h"].astype(BF16)) * j["ecum"]
        j["o_ref"][:, j["gs"]] = (yd + y_off).astype(BF16)
    for j in jobs:
        h_ref[j["d"], j["g"]] = j["h"] * j["etot"] + _dot_tn(j["bg"], j["xdw"])


def ssd_scan(rw, p, sm, dtT, alx, alc):
    Q = SSM_CHUNK
    blks, nsteps = _scan_blocks(rw, Q)
    R = p.shape[0]
    in_specs = []
    for d in range(2):
        f = blks[d]
        in_specs += [
            pl.BlockSpec((Q, 512), lambda b, s, f=f: (f(b, s), C_XBC // 512)),
            pl.BlockSpec((Q, 256), lambda b, s, f=f: (f(b, s), C_XBC // 256 + 2)),
            pl.BlockSpec((Q, 256), lambda b, s, f=f: (f(b, s), C_XBC // 256 + 3)),
            pl.BlockSpec((Q, 128), lambda b, s, f=f: (f(b, s), 0)),
            pl.BlockSpec((1, 8, Q), lambda b, s, f=f, d=d: (d, 0, f(b, s))),
        ]
    in_specs += [pl.BlockSpec((2, 1, 512), lambda b, s: (0, 0, 0)), pl.BlockSpec((2, 8, 1), lambda b, s: (0, 0, 0))]
    ops = (p, p, p, sm, dtT)
    return pl.pallas_call(
        _ssd_kernel,
        grid=(rw.B, nsteps),
        in_specs=in_specs,
        out_specs=[pl.BlockSpec((Q, 512), lambda b, s, f=blks[d]: (f(b, s), 0)) for d in range(2)],
        out_shape=[jax.ShapeDtypeStruct((R, 512), BF16)] * 2,
        scratch_shapes=[pltpu.VMEM((2, SSM_GROUPS, SSM_STATE, SSM_GW), F32)],
        compiler_params=_cp("arbitrary", "arbitrary"),
        name="ssd_scan",
    )(*ops, *ops, alx, alc)


def _split3(x):
    x1 = x.astype(BF16)
    r = x - x1.astype(F32)
    x2 = r.astype(BF16)
    x3 = (r - x2.astype(F32)).astype(BF16)
    return x1, x2, x3


def _dot_01_lhs(m01, x):
    x1, x2, x3 = _split3(x)
    return _dot(m01, x1) + _dot(m01, x2) + _dot(m01, x3)


def _dot_01_rhs(x, m01):
    x1, x2, x3 = _split3(x)
    return _dot(x1, m01) + _dot(x2, m01) + _dot(x3, m01)


GDN_ROWS = 256


def _gdn_prep_kernel(q_ref, k_ref, v_ref, sm_ref, gT_ref, u_ref, w_ref, qg_ref, kd_ref, qk_ref, egl_ref):
    C = GDN_CHUNK
    row = lax.broadcasted_iota(jnp.int32, (C, C), 0)
    col = lax.broadcasted_iota(jnp.int32, (C, C), 1)
    lane2 = lax.broadcasted_iota(jnp.int32, (C, 2 * C), 1)
    row2 = lax.broadcasted_iota(jnp.int32, (C, 2 * C), 0)
    col2 = lane2 & (C - 1)
    left = lane2 < C
    lane_k = lax.broadcasted_iota(jnp.int32, (C, 2 * GDN_DK), 1) < GDN_DK
    jobs = []
    for d in range(2):
        keep = (col <= row) if d == 0 else (col >= row)
        tri = keep.astype(BF16)
        tri_t2 = ((row2 <= col2) if d == 0 else (row2 >= col2)).astype(BF16)
        last = C - 1 if d == 0 else 0
        for c in range(GDN_ROWS // C):
            rows = slice(c * C, (c + 1) * C)
            smc = sm_ref[rows, :]
            cums = _dot_01_lhs(tri, smc)
            cumr2 = _dot_01_rhs(gT_ref[c, 8 * d:8 * d + 8, :], tri_t2)
            tot = cums[last:last + 1, :]
            for h in range(GDN_HEADS):
                lg = 16 + 8 * d + h
                jobs.append(dict(d=d, c=c, h=h, rows=rows, hs=slice(h * 128, (h + 1) * 128), keep=keep,
                                 gc=cums[:, lg:lg + 1], beta=smc[:, lg + 4:lg + 5],
                                 gl=tot[:, lg:lg + 1], gr=cumr2[h:h + 1, :C], gr2=cumr2[h:h + 1, :]))

    def block_diag(x):
        return jnp.concatenate([jnp.where(left, x, 0.0), jnp.where(left, 0.0, x)], axis=0).astype(BF16)

    pairs = []
    for i in range(0, len(jobs), 2):
        j0, j1 = jobs[i], jobs[i + 1]
        d, rows = j0["d"], j0["rows"]
        ps = slice(j0["hs"].start, j1["hs"].stop)
        keep2 = (col2 <= row2) if d == 0 else (col2 >= row2)
        late2, early2 = (row2, col2) if d == 0 else (col2, row2)
        lev2 = [(((row2 ^ col2) >> (t + 1)) == 0) & ((late2 & (1 << t)) != 0) & ((early2 & (1 << t)) == 0)
                for t in range(6)]
        gcp = jnp.where(left, j0["gc"], j1["gc"])
        grp = jnp.where(left[0:1], j0["gr2"], j1["gr2"])
        decp = jnp.where(keep2, jnp.exp(gcp - grp), 0.0)
        kp = k_ref[rows, ps].astype(F32)
        kbp = kp * jnp.where(lane_k, j0["beta"], j1["beta"])
        kstack = jnp.concatenate([jnp.where(lane_k, kp, 0.0), jnp.where(lane_k, 0.0, kp)], axis=0).astype(BF16)
        a = _dot_nt(kbp.astype(BF16), kstack) * decp
        pairs.append(dict(j0=j0, j1=j1, a=a, lev=lev2, n=-jnp.where(lev2[0], a, 0.0)))
    for j in jobs:
        q = q_ref[j["rows"], j["hs"]].astype(BF16)
        k = k_ref[j["rows"], j["hs"]].astype(BF16)
        dec = jnp.where(j["keep"], jnp.exp(j["gc"] - j["gr"]), 0.0)
        qk_ref[j["d"], j["c"], j["h"]] = (_dot_nt(q, k) * dec).astype(BF16)
    for lev in range(1, 6):
        for pr in pairs:
            l = jnp.where(pr["lev"][lev], pr["a"], 0.0)
            pr["y"] = l + _dot(l.astype(BF16), block_diag(pr["n"]))
        for pr in pairs:
            pr["n"] = pr["n"] - pr["y"] - _dot(pr["n"].astype(BF16), block_diag(pr["y"]))
    for pr in pairs:
        rhs_pair = []
        for j in (pr["j0"], pr["j1"]):
            rows, hs, beta = j["rows"], j["hs"], j["beta"]
            k = k_ref[rows, hs].astype(F32)
            j["eg"] = jnp.exp(j["gc"])
            j["rhs"] = jnp.concatenate([v_ref[rows, hs].astype(F32) * beta, k * beta * j["eg"]], axis=1)
            rhs_pair.append(j["rhs"])
        rstack = jnp.concatenate(rhs_pair, axis=0).astype(BF16)
        pr["j0"]["cor"] = _dot(jnp.where(left, pr["n"], 0.0).astype(BF16), rstack)
        pr["j1"]["cor"] = _dot(jnp.where(left, 0.0, pr["n"]).astype(BF16), rstack)
    for j in jobs:
        d, rows, hs, gc, gl, eg = j["d"], j["rows"], j["hs"], j["gc"], j["gl"], j["eg"]
        q = q_ref[rows, hs].astype(F32)
        k = k_ref[rows, hs].astype(F32)
        sol = j["rhs"] + j["cor"]
        u_ref[d, rows, hs] = sol[:, :GDN_DV].astype(BF16)
        w_ref[d, rows, hs] = sol[:, GDN_DV:].astype(BF16)
        qg_ref[d, rows, hs] = (q * eg).astype(BF16)
        kd_ref[d, rows, hs] = (k * jnp.exp(gl - gc)).astype(BF16)
        egl_ref[d, j["c"], :, hs] = jnp.broadcast_to(jnp.exp(gl), (8, 128))


def gdn_prep(p, sm, gT):
    R = p.shape[0]
    T, C = GDN_ROWS, GDN_CHUNK
    nc = T // C
    col = lambda k: pl.BlockSpec((T, 512), lambda i: (i, C_QKV // 512 + k))
    dirrow = pl.BlockSpec((2, T, 512), lambda i: (0, i, 0))
    return pl.pallas_call(
        _gdn_prep_kernel,
        grid=(R // T,),
        in_specs=[col(0), col(1), col(2),
                  pl.BlockSpec((T, 128), lambda i: (i, 0)),
                  pl.BlockSpec((nc, 16, C), lambda i: (i, 0, 0))],
        out_specs=[dirrow, dirrow, dirrow, dirrow,
                   pl.BlockSpec((2, nc, GDN_HEADS, C, C), lambda i: (0, i, 0, 0, 0)),
                   pl.BlockSpec((2, nc, 8, 512), lambda i: (0, i, 0, 0))],
        out_shape=[jax.ShapeDtypeStruct((2, R, 512), BF16),
                   jax.ShapeDtypeStruct((2, R, 512), BF16),
                   jax.ShapeDtypeStruct((2, R, 512), BF16),
                   jax.ShapeDtypeStruct((2, R, 512), BF16),
                   jax.ShapeDtypeStruct((2, R // C, GDN_HEADS, C, C), BF16),
                   jax.ShapeDtypeStruct((2, R // C, 8, 512), F32)],
        compiler_params=_cp("arbitrary"),
        name="gdn_prep",
    )(p, p, p, sm, gT)


def _gdn_scan_kernel(uf, wf, qgf, kdf, qkf, eglf, ub, wb, qgb, kdb, qkb, eglb, of_ref, ob_ref, s_ref):
    C = GDN_CHUNK
    nch = GDN_ROWS // C

    @pl.when(pl.program_id(1) == 0)
    def _():
        s_ref[...] = jnp.zeros_like(s_ref)

    dirs = ((uf, wf, qgf, kdf, qkf, eglf, of_ref), (ub, wb, qgb, kdb, qkb, eglb, ob_ref))
    chains = [(d, h) for d in range(2) for h in range(GDN_HEADS)]
    S = {ch: s_ref[ch[0], ch[1]] for ch in chains}
    for i in range(nch):
        Sb, vnb, rows_of, c_of = {}, {}, {}, {}
        for d, h in chains:
            c_of[d] = i if d == 0 else nch - 1 - i
            rows_of[d] = slice(c_of[d] * C, (c_of[d] + 1) * C)
        for d, h in chains:
            hs = slice(h * 128, (h + 1) * 128)
            Sb[d, h] = S[d, h].astype(BF16)
            v_new = dirs[d][0][0, rows_of[d], hs].astype(F32) - _dot(dirs[d][1][0, rows_of[d], hs], Sb[d, h])
            vnb[d, h] = v_new.astype(BF16)
        for d, h in chains:
            hs = slice(h * 128, (h + 1) * 128)
            u_ref, w_ref, qg_ref, kd_ref, qk_ref, egl_ref, o_ref = dirs[d]
            S[d, h] = S[d, h] * egl_ref[0, c_of[d], 0:1, hs] + _dot_tn(kd_ref[0, rows_of[d], hs], vnb[d, h])
        for d, h in chains:
            hs = slice(h * 128, (h + 1) * 128)
            u_ref, w_ref, qg_ref, kd_ref, qk_ref, egl_ref, o_ref = dirs[d]
            o_ref[rows_of[d], hs] = (_dot(qg_ref[0, rows_of[d], hs], Sb[d, h])
                                     + _dot(qk_ref[0, c_of[d], h], vnb[d, h])).astype(BF16)
    for ch in chains:
        s_ref[ch[0], ch[1]] = S[ch]


def gdn_scan(rw, u, w, qg, kd, qk, egl):
    T, C = GDN_ROWS, GDN_CHUNK
    nc = T // C
    R = u.shape[1]
    nbc, nbl, base = rw.Lc // T, rw.Ll // T, rw.NC // T

    def blk(d):
        def f(b, s):
            jc = s if d == 0 else nbc - 1 - s
            jl = (s - nbc) if d == 0 else nbl - 1 - (s - nbc)
            return jnp.where(s < nbc, b * nbc + jc, base + b * nbl + jl)
        return f

    in_specs = []
    for d in range(2):
        f = blk(d)
        rowspec = pl.BlockSpec((1, T, 512), lambda b, s, f=f, d=d: (d, f(b, s), 0))
        in_specs += [rowspec, rowspec, rowspec, rowspec,
                     pl.BlockSpec((1, nc, GDN_HEADS, C, C), lambda b, s, f=f, d=d: (d, f(b, s), 0, 0, 0)),
                     pl.BlockSpec((1, nc, 8, 512), lambda b, s, f=f, d=d: (d, f(b, s), 0, 0))]
    out_specs = [pl.BlockSpec((T, 512), lambda b, s, f=blk(d): (f(b, s), 0)) for d in range(2)]
    ops = (u, w, qg, kd, qk, egl)
    return pl.pallas_call(
        _gdn_scan_kernel,
        grid=(rw.B, nbc + nbl),
        in_specs=in_specs,
        out_specs=out_specs,
        out_shape=[jax.ShapeDtypeStruct((R, 512), BF16)] * 2,
        scratch_shapes=[pltpu.VMEM((2, GDN_HEADS, GDN_DK, GDN_DV), F32)],
        compiler_params=_cp("arbitrary", "arbitrary"),
        name="gdn_scan",
    )(*ops, *ops)


def _merge_kernel(yhc_ref, yhl_ref, sf_ref, sb_ref, sx_ref, sz_ref, dx_ref, snw_ref, gf_ref, gb_ref, gg_ref, gnw_ref,
                  g0_ref, g1_ref, g2_ref, w0_ref, w1_ref, w2_ref, wo_ref, xc_ref, xl_ref, mod_ref, o_ref,
                  ys_ref, yg_ref, *, nctx_blk):
    tm = xc_ref.shape[0]
    rp = 64
    for r in range(tm // rp):
        rs = slice(r * rp, (r + 1) * rp)
        y = (sf_ref[rs, :].astype(F32) + sb_ref[rs, :].astype(F32)
             + sx_ref[rs, :].astype(F32) * dx_ref[...])
        y = y * _silu(sz_ref[rs, :].astype(F32))
        parts = []
        for g in range(SSM_GROUPS):
            yg = y[:, g * SSM_GW:(g + 1) * SSM_GW]
            parts.append(yg * lax.rsqrt(jnp.mean(yg * yg, axis=-1, keepdims=True) + EPS))
        ys_ref[rs, :] = (jnp.concatenate(parts, axis=1) * snw_ref[...]).astype(BF16)
        o = gf_ref[rs, :].astype(F32) + gb_ref[rs, :].astype(F32)
        parts = []
        for h in range(GDN_HEADS):
            oh = o[:, h * 128:(h + 1) * 128]
            parts.append(oh * lax.rsqrt(jnp.mean(oh * oh, axis=-1, keepdims=True) + EPS))
        yg_ref[rs, :] = (jnp.concatenate(parts, axis=1) * gnw_ref[...]
                         * _silu(gg_ref[rs, :].astype(F32))).astype(BF16)
    is_ctx = pl.program_id(0) < nctx_blk
    yh = jnp.where(is_ctx, yhc_ref[...], yhl_ref[...])
    m = (_sigmoid(g0_ref[...].astype(F32)) * _dot(yh, w0_ref[...])
         + _sigmoid(g1_ref[...].astype(F32)) * _dot(ys_ref[...], w1_ref[...])
         + _sigmoid(g2_ref[...].astype(F32)) * _dot(yg_ref[...], w2_ref[...]))
    x = jnp.where(is_ctx, xc_ref[...], xl_ref[...])
    o_ref[...] = x + mod_ref[0, 2:3, :] * _dot(m.astype(BF16), wo_ref[...])


def merge(rw, l, yh, y_f, y_b, dx, ssm_nw, o_f, o_b, gdn_nw, p, w0, w1, w2, wo, xs, mod, skip_ctx=False):
    R = rw.R
    D = xs[0].shape[1]
    tm = min(rw.tm, 512)
    mi = rw.mod_index(tm)
    skip = rw.NC // tm if skip_ctx else 0
    yspec = pl.BlockSpec((tm, 512), lambda i: (i + skip, 0))
    pspec = lambda col: pl.BlockSpec((tm, 512), lambda i: (i + skip, col // 512))
    vec = pl.BlockSpec((1, 512), lambda i: (0, 0))
    gspec = lambda k: pl.BlockSpec((tm, D), lambda i: (i + skip, C_GATE // D + k))
    wspec = pl.BlockSpec((None, 512, D), lambda i: (l, 0, 0))
    return pl.pallas_call(
        functools.partial(_merge_kernel, nctx_blk=rw.NC // tm - skip),
        grid=(R // tm - skip,),
        in_specs=_stream_specs(rw, tm, yh, 1, skip) + [
                  yspec, yspec, pspec(C_XBC), pspec(C_Z), vec, vec,
                  yspec, yspec, pspec(C_GG), vec,
                  gspec(0), gspec(1), gspec(2), wspec, wspec, wspec,
                  pl.BlockSpec((None, D, D), lambda i: (l, 0, 0))]
                 + _stream_specs(rw, tm, xs, 1, skip)
                 + [pl.BlockSpec((None, 1, 8, D), lambda i: (l, mi(i + skip), 0, 0))],
        out_specs=pl.BlockSpec((tm, D), lambda i: (i + skip, 0)),
        out_shape=jax.ShapeDtypeStruct((R, D), F32),
        scratch_shapes=[pltpu.VMEM((tm, 512), BF16), pltpu.VMEM((tm, 512), BF16)],
        compiler_params=_cp("arbitrary"),
        name="merge",
    )(yh[0], yh[1], y_f, y_b, p, p, dx, ssm_nw, o_f, o_b, p, gdn_nw, p, p, p, w0, w1, w2, wo, xs[0], xs[1], mod)


def _swiglu_up_kernel(x_ref, nw_ref, mod_ref, wg_ref, wu_ref, o_ref, h_ref, g0_ref, g1_ref, u0_ref, u1_ref):
    @pl.when(pl.program_id(1) == 0)
    def _():
        h = _norm_mod(x_ref[...], nw_ref[...], mod_ref[0, 4:5, :], mod_ref[0, 3:4, :])
        h_ref[...] = h.astype(BF16)

    T, tn = o_ref.shape
    rows = g0_ref.shape[0]
    gs, us = (g0_ref, g1_ref), (u0_ref, u1_ref)

    def project(r):
        hh = h_ref[r * rows:(r + 1) * rows, :]
        gs[r % 2][...] = _dot(hh, wg_ref[...])
        us[r % 2][...] = _dot(hh, wu_ref[...])

    def finish(r):
        for q in range(rows // 64):
            for c in range(tn // 128):
                ps = (slice(q * 64, (q + 1) * 64), slice(c * 128, (c + 1) * 128))
                y = _silu(gs[r % 2][ps]) * us[r % 2][ps]
                o_ref[r * rows + q * 64:r * rows + (q + 1) * 64, ps[1]] = y.astype(o_ref.dtype)

    for r in range(T // rows):
        project(r)
        if r > 0:
            finish(r - 1)
    finish(T // rows - 1)


def swiglu_up(rw, l, x, nw, mod, wgu, skip_ctx=False):
    R, D = x.shape
    tm = min(rw.tm, 512)
    tn = D_FF
    nj = D_FF // tn
    mi = rw.mod_index(tm)
    skip = rw.NC // tm if skip_ctx else 0
    return pl.pallas_call(
        _swiglu_up_kernel,
        grid=(R // tm - skip, nj),
        in_specs=[
            pl.BlockSpec((tm, D), lambda i, j: (i + skip, 0)),
            pl.BlockSpec((None, 1, D), lambda i, j: (l, 0, 0)),
            pl.BlockSpec((None, 1, 8, D), lambda i, j: (l, mi(i + skip), 0, 0)),
            pl.BlockSpec((None, D, tn), lambda i, j: (l, 0, j)),
            pl.BlockSpec((None, D, tn), lambda i, j: (l, 0, nj + j)),
        ],
        out_specs=pl.BlockSpec((tm, tn), lambda i, j: (i + skip, j)),
        out_shape=jax.ShapeDtypeStruct((R, D_FF), BF16),
        scratch_shapes=[pltpu.VMEM((tm, D), BF16)] + [pltpu.VMEM((min(256, tm), tn), F32)] * 4,
        compiler_params=_cp("arbitrary", "arbitrary"),
        name="swiglu_up",
    )(x, nw, mod, wgu, wgu)


def _swiglu_down_kernel(a_ref, w_ref, x_ref, mod_ref, fw_ref, o_ref, *, final):
    y = x_ref[...] + mod_ref[0, 5:6, :] * _dot(a_ref[...], w_ref[...])
    if final:
        y = y * lax.rsqrt(jnp.mean(y * y, axis=-1, keepdims=True) + EPS) * fw_ref[...]
    o_ref[...] = y


def swiglu_down(rw, l, a, w, x, mod, final_w, final=False):
    R, D = x.shape
    tm = min(rw.tm, 512)
    mi = rw.mod_index(tm)
    skip = rw.NC // tm if final else 0
    return pl.pallas_call(
        functools.partial(_swiglu_down_kernel, final=final),
        grid=(R // tm - skip,),
        in_specs=[
            pl.BlockSpec((tm, D_FF), lambda i: (i + skip, 0)),
            pl.BlockSpec((None, D_FF, D), lambda i: (l, 0, 0)),
            pl.BlockSpec((tm, D), lambda i: (i + skip, 0)),
            pl.BlockSpec((None, 1, 8, D), lambda i: (l, mi(i + skip), 0, 0)),
            pl.BlockSpec((1, D), lambda i: (0, 0)),
        ],
        out_specs=pl.BlockSpec((tm, D), lambda i: (i, 0)),
        out_shape=jax.ShapeDtypeStruct((R - skip * tm, D), F32),
        compiler_params=_cp("arbitrary"),
        name="swiglu_down",
    )(a, w, x, mod, final_w.reshape(1, D))


def _regroup_w_in(w_in):
    o_dt = 3072
    o_gdn = 3088
    o_a = o_gdn + 2048
    o_b = o_a + 8
    o_gate = o_gdn + 2064
    wt = jnp.swapaxes(w_in, 1, 2).astype(BF16)
    pieces = [
        wt[:, 0:3072],
        wt[:, o_gdn:o_gdn + 2048],
        wt[:, o_gate:o_gate + 3072],
        wt[:, o_dt:o_dt + 16],
        wt[:, o_a:o_a + 4], wt[:, o_b:o_b + 4],
        wt[:, o_a + 4:o_a + 8], wt[:, o_b + 4:o_b + 8],
        jnp.zeros((wt.shape[0], N_IN_PAD - C_SM - 32, wt.shape[2]), wt.dtype),
    ]
    return jnp.swapaxes(jnp.concatenate(pieces, axis=1), 1, 2)


def kernel(x, c, ctx, c_ctx, w_ada, b_ada, norm1_w, norm2_w, w_in, hy_conv_w, hy_conv_b, hy_w1, hy_b1, hy_w2, hy_b2, hy_w3, hy_freq, hy_bias, ssm_conv_w, ssm_conv_b, ssm_dt_bias, ssm_A_log, ssm_D, ssm_norm_w, gdn_conv_w, gdn_dt_bias, gdn_A_log, gdn_norm_w, w_hy_out, w_ssm_out, w_gdn_out, w_out, w_gate_up, w_down, final_norm_w):
    B, Ll, D = x.shape
    Lc = ctx.shape[1]
    depth = w_ada.shape[0]
    assert Lc == CONV_ROWS and D == D_MODEL and B <= 15
    rw = Rows(B, Lc, Ll)
    R, NC = rw.R, rw.NC

    xs = (ctx.reshape(B * Lc, D), x.reshape(B * Ll, D))

    svec = jnp.concatenate([c_ctx[None, :], c, jnp.zeros((15 - B, D), F32)], axis=0)
    mod = ada_modulation(svec, w_ada, b_ada)
    mod = jnp.pad(mod.reshape(depth, 16, 6, D), ((0, 0), (0, 0), (0, 2), (0, 0)))

    w_in_r = _regroup_w_in(w_in)
    par = _in_proj_params(hy_conv_w, hy_conv_b, ssm_conv_w, ssm_conv_b, gdn_conv_w, ssm_dt_bias, gdn_dt_bias,
                          gdn_A_log)
    norm1 = norm1_w.reshape(depth, 1, D)
    norm2 = norm2_w.reshape(depth, 1, D)
    w_hy_o, w_ssm_o, w_gdn_o, w_o = (w.astype(BF16) for w in (w_hy_out, w_ssm_out, w_gdn_out, w_out))
    w_gu, w_dn = w_gate_up.astype(BF16), w_down.astype(BF16)
    dft_l = dft_tables_split(Ll)
    dft_c = dft_tables_split(Lc)
    feat_l, feat_c = hy_features(Ll), hy_features(Lc)

    for l in range(depth):
        p, sm = in_proj(rw, xs, l, norm1, mod, w_in_r, par)

        sm32_t = sm[:, :32].T
        dt_t = sm32_t[:16].reshape(2, 8, R)
        g_t = sm32_t[16:32].reshape(16, R // GDN_CHUNK, GDN_CHUNK).transpose(1, 0, 2)

        alx = jnp.repeat(ssm_A_log[l], SSM_HEAD_DIM, axis=-1).reshape(2, 1, 512)
        alc = ssm_A_log[l].reshape(2, 8, 1)
        y_f, y_b = ssd_scan(rw, p, sm, dt_t, alx, alc)
        dx = jnp.repeat(ssm_D[l], SSM_HEAD_DIM).reshape(1, 512)

        o_f, o_b = gdn_scan(rw, *gdn_prep(p, sm, g_t))

        last = l == depth - 1
        hyu = p
        parts = []
        for (Bn, L, blk0, (fwd, inv), feat) in ((B, Lc, 0, dft_c, feat_c), (B, Ll, NC // Ll, dft_l, feat_l)):
            if NC % L:
                raise ValueError("latent length must divide the context row count")
            if last and blk0 == 0:
                parts.append(None)
                continue
            filt = hy_filter(feat, hy_w1[l], hy_b1[l], hy_w2[l], hy_b2[l], hy_w3[l], hy_freq[l])
            kspec = filter_spectrum(fwd, *filt)
            z1 = long_conv(Bn, L, hyu, blk0, 0, hyu, blk0, 1, hy_bias[l, 0], fwd, inv, kspec, 0, F32)
            yy = long_conv(Bn, L, z1, 0, 0, hyu, blk0, 2, hy_bias[l, 1], fwd, inv, kspec, 1, BF16)
            parts.append(yy)
        y_hy = (parts[1] if last else parts[0], parts[1])

        xa = merge(rw, l, y_hy, y_f, y_b, dx, ssm_norm_w[l].reshape(1, 512),
                   o_f, o_b, jnp.tile(gdn_norm_w[l], GDN_HEADS).reshape(1, 512), p,
                   w_hy_o, w_ssm_o, w_gdn_o, w_o, xs, mod, skip_ctx=last)
        act = swiglu_up(rw, l, xa, norm2, mod, w_gu, skip_ctx=last)
        xa = swiglu_down(rw, l, act, w_dn, xa, mod, final_norm_w, final=last)
        xs = (xa, xa)

    return xa.reshape(B, Ll, D)
```

```python
import functools
import math

import jax
import jax.numpy as jnp
import numpy as np
from jax import lax
from jax.experimental import pallas as pl
from jax.experimental.pallas import tpu as pltpu

F32 = jnp.float32
BF16 = jnp.bfloat16
HI = lax.Precision.HIGHEST

EPS = 1e-6
D_MODEL = 1024
GRID_W = 64

HY_WIDTH = 512
HY_BANDS = 16
HY_EMB = 1 + 2 * HY_BANDS
HY_HIDDEN = 64
HY_SHORT_DECAY_PCT = 0.3
HY_LONG_DECAY_PCT = 1.5
HY_TARGET = 1e-2

SSM_HEADS = 8
SSM_HEAD_DIM = 64
SSM_WIDTH = 512
SSM_GROUPS = 2
SSM_HPG = 4
SSM_STATE = 128
SSM_CHUNK = 128
SSM_GW = SSM_HPG * SSM_HEAD_DIM

GDN_HEADS = 4
GDN_DK = 128
GDN_DV = 128
GDN_CHUNK = 64

D_FF = 2816

C_HY = 0
C_Z = 1536
C_XBC = 2048
C_QKV = 3072
C_GG = 4608
C_GATE = 5120
C_SM = 8192

CONV_ROWS = 256
FREQ_BLK = 256

VMEM_LIMIT = 56 * 1024 * 1024


def _cp(*sem):
    return pltpu.CompilerParams(dimension_semantics=sem, vmem_limit_bytes=VMEM_LIMIT)


def _sigmoid(x):
    return 1.0 / (1.0 + jnp.exp(-x))


def _silu(x):
    return x * _sigmoid(x)


def _softplus(x):
    return jnp.maximum(x, 0.0) + jnp.log1p(jnp.exp(-jnp.abs(x)))


def _dot(a, b, precision=None):
    return jnp.dot(a, b, precision=precision, preferred_element_type=F32)


def _dot_nt(a, b):
    return lax.dot_general(a, b, (((1,), (1,)), ((), ())), preferred_element_type=F32)


def _dot_tn(a, b):
    return lax.dot_general(a, b, (((0,), (0,)), ((), ())), preferred_element_type=F32)


def _ada_kernel(s_ref, w_ref, b_ref, o_ref):
    s = _silu(s_ref[...])
    o_ref[0] = _dot(s, w_ref[0], HI) + b_ref[0]


def ada_modulation(svec, w_ada, b_ada):
    depth = w_ada.shape[0]
    D = D_MODEL
    return pl.pallas_call(
        _ada_kernel,
        grid=(depth, 6),
        in_specs=[
            pl.BlockSpec((16, D), lambda l, j: (0, 0)),
            pl.BlockSpec((1, D, D), lambda l, j: (l, 0, j)),
            pl.BlockSpec((1, 1, D), lambda l, j: (l, 0, j)),
        ],
        out_specs=pl.BlockSpec((1, 16, D), lambda l, j: (l, 0, j)),
        out_shape=jax.ShapeDtypeStruct((depth, 16, 6 * D), F32),
        compiler_params=_cp("arbitrary", "arbitrary"),
        name="ada",
    )(svec, w_ada, b_ada.reshape(depth, 1, 6 * D))


def _norm_mod(x, nw, scale, shift):
    ms = jnp.mean(x * x, axis=-1, keepdims=True)
    return (x * lax.rsqrt(ms + EPS) * nw) * (1.0 + scale) + shift


IN_FLIGHT = 3
N_IN_PAD = C_SM + 128
IN_TN = N_IN_PAD // 5
MODE_RAW, MODE_CONV, MODE_CONV_SILU, MODE_CONV_SILU_L2, MODE_SMALL = range(5)


def _tile_mode(tile):
    col = tile * 128
    if col < C_Z:
        return MODE_CONV
    if col < C_XBC:
        return MODE_RAW
    if col < C_QKV:
        return MODE_CONV_SILU
    if col < C_QKV + 1024:
        return MODE_CONV_SILU_L2
    if col < C_GG:
        return MODE_CONV_SILU
    if col < C_SM:
        return MODE_RAW
    return MODE_SMALL
PAR_W0, PAR_W1, PAR_W2, PAR_BIAS, PAR_L2SCALE, PAR_SBIAS, PAR_SALOG, PAR_SKIND = range(8)


def _in_proj_kernel(xc_ref, xl_ref, nw_ref, mod_ref, w_ref, par_ref, o_ref, sm_ref, h_ref, raw0_ref, raw1_ref, *,
                    nctx_blk):
    j = pl.program_id(1)
    nj = N_IN_PAD // IN_TN
    raws = (raw0_ref, raw1_ref)

    @pl.when((j == 0) & (pl.program_id(0) < nctx_blk))
    def _():
        h = _norm_mod(xc_ref[...], nw_ref[...], mod_ref[0, 1:2, :], mod_ref[0, 0:1, :])
        h_ref[...] = h.astype(BF16)

    @pl.when((j == 0) & (pl.program_id(0) >= nctx_blk))
    def _():
        h = _norm_mod(xl_ref[...], nw_ref[...], mod_ref[0, 1:2, :], mod_ref[0, 0:1, :])
        h_ref[...] = h.astype(BF16)

    T = h_ref.shape[0]
    G = GRID_W
    per_ctx = CONV_ROWS // G
    is_latent = pl.program_id(0) >= nctx_blk
    sub = lax.broadcasted_iota(jnp.int32, (8, 128), 0)

    def raw_piece(src, g, c):
        return src[g * G:(g + 1) * G, c * 128:(c + 1) * 128]

    retired = []

    def retire(y):
        bits = pltpu.bitcast(y[0:8], jnp.int32)
        zero = lax.shift_right_logical(lax.shift_right_logical(bits, 16), 16)
        retired.append(jnp.tile(zero.astype(F32), (G // 8, 1)))

    def conv(src, g, c):
        cs = slice(c * 128, (c + 1) * 128)
        x = raw_piece(src, g, c)
        if len(retired) >= IN_FLIGHT:
            x = x + retired[-IN_FLIGHT]
        zero = jnp.zeros((1, 128), F32)
        before = zero if g % per_ctx == 0 else jnp.where(is_latent, 0.0, src[g * G - 1:g * G, cs])
        after = zero if g % per_ctx == per_ctx - 1 else jnp.where(is_latent, 0.0, src[(g + 1) * G:(g + 1) * G + 1, cs])
        rp = pltpu.roll(x, 1, 0)
        rn = pltpu.roll(x, G - 1, 0)
        prev = jnp.concatenate([jnp.where(sub == 0, before, rp[0:8]), rp[8:]], axis=0)
        nxt = jnp.concatenate([rn[:G - 8], jnp.where(sub == 7, after, rn[G - 8:])], axis=0)
        return (prev * par_ref[PAR_W0:PAR_W0 + 1, cs] + x * par_ref[PAR_W1:PAR_W1 + 1, cs]
                + nxt * par_ref[PAR_W2:PAR_W2 + 1, cs] + par_ref[PAR_BIAS:PAR_BIAS + 1, cs])

    def conv_silu(src, g, c):
        return _silu(conv(src, g, c))

    def conv_silu_l2(src, g, c):
        y = _silu(conv(src, g, c))
        y = y * lax.rsqrt(jnp.sum(y * y, axis=-1, keepdims=True) + EPS)
        return y * par_ref[PAR_L2SCALE:PAR_L2SCALE + 1, c * 128:(c + 1) * 128]

    def small(src, g, c):
        cs = slice(c * 128, (c + 1) * 128)
        acc = raw_piece(src, g, c)
        kind = par_ref[PAR_SKIND:PAR_SKIND + 1, cs]
        sp = _softplus(acc + par_ref[PAR_SBIAS:PAR_SBIAS + 1, cs])
        dec = -jnp.exp(par_ref[PAR_SALOG:PAR_SALOG + 1, cs]) * sp
        return jnp.where(kind == 0.0, sp, jnp.where(kind == 1.0, dec, jnp.where(kind == 2.0, _sigmoid(acc), 0.0)))

    rows_mm = 512
    tiles = IN_TN // 128
    piece_fn = {MODE_RAW: raw_piece, MODE_CONV: conv, MODE_CONV_SILU: conv_silu,
                MODE_CONV_SILU_L2: conv_silu_l2, MODE_SMALL: small}

    heavy_modes = (MODE_CONV_SILU, MODE_CONV_SILU_L2)
    col_slices = [slice(c0, min(c0 + 256, IN_TN)) for c0 in range(0, IN_TN, 256)]

    def project(dst, r, cs=slice(None)):
        rs = slice(r * rows_mm, (r + 1) * rows_mm)
        dst[rs, cs] = _dot(h_ref[rs, :], w_ref[:, cs])

    def finish(src, blk, g, c):
        mode = _tile_mode(blk * tiles + c)
        y = piece_fn[mode](src, g, c)
        if mode in heavy_modes:
            retire(y)
        if mode == MODE_SMALL:
            sm_ref[g * G:(g + 1) * G, :] = y
            y = jnp.zeros_like(y)
        o_ref[g * G:(g + 1) * G, c * 128:(c + 1) * 128] = y.astype(o_ref.dtype)

    for step in range(nj + 1):
        @pl.when(j == step)
        def _(step=step):
            blk = step - 1
            src, dst = raws[blk % 2], raws[step % 2]
            retired.clear()
            light = blk >= 0 and not any(_tile_mode(blk * tiles + c) in heavy_modes for c in range(tiles))
            for r in range(T // rows_mm):
                pieces = [] if blk < 0 else [(g, c) for g in range(r * rows_mm // G, (r + 1) * rows_mm // G)
                                             for c in range(tiles)]
                if step == nj:
                    slabs = []
                elif light:
                    slabs = col_slices
                else:
                    slabs = [slice(None)]
                per = -(-len(pieces) // max(len(slabs), 1))
                for n in range(max(len(slabs), 1)):
                    if n < len(slabs):
                        project(dst, r, slabs[n])
                    for g, c in pieces[n * per:(n + 1) * per]:
                        finish(src, blk, g, c)


class Rows:
    def __init__(self, B, Lc, Ll):
        self.B, self.Lc, self.Ll = B, Lc, Ll
        self.NC = B * Lc
        self.R = B * Lc + B * Ll
        assert self.NC % Ll == 0 or Ll % self.NC == 0
        tm = 1024
        while self.NC % tm or Ll % tm:
            tm //= 2
        self.tm = tm

    def mod_index(self, tm):
        nctx = self.NC // tm
        per = self.Ll // tm
        return lambda i: jnp.where(i < nctx, 0, 1 + (i - nctx) // per)


def _stream_specs(rw, tm, xs, ngrid, skip=0):
    xc, xl = xs
    nctx = rw.NC // tm
    off = nctx if xl.shape[0] == rw.R else 0
    D = xc.shape[1]
    if ngrid == 1:
        return [pl.BlockSpec((tm, D), lambda i: (jnp.minimum(i + skip, nctx - 1), 0)),
                pl.BlockSpec((tm, D), lambda i: (jnp.maximum(i + skip - nctx, 0) + off, 0))]
    return [pl.BlockSpec((tm, D), lambda i, j: (jnp.minimum(i + skip, nctx - 1), 0)),
            pl.BlockSpec((tm, D), lambda i, j: (jnp.maximum(i + skip - nctx, 0) + off, 0))]


def in_proj(rw, xs, l, nw, mod, w, par):
    R = rw.R
    D = xs[0].shape[1]
    N = w.shape[2]
    tm, tn = rw.tm, IN_TN
    nj = N // tn
    assert N == N_IN_PAD
    mi = rw.mod_index(tm)
    done = lambda j: jnp.maximum(j - 1, 0)
    return pl.pallas_call(
        functools.partial(_in_proj_kernel, nctx_blk=rw.NC // tm),
        grid=(R // tm, nj + 1),
        in_specs=_stream_specs(rw, tm, xs, 2) + [
            pl.BlockSpec((None, 1, D), lambda i, j: (l, 0, 0)),
            pl.BlockSpec((None, 1, 8, D), lambda i, j: (l, mi(i), 0, 0)),
            pl.BlockSpec((None, D, tn), lambda i, j: (l, 0, jnp.minimum(j, nj - 1))),
            pl.BlockSpec((None, 8, tn), lambda i, j: (l, 0, done(j))),
        ],
        out_specs=[pl.BlockSpec((tm, tn), lambda i, j: (i, done(j))),
                   pl.BlockSpec((tm, 128), lambda i, j: (i, 0))],
        out_shape=[jax.ShapeDtypeStruct((R, N), BF16), jax.ShapeDtypeStruct((R, 128), F32)],
        scratch_shapes=[pltpu.VMEM((tm, D), BF16), pltpu.VMEM((tm, tn), F32), pltpu.VMEM((tm, tn), F32)],
        compiler_params=_cp("arbitrary", "arbitrary"),
        name="in_proj",
    )(xs[0], xs[1], nw, mod, w, par)


def _in_proj_params(hy_conv_w, hy_conv_b, ssm_conv_w, ssm_conv_b, gdn_conv_w, ssm_dt_bias, gdn_dt_bias, gdn_A_log):
    depth = hy_conv_w.shape[0]

    def row(pieces):
        out, pos = [], 0
        for off, a in pieces:
            out += [jnp.zeros((depth, off - pos), F32), a.astype(F32)]
            pos = off + a.shape[1]
        return jnp.concatenate(out + [jnp.zeros((depth, N_IN_PAD - pos), F32)], axis=1)
    z4 = jnp.zeros((depth, 4), F32)
    conv = [row([(C_HY, hy_conv_w[:, t]), (C_XBC, ssm_conv_w[:, t]), (C_QKV, gdn_conv_w[:, t])]) for t in range(3)]
    bias = row([(C_HY, hy_conv_b), (C_XBC, ssm_conv_b)])
    l2s = row([(C_QKV, jnp.full((depth, 512), GDN_DK ** -0.5, F32)), (C_QKV + 512, jnp.ones((depth, 512), F32))])
    sbias = row([(C_SM, jnp.concatenate([ssm_dt_bias.reshape(depth, 16), gdn_dt_bias[:, 0], z4,
                                         gdn_dt_bias[:, 1], z4], axis=1))])
    salog = row([(C_SM + 16, jnp.concatenate([gdn_A_log[:, 0], z4, gdn_A_log[:, 1], z4], axis=1))])
    kind = np.full((depth, N_IN_PAD), 3.0, np.float32)
    kind[:, C_SM:C_SM + 16] = 0.0
    kind[:, C_SM + 16:C_SM + 20] = 1.0
    kind[:, C_SM + 24:C_SM + 28] = 1.0
    kind[:, C_SM + 20:C_SM + 24] = 2.0
    kind[:, C_SM + 28:C_SM + 32] = 2.0
    return jnp.stack(conv + [bias, l2s, sbias, salog, jnp.asarray(kind)], axis=1)


def _hy_filter_kernel(z_ref, w1_ref, b1_ref, w2_ref, b2_ref, w3_ref, f0_ref, f1_ref, win_ref, oe_ref, oo_ref,
                      h_ref, split_ref):
    @pl.when(pl.program_id(1) == 0)
    def _():
        h1 = jnp.sin(f0_ref[...] * (_dot(z_ref[...], w1_ref[...], HI) + b1_ref[...]))
        h_ref[...] = jnp.sin(f1_ref[...] * (_dot(h1, w2_ref[...], HI) + b2_ref[...]))

    a1 = h_ref[...].astype(BF16)
    a2 = (h_ref[...] - a1.astype(F32)).astype(BF16)
    b1 = w3_ref[...].astype(BF16)
    b2 = (w3_ref[...] - b1.astype(F32)).astype(BF16)
    h = (_dot(a1, b1) + _dot(a1, b2) + _dot(a2, b1)) * win_ref[...]
    tl = h.shape[0]
    row = lax.broadcasted_iota(jnp.int32, (tl, 1), 0) + pl.program_id(0) * tl
    drop = (row == 0) & (pl.program_id(1) % 2 == 1)
    h = jnp.where(drop, 0.0, h)
    for c in range(h.shape[1] // 128):
        cs = slice(c * 128, (c + 1) * 128)
        s_c = split_ref.at[c]
        s_c[...] = h[:, cs]
        oe_ref[:, cs] = s_c[pl.ds(0, tl // 2, stride=2), :].astype(oe_ref.dtype)
        oo_ref[:, cs] = s_c[pl.ds(1, tl // 2, stride=2), :].astype(oo_ref.dtype)


def hy_features(L):
    t = jnp.linspace(0.0, 1.0, L, dtype=F32)[:, None]
    w = 2.0 * math.pi * jnp.arange(L, dtype=F32)[:, None] / L
    f = jnp.linspace(1e-4, HY_BANDS - 1, HY_BANDS, dtype=F32)[None, :]
    z = jnp.concatenate([t, jnp.cos(f * w), -jnp.sin(f * w)], axis=-1)
    z = jnp.pad(z, ((0, 0), (0, 128 - HY_EMB)))
    min_decay = math.log(HY_TARGET) / HY_LONG_DECAY_PCT
    max_decay = math.log(HY_TARGET) / HY_SHORT_DECAY_PCT
    deltas = jnp.linspace(min_decay, max_decay, HY_WIDTH, dtype=F32)
    window = jnp.exp(-t * jnp.abs(deltas))
    return z, window


def hy_filter(feat, w1, b1, w2, b2, w3, freq):
    z, window = feat
    L = z.shape[0]
    H = HY_HIDDEN
    w1p = jnp.pad(w1, ((0, 128 - HY_EMB), (0, 128 - H)))
    w2p = jnp.pad(w2, ((0, 128 - H), (0, 128 - H)))
    w3p = jnp.pad(w3, ((0, 128 - H), (0, 0)))
    pad1 = lambda v: jnp.pad(v, (0, 128 - H)).reshape(1, 128)
    tl = 256
    full = lambda shape: pl.BlockSpec(shape, lambda i, j: (0, 0))
    return pl.pallas_call(
        _hy_filter_kernel,
        grid=(L // tl, 4),
        in_specs=[
            pl.BlockSpec((tl, 128), lambda i, j: (i, 0)),
            full((128, 128)), full((1, 128)), full((128, 128)), full((1, 128)),
            pl.BlockSpec((128, HY_WIDTH), lambda i, j: (0, j)),
            full((1, 128)), full((1, 128)),
            pl.BlockSpec((tl, HY_WIDTH), lambda i, j: (i, 0)),
        ],
        out_specs=[pl.BlockSpec((tl // 2, HY_WIDTH), lambda i, j: (i, j))] * 2,
        out_shape=[jax.ShapeDtypeStruct((L // 2, 4 * HY_WIDTH), BF16)] * 2,
        scratch_shapes=[pltpu.VMEM((tl, 128), F32), pltpu.VMEM((HY_WIDTH // 128, tl, 128), F32)],
        compiler_params=_cp("arbitrary", "arbitrary"),
        name="hy_filter",
    )(z, w1p, pad1(b1), w2p, pad1(b2), w3p, pad1(freq[0]), pad1(freq[1]), window)


def dft_tables_split(L):
    N = 2 * L
    H = L // 2
    q = np.arange(H, dtype=np.int64)[:, None]
    m = np.arange(H, dtype=np.int64)[None, :]
    ang_e = ((q * 2 * m) % N).astype(np.float64) * (2.0 * math.pi / N)
    ang_o = ((q * (2 * m + 1)) % N).astype(np.float64) * (2.0 * math.pi / N)
    alt = (1 - 2 * (m % 2)).astype(np.float64)
    ce, co = np.cos(ang_e), np.cos(ang_o)
    se = np.where(q == 0, alt, -np.sin(ang_e))
    so = np.where(q == 0, -alt, -np.sin(ang_o))
    w = np.where(q == 0, 1.0, 2.0) / N
    ise = np.where(q == 0, 2.0 / N * alt, -np.sin(ang_e) * w)
    iso = np.where(q == 0, -2.0 / N * alt, -np.sin(ang_o) * w)
    fwd = np.stack([ce, co, se, so])
    inv = np.stack([(ce * w).T, ise.T, (co * w).T, iso.T])
    return jnp.asarray(fwd, dtype=BF16), jnp.asarray(inv, dtype=BF16)


def _filter_spectrum_kernel(fwd_ref, he_ref, ho_ref, rlo_ref, rhi_ref, ilo_ref, ihi_ref):
    he, ho = he_ref[...], ho_ref[...]
    ae, ao = _dot(fwd_ref[0], he), _dot(fwd_ref[1], ho)
    be, bo = _dot(fwd_ref[2], he), _dot(fwd_ref[3], ho)
    rlo_ref[...] = ae + ao
    rhi_ref[...] = ae - ao
    first = (lax.broadcasted_iota(jnp.int32, (fwd_ref.shape[1], 1), 0) == 0) & (pl.program_id(0) == 0)
    ilo_ref[...] = jnp.where(first, be, be + bo)
    ihi_ref[...] = jnp.where(first, bo, bo - be)


def filter_spectrum(fwd, taps_even, taps_odd):
    H, N = taps_even.shape
    C = HY_WIDTH
    FB = min(FREQ_BLK, H)
    tap = pl.BlockSpec((H, C), lambda f, j: (0, j))
    out = pl.BlockSpec((FB, C), lambda f, j: (f, j))
    return pl.pallas_call(
        _filter_spectrum_kernel,
        grid=(H // FB, N // C),
        in_specs=[pl.BlockSpec((4, FB, H), lambda f, j: (0, f, 0)), tap, tap],
        out_specs=[out] * 4,
        out_shape=[jax.ShapeDtypeStruct((H, N), F32)] * 4,
        compiler_params=_cp("arbitrary", "arbitrary"),
        name="filter_spectrum",
    )(fwd, taps_even, taps_odd)


def _long_conv_kernel(u_ref, g_ref, bias_ref, fwd_ref, inv_ref, ar0_ref, ar1_ref, ar0h_ref, ar1h_ref,
                      ai0_ref, ai1_ref, ai0h_ref, ai1h_ref, o_ref, ue_ref, uo_ref, acce_ref, acco_ref, y_ref):
    f = pl.program_id(1)
    half = ue_ref.shape[0]

    lane_tiles = [slice(c * 128, (c + 1) * 128) for c in range(y_ref.shape[0])]

    @pl.when(f == 0)
    def _():
        for c, cs in enumerate(lane_tiles):
            y_c = y_ref.at[c]
            y_c[...] = u_ref[:, cs].astype(F32)
            ue_ref[:, cs] = y_c[pl.ds(0, half, stride=2), :].astype(BF16)
            uo_ref[:, cs] = y_c[pl.ds(1, half, stride=2), :].astype(BF16)
        acce_ref[...] = jnp.zeros_like(acce_ref)
        acco_ref[...] = jnp.zeros_like(acco_ref)

    ue, uo = ue_ref[...], uo_ref[...]
    ae, ao = _dot(fwd_ref[0], ue), _dot(fwd_ref[1], uo)
    be, bo = _dot(fwd_ref[2], ue), _dot(fwd_ref[3], uo)
    ur, ur2 = ae + ao, ae - ao
    ui, ui2 = be + bo, bo - be
    first = (lax.broadcasted_iota(jnp.int32, (fwd_ref.shape[1], 1), 0) == 0) & (f == 0)
    kr, kr2 = ar0_ref[...] + ar1_ref[...], ar0h_ref[...] + ar1h_ref[...]
    ki = jnp.where(first, ai0_ref[...] + ai1_ref[...], ai0_ref[...] - ai1_ref[...])
    ki2 = ai0h_ref[...] - ai1h_ref[...]
    pr, pi = ur * kr - ui * ki, ur * ki + ui * kr
    pr2, pi2 = ur2 * kr2 - ui2 * ki2, ur2 * ki2 + ui2 * kr2
    dc, ny = ur * kr, ur2 * kr2
    gr = jnp.where(first, dc + ny, pr + pr2)
    gi = jnp.where(first, be * ki - bo * ki2, pi - pi2)
    hr = jnp.where(first, dc - ny, pr - pr2)
    hi = jnp.where(first, be * ki2 + bo * ki, pi + pi2)
    acce_ref[...] += _dot(inv_ref[0], gr.astype(BF16)) + _dot(inv_ref[1], gi.astype(BF16))
    acco_ref[...] += _dot(inv_ref[2], hr.astype(BF16)) + _dot(inv_ref[3], hi.astype(BF16))

    @pl.when(f == pl.num_programs(1) - 1)
    def _():
        for c, cs in enumerate(lane_tiles):
            y_c = y_ref.at[c]
            y_c[pl.ds(0, half, stride=2), :] = acce_ref[:, cs]
            y_c[pl.ds(1, half, stride=2), :] = acco_ref[:, cs]
            u = u_ref[:, cs].astype(F32)
            o_ref[:, cs] = (g_ref[:, cs].astype(F32) * (y_c[...] + u * bias_ref[:, cs])).astype(o_ref.dtype)


def long_conv(B, L, u, u_rb0, u_cb, gate, g_rb0, gate_cb, bias, fwd, inv, kspec, order, out_dtype):
    C = HY_WIDTH
    H = L // 2
    FB = min(FREQ_BLK, H)
    nfb = H // FB
    kblk = lambda part, d: pl.BlockSpec((FB, C), lambda b, f: (f, 2 * order + d))
    kops = [kspec[part] for part in range(4) for _ in range(2)]
    return pl.pallas_call(
        _long_conv_kernel,
        grid=(B, nfb),
        in_specs=[
            pl.BlockSpec((L, C), lambda b, f: (u_rb0 + b, u_cb)),
            pl.BlockSpec((L, C), lambda b, f: (g_rb0 + b, gate_cb)),
            pl.BlockSpec((1, C), lambda b, f: (0, 0)),
            pl.BlockSpec((4, FB, H), lambda b, f: (0, f, 0)),
            pl.BlockSpec((4, H, FB), lambda b, f: (0, 0, f)),
            kblk(0, 0), kblk(0, 1), kblk(1, 0), kblk(1, 1), kblk(2, 0), kblk(2, 1), kblk(3, 0), kblk(3, 1),
        ],
        out_specs=pl.BlockSpec((L, C), lambda b, f: (b, 0)),
        out_shape=jax.ShapeDtypeStruct((B * L, C), out_dtype),
        scratch_shapes=[pltpu.VMEM((H, C), BF16), pltpu.VMEM((H, C), BF16),
                        pltpu.VMEM((H, C), F32), pltpu.VMEM((H, C), F32), pltpu.VMEM((C // 128, L, 128), F32)],
        compiler_params=_cp("arbitrary", "arbitrary"),
        name="long_conv",
    )(u, gate, bias.reshape(1, C), fwd, inv, *kops)


def _scan_blocks(rw, rows):
    nbc, nbl, base = rw.Lc // rows, rw.Ll // rows, rw.NC // rows

    def make(d):
        def f(b, s):
            jc = s if d == 0 else nbc - 1 - s
            jl = (s - nbc) if d == 0 else nbl - 1 - (s - nbc)
            return jnp.where(s < nbc, b * nbc + jc, base + b * nbl + jl)
        return f

    return [make(0), make(1)], nbc + nbl


def _expand_lanes(x, base, n, width):
    rows = x.shape[0]
    per = 128 // width
    lane = lax.broadcasted_iota(jnp.int32, (rows, 128), 1)
    tiles = []
    for t in range(n // per):
        c0 = base + t * per
        tile = jnp.broadcast_to(x[:, c0:c0 + 1], (rows, 128))
        for i in range(1, per):
            tile = jnp.where(lane >= i * width, jnp.broadcast_to(x[:, c0 + i:c0 + i + 1], (rows, 128)), tile)
        tiles.append(tile)
    return jnp.concatenate(tiles, axis=1)


def _ssd_kernel(xf, bf, cf, smf, dtf, xb, bb, cb_, smb, dtb, alx_ref, alc_ref, of_ref, ob_ref, h_ref):
    Q = SSM_CHUNK
    GW = SSM_GW

    @pl.when(pl.program_id(1) == 0)
    def _():
        h_ref[...] = jnp.zeros_like(h_ref)

    row = lax.broadcasted_iota(jnp.int32, (Q, Q), 0)
    col = lax.broadcasted_iota(jnp.int32, (Q, Q), 1)
    lane_head = lax.broadcasted_iota(jnp.int32, (Q, GW), 1) // SSM_HEAD_DIM
    dirs = ((xf, bf, cf, smf, dtf, of_ref), (xb, bb, cb_, smb, dtb, ob_ref))
    jobs = []
    for d in range(2):
        x_ref, b_ref, c_ref, sm_ref, dt_ref, o_ref = dirs[d]
        keep = (col <= row) if d == 0 else (col >= row)
        tri = keep.astype(BF16)
        tri_t = ((row <= col) if d == 0 else (row >= col)).astype(BF16)
        sm = sm_ref[...]
        a_x = -jnp.exp(alx_ref[d])
        dtx = _expand_lanes(sm, 8 * d, SSM_HEADS, SSM_HEAD_DIM)
        cumx = _expand_lanes(_dot_01_lhs(tri, sm), 8 * d, SSM_HEADS, SSM_HEAD_DIM) * a_x
        cumr = _dot_01_rhs(dt_ref[0], tri_t) * (-jnp.exp(alc_ref[d]))
        last = Q - 1 if d == 0 else 0
        totx = cumx[last:last + 1, :]
        xd = x_ref[...].astype(F32) * dtx
        xdw = xd * jnp.exp(totx - cumx)
        ecum = jnp.exp(cumx)
        for g in range(SSM_GROUPS):
            gs = slice(g * GW, (g + 1) * GW)
            jobs.append(dict(d=d, g=g, gs=gs, keep=keep, cumx=cumx, cumr=cumr, o_ref=o_ref,
                             bg=b_ref[:, g * SSM_STATE:(g + 1) * SSM_STATE].astype(BF16),
                             cg=c_ref[:, g * SSM_STATE:(g + 1) * SSM_STATE].astype(BF16),
                             xdg=xd[:, gs], xdw=xdw[:, gs].astype(BF16), ecum=ecum[:, gs],
                             etot=jnp.exp(totx[:, gs])))
    for j in jobs:
        j["cb"] = _dot_nt(j["cg"], j["bg"])
        j["h"] = h_ref[j["d"], j["g"]]
    for j in jobs:
        ms, xs = [], []
        for e4 in range(SSM_HPG):
            e = j["g"] * SSM_HPG + e4
            diff = j["cumx"][:, e * SSM_HEAD_DIM:e * SSM_HEAD_DIM + 1] - j["cumr"][e:e + 1, :]
            ms.append((j["cb"] * jnp.where(j["keep"], jnp.exp(diff), 0.0)).astype(BF16))
            xs.append(jnp.where(lane_head == e4, j["xdg"], 0.0).astype(BF16))
        yd = _dot(jnp.concatenate(ms, axis=1), jnp.concatenate(xs, axis=0))
        y_off = _dot(j["cg"], j["h"].astype(BF16)) * j["ecum"]
        j["o_ref"][:, j["gs"]] = (yd + y_off).astype(BF16)
    for j in jobs:
        h_ref[j["d"], j["g"]] = j["h"] * j["etot"] + _dot_tn(j["bg"], j["xdw"])


def ssd_scan(rw, p, sm, dtT, alx, alc):
    Q = SSM_CHUNK
    blks, nsteps = _scan_blocks(rw, Q)
    R = p.shape[0]
    in_specs = []
    for d in range(2):
        f = blks[d]
        in_specs += [
            pl.BlockSpec((Q, 512), lambda b, s, f=f: (f(b, s), C_XBC // 512)),
            pl.BlockSpec((Q, 256), lambda b, s, f=f: (f(b, s), C_XBC // 256 + 2)),
            pl.BlockSpec((Q, 256), lambda b, s, f=f: (f(b, s), C_XBC // 256 + 3)),
            pl.BlockSpec((Q, 128), lambda b, s, f=f: (f(b, s), 0)),
            pl.BlockSpec((1, 8, Q), lambda b, s, f=f, d=d: (d, 0, f(b, s))),
        ]
    in_specs += [pl.BlockSpec((2, 1, 512), lambda b, s: (0, 0, 0)), pl.BlockSpec((2, 8, 1), lambda b, s: (0, 0, 0))]
    ops = (p, p, p, sm, dtT)
    return pl.pallas_call(
        _ssd_kernel,
        grid=(rw.B, nsteps),
        in_specs=in_specs,
        out_specs=[pl.BlockSpec((Q, 512), lambda b, s, f=blks[d]: (f(b, s), 0)) for d in range(2)],
        out_shape=[jax.ShapeDtypeStruct((R, 512), BF16)] * 2,
        scratch_shapes=[pltpu.VMEM((2, SSM_GROUPS, SSM_STATE, SSM_GW), F32)],
        compiler_params=_cp("arbitrary", "arbitrary"),
        name="ssd_scan",
    )(*ops, *ops, alx, alc)


def _split3(x):
    x1 = x.astype(BF16)
    r = x - x1.astype(F32)
    x2 = r.astype(BF16)
    x3 = (r - x2.astype(F32)).astype(BF16)
    return x1, x2, x3


def _dot_01_lhs(m01, x):
    x1, x2, x3 = _split3(x)
    return _dot(m01, x1) + _dot(m01, x2) + _dot(m01, x3)


def _dot_01_rhs(x, m01):
    x1, x2, x3 = _split3(x)
    return _dot(x1, m01) + _dot(x2, m01) + _dot(x3, m01)


GDN_ROWS = 256


def _gdn_prep_kernel(q_ref, k_ref, v_ref, sm_ref, gT_ref, u_ref, w_ref, qg_ref, kd_ref, qk_ref, egl_ref):
    C = GDN_CHUNK
    row = lax.broadcasted_iota(jnp.int32, (C, C), 0)
    col = lax.broadcasted_iota(jnp.int32, (C, C), 1)
    lane2 = lax.broadcasted_iota(jnp.int32, (C, 2 * C), 1)
    row2 = lax.broadcasted_iota(jnp.int32, (C, 2 * C), 0)
    col2 = lane2 & (C - 1)
    left = lane2 < C
    lane_k = lax.broadcasted_iota(jnp.int32, (C, 2 * GDN_DK), 1) < GDN_DK
    jobs = []
    for d in range(2):
        keep = (col <= row) if d == 0 else (col >= row)
        tri = keep.astype(BF16)
        tri_t2 = ((row2 <= col2) if d == 0 else (row2 >= col2)).astype(BF16)
        last = C - 1 if d == 0 else 0
        for c in range(GDN_ROWS // C):
            rows = slice(c * C, (c + 1) * C)
            smc = sm_ref[rows, :]
            cums = _dot_01_lhs(tri, smc)
            cumr2 = _dot_01_rhs(gT_ref[c, 8 * d:8 * d + 8, :], tri_t2)
            tot = cums[last:last + 1, :]
            for h in range(GDN_HEADS):
                lg = 16 + 8 * d + h
                jobs.append(dict(d=d, c=c, h=h, rows=rows, hs=slice(h * 128, (h + 1) * 128), keep=keep,
                                 gc=cums[:, lg:lg + 1], beta=smc[:, lg + 4:lg + 5],
                                 gl=tot[:, lg:lg + 1], gr=cumr2[h:h + 1, :C], gr2=cumr2[h:h + 1, :]))

    def block_diag(x):
        return jnp.concatenate([jnp.where(left, x, 0.0), jnp.where(left, 0.0, x)], axis=0).astype(BF16)

    pairs = []
    for i in range(0, len(jobs), 2):
        j0, j1 = jobs[i], jobs[i + 1]
        d, rows = j0["d"], j0["rows"]
        ps = slice(j0["hs"].start, j1["hs"].stop)
        keep2 = (col2 <= row2) if d == 0 else (col2 >= row2)
        late2, early2 = (row2, col2) if d == 0 else (col2, row2)
        lev2 = [(((row2 ^ col2) >> (t + 1)) == 0) & ((late2 & (1 << t)) != 0) & ((early2 & (1 << t)) == 0)
                for t in range(6)]
        gcp = jnp.where(left, j0["gc"], j1["gc"])
        grp = jnp.where(left[0:1], j0["gr2"], j1["gr2"])
        decp = jnp.where(keep2, jnp.exp(gcp - grp), 0.0)
        kp = k_ref[rows, ps].astype(F32)
        kbp = kp * jnp.where(lane_k, j0["beta"], j1["beta"])
        kstack = jnp.concatenate([jnp.where(lane_k, kp, 0.0), jnp.where(lane_k, 0.0, kp)], axis=0).astype(BF16)
        a = _dot_nt(kbp.astype(BF16), kstack) * decp
        pairs.append(dict(j0=j0, j1=j1, a=a, lev=lev2, n=-jnp.where(lev2[0], a, 0.0)))
    for j in jobs:
        q = q_ref[j["rows"], j["hs"]].astype(BF16)
        k = k_ref[j["rows"], j["hs"]].astype(BF16)
        dec = jnp.where(j["keep"], jnp.exp(j["gc"] - j["gr"]), 0.0)
        qk_ref[j["d"], j["c"], j["h"]] = (_dot_nt(q, k) * dec).astype(BF16)
    for lev in range(1, 6):
        for pr in pairs:
            l = jnp.where(pr["lev"][lev], pr["a"], 0.0)
            pr["y"] = l + _dot(l.astype(BF16), block_diag(pr["n"]))
        for pr in pairs:
            pr["n"] = pr["n"] - pr["y"] - _dot(pr["n"].astype(BF16), block_diag(pr["y"]))
    for pr in pairs:
        rhs_pair = []
        for j in (pr["j0"], pr["j1"]):
            rows, hs, beta = j["rows"], j["hs"], j["beta"]
            k = k_ref[rows, hs].astype(F32)
            j["eg"] = jnp.exp(j["gc"])
            j["rhs"] = jnp.concatenate([v_ref[rows, hs].astype(F32) * beta, k * beta * j["eg"]], axis=1)
            rhs_pair.append(j["rhs"])
        rstack = jnp.concatenate(rhs_pair, axis=0).astype(BF16)
        pr["j0"]["cor"] = _dot(jnp.where(left, pr["n"], 0.0).astype(BF16), rstack)
        pr["j1"]["cor"] = _dot(jnp.where(left, 0.0, pr["n"]).astype(BF16), rstack)
    for j in jobs:
        d, rows, hs, gc, gl, eg = j["d"], j["rows"], j["hs"], j["gc"], j["gl"], j["eg"]
        q = q_ref[rows, hs].astype(F32)
        k = k_ref[rows, hs].astype(F32)
        sol = j["rhs"] + j["cor"]
        u_ref[d, rows, hs] = sol[:, :GDN_DV].astype(BF16)
        w_ref[d, rows, hs] = sol[:, GDN_DV:].astype(BF16)
        qg_ref[d, rows, hs] = (q * eg).astype(BF16)
        kd_ref[d, rows, hs] = (k * jnp.exp(gl - gc)).astype(BF16)
        egl_ref[d, j["c"], :, hs] = jnp.broadcast_to(jnp.exp(gl), (8, 128))


def gdn_prep(p, sm, gT):
    R = p.shape[0]
    T, C = GDN_ROWS, GDN_CHUNK
    nc = T // C
    col = lambda k: pl.BlockSpec((T, 512), lambda i: (i, C_QKV // 512 + k))
    dirrow = pl.BlockSpec((2, T, 512), lambda i: (0, i, 0))
    return pl.pallas_call(
        _gdn_prep_kernel,
        grid=(R // T,),
        in_specs=[col(0), col(1), col(2),
                  pl.BlockSpec((T, 128), lambda i: (i, 0)),
                  pl.BlockSpec((nc, 16, C), lambda i: (i, 0, 0))],
        out_specs=[dirrow, dirrow, dirrow, dirrow,
                   pl.BlockSpec((2, nc, GDN_HEADS, C, C), lambda i: (0, i, 0, 0, 0)),
                   pl.BlockSpec((2, nc, 8, 512), lambda i: (0, i, 0, 0))],
        out_shape=[jax.ShapeDtypeStruct((2, R, 512), BF16),
                   jax.ShapeDtypeStruct((2, R, 512), BF16),
                   jax.ShapeDtypeStruct((2, R, 512), BF16),
                   jax.ShapeDtypeStruct((2, R, 512), BF16),
                   jax.ShapeDtypeStruct((2, R // C, GDN_HEADS, C, C), BF16),
                   jax.ShapeDtypeStruct((2, R // C, 8, 512), F32)],
        compiler_params=_cp("arbitrary"),
        name="gdn_prep",
    )(p, p, p, sm, gT)


def _gdn_scan_kernel(uf, wf, qgf, kdf, qkf, eglf, ub, wb, qgb, kdb, qkb, eglb, of_ref, ob_ref, s_ref):
    C = GDN_CHUNK
    nch = GDN_ROWS // C

    @pl.when(pl.program_id(1) == 0)
    def _():
        s_ref[...] = jnp.zeros_like(s_ref)

    dirs = ((uf, wf, qgf, kdf, qkf, eglf, of_ref), (ub, wb, qgb, kdb, qkb, eglb, ob_ref))
    chains = [(d, h) for d in range(2) for h in range(GDN_HEADS)]
    S = {ch: s_ref[ch[0], ch[1]] for ch in chains}
    for i in range(nch):
        Sb, vnb, rows_of, c_of = {}, {}, {}, {}
        for d, h in chains:
            c_of[d] = i if d == 0 else nch - 1 - i
            rows_of[d] = slice(c_of[d] * C, (c_of[d] + 1) * C)
        for d, h in chains:
            hs = slice(h * 128, (h + 1) * 128)
            Sb[d, h] = S[d, h].astype(BF16)
            v_new = dirs[d][0][0, rows_of[d], hs].astype(F32) - _dot(dirs[d][1][0, rows_of[d], hs], Sb[d, h])
            vnb[d, h] = v_new.astype(BF16)
        for d, h in chains:
            hs = slice(h * 128, (h + 1) * 128)
            u_ref, w_ref, qg_ref, kd_ref, qk_ref, egl_ref, o_ref = dirs[d]
            S[d, h] = S[d, h] * egl_ref[0, c_of[d], 0:1, hs] + _dot_tn(kd_ref[0, rows_of[d], hs], vnb[d, h])
        for d, h in chains:
            hs = slice(h * 128, (h + 1) * 128)
            u_ref, w_ref, qg_ref, kd_ref, qk_ref, egl_ref, o_ref = dirs[d]
            o_ref[rows_of[d], hs] = (_dot(qg_ref[0, rows_of[d], hs], Sb[d, h])
                                     + _dot(qk_ref[0, c_of[d], h], vnb[d, h])).astype(BF16)
    for ch in chains:
        s_ref[ch[0], ch[1]] = S[ch]


def gdn_scan(rw, u, w, qg, kd, qk, egl):
    T, C = GDN_ROWS, GDN_CHUNK
    nc = T // C
    R = u.shape[1]
    nbc, nbl, base = rw.Lc // T, rw.Ll // T, rw.NC // T

    def blk(d):
        def f(b, s):
            jc = s if d == 0 else nbc - 1 - s
            jl = (s - nbc) if d == 0 else nbl - 1 - (s - nbc)
            return jnp.where(s < nbc, b * nbc + jc, base + b * nbl + jl)
        return f

    in_specs = []
    for d in range(2):
        f = blk(d)
        rowspec = pl.BlockSpec((1, T, 512), lambda b, s, f=f, d=d: (d, f(b, s), 0))
        in_specs += [rowspec, rowspec, rowspec, rowspec,
                     pl.BlockSpec((1, nc, GDN_HEADS, C, C), lambda b, s, f=f, d=d: (d, f(b, s), 0, 0, 0)),
                     pl.BlockSpec((1, nc, 8, 512), lambda b, s, f=f, d=d: (d, f(b, s), 0, 0))]
    out_specs = [pl.BlockSpec((T, 512), lambda b, s, f=blk(d): (f(b, s), 0)) for d in range(2)]
    ops = (u, w, qg, kd, qk, egl)
    return pl.pallas_call(
        _gdn_scan_kernel,
        grid=(rw.B, nbc + nbl),
        in_specs=in_specs,
        out_specs=out_specs,
        out_shape=[jax.ShapeDtypeStruct((R, 512), BF16)] * 2,
        scratch_shapes=[pltpu.VMEM((2, GDN_HEADS, GDN_DK, GDN_DV), F32)],
        compiler_params=_cp("arbitrary", "arbitrary"),
        name="gdn_scan",
    )(*ops, *ops)


def _merge_kernel(yhc_ref, yhl_ref, sf_ref, sb_ref, sx_ref, sz_ref, dx_ref, snw_ref, gf_ref, gb_ref, gg_ref, gnw_ref,
                  g0_ref, g1_ref, g2_ref, w0_ref, w1_ref, w2_ref, wo_ref, xc_ref, xl_ref, mod_ref, o_ref,
                  ys_ref, yg_ref, *, nctx_blk):
    tm = xc_ref.shape[0]
    rp = 64
    for r in range(tm // rp):
        rs = slice(r * rp, (r + 1) * rp)
        y = (sf_ref[rs, :].astype(F32) + sb_ref[rs, :].astype(F32)
             + sx_ref[rs, :].astype(F32) * dx_ref[...])
        y = y * _silu(sz_ref[rs, :].astype(F32))
        parts = []
        for g in range(SSM_GROUPS):
            yg = y[:, g * SSM_GW:(g + 1) * SSM_GW]
            parts.append(yg * lax.rsqrt(jnp.mean(yg * yg, axis=-1, keepdims=True) + EPS))
        ys_ref[rs, :] = (jnp.concatenate(parts, axis=1) * snw_ref[...]).astype(BF16)
        o = gf_ref[rs, :].astype(F32) + gb_ref[rs, :].astype(F32)
        parts = []
        for h in range(GDN_HEADS):
            oh = o[:, h * 128:(h + 1) * 128]
            parts.append(oh * lax.rsqrt(jnp.mean(oh * oh, axis=-1, keepdims=True) + EPS))
        yg_ref[rs, :] = (jnp.concatenate(parts, axis=1) * gnw_ref[...]
                         * _silu(gg_ref[rs, :].astype(F32))).astype(BF16)
    is_ctx = pl.program_id(0) < nctx_blk
    yh = jnp.where(is_ctx, yhc_ref[...], yhl_ref[...])
    m = (_sigmoid(g0_ref[...].astype(F32)) * _dot(yh, w0_ref[...])
         + _sigmoid(g1_ref[...].astype(F32)) * _dot(ys_ref[...], w1_ref[...])
         + _sigmoid(g2_ref[...].astype(F32)) * _dot(yg_ref[...], w2_ref[...]))
    x = jnp.where(is_ctx, xc_ref[...], xl_ref[...])
    o_ref[...] = x + mod_ref[0, 2:3, :] * _dot(m.astype(BF16), wo_ref[...])


def merge(rw, l, yh, y_f, y_b, dx, ssm_nw, o_f, o_b, gdn_nw, p, w0, w1, w2, wo, xs, mod, skip_ctx=False):
    R = rw.R
    D = xs[0].shape[1]
    tm = min(rw.tm, 512)
    mi = rw.mod_index(tm)
    skip = rw.NC // tm if skip_ctx else 0
    yspec = pl.BlockSpec((tm, 512), lambda i: (i + skip, 0))
    pspec = lambda col: pl.BlockSpec((tm, 512), lambda i: (i + skip, col // 512))
    vec = pl.BlockSpec((1, 512), lambda i: (0, 0))
    gspec = lambda k: pl.BlockSpec((tm, D), lambda i: (i + skip, C_GATE // D + k))
    wspec = pl.BlockSpec((None, 512, D), lambda i: (l, 0, 0))
    return pl.pallas_call(
        functools.partial(_merge_kernel, nctx_blk=rw.NC // tm - skip),
        grid=(R // tm - skip,),
        in_specs=_stream_specs(rw, tm, yh, 1, skip) + [
                  yspec, yspec, pspec(C_XBC), pspec(C_Z), vec, vec,
                  yspec, yspec, pspec(C_GG), vec,
                  gspec(0), gspec(1), gspec(2), wspec, wspec, wspec,
                  pl.BlockSpec((None, D, D), lambda i: (l, 0, 0))]
                 + _stream_specs(rw, tm, xs, 1, skip)
                 + [pl.BlockSpec((None, 1, 8, D), lambda i: (l, mi(i + skip), 0, 0))],
        out_specs=pl.BlockSpec((tm, D), lambda i: (i + skip, 0)),
        out_shape=jax.ShapeDtypeStruct((R, D), F32),
        scratch_shapes=[pltpu.VMEM((tm, 512), BF16), pltpu.VMEM((tm, 512), BF16)],
        compiler_params=_cp("arbitrary"),
        name="merge",
    )(yh[0], yh[1], y_f, y_b, p, p, dx, ssm_nw, o_f, o_b, p, gdn_nw, p, p, p, w0, w1, w2, wo, xs[0], xs[1], mod)


def _swiglu_up_kernel(x_ref, nw_ref, mod_ref, wg_ref, wu_ref, o_ref, h_ref, g0_ref, g1_ref, u0_ref, u1_ref):
    @pl.when(pl.program_id(1) == 0)
    def _():
        h = _norm_mod(x_ref[...], nw_ref[...], mod_ref[0, 4:5, :], mod_ref[0, 3:4, :])
        h_ref[...] = h.astype(BF16)

    T, tn = o_ref.shape
    rows = g0_ref.shape[0]
    gs, us = (g0_ref, g1_ref), (u0_ref, u1_ref)

    def project(r):
        hh = h_ref[r * rows:(r + 1) * rows, :]
        gs[r % 2][...] = _dot(hh, wg_ref[...])
        us[r % 2][...] = _dot(hh, wu_ref[...])

    def finish(r):
        for q in range(rows // 64):
            for c in range(tn // 128):
                ps = (slice(q * 64, (q + 1) * 64), slice(c * 128, (c + 1) * 128))
                y = _silu(gs[r % 2][ps]) * us[r % 2][ps]
                o_ref[r * rows + q * 64:r * rows + (q + 1) * 64, ps[1]] = y.astype(o_ref.dtype)

    for r in range(T // rows):
        project(r)
        if r > 0:
            finish(r - 1)
    finish(T // rows - 1)


def swiglu_up(rw, l, x, nw, mod, wgu, skip_ctx=False):
    R, D = x.shape
    tm = min(rw.tm, 512)
    tn = D_FF
    nj = D_FF // tn
    mi = rw.mod_index(tm)
    skip = rw.NC // tm if skip_ctx else 0
    return pl.pallas_call(
        _swiglu_up_kernel,
        grid=(R // tm - skip, nj),
        in_specs=[
            pl.BlockSpec((tm, D), lambda i, j: (i + skip, 0)),
            pl.BlockSpec((None, 1, D), lambda i, j: (l, 0, 0)),
            pl.BlockSpec((None, 1, 8, D), lambda i, j: (l, mi(i + skip), 0, 0)),
            pl.BlockSpec((None, D, tn), lambda i, j: (l, 0, j)),
            pl.BlockSpec((None, D, tn), lambda i, j: (l, 0, nj + j)),
        ],
        out_specs=pl.BlockSpec((tm, tn), lambda i, j: (i + skip, j)),
        out_shape=jax.ShapeDtypeStruct((R, D_FF), BF16),
        scratch_shapes=[pltpu.VMEM((tm, D), BF16)] + [pltpu.VMEM((min(256, tm), tn), F32)] * 4,
        compiler_params=_cp("arbitrary", "arbitrary"),
        name="swiglu_up",
    )(x, nw, mod, wgu, wgu)


def _swiglu_down_kernel(a_ref, w_ref, x_ref, mod_ref, fw_ref, o_ref, *, final):
    y = x_ref[...] + mod_ref[0, 5:6, :] * _dot(a_ref[...], w_ref[...])
    if final:
        y = y * lax.rsqrt(jnp.mean(y * y, axis=-1, keepdims=True) + EPS) * fw_ref[...]
    o_ref[...] = y


def swiglu_down(rw, l, a, w, x, mod, final_w, final=False):
    R, D = x.shape
    tm = min(rw.tm, 512)
    mi = rw.mod_index(tm)
    skip = rw.NC // tm if final else 0
    return pl.pallas_call(
        functools.partial(_swiglu_down_kernel, final=final),
        grid=(R // tm - skip,),
        in_specs=[
            pl.BlockSpec((tm, D_FF), lambda i: (i + skip, 0)),
            pl.BlockSpec((None, D_FF, D), lambda i: (l, 0, 0)),
            pl.BlockSpec((tm, D), lambda i: (i + skip, 0)),
            pl.BlockSpec((None, 1, 8, D), lambda i: (l, mi(i + skip), 0, 0)),
            pl.BlockSpec((1, D), lambda i: (0, 0)),
        ],
        out_specs=pl.BlockSpec((tm, D), lambda i: (i, 0)),
        out_shape=jax.ShapeDtypeStruct((R - skip * tm, D), F32),
        compiler_params=_cp("arbitrary"),
        name="swiglu_down",
    )(a, w, x, mod, final_w.reshape(1, D))


def _regroup_w_in(w_in):
    o_dt = 3072
    o_gdn = 3088
    o_a = o_gdn + 2048
    o_b = o_a + 8
    o_gate = o_gdn + 2064
    wt = jnp.swapaxes(w_in, 1, 2).astype(BF16)
    pieces = [
        wt[:, 0:3072],
        wt[:, o_gdn:o_gdn + 2048],
        wt[:, o_gate:o_gate + 3072],
        wt[:, o_dt:o_dt + 16],
        wt[:, o_a:o_a + 4], wt[:, o_b:o_b + 4],
        wt[:, o_a + 4:o_a + 8], wt[:, o_b + 4:o_b + 8],
        jnp.zeros((wt.shape[0], N_IN_PAD - C_SM - 32, wt.shape[2]), wt.dtype),
    ]
    return jnp.swapaxes(jnp.concatenate(pieces, axis=1), 1, 2)


def kernel(x, c, ctx, c_ctx, w_ada, b_ada, norm1_w, norm2_w, w_in, hy_conv_w, hy_conv_b, hy_w1, hy_b1, hy_w2, hy_b2, hy_w3, hy_freq, hy_bias, ssm_conv_w, ssm_conv_b, ssm_dt_bias, ssm_A_log, ssm_D, ssm_norm_w, gdn_conv_w, gdn_dt_bias, gdn_A_log, gdn_norm_w, w_hy_out, w_ssm_out, w_gdn_out, w_out, w_gate_up, w_down, final_norm_w):
    B, Ll, D = x.shape
    Lc = ctx.shape[1]
    depth = w_ada.shape[0]
    assert Lc == CONV_ROWS and D == D_MODEL and B <= 15
    rw = Rows(B, Lc, Ll)
    R, NC = rw.R, rw.NC

    xs = (ctx.reshape(B * Lc, D), x.reshape(B * Ll, D))

    svec = jnp.concatenate([c_ctx[None, :], c, jnp.zeros((15 - B, D), F32)], axis=0)
    mod = ada_modulation(svec, w_ada, b_ada)
    mod = jnp.pad(mod.reshape(depth, 16, 6, D), ((0, 0), (0, 0), (0, 2), (0, 0)))

    w_in_r = _regroup_w_in(w_in)
    par = _in_proj_params(hy_conv_w, hy_conv_b, ssm_conv_w, ssm_conv_b, gdn_conv_w, ssm_dt_bias, gdn_dt_bias,
                          gdn_A_log)
    norm1 = norm1_w.reshape(depth, 1, D)
    norm2 = norm2_w.reshape(depth, 1, D)
    w_hy_o, w_ssm_o, w_gdn_o, w_o = (w.astype(BF16) for w in (w_hy_out, w_ssm_out, w_gdn_out, w_out))
    w_gu, w_dn = w_gate_up.astype(BF16), w_down.astype(BF16)
    dft_l = dft_tables_split(Ll)
    dft_c = dft_tables_split(Lc)
    feat_l, feat_c = hy_features(Ll), hy_features(Lc)

    for l in range(depth):
        p, sm = in_proj(rw, xs, l, norm1, mod, w_in_r, par)

        sm32_t = sm[:, :32].T
        dt_t = sm32_t[:16].reshape(2, 8, R)
        g_t = sm32_t[16:32].reshape(16, R // GDN_CHUNK, GDN_CHUNK).transpose(1, 0, 2)

        alx = jnp.repeat(ssm_A_log[l], SSM_HEAD_DIM, axis=-1).reshape(2, 1, 512)
        alc = ssm_A_log[l].reshape(2, 8, 1)
        y_f, y_b = ssd_scan(rw, p, sm, dt_t, alx, alc)
        dx = jnp.repeat(ssm_D[l], SSM_HEAD_DIM).reshape(1, 512)

        o_f, o_b = gdn_scan(rw, *gdn_prep(p, sm, g_t))

        last = l == depth - 1
        hyu = p
        parts = []
        for (Bn, L, blk0, (fwd, inv), feat) in ((B, Lc, 0, dft_c, feat_c), (B, Ll, NC // Ll, dft_l, feat_l)):
            if NC % L:
                raise ValueError("latent length must divide the context row count")
            if last and blk0 == 0:
                parts.append(None)
                continue
            filt = hy_filter(feat, hy_w1[l], hy_b1[l], hy_w2[l], hy_b2[l], hy_w3[l], hy_freq[l])
            kspec = filter_spectrum(fwd, *filt)
            z1 = long_conv(Bn, L, hyu, blk0, 0, hyu, blk0, 1, hy_bias[l, 0], fwd, inv, kspec, 0, F32)
            yy = long_conv(Bn, L, z1, 0, 0, hyu, blk0, 2, hy_bias[l, 1], fwd, inv, kspec, 1, BF16)
            parts.append(yy)
        y_hy = (parts[1] if last else parts[0], parts[1])

        xa = merge(rw, l, y_hy, y_f, y_b, dx, ssm_norm_w[l].reshape(1, 512),
                   o_f, o_b, jnp.tile(gdn_norm_w[l], GDN_HEADS).reshape(1, 512), p,
                   w_hy_o, w_ssm_o, w_gdn_o, w_o, xs, mod, skip_ctx=last)
        act = swiglu_up(rw, l, xa, norm2, mod, w_gu, skip_ctx=last)
        xa = swiglu_down(rw, l, act, w_dn, xa, mod, final_norm_w, final=last)
        xs = (xa, xa)

    return xa.reshape(B, Ll, D)
```
